```python
import math
import jax, jax.numpy as jnp
from jax import lax
import numpy as np

D_MODEL = 1024
BATCH = 32
SEQ = 2048
DEPTH = 1

SB_HEADS = 16
SB_HEAD_DIM = 64
SB_WIDTH = SB_HEADS * SB_HEAD_DIM
CONV_CHANNELS = D_MODEL
CONV_WIDTH = 31
D_FF = 2816
Q_BLOCK = 128
N_SUBLAYERS = 3
N_MOD = 3
MACARON_WEIGHT = 0.5
DEEPNORM_ALPHA = (2.0 * DEPTH) ** 0.25
DEEPNORM_BETA = (8.0 * DEPTH) ** -0.25
LN_EPS = 1e-5
IN_SPLITS = (SB_WIDTH, SB_WIDTH, SB_WIDTH, CONV_CHANNELS, CONV_CHANNELS, D_MODEL, D_MODEL)
IN_WIDTH = sum(IN_SPLITS)

kernel_name = "hybrid_stickbreak_conformer_macaron_deepnorm_adaln"


def layer_norm(x, g, b):
    xf = x.astype(jnp.float32)
    mu = jnp.mean(xf, axis=-1, keepdims=True)
    var = jnp.mean(jnp.square(xf - mu), axis=-1, keepdims=True)
    y = (xf - mu) * lax.rsqrt(var + LN_EPS) * g.astype(jnp.float32) + b.astype(jnp.float32)
    return y.astype(x.dtype)


def swiglu(u, w_gu, w_down):
    a, g = jnp.split(u @ w_gu, 2, axis=-1)
    return (jax.nn.silu(a) * g) @ w_down


def stick_breaking_attention(q, k, v):
    seq = q.shape[2]
    scale = 1.0 / math.sqrt(q.shape[-1])
    outs = []
    for blk in range(seq // Q_BLOCK):
        start = blk * Q_BLOCK
        end = start + Q_BLOCK
        qb = q[:, :, start:end]
        kb = k[:, :, :end]
        vb = v[:, :, :end]
        z = jnp.einsum('bhqd,bhkd->bhqk', qb, kb).astype(jnp.float32) * scale
        t_idx = start + jnp.arange(Q_BLOCK)[:, None]
        s_idx = jnp.arange(end)[None, :]
        mask = s_idx < t_idx
        log_keep = jnp.where(mask, jax.nn.log_sigmoid(-z), 0.0)
        between = lax.cumsum(log_keep, axis=3, reverse=True) - log_keep
        log_w = jax.nn.log_sigmoid(z) + between
        w = jnp.where(mask, jnp.exp(log_w), 0.0)
        outs.append(jnp.einsum('bhqk,bhkd->bhqd', w.astype(vb.dtype), vb))
    return jnp.concatenate(outs, axis=2)


def conformer_conv(a, b, w_dw, b_dw, g, beta):
    h = a * jax.nn.sigmoid(b)
    h = lax.conv_general_dilated(
        h, w_dw[:, None, :].astype(h.dtype), window_strides=(1,),
        padding=[(CONV_WIDTH - 1, 0)], dimension_numbers=('NWC', 'WIO', 'NWC'),
        feature_group_count=CONV_CHANNELS) + b_dw
    return jax.nn.silu(layer_norm(h, g, beta))


def _fwd_setup_inputs(seed: int = 0) -> dict:
    key = jax.random.key(seed)
    ks = jax.random.split(key, 32)
    L, D, C = DEPTH, D_MODEL, CONV_CHANNELS

    def nrm(k, shape, std):
        return jax.random.normal(k, shape, jnp.float32) * std

    x = nrm(ks[0], (BATCH, SEQ, D), 1.0)
    c = nrm(ks[1], (BATCH, D), 1.0)
    w_ada = nrm(ks[2], (L, D, N_SUBLAYERS * N_MOD * D), 0.5 * D ** -0.5)
    b_ada = nrm(ks[3], (L, N_SUBLAYERS * N_MOD * D), 0.02)
    ffn1_w_gu = nrm(ks[4], (L, D, 2 * D_FF), D ** -0.5)
    ffn1_w_down = nrm(ks[5], (L, D_FF, D), D_FF ** -0.5 * DEEPNORM_BETA)
    ln1_g = 1.0 + nrm(ks[6], (L, D), 0.02)
    ln1_b = nrm(ks[7], (L, D), 0.02)
    w_qk = nrm(ks[8], (L, D, 2 * SB_WIDTH), D ** -0.5)
    w_v = nrm(ks[9], (L, D, SB_WIDTH), D ** -0.5 * DEEPNORM_BETA)
    w_rest = nrm(ks[10], (L, D, 2 * C + 2 * D), D ** -0.5)
    w_in = jnp.concatenate([w_qk, w_v, w_rest], axis=-1)
    w_sb_out = nrm(ks[11], (L, SB_WIDTH, D), SB_WIDTH ** -0.5 * DEEPNORM_BETA)
    conv_w = nrm(ks[12], (L, CONV_WIDTH, C), CONV_WIDTH ** -0.5)
    conv_b = nrm(ks[13], (L, C), 0.02)
    conv_ln_g = 1.0 + nrm(ks[14], (L, C), 0.02)
    conv_ln_b = nrm(ks[15], (L, C), 0.02)
    w_conv_out = nrm(ks[16], (L, C, D), C ** -0.5 * DEEPNORM_BETA)
    w_out = nrm(ks[17], (L, D, D), D ** -0.5 * DEEPNORM_BETA)
    ln2_g = 1.0 + nrm(ks[18], (L, D), 0.02)
    ln2_b = nrm(ks[19], (L, D), 0.02)
    ffn2_w_gu = nrm(ks[20], (L, D, 2 * D_FF), D ** -0.5)
    ffn2_w_down = nrm(ks[21], (L, D_FF, D), D_FF ** -0.5 * DEEPNORM_BETA)
    ln3_g = 1.0 + nrm(ks[22], (L, D), 0.02)
    ln3_b = nrm(ks[23], (L, D), 0.02)
    return {"x": x, "c": c, "w_ada": w_ada, "b_ada": b_ada,
            "ffn1_w_gu": ffn1_w_gu, "ffn1_w_down": ffn1_w_down, "ln1_g": ln1_g, "ln1_b": ln1_b,
            "w_in": w_in, "w_sb_out": w_sb_out, "conv_w": conv_w, "conv_b": conv_b,
            "conv_ln_g": conv_ln_g, "conv_ln_b": conv_ln_b, "w_conv_out": w_conv_out,
            "w_out": w_out, "ln2_g": ln2_g, "ln2_b": ln2_b,
            "ffn2_w_gu": ffn2_w_gu, "ffn2_w_down": ffn2_w_down, "ln3_g": ln3_g, "ln3_b": ln3_b}


def _fwd_reference(x, c, w_ada, b_ada, ffn1_w_gu, ffn1_w_down, ln1_g, ln1_b, w_in, w_sb_out,
              conv_w, conv_b, conv_ln_g, conv_ln_b, w_conv_out, w_out, ln2_g, ln2_b,
              ffn2_w_gu, ffn2_w_down, ln3_g, ln3_b):
    bsz, seq, _ = x.shape
    split_idx = list(np.cumsum(IN_SPLITS)[:-1])
    for l in range(DEPTH):
        mod = (jax.nn.silu(c) @ w_ada[l] + b_ada[l]).reshape(bsz, N_SUBLAYERS * N_MOD, 1, D_MODEL)
        sh1, sc1, g1, sh2, sc2, g2, sh3, sc3, g3 = [mod[:, i] for i in range(N_SUBLAYERS * N_MOD)]

        u = x * (1.0 + sc1) + sh1
        x = layer_norm(DEEPNORM_ALPHA * x + g1 * (MACARON_WEIGHT * swiglu(u, ffn1_w_gu[l], ffn1_w_down[l])),
                       ln1_g[l], ln1_b[l])

        u = x * (1.0 + sc2) + sh2
        q, k, v, glu_a, glu_b, gate_a, gate_b = jnp.split(u @ w_in[l], split_idx, axis=-1)
        heads = lambda t: t.reshape(bsz, seq, SB_HEADS, SB_HEAD_DIM).transpose(0, 2, 1, 3)
        y_sb = stick_breaking_attention(heads(q), heads(k), heads(v))
        y_sb = y_sb.transpose(0, 2, 1, 3).reshape(bsz, seq, SB_WIDTH) @ w_sb_out[l]
        y_conv = conformer_conv(glu_a, glu_b, conv_w[l], conv_b[l], conv_ln_g[l], conv_ln_b[l]) @ w_conv_out[l]
        merged = jax.nn.sigmoid(gate_a) * y_sb + jax.nn.sigmoid(gate_b) * y_conv
        x = layer_norm(DEEPNORM_ALPHA * x + g2 * (merged @ w_out[l]), ln2_g[l], ln2_b[l])

        u = x * (1.0 + sc3) + sh3
        x = layer_norm(DEEPNORM_ALPHA * x + g3 * (MACARON_WEIGHT * swiglu(u, ffn2_w_gu[l], ffn2_w_down[l])),
                       ln3_g[l], ln3_b[l])
    return x


import jax as _jax
import jax.numpy as _jnp

TWIN_FORMAT = 'train_step'
FWD_PARAMS = ['x', 'c', 'w_ada', 'b_ada', 'ffn1_w_gu', 'ffn1_w_down', 'ln1_g', 'ln1_b', 'w_in', 'w_sb_out', 'conv_w', 'conv_b', 'conv_ln_g', 'conv_ln_b', 'w_conv_out', 'w_out', 'ln2_g', 'ln2_b', 'ffn2_w_gu', 'ffn2_w_down', 'ln3_g', 'ln3_b']
TWIN_WEIGHTS = ['w_ada', 'b_ada', 'ffn1_w_gu', 'ffn1_w_down', 'ln1_g', 'ln1_b', 'w_in', 'w_sb_out', 'conv_w', 'conv_b', 'conv_ln_g', 'conv_ln_b', 'w_conv_out', 'w_out', 'ln2_g', 'ln2_b', 'ffn2_w_gu', 'ffn2_w_down', 'ln3_g', 'ln3_b']
TWIN_DIFF_INPUT = 'x'
TWIN_INPUTS = ['x', 'c', 'w_ada', 'b_ada', 'ffn1_w_gu', 'ffn1_w_down', 'ln1_g', 'ln1_b', 'w_in', 'w_sb_out', 'conv_w', 'conv_b', 'conv_ln_g', 'conv_ln_b', 'w_conv_out', 'w_out', 'ln2_g', 'ln2_b', 'ffn2_w_gu', 'ffn2_w_down', 'ln3_g', 'ln3_b', 'loss_target', 'm_w_ada', 'm_b_ada', 'm_ffn1_w_gu', 'm_ffn1_w_down', 'm_ln1_g', 'm_ln1_b', 'm_w_in', 'm_w_sb_out', 'm_conv_w', 'm_conv_b', 'm_conv_ln_g', 'm_conv_ln_b', 'm_w_conv_out', 'm_w_out', 'm_ln2_g', 'm_ln2_b', 'm_ffn2_w_gu', 'm_ffn2_w_down', 'm_ln3_g', 'm_ln3_b', 'v_w_ada', 'v_b_ada', 'v_ffn1_w_gu', 'v_ffn1_w_down', 'v_ln1_g', 'v_ln1_b', 'v_w_in', 'v_w_sb_out', 'v_conv_w', 'v_conv_b', 'v_conv_ln_g', 'v_conv_ln_b', 'v_w_conv_out', 'v_w_out', 'v_ln2_g', 'v_ln2_b', 'v_ffn2_w_gu', 'v_ffn2_w_down', 'v_ln3_g', 'v_ln3_b']
TWIN_OUTPUTS = ['loss', 'grad_x', 'grad_w_ada', 'grad_b_ada', 'grad_ffn1_w_gu', 'grad_ffn1_w_down', 'grad_ln1_g', 'grad_ln1_b', 'grad_w_in', 'grad_w_sb_out', 'grad_conv_w', 'grad_conv_b', 'grad_conv_ln_g', 'grad_conv_ln_b', 'grad_w_conv_out', 'grad_w_out', 'grad_ln2_g', 'grad_ln2_b', 'grad_ffn2_w_gu', 'grad_ffn2_w_down', 'grad_ln3_g', 'grad_ln3_b', 'delta_w_ada', 'delta_b_ada', 'delta_ffn1_w_gu', 'delta_ffn1_w_down', 'delta_ln1_g', 'delta_ln1_b', 'delta_w_in', 'delta_w_sb_out', 'delta_conv_w', 'delta_conv_b', 'delta_conv_ln_g', 'delta_conv_ln_b', 'delta_w_conv_out', 'delta_w_out', 'delta_ln2_g', 'delta_ln2_b', 'delta_ffn2_w_gu', 'delta_ffn2_w_down', 'delta_ln3_g', 'delta_ln3_b', 'new_m_w_ada', 'new_m_b_ada', 'new_m_ffn1_w_gu', 'new_m_ffn1_w_down', 'new_m_ln1_g', 'new_m_ln1_b', 'new_m_w_in', 'new_m_w_sb_out', 'new_m_conv_w', 'new_m_conv_b', 'new_m_conv_ln_g', 'new_m_conv_ln_b', 'new_m_w_conv_out', 'new_m_w_out', 'new_m_ln2_g', 'new_m_ln2_b', 'new_m_ffn2_w_gu', 'new_m_ffn2_w_down', 'new_m_ln3_g', 'new_m_ln3_b', 'new_v_w_ada', 'new_v_b_ada', 'new_v_ffn1_w_gu', 'new_v_ffn1_w_down', 'new_v_ln1_g', 'new_v_ln1_b', 'new_v_w_in', 'new_v_w_sb_out', 'new_v_conv_w', 'new_v_conv_b', 'new_v_conv_ln_g', 'new_v_conv_ln_b', 'new_v_w_conv_out', 'new_v_w_out', 'new_v_ln2_g', 'new_v_ln2_b', 'new_v_ffn2_w_gu', 'new_v_ffn2_w_down', 'new_v_ln3_g', 'new_v_ln3_b']
TWIN_LEAF_KINDS = {'loss': 'loss', 'grad_x': 'grad_x', 'grad_w_ada': 'grad_w', 'grad_b_ada': 'grad_w', 'grad_ffn1_w_gu': 'grad_w', 'grad_ffn1_w_down': 'grad_w', 'grad_ln1_g': 'grad_w', 'grad_ln1_b': 'grad_w', 'grad_w_in': 'grad_w', 'grad_w_sb_out': 'grad_w', 'grad_conv_w': 'grad_w', 'grad_conv_b': 'grad_w', 'grad_conv_ln_g': 'grad_w', 'grad_conv_ln_b': 'grad_w', 'grad_w_conv_out': 'grad_w', 'grad_w_out': 'grad_w', 'grad_ln2_g': 'grad_w', 'grad_ln2_b': 'grad_w', 'grad_ffn2_w_gu': 'grad_w', 'grad_ffn2_w_down': 'grad_w', 'grad_ln3_g': 'grad_w', 'grad_ln3_b': 'grad_w', 'delta_w_ada': 'delta_w', 'delta_b_ada': 'delta_w', 'delta_ffn1_w_gu': 'delta_w', 'delta_ffn1_w_down': 'delta_w', 'delta_ln1_g': 'delta_w', 'delta_ln1_b': 'delta_w', 'delta_w_in': 'delta_w', 'delta_w_sb_out': 'delta_w', 'delta_conv_w': 'delta_w', 'delta_conv_b': 'delta_w', 'delta_conv_ln_g': 'delta_w', 'delta_conv_ln_b': 'delta_w', 'delta_w_conv_out': 'delta_w', 'delta_w_out': 'delta_w', 'delta_ln2_g': 'delta_w', 'delta_ln2_b': 'delta_w', 'delta_ffn2_w_gu': 'delta_w', 'delta_ffn2_w_down': 'delta_w', 'delta_ln3_g': 'delta_w', 'delta_ln3_b': 'delta_w', 'new_m_w_ada': 'new_m', 'new_m_b_ada': 'new_m', 'new_m_ffn1_w_gu': 'new_m', 'new_m_ffn1_w_down': 'new_m', 'new_m_ln1_g': 'new_m', 'new_m_ln1_b': 'new_m', 'new_m_w_in': 'new_m', 'new_m_w_sb_out': 'new_m', 'new_m_conv_w': 'new_m', 'new_m_conv_b': 'new_m', 'new_m_conv_ln_g': 'new_m', 'new_m_conv_ln_b': 'new_m', 'new_m_w_conv_out': 'new_m', 'new_m_w_out': 'new_m', 'new_m_ln2_g': 'new_m', 'new_m_ln2_b': 'new_m', 'new_m_ffn2_w_gu': 'new_m', 'new_m_ffn2_w_down': 'new_m', 'new_m_ln3_g': 'new_m', 'new_m_ln3_b': 'new_m', 'new_v_w_ada': 'new_v', 'new_v_b_ada': 'new_v', 'new_v_ffn1_w_gu': 'new_v', 'new_v_ffn1_w_down': 'new_v', 'new_v_ln1_g': 'new_v', 'new_v_ln1_b': 'new_v', 'new_v_w_in': 'new_v', 'new_v_w_sb_out': 'new_v', 'new_v_conv_w': 'new_v', 'new_v_conv_b': 'new_v', 'new_v_conv_ln_g': 'new_v', 'new_v_conv_ln_b': 'new_v', 'new_v_w_conv_out': 'new_v', 'new_v_w_out': 'new_v', 'new_v_ln2_g': 'new_v', 'new_v_ln2_b': 'new_v', 'new_v_ffn2_w_gu': 'new_v', 'new_v_ffn2_w_down': 'new_v', 'new_v_ln3_g': 'new_v', 'new_v_ln3_b': 'new_v'}


def _forward(args):
    return _fwd_reference(*[args[k] for k in FWD_PARAMS])


def _output_shape():
    out = _jax.eval_shape(lambda: _forward(_fwd_setup_inputs(0)))
    return out.shape, out.dtype

N_MICROBATCH = 1
ADAM_LR = 0.001
ADAM_B1 = 0.9
ADAM_B2 = 0.999
ADAM_EPS = 1e-08
ADAM_WD = 0.01
ADAM_STEP = 10
PER_EXAMPLE_BATCH_AXIS = {'x': 0, 'c': 0, 'loss_target': 0}
SHARED_INPUTS = []
_WEIGHT_DTYPES = {'w_ada': _jnp.float32, 'b_ada': _jnp.float32, 'ffn1_w_gu': _jnp.float32, 'ffn1_w_down': _jnp.float32, 'ln1_g': _jnp.float32, 'ln1_b': _jnp.float32, 'w_in': _jnp.float32, 'w_sb_out': _jnp.float32, 'conv_w': _jnp.float32, 'conv_b': _jnp.float32, 'conv_ln_g': _jnp.float32, 'conv_ln_b': _jnp.float32, 'w_conv_out': _jnp.float32, 'w_out': _jnp.float32, 'ln2_g': _jnp.float32, 'ln2_b': _jnp.float32, 'ffn2_w_gu': _jnp.float32, 'ffn2_w_down': _jnp.float32, 'ln3_g': _jnp.float32, 'ln3_b': _jnp.float32}
MOMENT_SCALE = {'w_ada': 1.705465e-02, 'b_ada': 2.807092e-02, 'ffn1_w_gu': 8.386618e-03, 'ffn1_w_down': 2.313511e-02, 'ln1_g': 1.915132e+00, 'ln1_b': 7.301119e-01, 'w_in': 5.512558e-03, 'w_sb_out': 1.025296e-02, 'conv_w': 8.150519e-03, 'conv_b': 1.669784e-02, 'conv_ln_g': 1.173027e-02, 'conv_ln_b': 1.183940e-02, 'w_conv_out': 1.420856e-02, 'w_out': 1.749106e-02, 'ln2_g': 1.907262e+00, 'ln2_b': 7.287659e-01, 'ffn2_w_gu': 8.533140e-03, 'ffn2_w_down': 2.355121e-02, 'ln3_g': 6.397660e+01, 'ln3_b': 1.422484e+00}


def _to_microbatches(a, axis):
    t = _jnp.moveaxis(a, axis, 0)
    t = t.reshape((N_MICROBATCH, t.shape[0] // N_MICROBATCH) + t.shape[1:])
    return _jnp.moveaxis(t, 1, axis + 1)


def setup_inputs(seed: int = 0) -> dict:
    inp = _fwd_setup_inputs(seed)
    key = _jax.random.fold_in(_jax.random.key(seed), 7919)
    shape, _ = _output_shape()
    out = dict(inp)
    out["loss_target"] = _jax.random.normal(_jax.random.fold_in(key, 0), shape, _jnp.float32)
    for i, name in enumerate(TWIN_WEIGHTS):
        w = inp[name].astype(_jnp.float32)
        if MOMENT_SCALE is None:
            s = _jnp.sqrt(_jnp.mean(_jnp.square(w)) + 1e-30)
        else:
            s = MOMENT_SCALE[name]
        km, kv = _jax.random.split(_jax.random.fold_in(key, i + 1))
        out[name] = w
        out["m_" + name] = s * _jax.random.normal(km, w.shape, _jnp.float32)
        out["v_" + name] = (s * s) * _jax.random.uniform(kv, w.shape, _jnp.float32, 0.5, 1.5)
    if N_MICROBATCH > 1:
        for name, axis in PER_EXAMPLE_BATCH_AXIS.items():
            out[name] = _to_microbatches(out[name], axis)
    return {'x': out['x'], 'c': out['c'], 'w_ada': out['w_ada'], 'b_ada': out['b_ada'], 'ffn1_w_gu': out['ffn1_w_gu'], 'ffn1_w_down': out['ffn1_w_down'], 'ln1_g': out['ln1_g'], 'ln1_b': out['ln1_b'], 'w_in': out['w_in'], 'w_sb_out': out['w_sb_out'], 'conv_w': out['conv_w'], 'conv_b': out['conv_b'], 'conv_ln_g': out['conv_ln_g'], 'conv_ln_b': out['conv_ln_b'], 'w_conv_out': out['w_conv_out'], 'w_out': out['w_out'], 'ln2_g': out['ln2_g'], 'ln2_b': out['ln2_b'], 'ffn2_w_gu': out['ffn2_w_gu'], 'ffn2_w_down': out['ffn2_w_down'], 'ln3_g': out['ln3_g'], 'ln3_b': out['ln3_b'], 'loss_target': out['loss_target'], 'm_w_ada': out['m_w_ada'], 'm_b_ada': out['m_b_ada'], 'm_ffn1_w_gu': out['m_ffn1_w_gu'], 'm_ffn1_w_down': out['m_ffn1_w_down'], 'm_ln1_g': out['m_ln1_g'], 'm_ln1_b': out['m_ln1_b'], 'm_w_in': out['m_w_in'], 'm_w_sb_out': out['m_w_sb_out'], 'm_conv_w': out['m_conv_w'], 'm_conv_b': out['m_conv_b'], 'm_conv_ln_g': out['m_conv_ln_g'], 'm_conv_ln_b': out['m_conv_ln_b'], 'm_w_conv_out': out['m_w_conv_out'], 'm_w_out': out['m_w_out'], 'm_ln2_g': out['m_ln2_g'], 'm_ln2_b': out['m_ln2_b'], 'm_ffn2_w_gu': out['m_ffn2_w_gu'], 'm_ffn2_w_down': out['m_ffn2_w_down'], 'm_ln3_g': out['m_ln3_g'], 'm_ln3_b': out['m_ln3_b'], 'v_w_ada': out['v_w_ada'], 'v_b_ada': out['v_b_ada'], 'v_ffn1_w_gu': out['v_ffn1_w_gu'], 'v_ffn1_w_down': out['v_ffn1_w_down'], 'v_ln1_g': out['v_ln1_g'], 'v_ln1_b': out['v_ln1_b'], 'v_w_in': out['v_w_in'], 'v_w_sb_out': out['v_w_sb_out'], 'v_conv_w': out['v_conv_w'], 'v_conv_b': out['v_conv_b'], 'v_conv_ln_g': out['v_conv_ln_g'], 'v_conv_ln_b': out['v_conv_ln_b'], 'v_w_conv_out': out['v_w_conv_out'], 'v_w_out': out['v_w_out'], 'v_ln2_g': out['v_ln2_g'], 'v_ln2_b': out['v_ln2_b'], 'v_ffn2_w_gu': out['v_ffn2_w_gu'], 'v_ffn2_w_down': out['v_ffn2_w_down'], 'v_ln3_g': out['v_ln3_g'], 'v_ln3_b': out['v_ln3_b']}


def _loss(weights, diff, rest, loss_target):
    with _jax.named_scope("forward"):
        args = {**rest, TWIN_DIFF_INPUT: diff, **{k: w.astype(_WEIGHT_DTYPES[k]) for k, w in weights.items()}}
        y = _forward(args)
    with _jax.named_scope("loss_head"):
        err = _jnp.square(y.astype(_jnp.float32) - loss_target)
        return 0.5 * _jnp.sum(_jnp.mean(err, axis=-1)) if err.ndim else 0.5 * err


def _adamw(w, g, m, v):
    m = ADAM_B1 * m + (1.0 - ADAM_B1) * g
    v = ADAM_B2 * v + (1.0 - ADAM_B2) * _jnp.square(g)
    m_hat = m / (1.0 - ADAM_B1 ** ADAM_STEP)
    v_hat = v / (1.0 - ADAM_B2 ** ADAM_STEP)
    delta = -ADAM_LR * (m_hat / (_jnp.sqrt(v_hat) + ADAM_EPS) + ADAM_WD * w)
    return delta, m, v


def reference(x, c, w_ada, b_ada, ffn1_w_gu, ffn1_w_down, ln1_g, ln1_b, w_in, w_sb_out, conv_w, conv_b, conv_ln_g, conv_ln_b, w_conv_out, w_out, ln2_g, ln2_b, ffn2_w_gu, ffn2_w_down, ln3_g, ln3_b, loss_target, m_w_ada, m_b_ada, m_ffn1_w_gu, m_ffn1_w_down, m_ln1_g, m_ln1_b, m_w_in, m_w_sb_out, m_conv_w, m_conv_b, m_conv_ln_g, m_conv_ln_b, m_w_conv_out, m_w_out, m_ln2_g, m_ln2_b, m_ffn2_w_gu, m_ffn2_w_down, m_ln3_g, m_ln3_b, v_w_ada, v_b_ada, v_ffn1_w_gu, v_ffn1_w_down, v_ln1_g, v_ln1_b, v_w_in, v_w_sb_out, v_conv_w, v_conv_b, v_conv_ln_g, v_conv_ln_b, v_w_conv_out, v_w_out, v_ln2_g, v_ln2_b, v_ffn2_w_gu, v_ffn2_w_down, v_ln3_g, v_ln3_b):
    given = dict(x=x, c=c, w_ada=w_ada, b_ada=b_ada, ffn1_w_gu=ffn1_w_gu, ffn1_w_down=ffn1_w_down, ln1_g=ln1_g, ln1_b=ln1_b, w_in=w_in, w_sb_out=w_sb_out, conv_w=conv_w, conv_b=conv_b, conv_ln_g=conv_ln_g, conv_ln_b=conv_ln_b, w_conv_out=w_conv_out, w_out=w_out, ln2_g=ln2_g, ln2_b=ln2_b, ffn2_w_gu=ffn2_w_gu, ffn2_w_down=ffn2_w_down, ln3_g=ln3_g, ln3_b=ln3_b, loss_target=loss_target, m_w_ada=m_w_ada, m_b_ada=m_b_ada, m_ffn1_w_gu=m_ffn1_w_gu, m_ffn1_w_down=m_ffn1_w_down, m_ln1_g=m_ln1_g, m_ln1_b=m_ln1_b, m_w_in=m_w_in, m_w_sb_out=m_w_sb_out, m_conv_w=m_conv_w, m_conv_b=m_conv_b, m_conv_ln_g=m_conv_ln_g, m_conv_ln_b=m_conv_ln_b, m_w_conv_out=m_w_conv_out, m_w_out=m_w_out, m_ln2_g=m_ln2_g, m_ln2_b=m_ln2_b, m_ffn2_w_gu=m_ffn2_w_gu, m_ffn2_w_down=m_ffn2_w_down, m_ln3_g=m_ln3_g, m_ln3_b=m_ln3_b, v_w_ada=v_w_ada, v_b_ada=v_b_ada, v_ffn1_w_gu=v_ffn1_w_gu, v_ffn1_w_down=v_ffn1_w_down, v_ln1_g=v_ln1_g, v_ln1_b=v_ln1_b, v_w_in=v_w_in, v_w_sb_out=v_w_sb_out, v_conv_w=v_conv_w, v_conv_b=v_conv_b, v_conv_ln_g=v_conv_ln_g, v_conv_ln_b=v_conv_ln_b, v_w_conv_out=v_w_conv_out, v_w_out=v_w_out, v_ln2_g=v_ln2_g, v_ln2_b=v_ln2_b, v_ffn2_w_gu=v_ffn2_w_gu, v_ffn2_w_down=v_ffn2_w_down, v_ln3_g=v_ln3_g, v_ln3_b=v_ln3_b)
    weights = {n: given[n] for n in TWIN_WEIGHTS}
    shared = {n: given[n] for n in SHARED_INPUTS}
    per_example = {n: given[n] for n in ['x', 'c']}
    grad_fn = _jax.value_and_grad(_loss, argnums=(0, 1))

    def one_microbatch(ex, loss_target):
        ex = dict(ex)
        diff = ex.pop(TWIN_DIFF_INPUT)
        return grad_fn(weights, diff, {**shared, **ex}, loss_target)

    if N_MICROBATCH == 1:
        loss, (grad_w, grad_x) = one_microbatch(per_example, given["loss_target"])
    else:
        def body(carry, xs):
            loss_sum, grad_sum = carry
            l_k, (gw_k, gx_k) = one_microbatch(xs[0], xs[1])
            with _jax.named_scope("update"):
                return (loss_sum + l_k, _jax.tree.map(_jnp.add, grad_sum, gw_k)), gx_k

        init = (_jnp.zeros((), _jnp.float32), _jax.tree.map(_jnp.zeros_like, weights))
        (loss, grad_w), grad_x = _jax.lax.scan(body, init, (per_example, given["loss_target"]))
    with _jax.named_scope("update"):
        delta_w, new_m, new_v = {}, {}, {}
        for n in TWIN_WEIGHTS:
            delta_w[n], new_m[n], new_v[n] = _adamw(weights[n], grad_w[n], given["m_" + n], given["v_" + n])
    return (loss, grad_x, *[grad_w[n] for n in TWIN_WEIGHTS], *[delta_w[n] for n in TWIN_WEIGHTS],
            *[new_m[n] for n in TWIN_WEIGHTS], *[new_v[n] for n in TWIN_WEIGHTS])
```

```python
import functools

import jax
import jax.numpy as jnp
from jax import lax
from jax.experimental import pallas as pl
from jax.experimental.pallas import tpu as pltpu

F32 = jnp.float32
BF16 = jnp.bfloat16

D_MODEL = 1024
D_FF = 2816
HEADS = 16
HEAD_DIM = 64
LANES = 128
CONV_TAPS = 31
HALO = 32
N_CHIPS = 4
N_DEV = 8
ALPHA = 2.0 ** 0.25
LN_EPS = 1e-5
ATT_BLOCK = 128
VMEM_LIMIT = 56 * 1024 * 1024

ADAM_LR = 0.001
ADAM_B1 = 0.9
ADAM_B2 = 0.999
ADAM_EPS = 1e-08
ADAM_WD = 0.01
ADAM_STEP = 10

MESH = pl.DeviceIdType.MESH


def _pick(n, cands):
    for t in cands:
        if t <= n and n % t == 0:
            return t
    return n


def _params(sem):
    return pltpu.CompilerParams(dimension_semantics=sem, vmem_limit_bytes=VMEM_LIMIT)


def _sigmoid(z):
    t = jnp.exp(-jnp.abs(z))
    return jnp.where(z >= 0, 1.0, t) / (1.0 + t)


def _silu(z):
    return z * _sigmoid(z)


def _dsilu(z):
    s = _sigmoid(z)
    return s * (1.0 + z * (1.0 - s))


def _ln_stats(r):
    mu = jnp.mean(r, axis=-1, keepdims=True)
    d = r - mu
    var = jnp.mean(d * d, axis=-1, keepdims=True)
    rstd = lax.rsqrt(var + LN_EPS)
    return d * rstd, rstd


def _colsum(v):
    return jnp.sum(v, axis=0, keepdims=True)


_DIMS = {"nn": (((1,), (0,)), ((), ())), "nt": (((1,), (1,)), ((), ())), "tn": (((0,), (0,)), ((), ()))}
_TN_CANDS = (1408, 1792, 1152, 1024, 512, 256, 128)
_TK_CANDS = (1024, 1408, 896, 512, 256, 128)


def _matmul(a, b, *, mode, out_dtype, name, bias=None, out_stacked=False):
    b_stacked = b.ndim == 3
    if mode == "nn":
        m, k = a.shape
        n_c = b.shape[-1]
        n = n_c * (N_CHIPS if b_stacked else 1)
        k_c = k
    elif mode == "nt":
        m, k = a.shape
        n = b.shape[-2]
        k_c = b.shape[-1]
        n_c = n
    else:
        k, m = a.shape
        n = b.shape[1]
        n_c = n // N_CHIPS if out_stacked else n
        k_c = k
    if mode == "tn":
        tm = _pick(m, (1024, 1408, 512, 256, 128))
        tk = _pick(k, (512, 256, 128, 64, 32, 16))
    else:
        tm = _pick(m, (1024, 512, 256, 128, 64, 32, 16))
        tk = _pick(k_c, _TK_CANDS)
    tn = _pick(n_c, _TN_CANDS)
    nb = n_c // tn
    kb = k_c // tk
    nk = k // tk
    grid = (m // tm, n // tn, nk)

    if mode == "nn":
        a_spec = pl.BlockSpec((tm, tk), lambda i, j, kk: (i, kk))
        if b_stacked:
            b_spec = pl.BlockSpec((None, tk, tn), lambda i, j, kk: (j // nb, kk, j % nb))
        else:
            b_spec = pl.BlockSpec((tk, tn), lambda i, j, kk: (kk, j))
    elif mode == "nt":
        a_spec = pl.BlockSpec((tm, tk), lambda i, j, kk: (i, kk))
        if b_stacked:
            b_spec = pl.BlockSpec((None, tn, tk), lambda i, j, kk: (kk // kb, j, kk % kb))
        else:
            b_spec = pl.BlockSpec((tn, tk), lambda i, j, kk: (j, kk))
    else:
        a_spec = pl.BlockSpec((tk, tm), lambda i, j, kk: (kk, i))
        b_spec = pl.BlockSpec((tk, tn), lambda i, j, kk: (kk, j))
    if out_stacked:
        out_shape = jax.ShapeDtypeStruct((N_CHIPS, m, n_c), out_dtype)
        o_spec = pl.BlockSpec((None, tm, tn), lambda i, j, kk: (j // nb, i, j % nb))
    else:
        out_shape = jax.ShapeDtypeStruct((m, n), out_dtype)
        o_spec = pl.BlockSpec((tm, tn), lambda i, j, kk: (i, j))
    in_specs = [a_spec, b_spec]
    args = [a, b]
    if bias is not None:
        in_specs.append(pl.BlockSpec((1, tn), lambda i, j, kk: (0, j)))
        args.append(bias)
    dims = _DIMS[mode]

    def body(*refs):
        a_ref, b_ref = refs[0], refs[1]
        bias_ref = refs[2] if bias is not None else None
        o_ref, acc_ref = refs[-2], refs[-1]
        kk = pl.program_id(2)

        @pl.when(kk == 0)
        def _():
            acc_ref[...] = jnp.zeros_like(acc_ref)

        acc_ref[...] += lax.dot_general(a_ref[...], b_ref[...], dims, preferred_element_type=F32)

        @pl.when(kk == nk - 1)
        def _():
            r = acc_ref[...]
            if bias_ref is not None:
                r = r + bias_ref[...]
            o_ref[...] = r.astype(o_ref.dtype)

    return pl.pallas_call(
        body, name=name, grid=grid, in_specs=in_specs, out_specs=o_spec, out_shape=out_shape,
        scratch_shapes=[pltpu.VMEM((tm, tn), F32)],
        compiler_params=_params(("parallel", "parallel", "arbitrary")),
    )(*args)


def _row_grid(bsz, seq, ts):
    ns = seq // ts
    return (bsz, ns), ns


def _rows(ts, width, ns, col=0):
    return pl.BlockSpec((ts, width), lambda b, s: (b * ns + s, col))


def _mod_spec():
    return pl.BlockSpec((None, 9, D_MODEL), lambda b, s: (b, 0, 0))


def _vec_spec(rows=1, width=D_MODEL):
    return pl.BlockSpec((rows, width), lambda b, s: (0, 0))


def _silu_pad(c):
    bsz = c.shape[0]

    def body(c_ref, o_ref):
        o_ref[...] = jnp.zeros_like(o_ref)
        o_ref[0:bsz, :] = _silu(c_ref[...]).astype(BF16)

    return pl.pallas_call(body, name="silu_pad", out_shape=jax.ShapeDtypeStruct((16, D_MODEL), BF16))(c)


def _mod_in(x, mod, bsz, seq, sub):
    ts = _pick(seq, (512, 256, 128))
    grid, ns = _row_grid(bsz, seq, ts)

    def body(x_ref, mod_ref, u_ref):
        sh = mod_ref[3 * sub:3 * sub + 1, :]
        sc = mod_ref[3 * sub + 1:3 * sub + 2, :]
        u_ref[...] = (x_ref[...] * (1.0 + sc) + sh).astype(BF16)

    return pl.pallas_call(
        body, name=f"mod_in{sub}", grid=grid, in_specs=[_rows(ts, D_MODEL, ns), _mod_spec()],
        out_specs=_rows(ts, D_MODEL, ns), out_shape=jax.ShapeDtypeStruct(x.shape, BF16),
        compiler_params=_params(("parallel", "parallel")),
    )(x, mod)


def _swiglu_act(h, bsz, seq, name):
    ts = _pick(seq, (256, 128))
    grid, ns = _row_grid(bsz, seq, ts)

    def body(h_ref, p_ref):
        a = h_ref[:, :D_FF]
        g = h_ref[:, D_FF:]
        p_ref[...] = (_silu(a) * g).astype(BF16)

    return pl.pallas_call(
        body, name=name, grid=grid, in_specs=[_rows(ts, 2 * D_FF, ns)],
        out_specs=_rows(ts, D_FF, ns), out_shape=jax.ShapeDtypeStruct((h.shape[0], D_FF), BF16),
        compiler_params=_params(("parallel", "parallel")),
    )(h)


def _swiglu_bwd(h, dp, bsz, seq, name):
    ts = _pick(seq, (256, 128))
    grid, ns = _row_grid(bsz, seq, ts)

    def body(h_ref, dp_ref, dh_ref):
        a = h_ref[:, :D_FF]
        g = h_ref[:, D_FF:]
        dp_v = dp_ref[...]
        dh_ref[:, :D_FF] = (dp_v * g * _dsilu(a)).astype(BF16)
        dh_ref[:, D_FF:] = (dp_v * _silu(a)).astype(BF16)

    return pl.pallas_call(
        body, name=name, grid=grid, in_specs=[_rows(ts, 2 * D_FF, ns), _rows(ts, D_FF, ns)],
        out_specs=_rows(ts, 2 * D_FF, ns), out_shape=jax.ShapeDtypeStruct(h.shape, BF16),
        compiler_params=_params(("parallel", "parallel")),
    )(h, dp)


def _res_ln_fwd(x, f, mod, ln_g, ln_b, bsz, seq, sub, weight, target=None):
    ts = _pick(seq, (256, 128))
    grid, ns = _row_grid(bsz, seq, ts)
    last = target is not None

    def body(*refs):
        x_ref, f_ref, mod_ref, g_ref, b_ref = refs[:5]
        gate = mod_ref[3 * sub + 2:3 * sub + 3, :]
        r = ALPHA * x_ref[...] + gate * (weight * f_ref[...])
        xhat, _ = _ln_stats(r)
        xo = xhat * g_ref[...] + b_ref[...]
        if last:
            t_ref, r_ref, dy_ref, loss_ref = refs[5:]
            diff = xo - t_ref[...]
            dy_ref[...] = diff * (1.0 / D_MODEL)
            part = 0.5 * jnp.sum(jnp.mean(diff * diff, axis=-1, keepdims=True), axis=0, keepdims=True)

            @pl.when((pl.program_id(0) == 0) & (pl.program_id(1) == 0))
            def _():
                loss_ref[...] = jnp.zeros_like(loss_ref)

            loss_ref[...] += jnp.broadcast_to(part, loss_ref.shape)
        else:
            r_ref, xo_ref, u_ref = refs[5:]
            xo_ref[...] = xo
            sh = mod_ref[3 * sub + 3:3 * sub + 4, :]
            sc = mod_ref[3 * sub + 4:3 * sub + 5, :]
            u_ref[...] = (xo * (1.0 + sc) + sh).astype(BF16)
        r_ref[...] = r

    row = _rows(ts, D_MODEL, ns)
    in_specs = [row, row, _mod_spec(), _vec_spec(), _vec_spec()]
    args = [x, f, mod, ln_g, ln_b]
    if last:
        in_specs.append(row)
        args.append(target)
        out_specs = [row, row, _vec_spec(8, LANES)]
        out_shape = [jax.ShapeDtypeStruct(x.shape, F32), jax.ShapeDtypeStruct(x.shape, F32),
                     jax.ShapeDtypeStruct((8, LANES), F32)]
        sem = ("arbitrary", "arbitrary")
    else:
        out_specs = [row, row, row]
        out_shape = [jax.ShapeDtypeStruct(x.shape, F32), jax.ShapeDtypeStruct(x.shape, F32),
                     jax.ShapeDtypeStruct(x.shape, BF16)]
        sem = ("parallel", "parallel")
    return pl.pallas_call(
        body, name=f"res_ln_fwd{sub}", grid=grid, in_specs=in_specs, out_specs=out_specs, out_shape=out_shape,
        compiler_params=_params(sem),
    )(*args)


def _res_ln_bwd(r, dxo, f, mod, ln_g, bsz, seq, sub, weight):
    ts = _pick(seq, (256, 128))
    grid, ns = _row_grid(bsz, seq, ts)

    def body(r_ref, dxo_ref, f_ref, mod_ref, g_ref, dxres_ref, df_ref, lns_ref, gs_ref):
        b, s = pl.program_id(0), pl.program_id(1)
        gate = mod_ref[3 * sub + 2:3 * sub + 3, :]
        xhat, rstd = _ln_stats(r_ref[...])
        dxo_v = dxo_ref[...]
        dxhat = dxo_v * g_ref[...]
        m1 = jnp.mean(dxhat, axis=-1, keepdims=True)
        m2 = jnp.mean(dxhat * xhat, axis=-1, keepdims=True)
        dr = rstd * (dxhat - m1 - xhat * m2)
        dxres_ref[...] = ALPHA * dr
        df_ref[...] = (dr * (gate * weight)).astype(BF16)

        @pl.when((b == 0) & (s == 0))
        def _():
            lns_ref[...] = jnp.zeros_like(lns_ref)

        @pl.when(s == 0)
        def _():
            gs_ref[...] = jnp.zeros_like(gs_ref)

        lns_ref[0:1, :] += _colsum(dxo_v * xhat)
        lns_ref[1:2, :] += _colsum(dxo_v)
        gs_ref[0:1, :] += _colsum(dr * (weight * f_ref[...]))

    row = _rows(ts, D_MODEL, ns)
    return pl.pallas_call(
        body, name=f"res_ln_bwd{sub}", grid=grid,
        in_specs=[row, row, row, _mod_spec(), _vec_spec()],
        out_specs=[row, row, _vec_spec(8), pl.BlockSpec((None, 8, D_MODEL), lambda b, s: (b, 0, 0))],
        out_shape=[jax.ShapeDtypeStruct(r.shape, F32), jax.ShapeDtypeStruct(r.shape, BF16),
                   jax.ShapeDtypeStruct((8, D_MODEL), F32), jax.ShapeDtypeStruct((bsz, 8, D_MODEL), F32)],
        compiler_params=_params(("arbitrary", "arbitrary")),
    )(r, dxo, f, mod, ln_g)


def _mod_bwd(dxres, du, x, mod, bsz, seq, sub):
    ts = _pick(seq, (256, 128))
    grid, ns = _row_grid(bsz, seq, ts)

    def body(dxres_ref, du_ref, x_ref, mod_ref, dx_ref, st_ref):
        s = pl.program_id(1)
        sc = mod_ref[3 * sub + 1:3 * sub + 2, :]
        du_v = du_ref[...]
        dx_ref[...] = dxres_ref[...] + du_v * (1.0 + sc)

        @pl.when(s == 0)
        def _():
            st_ref[...] = jnp.zeros_like(st_ref)

        st_ref[0:1, :] += _colsum(du_v)
        st_ref[1:2, :] += _colsum(du_v * x_ref[...])

    row = _rows(ts, D_MODEL, ns)
    return pl.pallas_call(
        body, name=f"mod_bwd{sub}", grid=grid, in_specs=[row, row, row, _mod_spec()],
        out_specs=[row, pl.BlockSpec((None, 8, D_MODEL), lambda b, s: (b, 0, 0))],
        out_shape=[jax.ShapeDtypeStruct(x.shape, F32), jax.ShapeDtypeStruct((bsz, 8, D_MODEL), F32)],
        compiler_params=_params(("parallel", "arbitrary")),
    )(dxres, du, x, mod)


_COL_GLU_A, _COL_GLU_B, _COL_GATE_A, _COL_GATE_B = 3, 4, 5, 6


def _merge_fwd(proj, ysb, yconv, bsz, seq):
    ts = _pick(seq, (512, 256, 128))
    grid, ns = _row_grid(bsz, seq, ts)

    def body(ga_ref, gb_ref, ysb_ref, yc_ref, o_ref):
        o_ref[...] = (_sigmoid(ga_ref[...]) * ysb_ref[...] + _sigmoid(gb_ref[...]) * yc_ref[...]).astype(BF16)

    row = _rows(ts, D_MODEL, ns)
    return pl.pallas_call(
        body, name="merge_fwd", grid=grid,
        in_specs=[_rows(ts, D_MODEL, ns, _COL_GATE_A), _rows(ts, D_MODEL, ns, _COL_GATE_B), row, row],
        out_specs=row, out_shape=jax.ShapeDtypeStruct(ysb.shape, BF16),
        compiler_params=_params(("parallel", "parallel")),
    )(proj, proj, ysb, yconv)


def _merge_bwd(proj, ysb, yconv, dmerged, bsz, seq):
    ts = _pick(seq, (256, 128))
    grid, ns = _row_grid(bsz, seq, ts)

    def body(ga_ref, gb_ref, ysb_ref, yc_ref, dm_ref, dysb_ref, dyc_ref, dg_ref):
        sa = _sigmoid(ga_ref[...])
        sb = _sigmoid(gb_ref[...])
        dm = dm_ref[...]
        dysb_ref[...] = (dm * sa).astype(BF16)
        dyc_ref[...] = (dm * sb).astype(BF16)
        dg_ref[:, :D_MODEL] = (dm * ysb_ref[...] * sa * (1.0 - sa)).astype(BF16)
        dg_ref[:, D_MODEL:] = (dm * yc_ref[...] * sb * (1.0 - sb)).astype(BF16)

    row = _rows(ts, D_MODEL, ns)
    t = ysb.shape[0]
    return pl.pallas_call(
        body, name="merge_bwd", grid=grid,
        in_specs=[_rows(ts, D_MODEL, ns, _COL_GATE_A), _rows(ts, D_MODEL, ns, _COL_GATE_B), row, row, row],
        out_specs=[row, row, _rows(ts, 2 * D_MODEL, ns)],
        out_shape=[jax.ShapeDtypeStruct((t, D_MODEL), BF16), jax.ShapeDtypeStruct((t, D_MODEL), BF16),
                   jax.ShapeDtypeStruct((t, 2 * D_MODEL), BF16)],
        compiler_params=_params(("parallel", "parallel")),
    )(proj, proj, ysb, yconv, dmerged)


_CONV_ROWS = 128


def _halo_prev(tt, ns, col):
    r = tt // HALO
    return pl.BlockSpec((HALO, D_MODEL), lambda b, s: (jnp.maximum((b * ns + s) * r - 1, 0), col))


def _halo_next(tt, ns, nblk, col):
    r = tt // HALO
    return pl.BlockSpec((HALO, D_MODEL), lambda b, s: (jnp.minimum((b * ns + s + 1) * r, nblk - 1), col))


def _fill_hc(hpad, a_ref, b_ref, ha_ref, hb_ref, s):
    halo = ha_ref[...] * _sigmoid(hb_ref[...])
    hpad[0:HALO, :] = jnp.where(s > 0, halo, 0.0)
    hpad[HALO:, :] = a_ref[...] * _sigmoid(b_ref[...])


def _conv_fwd(proj, conv_w, conv_b, ln_g, ln_b, bsz, seq):
    tt = _CONV_ROWS
    grid, ns = _row_grid(bsz, seq, tt)
    off = HALO - (CONV_TAPS - 1)

    def body(a_ref, b_ref, ha_ref, hb_ref, w_ref, cb_ref, g_ref, bb_ref, cs_ref, cv_ref, hpad):
        _fill_hc(hpad, a_ref, b_ref, ha_ref, hb_ref, pl.program_id(1))
        acc = jnp.zeros((tt, D_MODEL), F32)
        for j in range(CONV_TAPS):
            acc = acc + w_ref[j:j + 1, :] * hpad[off + j:off + j + tt, :]
        cv = acc + cb_ref[...]
        cv_ref[...] = cv
        xhat, _ = _ln_stats(cv)
        cs_ref[...] = _silu(xhat * g_ref[...] + bb_ref[...]).astype(BF16)

    row = _rows(tt, D_MODEL, ns)
    t = proj.shape[0]
    return pl.pallas_call(
        body, name="conv_fwd", grid=grid,
        in_specs=[_rows(tt, D_MODEL, ns, _COL_GLU_A), _rows(tt, D_MODEL, ns, _COL_GLU_B),
                  _halo_prev(tt, ns, _COL_GLU_A), _halo_prev(tt, ns, _COL_GLU_B),
                  _vec_spec(32), _vec_spec(), _vec_spec(), _vec_spec()],
        out_specs=[row, row],
        out_shape=[jax.ShapeDtypeStruct((t, D_MODEL), BF16), jax.ShapeDtypeStruct((t, D_MODEL), F32)],
        scratch_shapes=[pltpu.VMEM((HALO + tt, D_MODEL), F32)],
        compiler_params=_params(("parallel", "parallel")),
    )(proj, proj, proj, proj, conv_w, conv_b, ln_g, ln_b)


def _conv_bwd_ln(dcs, cv, ln_g, ln_b, bsz, seq):
    ts = _pick(seq, (256, 128))
    grid, ns = _row_grid(bsz, seq, ts)

    def body(dcs_ref, cv_ref, g_ref, b_ref, dcv_ref, st_ref):
        xhat, rstd = _ln_stats(cv_ref[...])
        cl = xhat * g_ref[...] + b_ref[...]
        dcl = dcs_ref[...] * _dsilu(cl)
        dxhat = dcl * g_ref[...]
        m1 = jnp.mean(dxhat, axis=-1, keepdims=True)
        m2 = jnp.mean(dxhat * xhat, axis=-1, keepdims=True)
        dcv = rstd * (dxhat - m1 - xhat * m2)
        dcv_ref[...] = dcv

        @pl.when((pl.program_id(0) == 0) & (pl.program_id(1) == 0))
        def _():
            st_ref[...] = jnp.zeros_like(st_ref)

        st_ref[0:1, :] += _colsum(dcl * xhat)
        st_ref[1:2, :] += _colsum(dcl)
        st_ref[2:3, :] += _colsum(dcv)

    row = _rows(ts, D_MODEL, ns)
    return pl.pallas_call(
        body, name="conv_bwd_ln", grid=grid, in_specs=[row, row, _vec_spec(), _vec_spec()],
        out_specs=[row, _vec_spec(8)],
        out_shape=[jax.ShapeDtypeStruct(cv.shape, F32), jax.ShapeDtypeStruct((8, D_MODEL), F32)],
        compiler_params=_params(("arbitrary", "arbitrary")),
    )(dcs, cv, ln_g, ln_b)


def _conv_bwd_taps(proj, dcv, conv_w, bsz, seq):
    tt = _CONV_ROWS
    grid, ns = _row_grid(bsz, seq, tt)
    off = HALO - (CONV_TAPS - 1)
    nblk = proj.shape[0] // HALO

    def body(a_ref, b_ref, ha_ref, hb_ref, d_ref, dn_ref, w_ref, dglu_ref, dw_ref, hpad, dpad):
        s = pl.program_id(1)
        _fill_hc(hpad, a_ref, b_ref, ha_ref, hb_ref, s)
        dcv = d_ref[...]
        dpad[0:tt, :] = dcv
        dpad[tt:, :] = jnp.where(s < ns - 1, dn_ref[...], 0.0)

        @pl.when((pl.program_id(0) == 0) & (s == 0))
        def _():
            dw_ref[...] = jnp.zeros_like(dw_ref)

        dhc = jnp.zeros((tt, D_MODEL), F32)
        for j in range(CONV_TAPS):
            dhc = dhc + w_ref[j:j + 1, :] * dpad[CONV_TAPS - 1 - j:CONV_TAPS - 1 - j + tt, :]
            dw_ref[j:j + 1, :] += _colsum(dcv * hpad[off + j:off + j + tt, :])
        sb = _sigmoid(b_ref[...])
        dglu_ref[:, :D_MODEL] = (dhc * sb).astype(BF16)
        dglu_ref[:, D_MODEL:] = (dhc * a_ref[...] * sb * (1.0 - sb)).astype(BF16)

    t = proj.shape[0]
    return pl.pallas_call(
        body, name="conv_bwd_taps", grid=grid,
        in_specs=[_rows(tt, D_MODEL, ns, _COL_GLU_A), _rows(tt, D_MODEL, ns, _COL_GLU_B),
                  _halo_prev(tt, ns, _COL_GLU_A), _halo_prev(tt, ns, _COL_GLU_B),
                  _rows(tt, D_MODEL, ns), _halo_next(tt, ns, nblk, 0), _vec_spec(32)],
        out_specs=[_rows(tt, 2 * D_MODEL, ns), _vec_spec(32)],
        out_shape=[jax.ShapeDtypeStruct((t, 2 * D_MODEL), BF16), jax.ShapeDtypeStruct((32, D_MODEL), F32)],
        scratch_shapes=[pltpu.VMEM((HALO + tt, D_MODEL), F32), pltpu.VMEM((tt + HALO, D_MODEL), F32)],
        compiler_params=_params(("arbitrary", "arbitrary")),
    )(proj, proj, proj, proj, dcv, dcv, conv_w)


_NT = (((1,), (1,)), ((), ()))
_TN = (((0,), (0,)), ((), ()))


def _dot(a, b, dims=None):
    if dims is None:
        return jnp.dot(a, b, preferred_element_type=F32)
    return lax.dot_general(a, b, dims, preferred_element_type=F32)


def _tri_dot(v, tri):
    hi = v.astype(BF16)
    lo = (v - hi.astype(F32)).astype(BF16)
    return _dot(hi, tri) + _dot(lo, tri)


def _softplus_parts(z):
    t = jnp.exp(-jnp.abs(z))
    den = 1.0 + t
    return jnp.maximum(z, 0.0) + jnp.log(den), t, den


def _attn_fwd(proj, bsz, seq):
    blk = ATT_BLOCK
    nq = seq // blk
    n_pairs = D_MODEL // LANES

    def body(q_ref, k_ref, v_ref, y_ref, rt_ref):
        qi = pl.program_id(2)
        lane = lax.broadcasted_iota(jnp.int32, (blk, LANES), 1)
        first = lane < HEAD_DIM
        q2 = q_ref[...] * 0.125
        q_heads = (jnp.where(first, q2, 0.0).astype(BF16), jnp.where(first, 0.0, q2).astype(BF16))
        rr = lax.broadcasted_iota(jnp.int32, (blk, blk), 0)
        cc = lax.broadcasted_iota(jnp.int32, (blk, blk), 1)
        tri_ge = (rr >= cc).astype(BF16)
        causal = cc < rr

        def tile(kb, carry, masked):
            k_blk = k_ref[pl.ds(pl.multiple_of(kb * blk, blk), blk), :].astype(BF16)
            v_blk = v_ref[pl.ds(pl.multiple_of(kb * blk, blk), blk), :].astype(BF16)
            new = []
            for h in range(2):
                run, acc = carry[2 * h], carry[2 * h + 1]
                z = _dot(q_heads[h], k_blk, _NT)
                sp, _, _ = _softplus_parts(z)
                neg = -sp
                if masked:
                    neg = jnp.where(causal, neg, 0.0)
                r_in = _tri_dot(neg, tri_ge) + run
                w = jnp.exp(z + r_in)
                if masked:
                    w = jnp.where(causal, w, 0.0)
                acc = acc + _dot(w.astype(BF16), v_blk)
                run = run + jnp.sum(neg, axis=1, keepdims=True)
                new += [run, acc]
            return tuple(new)

        zero_run = jnp.zeros((blk, 1), F32)
        zero_acc = jnp.zeros((blk, LANES), F32)
        carry = tile(qi, (zero_run, zero_acc, zero_run, zero_acc), True)
        carry = lax.fori_loop(0, qi, lambda it, c: tile(qi - 1 - it, c, False), carry)
        y_ref[...] = jnp.where(first, carry[1], carry[3]).astype(BF16)
        rt_ref[...] = jnp.where(first, jnp.broadcast_to(carry[0], (blk, LANES)),
                                jnp.broadcast_to(carry[2], (blk, LANES)))

    t = proj.shape[0]
    q_spec = pl.BlockSpec((blk, LANES), lambda b, p, i: (b * nq + i, p))
    return pl.pallas_call(
        body, name="attn_fwd", grid=(bsz, n_pairs, nq),
        in_specs=[q_spec,
                  pl.BlockSpec((seq, LANES), lambda b, p, i: (b, n_pairs + p)),
                  pl.BlockSpec((seq, LANES), lambda b, p, i: (b, 2 * n_pairs + p))],
        out_specs=[q_spec, q_spec],
        out_shape=[jax.ShapeDtypeStruct((t, D_MODEL), BF16), jax.ShapeDtypeStruct((t, D_MODEL), F32)],
        compiler_params=_params(("parallel", "parallel", "arbitrary")),
    )(proj, proj, proj)


def _attn_bwd(proj, rtot, dy, bsz, seq):
    blk = ATT_BLOCK
    nq = seq // blk
    n_pairs = D_MODEL // LANES

    def body(q_ref, k_ref, v_ref, dy_ref, rt_ref, dq_ref, dk_ref, dv_ref, dk_acc, dv_acc):
        qi = pl.program_id(2)

        @pl.when(qi == 0)
        def _():
            dk_acc[...] = jnp.zeros_like(dk_acc)
            dv_acc[...] = jnp.zeros_like(dv_acc)

        lane = lax.broadcasted_iota(jnp.int32, (blk, LANES), 1)
        first = lane < HEAD_DIM
        q2 = q_ref[...] * 0.125
        q2b = q2.astype(BF16)
        q_heads = (jnp.where(first, q2, 0.0).astype(BF16), jnp.where(first, 0.0, q2).astype(BF16))
        dy2 = dy_ref[...]
        dy_heads = (jnp.where(first, dy2, jnp.zeros_like(dy2)), jnp.where(first, jnp.zeros_like(dy2), dy2))
        rt = (rt_ref[:, 0:1], rt_ref[:, HEAD_DIM:HEAD_DIM + 1])
        rr = lax.broadcasted_iota(jnp.int32, (blk, blk), 0)
        cc = lax.broadcasted_iota(jnp.int32, (blk, blk), 1)
        tri_lt = (rr < cc).astype(BF16)
        tri_le = (rr <= cc).astype(BF16)
        causal = cc < rr

        def tile(kb, carry, masked):
            rows = pl.ds(pl.multiple_of(kb * blk, blk), blk)
            k_blk = k_ref[rows, :].astype(BF16)
            v_blk = v_ref[rows, :].astype(BF16)
            new = []
            dk_parts, dv_parts = [], []
            for h in range(2):
                pre, esum, dq = carry[3 * h], carry[3 * h + 1], carry[3 * h + 2]
                z = _dot(q_heads[h], k_blk, _NT)
                sp, t, den = _softplus_parts(z)
                sig = jnp.where(z >= 0, 1.0, t) / den
                neg = -sp
                if masked:
                    neg = jnp.where(causal, neg, 0.0)
                prefix = pre + _tri_dot(neg, tri_lt)
                w = jnp.exp(z + (rt[h] - prefix))
                if masked:
                    w = jnp.where(causal, w, 0.0)
                dw = _dot(dy_heads[h], v_blk, _NT)
                e = dw * w
                big_e = esum + _tri_dot(e, tri_le)
                dz = e - sig * big_e
                if masked:
                    dz = jnp.where(causal, dz, 0.0)
                dzb = dz.astype(BF16)
                wb = w.astype(BF16)
                new += [pre + jnp.sum(neg, axis=1, keepdims=True), esum + jnp.sum(e, axis=1, keepdims=True),
                        dq + _dot(dzb, k_blk)]
                dk_parts.append(_dot(dzb, q2b, _TN))
                dv_parts.append(_dot(wb, dy2, _TN))
            dk_acc[rows, :] += jnp.where(first, dk_parts[0], dk_parts[1])
            dv_acc[rows, :] += jnp.where(first, dv_parts[0], dv_parts[1])
            return tuple(new)

        zero_run = jnp.zeros((blk, 1), F32)
        zero_acc = jnp.zeros((blk, LANES), F32)
        carry = (zero_run, zero_run, zero_acc, zero_run, zero_run, zero_acc)
        carry = lax.fori_loop(0, qi, lambda kb, c: tile(kb, c, False), carry)
        carry = tile(qi, carry, True)
        dq_ref[...] = (jnp.where(first, carry[2], carry[5]) * 0.125).astype(BF16)

        @pl.when(qi == nq - 1)
        def _():
            dk_ref[...] = dk_acc[...].astype(BF16)
            dv_ref[...] = dv_acc[...].astype(BF16)

    t = proj.shape[0]
    q_spec = pl.BlockSpec((blk, LANES), lambda b, p, i: (b * nq + i, p))
    kv_out = pl.BlockSpec((seq, LANES), lambda b, p, i: (b, p))
    out = jax.ShapeDtypeStruct((t, D_MODEL), BF16)
    return pl.pallas_call(
        body, name="attn_bwd", grid=(bsz, n_pairs, nq),
        in_specs=[q_spec,
                  pl.BlockSpec((seq, LANES), lambda b, p, i: (b, n_pairs + p)),
                  pl.BlockSpec((seq, LANES), lambda b, p, i: (b, 2 * n_pairs + p)),
                  q_spec, q_spec],
        out_specs=[q_spec, kv_out, kv_out], out_shape=[out, out, out],
        scratch_shapes=[pltpu.VMEM((seq, LANES), F32), pltpu.VMEM((seq, LANES), F32)],
        compiler_params=_params(("parallel", "parallel", "arbitrary")),
    )(proj, proj, proj, dy, rtot)


def _adamw(w, g, m, v, name):
    rows, cols = w.shape
    tr = _pick(rows, (256, 352, 128, 64, 32, 16, 8))
    c1 = 1.0 - ADAM_B1 ** ADAM_STEP
    c2 = 1.0 - ADAM_B2 ** ADAM_STEP

    def body(w_ref, g_ref, m_ref, v_ref, d_ref, nm_ref, nv_ref):
        g_v = g_ref[...]
        nm = ADAM_B1 * m_ref[...] + (1.0 - ADAM_B1) * g_v
        nv = ADAM_B2 * v_ref[...] + (1.0 - ADAM_B2) * (g_v * g_v)
        nm_ref[...] = nm
        nv_ref[...] = nv
        d_ref[...] = -ADAM_LR * ((nm / c1) / (jnp.sqrt(nv / c2) + ADAM_EPS) + ADAM_WD * w_ref[...])

    spec = pl.BlockSpec((tr, cols), lambda i: (i, 0))
    shape = jax.ShapeDtypeStruct(w.shape, F32)
    return pl.pallas_call(
        body, name=name, grid=(rows // tr,), in_specs=[spec] * 4, out_specs=[spec] * 3, out_shape=[shape] * 3,
        compiler_params=_params(("parallel",)),
    )(w, g, m, v)


def _ffn_fwd(u, w_gu, w_down, bsz, seq, tag):
    h = _matmul(u, w_gu, mode="nn", out_dtype=F32, name=f"{tag}_up")
    p = _swiglu_act(h, bsz, seq, f"{tag}_act")
    f = _matmul(p, w_down, mode="nn", out_dtype=F32, name=f"{tag}_down")
    return h, p, f


def _ffn_bwd(df, u, h, p, w_gu, w_down, bsz, seq, tag):
    dp = _matmul(df, w_down, mode="nt", out_dtype=F32, name=f"{tag}_ddown")
    dh = _swiglu_bwd(h, dp, bsz, seq, f"{tag}_dact")
    g_down = _matmul(p, df, mode="tn", out_dtype=F32, name=f"{tag}_gdown")
    g_gu = _matmul(u, dh, mode="tn", out_dtype=F32, name=f"{tag}_ggu", out_stacked=True)
    du = _matmul(dh, w_gu, mode="nt", out_dtype=F32, name=f"{tag}_dup")
    return du, g_gu, g_down


def _local_step(x, c, target, wts, vecs):
    bsz, seq, _ = x.shape
    t = bsz * seq
    x0 = x.reshape(t, D_MODEL)
    tgt = target.reshape(t, D_MODEL)

    sc = _silu_pad(c)
    mod16 = _matmul(sc, wts["w_ada"], mode="nn", out_dtype=F32, name="ada_fwd", bias=vecs["b_ada"])
    mod = mod16[:bsz].reshape(bsz, 9, D_MODEL)

    u1 = _mod_in(x0, mod, bsz, seq, 0)
    h1, p1, f1 = _ffn_fwd(u1, wts["ffn1_w_gu"], wts["ffn1_w_down"], bsz, seq, "ffn1")
    r1, x1, u2 = _res_ln_fwd(x0, f1, mod, vecs["ln1_g"], vecs["ln1_b"], bsz, seq, 0, 0.5)

    proj = _matmul(u2, wts["w_in"], mode="nn", out_dtype=F32, name="mix_in")
    ya, rtot = _attn_fwd(proj, bsz, seq)
    cs, cv = _conv_fwd(proj, wts["conv_w"], vecs["conv_b"], vecs["conv_ln_g"], vecs["conv_ln_b"], bsz, seq)
    ysb = _matmul(ya, wts["w_sb_out"], mode="nn", out_dtype=F32, name="sb_out")
    yconv = _matmul(cs, wts["w_conv_out"], mode="nn", out_dtype=F32, name="conv_out")
    merged = _merge_fwd(proj, ysb, yconv, bsz, seq)
    o2 = _matmul(merged, wts["w_out"], mode="nn", out_dtype=F32, name="mix_out")
    r2, x2, u3 = _res_ln_fwd(x1, o2, mod, vecs["ln2_g"], vecs["ln2_b"], bsz, seq, 1, 1.0)

    h3, p3, f3 = _ffn_fwd(u3, wts["ffn2_w_gu"], wts["ffn2_w_down"], bsz, seq, "ffn2")
    r3, dy, loss_blk = _res_ln_fwd(x2, f3, mod, vecs["ln3_g"], vecs["ln3_b"], bsz, seq, 2, 0.5, target=tgt)

    grads = {}
    dxres, df, ln3s, g3s = _res_ln_bwd(r3, dy, f3, mod, vecs["ln3_g"], bsz, seq, 2, 0.5)
    du, grads["ffn2_w_gu"], grads["ffn2_w_down"] = _ffn_bwd(
        df, u3, h3, p3, wts["ffn2_w_gu"], wts["ffn2_w_down"], bsz, seq, "ffn2")
    dx2, m3s = _mod_bwd(dxres, du, x2, mod, bsz, seq, 2)

    dxres, do2, ln2s, g2s = _res_ln_bwd(r2, dx2, o2, mod, vecs["ln2_g"], bsz, seq, 1, 1.0)
    dmerged = _matmul(do2, wts["w_out"], mode="nt", out_dtype=F32, name="mix_out_d")
    grads["w_out"] = _matmul(merged, do2, mode="tn", out_dtype=F32, name="mix_out_g")
    dysb, dyconv, dgate = _merge_bwd(proj, ysb, yconv, dmerged, bsz, seq)
    dya = _matmul(dysb, wts["w_sb_out"], mode="nt", out_dtype=BF16, name="sb_out_d")
    grads["w_sb_out"] = _matmul(ya, dysb, mode="tn", out_dtype=F32, name="sb_out_g")
    dcs = _matmul(dyconv, wts["w_conv_out"], mode="nt", out_dtype=F32, name="conv_out_d")
    grads["w_conv_out"] = _matmul(cs, dyconv, mode="tn", out_dtype=F32, name="conv_out_g")
    dcv, convs = _conv_bwd_ln(dcs, cv, vecs["conv_ln_g"], vecs["conv_ln_b"], bsz, seq)
    dglu, g_conv_w = _conv_bwd_taps(proj, dcv, wts["conv_w"], bsz, seq)
    dq, dk, dv = _attn_bwd(proj, rtot, dya, bsz, seq)
    dproj = jnp.concatenate([dq, dk, dv, dglu, dgate], axis=1)
    grads["w_in"] = _matmul(u2, dproj, mode="tn", out_dtype=F32, name="mix_in_g", out_stacked=True)
    du = _matmul(dproj, wts["w_in"], mode="nt", out_dtype=F32, name="mix_in_d")
    dx1, m2s = _mod_bwd(dxres, du, x1, mod, bsz, seq, 1)

    dxres, df, ln1s, g1s = _res_ln_bwd(r1, dx1, f1, mod, vecs["ln1_g"], bsz, seq, 0, 0.5)
    du, grads["ffn1_w_gu"], grads["ffn1_w_down"] = _ffn_bwd(
        df, u1, h1, p1, wts["ffn1_w_gu"], wts["ffn1_w_down"], bsz, seq, "ffn1")
    grad_x, m1s = _mod_bwd(dxres, du, x0, mod, bsz, seq, 0)

    dmod = jnp.stack([m1s[:, 0], m1s[:, 1], g1s[:, 0], m2s[:, 0], m2s[:, 1], g2s[:, 0],
                      m3s[:, 0], m3s[:, 1], g3s[:, 0]], axis=1)
    dmod16 = jnp.zeros((16, 9 * D_MODEL), F32).at[:bsz].set(dmod.reshape(bsz, 9 * D_MODEL))
    grads["w_ada"] = _matmul(sc, dmod16.astype(BF16), mode="tn", out_dtype=F32, name="ada_g", out_stacked=True)

    small = {"dmod": dmod, "ln1": ln1s, "ln2": ln2s, "ln3": ln3s, "conv": convs, "conv_w": g_conv_w,
             "loss": loss_blk}
    return grad_x.reshape(x.shape), grads, small


_HBM = pl.BlockSpec(memory_space=pltpu.HBM)


def _position():
    return lax.axis_index("x"), lax.axis_index("y"), lax.axis_index("c")


def _other_chips(x, y):
    return [(1 - x, y), (x, 1 - y), (1 - x, 1 - y)]


def _all_gather_weights(blocks):
    n = len(blocks)

    def body(*refs):
        ins, outs = refs[:n], refs[n:2 * n]
        send_sems, recv_sems, local_sems = refs[2 * n:]
        x, y, c = _position()
        me = 2 * x + y
        chips = _other_chips(x, y)
        copies = []
        for i in range(n):
            loc = pltpu.make_async_copy(ins[i], outs[i].at[me], local_sems.at[i])
            loc.start()
            copies.append(loc)
            for j, (px, py) in enumerate(chips):
                cp = pltpu.make_async_remote_copy(
                    src_ref=ins[i], dst_ref=outs[i].at[me], send_sem=send_sems.at[3 * i + j],
                    recv_sem=recv_sems.at[3 * i + j], device_id=(px, py, c), device_id_type=MESH)
                cp.start()
                copies.append(cp)
        for i in range(n):
            for j, (px, py) in enumerate(chips):
                pltpu.make_async_remote_copy(
                    src_ref=ins[i], dst_ref=outs[i].at[2 * px + py], send_sem=send_sems.at[3 * i + j],
                    recv_sem=recv_sems.at[3 * i + j], device_id=(px, py, c), device_id_type=MESH).wait_recv()
        for i in range(n):
            pltpu.make_async_copy(ins[i], outs[i].at[me], local_sems.at[i]).wait()
            for j, (px, py) in enumerate(chips):
                pltpu.make_async_remote_copy(
                    src_ref=ins[i], dst_ref=outs[i].at[me], send_sem=send_sems.at[3 * i + j],
                    recv_sem=recv_sems.at[3 * i + j], device_id=(px, py, c), device_id_type=MESH).wait_send()

    return pl.pallas_call(
        body, name="all_gather_weights",
        out_shape=[jax.ShapeDtypeStruct((N_CHIPS,) + b.shape, b.dtype) for b in blocks],
        in_specs=[_HBM] * n, out_specs=[_HBM] * n,
        scratch_shapes=[pltpu.SemaphoreType.DMA((3 * n,)), pltpu.SemaphoreType.DMA((3 * n,)),
                        pltpu.SemaphoreType.DMA((n,))],
    )(*blocks)


def _pair_swap(grads):
    n = len(grads)

    def body(*refs):
        ins, outs = refs[:n], refs[n:2 * n]
        send_sems, recv_sems = refs[2 * n:]
        x, y, c = _position()

        def copy(i, k):
            return pltpu.make_async_remote_copy(
                src_ref=ins[i].at[k, 1 - c], dst_ref=outs[i].at[k], send_sem=send_sems.at[N_CHIPS * i + k],
                recv_sem=recv_sems.at[N_CHIPS * i + k], device_id=(x, y, 1 - c), device_id_type=MESH)

        for i in range(n):
            for k in range(N_CHIPS):
                copy(i, k).start()
        for i in range(n):
            for k in range(N_CHIPS):
                copy(i, k).wait_recv()
        for i in range(n):
            for k in range(N_CHIPS):
                copy(i, k).wait_send()

    return pl.pallas_call(
        body, name="grad_pair_swap",
        out_shape=[jax.ShapeDtypeStruct((N_CHIPS,) + g.shape[2:], F32) for g in grads],
        in_specs=[_HBM] * n, out_specs=[_HBM] * n,
        scratch_shapes=[pltpu.SemaphoreType.DMA((N_CHIPS * n,)), pltpu.SemaphoreType.DMA((N_CHIPS * n,))],
    )(*grads)


def _pair_add(g, got, core, name):
    _, _, rh, cols = g.shape
    tr = _pick(rh, (256, 176, 128, 64, 32, 16, 8))

    def body(core_ref, g_ref, got_ref, o_ref):
        o_ref[...] = g_ref[...] + got_ref[...]

    blk = pl.BlockSpec((None, tr, cols), lambda k, r, core_ref: (k, r, 0))
    return pl.pallas_call(
        body, name=name,
        grid_spec=pltpu.PrefetchScalarGridSpec(
            num_scalar_prefetch=1, grid=(N_CHIPS, rh // tr),
            in_specs=[pl.BlockSpec((None, None, tr, cols), lambda k, r, core_ref: (k, core_ref[0], r, 0)), blk],
            out_specs=blk),
        out_shape=jax.ShapeDtypeStruct((N_CHIPS, rh, cols), F32),
        compiler_params=_params(("parallel", "parallel")),
    )(core, g, got)


def _chip_scatter(sums):
    n = len(sums)

    def body(*refs):
        ins, outs = refs[:n], refs[n:2 * n]
        send_sems, recv_sems, local_sems = refs[2 * n:]
        x, y, c = _position()
        me = 2 * x + y
        chips = _other_chips(x, y)

        def local(i):
            return pltpu.make_async_copy(ins[i].at[me], outs[i].at[me], local_sems.at[i])

        def send(i, j):
            px, py = chips[j]
            return pltpu.make_async_remote_copy(
                src_ref=ins[i].at[2 * px + py], dst_ref=outs[i].at[me], send_sem=send_sems.at[3 * i + j],
                recv_sem=recv_sems.at[3 * i + j], device_id=(px, py, c), device_id_type=MESH)

        def recv(i, j):
            px, py = chips[j]
            return pltpu.make_async_remote_copy(
                src_ref=ins[i].at[me], dst_ref=outs[i].at[2 * px + py], send_sem=send_sems.at[3 * i + j],
                recv_sem=recv_sems.at[3 * i + j], device_id=(px, py, c), device_id_type=MESH)

        for i in range(n):
            local(i).start()
            for j in range(3):
                send(i, j).start()
        for i in range(n):
            for j in range(3):
                recv(i, j).wait_recv()
        for i in range(n):
            local(i).wait()
            for j in range(3):
                send(i, j).wait_send()

    return pl.pallas_call(
        body, name="grad_chip_scatter",
        out_shape=[jax.ShapeDtypeStruct(s.shape, F32) for s in sums],
        in_specs=[_HBM] * n, out_specs=[_HBM] * n,
        scratch_shapes=[pltpu.SemaphoreType.DMA((3 * n,)), pltpu.SemaphoreType.DMA((3 * n,)),
                        pltpu.SemaphoreType.DMA((n,))],
    )(*sums)


def _chip_sum(parts, name):
    _, rh, cols = parts.shape
    tr = _pick(rh, (256, 176, 128, 64, 32, 16, 8))

    def body(p_ref, o_ref):
        o_ref[...] = ((p_ref[0] + p_ref[1]) + p_ref[2]) + p_ref[3]

    return pl.pallas_call(
        body, name=name, grid=(rh // tr,),
        in_specs=[pl.BlockSpec((N_CHIPS, tr, cols), lambda r: (0, r, 0))],
        out_specs=pl.BlockSpec((tr, cols), lambda r: (r, 0)),
        out_shape=jax.ShapeDtypeStruct((rh, cols), F32),
        compiler_params=_params(("parallel",)),
    )(parts)


def _pair_gather(halves):
    n = len(halves)

    def body(*refs):
        ins, outs = refs[:n], refs[n:2 * n]
        send_sems, recv_sems, local_sems = refs[2 * n:]
        x, y, c = _position()

        def local(i):
            return pltpu.make_async_copy(ins[i], outs[i].at[c], local_sems.at[i])

        def send(i):
            return pltpu.make_async_remote_copy(
                src_ref=ins[i], dst_ref=outs[i].at[c], send_sem=send_sems.at[i], recv_sem=recv_sems.at[i],
                device_id=(x, y, 1 - c), device_id_type=MESH)

        def recv(i):
            return pltpu.make_async_remote_copy(
                src_ref=ins[i], dst_ref=outs[i].at[1 - c], send_sem=send_sems.at[i], recv_sem=recv_sems.at[i],
                device_id=(x, y, 1 - c), device_id_type=MESH)

        for i in range(n):
            local(i).start()
            send(i).start()
        for i in range(n):
            recv(i).wait_recv()
        for i in range(n):
            local(i).wait()
            send(i).wait_send()

    return pl.pallas_call(
        body, name="grad_pair_gather",
        out_shape=[jax.ShapeDtypeStruct((2,) + h.shape, F32) for h in halves],
        in_specs=[_HBM] * n, out_specs=[_HBM] * n,
        scratch_shapes=[pltpu.SemaphoreType.DMA((n,)), pltpu.SemaphoreType.DMA((n,)),
                        pltpu.SemaphoreType.DMA((n,))],
    )(*halves)


_MOD_ROWS = 16


def _small_all_reduce(buf, bsz):
    rows, cols = buf.shape
    head = bsz * _MOD_ROWS
    out_rows = rows - head + _MOD_ROWS

    def body(in_ref, o_ref, gath, send_sems, recv_sems):
        x, y, c = _position()
        me = 4 * x + 2 * y + c

        def peer(mask):
            return (x ^ (mask >> 2), y ^ ((mask >> 1) & 1), c ^ (mask & 1))

        def copy(mask):
            return pltpu.make_async_remote_copy(
                src_ref=in_ref, dst_ref=gath.at[me], send_sem=send_sems.at[mask - 1],
                recv_sem=recv_sems.at[mask - 1], device_id=peer(mask), device_id_type=MESH)

        def arrival(mask):
            px, py, pc = peer(mask)
            return pltpu.make_async_remote_copy(
                src_ref=in_ref, dst_ref=gath.at[4 * px + 2 * py + pc], send_sem=send_sems.at[mask - 1],
                recv_sem=recv_sems.at[mask - 1], device_id=peer(mask), device_id_type=MESH)

        for mask in range(1, N_DEV):
            copy(mask).start()
        gath[me] = in_ref[...]
        for mask in range(1, N_DEV):
            arrival(mask).wait_recv()
        for mask in range(1, N_DEV):
            copy(mask).wait_send()
        acc = gath[0]
        for d in range(1, N_DEV):
            acc = acc + gath[d]
        mod = acc[0:_MOD_ROWS]
        for s in range(1, bsz):
            mod = mod + acc[s * _MOD_ROWS:(s + 1) * _MOD_ROWS]
        o_ref[0:_MOD_ROWS, :] = mod
        o_ref[_MOD_ROWS:, :] = acc[head:]

    vm = pl.BlockSpec(memory_space=pltpu.VMEM)
    return pl.pallas_call(
        body, name="small_all_reduce", in_specs=[vm], out_specs=vm,
        out_shape=jax.ShapeDtypeStruct((out_rows, cols), F32),
        scratch_shapes=[pltpu.VMEM((N_DEV, rows, cols), F32), pltpu.SemaphoreType.DMA((N_DEV - 1,)),
                        pltpu.SemaphoreType.DMA((N_DEV - 1,))],
        compiler_params=pltpu.CompilerParams(vmem_limit_bytes=VMEM_LIMIT),
    )(buf)


_COL_SHARDED = ("w_ada", "ffn1_w_gu", "w_in", "ffn2_w_gu")
_ROW_SHARDED = ("ffn1_w_down", "w_sb_out", "w_conv_out", "w_out", "ffn2_w_down")
_BIG = ("w_ada", "ffn1_w_gu", "ffn1_w_down", "w_in", "w_sb_out", "w_conv_out", "w_out", "ffn2_w_gu", "ffn2_w_down")
_VECS = ("b_ada", "ln1_g", "ln1_b", "conv_b", "conv_ln_g", "conv_ln_b", "ln2_g", "ln2_b", "ln3_g", "ln3_b")
_WEIGHTS = ("w_ada", "b_ada", "ffn1_w_gu", "ffn1_w_down", "ln1_g", "ln1_b", "w_in", "w_sb_out", "conv_w", "conv_b",
            "conv_ln_g", "conv_ln_b", "w_conv_out", "w_out", "ln2_g", "ln2_b", "ffn2_w_gu", "ffn2_w_down",
            "ln3_g", "ln3_b")


def _step(x, c, target, w, m, v):
    bsz = x.shape[0]
    chip = 2 * lax.axis_index("x") + lax.axis_index("y")
    core = lax.axis_index("c")

    conv_w_local = jnp.pad(w["conv_w"][0], ((0, 1), (0, 0)))
    gathered = _all_gather_weights([w[n][0].astype(BF16) for n in _BIG] + [conv_w_local])
    wts = {}
    for n, g in zip(_BIG, gathered[:-1]):
        wts[n] = g if n in _COL_SHARDED else g.reshape(g.shape[0] * g.shape[1], g.shape[2])
    wts["conv_w"] = gathered[-1].transpose(1, 0, 2).reshape(32, D_MODEL)
    vecs = {n: w[n] for n in _VECS}

    grad_x, grads, small = _local_step(x, c, target, wts, vecs)

    views = []
    for n in _BIG:
        g = grads[n]
        rows, cols = w[n].shape[1:]
        views.append(g.reshape(N_CHIPS, 2, rows // 2, cols))
    got = _pair_swap(views)
    core_arr = jnp.reshape(core, (1,)).astype(jnp.int32)
    pair_sums = [_pair_add(g, r, core_arr, f"pair_add_{n}") for n, g, r in zip(_BIG, views, got)]
    parts = _chip_scatter(pair_sums)
    halves = [_chip_sum(p, f"chip_sum_{n}") for n, p in zip(_BIG, parts)]
    full = _pair_gather(halves)
    g_out = {n: f.reshape(w[n].shape[1:]) for n, f in zip(_BIG, full)}

    dmod = jnp.pad(small["dmod"], ((0, 0), (0, _MOD_ROWS - 9), (0, 0))).reshape(bsz * _MOD_ROWS, D_MODEL)
    loss_rows = jnp.pad(small["loss"], ((0, 0), (0, D_MODEL - LANES)))
    buf = jnp.concatenate([dmod, small["ln1"], small["ln2"], small["ln3"], small["conv"], small["conv_w"],
                           loss_rows], axis=0)
    red = _small_all_reduce(buf, bsz)
    o = _MOD_ROWS
    g_out["b_ada"] = red[0:9].reshape(1, 9 * D_MODEL)
    g_out["ln1_g"], g_out["ln1_b"] = red[o:o + 1], red[o + 1:o + 2]
    g_out["ln2_g"], g_out["ln2_b"] = red[o + 8:o + 9], red[o + 9:o + 10]
    g_out["ln3_g"], g_out["ln3_b"] = red[o + 16:o + 17], red[o + 17:o + 18]
    g_out["conv_ln_g"], g_out["conv_ln_b"], g_out["conv_b"] = red[o + 24:o + 25], red[o + 25:o + 26], red[o + 26:o + 27]
    cw = w["conv_w"].shape[2]
    g_out["conv_w"] = lax.dynamic_slice(red[o + 32:o + 32 + CONV_TAPS], (0, chip * cw), (CONV_TAPS, cw))
    loss = red[o + 64, 0]

    outs_g, outs_d, outs_m, outs_v = [], [], [], []
    for n in _WEIGHTS:
        shape = w[n].shape
        flat = shape[1:] if len(shape) == 3 else shape
        d, nm, nv = _adamw(w[n].reshape(flat), g_out[n].reshape(flat), m[n].reshape(flat), v[n].reshape(flat),
                           f"adamw_{n}")
        outs_g.append(g_out[n].reshape(shape))
        outs_d.append(d.reshape(shape))
        outs_m.append(nm.reshape(shape))
        outs_v.append(nv.reshape(shape))
    return (loss, grad_x, *outs_g, *outs_d, *outs_m, *outs_v)


def kernel(x, c, w_ada, b_ada, ffn1_w_gu, ffn1_w_down, ln1_g, ln1_b, w_in, w_sb_out, conv_w, conv_b, conv_ln_g, conv_ln_b, w_conv_out, w_out, ln2_g, ln2_b, ffn2_w_gu, ffn2_w_down, ln3_g, ln3_b, loss_target, m_w_ada, m_b_ada, m_ffn1_w_gu, m_ffn1_w_down, m_ln1_g, m_ln1_b, m_w_in, m_w_sb_out, m_conv_w, m_conv_b, m_conv_ln_g, m_conv_ln_b, m_w_conv_out, m_w_out, m_ln2_g, m_ln2_b, m_ffn2_w_gu, m_ffn2_w_down, m_ln3_g, m_ln3_b, v_w_ada, v_b_ada, v_ffn1_w_gu, v_ffn1_w_down, v_ln1_g, v_ln1_b, v_w_in, v_w_sb_out, v_conv_w, v_conv_b, v_conv_ln_g, v_conv_ln_b, v_w_conv_out, v_w_out, v_ln2_g, v_ln2_b, v_ffn2_w_gu, v_ffn2_w_down, v_ln3_g, v_ln3_b):
    given = dict(locals())
    w = {n: given[n] for n in _WEIGHTS}
    m = {n: given["m_" + n] for n in _WEIGHTS}
    v = {n: given["v_" + n] for n in _WEIGHTS}
    return _step(x, c, loss_target, w, m, v)
```

```python
import functools

import jax
import jax.numpy as jnp
from jax import lax
from jax.experimental import pallas as pl
from jax.experimental.pallas import tpu as pltpu

F32 = jnp.float32
BF16 = jnp.bfloat16

D_MODEL = 1024
D_FF = 2816
HEADS = 16
HEAD_DIM = 64
LANES = 128
CONV_TAPS = 31
HALO = 32
N_CHIPS = 4
N_DEV = 8
ALPHA = 2.0 ** 0.25
LN_EPS = 1e-5
ATT_BLOCK = 256
VMEM_LIMIT = 56 * 1024 * 1024

ADAM_LR = 0.001
ADAM_B1 = 0.9
ADAM_B2 = 0.999
ADAM_EPS = 1e-08
ADAM_WD = 0.01
ADAM_STEP = 10

MESH = pl.DeviceIdType.MESH


def _pick(n, cands):
    for t in cands:
        if t <= n and n % t == 0:
            return t
    return n


def _params(sem):
    return pltpu.CompilerParams(dimension_semantics=sem, vmem_limit_bytes=VMEM_LIMIT)


def _sigmoid(z):
    t = jnp.exp(-jnp.abs(z))
    return jnp.where(z >= 0, 1.0, t) / (1.0 + t)


def _silu(z):
    return z * _sigmoid(z)


def _dsilu(z):
    s = _sigmoid(z)
    return s * (1.0 + z * (1.0 - s))


def _ln_stats(r):
    mu = jnp.mean(r, axis=-1, keepdims=True)
    d = r - mu
    var = jnp.mean(d * d, axis=-1, keepdims=True)
    rstd = lax.rsqrt(var + LN_EPS)
    return d * rstd, rstd


def _colsum(v):
    return jnp.sum(v, axis=0, keepdims=True)


_DIMS = {"nn": (((1,), (0,)), ((), ())), "nt": (((1,), (1,)), ((), ())), "tn": (((0,), (0,)), ((), ()))}
_TN_CANDS = (1408, 1792, 1152, 1024, 512, 256, 128)
_TK_CANDS = (1024, 1408, 896, 512, 256, 128)


def _matmul(a, b, *, mode, out_dtype, name, bias=None, out_stacked=False):
    b_stacked = b.ndim == 3
    if mode == "nn":
        m, k = a.shape
        n_c = b.shape[-1]
        n = n_c * (N_CHIPS if b_stacked else 1)
        k_c = k
    elif mode == "nt":
        m, k = a.shape
        n = b.shape[-2]
        k_c = b.shape[-1]
        n_c = n
    else:
        k, m = a.shape
        n = b.shape[1]
        n_c = n // N_CHIPS if out_stacked else n
        k_c = k
    if mode == "tn":
        tm = _pick(m, (1024, 1408, 512, 256, 128))
        tk = _pick(k, (512, 256, 128, 64, 32, 16))
    else:
        tm = _pick(m, (1024, 512, 256, 128, 64, 32, 16))
        tk = _pick(k_c, _TK_CANDS)
    tn = _pick(n_c, _TN_CANDS)
    nb = n_c // tn
    kb = k_c // tk
    nk = k // tk
    grid = (m // tm, n // tn, nk)

    if mode == "nn":
        a_spec = pl.BlockSpec((tm, tk), lambda i, j, kk: (i, kk))
        if b_stacked:
            b_spec = pl.BlockSpec((None, tk, tn), lambda i, j, kk: (j // nb, kk, j % nb))
        else:
            b_spec = pl.BlockSpec((tk, tn), lambda i, j, kk: (kk, j))
    elif mode == "nt":
        a_spec = pl.BlockSpec((tm, tk), lambda i, j, kk: (i, kk))
        if b_stacked:
            b_spec = pl.BlockSpec((None, tn, tk), lambda i, j, kk: (kk // kb, j, kk % kb))
        else:
            b_spec = pl.BlockSpec((tn, tk), lambda i, j, kk: (j, kk))
    else:
        a_spec = pl.BlockSpec((tk, tm), lambda i, j, kk: (kk, i))
        b_spec = pl.BlockSpec((tk, tn), lambda i, j, kk: (kk, j))
    if out_stacked:
        out_shape = jax.ShapeDtypeStruct((N_CHIPS, m, n_c), out_dtype)
        o_spec = pl.BlockSpec((None, tm, tn), lambda i, j, kk: (j // nb, i, j % nb))
    else:
        out_shape = jax.ShapeDtypeStruct((m, n), out_dtype)
        o_spec = pl.BlockSpec((tm, tn), lambda i, j, kk: (i, j))
    in_specs = [a_spec, b_spec]
    args = [a, b]
    if bias is not None:
        in_specs.append(pl.BlockSpec((1, tn), lambda i, j, kk: (0, j)))
        args.append(bias)
    dims = _DIMS[mode]

    def body(*refs):
        a_ref, b_ref = refs[0], refs[1]
        bias_ref = refs[2] if bias is not None else None
        o_ref, acc_ref = refs[-2], refs[-1]
        kk = pl.program_id(2)

        @pl.when(kk == 0)
        def _():
            acc_ref[...] = jnp.zeros_like(acc_ref)

        acc_ref[...] += lax.dot_general(a_ref[...], b_ref[...], dims, preferred_element_type=F32)

        @pl.when(kk == nk - 1)
        def _():
            r = acc_ref[...]
            if bias_ref is not None:
                r = r + bias_ref[...]
            o_ref[...] = r.astype(o_ref.dtype)

    return pl.pallas_call(
        body, name=name, grid=grid, in_specs=in_specs, out_specs=o_spec, out_shape=out_shape,
        scratch_shapes=[pltpu.VMEM((tm, tn), F32)],
        compiler_params=_params(("parallel", "parallel", "arbitrary")),
    )(*args)


def _row_grid(bsz, seq, ts):
    ns = seq // ts
    return (bsz, ns), ns


def _rows(ts, width, ns, col=0):
    return pl.BlockSpec((ts, width), lambda b, s: (b * ns + s, col))


def _mod_spec():
    return pl.BlockSpec((None, 9, D_MODEL), lambda b, s: (b, 0, 0))


def _vec_spec(rows=1, width=D_MODEL):
    return pl.BlockSpec((rows, width), lambda b, s: (0, 0))


def _silu_pad(c):
    bsz = c.shape[0]

    def body(c_ref, o_ref):
        o_ref[...] = jnp.zeros_like(o_ref)
        o_ref[0:bsz, :] = _silu(c_ref[...]).astype(BF16)

    return pl.pallas_call(body, name="silu_pad", out_shape=jax.ShapeDtypeStruct((16, D_MODEL), BF16))(c)


def _mod_in(x, mod, bsz, seq, sub):
    ts = _pick(seq, (512, 256, 128))
    grid, ns = _row_grid(bsz, seq, ts)

    def body(x_ref, mod_ref, u_ref):
        sh = mod_ref[3 * sub:3 * sub + 1, :]
        sc = mod_ref[3 * sub + 1:3 * sub + 2, :]
        u_ref[...] = (x_ref[...] * (1.0 + sc) + sh).astype(BF16)

    return pl.pallas_call(
        body, name=f"mod_in{sub}", grid=grid, in_specs=[_rows(ts, D_MODEL, ns), _mod_spec()],
        out_specs=_rows(ts, D_MODEL, ns), out_shape=jax.ShapeDtypeStruct(x.shape, BF16),
        compiler_params=_params(("parallel", "parallel")),
    )(x, mod)


def _swiglu_act(h, bsz, seq, name):
    ts = _pick(seq, (256, 128))
    grid, ns = _row_grid(bsz, seq, ts)

    def body(h_ref, p_ref):
        a = h_ref[:, :D_FF]
        g = h_ref[:, D_FF:]
        p_ref[...] = (_silu(a) * g).astype(BF16)

    return pl.pallas_call(
        body, name=name, grid=grid, in_specs=[_rows(ts, 2 * D_FF, ns)],
        out_specs=_rows(ts, D_FF, ns), out_shape=jax.ShapeDtypeStruct((h.shape[0], D_FF), BF16),
        compiler_params=_params(("parallel", "parallel")),
    )(h)


def _swiglu_bwd(h, dp, bsz, seq, name):
    ts = _pick(seq, (256, 128))
    grid, ns = _row_grid(bsz, seq, ts)

    def body(h_ref, dp_ref, dh_ref):
        a = h_ref[:, :D_FF]
        g = h_ref[:, D_FF:]
        dp_v = dp_ref[...]
        dh_ref[:, :D_FF] = (dp_v * g * _dsilu(a)).astype(BF16)
        dh_ref[:, D_FF:] = (dp_v * _silu(a)).astype(BF16)

    return pl.pallas_call(
        body, name=name, grid=grid, in_specs=[_rows(ts, 2 * D_FF, ns), _rows(ts, D_FF, ns)],
        out_specs=_rows(ts, 2 * D_FF, ns), out_shape=jax.ShapeDtypeStruct(h.shape, BF16),
        compiler_params=_params(("parallel", "parallel")),
    )(h, dp)


def _res_ln_fwd(x, f, mod, ln_g, ln_b, bsz, seq, sub, weight, target=None):
    ts = _pick(seq, (256, 128))
    grid, ns = _row_grid(bsz, seq, ts)
    last = target is not None

    def body(*refs):
        x_ref, f_ref, mod_ref, g_ref, b_ref = refs[:5]
        gate = mod_ref[3 * sub + 2:3 * sub + 3, :]
        r = ALPHA * x_ref[...] + gate * (weight * f_ref[...])
        xhat, _ = _ln_stats(r)
        xo = xhat * g_ref[...] + b_ref[...]
        if last:
            t_ref, r_ref, dy_ref, loss_ref = refs[5:]
            diff = xo - t_ref[...]
            dy_ref[...] = diff * (1.0 / D_MODEL)
            part = 0.5 * jnp.sum(jnp.mean(diff * diff, axis=-1, keepdims=True), axis=0, keepdims=True)

            @pl.when((pl.program_id(0) == 0) & (pl.program_id(1) == 0))
            def _():
                loss_ref[...] = jnp.zeros_like(loss_ref)

            loss_ref[...] += jnp.broadcast_to(part, loss_ref.shape)
        else:
            r_ref, xo_ref, u_ref = refs[5:]
            xo_ref[...] = xo
            sh = mod_ref[3 * sub + 3:3 * sub + 4, :]
            sc = mod_ref[3 * sub + 4:3 * sub + 5, :]
            u_ref[...] = (xo * (1.0 + sc) + sh).astype(BF16)
        r_ref[...] = r

    row = _rows(ts, D_MODEL, ns)
    in_specs = [row, row, _mod_spec(), _vec_spec(), _vec_spec()]
    args = [x, f, mod, ln_g, ln_b]
    if last:
        in_specs.append(row)
        args.append(target)
        out_specs = [row, row, _vec_spec(8, LANES)]
        out_shape = [jax.ShapeDtypeStruct(x.shape, F32), jax.ShapeDtypeStruct(x.shape, F32),
                     jax.ShapeDtypeStruct((8, LANES), F32)]
        sem = ("arbitrary", "arbitrary")
    else:
        out_specs = [row, row, row]
        out_shape = [jax.ShapeDtypeStruct(x.shape, F32), jax.ShapeDtypeStruct(x.shape, F32),
                     jax.ShapeDtypeStruct(x.shape, BF16)]
        sem = ("parallel", "parallel")
    return pl.pallas_call(
        body, name=f"res_ln_fwd{sub}", grid=grid, in_specs=in_specs, out_specs=out_specs, out_shape=out_shape,
        compiler_params=_params(sem),
    )(*args)


def _res_ln_bwd(r, dxo, f, mod, ln_g, bsz, seq, sub, weight):
    ts = _pick(seq, (256, 128))
    grid, ns = _row_grid(bsz, seq, ts)

    def body(r_ref, dxo_ref, f_ref, mod_ref, g_ref, dxres_ref, df_ref, lns_ref, gs_ref):
        b, s = pl.program_id(0), pl.program_id(1)
        gate = mod_ref[3 * sub + 2:3 * sub + 3, :]
        xhat, rstd = _ln_stats(r_ref[...])
        dxo_v = dxo_ref[...]
        dxhat = dxo_v * g_ref[...]
        m1 = jnp.mean(dxhat, axis=-1, keepdims=True)
        m2 = jnp.mean(dxhat * xhat, axis=-1, keepdims=True)
        dr = rstd * (dxhat - m1 - xhat * m2)
        dxres_ref[...] = ALPHA * dr
        df_ref[...] = (dr * (gate * weight)).astype(BF16)

        @pl.when((b == 0) & (s == 0))
        def _():
            lns_ref[...] = jnp.zeros_like(lns_ref)

        @pl.when(s == 0)
        def _():
            gs_ref[...] = jnp.zeros_like(gs_ref)

        lns_ref[0:1, :] += _colsum(dxo_v * xhat)
        lns_ref[1:2, :] += _colsum(dxo_v)
        gs_ref[0:1, :] += _colsum(dr * (weight * f_ref[...]))

    row = _rows(ts, D_MODEL, ns)
    return pl.pallas_call(
        body, name=f"res_ln_bwd{sub}", grid=grid,
        in_specs=[row, row, row, _mod_spec(), _vec_spec()],
        out_specs=[row, row, _vec_spec(8), pl.BlockSpec((None, 8, D_MODEL), lambda b, s: (b, 0, 0))],
        out_shape=[jax.ShapeDtypeStruct(r.shape, F32), jax.ShapeDtypeStruct(r.shape, BF16),
                   jax.ShapeDtypeStruct((8, D_MODEL), F32), jax.ShapeDtypeStruct((bsz, 8, D_MODEL), F32)],
        compiler_params=_params(("arbitrary", "arbitrary")),
    )(r, dxo, f, mod, ln_g)


def _mod_bwd(dxres, du, x, mod, bsz, seq, sub):
    ts = _pick(seq, (256, 128))
    grid, ns = _row_grid(bsz, seq, ts)

    def body(dxres_ref, du_ref, x_ref, mod_ref, dx_ref, st_ref):
        s = pl.program_id(1)
        sc = mod_ref[3 * sub + 1:3 * sub + 2, :]
        du_v = du_ref[...]
        dx_ref[...] = dxres_ref[...] + du_v * (1.0 + sc)

        @pl.when(s == 0)
        def _():
            st_ref[...] = jnp.zeros_like(st_ref)

        st_ref[0:1, :] += _colsum(du_v)
        st_ref[1:2, :] += _colsum(du_v * x_ref[...])

    row = _rows(ts, D_MODEL, ns)
    return pl.pallas_call(
        body, name=f"mod_bwd{sub}", grid=grid, in_specs=[row, row, row, _mod_spec()],
        out_specs=[row, pl.BlockSpec((None, 8, D_MODEL), lambda b, s: (b, 0, 0))],
        out_shape=[jax.ShapeDtypeStruct(x.shape, F32), jax.ShapeDtypeStruct((bsz, 8, D_MODEL), F32)],
        compiler_params=_params(("parallel", "arbitrary")),
    )(dxres, du, x, mod)


_COL_GLU_A, _COL_GLU_B, _COL_GATE_A, _COL_GATE_B = 3, 4, 5, 6


def _merge_fwd(proj, ysb, yconv, bsz, seq):
    ts = _pick(seq, (512, 256, 128))
    grid, ns = _row_grid(bsz, seq, ts)

    def body(ga_ref, gb_ref, ysb_ref, yc_ref, o_ref):
        o_ref[...] = (_sigmoid(ga_ref[...]) * ysb_ref[...] + _sigmoid(gb_ref[...]) * yc_ref[...]).astype(BF16)

    row = _rows(ts, D_MODEL, ns)
    return pl.pallas_call(
        body, name="merge_fwd", grid=grid,
        in_specs=[_rows(ts, D_MODEL, ns, _COL_GATE_A), _rows(ts, D_MODEL, ns, _COL_GATE_B), row, row],
        out_specs=row, out_shape=jax.ShapeDtypeStruct(ysb.shape, BF16),
        compiler_params=_params(("parallel", "parallel")),
    )(proj, proj, ysb, yconv)


def _merge_bwd(proj, ysb, yconv, dmerged, bsz, seq):
    ts = _pick(seq, (256, 128))
    grid, ns = _row_grid(bsz, seq, ts)

    def body(ga_ref, gb_ref, ysb_ref, yc_ref, dm_ref, dysb_ref, dyc_ref, dg_ref):
        sa = _sigmoid(ga_ref[...])
        sb = _sigmoid(gb_ref[...])
        dm = dm_ref[...]
        dysb_ref[...] = (dm * sa).astype(BF16)
        dyc_ref[...] = (dm * sb).astype(BF16)
        dg_ref[:, :D_MODEL] = (dm * ysb_ref[...] * sa * (1.0 - sa)).astype(BF16)
        dg_ref[:, D_MODEL:] = (dm * yc_ref[...] * sb * (1.0 - sb)).astype(BF16)

    row = _rows(ts, D_MODEL, ns)
    t = ysb.shape[0]
    return pl.pallas_call(
        body, name="merge_bwd", grid=grid,
        in_specs=[_rows(ts, D_MODEL, ns, _COL_GATE_A), _rows(ts, D_MODEL, ns, _COL_GATE_B), row, row, row],
        out_specs=[row, row, _rows(ts, 2 * D_MODEL, ns)],
        out_shape=[jax.ShapeDtypeStruct((t, D_MODEL), BF16), jax.ShapeDtypeStruct((t, D_MODEL), BF16),
                   jax.ShapeDtypeStruct((t, 2 * D_MODEL), BF16)],
        compiler_params=_params(("parallel", "parallel")),
    )(proj, proj, ysb, yconv, dmerged)


_CONV_ROWS = 128


def _halo_prev(tt, ns, col):
    r = tt // HALO
    return pl.BlockSpec((HALO, D_MODEL), lambda b, s: (jnp.maximum((b * ns + s) * r - 1, 0), col))


def _halo_next(tt, ns, nblk, col):
    r = tt // HALO
    return pl.BlockSpec((HALO, D_MODEL), lambda b, s: (jnp.minimum((b * ns + s + 1) * r, nblk - 1), col))


def _fill_hc(hpad, a_ref, b_ref, ha_ref, hb_ref, s):
    halo = ha_ref[...] * _sigmoid(hb_ref[...])
    hpad[0:HALO, :] = jnp.where(s > 0, halo, 0.0)
    hpad[HALO:, :] = a_ref[...] * _sigmoid(b_ref[...])


def _conv_fwd(proj, conv_w, conv_b, ln_g, ln_b, bsz, seq):
    tt = _CONV_ROWS
    grid, ns = _row_grid(bsz, seq, tt)
    off = HALO - (CONV_TAPS - 1)

    def body(a_ref, b_ref, ha_ref, hb_ref, w_ref, cb_ref, g_ref, bb_ref, cs_ref, cv_ref, hpad):
        _fill_hc(hpad, a_ref, b_ref, ha_ref, hb_ref, pl.program_id(1))
        acc = jnp.zeros((tt, D_MODEL), F32)
        for j in range(CONV_TAPS):
            acc = acc + w_ref[j:j + 1, :] * hpad[off + j:off + j + tt, :]
        cv = acc + cb_ref[...]
        cv_ref[...] = cv
        xhat, _ = _ln_stats(cv)
        cs_ref[...] = _silu(xhat * g_ref[...] + bb_ref[...]).astype(BF16)

    row = _rows(tt, D_MODEL, ns)
    t = proj.shape[0]
    return pl.pallas_call(
        body, name="conv_fwd", grid=grid,
        in_specs=[_rows(tt, D_MODEL, ns, _COL_GLU_A), _rows(tt, D_MODEL, ns, _COL_GLU_B),
                  _halo_prev(tt, ns, _COL_GLU_A), _halo_prev(tt, ns, _COL_GLU_B),
                  _vec_spec(32), _vec_spec(), _vec_spec(), _vec_spec()],
        out_specs=[row, row],
        out_shape=[jax.ShapeDtypeStruct((t, D_MODEL), BF16), jax.ShapeDtypeStruct((t, D_MODEL), F32)],
        scratch_shapes=[pltpu.VMEM((HALO + tt, D_MODEL), F32)],
        compiler_params=_params(("parallel", "parallel")),
    )(proj, proj, proj, proj, conv_w, conv_b, ln_g, ln_b)


def _conv_bwd_ln(dcs, cv, ln_g, ln_b, bsz, seq):
    ts = _pick(seq, (256, 128))
    grid, ns = _row_grid(bsz, seq, ts)

    def body(dcs_ref, cv_ref, g_ref, b_ref, dcv_ref, st_ref):
        xhat, rstd = _ln_stats(cv_ref[...])
        cl = xhat * g_ref[...] + b_ref[...]
        dcl = dcs_ref[...] * _dsilu(cl)
        dxhat = dcl * g_ref[...]
        m1 = jnp.mean(dxhat, axis=-1, keepdims=True)
        m2 = jnp.mean(dxhat * xhat, axis=-1, keepdims=True)
        dcv = rstd * (dxhat - m1 - xhat * m2)
        dcv_ref[...] = dcv

        @pl.when((pl.program_id(0) == 0) & (pl.program_id(1) == 0))
        def _():
            st_ref[...] = jnp.zeros_like(st_ref)

        st_ref[0:1, :] += _colsum(dcl * xhat)
        st_ref[1:2, :] += _colsum(dcl)
        st_ref[2:3, :] += _colsum(dcv)

    row = _rows(ts, D_MODEL, ns)
    return pl.pallas_call(
        body, name="conv_bwd_ln", grid=grid, in_specs=[row, row, _vec_spec(), _vec_spec()],
        out_specs=[row, _vec_spec(8)],
        out_shape=[jax.ShapeDtypeStruct(cv.shape, F32), jax.ShapeDtypeStruct((8, D_MODEL), F32)],
        compiler_params=_params(("arbitrary", "arbitrary")),
    )(dcs, cv, ln_g, ln_b)


def _conv_bwd_taps(proj, dcv, conv_w, bsz, seq):
    tt = _CONV_ROWS
    grid, ns = _row_grid(bsz, seq, tt)
    off = HALO - (CONV_TAPS - 1)
    nblk = proj.shape[0] // HALO

    def body(a_ref, b_ref, ha_ref, hb_ref, d_ref, dn_ref, w_ref, dglu_ref, dw_ref, hpad, dpad):
        s = pl.program_id(1)
        _fill_hc(hpad, a_ref, b_ref, ha_ref, hb_ref, s)
        dcv = d_ref[...]
        dpad[0:tt, :] = dcv
        dpad[tt:, :] = jnp.where(s < ns - 1, dn_ref[...], 0.0)

        @pl.when((pl.program_id(0) == 0) & (s == 0))
        def _():
            dw_ref[...] = jnp.zeros_like(dw_ref)

        dhc = jnp.zeros((tt, D_MODEL), F32)
        for j in range(CONV_TAPS):
            dhc = dhc + w_ref[j:j + 1, :] * dpad[CONV_TAPS - 1 - j:CONV_TAPS - 1 - j + tt, :]
            dw_ref[j:j + 1, :] += _colsum(dcv * hpad[off + j:off + j + tt, :])
        sb = _sigmoid(b_ref[...])
        dglu_ref[:, :D_MODEL] = (dhc * sb).astype(BF16)
        dglu_ref[:, D_MODEL:] = (dhc * a_ref[...] * sb * (1.0 - sb)).astype(BF16)

    t = proj.shape[0]
    return pl.pallas_call(
        body, name="conv_bwd_taps", grid=grid,
        in_specs=[_rows(tt, D_MODEL, ns, _COL_GLU_A), _rows(tt, D_MODEL, ns, _COL_GLU_B),
                  _halo_prev(tt, ns, _COL_GLU_A), _halo_prev(tt, ns, _COL_GLU_B),
                  _rows(tt, D_MODEL, ns), _halo_next(tt, ns, nblk, 0), _vec_spec(32)],
        out_specs=[_rows(tt, 2 * D_MODEL, ns), _vec_spec(32)],
        out_shape=[jax.ShapeDtypeStruct((t, 2 * D_MODEL), BF16), jax.ShapeDtypeStruct((32, D_MODEL), F32)],
        scratch_shapes=[pltpu.VMEM((HALO + tt, D_MODEL), F32), pltpu.VMEM((tt + HALO, D_MODEL), F32)],
        compiler_params=_params(("arbitrary", "arbitrary")),
    )(proj, proj, proj, proj, dcv, dcv, conv_w)


_NT = (((1,), (1,)), ((), ()))
_TN = (((0,), (0,)), ((), ()))


def _dot(a, b, dims=None):
    if dims is None:
        return jnp.dot(a, b, preferred_element_type=F32)
    return lax.dot_general(a, b, dims, preferred_element_type=F32)


def _tri_dot(v, tri):
    hi = v.astype(BF16)
    lo = (v - hi.astype(F32)).astype(BF16)
    return _dot(hi, tri) + _dot(lo, tri)


def _softplus_parts(z):
    t = jnp.exp(-jnp.abs(z))
    den = 1.0 + t
    return jnp.maximum(z, 0.0) + jnp.log(den), t, den


def _attn_fwd(proj, bsz, seq):
    blk = ATT_BLOCK
    nq = seq // blk
    n_pairs = D_MODEL // LANES

    def body(q_ref, k_ref, v_ref, y_ref, rt_ref):
        qi = pl.program_id(2)
        lane = lax.broadcasted_iota(jnp.int32, (blk, LANES), 1)
        first = lane < HEAD_DIM
        q2 = q_ref[...] * 0.125
        q_heads = (jnp.where(first, q2, 0.0).astype(BF16), jnp.where(first, 0.0, q2).astype(BF16))
        rr = lax.broadcasted_iota(jnp.int32, (blk, blk), 0)
        cc = lax.broadcasted_iota(jnp.int32, (blk, blk), 1)
        tri_ge = (rr >= cc).astype(BF16)
        causal = cc < rr

        def tile(kb, carry, masked):
            k_blk = k_ref[pl.ds(pl.multiple_of(kb * blk, blk), blk), :].astype(BF16)
            v_blk = v_ref[pl.ds(pl.multiple_of(kb * blk, blk), blk), :].astype(BF16)
            new = []
            for h in range(2):
                run, acc = carry[2 * h], carry[2 * h + 1]
                z = _dot(q_heads[h], k_blk, _NT)
                sp, _, _ = _softplus_parts(z)
                neg = -sp
                if masked:
                    neg = jnp.where(causal, neg, 0.0)
                r_in = _tri_dot(neg, tri_ge) + run
                w = jnp.exp(z + r_in)
                if masked:
                    w = jnp.where(causal, w, 0.0)
                acc = acc + _dot(w.astype(BF16), v_blk)
                run = run + jnp.sum(neg, axis=1, keepdims=True)
                new += [run, acc]
            return tuple(new)

        zero_run = jnp.zeros((blk, 1), F32)
        zero_acc = jnp.zeros((blk, LANES), F32)
        carry = tile(qi, (zero_run, zero_acc, zero_run, zero_acc), True)
        carry = lax.fori_loop(0, qi, lambda it, c: tile(qi - 1 - it, c, False), carry)
        y_ref[...] = jnp.where(first, carry[1], carry[3]).astype(BF16)
        rt_ref[...] = jnp.where(first, jnp.broadcast_to(carry[0], (blk, LANES)),
                                jnp.broadcast_to(carry[2], (blk, LANES)))

    t = proj.shape[0]
    q_spec = pl.BlockSpec((blk, LANES), lambda b, p, i: (b * nq + i, p))
    return pl.pallas_call(
        body, name="attn_fwd", grid=(bsz, n_pairs, nq),
        in_specs=[q_spec,
                  pl.BlockSpec((seq, LANES), lambda b, p, i: (b, n_pairs + p)),
                  pl.BlockSpec((seq, LANES), lambda b, p, i: (b, 2 * n_pairs + p))],
        out_specs=[q_spec, q_spec],
        out_shape=[jax.ShapeDtypeStruct((t, D_MODEL), BF16), jax.ShapeDtypeStruct((t, D_MODEL), F32)],
        compiler_params=_params(("parallel", "parallel", "arbitrary")),
    )(proj, proj, proj)


def _attn_bwd(proj, rtot, dy, bsz, seq):
    blk = ATT_BLOCK
    nq = seq // blk
    n_pairs = D_MODEL // LANES

    def body(q_ref, k_ref, v_ref, dy_ref, rt_ref, dq_ref, dk_ref, dv_ref, dk_acc, dv_acc):
        qi = pl.program_id(2)

        @pl.when(qi == 0)
        def _():
            dk_acc[...] = jnp.zeros_like(dk_acc)
            dv_acc[...] = jnp.zeros_like(dv_acc)

        lane = lax.broadcasted_iota(jnp.int32, (blk, LANES), 1)
        first = lane < HEAD_DIM
        q2 = q_ref[...] * 0.125
        q2b = q2.astype(BF16)
        q_heads = (jnp.where(first, q2, 0.0).astype(BF16), jnp.where(first, 0.0, q2).astype(BF16))
        dy2 = dy_ref[...]
        dy_heads = (jnp.where(first, dy2, jnp.zeros_like(dy2)), jnp.where(first, jnp.zeros_like(dy2), dy2))
        rt = (rt_ref[:, 0:1], rt_ref[:, HEAD_DIM:HEAD_DIM + 1])
        rr = lax.broadcasted_iota(jnp.int32, (blk, blk), 0)
        cc = lax.broadcasted_iota(jnp.int32, (blk, blk), 1)
        tri_lt = (rr < cc).astype(BF16)
        tri_le = (rr <= cc).astype(BF16)
        causal = cc < rr

        def tile(kb, carry, masked):
            rows = pl.ds(pl.multiple_of(kb * blk, blk), blk)
            k_blk = k_ref[rows, :].astype(BF16)
            v_blk = v_ref[rows, :].astype(BF16)
            new = []
            dk_parts, dv_parts = [], []
            for h in range(2):
                pre, esum, dq = carry[3 * h], carry[3 * h + 1], carry[3 * h + 2]
                z = _dot(q_heads[h], k_blk, _NT)
                sp, t, den = _softplus_parts(z)
                sig = jnp.where(z >= 0, 1.0, t) / den
                neg = -sp
                if masked:
                    neg = jnp.where(causal, neg, 0.0)
                prefix = pre + _tri_dot(neg, tri_lt)
                w = jnp.exp(z + (rt[h] - prefix))
                if masked:
                    w = jnp.where(causal, w, 0.0)
                dw = _dot(dy_heads[h], v_blk, _NT)
                e = dw * w
                big_e = esum + _tri_dot(e, tri_le)
                dz = e - sig * big_e
                if masked:
                    dz = jnp.where(causal, dz, 0.0)
                dzb = dz.astype(BF16)
                wb = w.astype(BF16)
                new += [pre + jnp.sum(neg, axis=1, keepdims=True), esum + jnp.sum(e, axis=1, keepdims=True),
                        dq + _dot(dzb, k_blk)]
                dk_parts.append(_dot(dzb, q2b, _TN))
                dv_parts.append(_dot(wb, dy2, _TN))
            dk_acc[rows, :] += jnp.where(first, dk_parts[0], dk_parts[1])
            dv_acc[rows, :] += jnp.where(first, dv_parts[0], dv_parts[1])
            return tuple(new)

        zero_run = jnp.zeros((blk, 1), F32)
        zero_acc = jnp.zeros((blk, LANES), F32)
        carry = (zero_run, zero_run, zero_acc, zero_run, zero_run, zero_acc)
        carry = lax.fori_loop(0, qi, lambda kb, c: tile(kb, c, False), carry)
        carry = tile(qi, carry, True)
        dq_ref[...] = (jnp.where(first, carry[2], carry[5]) * 0.125).astype(BF16)

        @pl.when(qi == nq - 1)
        def _():
            dk_ref[...] = dk_acc[...].astype(BF16)
            dv_ref[...] = dv_acc[...].astype(BF16)

    t = proj.shape[0]
    q_spec = pl.BlockSpec((blk, LANES), lambda b, p, i: (b * nq + i, p))
    kv_out = pl.BlockSpec((seq, LANES), lambda b, p, i: (b, p))
    out = jax.ShapeDtypeStruct((t, D_MODEL), BF16)
    return pl.pallas_call(
        body, name="attn_bwd", grid=(bsz, n_pairs, nq),
        in_specs=[q_spec,
                  pl.BlockSpec((seq, LANES), lambda b, p, i: (b, n_pairs + p)),
                  pl.BlockSpec((seq, LANES), lambda b, p, i: (b, 2 * n_pairs + p)),
                  q_spec, q_spec],
        out_specs=[q_spec, kv_out, kv_out], out_shape=[out, out, out],
        scratch_shapes=[pltpu.VMEM((seq, LANES), F32), pltpu.VMEM((seq, LANES), F32)],
        compiler_params=_params(("parallel", "parallel", "arbitrary")),
    )(proj, proj, proj, dy, rtot)


def _adamw(w, g, m, v, name):
    rows, cols = w.shape
    tr = _pick(rows, (256, 352, 128, 64, 32, 16, 8))
    c1 = 1.0 - ADAM_B1 ** ADAM_STEP
    c2 = 1.0 - ADAM_B2 ** ADAM_STEP

    def body(w_ref, g_ref, m_ref, v_ref, d_ref, nm_ref, nv_ref):
        g_v = g_ref[...]
        nm = ADAM_B1 * m_ref[...] + (1.0 - ADAM_B1) * g_v
        nv = ADAM_B2 * v_ref[...] + (1.0 - ADAM_B2) * (g_v * g_v)
        nm_ref[...] = nm
        nv_ref[...] = nv
        d_ref[...] = -ADAM_LR * ((nm / c1) / (jnp.sqrt(nv / c2) + ADAM_EPS) + ADAM_WD * w_ref[...])

    spec = pl.BlockSpec((tr, cols), lambda i: (i, 0))
    shape = jax.ShapeDtypeStruct(w.shape, F32)
    return pl.pallas_call(
        body, name=name, grid=(rows // tr,), in_specs=[spec] * 4, out_specs=[spec] * 3, out_shape=[shape] * 3,
        compiler_params=_params(("parallel",)),
    )(w, g, m, v)


def _ffn_fwd(u, w_gu, w_down, bsz, seq, tag):
    h = _matmul(u, w_gu, mode="nn", out_dtype=F32, name=f"{tag}_up")
    p = _swiglu_act(h, bsz, seq, f"{tag}_act")
    f = _matmul(p, w_down, mode="nn", out_dtype=F32, name=f"{tag}_down")
    return h, p, f


def _ffn_bwd(df, u, h, p, w_gu, w_down, bsz, seq, tag):
    dp = _matmul(df, w_down, mode="nt", out_dtype=F32, name=f"{tag}_ddown")
    dh = _swiglu_bwd(h, dp, bsz, seq, f"{tag}_dact")
    g_down = _matmul(p, df, mode="tn", out_dtype=F32, name=f"{tag}_gdown")
    g_gu = _matmul(u, dh, mode="tn", out_dtype=F32, name=f"{tag}_ggu", out_stacked=True)
    du = _matmul(dh, w_gu, mode="nt", out_dtype=F32, name=f"{tag}_dup")
    return du, g_gu, g_down


def _local_step(x, c, target, wts, vecs):
    bsz, seq, _ = x.shape
    t = bsz * seq
    x0 = x.reshape(t, D_MODEL)
    tgt = target.reshape(t, D_MODEL)

    sc = _silu_pad(c)
    mod16 = _matmul(sc, wts["w_ada"], mode="nn", out_dtype=F32, name="ada_fwd", bias=vecs["b_ada"])
    mod = mod16[:bsz].reshape(bsz, 9, D_MODEL)

    u1 = _mod_in(x0, mod, bsz, seq, 0)
    h1, p1, f1 = _ffn_fwd(u1, wts["ffn1_w_gu"], wts["ffn1_w_down"], bsz, seq, "ffn1")
    r1, x1, u2 = _res_ln_fwd(x0, f1, mod, vecs["ln1_g"], vecs["ln1_b"], bsz, seq, 0, 0.5)

    proj = _matmul(u2, wts["w_in"], mode="nn", out_dtype=F32, name="mix_in")
    ya, rtot = _attn_fwd(proj, bsz, seq)
    cs, cv = _conv_fwd(proj, wts["conv_w"], vecs["conv_b"], vecs["conv_ln_g"], vecs["conv_ln_b"], bsz, seq)
    ysb = _matmul(ya, wts["w_sb_out"], mode="nn", out_dtype=F32, name="sb_out")
    yconv = _matmul(cs, wts["w_conv_out"], mode="nn", out_dtype=F32, name="conv_out")
    merged = _merge_fwd(proj, ysb, yconv, bsz, seq)
    o2 = _matmul(merged, wts["w_out"], mode="nn", out_dtype=F32, name="mix_out")
    r2, x2, u3 = _res_ln_fwd(x1, o2, mod, vecs["ln2_g"], vecs["ln2_b"], bsz, seq, 1, 1.0)

    h3, p3, f3 = _ffn_fwd(u3, wts["ffn2_w_gu"], wts["ffn2_w_down"], bsz, seq, "ffn2")
    r3, dy, loss_blk = _res_ln_fwd(x2, f3, mod, vecs["ln3_g"], vecs["ln3_b"], bsz, seq, 2, 0.5, target=tgt)

    grads = {}
    dxres, df, ln3s, g3s = _res_ln_bwd(r3, dy, f3, mod, vecs["ln3_g"], bsz, seq, 2, 0.5)
    du, grads["ffn2_w_gu"], grads["ffn2_w_down"] = _ffn_bwd(
        df, u3, h3, p3, wts["ffn2_w_gu"], wts["ffn2_w_down"], bsz, seq, "ffn2")
    dx2, m3s = _mod_bwd(dxres, du, x2, mod, bsz, seq, 2)

    dxres, do2, ln2s, g2s = _res_ln_bwd(r2, dx2, o2, mod, vecs["ln2_g"], bsz, seq, 1, 1.0)
    dmerged = _matmul(do2, wts["w_out"], mode="nt", out_dtype=F32, name="mix_out_d")
    grads["w_out"] = _matmul(merged, do2, mode="tn", out_dtype=F32, name="mix_out_g")
    dysb, dyconv, dgate = _merge_bwd(proj, ysb, yconv, dmerged, bsz, seq)
    dya = _matmul(dysb, wts["w_sb_out"], mode="nt", out_dtype=BF16, name="sb_out_d")
    grads["w_sb_out"] = _matmul(ya, dysb, mode="tn", out_dtype=F32, name="sb_out_g")
    dcs = _matmul(dyconv, wts["w_conv_out"], mode="nt", out_dtype=F32, name="conv_out_d")
    grads["w_conv_out"] = _matmul(cs, dyconv, mode="tn", out_dtype=F32, name="conv_out_g")
    dcv, convs = _conv_bwd_ln(dcs, cv, vecs["conv_ln_g"], vecs["conv_ln_b"], bsz, seq)
    dglu, g_conv_w = _conv_bwd_taps(proj, dcv, wts["conv_w"], bsz, seq)
    dq, dk, dv = _attn_bwd(proj, rtot, dya, bsz, seq)
    dproj = jnp.concatenate([dq, dk, dv, dglu, dgate], axis=1)
    grads["w_in"] = _matmul(u2, dproj, mode="tn", out_dtype=F32, name="mix_in_g", out_stacked=True)
    du = _matmul(dproj, wts["w_in"], mode="nt", out_dtype=F32, name="mix_in_d")
    dx1, m2s = _mod_bwd(dxres, du, x1, mod, bsz, seq, 1)

    dxres, df, ln1s, g1s = _res_ln_bwd(r1, dx1, f1, mod, vecs["ln1_g"], bsz, seq, 0, 0.5)
    du, grads["ffn1_w_gu"], grads["ffn1_w_down"] = _ffn_bwd(
        df, u1, h1, p1, wts["ffn1_w_gu"], wts["ffn1_w_down"], bsz, seq, "ffn1")
    grad_x, m1s = _mod_bwd(dxres, du, x0, mod, bsz, seq, 0)

    dmod = jnp.stack([m1s[:, 0], m1s[:, 1], g1s[:, 0], m2s[:, 0], m2s[:, 1], g2s[:, 0],
                      m3s[:, 0], m3s[:, 1], g3s[:, 0]], axis=1)
    dmod16 = jnp.zeros((16, 9 * D_MODEL), F32).at[:bsz].set(dmod.reshape(bsz, 9 * D_MODEL))
    grads["w_ada"] = _matmul(sc, dmod16.astype(BF16), mode="tn", out_dtype=F32, name="ada_g", out_stacked=True)

    small = {"dmod": dmod, "ln1": ln1s, "ln2": ln2s, "ln3": ln3s, "conv": convs, "conv_w": g_conv_w,
             "loss": loss_blk}
    return grad_x.reshape(x.shape), grads, small


_HBM = pl.BlockSpec(memory_space=pltpu.HBM)


def _position():
    return lax.axis_index("x"), lax.axis_index("y"), lax.axis_index("c")


def _other_chips(x, y):
    return [(1 - x, y), (x, 1 - y), (1 - x, 1 - y)]


def _cast_into_stack(w_local, chip, name):
    rows, cols = w_local.shape
    tr = _pick(rows, (256, 352, 128, 64, 32, 16))

    def body(chip_ref, w_ref, o_ref):
        o_ref[...] = w_ref[...].astype(BF16)

    return pl.pallas_call(
        body, name=name,
        grid_spec=pltpu.PrefetchScalarGridSpec(
            num_scalar_prefetch=1, grid=(rows // tr,),
            in_specs=[pl.BlockSpec((tr, cols), lambda r, chip_ref: (r, 0))],
            out_specs=pl.BlockSpec((None, tr, cols), lambda r, chip_ref: (chip_ref[0], r, 0))),
        out_shape=jax.ShapeDtypeStruct((N_CHIPS, rows, cols), BF16),
        compiler_params=_params(("parallel",)),
    )(chip, w_local)


def _all_gather_weights(stacks, small):
    n = len(stacks)

    def body(*refs):
        ins, small_in, outs, small_out = refs[:n], refs[n], refs[n + 1:2 * n + 1], refs[2 * n + 1]
        send_sems, recv_sems, fwd_send_sems, fwd_recv_sems, small_sems = refs[2 * n + 2:]
        x, y, c = _position()
        me = 2 * x + y
        chips = _other_chips(x, y)

        def send(i, j):
            px, py = chips[j]
            return pltpu.make_async_remote_copy(
                src_ref=ins[i].at[me, c], dst_ref=outs[i].at[me, c], send_sem=send_sems.at[3 * i + j],
                recv_sem=recv_sems.at[3 * i + j], device_id=(px, py, c), device_id_type=MESH)

        def landed(i, j):
            px, py = chips[j]
            return pltpu.make_async_remote_copy(
                src_ref=ins[i].at[me, c], dst_ref=outs[i].at[2 * px + py, c], send_sem=send_sems.at[3 * i + j],
                recv_sem=recv_sems.at[3 * i + j], device_id=(px, py, c), device_id_type=MESH)

        def forward(i, j, half):
            px, py = chips[j]
            blk = outs[i].at[2 * px + py, half]
            return pltpu.make_async_remote_copy(
                src_ref=blk, dst_ref=blk, send_sem=fwd_send_sems.at[3 * i + j],
                recv_sem=fwd_recv_sems.at[3 * i + j], device_id=(x, y, 1 - c), device_id_type=MESH)

        def small_copy(j, slot):
            px, py = chips[j]
            return pltpu.make_async_remote_copy(
                src_ref=small_in, dst_ref=small_out.at[slot], send_sem=small_sems.at[j],
                recv_sem=small_sems.at[3 + j], device_id=(px, py, c), device_id_type=MESH)

        own_small = pltpu.make_async_copy(small_in, small_out.at[me], small_sems.at[6])
        own_small.start()
        for j in range(3):
            small_copy(j, me).start()
        for i in range(n):
            for j in range(3):
                send(i, j).start()
        for i in range(n):
            for j in range(3):
                landed(i, j).wait_recv()
                forward(i, j, c).start()
        for i in range(n):
            for j in range(3):
                forward(i, j, 1 - c).wait_recv()
        for j, (px, py) in enumerate(chips):
            small_copy(j, 2 * px + py).wait_recv()
        own_small.wait()
        for j in range(3):
            small_copy(j, me).wait_send()
        for i in range(n):
            for j in range(3):
                send(i, j).wait_send()
                forward(i, j, c).wait_send()

    return pl.pallas_call(
        body, name="all_gather_weights",
        out_shape=[jax.ShapeDtypeStruct(s.shape, s.dtype) for s in stacks]
        + [jax.ShapeDtypeStruct((N_CHIPS,) + small.shape, small.dtype)],
        in_specs=[_HBM] * (n + 1), out_specs=[_HBM] * (n + 1),
        input_output_aliases={i: i for i in range(n)},
        scratch_shapes=[pltpu.SemaphoreType.DMA((3 * n,)), pltpu.SemaphoreType.DMA((3 * n,)),
                        pltpu.SemaphoreType.DMA((3 * n,)), pltpu.SemaphoreType.DMA((3 * n,)),
                        pltpu.SemaphoreType.DMA((7,))],
    )(*stacks, small)


def _pair_swap(grads):
    n = len(grads)

    def body(*refs):
        ins, outs = refs[:n], refs[n:2 * n]
        send_sems, recv_sems = refs[2 * n:]
        x, y, c = _position()

        def copy(i, k):
            return pltpu.make_async_remote_copy(
                src_ref=ins[i].at[k, 1 - c], dst_ref=outs[i].at[k], send_sem=send_sems.at[N_CHIPS * i + k],
                recv_sem=recv_sems.at[N_CHIPS * i + k], device_id=(x, y, 1 - c), device_id_type=MESH)

        for i in range(n):
            for k in range(N_CHIPS):
                copy(i, k).start()
        for i in range(n):
            for k in range(N_CHIPS):
                copy(i, k).wait_recv()
        for i in range(n):
            for k in range(N_CHIPS):
                copy(i, k).wait_send()

    return pl.pallas_call(
        body, name="grad_pair_swap",
        out_shape=[jax.ShapeDtypeStruct((N_CHIPS,) + g.shape[2:], F32) for g in grads],
        in_specs=[_HBM] * n, out_specs=[_HBM] * n,
        scratch_shapes=[pltpu.SemaphoreType.DMA((N_CHIPS * n,)), pltpu.SemaphoreType.DMA((N_CHIPS * n,))],
    )(*grads)


def _pair_add(g, got, place, name):
    _, _, rh, cols = g.shape
    tr = _pick(rh, (256, 176, 128, 64, 32, 16, 8))

    def body(place_ref, g_ref, got_ref, p_ref, own_ref):
        s = g_ref[...] + got_ref[...]
        p_ref[...] = s.astype(BF16)

        @pl.when(pl.program_id(1) == place_ref[1])
        def _():
            own_ref[...] = s

    blk = pl.BlockSpec((None, tr, cols), lambda r, k, place_ref: (k, r, 0))
    return pl.pallas_call(
        body, name=name,
        grid_spec=pltpu.PrefetchScalarGridSpec(
            num_scalar_prefetch=1, grid=(rh // tr, N_CHIPS),
            in_specs=[pl.BlockSpec((None, None, tr, cols), lambda r, k, place_ref: (k, place_ref[0], r, 0)), blk],
            out_specs=[blk, pl.BlockSpec((tr, cols), lambda r, k, place_ref: (r, 0))]),
        out_shape=[jax.ShapeDtypeStruct((N_CHIPS, rh, cols), BF16), jax.ShapeDtypeStruct((rh, cols), F32)],
        compiler_params=_params(("parallel", "arbitrary")),
    )(place, g, got)


def _chip_scatter(sums):
    n = len(sums)

    def body(*refs):
        ins, outs = refs[:n], refs[n:2 * n]
        send_sems, recv_sems = refs[2 * n:]
        x, y, c = _position()
        chips = _other_chips(x, y)

        def copy(i, j):
            px, py = chips[j]
            return pltpu.make_async_remote_copy(
                src_ref=ins[i].at[2 * px + py], dst_ref=outs[i].at[j], send_sem=send_sems.at[3 * i + j],
                recv_sem=recv_sems.at[3 * i + j], device_id=(px, py, c), device_id_type=MESH)

        for i in range(n):
            for j in range(3):
                copy(i, j).start()
        for i in range(n):
            for j in range(3):
                copy(i, j).wait_recv()
        for i in range(n):
            for j in range(3):
                copy(i, j).wait_send()

    return pl.pallas_call(
        body, name="grad_chip_scatter",
        out_shape=[jax.ShapeDtypeStruct((3,) + s.shape[1:], BF16) for s in sums],
        in_specs=[_HBM] * n, out_specs=[_HBM] * n,
        scratch_shapes=[pltpu.SemaphoreType.DMA((3 * n,)), pltpu.SemaphoreType.DMA((3 * n,))],
    )(*sums)


def _chip_sum(own, parts, place, name):
    rh, cols = own.shape
    tr = _pick(rh, (256, 176, 128, 64, 32, 16, 8))

    def body(place_ref, own_ref, p_ref, o_ref):
        o_ref[...] = ((own_ref[...] + p_ref[0].astype(F32)) + p_ref[1].astype(F32)) + p_ref[2].astype(F32)

    return pl.pallas_call(
        body, name=name,
        grid_spec=pltpu.PrefetchScalarGridSpec(
            num_scalar_prefetch=1, grid=(rh // tr,),
            in_specs=[pl.BlockSpec((tr, cols), lambda r, place_ref: (r, 0)),
                      pl.BlockSpec((3, tr, cols), lambda r, place_ref: (0, r, 0))],
            out_specs=pl.BlockSpec((None, tr, cols), lambda r, place_ref: (place_ref[0], r, 0))),
        out_shape=jax.ShapeDtypeStruct((2, rh, cols), F32),
        compiler_params=_params(("parallel",)),
    )(place, own, parts)


def _pair_gather(halves):
    n = len(halves)

    def body(*refs):
        ins, outs = refs[:n], refs[n:2 * n]
        send_sems, recv_sems = refs[2 * n:]
        x, y, c = _position()

        def send(i):
            return pltpu.make_async_remote_copy(
                src_ref=ins[i].at[c], dst_ref=outs[i].at[c], send_sem=send_sems.at[i], recv_sem=recv_sems.at[i],
                device_id=(x, y, 1 - c), device_id_type=MESH)

        def landed(i):
            return pltpu.make_async_remote_copy(
                src_ref=ins[i].at[c], dst_ref=outs[i].at[1 - c], send_sem=send_sems.at[i], recv_sem=recv_sems.at[i],
                device_id=(x, y, 1 - c), device_id_type=MESH)

        for i in range(n):
            send(i).start()
        for i in range(n):
            landed(i).wait_recv()
        for i in range(n):
            send(i).wait_send()

    return pl.pallas_call(
        body, name="grad_pair_gather",
        out_shape=[jax.ShapeDtypeStruct(h.shape, F32) for h in halves],
        in_specs=[_HBM] * n, out_specs=[_HBM] * n,
        input_output_aliases={i: i for i in range(n)},
        scratch_shapes=[pltpu.SemaphoreType.DMA((n,)), pltpu.SemaphoreType.DMA((n,))],
    )(*halves)


_MOD_ROWS = 16


def _small_all_reduce(buf, bsz):
    rows, cols = buf.shape
    head = bsz * _MOD_ROWS
    out_rows = rows - head + _MOD_ROWS

    def body(in_ref, o_ref, gath, send_sems, recv_sems):
        x, y, c = _position()
        me = 4 * x + 2 * y + c

        def peer(mask):
            return (x ^ (mask >> 2), y ^ ((mask >> 1) & 1), c ^ (mask & 1))

        def copy(mask):
            return pltpu.make_async_remote_copy(
                src_ref=in_ref, dst_ref=gath.at[me], send_sem=send_sems.at[mask - 1],
                recv_sem=recv_sems.at[mask - 1], device_id=peer(mask), device_id_type=MESH)

        def arrival(mask):
            px, py, pc = peer(mask)
            return pltpu.make_async_remote_copy(
                src_ref=in_ref, dst_ref=gath.at[4 * px + 2 * py + pc], send_sem=send_sems.at[mask - 1],
                recv_sem=recv_sems.at[mask - 1], device_id=peer(mask), device_id_type=MESH)

        for mask in range(1, N_DEV):
            copy(mask).start()
        gath[me] = in_ref[...]
        for mask in range(1, N_DEV):
            arrival(mask).wait_recv()
        for mask in range(1, N_DEV):
            copy(mask).wait_send()
        acc = gath[0]
        for d in range(1, N_DEV):
            acc = acc + gath[d]
        mod = acc[0:_MOD_ROWS]
        for s in range(1, bsz):
            mod = mod + acc[s * _MOD_ROWS:(s + 1) * _MOD_ROWS]
        o_ref[0:_MOD_ROWS, :] = mod
        o_ref[_MOD_ROWS:, :] = acc[head:]

    vm = pl.BlockSpec(memory_space=pltpu.VMEM)
    return pl.pallas_call(
        body, name="small_all_reduce", in_specs=[vm], out_specs=vm,
        out_shape=jax.ShapeDtypeStruct((out_rows, cols), F32),
        scratch_shapes=[pltpu.VMEM((N_DEV, rows, cols), F32), pltpu.SemaphoreType.DMA((N_DEV - 1,)),
                        pltpu.SemaphoreType.DMA((N_DEV - 1,))],
        compiler_params=pltpu.CompilerParams(vmem_limit_bytes=VMEM_LIMIT),
    )(buf)


_COL_SHARDED = ("w_ada", "ffn1_w_gu", "w_in", "ffn2_w_gu")
_ROW_SHARDED = ("ffn1_w_down", "w_sb_out", "w_conv_out", "w_out", "ffn2_w_down")
_BIG = ("w_ada", "ffn1_w_gu", "ffn1_w_down", "w_in", "w_sb_out", "w_conv_out", "w_out", "ffn2_w_gu", "ffn2_w_down")
_VECS = ("b_ada", "ln1_g", "ln1_b", "conv_b", "conv_ln_g", "conv_ln_b", "ln2_g", "ln2_b", "ln3_g", "ln3_b")
_WEIGHTS = ("w_ada", "b_ada", "ffn1_w_gu", "ffn1_w_down", "ln1_g", "ln1_b", "w_in", "w_sb_out", "conv_w", "conv_b",
            "conv_ln_g", "conv_ln_b", "w_conv_out", "w_out", "ln2_g", "ln2_b", "ffn2_w_gu", "ffn2_w_down",
            "ln3_g", "ln3_b")


def _step(x, c, target, w, m, v):
    bsz = x.shape[0]
    chip = 2 * lax.axis_index("x") + lax.axis_index("y")
    core = lax.axis_index("c")

    chip_arr = jnp.reshape(chip, (1,)).astype(jnp.int32)
    place = jnp.stack([core, chip]).astype(jnp.int32)

    conv_w_local = jnp.pad(w["conv_w"][0], ((0, 1), (0, 0)))
    stacks = []
    for n in _BIG:
        rows, cols = w[n].shape[1:]
        s = _cast_into_stack(w[n][0], chip_arr, f"cast_{n}")
        stacks.append(s.reshape(N_CHIPS, 2, rows // 2, cols))
    gathered = _all_gather_weights(stacks, conv_w_local)
    wts = {}
    for n, g in zip(_BIG, gathered[:-1]):
        rows, cols = w[n].shape[1:]
        wts[n] = g.reshape(N_CHIPS, rows, cols) if n in _COL_SHARDED else g.reshape(N_CHIPS * rows, cols)
    wts["conv_w"] = gathered[-1].transpose(1, 0, 2).reshape(32, D_MODEL)
    vecs = {n: w[n] for n in _VECS}

    grad_x, grads, small = _local_step(x, c, target, wts, vecs)

    views = []
    for n in _BIG:
        g = grads[n]
        rows, cols = w[n].shape[1:]
        views.append(g.reshape(N_CHIPS, 2, rows // 2, cols))
    got = _pair_swap(views)
    pair_sums = [_pair_add(g, r, place, f"pair_add_{n}") for n, g, r in zip(_BIG, views, got)]
    parts = _chip_scatter([p for p, _ in pair_sums])
    halves = [_chip_sum(own, p, place, f"chip_sum_{n}") for n, (_, own), p in zip(_BIG, pair_sums, parts)]
    full = _pair_gather(halves)
    g_out = {n: f.reshape(w[n].shape[1:]) for n, f in zip(_BIG, full)}

    dmod = jnp.pad(small["dmod"], ((0, 0), (0, _MOD_ROWS - 9), (0, 0))).reshape(bsz * _MOD_ROWS, D_MODEL)
    loss_rows = jnp.pad(small["loss"], ((0, 0), (0, D_MODEL - LANES)))
    buf = jnp.concatenate([dmod, small["ln1"], small["ln2"], small["ln3"], small["conv"], small["conv_w"],
                           loss_rows], axis=0)
    red = _small_all_reduce(buf, bsz)
    o = _MOD_ROWS
    g_out["b_ada"] = red[0:9].reshape(1, 9 * D_MODEL)
    g_out["ln1_g"], g_out["ln1_b"] = red[o:o + 1], red[o + 1:o + 2]
    g_out["ln2_g"], g_out["ln2_b"] = red[o + 8:o + 9], red[o + 9:o + 10]
    g_out["ln3_g"], g_out["ln3_b"] = red[o + 16:o + 17], red[o + 17:o + 18]
    g_out["conv_ln_g"], g_out["conv_ln_b"], g_out["conv_b"] = red[o + 24:o + 25], red[o + 25:o + 26], red[o + 26:o + 27]
    cw = w["conv_w"].shape[2]
    g_out["conv_w"] = lax.dynamic_slice(red[o + 32:o + 32 + CONV_TAPS], (0, chip * cw), (CONV_TAPS, cw))
    loss = red[o + 64, 0]

    outs_g, outs_d, outs_m, outs_v = [], [], [], []
    for n in _WEIGHTS:
        shape = w[n].shape
        flat = shape[1:] if len(shape) == 3 else shape
        d, nm, nv = _adamw(w[n].reshape(flat), g_out[n].reshape(flat), m[n].reshape(flat), v[n].reshape(flat),
                           f"adamw_{n}")
        outs_g.append(g_out[n].reshape(shape))
        outs_d.append(d.reshape(shape))
        outs_m.append(nm.reshape(shape))
        outs_v.append(nv.reshape(shape))
    return (loss, grad_x, *outs_g, *outs_d, *outs_m, *outs_v)


def kernel(x, c, w_ada, b_ada, ffn1_w_gu, ffn1_w_down, ln1_g, ln1_b, w_in, w_sb_out, conv_w, conv_b, conv_ln_g, conv_ln_b, w_conv_out, w_out, ln2_g, ln2_b, ffn2_w_gu, ffn2_w_down, ln3_g, ln3_b, loss_target, m_w_ada, m_b_ada, m_ffn1_w_gu, m_ffn1_w_down, m_ln1_g, m_ln1_b, m_w_in, m_w_sb_out, m_conv_w, m_conv_b, m_conv_ln_g, m_conv_ln_b, m_w_conv_out, m_w_out, m_ln2_g, m_ln2_b, m_ffn2_w_gu, m_ffn2_w_down, m_ln3_g, m_ln3_b, v_w_ada, v_b_ada, v_ffn1_w_gu, v_ffn1_w_down, v_ln1_g, v_ln1_b, v_w_in, v_w_sb_out, v_conv_w, v_conv_b, v_conv_ln_g, v_conv_ln_b, v_w_conv_out, v_w_out, v_ln2_g, v_ln2_b, v_ffn2_w_gu, v_ffn2_w_down, v_ln3_g, v_ln3_b):
    given = dict(locals())
    w = {n: given[n] for n in _WEIGHTS}
    m = {n: given["m_" + n] for n in _WEIGHTS}
    v = {n: given["v_" + n] for n in _WEIGHTS}
    return _step(x, c, loss_target, w, m, v)
```

```python
import functools

import jax
import jax.numpy as jnp
from jax import lax
from jax.experimental import pallas as pl
from jax.experimental.pallas import tpu as pltpu

F32 = jnp.float32
BF16 = jnp.bfloat16

D_MODEL = 1024
D_FF = 2816
HEADS = 16
HEAD_DIM = 64
LANES = 128
CONV_TAPS = 31
HALO = 32
N_CHIPS = 4
N_DEV = 8
ALPHA = 2.0 ** 0.25
LN_EPS = 1e-5
ATT_BLOCK = 256
VMEM_LIMIT = 56 * 1024 * 1024

ADAM_LR = 0.001
ADAM_B1 = 0.9
ADAM_B2 = 0.999
ADAM_EPS = 1e-08
ADAM_WD = 0.01
ADAM_STEP = 10

MESH = pl.DeviceIdType.MESH


def _pick(n, cands):
    for t in cands:
        if t <= n and n % t == 0:
            return t
    return n


def _params(sem):
    return pltpu.CompilerParams(dimension_semantics=sem, vmem_limit_bytes=VMEM_LIMIT)


def _sigmoid(z):
    t = jnp.exp(-jnp.abs(z))
    return jnp.where(z >= 0, 1.0, t) / (1.0 + t)


def _silu(z):
    return z * _sigmoid(z)


def _dsilu(z):
    s = _sigmoid(z)
    return s * (1.0 + z * (1.0 - s))


def _ln_stats(r):
    mu = jnp.mean(r, axis=-1, keepdims=True)
    d = r - mu
    var = jnp.mean(d * d, axis=-1, keepdims=True)
    rstd = lax.rsqrt(var + LN_EPS)
    return d * rstd, rstd


def _colsum(v):
    return jnp.sum(v, axis=0, keepdims=True)


_DIMS = {"nn": (((1,), (0,)), ((), ())), "nt": (((1,), (1,)), ((), ())), "tn": (((0,), (0,)), ((), ()))}
_TN_CANDS = (1408, 1792, 1152, 1024, 512, 256, 128)
_TK_CANDS = (1024, 1408, 896, 512, 256, 128)


def _matmul(a, b, *, mode, out_dtype, name, bias=None, out_stacked=False):
    b_stacked = b.ndim == 3
    if mode == "nn":
        m, k = a.shape
        n_c = b.shape[-1]
        n = n_c * (N_CHIPS if b_stacked else 1)
        k_c = k
    elif mode == "nt":
        m, k = a.shape
        n = b.shape[-2]
        k_c = b.shape[-1]
        n_c = n
    else:
        k, m = a.shape
        n = b.shape[1]
        n_c = n // N_CHIPS if out_stacked else n
        k_c = k
    if mode == "tn":
        tm = _pick(m, (1024, 1408, 512, 256, 128))
        tk = _pick(k, (512, 256, 128, 64, 32, 16))
    else:
        tm = _pick(m, (1024, 512, 256, 128, 64, 32, 16))
        tk = _pick(k_c, _TK_CANDS)
    tn = _pick(n_c, _TN_CANDS)
    nb = n_c // tn
    kb = k_c // tk
    nk = k // tk
    grid = (m // tm, n // tn, nk)

    if mode == "nn":
        a_spec = pl.BlockSpec((tm, tk), lambda i, j, kk: (i, kk))
        if b_stacked:
            b_spec = pl.BlockSpec((None, tk, tn), lambda i, j, kk: (j // nb, kk, j % nb))
        else:
            b_spec = pl.BlockSpec((tk, tn), lambda i, j, kk: (kk, j))
    elif mode == "nt":
        a_spec = pl.BlockSpec((tm, tk), lambda i, j, kk: (i, kk))
        if b_stacked:
            b_spec = pl.BlockSpec((None, tn, tk), lambda i, j, kk: (kk // kb, j, kk % kb))
        else:
            b_spec = pl.BlockSpec((tn, tk), lambda i, j, kk: (j, kk))
    else:
        a_spec = pl.BlockSpec((tk, tm), lambda i, j, kk: (kk, i))
        b_spec = pl.BlockSpec((tk, tn), lambda i, j, kk: (kk, j))
    if out_stacked:
        out_shape = jax.ShapeDtypeStruct((N_CHIPS, m, n_c), out_dtype)
        o_spec = pl.BlockSpec((None, tm, tn), lambda i, j, kk: (j // nb, i, j % nb))
    else:
        out_shape = jax.ShapeDtypeStruct((m, n), out_dtype)
        o_spec = pl.BlockSpec((tm, tn), lambda i, j, kk: (i, j))
    in_specs = [a_spec, b_spec]
    args = [a, b]
    if bias is not None:
        in_specs.append(pl.BlockSpec((1, tn), lambda i, j, kk: (0, j)))
        args.append(bias)
    dims = _DIMS[mode]

    def body(*refs):
        a_ref, b_ref = refs[0], refs[1]
        bias_ref = refs[2] if bias is not None else None
        o_ref, acc_ref = refs[-2], refs[-1]
        kk = pl.program_id(2)

        @pl.when(kk == 0)
        def _():
            acc_ref[...] = jnp.zeros_like(acc_ref)

        acc_ref[...] += lax.dot_general(a_ref[...], b_ref[...], dims, preferred_element_type=F32)

        @pl.when(kk == nk - 1)
        def _():
            r = acc_ref[...]
            if bias_ref is not None:
                r = r + bias_ref[...]
            o_ref[...] = r.astype(o_ref.dtype)

    return pl.pallas_call(
        body, name=name, grid=grid, in_specs=in_specs, out_specs=o_spec, out_shape=out_shape,
        scratch_shapes=[pltpu.VMEM((tm, tn), F32)],
        compiler_params=_params(("parallel", "parallel", "arbitrary")),
    )(*args)


def _row_grid(bsz, seq, ts):
    ns = seq // ts
    return (bsz, ns), ns


def _rows(ts, width, ns, col=0):
    return pl.BlockSpec((ts, width), lambda b, s: (b * ns + s, col))


def _mod_spec():
    return pl.BlockSpec((None, 9, D_MODEL), lambda b, s: (b, 0, 0))


def _vec_spec(rows=1, width=D_MODEL):
    return pl.BlockSpec((rows, width), lambda b, s: (0, 0))


def _silu_pad(c):
    bsz = c.shape[0]

    def body(c_ref, o_ref):
        o_ref[...] = jnp.zeros_like(o_ref)
        o_ref[0:bsz, :] = _silu(c_ref[...]).astype(BF16)

    return pl.pallas_call(body, name="silu_pad", out_shape=jax.ShapeDtypeStruct((16, D_MODEL), BF16))(c)


def _mod_in(x, mod, bsz, seq, sub):
    ts = _pick(seq, (512, 256, 128))
    grid, ns = _row_grid(bsz, seq, ts)

    def body(x_ref, mod_ref, u_ref):
        sh = mod_ref[3 * sub:3 * sub + 1, :]
        sc = mod_ref[3 * sub + 1:3 * sub + 2, :]
        u_ref[...] = (x_ref[...] * (1.0 + sc) + sh).astype(BF16)

    return pl.pallas_call(
        body, name=f"mod_in{sub}", grid=grid, in_specs=[_rows(ts, D_MODEL, ns), _mod_spec()],
        out_specs=_rows(ts, D_MODEL, ns), out_shape=jax.ShapeDtypeStruct(x.shape, BF16),
        compiler_params=_params(("parallel", "parallel")),
    )(x, mod)


def _swiglu_act(h, bsz, seq, name):
    ts = _pick(seq, (256, 128))
    grid, ns = _row_grid(bsz, seq, ts)

    def body(h_ref, p_ref):
        a = h_ref[:, :D_FF]
        g = h_ref[:, D_FF:]
        p_ref[...] = (_silu(a) * g).astype(BF16)

    return pl.pallas_call(
        body, name=name, grid=grid, in_specs=[_rows(ts, 2 * D_FF, ns)],
        out_specs=_rows(ts, D_FF, ns), out_shape=jax.ShapeDtypeStruct((h.shape[0], D_FF), BF16),
        compiler_params=_params(("parallel", "parallel")),
    )(h)


def _swiglu_bwd(h, dp, bsz, seq, name):
    ts = _pick(seq, (256, 128))
    grid, ns = _row_grid(bsz, seq, ts)

    def body(h_ref, dp_ref, dh_ref):
        a = h_ref[:, :D_FF]
        g = h_ref[:, D_FF:]
        dp_v = dp_ref[...]
        dh_ref[:, :D_FF] = (dp_v * g * _dsilu(a)).astype(BF16)
        dh_ref[:, D_FF:] = (dp_v * _silu(a)).astype(BF16)

    return pl.pallas_call(
        body, name=name, grid=grid, in_specs=[_rows(ts, 2 * D_FF, ns), _rows(ts, D_FF, ns)],
        out_specs=_rows(ts, 2 * D_FF, ns), out_shape=jax.ShapeDtypeStruct(h.shape, BF16),
        compiler_params=_params(("parallel", "parallel")),
    )(h, dp)


def _res_ln_fwd(x, f, mod, ln_g, ln_b, bsz, seq, sub, weight, target=None):
    ts = _pick(seq, (256, 128))
    grid, ns = _row_grid(bsz, seq, ts)
    last = target is not None

    def body(*refs):
        x_ref, f_ref, mod_ref, g_ref, b_ref = refs[:5]
        gate = mod_ref[3 * sub + 2:3 * sub + 3, :]
        r = ALPHA * x_ref[...] + gate * (weight * f_ref[...])
        xhat, _ = _ln_stats(r)
        xo = xhat * g_ref[...] + b_ref[...]
        if last:
            t_ref, r_ref, dy_ref, loss_ref = refs[5:]
            diff = xo - t_ref[...]
            dy_ref[...] = diff * (1.0 / D_MODEL)
            part = 0.5 * jnp.sum(jnp.mean(diff * diff, axis=-1, keepdims=True), axis=0, keepdims=True)

            @pl.when((pl.program_id(0) == 0) & (pl.program_id(1) == 0))
            def _():
                loss_ref[...] = jnp.zeros_like(loss_ref)

            loss_ref[...] += jnp.broadcast_to(part, loss_ref.shape)
        else:
            r_ref, xo_ref, u_ref = refs[5:]
            xo_ref[...] = xo
            sh = mod_ref[3 * sub + 3:3 * sub + 4, :]
            sc = mod_ref[3 * sub + 4:3 * sub + 5, :]
            u_ref[...] = (xo * (1.0 + sc) + sh).astype(BF16)
        r_ref[...] = r

    row = _rows(ts, D_MODEL, ns)
    in_specs = [row, row, _mod_spec(), _vec_spec(), _vec_spec()]
    args = [x, f, mod, ln_g, ln_b]
    if last:
        in_specs.append(row)
        args.append(target)
        out_specs = [row, row, _vec_spec(8, LANES)]
        out_shape = [jax.ShapeDtypeStruct(x.shape, F32), jax.ShapeDtypeStruct(x.shape, F32),
                     jax.ShapeDtypeStruct((8, LANES), F32)]
        sem = ("arbitrary", "arbitrary")
    else:
        out_specs = [row, row, row]
        out_shape = [jax.ShapeDtypeStruct(x.shape, F32), jax.ShapeDtypeStruct(x.shape, F32),
                     jax.ShapeDtypeStruct(x.shape, BF16)]
        sem = ("parallel", "parallel")
    return pl.pallas_call(
        body, name=f"res_ln_fwd{sub}", grid=grid, in_specs=in_specs, out_specs=out_specs, out_shape=out_shape,
        compiler_params=_params(sem),
    )(*args)


def _res_ln_bwd(r, dxo, f, mod, ln_g, bsz, seq, sub, weight):
    ts = _pick(seq, (256, 128))
    grid, ns = _row_grid(bsz, seq, ts)

    def body(r_ref, dxo_ref, f_ref, mod_ref, g_ref, dxres_ref, df_ref, lns_ref, gs_ref):
        b, s = pl.program_id(0), pl.program_id(1)
        gate = mod_ref[3 * sub + 2:3 * sub + 3, :]
        xhat, rstd = _ln_stats(r_ref[...])
        dxo_v = dxo_ref[...]
        dxhat = dxo_v * g_ref[...]
        m1 = jnp.mean(dxhat, axis=-1, keepdims=True)
        m2 = jnp.mean(dxhat * xhat, axis=-1, keepdims=True)
        dr = rstd * (dxhat - m1 - xhat * m2)
        dxres_ref[...] = ALPHA * dr
        df_ref[...] = (dr * (gate * weight)).astype(BF16)

        @pl.when((b == 0) & (s == 0))
        def _():
            lns_ref[...] = jnp.zeros_like(lns_ref)

        @pl.when(s == 0)
        def _():
            gs_ref[...] = jnp.zeros_like(gs_ref)

        lns_ref[0:1, :] += _colsum(dxo_v * xhat)
        lns_ref[1:2, :] += _colsum(dxo_v)
        gs_ref[0:1, :] += _colsum(dr * (weight * f_ref[...]))

    row = _rows(ts, D_MODEL, ns)
    return pl.pallas_call(
        body, name=f"res_ln_bwd{sub}", grid=grid,
        in_specs=[row, row, row, _mod_spec(), _vec_spec()],
        out_specs=[row, row, _vec_spec(8), pl.BlockSpec((None, 8, D_MODEL), lambda b, s: (b, 0, 0))],
        out_shape=[jax.ShapeDtypeStruct(r.shape, F32), jax.ShapeDtypeStruct(r.shape, BF16),
                   jax.ShapeDtypeStruct((8, D_MODEL), F32), jax.ShapeDtypeStruct((bsz, 8, D_MODEL), F32)],
        compiler_params=_params(("arbitrary", "arbitrary")),
    )(r, dxo, f, mod, ln_g)


def _mod_bwd(dxres, du, x, mod, bsz, seq, sub):
    ts = _pick(seq, (256, 128))
    grid, ns = _row_grid(bsz, seq, ts)

    def body(dxres_ref, du_ref, x_ref, mod_ref, dx_ref, st_ref):
        s = pl.program_id(1)
        sc = mod_ref[3 * sub + 1:3 * sub + 2, :]
        du_v = du_ref[...]
        dx_ref[...] = dxres_ref[...] + du_v * (1.0 + sc)

        @pl.when(s == 0)
        def _():
            st_ref[...] = jnp.zeros_like(st_ref)

        st_ref[0:1, :] += _colsum(du_v)
        st_ref[1:2, :] += _colsum(du_v * x_ref[...])

    row = _rows(ts, D_MODEL, ns)
    return pl.pallas_call(
        body, name=f"mod_bwd{sub}", grid=grid, in_specs=[row, row, row, _mod_spec()],
        out_specs=[row, pl.BlockSpec((None, 8, D_MODEL), lambda b, s: (b, 0, 0))],
        out_shape=[jax.ShapeDtypeStruct(x.shape, F32), jax.ShapeDtypeStruct((bsz, 8, D_MODEL), F32)],
        compiler_params=_params(("parallel", "arbitrary")),
    )(dxres, du, x, mod)


_COL_GLU_A, _COL_GLU_B, _COL_GATE_A, _COL_GATE_B = 3, 4, 5, 6


def _merge_fwd(proj, ysb, yconv, bsz, seq):
    ts = _pick(seq, (512, 256, 128))
    grid, ns = _row_grid(bsz, seq, ts)

    def body(ga_ref, gb_ref, ysb_ref, yc_ref, o_ref):
        o_ref[...] = (_sigmoid(ga_ref[...]) * ysb_ref[...] + _sigmoid(gb_ref[...]) * yc_ref[...]).astype(BF16)

    row = _rows(ts, D_MODEL, ns)
    return pl.pallas_call(
        body, name="merge_fwd", grid=grid,
        in_specs=[_rows(ts, D_MODEL, ns, _COL_GATE_A), _rows(ts, D_MODEL, ns, _COL_GATE_B), row, row],
        out_specs=row, out_shape=jax.ShapeDtypeStruct(ysb.shape, BF16),
        compiler_params=_params(("parallel", "parallel")),
    )(proj, proj, ysb, yconv)


def _merge_bwd(proj, ysb, yconv, dmerged, bsz, seq):
    ts = _pick(seq, (256, 128))
    grid, ns = _row_grid(bsz, seq, ts)

    def body(ga_ref, gb_ref, ysb_ref, yc_ref, dm_ref, dysb_ref, dyc_ref, dg_ref):
        sa = _sigmoid(ga_ref[...])
        sb = _sigmoid(gb_ref[...])
        dm = dm_ref[...]
        dysb_ref[...] = (dm * sa).astype(BF16)
        dyc_ref[...] = (dm * sb).astype(BF16)
        dg_ref[:, :D_MODEL] = (dm * ysb_ref[...] * sa * (1.0 - sa)).astype(BF16)
        dg_ref[:, D_MODEL:] = (dm * yc_ref[...] * sb * (1.0 - sb)).astype(BF16)

    row = _rows(ts, D_MODEL, ns)
    t = ysb.shape[0]
    return pl.pallas_call(
        body, name="merge_bwd", grid=grid,
        in_specs=[_rows(ts, D_MODEL, ns, _COL_GATE_A), _rows(ts, D_MODEL, ns, _COL_GATE_B), row, row, row],
        out_specs=[row, row, _rows(ts, 2 * D_MODEL, ns)],
        out_shape=[jax.ShapeDtypeStruct((t, D_MODEL), BF16), jax.ShapeDtypeStruct((t, D_MODEL), BF16),
                   jax.ShapeDtypeStruct((t, 2 * D_MODEL), BF16)],
        compiler_params=_params(("parallel", "parallel")),
    )(proj, proj, ysb, yconv, dmerged)


_CONV_ROWS = 128


def _halo_prev(tt, ns, col):
    r = tt // HALO
    return pl.BlockSpec((HALO, D_MODEL), lambda b, s: (jnp.maximum((b * ns + s) * r - 1, 0), col))


def _halo_next(tt, ns, nblk, col):
    r = tt // HALO
    return pl.BlockSpec((HALO, D_MODEL), lambda b, s: (jnp.minimum((b * ns + s + 1) * r, nblk - 1), col))


def _fill_hc(hpad, a_ref, b_ref, ha_ref, hb_ref, s):
    halo = ha_ref[...] * _sigmoid(hb_ref[...])
    hpad[0:HALO, :] = jnp.where(s > 0, halo, 0.0)
    hpad[HALO:, :] = a_ref[...] * _sigmoid(b_ref[...])


def _conv_fwd(proj, conv_w, conv_b, ln_g, ln_b, bsz, seq):
    tt = _CONV_ROWS
    grid, ns = _row_grid(bsz, seq, tt)
    off = HALO - (CONV_TAPS - 1)

    def body(a_ref, b_ref, ha_ref, hb_ref, w_ref, cb_ref, g_ref, bb_ref, cs_ref, cv_ref, hpad):
        _fill_hc(hpad, a_ref, b_ref, ha_ref, hb_ref, pl.program_id(1))
        acc = jnp.zeros((tt, D_MODEL), F32)
        for j in range(CONV_TAPS):
            acc = acc + w_ref[j:j + 1, :] * hpad[off + j:off + j + tt, :]
        cv = acc + cb_ref[...]
        cv_ref[...] = cv
        xhat, _ = _ln_stats(cv)
        cs_ref[...] = _silu(xhat * g_ref[...] + bb_ref[...]).astype(BF16)

    row = _rows(tt, D_MODEL, ns)
    t = proj.shape[0]
    return pl.pallas_call(
        body, name="conv_fwd", grid=grid,
        in_specs=[_rows(tt, D_MODEL, ns, _COL_GLU_A), _rows(tt, D_MODEL, ns, _COL_GLU_B),
                  _halo_prev(tt, ns, _COL_GLU_A), _halo_prev(tt, ns, _COL_GLU_B),
                  _vec_spec(32), _vec_spec(), _vec_spec(), _vec_spec()],
        out_specs=[row, row],
        out_shape=[jax.ShapeDtypeStruct((t, D_MODEL), BF16), jax.ShapeDtypeStruct((t, D_MODEL), F32)],
        scratch_shapes=[pltpu.VMEM((HALO + tt, D_MODEL), F32)],
        compiler_params=_params(("parallel", "parallel")),
    )(proj, proj, proj, proj, conv_w, conv_b, ln_g, ln_b)


def _conv_bwd_ln(dcs, cv, ln_g, ln_b, bsz, seq):
    ts = _pick(seq, (256, 128))
    grid, ns = _row_grid(bsz, seq, ts)

    def body(dcs_ref, cv_ref, g_ref, b_ref, dcv_ref, st_ref):
        xhat, rstd = _ln_stats(cv_ref[...])
        cl = xhat * g_ref[...] + b_ref[...]
        dcl = dcs_ref[...] * _dsilu(cl)
        dxhat = dcl * g_ref[...]
        m1 = jnp.mean(dxhat, axis=-1, keepdims=True)
        m2 = jnp.mean(dxhat * xhat, axis=-1, keepdims=True)
        dcv = rstd * (dxhat - m1 - xhat * m2)
        dcv_ref[...] = dcv

        @pl.when((pl.program_id(0) == 0) & (pl.program_id(1) == 0))
        def _():
            st_ref[...] = jnp.zeros_like(st_ref)

        st_ref[0:1, :] += _colsum(dcl * xhat)
        st_ref[1:2, :] += _colsum(dcl)
        st_ref[2:3, :] += _colsum(dcv)

    row = _rows(ts, D_MODEL, ns)
    return pl.pallas_call(
        body, name="conv_bwd_ln", grid=grid, in_specs=[row, row, _vec_spec(), _vec_spec()],
        out_specs=[row, _vec_spec(8)],
        out_shape=[jax.ShapeDtypeStruct(cv.shape, F32), jax.ShapeDtypeStruct((8, D_MODEL), F32)],
        compiler_params=_params(("arbitrary", "arbitrary")),
    )(dcs, cv, ln_g, ln_b)


def _conv_bwd_taps(proj, dcv, conv_w, bsz, seq):
    tt = _CONV_ROWS
    grid, ns = _row_grid(bsz, seq, tt)
    off = HALO - (CONV_TAPS - 1)
    nblk = proj.shape[0] // HALO

    def body(a_ref, b_ref, ha_ref, hb_ref, d_ref, dn_ref, w_ref, dglu_ref, dw_ref, hpad, dpad):
        s = pl.program_id(1)
        _fill_hc(hpad, a_ref, b_ref, ha_ref, hb_ref, s)
        dcv = d_ref[...]
        dpad[0:tt, :] = dcv
        dpad[tt:, :] = jnp.where(s < ns - 1, dn_ref[...], 0.0)

        @pl.when((pl.program_id(0) == 0) & (s == 0))
        def _():
            dw_ref[...] = jnp.zeros_like(dw_ref)

        dhc = jnp.zeros((tt, D_MODEL), F32)
        for j in range(CONV_TAPS):
            dhc = dhc + w_ref[j:j + 1, :] * dpad[CONV_TAPS - 1 - j:CONV_TAPS - 1 - j + tt, :]
            dw_ref[j:j + 1, :] += _colsum(dcv * hpad[off + j:off + j + tt, :])
        sb = _sigmoid(b_ref[...])
        dglu_ref[:, :D_MODEL] = (dhc * sb).astype(BF16)
        dglu_ref[:, D_MODEL:] = (dhc * a_ref[...] * sb * (1.0 - sb)).astype(BF16)

    t = proj.shape[0]
    return pl.pallas_call(
        body, name="conv_bwd_taps", grid=grid,
        in_specs=[_rows(tt, D_MODEL, ns, _COL_GLU_A), _rows(tt, D_MODEL, ns, _COL_GLU_B),
                  _halo_prev(tt, ns, _COL_GLU_A), _halo_prev(tt, ns, _COL_GLU_B),
                  _rows(tt, D_MODEL, ns), _halo_next(tt, ns, nblk, 0), _vec_spec(32)],
        out_specs=[_rows(tt, 2 * D_MODEL, ns), _vec_spec(32)],
        out_shape=[jax.ShapeDtypeStruct((t, 2 * D_MODEL), BF16), jax.ShapeDtypeStruct((32, D_MODEL), F32)],
        scratch_shapes=[pltpu.VMEM((HALO + tt, D_MODEL), F32), pltpu.VMEM((tt + HALO, D_MODEL), F32)],
        compiler_params=_params(("arbitrary", "arbitrary")),
    )(proj, proj, proj, proj, dcv, dcv, conv_w)


_NT = (((1,), (1,)), ((), ()))
_TN = (((0,), (0,)), ((), ()))


def _dot(a, b, dims=None):
    if dims is None:
        return jnp.dot(a, b, preferred_element_type=F32)
    return lax.dot_general(a, b, dims, preferred_element_type=F32)


def _tri_dot(v, tri2):
    hi = v.astype(BF16)
    lo = (v - hi.astype(F32)).astype(BF16)
    return _dot(jnp.concatenate([hi, lo], axis=1), tri2)


def _tri2(mask):
    t = mask.astype(BF16)
    return jnp.concatenate([t, t], axis=0)


def _softplus_parts(z):
    t = jnp.exp(-jnp.abs(z))
    den = 1.0 + t
    return jnp.maximum(z, 0.0) + jnp.log(den), t, den


def _attn_fwd(proj, bsz, seq):
    blk = ATT_BLOCK
    nq = seq // blk
    n_pairs = D_MODEL // LANES

    def body(q_ref, k_ref, v_ref, y_ref, rt_ref, zr_buf, ns_buf, run_buf, acc_buf):
        qi = pl.program_id(2)
        lane = lax.broadcasted_iota(jnp.int32, (blk, LANES), 1)
        first = lane < HEAD_DIM
        q2 = q_ref[...] * 0.125
        q_heads = (jnp.where(first, q2, 0.0).astype(BF16), jnp.where(first, 0.0, q2).astype(BF16))
        rr = lax.broadcasted_iota(jnp.int32, (blk, blk), 0)
        cc = lax.broadcasted_iota(jnp.int32, (blk, blk), 1)
        tri_ge = _tri2(rr >= cc)
        causal = cc < rr

        def scores(kb, slot, masked):
            k_blk = k_ref[pl.ds(pl.multiple_of(kb * blk, blk), blk), :].astype(BF16)
            for h in range(2):
                z = _dot(q_heads[h], k_blk, _NT)
                sp, _, _ = _softplus_parts(z)
                neg = -sp
                if masked:
                    neg = jnp.where(causal, neg, 0.0)
                    z = jnp.where(causal, z, -1e30)
                zr_buf[slot, h] = z + _tri_dot(neg, tri_ge)
                ns_buf[slot, h] = jnp.sum(neg, axis=1, keepdims=True)

        def weigh(kb, slot):
            v_blk = v_ref[pl.ds(pl.multiple_of(kb * blk, blk), blk), :].astype(BF16)
            for h in range(2):
                run = run_buf[h]
                w = jnp.exp(zr_buf[slot, h] + run)
                acc_buf[h] += _dot(w.astype(BF16), v_blk)
                run_buf[h] = run + ns_buf[slot, h]

        run_buf[...] = jnp.zeros_like(run_buf)
        acc_buf[...] = jnp.zeros_like(acc_buf)
        scores(qi, 0, True)

        def two_steps(p, carry):
            t = 2 * p
            scores(qi - t - 1, 1, False)
            weigh(qi - t, 0)
            scores(qi - t - 2, 0, False)
            weigh(qi - t - 1, 1)
            return carry

        lax.fori_loop(0, qi // 2, two_steps, 0)

        @pl.when(qi % 2 == 1)
        def _():
            scores(0, 1, False)
            weigh(1, 0)
            weigh(0, 1)

        @pl.when(qi % 2 == 0)
        def _():
            weigh(0, 0)

        y_ref[...] = jnp.where(first, acc_buf[0], acc_buf[1]).astype(BF16)
        rt_ref[...] = jnp.where(first, jnp.broadcast_to(run_buf[0], (blk, LANES)),
                                jnp.broadcast_to(run_buf[1], (blk, LANES)))

    t = proj.shape[0]
    q_spec = pl.BlockSpec((blk, LANES), lambda b, p, i: (b * nq + i, p))
    return pl.pallas_call(
        body, name="attn_fwd", grid=(bsz, n_pairs, nq),
        in_specs=[q_spec,
                  pl.BlockSpec((seq, LANES), lambda b, p, i: (b, n_pairs + p)),
                  pl.BlockSpec((seq, LANES), lambda b, p, i: (b, 2 * n_pairs + p))],
        out_specs=[q_spec, q_spec],
        out_shape=[jax.ShapeDtypeStruct((t, D_MODEL), BF16), jax.ShapeDtypeStruct((t, D_MODEL), F32)],
        scratch_shapes=[pltpu.VMEM((2, 2, blk, blk), F32), pltpu.VMEM((2, 2, blk, 1), F32),
                        pltpu.VMEM((2, blk, 1), F32), pltpu.VMEM((2, blk, LANES), F32)],
        compiler_params=_params(("parallel", "parallel", "arbitrary")),
    )(proj, proj, proj)


def _attn_bwd(proj, rtot, dy, bsz, seq):
    blk = ATT_BLOCK
    nq = seq // blk
    n_pairs = D_MODEL // LANES

    def body(q_ref, k_ref, v_ref, dy_ref, rt_ref, dq_ref, dk_ref, dv_ref, dk_acc, dv_acc,
             a_buf, sig_buf, dw_buf, ns_buf, pre_buf, es_buf, dq_buf):
        qi = pl.program_id(2)

        @pl.when(qi == 0)
        def _():
            dk_acc[...] = jnp.zeros_like(dk_acc)
            dv_acc[...] = jnp.zeros_like(dv_acc)

        lane = lax.broadcasted_iota(jnp.int32, (blk, LANES), 1)
        first = lane < HEAD_DIM
        q2 = q_ref[...] * 0.125
        q2b = q2.astype(BF16)
        q_heads = (jnp.where(first, q2, 0.0).astype(BF16), jnp.where(first, 0.0, q2).astype(BF16))
        dy2 = dy_ref[...]
        dy_heads = (jnp.where(first, dy2, jnp.zeros_like(dy2)), jnp.where(first, jnp.zeros_like(dy2), dy2))
        rt = (rt_ref[:, 0:1], rt_ref[:, HEAD_DIM:HEAD_DIM + 1])
        rr = lax.broadcasted_iota(jnp.int32, (blk, blk), 0)
        cc = lax.broadcasted_iota(jnp.int32, (blk, blk), 1)
        tri_lt = _tri2(rr < cc)
        tri_le = _tri2(rr <= cc)
        causal = cc < rr

        def scores(kb, slot):
            rows = pl.ds(pl.multiple_of(kb * blk, blk), blk)
            k_blk = k_ref[rows, :].astype(BF16)
            v_blk = v_ref[rows, :].astype(BF16)
            keep = jnp.logical_or(causal, kb < qi)
            for h in range(2):
                z = jnp.where(keep, _dot(q_heads[h], k_blk, _NT), -1e30)
                sp, t, den = _softplus_parts(z)
                neg = -sp
                a_buf[slot, h] = z - _tri_dot(neg, tri_lt)
                sig_buf[slot, h] = jnp.where(z >= 0, 1.0, t) / den
                ns_buf[slot, h] = jnp.sum(neg, axis=1, keepdims=True)
                dw_buf[slot, h] = _dot(dy_heads[h], v_blk, _NT)

        def finish(kb, slot):
            rows = pl.ds(pl.multiple_of(kb * blk, blk), blk)
            k_blk = k_ref[rows, :].astype(BF16)
            dk_parts, dv_parts = [], []
            for h in range(2):
                pre, esum = pre_buf[h], es_buf[h]
                w = jnp.exp(a_buf[slot, h] + (rt[h] - pre))
                e = dw_buf[slot, h] * w
                dz = e - sig_buf[slot, h] * (esum + _tri_dot(e, tri_le))
                pre_buf[h] = pre + ns_buf[slot, h]
                es_buf[h] = esum + jnp.sum(e, axis=1, keepdims=True)
                dzb = dz.astype(BF16)
                dq_buf[h] += _dot(dzb, k_blk)
                dk_parts.append(_dot(dzb, q2b, _TN))
                dv_parts.append(_dot(w.astype(BF16), dy2, _TN))
            dk_acc[rows, :] += jnp.where(first, dk_parts[0], dk_parts[1])
            dv_acc[rows, :] += jnp.where(first, dv_parts[0], dv_parts[1])

        pre_buf[...] = jnp.zeros_like(pre_buf)
        es_buf[...] = jnp.zeros_like(es_buf)
        dq_buf[...] = jnp.zeros_like(dq_buf)
        scores(0, 0)

        def two_steps(p, carry):
            t = 2 * p
            scores(t + 1, 1)
            finish(t, 0)
            scores(t + 2, 0)
            finish(t + 1, 1)
            return carry

        lax.fori_loop(0, qi // 2, two_steps, 0)

        @pl.when(qi % 2 == 1)
        def _():
            scores(qi, 1)
            finish(qi - 1, 0)
            finish(qi, 1)

        @pl.when(qi % 2 == 0)
        def _():
            finish(qi, 0)

        dq_ref[...] = (jnp.where(first, dq_buf[0], dq_buf[1]) * 0.125).astype(BF16)

        @pl.when(qi == nq - 1)
        def _():
            dk_ref[...] = dk_acc[...].astype(BF16)
            dv_ref[...] = dv_acc[...].astype(BF16)

    t = proj.shape[0]
    q_spec = pl.BlockSpec((blk, LANES), lambda b, p, i: (b * nq + i, p))
    kv_out = pl.BlockSpec((seq, LANES), lambda b, p, i: (b, p))
    out = jax.ShapeDtypeStruct((t, D_MODEL), BF16)
    return pl.pallas_call(
        body, name="attn_bwd", grid=(bsz, n_pairs, nq),
        in_specs=[q_spec,
                  pl.BlockSpec((seq, LANES), lambda b, p, i: (b, n_pairs + p)),
                  pl.BlockSpec((seq, LANES), lambda b, p, i: (b, 2 * n_pairs + p)),
                  q_spec, q_spec],
        out_specs=[q_spec, kv_out, kv_out], out_shape=[out, out, out],
        scratch_shapes=[pltpu.VMEM((seq, LANES), F32), pltpu.VMEM((seq, LANES), F32),
                        pltpu.VMEM((2, 2, blk, blk), F32), pltpu.VMEM((2, 2, blk, blk), F32),
                        pltpu.VMEM((2, 2, blk, blk), F32), pltpu.VMEM((2, 2, blk, 1), F32),
                        pltpu.VMEM((2, blk, 1), F32), pltpu.VMEM((2, blk, 1), F32),
                        pltpu.VMEM((2, blk, LANES), F32)],
        compiler_params=_params(("parallel", "parallel", "arbitrary")),
    )(proj, proj, proj, dy, rtot)


def _adamw(w, g, m, v, name):
    rows, cols = w.shape
    tr = _pick(rows, (256, 352, 128, 64, 32, 16, 8))
    c1 = 1.0 - ADAM_B1 ** ADAM_STEP
    c2 = 1.0 - ADAM_B2 ** ADAM_STEP

    def body(w_ref, g_ref, m_ref, v_ref, d_ref, nm_ref, nv_ref):
        g_v = g_ref[...]
        nm = ADAM_B1 * m_ref[...] + (1.0 - ADAM_B1) * g_v
        nv = ADAM_B2 * v_ref[...] + (1.0 - ADAM_B2) * (g_v * g_v)
        nm_ref[...] = nm
        nv_ref[...] = nv
        d_ref[...] = -ADAM_LR * ((nm / c1) / (jnp.sqrt(nv / c2) + ADAM_EPS) + ADAM_WD * w_ref[...])

    spec = pl.BlockSpec((tr, cols), lambda i: (i, 0))
    shape = jax.ShapeDtypeStruct(w.shape, F32)
    return pl.pallas_call(
        body, name=name, grid=(rows // tr,), in_specs=[spec] * 4, out_specs=[spec] * 3, out_shape=[shape] * 3,
        compiler_params=_params(("parallel",)),
    )(w, g, m, v)


def _ffn_fwd(u, w_gu, w_down, bsz, seq, tag):
    h = _matmul(u, w_gu, mode="nn", out_dtype=F32, name=f"{tag}_up")
    p = _swiglu_act(h, bsz, seq, f"{tag}_act")
    f = _matmul(p, w_down, mode="nn", out_dtype=F32, name=f"{tag}_down")
    return h, p, f


def _ffn_bwd(df, u, h, p, w_gu, w_down, bsz, seq, tag):
    dp = _matmul(df, w_down, mode="nt", out_dtype=F32, name=f"{tag}_ddown")
    dh = _swiglu_bwd(h, dp, bsz, seq, f"{tag}_dact")
    g_down = _matmul(p, df, mode="tn", out_dtype=F32, name=f"{tag}_gdown")
    g_gu = _matmul(u, dh, mode="tn", out_dtype=F32, name=f"{tag}_ggu", out_stacked=True)
    du = _matmul(dh, w_gu, mode="nt", out_dtype=F32, name=f"{tag}_dup")
    return du, g_gu, g_down


def _local_step(x, c, target, wts, vecs):
    bsz, seq, _ = x.shape
    t = bsz * seq
    x0 = x.reshape(t, D_MODEL)
    tgt = target.reshape(t, D_MODEL)

    sc = _silu_pad(c)
    mod16 = _matmul(sc, wts["w_ada"], mode="nn", out_dtype=F32, name="ada_fwd", bias=vecs["b_ada"])
    mod = mod16[:bsz].reshape(bsz, 9, D_MODEL)

    u1 = _mod_in(x0, mod, bsz, seq, 0)
    h1, p1, f1 = _ffn_fwd(u1, wts["ffn1_w_gu"], wts["ffn1_w_down"], bsz, seq, "ffn1")
    r1, x1, u2 = _res_ln_fwd(x0, f1, mod, vecs["ln1_g"], vecs["ln1_b"], bsz, seq, 0, 0.5)

    proj = _matmul(u2, wts["w_in"], mode="nn", out_dtype=F32, name="mix_in")
    ya, rtot = _attn_fwd(proj, bsz, seq)
    cs, cv = _conv_fwd(proj, wts["conv_w"], vecs["conv_b"], vecs["conv_ln_g"], vecs["conv_ln_b"], bsz, seq)
    ysb = _matmul(ya, wts["w_sb_out"], mode="nn", out_dtype=F32, name="sb_out")
    yconv = _matmul(cs, wts["w_conv_out"], mode="nn", out_dtype=F32, name="conv_out")
    merged = _merge_fwd(proj, ysb, yconv, bsz, seq)
    o2 = _matmul(merged, wts["w_out"], mode="nn", out_dtype=F32, name="mix_out")
    r2, x2, u3 = _res_ln_fwd(x1, o2, mod, vecs["ln2_g"], vecs["ln2_b"], bsz, seq, 1, 1.0)

    h3, p3, f3 = _ffn_fwd(u3, wts["ffn2_w_gu"], wts["ffn2_w_down"], bsz, seq, "ffn2")
    r3, dy, loss_blk = _res_ln_fwd(x2, f3, mod, vecs["ln3_g"], vecs["ln3_b"], bsz, seq, 2, 0.5, target=tgt)

    grads = {}
    dxres, df, ln3s, g3s = _res_ln_bwd(r3, dy, f3, mod, vecs["ln3_g"], bsz, seq, 2, 0.5)
    du, grads["ffn2_w_gu"], grads["ffn2_w_down"] = _ffn_bwd(
        df, u3, h3, p3, wts["ffn2_w_gu"], wts["ffn2_w_down"], bsz, seq, "ffn2")
    dx2, m3s = _mod_bwd(dxres, du, x2, mod, bsz, seq, 2)

    dxres, do2, ln2s, g2s = _res_ln_bwd(r2, dx2, o2, mod, vecs["ln2_g"], bsz, seq, 1, 1.0)
    dmerged = _matmul(do2, wts["w_out"], mode="nt", out_dtype=F32, name="mix_out_d")
    grads["w_out"] = _matmul(merged, do2, mode="tn", out_dtype=F32, name="mix_out_g")
    dysb, dyconv, dgate = _merge_bwd(proj, ysb, yconv, dmerged, bsz, seq)
    dya = _matmul(dysb, wts["w_sb_out"], mode="nt", out_dtype=BF16, name="sb_out_d")
    grads["w_sb_out"] = _matmul(ya, dysb, mode="tn", out_dtype=F32, name="sb_out_g")
    dcs = _matmul(dyconv, wts["w_conv_out"], mode="nt", out_dtype=F32, name="conv_out_d")
    grads["w_conv_out"] = _matmul(cs, dyconv, mode="tn", out_dtype=F32, name="conv_out_g")
    dcv, convs = _conv_bwd_ln(dcs, cv, vecs["conv_ln_g"], vecs["conv_ln_b"], bsz, seq)
    dglu, g_conv_w = _conv_bwd_taps(proj, dcv, wts["conv_w"], bsz, seq)
    dq, dk, dv = _attn_bwd(proj, rtot, dya, bsz, seq)
    dproj = jnp.concatenate([dq, dk, dv, dglu, dgate], axis=1)
    grads["w_in"] = _matmul(u2, dproj, mode="tn", out_dtype=F32, name="mix_in_g", out_stacked=True)
    du = _matmul(dproj, wts["w_in"], mode="nt", out_dtype=F32, name="mix_in_d")
    dx1, m2s = _mod_bwd(dxres, du, x1, mod, bsz, seq, 1)

    dxres, df, ln1s, g1s = _res_ln_bwd(r1, dx1, f1, mod, vecs["ln1_g"], bsz, seq, 0, 0.5)
    du, grads["ffn1_w_gu"], grads["ffn1_w_down"] = _ffn_bwd(
        df, u1, h1, p1, wts["ffn1_w_gu"], wts["ffn1_w_down"], bsz, seq, "ffn1")
    grad_x, m1s = _mod_bwd(dxres, du, x0, mod, bsz, seq, 0)

    dmod = jnp.stack([m1s[:, 0], m1s[:, 1], g1s[:, 0], m2s[:, 0], m2s[:, 1], g2s[:, 0],
                      m3s[:, 0], m3s[:, 1], g3s[:, 0]], axis=1)
    dmod16 = jnp.zeros((16, 9 * D_MODEL), F32).at[:bsz].set(dmod.reshape(bsz, 9 * D_MODEL))
    grads["w_ada"] = _matmul(sc, dmod16.astype(BF16), mode="tn", out_dtype=F32, name="ada_g", out_stacked=True)

    small = {"dmod": dmod, "ln1": ln1s, "ln2": ln2s, "ln3": ln3s, "conv": convs, "conv_w": g_conv_w,
             "loss": loss_blk}
    return grad_x.reshape(x.shape), grads, small


_HBM = pl.BlockSpec(memory_space=pltpu.HBM)


def _position():
    return lax.axis_index("x"), lax.axis_index("y"), lax.axis_index("c")


def _other_chips(x, y):
    return [(1 - x, y), (x, 1 - y), (1 - x, 1 - y)]


def _cast_into_stack(w_local, chip, name):
    rows, cols = w_local.shape
    tr = _pick(rows, (256, 352, 128, 64, 32, 16))

    def body(chip_ref, w_ref, o_ref):
        o_ref[...] = w_ref[...].astype(BF16)

    return pl.pallas_call(
        body, name=name,
        grid_spec=pltpu.PrefetchScalarGridSpec(
            num_scalar_prefetch=1, grid=(rows // tr,),
            in_specs=[pl.BlockSpec((tr, cols), lambda r, chip_ref: (r, 0))],
            out_specs=pl.BlockSpec((None, tr, cols), lambda r, chip_ref: (chip_ref[0], r, 0))),
        out_shape=jax.ShapeDtypeStruct((N_CHIPS, rows, cols), BF16),
        compiler_params=_params(("parallel",)),
    )(chip, w_local)


def _all_gather_weights(stacks, small):
    n = len(stacks)

    def body(*refs):
        ins, small_in, outs, small_out = refs[:n], refs[n], refs[n + 1:2 * n + 1], refs[2 * n + 1]
        send_sems, recv_sems, fwd_send_sems, fwd_recv_sems, small_sems = refs[2 * n + 2:]
        x, y, c = _position()
        me = 2 * x + y
        chips = _other_chips(x, y)

        def send(i, j):
            px, py = chips[j]
            return pltpu.make_async_remote_copy(
                src_ref=ins[i].at[me, c], dst_ref=outs[i].at[me, c], send_sem=send_sems.at[3 * i + j],
                recv_sem=recv_sems.at[3 * i + j], device_id=(px, py, c), device_id_type=MESH)

        def landed(i, j):
            px, py = chips[j]
            return pltpu.make_async_remote_copy(
                src_ref=ins[i].at[me, c], dst_ref=outs[i].at[2 * px + py, c], send_sem=send_sems.at[3 * i + j],
                recv_sem=recv_sems.at[3 * i + j], device_id=(px, py, c), device_id_type=MESH)

        def forward(i, j, half):
            px, py = chips[j]
            blk = outs[i].at[2 * px + py, half]
            return pltpu.make_async_remote_copy(
                src_ref=blk, dst_ref=blk, send_sem=fwd_send_sems.at[3 * i + j],
                recv_sem=fwd_recv_sems.at[3 * i + j], device_id=(x, y, 1 - c), device_id_type=MESH)

        def small_copy(j, slot):
            px, py = chips[j]
            return pltpu.make_async_remote_copy(
                src_ref=small_in, dst_ref=small_out.at[slot], send_sem=small_sems.at[j],
                recv_sem=small_sems.at[3 + j], device_id=(px, py, c), device_id_type=MESH)

        own_small = pltpu.make_async_copy(small_in, small_out.at[me], small_sems.at[6])
        own_small.start()
        for j in range(3):
            small_copy(j, me).start()
        for i in range(n):
            for j in range(3):
                send(i, j).start()
        for i in range(n):
            for j in range(3):
                landed(i, j).wait_recv()
                forward(i, j, c).start()
        for i in range(n):
            for j in range(3):
                forward(i, j, 1 - c).wait_recv()
        for j, (px, py) in enumerate(chips):
            small_copy(j, 2 * px + py).wait_recv()
        own_small.wait()
        for j in range(3):
            small_copy(j, me).wait_send()
        for i in range(n):
            for j in range(3):
                send(i, j).wait_send()
                forward(i, j, c).wait_send()

    return pl.pallas_call(
        body, name="all_gather_weights",
        out_shape=[jax.ShapeDtypeStruct(s.shape, s.dtype) for s in stacks]
        + [jax.ShapeDtypeStruct((N_CHIPS,) + small.shape, small.dtype)],
        in_specs=[_HBM] * (n + 1), out_specs=[_HBM] * (n + 1),
        input_output_aliases={i: i for i in range(n)},
        scratch_shapes=[pltpu.SemaphoreType.DMA((3 * n,)), pltpu.SemaphoreType.DMA((3 * n,)),
                        pltpu.SemaphoreType.DMA((3 * n,)), pltpu.SemaphoreType.DMA((3 * n,)),
                        pltpu.SemaphoreType.DMA((7,))],
    )(*stacks, small)


def _pair_swap(grads):
    n = len(grads)

    def body(*refs):
        ins, outs = refs[:n], refs[n:2 * n]
        send_sems, recv_sems = refs[2 * n:]
        x, y, c = _position()

        def copy(i, k):
            return pltpu.make_async_remote_copy(
                src_ref=ins[i].at[k, 1 - c], dst_ref=outs[i].at[k], send_sem=send_sems.at[N_CHIPS * i + k],
                recv_sem=recv_sems.at[N_CHIPS * i + k], device_id=(x, y, 1 - c), device_id_type=MESH)

        for i in range(n):
            for k in range(N_CHIPS):
                copy(i, k).start()
        for i in range(n):
            for k in range(N_CHIPS):
                copy(i, k).wait_recv()
        for i in range(n):
            for k in range(N_CHIPS):
                copy(i, k).wait_send()

    return pl.pallas_call(
        body, name="grad_pair_swap",
        out_shape=[jax.ShapeDtypeStruct((N_CHIPS,) + g.shape[2:], F32) for g in grads],
        in_specs=[_HBM] * n, out_specs=[_HBM] * n,
        scratch_shapes=[pltpu.SemaphoreType.DMA((N_CHIPS * n,)), pltpu.SemaphoreType.DMA((N_CHIPS * n,))],
    )(*grads)


def _pair_add(g, got, place, name):
    _, _, rh, cols = g.shape
    tr = _pick(rh, (256, 176, 128, 64, 32, 16, 8))

    def body(place_ref, g_ref, got_ref, p_ref, own_ref):
        s = g_ref[...] + got_ref[...]
        p_ref[...] = s.astype(BF16)

        @pl.when(pl.program_id(1) == place_ref[1])
        def _():
            own_ref[...] = s

    blk = pl.BlockSpec((None, tr, cols), lambda r, k, place_ref: (k, r, 0))
    return pl.pallas_call(
        body, name=name,
        grid_spec=pltpu.PrefetchScalarGridSpec(
            num_scalar_prefetch=1, grid=(rh // tr, N_CHIPS),
            in_specs=[pl.BlockSpec((None, None, tr, cols), lambda r, k, place_ref: (k, place_ref[0], r, 0)), blk],
            out_specs=[blk, pl.BlockSpec((tr, cols), lambda r, k, place_ref: (r, 0))]),
        out_shape=[jax.ShapeDtypeStruct((N_CHIPS, rh, cols), BF16), jax.ShapeDtypeStruct((rh, cols), F32)],
        compiler_params=_params(("parallel", "arbitrary")),
    )(place, g, got)


def _chip_scatter(sums):
    n = len(sums)

    def body(*refs):
        ins, outs = refs[:n], refs[n:2 * n]
        send_sems, recv_sems = refs[2 * n:]
        x, y, c = _position()
        chips = _other_chips(x, y)

        def copy(i, j):
            px, py = chips[j]
            return pltpu.make_async_remote_copy(
                src_ref=ins[i].at[2 * px + py], dst_ref=outs[i].at[j], send_sem=send_sems.at[3 * i + j],
                recv_sem=recv_sems.at[3 * i + j], device_id=(px, py, c), device_id_type=MESH)

        for i in range(n):
            for j in range(3):
                copy(i, j).start()
        for i in range(n):
            for j in range(3):
                copy(i, j).wait_recv()
        for i in range(n):
            for j in range(3):
                copy(i, j).wait_send()

    return pl.pallas_call(
        body, name="grad_chip_scatter",
        out_shape=[jax.ShapeDtypeStruct((3,) + s.shape[1:], BF16) for s in sums],
        in_specs=[_HBM] * n, out_specs=[_HBM] * n,
        scratch_shapes=[pltpu.SemaphoreType.DMA((3 * n,)), pltpu.SemaphoreType.DMA((3 * n,))],
    )(*sums)


def _chip_sum(own, parts, place, name):
    rh, cols = own.shape
    tr = _pick(rh, (256, 176, 128, 64, 32, 16, 8))

    def body(place_ref, own_ref, p_ref, o_ref):
        o_ref[...] = ((own_ref[...] + p_ref[0].astype(F32)) + p_ref[1].astype(F32)) + p_ref[2].astype(F32)

    return pl.pallas_call(
        body, name=name,
        grid_spec=pltpu.PrefetchScalarGridSpec(
            num_scalar_prefetch=1, grid=(rh // tr,),
            in_specs=[pl.BlockSpec((tr, cols), lambda r, place_ref: (r, 0)),
                      pl.BlockSpec((3, tr, cols), lambda r, place_ref: (0, r, 0))],
            out_specs=pl.BlockSpec((None, tr, cols), lambda r, place_ref: (place_ref[0], r, 0))),
        out_shape=jax.ShapeDtypeStruct((2, rh, cols), F32),
        compiler_params=_params(("parallel",)),
    )(place, own, parts)


def _pair_gather(halves):
    n = len(halves)

    def body(*refs):
        ins, outs = refs[:n], refs[n:2 * n]
        send_sems, recv_sems = refs[2 * n:]
        x, y, c = _position()

        def send(i):
            return pltpu.make_async_remote_copy(
                src_ref=ins[i].at[c], dst_ref=outs[i].at[c], send_sem=send_sems.at[i], recv_sem=recv_sems.at[i],
                device_id=(x, y, 1 - c), device_id_type=MESH)

        def landed(i):
            return pltpu.make_async_remote_copy(
                src_ref=ins[i].at[c], dst_ref=outs[i].at[1 - c], send_sem=send_sems.at[i], recv_sem=recv_sems.at[i],
                device_id=(x, y, 1 - c), device_id_type=MESH)

        for i in range(n):
            send(i).start()
        for i in range(n):
            landed(i).wait_recv()
        for i in range(n):
            send(i).wait_send()

    return pl.pallas_call(
        body, name="grad_pair_gather",
        out_shape=[jax.ShapeDtypeStruct(h.shape, F32) for h in halves],
        in_specs=[_HBM] * n, out_specs=[_HBM] * n,
        input_output_aliases={i: i for i in range(n)},
        scratch_shapes=[pltpu.SemaphoreType.DMA((n,)), pltpu.SemaphoreType.DMA((n,))],
    )(*halves)


_MOD_ROWS = 16


def _small_all_reduce(buf, bsz):
    rows, cols = buf.shape
    head = bsz * _MOD_ROWS
    out_rows = rows - head + _MOD_ROWS

    def body(in_ref, o_ref, gath, send_sems, recv_sems):
        x, y, c = _position()
        me = 4 * x + 2 * y + c

        def peer(mask):
            return (x ^ (mask >> 2), y ^ ((mask >> 1) & 1), c ^ (mask & 1))

        def copy(mask):
            return pltpu.make_async_remote_copy(
                src_ref=in_ref, dst_ref=gath.at[me], send_sem=send_sems.at[mask - 1],
                recv_sem=recv_sems.at[mask - 1], device_id=peer(mask), device_id_type=MESH)

        def arrival(mask):
            px, py, pc = peer(mask)
            return pltpu.make_async_remote_copy(
                src_ref=in_ref, dst_ref=gath.at[4 * px + 2 * py + pc], send_sem=send_sems.at[mask - 1],
                recv_sem=recv_sems.at[mask - 1], device_id=peer(mask), device_id_type=MESH)

        for mask in range(1, N_DEV):
            copy(mask).start()
        gath[me] = in_ref[...]
        for mask in range(1, N_DEV):
            arrival(mask).wait_recv()
        for mask in range(1, N_DEV):
            copy(mask).wait_send()
        acc = gath[0]
        for d in range(1, N_DEV):
            acc = acc + gath[d]
        mod = acc[0:_MOD_ROWS]
        for s in range(1, bsz):
            mod = mod + acc[s * _MOD_ROWS:(s + 1) * _MOD_ROWS]
        o_ref[0:_MOD_ROWS, :] = mod
        o_ref[_MOD_ROWS:, :] = acc[head:]

    vm = pl.BlockSpec(memory_space=pltpu.VMEM)
    return pl.pallas_call(
        body, name="small_all_reduce", in_specs=[vm], out_specs=vm,
        out_shape=jax.ShapeDtypeStruct((out_rows, cols), F32),
        scratch_shapes=[pltpu.VMEM((N_DEV, rows, cols), F32), pltpu.SemaphoreType.DMA((N_DEV - 1,)),
                        pltpu.SemaphoreType.DMA((N_DEV - 1,))],
        compiler_params=pltpu.CompilerParams(vmem_limit_bytes=VMEM_LIMIT),
    )(buf)


_COL_SHARDED = ("w_ada", "ffn1_w_gu", "w_in", "ffn2_w_gu")
_ROW_SHARDED = ("ffn1_w_down", "w_sb_out", "w_conv_out", "w_out", "ffn2_w_down")
_BIG = ("w_ada", "ffn1_w_gu", "ffn1_w_down", "w_in", "w_sb_out", "w_conv_out", "w_out", "ffn2_w_gu", "ffn2_w_down")
_VECS = ("b_ada", "ln1_g", "ln1_b", "conv_b", "conv_ln_g", "conv_ln_b", "ln2_g", "ln2_b", "ln3_g", "ln3_b")
_WEIGHTS = ("w_ada", "b_ada", "ffn1_w_gu", "ffn1_w_down", "ln1_g", "ln1_b", "w_in", "w_sb_out", "conv_w", "conv_b",
            "conv_ln_g", "conv_ln_b", "w_conv_out", "w_out", "ln2_g", "ln2_b", "ffn2_w_gu", "ffn2_w_down",
            "ln3_g", "ln3_b")


def _step(x, c, target, w, m, v):
    bsz = x.shape[0]
    chip = 2 * lax.axis_index("x") + lax.axis_index("y")
    core = lax.axis_index("c")

    chip_arr = jnp.reshape(chip, (1,)).astype(jnp.int32)
    place = jnp.stack([core, chip]).astype(jnp.int32)

    conv_w_local = jnp.pad(w["conv_w"][0], ((0, 1), (0, 0)))
    stacks = []
    for n in _BIG:
        rows, cols = w[n].shape[1:]
        s = _cast_into_stack(w[n][0], chip_arr, f"cast_{n}")
        stacks.append(s.reshape(N_CHIPS, 2, rows // 2, cols))
    gathered = _all_gather_weights(stacks, conv_w_local)
    wts = {}
    for n, g in zip(_BIG, gathered[:-1]):
        rows, cols = w[n].shape[1:]
        wts[n] = g.reshape(N_CHIPS, rows, cols) if n in _COL_SHARDED else g.reshape(N_CHIPS * rows, cols)
    wts["conv_w"] = gathered[-1].transpose(1, 0, 2).reshape(32, D_MODEL)
    vecs = {n: w[n] for n in _VECS}

    grad_x, grads, small = _local_step(x, c, target, wts, vecs)

    views = []
    for n in _BIG:
        g = grads[n]
        rows, cols = w[n].shape[1:]
        views.append(g.reshape(N_CHIPS, 2, rows // 2, cols))
    got = _pair_swap(views)
    pair_sums = [_pair_add(g, r, place, f"pair_add_{n}") for n, g, r in zip(_BIG, views, got)]
    parts = _chip_scatter([p for p, _ in pair_sums])
    halves = [_chip_sum(own, p, place, f"chip_sum_{n}") for n, (_, own), p in zip(_BIG, pair_sums, parts)]
    full = _pair_gather(halves)
    g_out = {n: f.reshape(w[n].shape[1:]) for n, f in zip(_BIG, full)}

    dmod = jnp.pad(small["dmod"], ((0, 0), (0, _MOD_ROWS - 9), (0, 0))).reshape(bsz * _MOD_ROWS, D_MODEL)
    loss_rows = jnp.pad(small["loss"], ((0, 0), (0, D_MODEL - LANES)))
    buf = jnp.concatenate([dmod, small["ln1"], small["ln2"], small["ln3"], small["conv"], small["conv_w"],
                           loss_rows], axis=0)
    red = _small_all_reduce(buf, bsz)
    o = _MOD_ROWS
    g_out["b_ada"] = red[0:9].reshape(1, 9 * D_MODEL)
    g_out["ln1_g"], g_out["ln1_b"] = red[o:o + 1], red[o + 1:o + 2]
    g_out["ln2_g"], g_out["ln2_b"] = red[o + 8:o + 9], red[o + 9:o + 10]
    g_out["ln3_g"], g_out["ln3_b"] = red[o + 16:o + 17], red[o + 17:o + 18]
    g_out["conv_ln_g"], g_out["conv_ln_b"], g_out["conv_b"] = red[o + 24:o + 25], red[o + 25:o + 26], red[o + 26:o + 27]
    cw = w["conv_w"].shape[2]
    g_out["conv_w"] = lax.dynamic_slice(red[o + 32:o + 32 + CONV_TAPS], (0, chip * cw), (CONV_TAPS, cw))
    loss = red[o + 64, 0]

    outs_g, outs_d, outs_m, outs_v = [], [], [], []
    for n in _WEIGHTS:
        shape = w[n].shape
        flat = shape[1:] if len(shape) == 3 else shape
        d, nm, nv = _adamw(w[n].reshape(flat), g_out[n].reshape(flat), m[n].reshape(flat), v[n].reshape(flat),
                           f"adamw_{n}")
        outs_g.append(g_out[n].reshape(shape))
        outs_d.append(d.reshape(shape))
        outs_m.append(nm.reshape(shape))
        outs_v.append(nv.reshape(shape))
    return (loss, grad_x, *outs_g, *outs_d, *outs_m, *outs_v)


def kernel(x, c, w_ada, b_ada, ffn1_w_gu, ffn1_w_down, ln1_g, ln1_b, w_in, w_sb_out, conv_w, conv_b, conv_ln_g, conv_ln_b, w_conv_out, w_out, ln2_g, ln2_b, ffn2_w_gu, ffn2_w_down, ln3_g, ln3_b, loss_target, m_w_ada, m_b_ada, m_ffn1_w_gu, m_ffn1_w_down, m_ln1_g, m_ln1_b, m_w_in, m_w_sb_out, m_conv_w, m_conv_b, m_conv_ln_g, m_conv_ln_b, m_w_conv_out, m_w_out, m_ln2_g, m_ln2_b, m_ffn2_w_gu, m_ffn2_w_down, m_ln3_g, m_ln3_b, v_w_ada, v_b_ada, v_ffn1_w_gu, v_ffn1_w_down, v_ln1_g, v_ln1_b, v_w_in, v_w_sb_out, v_conv_w, v_conv_b, v_conv_ln_g, v_conv_ln_b, v_w_conv_out, v_w_out, v_ln2_g, v_ln2_b, v_ffn2_w_gu, v_ffn2_w_down, v_ln3_g, v_ln3_b):
    given = dict(locals())
    w = {n: given[n] for n in _WEIGHTS}
    m = {n: given["m_" + n] for n in _WEIGHTS}
    v = {n: given["v_" + n] for n in _WEIGHTS}
    return _step(x, c, loss_target, w, m, v)
```

```python
import functools

import jax
import jax.numpy as jnp
from jax import lax
from jax.experimental import pallas as pl
from jax.experimental.pallas import tpu as pltpu

F32 = jnp.float32
BF16 = jnp.bfloat16

D_MODEL = 1024
D_FF = 2816
HEADS = 16
HEAD_DIM = 64
LANES = 128
CONV_TAPS = 31
HALO = 32
N_CHIPS = 4
N_DEV = 8
ALPHA = 2.0 ** 0.25
LN_EPS = 1e-5
ATT_BLOCK = 256
VMEM_LIMIT = 56 * 1024 * 1024

ADAM_LR = 0.001
ADAM_B1 = 0.9
ADAM_B2 = 0.999
ADAM_EPS = 1e-08
ADAM_WD = 0.01
ADAM_STEP = 10

MESH = pl.DeviceIdType.MESH


def _pick(n, cands):
    for t in cands:
        if t <= n and n % t == 0:
            return t
    return n


def _params(sem):
    return pltpu.CompilerParams(dimension_semantics=sem, vmem_limit_bytes=VMEM_LIMIT)


def _sigmoid(z):
    t = jnp.exp(-jnp.abs(z))
    return jnp.where(z >= 0, 1.0, t) / (1.0 + t)


def _silu(z):
    return z * _sigmoid(z)


def _dsilu(z):
    s = _sigmoid(z)
    return s * (1.0 + z * (1.0 - s))


def _ln_stats(r):
    mu = jnp.mean(r, axis=-1, keepdims=True)
    d = r - mu
    var = jnp.mean(d * d, axis=-1, keepdims=True)
    rstd = lax.rsqrt(var + LN_EPS)
    return d * rstd, rstd


def _colsum(v):
    return jnp.sum(v, axis=0, keepdims=True)


_DIMS = {"nn": (((1,), (0,)), ((), ())), "nt": (((1,), (1,)), ((), ())), "tn": (((0,), (0,)), ((), ()))}
_TN_CANDS = (1408, 1792, 1152, 1024, 512, 256, 128)
_TK_CANDS = (1024, 1408, 896, 512, 256, 128)


def _matmul(a, b, *, mode, out_dtype, name, bias=None, out_stacked=False):
    b_stacked = b.ndim == 3
    if mode == "nn":
        m, k = a.shape
        n_c = b.shape[-1]
        n = n_c * (N_CHIPS if b_stacked else 1)
        k_c = k
    elif mode == "nt":
        m, k = a.shape
        n = b.shape[-2]
        k_c = b.shape[-1]
        n_c = n
    else:
        k, m = a.shape
        n = b.shape[1]
        n_c = n // N_CHIPS if out_stacked else n
        k_c = k
    if mode == "tn":
        tm = _pick(m, (1024, 1408, 512, 256, 128))
        tk = _pick(k, (512, 256, 128, 64, 32, 16))
    else:
        tm = _pick(m, (1024, 512, 256, 128, 64, 32, 16))
        tk = _pick(k_c, _TK_CANDS)
    tn = _pick(n_c, _TN_CANDS)
    nb = n_c // tn
    kb = k_c // tk
    nk = k // tk
    grid = (m // tm, n // tn, nk)

    if mode == "nn":
        a_spec = pl.BlockSpec((tm, tk), lambda i, j, kk: (i, kk))
        if b_stacked:
            b_spec = pl.BlockSpec((None, tk, tn), lambda i, j, kk: (j // nb, kk, j % nb))
        else:
            b_spec = pl.BlockSpec((tk, tn), lambda i, j, kk: (kk, j))
    elif mode == "nt":
        a_spec = pl.BlockSpec((tm, tk), lambda i, j, kk: (i, kk))
        if b_stacked:
            b_spec = pl.BlockSpec((None, tn, tk), lambda i, j, kk: (kk // kb, j, kk % kb))
        else:
            b_spec = pl.BlockSpec((tn, tk), lambda i, j, kk: (j, kk))
    else:
        a_spec = pl.BlockSpec((tk, tm), lambda i, j, kk: (kk, i))
        b_spec = pl.BlockSpec((tk, tn), lambda i, j, kk: (kk, j))
    if out_stacked:
        out_shape = jax.ShapeDtypeStruct((N_CHIPS, m, n_c), out_dtype)
        o_spec = pl.BlockSpec((None, tm, tn), lambda i, j, kk: (j // nb, i, j % nb))
    else:
        out_shape = jax.ShapeDtypeStruct((m, n), out_dtype)
        o_spec = pl.BlockSpec((tm, tn), lambda i, j, kk: (i, j))
    in_specs = [a_spec, b_spec]
    args = [a, b]
    if bias is not None:
        in_specs.append(pl.BlockSpec((1, tn), lambda i, j, kk: (0, j)))
        args.append(bias)
    dims = _DIMS[mode]

    def body(*refs):
        a_ref, b_ref = refs[0], refs[1]
        bias_ref = refs[2] if bias is not None else None
        o_ref, acc_ref = refs[-2], refs[-1]
        kk = pl.program_id(2)

        @pl.when(kk == 0)
        def _():
            acc_ref[...] = jnp.zeros_like(acc_ref)

        acc_ref[...] += lax.dot_general(a_ref[...], b_ref[...], dims, preferred_element_type=F32)

        @pl.when(kk == nk - 1)
        def _():
            r = acc_ref[...]
            if bias_ref is not None:
                r = r + bias_ref[...]
            o_ref[...] = r.astype(o_ref.dtype)

    return pl.pallas_call(
        body, name=name, grid=grid, in_specs=in_specs, out_specs=o_spec, out_shape=out_shape,
        scratch_shapes=[pltpu.VMEM((tm, tn), F32)],
        compiler_params=_params(("parallel", "parallel", "arbitrary")),
    )(*args)


def _row_grid(bsz, seq, ts):
    ns = seq // ts
    return (bsz, ns), ns


def _rows(ts, width, ns, col=0):
    return pl.BlockSpec((ts, width), lambda b, s: (b * ns + s, col))


def _mod_spec():
    return pl.BlockSpec((None, 9, D_MODEL), lambda b, s: (b, 0, 0))


def _vec_spec(rows=1, width=D_MODEL):
    return pl.BlockSpec((rows, width), lambda b, s: (0, 0))


def _silu_pad(c):
    bsz = c.shape[0]

    def body(c_ref, o_ref):
        o_ref[...] = jnp.zeros_like(o_ref)
        o_ref[0:bsz, :] = _silu(c_ref[...]).astype(BF16)

    return pl.pallas_call(body, name="silu_pad", out_shape=jax.ShapeDtypeStruct((16, D_MODEL), BF16))(c)


def _mod_in(x, mod, bsz, seq, sub):
    ts = _pick(seq, (512, 256, 128))
    grid, ns = _row_grid(bsz, seq, ts)

    def body(x_ref, mod_ref, u_ref):
        sh = mod_ref[3 * sub:3 * sub + 1, :]
        sc = mod_ref[3 * sub + 1:3 * sub + 2, :]
        u_ref[...] = (x_ref[...] * (1.0 + sc) + sh).astype(BF16)

    return pl.pallas_call(
        body, name=f"mod_in{sub}", grid=grid, in_specs=[_rows(ts, D_MODEL, ns), _mod_spec()],
        out_specs=_rows(ts, D_MODEL, ns), out_shape=jax.ShapeDtypeStruct(x.shape, BF16),
        compiler_params=_params(("parallel", "parallel")),
    )(x, mod)


def _swiglu_act(h, bsz, seq, name):
    ts = _pick(seq, (256, 128))
    grid, ns = _row_grid(bsz, seq, ts)

    def body(h_ref, p_ref):
        a = h_ref[:, :D_FF]
        g = h_ref[:, D_FF:]
        p_ref[...] = (_silu(a) * g).astype(BF16)

    return pl.pallas_call(
        body, name=name, grid=grid, in_specs=[_rows(ts, 2 * D_FF, ns)],
        out_specs=_rows(ts, D_FF, ns), out_shape=jax.ShapeDtypeStruct((h.shape[0], D_FF), BF16),
        compiler_params=_params(("parallel", "parallel")),
    )(h)


def _swiglu_bwd(h, dp, bsz, seq, name):
    ts = _pick(seq, (256, 128))
    grid, ns = _row_grid(bsz, seq, ts)

    def body(h_ref, dp_ref, dh_ref):
        a = h_ref[:, :D_FF]
        g = h_ref[:, D_FF:]
        dp_v = dp_ref[...]
        dh_ref[:, :D_FF] = (dp_v * g * _dsilu(a)).astype(BF16)
        dh_ref[:, D_FF:] = (dp_v * _silu(a)).astype(BF16)

    return pl.pallas_call(
        body, name=name, grid=grid, in_specs=[_rows(ts, 2 * D_FF, ns), _rows(ts, D_FF, ns)],
        out_specs=_rows(ts, 2 * D_FF, ns), out_shape=jax.ShapeDtypeStruct(h.shape, BF16),
        compiler_params=_params(("parallel", "parallel")),
    )(h, dp)


def _res_ln_fwd(x, f, mod, ln_g, ln_b, bsz, seq, sub, weight, target=None):
    ts = _pick(seq, (256, 128))
    grid, ns = _row_grid(bsz, seq, ts)
    last = target is not None

    def body(*refs):
        x_ref, f_ref, mod_ref, g_ref, b_ref = refs[:5]
        gate = mod_ref[3 * sub + 2:3 * sub + 3, :]
        r = ALPHA * x_ref[...] + gate * (weight * f_ref[...])
        xhat, _ = _ln_stats(r)
        xo = xhat * g_ref[...] + b_ref[...]
        if last:
            t_ref, r_ref, dy_ref, loss_ref = refs[5:]
            diff = xo - t_ref[...]
            dy_ref[...] = diff * (1.0 / D_MODEL)
            part = 0.5 * jnp.sum(jnp.mean(diff * diff, axis=-1, keepdims=True), axis=0, keepdims=True)

            @pl.when((pl.program_id(0) == 0) & (pl.program_id(1) == 0))
            def _():
                loss_ref[...] = jnp.zeros_like(loss_ref)

            loss_ref[...] += jnp.broadcast_to(part, loss_ref.shape)
        else:
            r_ref, xo_ref, u_ref = refs[5:]
            xo_ref[...] = xo
            sh = mod_ref[3 * sub + 3:3 * sub + 4, :]
            sc = mod_ref[3 * sub + 4:3 * sub + 5, :]
            u_ref[...] = (xo * (1.0 + sc) + sh).astype(BF16)
        r_ref[...] = r

    row = _rows(ts, D_MODEL, ns)
    in_specs = [row, row, _mod_spec(), _vec_spec(), _vec_spec()]
    args = [x, f, mod, ln_g, ln_b]
    if last:
        in_specs.append(row)
        args.append(target)
        out_specs = [row, row, _vec_spec(8, LANES)]
        out_shape = [jax.ShapeDtypeStruct(x.shape, F32), jax.ShapeDtypeStruct(x.shape, F32),
                     jax.ShapeDtypeStruct((8, LANES), F32)]
        sem = ("arbitrary", "arbitrary")
    else:
        out_specs = [row, row, row]
        out_shape = [jax.ShapeDtypeStruct(x.shape, F32), jax.ShapeDtypeStruct(x.shape, F32),
                     jax.ShapeDtypeStruct(x.shape, BF16)]
        sem = ("parallel", "parallel")
    return pl.pallas_call(
        body, name=f"res_ln_fwd{sub}", grid=grid, in_specs=in_specs, out_specs=out_specs, out_shape=out_shape,
        compiler_params=_params(sem),
    )(*args)


def _res_ln_bwd(r, dxo, f, mod, ln_g, bsz, seq, sub, weight):
    ts = _pick(seq, (256, 128))
    grid, ns = _row_grid(bsz, seq, ts)

    def body(r_ref, dxo_ref, f_ref, mod_ref, g_ref, dxres_ref, df_ref, lns_ref, gs_ref):
        b, s = pl.program_id(0), pl.program_id(1)
        gate = mod_ref[3 * sub + 2:3 * sub + 3, :]
        xhat, rstd = _ln_stats(r_ref[...])
        dxo_v = dxo_ref[...]
        dxhat = dxo_v * g_ref[...]
        m1 = jnp.mean(dxhat, axis=-1, keepdims=True)
        m2 = jnp.mean(dxhat * xhat, axis=-1, keepdims=True)
        dr = rstd * (dxhat - m1 - xhat * m2)
        dxres_ref[...] = ALPHA * dr
        df_ref[...] = (dr * (gate * weight)).astype(BF16)

        @pl.when((b == 0) & (s == 0))
        def _():
            lns_ref[...] = jnp.zeros_like(lns_ref)

        @pl.when(s == 0)
        def _():
            gs_ref[...] = jnp.zeros_like(gs_ref)

        lns_ref[0:1, :] += _colsum(dxo_v * xhat)
        lns_ref[1:2, :] += _colsum(dxo_v)
        gs_ref[0:1, :] += _colsum(dr * (weight * f_ref[...]))

    row = _rows(ts, D_MODEL, ns)
    return pl.pallas_call(
        body, name=f"res_ln_bwd{sub}", grid=grid,
        in_specs=[row, row, row, _mod_spec(), _vec_spec()],
        out_specs=[row, row, _vec_spec(8), pl.BlockSpec((None, 8, D_MODEL), lambda b, s: (b, 0, 0))],
        out_shape=[jax.ShapeDtypeStruct(r.shape, F32), jax.ShapeDtypeStruct(r.shape, BF16),
                   jax.ShapeDtypeStruct((8, D_MODEL), F32), jax.ShapeDtypeStruct((bsz, 8, D_MODEL), F32)],
        compiler_params=_params(("arbitrary", "arbitrary")),
    )(r, dxo, f, mod, ln_g)


def _mod_bwd(dxres, du, x, mod, bsz, seq, sub):
    ts = _pick(seq, (256, 128))
    grid, ns = _row_grid(bsz, seq, ts)

    def body(dxres_ref, du_ref, x_ref, mod_ref, dx_ref, st_ref):
        s = pl.program_id(1)
        sc = mod_ref[3 * sub + 1:3 * sub + 2, :]
        du_v = du_ref[...]
        dx_ref[...] = dxres_ref[...] + du_v * (1.0 + sc)

        @pl.when(s == 0)
        def _():
            st_ref[...] = jnp.zeros_like(st_ref)

        st_ref[0:1, :] += _colsum(du_v)
        st_ref[1:2, :] += _colsum(du_v * x_ref[...])

    row = _rows(ts, D_MODEL, ns)
    return pl.pallas_call(
        body, name=f"mod_bwd{sub}", grid=grid, in_specs=[row, row, row, _mod_spec()],
        out_specs=[row, pl.BlockSpec((None, 8, D_MODEL), lambda b, s: (b, 0, 0))],
        out_shape=[jax.ShapeDtypeStruct(x.shape, F32), jax.ShapeDtypeStruct((bsz, 8, D_MODEL), F32)],
        compiler_params=_params(("parallel", "arbitrary")),
    )(dxres, du, x, mod)


_COL_GLU_A, _COL_GLU_B, _COL_GATE_A, _COL_GATE_B = 3, 4, 5, 6


def _merge_fwd(proj, ysb, yconv, bsz, seq):
    ts = _pick(seq, (512, 256, 128))
    grid, ns = _row_grid(bsz, seq, ts)

    def body(ga_ref, gb_ref, ysb_ref, yc_ref, o_ref):
        o_ref[...] = (_sigmoid(ga_ref[...]) * ysb_ref[...] + _sigmoid(gb_ref[...]) * yc_ref[...]).astype(BF16)

    row = _rows(ts, D_MODEL, ns)
    return pl.pallas_call(
        body, name="merge_fwd", grid=grid,
        in_specs=[_rows(ts, D_MODEL, ns, _COL_GATE_A), _rows(ts, D_MODEL, ns, _COL_GATE_B), row, row],
        out_specs=row, out_shape=jax.ShapeDtypeStruct(ysb.shape, BF16),
        compiler_params=_params(("parallel", "parallel")),
    )(proj, proj, ysb, yconv)


def _merge_bwd(proj, ysb, yconv, dmerged, bsz, seq):
    ts = _pick(seq, (256, 128))
    grid, ns = _row_grid(bsz, seq, ts)

    def body(ga_ref, gb_ref, ysb_ref, yc_ref, dm_ref, dysb_ref, dyc_ref, dg_ref):
        sa = _sigmoid(ga_ref[...])
        sb = _sigmoid(gb_ref[...])
        dm = dm_ref[...]
        dysb_ref[...] = (dm * sa).astype(BF16)
        dyc_ref[...] = (dm * sb).astype(BF16)
        dg_ref[:, :D_MODEL] = (dm * ysb_ref[...] * sa * (1.0 - sa)).astype(BF16)
        dg_ref[:, D_MODEL:] = (dm * yc_ref[...] * sb * (1.0 - sb)).astype(BF16)

    row = _rows(ts, D_MODEL, ns)
    t = ysb.shape[0]
    return pl.pallas_call(
        body, name="merge_bwd", grid=grid,
        in_specs=[_rows(ts, D_MODEL, ns, _COL_GATE_A), _rows(ts, D_MODEL, ns, _COL_GATE_B), row, row, row],
        out_specs=[row, row, _rows(ts, 2 * D_MODEL, ns)],
        out_shape=[jax.ShapeDtypeStruct((t, D_MODEL), BF16), jax.ShapeDtypeStruct((t, D_MODEL), BF16),
                   jax.ShapeDtypeStruct((t, 2 * D_MODEL), BF16)],
        compiler_params=_params(("parallel", "parallel")),
    )(proj, proj, ysb, yconv, dmerged)


_CONV_ROWS = 128


def _halo_prev(tt, ns, col):
    r = tt // HALO
    return pl.BlockSpec((HALO, D_MODEL), lambda b, s: (jnp.maximum((b * ns + s) * r - 1, 0), col))


def _halo_next(tt, ns, nblk, col):
    r = tt // HALO
    return pl.BlockSpec((HALO, D_MODEL), lambda b, s: (jnp.minimum((b * ns + s + 1) * r, nblk - 1), col))


def _windows(pad_ref, stage_ref, tt, offsets):
    for r in range(8):
        mine = [o for o in offsets if o % 8 == r]
        if not mine:
            continue
        n = max(mine) - r + tt
        stage_ref[0:n, :] = pad_ref[r:r + n, :]
        for o in mine:
            yield o, stage_ref[o - r:o - r + tt, :]


def _fill_hc(hpad, a_ref, b_ref, ha_ref, hb_ref, s):
    halo = ha_ref[...] * _sigmoid(hb_ref[...])
    hpad[0:HALO, :] = jnp.where(s > 0, halo, 0.0)
    hpad[HALO:, :] = a_ref[...] * _sigmoid(b_ref[...])


def _conv_fwd(proj, conv_w, conv_b, ln_g, ln_b, bsz, seq):
    tt = _CONV_ROWS
    grid, ns = _row_grid(bsz, seq, tt)
    off = HALO - (CONV_TAPS - 1)

    def body(a_ref, b_ref, ha_ref, hb_ref, w_ref, cb_ref, g_ref, bb_ref, cs_ref, cv_ref, hpad, stage):
        _fill_hc(hpad, a_ref, b_ref, ha_ref, hb_ref, pl.program_id(1))
        acc = jnp.zeros((tt, D_MODEL), F32)
        for o, win in _windows(hpad, stage, tt, [off + j for j in range(CONV_TAPS)]):
            acc = acc + w_ref[o - off:o - off + 1, :] * win
        cv = acc + cb_ref[...]
        cv_ref[...] = cv
        xhat, _ = _ln_stats(cv)
        cs_ref[...] = _silu(xhat * g_ref[...] + bb_ref[...]).astype(BF16)

    row = _rows(tt, D_MODEL, ns)
    t = proj.shape[0]
    return pl.pallas_call(
        body, name="conv_fwd", grid=grid,
        in_specs=[_rows(tt, D_MODEL, ns, _COL_GLU_A), _rows(tt, D_MODEL, ns, _COL_GLU_B),
                  _halo_prev(tt, ns, _COL_GLU_A), _halo_prev(tt, ns, _COL_GLU_B),
                  _vec_spec(32), _vec_spec(), _vec_spec(), _vec_spec()],
        out_specs=[row, row],
        out_shape=[jax.ShapeDtypeStruct((t, D_MODEL), BF16), jax.ShapeDtypeStruct((t, D_MODEL), F32)],
        scratch_shapes=[pltpu.VMEM((HALO + tt, D_MODEL), F32), pltpu.VMEM((HALO + tt, D_MODEL), F32)],
        compiler_params=_params(("parallel", "parallel")),
    )(proj, proj, proj, proj, conv_w, conv_b, ln_g, ln_b)


def _conv_bwd_ln(dcs, cv, ln_g, ln_b, bsz, seq):
    ts = _pick(seq, (256, 128))
    grid, ns = _row_grid(bsz, seq, ts)

    def body(dcs_ref, cv_ref, g_ref, b_ref, dcv_ref, st_ref):
        xhat, rstd = _ln_stats(cv_ref[...])
        cl = xhat * g_ref[...] + b_ref[...]
        dcl = dcs_ref[...] * _dsilu(cl)
        dxhat = dcl * g_ref[...]
        m1 = jnp.mean(dxhat, axis=-1, keepdims=True)
        m2 = jnp.mean(dxhat * xhat, axis=-1, keepdims=True)
        dcv = rstd * (dxhat - m1 - xhat * m2)
        dcv_ref[...] = dcv

        @pl.when((pl.program_id(0) == 0) & (pl.program_id(1) == 0))
        def _():
            st_ref[...] = jnp.zeros_like(st_ref)

        st_ref[0:1, :] += _colsum(dcl * xhat)
        st_ref[1:2, :] += _colsum(dcl)
        st_ref[2:3, :] += _colsum(dcv)

    row = _rows(ts, D_MODEL, ns)
    return pl.pallas_call(
        body, name="conv_bwd_ln", grid=grid, in_specs=[row, row, _vec_spec(), _vec_spec()],
        out_specs=[row, _vec_spec(8)],
        out_shape=[jax.ShapeDtypeStruct(cv.shape, F32), jax.ShapeDtypeStruct((8, D_MODEL), F32)],
        compiler_params=_params(("arbitrary", "arbitrary")),
    )(dcs, cv, ln_g, ln_b)


def _conv_bwd_taps(proj, dcv, conv_w, bsz, seq):
    tt = _CONV_ROWS
    grid, ns = _row_grid(bsz, seq, tt)
    off = HALO - (CONV_TAPS - 1)
    nblk = proj.shape[0] // HALO

    def body(a_ref, b_ref, ha_ref, hb_ref, d_ref, dn_ref, w_ref, dglu_ref, dw_ref, hpad, dpad, stage):
        s = pl.program_id(1)
        _fill_hc(hpad, a_ref, b_ref, ha_ref, hb_ref, s)
        dcv = d_ref[...]
        dpad[0:tt, :] = dcv
        dpad[tt:, :] = jnp.where(s < ns - 1, dn_ref[...], 0.0)

        @pl.when((pl.program_id(0) == 0) & (s == 0))
        def _():
            dw_ref[...] = jnp.zeros_like(dw_ref)

        dhc = jnp.zeros((tt, D_MODEL), F32)
        for o, win in _windows(dpad, stage, tt, list(range(CONV_TAPS))):
            j = CONV_TAPS - 1 - o
            dhc = dhc + w_ref[j:j + 1, :] * win
        for o, win in _windows(hpad, stage, tt, [off + j for j in range(CONV_TAPS)]):
            dw_ref[o - off:o - off + 1, :] += _colsum(dcv * win)
        sb = _sigmoid(b_ref[...])
        dglu_ref[:, :D_MODEL] = (dhc * sb).astype(BF16)
        dglu_ref[:, D_MODEL:] = (dhc * a_ref[...] * sb * (1.0 - sb)).astype(BF16)

    t = proj.shape[0]
    return pl.pallas_call(
        body, name="conv_bwd_taps", grid=grid,
        in_specs=[_rows(tt, D_MODEL, ns, _COL_GLU_A), _rows(tt, D_MODEL, ns, _COL_GLU_B),
                  _halo_prev(tt, ns, _COL_GLU_A), _halo_prev(tt, ns, _COL_GLU_B),
                  _rows(tt, D_MODEL, ns), _halo_next(tt, ns, nblk, 0), _vec_spec(32)],
        out_specs=[_rows(tt, 2 * D_MODEL, ns), _vec_spec(32)],
        out_shape=[jax.ShapeDtypeStruct((t, 2 * D_MODEL), BF16), jax.ShapeDtypeStruct((32, D_MODEL), F32)],
        scratch_shapes=[pltpu.VMEM((HALO + tt, D_MODEL), F32), pltpu.VMEM((tt + HALO, D_MODEL), F32),
                        pltpu.VMEM((HALO + tt, D_MODEL), F32)],
        compiler_params=_params(("arbitrary", "arbitrary")),
    )(proj, proj, proj, proj, dcv, dcv, conv_w)


_NT = (((1,), (1,)), ((), ()))
_TN = (((0,), (0,)), ((), ()))


def _dot(a, b, dims=None):
    if dims is None:
        return jnp.dot(a, b, preferred_element_type=F32)
    return lax.dot_general(a, b, dims, preferred_element_type=F32)


def _tri_dot(v, tri2):
    hi = v.astype(BF16)
    lo = (v - hi.astype(F32)).astype(BF16)
    return _dot(jnp.concatenate([hi, lo], axis=1), tri2)


def _tri2(mask):
    t = mask.astype(BF16)
    return jnp.concatenate([t, t], axis=0)


def _softplus_parts(z):
    t = jnp.exp(-jnp.abs(z))
    den = 1.0 + t
    return jnp.maximum(z, 0.0) + jnp.log(den), t, den


def _attn_fwd(proj, bsz, seq):
    blk = ATT_BLOCK
    nq = seq // blk
    n_pairs = D_MODEL // LANES

    def body(q_ref, k_ref, v_ref, y_ref, rt_ref, zr_buf, ns_buf, run_buf, acc_buf):
        qi = pl.program_id(2)
        lane = lax.broadcasted_iota(jnp.int32, (blk, LANES), 1)
        first = lane < HEAD_DIM
        q2 = q_ref[...] * 0.125
        q_heads = (jnp.where(first, q2, 0.0).astype(BF16), jnp.where(first, 0.0, q2).astype(BF16))
        rr = lax.broadcasted_iota(jnp.int32, (blk, blk), 0)
        cc = lax.broadcasted_iota(jnp.int32, (blk, blk), 1)
        tri_ge = _tri2(rr >= cc)
        causal = cc < rr

        def scores(kb, slot, masked, heads=(0, 1)):
            k_blk = k_ref[pl.ds(pl.multiple_of(kb * blk, blk), blk), :].astype(BF16)
            for h in heads:
                z = _dot(q_heads[h], k_blk, _NT)
                if masked:
                    z = jnp.where(causal, z, -1e30)
                sp, _, _ = _softplus_parts(z)
                neg = -sp
                zr_buf[slot, h] = z + _tri_dot(neg, tri_ge)
                ns_buf[slot, h] = jnp.sum(neg, axis=1, keepdims=True)

        def weigh(kb, slot, heads=(0, 1)):
            v_blk = v_ref[pl.ds(pl.multiple_of(kb * blk, blk), blk), :].astype(BF16)
            for h in heads:
                run = run_buf[h]
                w = jnp.exp(zr_buf[slot, h] + run)
                acc_buf[h] += _dot(w.astype(BF16), v_blk)
                run_buf[h] = run + ns_buf[slot, h]

        def step(kb_next, kb, slot):
            for h in range(2):
                scores(kb_next, 1 - slot, False, (h,))
                weigh(kb, slot, (h,))

        run_buf[...] = jnp.zeros_like(run_buf)
        acc_buf[...] = jnp.zeros_like(acc_buf)
        scores(qi, 0, True)

        def two_steps(p, carry):
            t = 2 * p
            step(qi - t - 1, qi - t, 0)
            step(qi - t - 2, qi - t - 1, 1)
            return carry

        lax.fori_loop(0, qi // 2, two_steps, 0)

        @pl.when(qi % 2 == 1)
        def _():
            step(0, 1, 0)
            weigh(0, 1)

        @pl.when(qi % 2 == 0)
        def _():
            weigh(0, 0)

        y_ref[...] = jnp.where(first, acc_buf[0], acc_buf[1]).astype(BF16)
        rt_ref[...] = jnp.where(first, jnp.broadcast_to(run_buf[0], (blk, LANES)),
                                jnp.broadcast_to(run_buf[1], (blk, LANES)))

    t = proj.shape[0]
    q_spec = pl.BlockSpec((blk, LANES), lambda b, p, i: (b * nq + i, p))
    return pl.pallas_call(
        body, name="attn_fwd", grid=(bsz, n_pairs, nq),
        in_specs=[q_spec,
                  pl.BlockSpec((seq, LANES), lambda b, p, i: (b, n_pairs + p)),
                  pl.BlockSpec((seq, LANES), lambda b, p, i: (b, 2 * n_pairs + p))],
        out_specs=[q_spec, q_spec],
        out_shape=[jax.ShapeDtypeStruct((t, D_MODEL), BF16), jax.ShapeDtypeStruct((t, D_MODEL), F32)],
        scratch_shapes=[pltpu.VMEM((2, 2, blk, blk), F32), pltpu.VMEM((2, 2, blk, 1), F32),
                        pltpu.VMEM((2, blk, 1), F32), pltpu.VMEM((2, blk, LANES), F32)],
        compiler_params=_params(("parallel", "parallel", "arbitrary")),
    )(proj, proj, proj)


def _attn_bwd(proj, rtot, dy, bsz, seq):
    blk = ATT_BLOCK
    nq = seq // blk
    n_pairs = D_MODEL // LANES

    def body(q_ref, k_ref, v_ref, dy_ref, rt_ref, dq_ref, dk_ref, dv_ref, dk_acc, dv_acc,
             a_buf, sig_buf, dw_buf, ns_buf, pre_buf, es_buf, dq_buf):
        qi = pl.program_id(2)

        @pl.when(qi == 0)
        def _():
            dk_acc[...] = jnp.zeros_like(dk_acc)
            dv_acc[...] = jnp.zeros_like(dv_acc)

        lane = lax.broadcasted_iota(jnp.int32, (blk, LANES), 1)
        first = lane < HEAD_DIM
        head_row = lax.broadcasted_iota(jnp.int32, (LANES, blk), 0) < HEAD_DIM
        q2 = q_ref[...] * 0.125
        q_rows = (jnp.where(first, q2, 0.0).astype(BF16), jnp.where(first, 0.0, q2).astype(BF16))
        q_t = q2.T
        q_heads = (jnp.where(head_row, q_t, 0.0).astype(BF16), jnp.where(head_row, 0.0, q_t).astype(BF16))
        dy2 = dy_ref[...].astype(F32)
        dy_rows = (jnp.where(first, dy2, 0.0).astype(BF16), jnp.where(first, 0.0, dy2).astype(BF16))
        dy_t = dy2.T
        dy_heads = (jnp.where(head_row, dy_t, 0.0).astype(BF16), jnp.where(head_row, 0.0, dy_t).astype(BF16))
        rt_t = rt_ref[...].T
        rt = (rt_t[0:1, :], rt_t[HEAD_DIM:HEAD_DIM + 1, :])
        rr = lax.broadcasted_iota(jnp.int32, (blk, blk), 0)
        cc = lax.broadcasted_iota(jnp.int32, (blk, blk), 1)
        lower = (cc < rr).astype(BF16)
        lower_eq = (cc <= rr).astype(BF16)
        tri_lt = jnp.concatenate([lower, lower], axis=1)
        tri_le = jnp.concatenate([lower_eq, lower_eq], axis=1)
        causal = rr < cc

        def tri_left(tri2, v):
            hi = v.astype(BF16)
            lo = (v - hi.astype(F32)).astype(BF16)
            return _dot(tri2, jnp.concatenate([hi, lo], axis=0))

        def scores(kb, slot, heads=(0, 1)):
            rows = pl.ds(pl.multiple_of(kb * blk, blk), blk)
            k_blk = k_ref[rows, :].astype(BF16)
            v_blk = v_ref[rows, :].astype(BF16)
            keep = jnp.logical_or(causal, kb < qi)
            for h in heads:
                z = jnp.where(keep, _dot(k_blk, q_heads[h]), -1e30)
                sp, t, den = _softplus_parts(z)
                neg = -sp
                a_buf[slot, h] = z - tri_left(tri_lt, neg)
                sig_buf[slot, h] = jnp.where(z >= 0, 1.0, t) / den
                ns_buf[slot, h] = jnp.sum(neg, axis=0, keepdims=True)
                dw_buf[slot, h] = _dot(v_blk, dy_heads[h])

        def finish(kb, slot, heads=(0, 1)):
            rows = pl.ds(pl.multiple_of(kb * blk, blk), blk)
            k_t = k_ref[rows, :].T.astype(BF16)
            for h in heads:
                pre, esum = pre_buf[h], es_buf[h]
                w = jnp.exp(a_buf[slot, h] + (rt[h] - pre))
                e = dw_buf[slot, h] * w
                dz = e - sig_buf[slot, h] * (esum + tri_left(tri_le, e))
                pre_buf[h] = pre + ns_buf[slot, h]
                es_buf[h] = esum + jnp.sum(e, axis=0, keepdims=True)
                dzb = dz.astype(BF16)
                dq_buf[h] += _dot(k_t, dzb)
                dk_acc[rows, :] += _dot(dzb, q_rows[h])
                dv_acc[rows, :] += _dot(w.astype(BF16), dy_rows[h])

        def step(kb_next, kb, slot):
            for h in range(2):
                scores(kb_next, 1 - slot, (h,))
                finish(kb, slot, (h,))

        pre_buf[...] = jnp.zeros_like(pre_buf)
        es_buf[...] = jnp.zeros_like(es_buf)
        dq_buf[...] = jnp.zeros_like(dq_buf)
        scores(0, 0)

        def two_steps(p, carry):
            t = 2 * p
            step(t + 1, t, 0)
            step(t + 2, t + 1, 1)
            return carry

        lax.fori_loop(0, qi // 2, two_steps, 0)

        @pl.when(qi % 2 == 1)
        def _():
            step(qi, qi - 1, 0)
            finish(qi, 1)

        @pl.when(qi % 2 == 0)
        def _():
            finish(qi, 0)

        dq_ref[...] = (jnp.where(head_row, dq_buf[0], dq_buf[1]).T * 0.125).astype(BF16)

        @pl.when(qi == nq - 1)
        def _():
            dk_ref[...] = dk_acc[...].astype(BF16)
            dv_ref[...] = dv_acc[...].astype(BF16)

    t = proj.shape[0]
    q_spec = pl.BlockSpec((blk, LANES), lambda b, p, i: (b * nq + i, p))
    kv_out = pl.BlockSpec((seq, LANES), lambda b, p, i: (b, p))
    out = jax.ShapeDtypeStruct((t, D_MODEL), BF16)
    return pl.pallas_call(
        body, name="attn_bwd", grid=(bsz, n_pairs, nq),
        in_specs=[q_spec,
                  pl.BlockSpec((seq, LANES), lambda b, p, i: (b, n_pairs + p)),
                  pl.BlockSpec((seq, LANES), lambda b, p, i: (b, 2 * n_pairs + p)),
                  q_spec, q_spec],
        out_specs=[q_spec, kv_out, kv_out], out_shape=[out, out, out],
        scratch_shapes=[pltpu.VMEM((seq, LANES), F32), pltpu.VMEM((seq, LANES), F32),
                        pltpu.VMEM((2, 2, blk, blk), F32), pltpu.VMEM((2, 2, blk, blk), F32),
                        pltpu.VMEM((2, 2, blk, blk), F32), pltpu.VMEM((2, 2, 1, blk), F32),
                        pltpu.VMEM((2, 1, blk), F32), pltpu.VMEM((2, 1, blk), F32),
                        pltpu.VMEM((2, LANES, blk), F32)],
        compiler_params=_params(("parallel", "parallel", "arbitrary")),
    )(proj, proj, proj, dy, rtot)


def _adamw(w, g, m, v, name):
    rows, cols = w.shape
    tr = _pick(rows, (256, 352, 128, 64, 32, 16, 8))
    c1 = 1.0 - ADAM_B1 ** ADAM_STEP
    c2 = 1.0 - ADAM_B2 ** ADAM_STEP

    def body(w_ref, g_ref, m_ref, v_ref, d_ref, nm_ref, nv_ref):
        g_v = g_ref[...]
        nm = ADAM_B1 * m_ref[...] + (1.0 - ADAM_B1) * g_v
        nv = ADAM_B2 * v_ref[...] + (1.0 - ADAM_B2) * (g_v * g_v)
        nm_ref[...] = nm
        nv_ref[...] = nv
        d_ref[...] = -ADAM_LR * ((nm / c1) / (jnp.sqrt(nv / c2) + ADAM_EPS) + ADAM_WD * w_ref[...])

    spec = pl.BlockSpec((tr, cols), lambda i: (i, 0))
    shape = jax.ShapeDtypeStruct(w.shape, F32)
    return pl.pallas_call(
        body, name=name, grid=(rows // tr,), in_specs=[spec] * 4, out_specs=[spec] * 3, out_shape=[shape] * 3,
        compiler_params=_params(("parallel",)),
    )(w, g, m, v)


def _ffn_fwd(u, w_gu, w_down, bsz, seq, tag):
    h = _matmul(u, w_gu, mode="nn", out_dtype=F32, name=f"{tag}_up")
    p = _swiglu_act(h, bsz, seq, f"{tag}_act")
    f = _matmul(p, w_down, mode="nn", out_dtype=F32, name=f"{tag}_down")
    return h, p, f


def _ffn_bwd(df, u, h, p, w_gu, w_down, bsz, seq, tag):
    dp = _matmul(df, w_down, mode="nt", out_dtype=F32, name=f"{tag}_ddown")
    dh = _swiglu_bwd(h, dp, bsz, seq, f"{tag}_dact")
    g_down = _matmul(p, df, mode="tn", out_dtype=F32, name=f"{tag}_gdown")
    g_gu = _matmul(u, dh, mode="tn", out_dtype=F32, name=f"{tag}_ggu", out_stacked=True)
    du = _matmul(dh, w_gu, mode="nt", out_dtype=F32, name=f"{tag}_dup")
    return du, g_gu, g_down


def _local_step(x, c, target, wts, vecs):
    bsz, seq, _ = x.shape
    t = bsz * seq
    x0 = x.reshape(t, D_MODEL)
    tgt = target.reshape(t, D_MODEL)

    sc = _silu_pad(c)
    mod16 = _matmul(sc, wts["w_ada"], mode="nn", out_dtype=F32, name="ada_fwd", bias=vecs["b_ada"])
    mod = mod16[:bsz].reshape(bsz, 9, D_MODEL)

    u1 = _mod_in(x0, mod, bsz, seq, 0)
    h1, p1, f1 = _ffn_fwd(u1, wts["ffn1_w_gu"], wts["ffn1_w_down"], bsz, seq, "ffn1")
    r1, x1, u2 = _res_ln_fwd(x0, f1, mod, vecs["ln1_g"], vecs["ln1_b"], bsz, seq, 0, 0.5)

    proj = _matmul(u2, wts["w_in"], mode="nn", out_dtype=F32, name="mix_in")
    ya, rtot = _attn_fwd(proj, bsz, seq)
    cs, cv = _conv_fwd(proj, wts["conv_w"], vecs["conv_b"], vecs["conv_ln_g"], vecs["conv_ln_b"], bsz, seq)
    ysb = _matmul(ya, wts["w_sb_out"], mode="nn", out_dtype=F32, name="sb_out")
    yconv = _matmul(cs, wts["w_conv_out"], mode="nn", out_dtype=F32, name="conv_out")
    merged = _merge_fwd(proj, ysb, yconv, bsz, seq)
    o2 = _matmul(merged, wts["w_out"], mode="nn", out_dtype=F32, name="mix_out")
    r2, x2, u3 = _res_ln_fwd(x1, o2, mod, vecs["ln2_g"], vecs["ln2_b"], bsz, seq, 1, 1.0)

    h3, p3, f3 = _ffn_fwd(u3, wts["ffn2_w_gu"], wts["ffn2_w_down"], bsz, seq, "ffn2")
    r3, dy, loss_blk = _res_ln_fwd(x2, f3, mod, vecs["ln3_g"], vecs["ln3_b"], bsz, seq, 2, 0.5, target=tgt)

    grads = {}
    dxres, df, ln3s, g3s = _res_ln_bwd(r3, dy, f3, mod, vecs["ln3_g"], bsz, seq, 2, 0.5)
    du, grads["ffn2_w_gu"], grads["ffn2_w_down"] = _ffn_bwd(
        df, u3, h3, p3, wts["ffn2_w_gu"], wts["ffn2_w_down"], bsz, seq, "ffn2")
    dx2, m3s = _mod_bwd(dxres, du, x2, mod, bsz, seq, 2)

    dxres, do2, ln2s, g2s = _res_ln_bwd(r2, dx2, o2, mod, vecs["ln2_g"], bsz, seq, 1, 1.0)
    dmerged = _matmul(do2, wts["w_out"], mode="nt", out_dtype=F32, name="mix_out_d")
    grads["w_out"] = _matmul(merged, do2, mode="tn", out_dtype=F32, name="mix_out_g")
    dysb, dyconv, dgate = _merge_bwd(proj, ysb, yconv, dmerged, bsz, seq)
    dya = _matmul(dysb, wts["w_sb_out"], mode="nt", out_dtype=BF16, name="sb_out_d")
    grads["w_sb_out"] = _matmul(ya, dysb, mode="tn", out_dtype=F32, name="sb_out_g")
    dcs = _matmul(dyconv, wts["w_conv_out"], mode="nt", out_dtype=F32, name="conv_out_d")
    grads["w_conv_out"] = _matmul(cs, dyconv, mode="tn", out_dtype=F32, name="conv_out_g")
    dcv, convs = _conv_bwd_ln(dcs, cv, vecs["conv_ln_g"], vecs["conv_ln_b"], bsz, seq)
    dglu, g_conv_w = _conv_bwd_taps(proj, dcv, wts["conv_w"], bsz, seq)
    dq, dk, dv = _attn_bwd(proj, rtot, dya, bsz, seq)
    dproj = jnp.concatenate([dq, dk, dv, dglu, dgate], axis=1)
    grads["w_in"] = _matmul(u2, dproj, mode="tn", out_dtype=F32, name="mix_in_g", out_stacked=True)
    du = _matmul(dproj, wts["w_in"], mode="nt", out_dtype=F32, name="mix_in_d")
    dx1, m2s = _mod_bwd(dxres, du, x1, mod, bsz, seq, 1)

    dxres, df, ln1s, g1s = _res_ln_bwd(r1, dx1, f1, mod, vecs["ln1_g"], bsz, seq, 0, 0.5)
    du, grads["ffn1_w_gu"], grads["ffn1_w_down"] = _ffn_bwd(
        df, u1, h1, p1, wts["ffn1_w_gu"], wts["ffn1_w_down"], bsz, seq, "ffn1")
    grad_x, m1s = _mod_bwd(dxres, du, x0, mod, bsz, seq, 0)

    dmod = jnp.stack([m1s[:, 0], m1s[:, 1], g1s[:, 0], m2s[:, 0], m2s[:, 1], g2s[:, 0],
                      m3s[:, 0], m3s[:, 1], g3s[:, 0]], axis=1)
    dmod16 = jnp.zeros((16, 9 * D_MODEL), F32).at[:bsz].set(dmod.reshape(bsz, 9 * D_MODEL))
    grads["w_ada"] = _matmul(sc, dmod16.astype(BF16), mode="tn", out_dtype=F32, name="ada_g", out_stacked=True)

    small = {"dmod": dmod, "ln1": ln1s, "ln2": ln2s, "ln3": ln3s, "conv": convs, "conv_w": g_conv_w,
             "loss": loss_blk}
    return grad_x.reshape(x.shape), grads, small


_HBM = pl.BlockSpec(memory_space=pltpu.HBM)


def _position():
    return lax.axis_index("x"), lax.axis_index("y"), lax.axis_index("c")


def _other_chips(x, y):
    return [(1 - x, y), (x, 1 - y), (1 - x, 1 - y)]


def _cast_into_stack(w_local, chip, name):
    rows, cols = w_local.shape
    tr = _pick(rows, (256, 352, 128, 64, 32, 16))

    def body(chip_ref, w_ref, o_ref):
        o_ref[...] = w_ref[...].astype(BF16)

    return pl.pallas_call(
        body, name=name,
        grid_spec=pltpu.PrefetchScalarGridSpec(
            num_scalar_prefetch=1, grid=(rows // tr,),
            in_specs=[pl.BlockSpec((tr, cols), lambda r, chip_ref: (r, 0))],
            out_specs=pl.BlockSpec((None, tr, cols), lambda r, chip_ref: (chip_ref[0], r, 0))),
        out_shape=jax.ShapeDtypeStruct((N_CHIPS, rows, cols), BF16),
        compiler_params=_params(("parallel",)),
    )(chip, w_local)


def _all_gather_weights(stacks, small):
    n = len(stacks)

    def body(*refs):
        ins, small_in, outs, small_out = refs[:n], refs[n], refs[n + 1:2 * n + 1], refs[2 * n + 1]
        send_sems, recv_sems, fwd_send_sems, fwd_recv_sems, small_sems = refs[2 * n + 2:]
        x, y, c = _position()
        me = 2 * x + y
        chips = _other_chips(x, y)

        def send(i, j):
            px, py = chips[j]
            return pltpu.make_async_remote_copy(
                src_ref=ins[i].at[me, c], dst_ref=outs[i].at[me, c], send_sem=send_sems.at[3 * i + j],
                recv_sem=recv_sems.at[3 * i + j], device_id=(px, py, c), device_id_type=MESH)

        def landed(i, j):
            px, py = chips[j]
            return pltpu.make_async_remote_copy(
                src_ref=ins[i].at[me, c], dst_ref=outs[i].at[2 * px + py, c], send_sem=send_sems.at[3 * i + j],
                recv_sem=recv_sems.at[3 * i + j], device_id=(px, py, c), device_id_type=MESH)

        def forward(i, j, half):
            px, py = chips[j]
            blk = outs[i].at[2 * px + py, half]
            return pltpu.make_async_remote_copy(
                src_ref=blk, dst_ref=blk, send_sem=fwd_send_sems.at[3 * i + j],
                recv_sem=fwd_recv_sems.at[3 * i + j], device_id=(x, y, 1 - c), device_id_type=MESH)

        def small_copy(j, slot):
            px, py = chips[j]
            return pltpu.make_async_remote_copy(
                src_ref=small_in, dst_ref=small_out.at[slot], send_sem=small_sems.at[j],
                recv_sem=small_sems.at[3 + j], device_id=(px, py, c), device_id_type=MESH)

        own_small = pltpu.make_async_copy(small_in, small_out.at[me], small_sems.at[6])
        own_small.start()
        for j in range(3):
            small_copy(j, me).start()
        for i in range(n):
            for j in range(3):
                send(i, j).start()
        for i in range(n):
            for j in range(3):
                landed(i, j).wait_recv()
                forward(i, j, c).start()
        for i in range(n):
            for j in range(3):
                forward(i, j, 1 - c).wait_recv()
        for j, (px, py) in enumerate(chips):
            small_copy(j, 2 * px + py).wait_recv()
        own_small.wait()
        for j in range(3):
            small_copy(j, me).wait_send()
        for i in range(n):
            for j in range(3):
                send(i, j).wait_send()
                forward(i, j, c).wait_send()

    return pl.pallas_call(
        body, name="all_gather_weights",
        out_shape=[jax.ShapeDtypeStruct(s.shape, s.dtype) for s in stacks]
        + [jax.ShapeDtypeStruct((N_CHIPS,) + small.shape, small.dtype)],
        in_specs=[_HBM] * (n + 1), out_specs=[_HBM] * (n + 1),
        input_output_aliases={i: i for i in range(n)},
        scratch_shapes=[pltpu.SemaphoreType.DMA((3 * n,)), pltpu.SemaphoreType.DMA((3 * n,)),
                        pltpu.SemaphoreType.DMA((3 * n,)), pltpu.SemaphoreType.DMA((3 * n,)),
                        pltpu.SemaphoreType.DMA((7,))],
    )(*stacks, small)


def _pair_swap(grads):
    n = len(grads)

    def body(*refs):
        ins, outs = refs[:n], refs[n:2 * n]
        send_sems, recv_sems = refs[2 * n:]
        x, y, c = _position()

        def copy(i, k):
            return pltpu.make_async_remote_copy(
                src_ref=ins[i].at[k, 1 - c], dst_ref=outs[i].at[k], send_sem=send_sems.at[N_CHIPS * i + k],
                recv_sem=recv_sems.at[N_CHIPS * i + k], device_id=(x, y, 1 - c), device_id_type=MESH)

        for i in range(n):
            for k in range(N_CHIPS):
                copy(i, k).start()
        for i in range(n):
            for k in range(N_CHIPS):
                copy(i, k).wait_recv()
        for i in range(n):
            for k in range(N_CHIPS):
                copy(i, k).wait_send()

    return pl.pallas_call(
        body, name="grad_pair_swap",
        out_shape=[jax.ShapeDtypeStruct((N_CHIPS,) + g.shape[2:], F32) for g in grads],
        in_specs=[_HBM] * n, out_specs=[_HBM] * n,
        scratch_shapes=[pltpu.SemaphoreType.DMA((N_CHIPS * n,)), pltpu.SemaphoreType.DMA((N_CHIPS * n,))],
    )(*grads)


def _pair_add(g, got, place, name):
    _, _, rh, cols = g.shape
    tr = _pick(rh, (256, 176, 128, 64, 32, 16, 8))

    def body(place_ref, g_ref, got_ref, p_ref, own_ref):
        s = g_ref[...] + got_ref[...]
        p_ref[...] = s.astype(BF16)

        @pl.when(pl.program_id(1) == place_ref[1])
        def _():
            own_ref[...] = s

    blk = pl.BlockSpec((None, tr, cols), lambda r, k, place_ref: (k, r, 0))
    return pl.pallas_call(
        body, name=name,
        grid_spec=pltpu.PrefetchScalarGridSpec(
            num_scalar_prefetch=1, grid=(rh // tr, N_CHIPS),
            in_specs=[pl.BlockSpec((None, None, tr, cols), lambda r, k, place_ref: (k, place_ref[0], r, 0)), blk],
            out_specs=[blk, pl.BlockSpec((tr, cols), lambda r, k, place_ref: (r, 0))]),
        out_shape=[jax.ShapeDtypeStruct((N_CHIPS, rh, cols), BF16), jax.ShapeDtypeStruct((rh, cols), F32)],
        compiler_params=_params(("parallel", "arbitrary")),
    )(place, g, got)


def _chip_scatter(sums):
    n = len(sums)

    def body(*refs):
        ins, outs = refs[:n], refs[n:2 * n]
        send_sems, recv_sems = refs[2 * n:]
        x, y, c = _position()
        chips = _other_chips(x, y)

        def copy(i, j):
            px, py = chips[j]
            return pltpu.make_async_remote_copy(
                src_ref=ins[i].at[2 * px + py], dst_ref=outs[i].at[j], send_sem=send_sems.at[3 * i + j],
                recv_sem=recv_sems.at[3 * i + j], device_id=(px, py, c), device_id_type=MESH)

        for i in range(n):
            for j in range(3):
                copy(i, j).start()
        for i in range(n):
            for j in range(3):
                copy(i, j).wait_recv()
        for i in range(n):
            for j in range(3):
                copy(i, j).wait_send()

    return pl.pallas_call(
        body, name="grad_chip_scatter",
        out_shape=[jax.ShapeDtypeStruct((3,) + s.shape[1:], BF16) for s in sums],
        in_specs=[_HBM] * n, out_specs=[_HBM] * n,
        scratch_shapes=[pltpu.SemaphoreType.DMA((3 * n,)), pltpu.SemaphoreType.DMA((3 * n,))],
    )(*sums)


def _chip_sum(own, parts, place, name):
    rh, cols = own.shape
    tr = _pick(rh, (256, 176, 128, 64, 32, 16, 8))

    def body(place_ref, own_ref, p_ref, o_ref):
        o_ref[...] = ((own_ref[...] + p_ref[0].astype(F32)) + p_ref[1].astype(F32)) + p_ref[2].astype(F32)

    return pl.pallas_call(
        body, name=name,
        grid_spec=pltpu.PrefetchScalarGridSpec(
            num_scalar_prefetch=1, grid=(rh // tr,),
            in_specs=[pl.BlockSpec((tr, cols), lambda r, place_ref: (r, 0)),
                      pl.BlockSpec((3, tr, cols), lambda r, place_ref: (0, r, 0))],
            out_specs=pl.BlockSpec((None, tr, cols), lambda r, place_ref: (place_ref[0], r, 0))),
        out_shape=jax.ShapeDtypeStruct((2, rh, cols), F32),
        compiler_params=_params(("parallel",)),
    )(place, own, parts)


def _pair_gather(halves):
    n = len(halves)

    def body(*refs):
        ins, outs = refs[:n], refs[n:2 * n]
        send_sems, recv_sems = refs[2 * n:]
        x, y, c = _position()

        def send(i):
            return pltpu.make_async_remote_copy(
                src_ref=ins[i].at[c], dst_ref=outs[i].at[c], send_sem=send_sems.at[i], recv_sem=recv_sems.at[i],
                device_id=(x, y, 1 - c), device_id_type=MESH)

        def landed(i):
            return pltpu.make_async_remote_copy(
                src_ref=ins[i].at[c], dst_ref=outs[i].at[1 - c], send_sem=send_sems.at[i], recv_sem=recv_sems.at[i],
                device_id=(x, y, 1 - c), device_id_type=MESH)

        for i in range(n):
            send(i).start()
        for i in range(n):
            landed(i).wait_recv()
        for i in range(n):
            send(i).wait_send()

    return pl.pallas_call(
        body, name="grad_pair_gather",
        out_shape=[jax.ShapeDtypeStruct(h.shape, F32) for h in halves],
        in_specs=[_HBM] * n, out_specs=[_HBM] * n,
        input_output_aliases={i: i for i in range(n)},
        scratch_shapes=[pltpu.SemaphoreType.DMA((n,)), pltpu.SemaphoreType.DMA((n,))],
    )(*halves)


_MOD_ROWS = 16


def _small_all_reduce(buf, bsz):
    rows, cols = buf.shape
    head = bsz * _MOD_ROWS
    out_rows = rows - head + _MOD_ROWS

    def body(in_ref, o_ref, gath, send_sems, recv_sems):
        x, y, c = _position()
        me = 4 * x + 2 * y + c

        def peer(mask):
            return (x ^ (mask >> 2), y ^ ((mask >> 1) & 1), c ^ (mask & 1))

        def copy(mask):
            return pltpu.make_async_remote_copy(
                src_ref=in_ref, dst_ref=gath.at[me], send_sem=send_sems.at[mask - 1],
                recv_sem=recv_sems.at[mask - 1], device_id=peer(mask), device_id_type=MESH)

        def arrival(mask):
            px, py, pc = peer(mask)
            return pltpu.make_async_remote_copy(
                src_ref=in_ref, dst_ref=gath.at[4 * px + 2 * py + pc], send_sem=send_sems.at[mask - 1],
                recv_sem=recv_sems.at[mask - 1], device_id=peer(mask), device_id_type=MESH)

        for mask in range(1, N_DEV):
            copy(mask).start()
        gath[me] = in_ref[...]
        for mask in range(1, N_DEV):
            arrival(mask).wait_recv()
        for mask in range(1, N_DEV):
            copy(mask).wait_send()
        acc = gath[0]
        for d in range(1, N_DEV):
            acc = acc + gath[d]
        mod = acc[0:_MOD_ROWS]
        for s in range(1, bsz):
            mod = mod + acc[s * _MOD_ROWS:(s + 1) * _MOD_ROWS]
        o_ref[0:_MOD_ROWS, :] = mod
        o_ref[_MOD_ROWS:, :] = acc[head:]

    vm = pl.BlockSpec(memory_space=pltpu.VMEM)
    return pl.pallas_call(
        body, name="small_all_reduce", in_specs=[vm], out_specs=vm,
        out_shape=jax.ShapeDtypeStruct((out_rows, cols), F32),
        scratch_shapes=[pltpu.VMEM((N_DEV, rows, cols), F32), pltpu.SemaphoreType.DMA((N_DEV - 1,)),
                        pltpu.SemaphoreType.DMA((N_DEV - 1,))],
        compiler_params=pltpu.CompilerParams(vmem_limit_bytes=VMEM_LIMIT),
    )(buf)


_COL_SHARDED = ("w_ada", "ffn1_w_gu", "w_in", "ffn2_w_gu")
_ROW_SHARDED = ("ffn1_w_down", "w_sb_out", "w_conv_out", "w_out", "ffn2_w_down")
_BIG = ("w_ada", "ffn1_w_gu", "ffn1_w_down", "w_in", "w_sb_out", "w_conv_out", "w_out", "ffn2_w_gu", "ffn2_w_down")
_VECS = ("b_ada", "ln1_g", "ln1_b", "conv_b", "conv_ln_g", "conv_ln_b", "ln2_g", "ln2_b", "ln3_g", "ln3_b")
_WEIGHTS = ("w_ada", "b_ada", "ffn1_w_gu", "ffn1_w_down", "ln1_g", "ln1_b", "w_in", "w_sb_out", "conv_w", "conv_b",
            "conv_ln_g", "conv_ln_b", "w_conv_out", "w_out", "ln2_g", "ln2_b", "ffn2_w_gu", "ffn2_w_down",
            "ln3_g", "ln3_b")


def _step(x, c, target, w, m, v):
    bsz = x.shape[0]
    chip = 2 * lax.axis_index("x") + lax.axis_index("y")
    core = lax.axis_index("c")

    chip_arr = jnp.reshape(chip, (1,)).astype(jnp.int32)
    place = jnp.stack([core, chip]).astype(jnp.int32)

    conv_w_local = jnp.pad(w["conv_w"][0], ((0, 1), (0, 0)))
    stacks = []
    for n in _BIG:
        rows, cols = w[n].shape[1:]
        s = _cast_into_stack(w[n][0], chip_arr, f"cast_{n}")
        stacks.append(s.reshape(N_CHIPS, 2, rows // 2, cols))
    gathered = _all_gather_weights(stacks, conv_w_local)
    wts = {}
    for n, g in zip(_BIG, gathered[:-1]):
        rows, cols = w[n].shape[1:]
        wts[n] = g.reshape(N_CHIPS, rows, cols) if n in _COL_SHARDED else g.reshape(N_CHIPS * rows, cols)
    wts["conv_w"] = gathered[-1].transpose(1, 0, 2).reshape(32, D_MODEL)
    vecs = {n: w[n] for n in _VECS}

    grad_x, grads, small = _local_step(x, c, target, wts, vecs)

    views = []
    for n in _BIG:
        g = grads[n]
        rows, cols = w[n].shape[1:]
        views.append(g.reshape(N_CHIPS, 2, rows // 2, cols))
    got = _pair_swap(views)
    pair_sums = [_pair_add(g, r, place, f"pair_add_{n}") for n, g, r in zip(_BIG, views, got)]
    parts = _chip_scatter([p for p, _ in pair_sums])
    halves = [_chip_sum(own, p, place, f"chip_sum_{n}") for n, (_, own), p in zip(_BIG, pair_sums, parts)]
    full = _pair_gather(halves)
    g_out = {n: f.reshape(w[n].shape[1:]) for n, f in zip(_BIG, full)}

    dmod = jnp.pad(small["dmod"], ((0, 0), (0, _MOD_ROWS - 9), (0, 0))).reshape(bsz * _MOD_ROWS, D_MODEL)
    loss_rows = jnp.pad(small["loss"], ((0, 0), (0, D_MODEL - LANES)))
    buf = jnp.concatenate([dmod, small["ln1"], small["ln2"], small["ln3"], small["conv"], small["conv_w"],
                           loss_rows], axis=0)
    red = _small_all_reduce(buf, bsz)
    o = _MOD_ROWS
    g_out["b_ada"] = red[0:9].reshape(1, 9 * D_MODEL)
    g_out["ln1_g"], g_out["ln1_b"] = red[o:o + 1], red[o + 1:o + 2]
    g_out["ln2_g"], g_out["ln2_b"] = red[o + 8:o + 9], red[o + 9:o + 10]
    g_out["ln3_g"], g_out["ln3_b"] = red[o + 16:o + 17], red[o + 17:o + 18]
    g_out["conv_ln_g"], g_out["conv_ln_b"], g_out["conv_b"] = red[o + 24:o + 25], red[o + 25:o + 26], red[o + 26:o + 27]
    cw = w["conv_w"].shape[2]
    g_out["conv_w"] = lax.dynamic_slice(red[o + 32:o + 32 + CONV_TAPS], (0, chip * cw), (CONV_TAPS, cw))
    loss = red[o + 64, 0]

    outs_g, outs_d, outs_m, outs_v = [], [], [], []
    for n in _WEIGHTS:
        shape = w[n].shape
        flat = shape[1:] if len(shape) == 3 else shape
        d, nm, nv = _adamw(w[n].reshape(flat), g_out[n].reshape(flat), m[n].reshape(flat), v[n].reshape(flat),
                           f"adamw_{n}")
        outs_g.append(g_out[n].reshape(shape))
        outs_d.append(d.reshape(shape))
        outs_m.append(nm.reshape(shape))
        outs_v.append(nv.reshape(shape))
    return (loss, grad_x, *outs_g, *outs_d, *outs_m, *outs_v)


def kernel(x, c, w_ada, b_ada, ffn1_w_gu, ffn1_w_down, ln1_g, ln1_b, w_in, w_sb_out, conv_w, conv_b, conv_ln_g, conv_ln_b, w_conv_out, w_out, ln2_g, ln2_b, ffn2_w_gu, ffn2_w_down, ln3_g, ln3_b, loss_target, m_w_ada, m_b_ada, m_ffn1_w_gu, m_ffn1_w_down, m_ln1_g, m_ln1_b, m_w_in, m_w_sb_out, m_conv_w, m_conv_b, m_conv_ln_g, m_conv_ln_b, m_w_conv_out, m_w_out, m_ln2_g, m_ln2_b, m_ffn2_w_gu, m_ffn2_w_down, m_ln3_g, m_ln3_b, v_w_ada, v_b_ada, v_ffn1_w_gu, v_ffn1_w_down, v_ln1_g, v_ln1_b, v_w_in, v_w_sb_out, v_conv_w, v_conv_b, v_conv_ln_g, v_conv_ln_b, v_w_conv_out, v_w_out, v_ln2_g, v_ln2_b, v_ffn2_w_gu, v_ffn2_w_down, v_ln3_g, v_ln3_b):
    given = dict(locals())
    w = {n: given[n] for n in _WEIGHTS}
    m = {n: given["m_" + n] for n in _WEIGHTS}
    v = {n: given["v_" + n] for n in _WEIGHTS}
    return _step(x, c, loss_target, w, m, v)
```

```python
import functools

import jax
import jax.numpy as jnp
from jax import lax
from jax.experimental import pallas as pl
from jax.experimental.pallas import tpu as pltpu

F32 = jnp.float32
BF16 = jnp.bfloat16

D_MODEL = 1024
D_FF = 2816
HEADS = 16
HEAD_DIM = 64
LANES = 128
CONV_TAPS = 31
HALO = 32
N_CHIPS = 4
N_DEV = 8
ALPHA = 2.0 ** 0.25
LN_EPS = 1e-5
ATT_BLOCK = 256
VMEM_LIMIT = 56 * 1024 * 1024

ADAM_LR = 0.001
ADAM_B1 = 0.9
ADAM_B2 = 0.999
ADAM_EPS = 1e-08
ADAM_WD = 0.01
ADAM_STEP = 10

MESH = pl.DeviceIdType.MESH


def _pick(n, cands):
    for t in cands:
        if t <= n and n % t == 0:
            return t
    return n


def _params(sem):
    return pltpu.CompilerParams(dimension_semantics=sem, vmem_limit_bytes=VMEM_LIMIT)


def _sigmoid(z):
    t = jnp.exp(-jnp.abs(z))
    return jnp.where(z >= 0, 1.0, t) / (1.0 + t)


def _silu(z):
    return z * _sigmoid(z)


def _dsilu(z):
    s = _sigmoid(z)
    return s * (1.0 + z * (1.0 - s))


def _ln_stats(r):
    mu = jnp.mean(r, axis=-1, keepdims=True)
    d = r - mu
    var = jnp.mean(d * d, axis=-1, keepdims=True)
    rstd = lax.rsqrt(var + LN_EPS)
    return d * rstd, rstd


def _colsum(v):
    return jnp.sum(v, axis=0, keepdims=True)


_DIMS = {"nn": (((1,), (0,)), ((), ())), "nt": (((1,), (1,)), ((), ())), "tn": (((0,), (0,)), ((), ()))}
_TN_CANDS = (1408, 1792, 1152, 1024, 512, 256, 128)
_TK_CANDS = (1024, 1408, 896, 512, 256, 128)


def _matmul(a, b, *, mode, out_dtype, name, bias=None, out_stacked=False):
    a_halves = mode == "nt" and a.ndim == 3
    b_halves = mode == "tn" and b.ndim == 3
    b_stacked = b.ndim == 3 and not b_halves
    if mode == "nn":
        m, k = a.shape
        n_c = b.shape[-1]
        n = n_c * (N_CHIPS if b_stacked else 1)
        k_c = k
    elif mode == "nt":
        m = a.shape[-2]
        k = a.shape[-1] * (2 if a_halves else 1)
        n = b.shape[-2]
        k_c = b.shape[-1]
        n_c = n
    else:
        k, m = a.shape
        n = b.shape[-1] * (2 if b_halves else 1)
        n_c = n // N_CHIPS if out_stacked else n
        k_c = k
    if mode == "tn":
        tm = _pick(m, (1024, 1408, 512, 256, 128))
        tk = _pick(k, (512, 256, 128, 64, 32, 16))
    else:
        tm = _pick(m, (1024, 512, 256, 128, 64, 32, 16))
        tk = _pick(k_c, _TK_CANDS)
    tn = _pick(n_c, _TN_CANDS)
    nb = n_c // tn
    kb = k_c // tk
    nk = k // tk
    grid = (m // tm, n // tn, nk)

    if mode == "nn":
        a_spec = pl.BlockSpec((tm, tk), lambda i, j, kk: (i, kk))
        if b_stacked:
            b_spec = pl.BlockSpec((None, tk, tn), lambda i, j, kk: (j // nb, kk, j % nb))
        else:
            b_spec = pl.BlockSpec((tk, tn), lambda i, j, kk: (kk, j))
    elif mode == "nt":
        if a_halves:
            ka = a.shape[-1] // tk
            a_spec = pl.BlockSpec((None, tm, tk), lambda i, j, kk: (kk // ka, i, kk % ka))
        else:
            a_spec = pl.BlockSpec((tm, tk), lambda i, j, kk: (i, kk))
        if b_stacked:
            b_spec = pl.BlockSpec((None, tn, tk), lambda i, j, kk: (kk // kb, j, kk % kb))
        else:
            b_spec = pl.BlockSpec((tn, tk), lambda i, j, kk: (j, kk))
    else:
        a_spec = pl.BlockSpec((tk, tm), lambda i, j, kk: (kk, i))
        if b_halves:
            nh = b.shape[-1] // tn
            b_spec = pl.BlockSpec((None, tk, tn), lambda i, j, kk: (j // nh, kk, j % nh))
        else:
            b_spec = pl.BlockSpec((tk, tn), lambda i, j, kk: (kk, j))
    if out_stacked:
        out_shape = jax.ShapeDtypeStruct((N_CHIPS, m, n_c), out_dtype)
        o_spec = pl.BlockSpec((None, tm, tn), lambda i, j, kk: (j // nb, i, j % nb))
    else:
        out_shape = jax.ShapeDtypeStruct((m, n), out_dtype)
        o_spec = pl.BlockSpec((tm, tn), lambda i, j, kk: (i, j))
    in_specs = [a_spec, b_spec]
    args = [a, b]
    if bias is not None:
        in_specs.append(pl.BlockSpec((1, tn), lambda i, j, kk: (0, j)))
        args.append(bias)
    dims = _DIMS[mode]

    def body(*refs):
        a_ref, b_ref = refs[0], refs[1]
        bias_ref = refs[2] if bias is not None else None
        o_ref, acc_ref = refs[-2], refs[-1]
        kk = pl.program_id(2)

        @pl.when(kk == 0)
        def _():
            acc_ref[...] = jnp.zeros_like(acc_ref)

        acc_ref[...] += lax.dot_general(a_ref[...], b_ref[...], dims, preferred_element_type=F32)

        @pl.when(kk == nk - 1)
        def _():
            r = acc_ref[...]
            if bias_ref is not None:
                r = r + bias_ref[...]
            o_ref[...] = r.astype(o_ref.dtype)

    return pl.pallas_call(
        body, name=name, grid=grid, in_specs=in_specs, out_specs=o_spec, out_shape=out_shape,
        scratch_shapes=[pltpu.VMEM((tm, tn), F32)],
        compiler_params=_params(("parallel", "parallel", "arbitrary")),
    )(*args)


def _row_grid(bsz, seq, ts):
    ns = seq // ts
    return (bsz, ns), ns


def _rows(ts, width, ns, col=0):
    return pl.BlockSpec((ts, width), lambda b, s: (b * ns + s, col))


def _mod_spec():
    return pl.BlockSpec((None, 9, D_MODEL), lambda b, s: (b, 0, 0))


def _vec_spec(rows=1, width=D_MODEL):
    return pl.BlockSpec((rows, width), lambda b, s: (0, 0))


def _silu_pad(c):
    bsz = c.shape[0]

    def body(c_ref, o_ref):
        o_ref[...] = jnp.zeros_like(o_ref)
        o_ref[0:bsz, :] = _silu(c_ref[...]).astype(BF16)

    return pl.pallas_call(body, name="silu_pad", out_shape=jax.ShapeDtypeStruct((16, D_MODEL), BF16))(c)


def _mod_in(x, mod, bsz, seq, sub):
    ts = _pick(seq, (512, 256, 128))
    grid, ns = _row_grid(bsz, seq, ts)

    def body(x_ref, mod_ref, u_ref):
        sh = mod_ref[3 * sub:3 * sub + 1, :]
        sc = mod_ref[3 * sub + 1:3 * sub + 2, :]
        u_ref[...] = (x_ref[...] * (1.0 + sc) + sh).astype(BF16)

    return pl.pallas_call(
        body, name=f"mod_in{sub}", grid=grid, in_specs=[_rows(ts, D_MODEL, ns), _mod_spec()],
        out_specs=_rows(ts, D_MODEL, ns), out_shape=jax.ShapeDtypeStruct(x.shape, BF16),
        compiler_params=_params(("parallel", "parallel")),
    )(x, mod)


_FFN_TN = D_FF // 2


def _ffn_up_act(u, w_gu, name):
    t = u.shape[0]
    tm = _pick(t, (512, 256, 128))
    tn = _FFN_TN

    def body(u_ref, wa_ref, wg_ref, h_ref, p_ref):
        u_v = u_ref[...]
        a = jnp.dot(u_v, wa_ref[...], preferred_element_type=F32)
        g = jnp.dot(u_v, wg_ref[...], preferred_element_type=F32)
        h_ref[0] = a.astype(BF16)
        h_ref[1] = g.astype(BF16)
        p_ref[...] = (_silu(a) * g).astype(BF16)

    return pl.pallas_call(
        body, name=name, grid=(2, t // tm),
        in_specs=[pl.BlockSpec((tm, D_MODEL), lambda j, i: (i, 0)),
                  pl.BlockSpec((None, D_MODEL, tn), lambda j, i: (j, 0, 0)),
                  pl.BlockSpec((None, D_MODEL, tn), lambda j, i: (j + 2, 0, 0))],
        out_specs=[pl.BlockSpec((2, tm, tn), lambda j, i: (0, i, j)),
                   pl.BlockSpec((tm, tn), lambda j, i: (i, j))],
        out_shape=[jax.ShapeDtypeStruct((2, t, D_FF), BF16), jax.ShapeDtypeStruct((t, D_FF), BF16)],
        compiler_params=_params(("parallel", "parallel")),
    )(u, w_gu, w_gu)


def _ffn_down_bwd_act(df, w_down, h, name):
    t = df.shape[0]
    tm = _pick(t, (512, 256, 128))
    tn = _FFN_TN

    def body(df_ref, wd_ref, h_ref, dh_ref):
        dp = lax.dot_general(df_ref[...], wd_ref[...], _DIMS["nt"], preferred_element_type=F32)
        a = h_ref[0].astype(F32)
        g = h_ref[1].astype(F32)
        dh_ref[0] = (dp * g * _dsilu(a)).astype(BF16)
        dh_ref[1] = (dp * _silu(a)).astype(BF16)

    blk = pl.BlockSpec((2, tm, tn), lambda j, i: (0, i, j))
    return pl.pallas_call(
        body, name=name, grid=(2, t // tm),
        in_specs=[pl.BlockSpec((tm, D_MODEL), lambda j, i: (i, 0)),
                  pl.BlockSpec((tn, D_MODEL), lambda j, i: (j, 0)), blk],
        out_specs=blk, out_shape=jax.ShapeDtypeStruct((2, t, D_FF), BF16),
        compiler_params=_params(("parallel", "parallel")),
    )(df, w_down, h)


def _res_ln_fwd(x, f, mod, ln_g, ln_b, bsz, seq, sub, weight, target=None):
    ts = _pick(seq, (256, 128))
    grid, ns = _row_grid(bsz, seq, ts)
    last = target is not None

    def body(*refs):
        x_ref, f_ref, mod_ref, g_ref, b_ref = refs[:5]
        gate = mod_ref[3 * sub + 2:3 * sub + 3, :]
        r = ALPHA * x_ref[...] + gate * (weight * f_ref[...])
        xhat, _ = _ln_stats(r)
        xo = xhat * g_ref[...] + b_ref[...]
        if last:
            t_ref, r_ref, dy_ref, loss_ref = refs[5:]
            diff = xo - t_ref[...]
            dy_ref[...] = diff * (1.0 / D_MODEL)
            part = 0.5 * jnp.sum(jnp.mean(diff * diff, axis=-1, keepdims=True), axis=0, keepdims=True)

            @pl.when((pl.program_id(0) == 0) & (pl.program_id(1) == 0))
            def _():
                loss_ref[...] = jnp.zeros_like(loss_ref)

            loss_ref[...] += jnp.broadcast_to(part, loss_ref.shape)
        else:
            r_ref, xo_ref, u_ref = refs[5:]
            xo_ref[...] = xo
            sh = mod_ref[3 * sub + 3:3 * sub + 4, :]
            sc = mod_ref[3 * sub + 4:3 * sub + 5, :]
            u_ref[...] = (xo * (1.0 + sc) + sh).astype(BF16)
        r_ref[...] = r

    row = _rows(ts, D_MODEL, ns)
    in_specs = [row, row, _mod_spec(), _vec_spec(), _vec_spec()]
    args = [x, f, mod, ln_g, ln_b]
    if last:
        in_specs.append(row)
        args.append(target)
        out_specs = [row, row, _vec_spec(8, LANES)]
        out_shape = [jax.ShapeDtypeStruct(x.shape, F32), jax.ShapeDtypeStruct(x.shape, F32),
                     jax.ShapeDtypeStruct((8, LANES), F32)]
        sem = ("arbitrary", "arbitrary")
    else:
        out_specs = [row, row, row]
        out_shape = [jax.ShapeDtypeStruct(x.shape, F32), jax.ShapeDtypeStruct(x.shape, F32),
                     jax.ShapeDtypeStruct(x.shape, BF16)]
        sem = ("parallel", "parallel")
    return pl.pallas_call(
        body, name=f"res_ln_fwd{sub}", grid=grid, in_specs=in_specs, out_specs=out_specs, out_shape=out_shape,
        compiler_params=_params(sem),
    )(*args)


def _res_ln_bwd(r, dxo, f, mod, ln_g, bsz, seq, sub, weight):
    ts = _pick(seq, (256, 128))
    grid, ns = _row_grid(bsz, seq, ts)

    def body(r_ref, dxo_ref, f_ref, mod_ref, g_ref, dxres_ref, df_ref, lns_ref, gs_ref):
        b, s = pl.program_id(0), pl.program_id(1)
        gate = mod_ref[3 * sub + 2:3 * sub + 3, :]
        xhat, rstd = _ln_stats(r_ref[...])
        dxo_v = dxo_ref[...]
        dxhat = dxo_v * g_ref[...]
        m1 = jnp.mean(dxhat, axis=-1, keepdims=True)
        m2 = jnp.mean(dxhat * xhat, axis=-1, keepdims=True)
        dr = rstd * (dxhat - m1 - xhat * m2)
        dxres_ref[...] = ALPHA * dr
        df_ref[...] = (dr * (gate * weight)).astype(BF16)

        @pl.when((b == 0) & (s == 0))
        def _():
            lns_ref[...] = jnp.zeros_like(lns_ref)

        @pl.when(s == 0)
        def _():
            gs_ref[...] = jnp.zeros_like(gs_ref)

        lns_ref[0:1, :] += _colsum(dxo_v * xhat)
        lns_ref[1:2, :] += _colsum(dxo_v)
        gs_ref[0:1, :] += _colsum(dr * (weight * f_ref[...]))

    row = _rows(ts, D_MODEL, ns)
    return pl.pallas_call(
        body, name=f"res_ln_bwd{sub}", grid=grid,
        in_specs=[row, row, row, _mod_spec(), _vec_spec()],
        out_specs=[row, row, _vec_spec(8), pl.BlockSpec((None, 8, D_MODEL), lambda b, s: (b, 0, 0))],
        out_shape=[jax.ShapeDtypeStruct(r.shape, F32), jax.ShapeDtypeStruct(r.shape, BF16),
                   jax.ShapeDtypeStruct((8, D_MODEL), F32), jax.ShapeDtypeStruct((bsz, 8, D_MODEL), F32)],
        compiler_params=_params(("arbitrary", "arbitrary")),
    )(r, dxo, f, mod, ln_g)


def _mod_bwd(dxres, du, x, mod, bsz, seq, sub):
    ts = _pick(seq, (256, 128))
    grid, ns = _row_grid(bsz, seq, ts)

    def body(dxres_ref, du_ref, x_ref, mod_ref, dx_ref, st_ref):
        s = pl.program_id(1)
        sc = mod_ref[3 * sub + 1:3 * sub + 2, :]
        du_v = du_ref[...]
        dx_ref[...] = dxres_ref[...] + du_v * (1.0 + sc)

        @pl.when(s == 0)
        def _():
            st_ref[...] = jnp.zeros_like(st_ref)

        st_ref[0:1, :] += _colsum(du_v)
        st_ref[1:2, :] += _colsum(du_v * x_ref[...])

    row = _rows(ts, D_MODEL, ns)
    return pl.pallas_call(
        body, name=f"mod_bwd{sub}", grid=grid, in_specs=[row, row, row, _mod_spec()],
        out_specs=[row, pl.BlockSpec((None, 8, D_MODEL), lambda b, s: (b, 0, 0))],
        out_shape=[jax.ShapeDtypeStruct(x.shape, F32), jax.ShapeDtypeStruct((bsz, 8, D_MODEL), F32)],
        compiler_params=_params(("parallel", "arbitrary")),
    )(dxres, du, x, mod)


_COL_GLU_A, _COL_GLU_B, _COL_GATE_A, _COL_GATE_B = 3, 4, 5, 6


def _merge_fwd(proj, ysb, yconv, bsz, seq):
    ts = _pick(seq, (512, 256, 128))
    grid, ns = _row_grid(bsz, seq, ts)

    def body(ga_ref, gb_ref, ysb_ref, yc_ref, o_ref):
        o_ref[...] = (_sigmoid(ga_ref[...]) * ysb_ref[...] + _sigmoid(gb_ref[...]) * yc_ref[...]).astype(BF16)

    row = _rows(ts, D_MODEL, ns)
    return pl.pallas_call(
        body, name="merge_fwd", grid=grid,
        in_specs=[_rows(ts, D_MODEL, ns, _COL_GATE_A), _rows(ts, D_MODEL, ns, _COL_GATE_B), row, row],
        out_specs=row, out_shape=jax.ShapeDtypeStruct(ysb.shape, BF16),
        compiler_params=_params(("parallel", "parallel")),
    )(proj, proj, ysb, yconv)


def _merge_bwd(proj, ysb, yconv, dmerged, bsz, seq):
    ts = _pick(seq, (256, 128))
    grid, ns = _row_grid(bsz, seq, ts)

    def body(ga_ref, gb_ref, ysb_ref, yc_ref, dm_ref, dysb_ref, dyc_ref, dg_ref):
        sa = _sigmoid(ga_ref[...])
        sb = _sigmoid(gb_ref[...])
        dm = dm_ref[...]
        dysb_ref[...] = (dm * sa).astype(BF16)
        dyc_ref[...] = (dm * sb).astype(BF16)
        dg_ref[:, :D_MODEL] = (dm * ysb_ref[...] * sa * (1.0 - sa)).astype(BF16)
        dg_ref[:, D_MODEL:] = (dm * yc_ref[...] * sb * (1.0 - sb)).astype(BF16)

    row = _rows(ts, D_MODEL, ns)
    t = ysb.shape[0]
    return pl.pallas_call(
        body, name="merge_bwd", grid=grid,
        in_specs=[_rows(ts, D_MODEL, ns, _COL_GATE_A), _rows(ts, D_MODEL, ns, _COL_GATE_B), row, row, row],
        out_specs=[row, row, _rows(ts, 2 * D_MODEL, ns)],
        out_shape=[jax.ShapeDtypeStruct((t, D_MODEL), BF16), jax.ShapeDtypeStruct((t, D_MODEL), BF16),
                   jax.ShapeDtypeStruct((t, 2 * D_MODEL), BF16)],
        compiler_params=_params(("parallel", "parallel")),
    )(proj, proj, ysb, yconv, dmerged)


_CONV_ROWS = 128


def _halo_prev(tt, ns, col):
    r = tt // HALO
    return pl.BlockSpec((HALO, D_MODEL), lambda b, s: (jnp.maximum((b * ns + s) * r - 1, 0), col))


def _halo_next(tt, ns, nblk, col):
    r = tt // HALO
    return pl.BlockSpec((HALO, D_MODEL), lambda b, s: (jnp.minimum((b * ns + s + 1) * r, nblk - 1), col))


def _windows(pad_ref, stage_ref, tt, offsets):
    for r in range(8):
        mine = [o for o in offsets if o % 8 == r]
        if not mine:
            continue
        n = max(mine) - r + tt
        stage_ref[0:n, :] = pad_ref[r:r + n, :]
        for o in mine:
            yield o, stage_ref[o - r:o - r + tt, :]


def _fill_hc(hpad, a_ref, b_ref, ha_ref, hb_ref, s):
    halo = ha_ref[...] * _sigmoid(hb_ref[...])
    hpad[0:HALO, :] = jnp.where(s > 0, halo, 0.0)
    hpad[HALO:, :] = a_ref[...] * _sigmoid(b_ref[...])


def _conv_fwd(proj, conv_w, conv_b, ln_g, ln_b, bsz, seq):
    tt = _CONV_ROWS
    grid, ns = _row_grid(bsz, seq, tt)
    off = HALO - (CONV_TAPS - 1)

    def body(a_ref, b_ref, ha_ref, hb_ref, w_ref, cb_ref, g_ref, bb_ref, cs_ref, cv_ref, hpad, stage):
        _fill_hc(hpad, a_ref, b_ref, ha_ref, hb_ref, pl.program_id(1))
        acc = jnp.zeros((tt, D_MODEL), F32)
        for o, win in _windows(hpad, stage, tt, [off + j for j in range(CONV_TAPS)]):
            acc = acc + w_ref[o - off:o - off + 1, :] * win
        cv = acc + cb_ref[...]
        cv_ref[...] = cv
        xhat, _ = _ln_stats(cv)
        cs_ref[...] = _silu(xhat * g_ref[...] + bb_ref[...]).astype(BF16)

    row = _rows(tt, D_MODEL, ns)
    t = proj.shape[0]
    return pl.pallas_call(
        body, name="conv_fwd", grid=grid,
        in_specs=[_rows(tt, D_MODEL, ns, _COL_GLU_A), _rows(tt, D_MODEL, ns, _COL_GLU_B),
                  _halo_prev(tt, ns, _COL_GLU_A), _halo_prev(tt, ns, _COL_GLU_B),
                  _vec_spec(32), _vec_spec(), _vec_spec(), _vec_spec()],
        out_specs=[row, row],
        out_shape=[jax.ShapeDtypeStruct((t, D_MODEL), BF16), jax.ShapeDtypeStruct((t, D_MODEL), F32)],
        scratch_shapes=[pltpu.VMEM((HALO + tt, D_MODEL), F32), pltpu.VMEM((HALO + tt, D_MODEL), F32)],
        compiler_params=_params(("parallel", "parallel")),
    )(proj, proj, proj, proj, conv_w, conv_b, ln_g, ln_b)


def _conv_bwd_ln(dcs, cv, ln_g, ln_b, bsz, seq):
    ts = _pick(seq, (256, 128))
    grid, ns = _row_grid(bsz, seq, ts)

    def body(dcs_ref, cv_ref, g_ref, b_ref, dcv_ref, st_ref):
        xhat, rstd = _ln_stats(cv_ref[...])
        cl = xhat * g_ref[...] + b_ref[...]
        dcl = dcs_ref[...] * _dsilu(cl)
        dxhat = dcl * g_ref[...]
        m1 = jnp.mean(dxhat, axis=-1, keepdims=True)
        m2 = jnp.mean(dxhat * xhat, axis=-1, keepdims=True)
        dcv = rstd * (dxhat - m1 - xhat * m2)
        dcv_ref[...] = dcv

        @pl.when((pl.program_id(0) == 0) & (pl.program_id(1) == 0))
        def _():
            st_ref[...] = jnp.zeros_like(st_ref)

        st_ref[0:1, :] += _colsum(dcl * xhat)
        st_ref[1:2, :] += _colsum(dcl)
        st_ref[2:3, :] += _colsum(dcv)

    row = _rows(ts, D_MODEL, ns)
    return pl.pallas_call(
        body, name="conv_bwd_ln", grid=grid, in_specs=[row, row, _vec_spec(), _vec_spec()],
        out_specs=[row, _vec_spec(8)],
        out_shape=[jax.ShapeDtypeStruct(cv.shape, F32), jax.ShapeDtypeStruct((8, D_MODEL), F32)],
        compiler_params=_params(("arbitrary", "arbitrary")),
    )(dcs, cv, ln_g, ln_b)


def _conv_bwd_taps(proj, dcv, conv_w, bsz, seq):
    tt = _CONV_ROWS
    grid, ns = _row_grid(bsz, seq, tt)
    off = HALO - (CONV_TAPS - 1)
    nblk = proj.shape[0] // HALO

    def body(a_ref, b_ref, ha_ref, hb_ref, d_ref, dn_ref, w_ref, dglu_ref, dw_ref, hpad, dpad, stage):
        s = pl.program_id(1)
        _fill_hc(hpad, a_ref, b_ref, ha_ref, hb_ref, s)
        dcv = d_ref[...]
        dpad[0:tt, :] = dcv
        dpad[tt:, :] = jnp.where(s < ns - 1, dn_ref[...], 0.0)

        @pl.when((pl.program_id(0) == 0) & (s == 0))
        def _():
            dw_ref[...] = jnp.zeros_like(dw_ref)

        dhc = jnp.zeros((tt, D_MODEL), F32)
        for o, win in _windows(dpad, stage, tt, list(range(CONV_TAPS))):
            j = CONV_TAPS - 1 - o
            dhc = dhc + w_ref[j:j + 1, :] * win
        for o, win in _windows(hpad, stage, tt, [off + j for j in range(CONV_TAPS)]):
            dw_ref[o - off:o - off + 1, :] += _colsum(dcv * win)
        sb = _sigmoid(b_ref[...])
        dglu_ref[:, :D_MODEL] = (dhc * sb).astype(BF16)
        dglu_ref[:, D_MODEL:] = (dhc * a_ref[...] * sb * (1.0 - sb)).astype(BF16)

    t = proj.shape[0]
    return pl.pallas_call(
        body, name="conv_bwd_taps", grid=grid,
        in_specs=[_rows(tt, D_MODEL, ns, _COL_GLU_A), _rows(tt, D_MODEL, ns, _COL_GLU_B),
                  _halo_prev(tt, ns, _COL_GLU_A), _halo_prev(tt, ns, _COL_GLU_B),
                  _rows(tt, D_MODEL, ns), _halo_next(tt, ns, nblk, 0), _vec_spec(32)],
        out_specs=[_rows(tt, 2 * D_MODEL, ns), _vec_spec(32)],
        out_shape=[jax.ShapeDtypeStruct((t, 2 * D_MODEL), BF16), jax.ShapeDtypeStruct((32, D_MODEL), F32)],
        scratch_shapes=[pltpu.VMEM((HALO + tt, D_MODEL), F32), pltpu.VMEM((tt + HALO, D_MODEL), F32),
                        pltpu.VMEM((HALO + tt, D_MODEL), F32)],
        compiler_params=_params(("arbitrary", "arbitrary")),
    )(proj, proj, proj, proj, dcv, dcv, conv_w)


_NT = (((1,), (1,)), ((), ()))
_TN = (((0,), (0,)), ((), ()))


def _dot(a, b, dims=None):
    if dims is None:
        return jnp.dot(a, b, preferred_element_type=F32)
    return lax.dot_general(a, b, dims, preferred_element_type=F32)


def _tri_dot(v, tri2):
    hi = v.astype(BF16)
    lo = (v - hi.astype(F32)).astype(BF16)
    return _dot(jnp.concatenate([hi, lo], axis=1), tri2)


def _tri2(mask):
    t = mask.astype(BF16)
    return jnp.concatenate([t, t], axis=0)


def _softplus_parts(z):
    t = jnp.exp(-jnp.abs(z))
    den = 1.0 + t
    return jnp.maximum(z, 0.0) + jnp.log(den), t, den


def _attn_fwd(proj, bsz, seq):
    blk = ATT_BLOCK
    nq = seq // blk
    n_pairs = D_MODEL // LANES

    def body(q_ref, k_ref, v_ref, y_ref, rt_ref, zr_buf, ns_buf, run_buf, acc_buf):
        qi = pl.program_id(2)
        lane = lax.broadcasted_iota(jnp.int32, (blk, LANES), 1)
        first = lane < HEAD_DIM
        q2 = q_ref[...] * 0.125
        q_heads = (jnp.where(first, q2, 0.0).astype(BF16), jnp.where(first, 0.0, q2).astype(BF16))
        rr = lax.broadcasted_iota(jnp.int32, (blk, blk), 0)
        cc = lax.broadcasted_iota(jnp.int32, (blk, blk), 1)
        tri_ge = _tri2(rr >= cc)
        causal = cc < rr

        def scores(kb, slot, masked, heads=(0, 1)):
            k_blk = k_ref[pl.ds(pl.multiple_of(kb * blk, blk), blk), :].astype(BF16)
            for h in heads:
                z = _dot(q_heads[h], k_blk, _NT)
                if masked:
                    z = jnp.where(causal, z, -1e30)
                sp, _, _ = _softplus_parts(z)
                neg = -sp
                zr_buf[slot, h] = z + _tri_dot(neg, tri_ge)
                ns_buf[slot, h] = jnp.sum(neg, axis=1, keepdims=True)

        def weigh(kb, slot, heads=(0, 1)):
            v_blk = v_ref[pl.ds(pl.multiple_of(kb * blk, blk), blk), :].astype(BF16)
            for h in heads:
                run = run_buf[h]
                w = jnp.exp(zr_buf[slot, h] + run)
                acc_buf[h] += _dot(w.astype(BF16), v_blk)
                run_buf[h] = run + ns_buf[slot, h]

        def step(kb_next, kb, slot):
            for h in range(2):
                scores(kb_next, 1 - slot, False, (h,))
                weigh(kb, slot, (h,))

        run_buf[...] = jnp.zeros_like(run_buf)
        acc_buf[...] = jnp.zeros_like(acc_buf)
        scores(qi, 0, True)

        def two_steps(p, carry):
            t = 2 * p
            step(qi - t - 1, qi - t, 0)
            step(qi - t - 2, qi - t - 1, 1)
            return carry

        lax.fori_loop(0, qi // 2, two_steps, 0)

        @pl.when(qi % 2 == 1)
        def _():
            step(0, 1, 0)
            weigh(0, 1)

        @pl.when(qi % 2 == 0)
        def _():
            weigh(0, 0)

        y_ref[...] = jnp.where(first, acc_buf[0], acc_buf[1]).astype(BF16)
        rt_ref[...] = jnp.where(first, jnp.broadcast_to(run_buf[0], (blk, LANES)),
                                jnp.broadcast_to(run_buf[1], (blk, LANES)))

    t = proj.shape[0]
    q_spec = pl.BlockSpec((blk, LANES), lambda b, p, i: (b * nq + i, p))
    return pl.pallas_call(
        body, name="attn_fwd", grid=(bsz, n_pairs, nq),
        in_specs=[q_spec,
                  pl.BlockSpec((seq, LANES), lambda b, p, i: (b, n_pairs + p)),
                  pl.BlockSpec((seq, LANES), lambda b, p, i: (b, 2 * n_pairs + p))],
        out_specs=[q_spec, q_spec],
        out_shape=[jax.ShapeDtypeStruct((t, D_MODEL), BF16), jax.ShapeDtypeStruct((t, D_MODEL), F32)],
        scratch_shapes=[pltpu.VMEM((2, 2, blk, blk), F32), pltpu.VMEM((2, 2, blk, 1), F32),
                        pltpu.VMEM((2, blk, 1), F32), pltpu.VMEM((2, blk, LANES), F32)],
        compiler_params=_params(("parallel", "parallel", "arbitrary")),
    )(proj, proj, proj)


def _attn_bwd(proj, rtot, dy, bsz, seq):
    blk = ATT_BLOCK
    nq = seq // blk
    n_pairs = D_MODEL // LANES

    def body(q_ref, k_ref, v_ref, dy_ref, rt_ref, dq_ref, dk_ref, dv_ref, dk_acc, dv_acc,
             a_buf, sig_buf, dw_buf, ns_buf, pre_buf, es_buf, dq_buf):
        qi = pl.program_id(2)

        @pl.when(qi == 0)
        def _():
            dk_acc[...] = jnp.zeros_like(dk_acc)
            dv_acc[...] = jnp.zeros_like(dv_acc)

        lane = lax.broadcasted_iota(jnp.int32, (blk, LANES), 1)
        first = lane < HEAD_DIM
        head_row = lax.broadcasted_iota(jnp.int32, (LANES, blk), 0) < HEAD_DIM
        q2 = q_ref[...] * 0.125
        q_rows = (jnp.where(first, q2, 0.0).astype(BF16), jnp.where(first, 0.0, q2).astype(BF16))
        q_t = q2.T
        q_heads = (jnp.where(head_row, q_t, 0.0).astype(BF16), jnp.where(head_row, 0.0, q_t).astype(BF16))
        dy2 = dy_ref[...].astype(F32)
        dy_rows = (jnp.where(first, dy2, 0.0).astype(BF16), jnp.where(first, 0.0, dy2).astype(BF16))
        dy_t = dy2.T
        dy_heads = (jnp.where(head_row, dy_t, 0.0).astype(BF16), jnp.where(head_row, 0.0, dy_t).astype(BF16))
        rt_t = rt_ref[...].T
        rt = (rt_t[0:1, :], rt_t[HEAD_DIM:HEAD_DIM + 1, :])
        rr = lax.broadcasted_iota(jnp.int32, (blk, blk), 0)
        cc = lax.broadcasted_iota(jnp.int32, (blk, blk), 1)
        lower = (cc < rr).astype(BF16)
        lower_eq = (cc <= rr).astype(BF16)
        tri_lt = jnp.concatenate([lower, lower], axis=1)
        tri_le = jnp.concatenate([lower_eq, lower_eq], axis=1)
        causal = rr < cc

        def tri_left(tri2, v):
            hi = v.astype(BF16)
            lo = (v - hi.astype(F32)).astype(BF16)
            return _dot(tri2, jnp.concatenate([hi, lo], axis=0))

        def scores(kb, slot, heads=(0, 1)):
            rows = pl.ds(pl.multiple_of(kb * blk, blk), blk)
            k_blk = k_ref[rows, :].astype(BF16)
            v_blk = v_ref[rows, :].astype(BF16)
            keep = jnp.logical_or(causal, kb < qi)
            for h in heads:
                z = jnp.where(keep, _dot(k_blk, q_heads[h]), -1e30)
                sp, t, den = _softplus_parts(z)
                neg = -sp
                a_buf[slot, h] = z - tri_left(tri_lt, neg)
                sig_buf[slot, h] = jnp.where(z >= 0, 1.0, t) / den
                ns_buf[slot, h] = jnp.sum(neg, axis=0, keepdims=True)
                dw_buf[slot, h] = _dot(v_blk, dy_heads[h])

        def finish(kb, slot, heads=(0, 1)):
            rows = pl.ds(pl.multiple_of(kb * blk, blk), blk)
            k_t = k_ref[rows, :].T.astype(BF16)
            for h in heads:
                pre, esum = pre_buf[h], es_buf[h]
                w = jnp.exp(a_buf[slot, h] + (rt[h] - pre))
                e = dw_buf[slot, h] * w
                dz = e - sig_buf[slot, h] * (esum + tri_left(tri_le, e))
                pre_buf[h] = pre + ns_buf[slot, h]
                es_buf[h] = esum + jnp.sum(e, axis=0, keepdims=True)
                dzb = dz.astype(BF16)
                dq_buf[h] += _dot(k_t, dzb)
                dk_acc[rows, :] += _dot(dzb, q_rows[h])
                dv_acc[rows, :] += _dot(w.astype(BF16), dy_rows[h])

        def step(kb_next, kb, slot):
            for h in range(2):
                scores(kb_next, 1 - slot, (h,))
                finish(kb, slot, (h,))

        pre_buf[...] = jnp.zeros_like(pre_buf)
        es_buf[...] = jnp.zeros_like(es_buf)
        dq_buf[...] = jnp.zeros_like(dq_buf)
        scores(0, 0)

        def two_steps(p, carry):
            t = 2 * p
            step(t + 1, t, 0)
            step(t + 2, t + 1, 1)
            return carry

        lax.fori_loop(0, qi // 2, two_steps, 0)

        @pl.when(qi % 2 == 1)
        def _():
            step(qi, qi - 1, 0)
            finish(qi, 1)

        @pl.when(qi % 2 == 0)
        def _():
            finish(qi, 0)

        dq_ref[...] = (jnp.where(head_row, dq_buf[0], dq_buf[1]).T * 0.125).astype(BF16)

        @pl.when(qi == nq - 1)
        def _():
            dk_ref[...] = dk_acc[...].astype(BF16)
            dv_ref[...] = dv_acc[...].astype(BF16)

    t = proj.shape[0]
    q_spec = pl.BlockSpec((blk, LANES), lambda b, p, i: (b * nq + i, p))
    kv_out = pl.BlockSpec((seq, LANES), lambda b, p, i: (b, p))
    out = jax.ShapeDtypeStruct((t, D_MODEL), BF16)
    return pl.pallas_call(
        body, name="attn_bwd", grid=(bsz, n_pairs, nq),
        in_specs=[q_spec,
                  pl.BlockSpec((seq, LANES), lambda b, p, i: (b, n_pairs + p)),
                  pl.BlockSpec((seq, LANES), lambda b, p, i: (b, 2 * n_pairs + p)),
                  q_spec, q_spec],
        out_specs=[q_spec, kv_out, kv_out], out_shape=[out, out, out],
        scratch_shapes=[pltpu.VMEM((seq, LANES), F32), pltpu.VMEM((seq, LANES), F32),
                        pltpu.VMEM((2, 2, blk, blk), F32), pltpu.VMEM((2, 2, blk, blk), F32),
                        pltpu.VMEM((2, 2, blk, blk), F32), pltpu.VMEM((2, 2, 1, blk), F32),
                        pltpu.VMEM((2, 1, blk), F32), pltpu.VMEM((2, 1, blk), F32),
                        pltpu.VMEM((2, LANES, blk), F32)],
        compiler_params=_params(("parallel", "parallel", "arbitrary")),
    )(proj, proj, proj, dy, rtot)


def _adamw(w, g, m, v, name):
    rows, cols = w.shape
    tr = _pick(rows, (256, 352, 128, 64, 32, 16, 8))
    c1 = 1.0 - ADAM_B1 ** ADAM_STEP
    c2 = 1.0 - ADAM_B2 ** ADAM_STEP

    def body(w_ref, g_ref, m_ref, v_ref, d_ref, nm_ref, nv_ref):
        g_v = g_ref[...]
        nm = ADAM_B1 * m_ref[...] + (1.0 - ADAM_B1) * g_v
        nv = ADAM_B2 * v_ref[...] + (1.0 - ADAM_B2) * (g_v * g_v)
        nm_ref[...] = nm
        nv_ref[...] = nv
        d_ref[...] = -ADAM_LR * ((nm / c1) / (jnp.sqrt(nv / c2) + ADAM_EPS) + ADAM_WD * w_ref[...])

    spec = pl.BlockSpec((tr, cols), lambda i: (i, 0))
    shape = jax.ShapeDtypeStruct(w.shape, F32)
    return pl.pallas_call(
        body, name=name, grid=(rows // tr,), in_specs=[spec] * 4, out_specs=[spec] * 3, out_shape=[shape] * 3,
        compiler_params=_params(("parallel",)),
    )(w, g, m, v)


def _ffn_fwd(u, w_gu, w_down, bsz, seq, tag):
    h, p = _ffn_up_act(u, w_gu, f"{tag}_up")
    f = _matmul(p, w_down, mode="nn", out_dtype=F32, name=f"{tag}_down")
    return h, p, f


def _ffn_bwd(df, u, h, p, w_gu, w_down, bsz, seq, tag):
    dh = _ffn_down_bwd_act(df, w_down, h, f"{tag}_ddown")
    g_down = _matmul(p, df, mode="tn", out_dtype=F32, name=f"{tag}_gdown")
    g_gu = _matmul(u, dh, mode="tn", out_dtype=F32, name=f"{tag}_ggu", out_stacked=True)
    du = _matmul(dh, w_gu, mode="nt", out_dtype=F32, name=f"{tag}_dup")
    return du, g_gu, g_down


def _local_step(x, c, target, wts, vecs):
    bsz, seq, _ = x.shape
    t = bsz * seq
    x0 = x.reshape(t, D_MODEL)
    tgt = target.reshape(t, D_MODEL)

    sc = _silu_pad(c)
    mod16 = _matmul(sc, wts["w_ada"], mode="nn", out_dtype=F32, name="ada_fwd", bias=vecs["b_ada"])
    mod = mod16[:bsz].reshape(bsz, 9, D_MODEL)

    u1 = _mod_in(x0, mod, bsz, seq, 0)
    h1, p1, f1 = _ffn_fwd(u1, wts["ffn1_w_gu"], wts["ffn1_w_down"], bsz, seq, "ffn1")
    r1, x1, u2 = _res_ln_fwd(x0, f1, mod, vecs["ln1_g"], vecs["ln1_b"], bsz, seq, 0, 0.5)

    proj = _matmul(u2, wts["w_in"], mode="nn", out_dtype=F32, name="mix_in")
    ya, rtot = _attn_fwd(proj, bsz, seq)
    cs, cv = _conv_fwd(proj, wts["conv_w"], vecs["conv_b"], vecs["conv_ln_g"], vecs["conv_ln_b"], bsz, seq)
    ysb = _matmul(ya, wts["w_sb_out"], mode="nn", out_dtype=F32, name="sb_out")
    yconv = _matmul(cs, wts["w_conv_out"], mode="nn", out_dtype=F32, name="conv_out")
    merged = _merge_fwd(proj, ysb, yconv, bsz, seq)
    o2 = _matmul(merged, wts["w_out"], mode="nn", out_dtype=F32, name="mix_out")
    r2, x2, u3 = _res_ln_fwd(x1, o2, mod, vecs["ln2_g"], vecs["ln2_b"], bsz, seq, 1, 1.0)

    h3, p3, f3 = _ffn_fwd(u3, wts["ffn2_w_gu"], wts["ffn2_w_down"], bsz, seq, "ffn2")
    r3, dy, loss_blk = _res_ln_fwd(x2, f3, mod, vecs["ln3_g"], vecs["ln3_b"], bsz, seq, 2, 0.5, target=tgt)

    grads = {}
    dxres, df, ln3s, g3s = _res_ln_bwd(r3, dy, f3, mod, vecs["ln3_g"], bsz, seq, 2, 0.5)
    du, grads["ffn2_w_gu"], grads["ffn2_w_down"] = _ffn_bwd(
        df, u3, h3, p3, wts["ffn2_w_gu"], wts["ffn2_w_down"], bsz, seq, "ffn2")
    dx2, m3s = _mod_bwd(dxres, du, x2, mod, bsz, seq, 2)

    dxres, do2, ln2s, g2s = _res_ln_bwd(r2, dx2, o2, mod, vecs["ln2_g"], bsz, seq, 1, 1.0)
    dmerged = _matmul(do2, wts["w_out"], mode="nt", out_dtype=F32, name="mix_out_d")
    grads["w_out"] = _matmul(merged, do2, mode="tn", out_dtype=F32, name="mix_out_g")
    dysb, dyconv, dgate = _merge_bwd(proj, ysb, yconv, dmerged, bsz, seq)
    dya = _matmul(dysb, wts["w_sb_out"], mode="nt", out_dtype=BF16, name="sb_out_d")
    grads["w_sb_out"] = _matmul(ya, dysb, mode="tn", out_dtype=F32, name="sb_out_g")
    dcs = _matmul(dyconv, wts["w_conv_out"], mode="nt", out_dtype=F32, name="conv_out_d")
    grads["w_conv_out"] = _matmul(cs, dyconv, mode="tn", out_dtype=F32, name="conv_out_g")
    dcv, convs = _conv_bwd_ln(dcs, cv, vecs["conv_ln_g"], vecs["conv_ln_b"], bsz, seq)
    dglu, g_conv_w = _conv_bwd_taps(proj, dcv, wts["conv_w"], bsz, seq)
    dq, dk, dv = _attn_bwd(proj, rtot, dya, bsz, seq)
    dproj = jnp.concatenate([dq, dk, dv, dglu, dgate], axis=1)
    grads["w_in"] = _matmul(u2, dproj, mode="tn", out_dtype=F32, name="mix_in_g", out_stacked=True)
    du = _matmul(dproj, wts["w_in"], mode="nt", out_dtype=F32, name="mix_in_d")
    dx1, m2s = _mod_bwd(dxres, du, x1, mod, bsz, seq, 1)

    dxres, df, ln1s, g1s = _res_ln_bwd(r1, dx1, f1, mod, vecs["ln1_g"], bsz, seq, 0, 0.5)
    du, grads["ffn1_w_gu"], grads["ffn1_w_down"] = _ffn_bwd(
        df, u1, h1, p1, wts["ffn1_w_gu"], wts["ffn1_w_down"], bsz, seq, "ffn1")
    grad_x, m1s = _mod_bwd(dxres, du, x0, mod, bsz, seq, 0)

    dmod = jnp.stack([m1s[:, 0], m1s[:, 1], g1s[:, 0], m2s[:, 0], m2s[:, 1], g2s[:, 0],
                      m3s[:, 0], m3s[:, 1], g3s[:, 0]], axis=1)
    dmod16 = jnp.zeros((16, 9 * D_MODEL), F32).at[:bsz].set(dmod.reshape(bsz, 9 * D_MODEL))
    grads["w_ada"] = _matmul(sc, dmod16.astype(BF16), mode="tn", out_dtype=F32, name="ada_g", out_stacked=True)

    small = {"dmod": dmod, "ln1": ln1s, "ln2": ln2s, "ln3": ln3s, "conv": convs, "conv_w": g_conv_w,
             "loss": loss_blk}
    return grad_x.reshape(x.shape), grads, small


_HBM = pl.BlockSpec(memory_space=pltpu.HBM)


def _position():
    return lax.axis_index("x"), lax.axis_index("y"), lax.axis_index("c")


def _other_chips(x, y):
    return [(1 - x, y), (x, 1 - y), (1 - x, 1 - y)]


def _cast_into_stack(w_local, chip, name):
    rows, cols = w_local.shape
    tr = _pick(rows, (256, 352, 128, 64, 32, 16))

    def body(chip_ref, w_ref, o_ref):
        o_ref[...] = w_ref[...].astype(BF16)

    return pl.pallas_call(
        body, name=name,
        grid_spec=pltpu.PrefetchScalarGridSpec(
            num_scalar_prefetch=1, grid=(rows // tr,),
            in_specs=[pl.BlockSpec((tr, cols), lambda r, chip_ref: (r, 0))],
            out_specs=pl.BlockSpec((None, tr, cols), lambda r, chip_ref: (chip_ref[0], r, 0))),
        out_shape=jax.ShapeDtypeStruct((N_CHIPS, rows, cols), BF16),
        compiler_params=_params(("parallel",)),
    )(chip, w_local)


def _all_gather_weights(stacks, small):
    n = len(stacks)

    def body(*refs):
        ins, small_in, outs, small_out = refs[:n], refs[n], refs[n + 1:2 * n + 1], refs[2 * n + 1]
        send_sems, recv_sems, fwd_send_sems, fwd_recv_sems, small_sems = refs[2 * n + 2:]
        x, y, c = _position()
        me = 2 * x + y
        chips = _other_chips(x, y)

        def send(i, j):
            px, py = chips[j]
            return pltpu.make_async_remote_copy(
                src_ref=ins[i].at[me, c], dst_ref=outs[i].at[me, c], send_sem=send_sems.at[3 * i + j],
                recv_sem=recv_sems.at[3 * i + j], device_id=(px, py, c), device_id_type=MESH)

        def landed(i, j):
            px, py = chips[j]
            return pltpu.make_async_remote_copy(
                src_ref=ins[i].at[me, c], dst_ref=outs[i].at[2 * px + py, c], send_sem=send_sems.at[3 * i + j],
                recv_sem=recv_sems.at[3 * i + j], device_id=(px, py, c), device_id_type=MESH)

        def forward(i, j, half):
            px, py = chips[j]
            blk = outs[i].at[2 * px + py, half]
            return pltpu.make_async_remote_copy(
                src_ref=blk, dst_ref=blk, send_sem=fwd_send_sems.at[3 * i + j],
                recv_sem=fwd_recv_sems.at[3 * i + j], device_id=(x, y, 1 - c), device_id_type=MESH)

        def small_copy(j, slot):
            px, py = chips[j]
            return pltpu.make_async_remote_copy(
                src_ref=small_in, dst_ref=small_out.at[slot], send_sem=small_sems.at[j],
                recv_sem=small_sems.at[3 + j], device_id=(px, py, c), device_id_type=MESH)

        own_small = pltpu.make_async_copy(small_in, small_out.at[me], small_sems.at[6])
        own_small.start()
        for j in range(3):
            small_copy(j, me).start()
        for i in range(n):
            for j in range(3):
                send(i, j).start()
        for i in range(n):
            for j in range(3):
                landed(i, j).wait_recv()
                forward(i, j, c).start()
        for i in range(n):
            for j in range(3):
                forward(i, j, 1 - c).wait_recv()
        for j, (px, py) in enumerate(chips):
            small_copy(j, 2 * px + py).wait_recv()
        own_small.wait()
        for j in range(3):
            small_copy(j, me).wait_send()
        for i in range(n):
            for j in range(3):
                send(i, j).wait_send()
                forward(i, j, c).wait_send()

    return pl.pallas_call(
        body, name="all_gather_weights",
        out_shape=[jax.ShapeDtypeStruct(s.shape, s.dtype) for s in stacks]
        + [jax.ShapeDtypeStruct((N_CHIPS,) + small.shape, small.dtype)],
        in_specs=[_HBM] * (n + 1), out_specs=[_HBM] * (n + 1),
        input_output_aliases={i: i for i in range(n)},
        scratch_shapes=[pltpu.SemaphoreType.DMA((3 * n,)), pltpu.SemaphoreType.DMA((3 * n,)),
                        pltpu.SemaphoreType.DMA((3 * n,)), pltpu.SemaphoreType.DMA((3 * n,)),
                        pltpu.SemaphoreType.DMA((7,))],
    )(*stacks, small)


def _pair_swap(grads):
    n = len(grads)

    def body(*refs):
        ins, outs = refs[:n], refs[n:2 * n]
        send_sems, recv_sems = refs[2 * n:]
        x, y, c = _position()

        def copy(i, k):
            return pltpu.make_async_remote_copy(
                src_ref=ins[i].at[k, 1 - c], dst_ref=outs[i].at[k], send_sem=send_sems.at[N_CHIPS * i + k],
                recv_sem=recv_sems.at[N_CHIPS * i + k], device_id=(x, y, 1 - c), device_id_type=MESH)

        for i in range(n):
            for k in range(N_CHIPS):
                copy(i, k).start()
        for i in range(n):
            for k in range(N_CHIPS):
                copy(i, k).wait_recv()
        for i in range(n):
            for k in range(N_CHIPS):
                copy(i, k).wait_send()

    return pl.pallas_call(
        body, name="grad_pair_swap",
        out_shape=[jax.ShapeDtypeStruct((N_CHIPS,) + g.shape[2:], F32) for g in grads],
        in_specs=[_HBM] * n, out_specs=[_HBM] * n,
        scratch_shapes=[pltpu.SemaphoreType.DMA((N_CHIPS * n,)), pltpu.SemaphoreType.DMA((N_CHIPS * n,))],
    )(*grads)


def _pair_add(g, got, place, name):
    _, _, rh, cols = g.shape
    tr = _pick(rh, (256, 176, 128, 64, 32, 16, 8))

    def body(place_ref, g_ref, got_ref, p_ref, own_ref):
        s = g_ref[...] + got_ref[...]
        p_ref[...] = s.astype(BF16)

        @pl.when(pl.program_id(1) == place_ref[1])
        def _():
            own_ref[...] = s

    blk = pl.BlockSpec((None, tr, cols), lambda r, k, place_ref: (k, r, 0))
    return pl.pallas_call(
        body, name=name,
        grid_spec=pltpu.PrefetchScalarGridSpec(
            num_scalar_prefetch=1, grid=(rh // tr, N_CHIPS),
            in_specs=[pl.BlockSpec((None, None, tr, cols), lambda r, k, place_ref: (k, place_ref[0], r, 0)), blk],
            out_specs=[blk, pl.BlockSpec((tr, cols), lambda r, k, place_ref: (r, 0))]),
        out_shape=[jax.ShapeDtypeStruct((N_CHIPS, rh, cols), BF16), jax.ShapeDtypeStruct((rh, cols), F32)],
        compiler_params=_params(("parallel", "arbitrary")),
    )(place, g, got)


def _chip_scatter(sums):
    n = len(sums)

    def body(*refs):
        ins, outs = refs[:n], refs[n:2 * n]
        send_sems, recv_sems = refs[2 * n:]
        x, y, c = _position()
        chips = _other_chips(x, y)

        def copy(i, j):
            px, py = chips[j]
            return pltpu.make_async_remote_copy(
                src_ref=ins[i].at[2 * px + py], dst_ref=outs[i].at[j], send_sem=send_sems.at[3 * i + j],
                recv_sem=recv_sems.at[3 * i + j], device_id=(px, py, c), device_id_type=MESH)

        for i in range(n):
            for j in range(3):
                copy(i, j).start()
        for i in range(n):
            for j in range(3):
                copy(i, j).wait_recv()
        for i in range(n):
            for j in range(3):
                copy(i, j).wait_send()

    return pl.pallas_call(
        body, name="grad_chip_scatter",
        out_shape=[jax.ShapeDtypeStruct((3,) + s.shape[1:], BF16) for s in sums],
        in_specs=[_HBM] * n, out_specs=[_HBM] * n,
        scratch_shapes=[pltpu.SemaphoreType.DMA((3 * n,)), pltpu.SemaphoreType.DMA((3 * n,))],
    )(*sums)


def _chip_sum(own, parts, place, name):
    rh, cols = own.shape
    tr = _pick(rh, (256, 176, 128, 64, 32, 16, 8))

    def body(place_ref, own_ref, p_ref, o_ref):
        o_ref[...] = ((own_ref[...] + p_ref[0].astype(F32)) + p_ref[1].astype(F32)) + p_ref[2].astype(F32)

    return pl.pallas_call(
        body, name=name,
        grid_spec=pltpu.PrefetchScalarGridSpec(
            num_scalar_prefetch=1, grid=(rh // tr,),
            in_specs=[pl.BlockSpec((tr, cols), lambda r, place_ref: (r, 0)),
                      pl.BlockSpec((3, tr, cols), lambda r, place_ref: (0, r, 0))],
            out_specs=pl.BlockSpec((None, tr, cols), lambda r, place_ref: (place_ref[0], r, 0))),
        out_shape=jax.ShapeDtypeStruct((2, rh, cols), F32),
        compiler_params=_params(("parallel",)),
    )(place, own, parts)


def _pair_gather(halves):
    n = len(halves)

    def body(*refs):
        ins, outs = refs[:n], refs[n:2 * n]
        send_sems, recv_sems = refs[2 * n:]
        x, y, c = _position()

        def send(i):
            return pltpu.make_async_remote_copy(
                src_ref=ins[i].at[c], dst_ref=outs[i].at[c], send_sem=send_sems.at[i], recv_sem=recv_sems.at[i],
                device_id=(x, y, 1 - c), device_id_type=MESH)

        def landed(i):
            return pltpu.make_async_remote_copy(
                src_ref=ins[i].at[c], dst_ref=outs[i].at[1 - c], send_sem=send_sems.at[i], recv_sem=recv_sems.at[i],
                device_id=(x, y, 1 - c), device_id_type=MESH)

        for i in range(n):
            send(i).start()
        for i in range(n):
            landed(i).wait_recv()
        for i in range(n):
            send(i).wait_send()

    return pl.pallas_call(
        body, name="grad_pair_gather",
        out_shape=[jax.ShapeDtypeStruct(h.shape, F32) for h in halves],
        in_specs=[_HBM] * n, out_specs=[_HBM] * n,
        input_output_aliases={i: i for i in range(n)},
        scratch_shapes=[pltpu.SemaphoreType.DMA((n,)), pltpu.SemaphoreType.DMA((n,))],
    )(*halves)


_MOD_ROWS = 16


def _small_all_reduce(buf, bsz):
    rows, cols = buf.shape
    head = bsz * _MOD_ROWS
    out_rows = rows - head + _MOD_ROWS

    def body(in_ref, o_ref, gath, send_sems, recv_sems):
        x, y, c = _position()
        me = 4 * x + 2 * y + c

        def peer(mask):
            return (x ^ (mask >> 2), y ^ ((mask >> 1) & 1), c ^ (mask & 1))

        def copy(mask):
            return pltpu.make_async_remote_copy(
                src_ref=in_ref, dst_ref=gath.at[me], send_sem=send_sems.at[mask - 1],
                recv_sem=recv_sems.at[mask - 1], device_id=peer(mask), device_id_type=MESH)

        def arrival(mask):
            px, py, pc = peer(mask)
            return pltpu.make_async_remote_copy(
                src_ref=in_ref, dst_ref=gath.at[4 * px + 2 * py + pc], send_sem=send_sems.at[mask - 1],
                recv_sem=recv_sems.at[mask - 1], device_id=peer(mask), device_id_type=MESH)

        for mask in range(1, N_DEV):
            copy(mask).start()
        gath[me] = in_ref[...]
        for mask in range(1, N_DEV):
            arrival(mask).wait_recv()
        for mask in range(1, N_DEV):
            copy(mask).wait_send()
        acc = gath[0]
        for d in range(1, N_DEV):
            acc = acc + gath[d]
        mod = acc[0:_MOD_ROWS]
        for s in range(1, bsz):
            mod = mod + acc[s * _MOD_ROWS:(s + 1) * _MOD_ROWS]
        o_ref[0:_MOD_ROWS, :] = mod
        o_ref[_MOD_ROWS:, :] = acc[head:]

    vm = pl.BlockSpec(memory_space=pltpu.VMEM)
    return pl.pallas_call(
        body, name="small_all_reduce", in_specs=[vm], out_specs=vm,
        out_shape=jax.ShapeDtypeStruct((out_rows, cols), F32),
        scratch_shapes=[pltpu.VMEM((N_DEV, rows, cols), F32), pltpu.SemaphoreType.DMA((N_DEV - 1,)),
                        pltpu.SemaphoreType.DMA((N_DEV - 1,))],
        compiler_params=pltpu.CompilerParams(vmem_limit_bytes=VMEM_LIMIT),
    )(buf)


_COL_SHARDED = ("w_ada", "ffn1_w_gu", "w_in", "ffn2_w_gu")
_ROW_SHARDED = ("ffn1_w_down", "w_sb_out", "w_conv_out", "w_out", "ffn2_w_down")
_BIG = ("w_ada", "ffn1_w_gu", "ffn1_w_down", "w_in", "w_sb_out", "w_conv_out", "w_out", "ffn2_w_gu", "ffn2_w_down")
_VECS = ("b_ada", "ln1_g", "ln1_b", "conv_b", "conv_ln_g", "conv_ln_b", "ln2_g", "ln2_b", "ln3_g", "ln3_b")
_WEIGHTS = ("w_ada", "b_ada", "ffn1_w_gu", "ffn1_w_down", "ln1_g", "ln1_b", "w_in", "w_sb_out", "conv_w", "conv_b",
            "conv_ln_g", "conv_ln_b", "w_conv_out", "w_out", "ln2_g", "ln2_b", "ffn2_w_gu", "ffn2_w_down",
            "ln3_g", "ln3_b")


def _step(x, c, target, w, m, v):
    bsz = x.shape[0]
    chip = 2 * lax.axis_index("x") + lax.axis_index("y")
    core = lax.axis_index("c")

    chip_arr = jnp.reshape(chip, (1,)).astype(jnp.int32)
    place = jnp.stack([core, chip]).astype(jnp.int32)

    conv_w_local = jnp.pad(w["conv_w"][0], ((0, 1), (0, 0)))
    stacks = []
    for n in _BIG:
        rows, cols = w[n].shape[1:]
        s = _cast_into_stack(w[n][0], chip_arr, f"cast_{n}")
        stacks.append(s.reshape(N_CHIPS, 2, rows // 2, cols))
    gathered = _all_gather_weights(stacks, conv_w_local)
    wts = {}
    for n, g in zip(_BIG, gathered[:-1]):
        rows, cols = w[n].shape[1:]
        wts[n] = g.reshape(N_CHIPS, rows, cols) if n in _COL_SHARDED else g.reshape(N_CHIPS * rows, cols)
    wts["conv_w"] = gathered[-1].transpose(1, 0, 2).reshape(32, D_MODEL)
    vecs = {n: w[n] for n in _VECS}

    grad_x, grads, small = _local_step(x, c, target, wts, vecs)

    views = []
    for n in _BIG:
        g = grads[n]
        rows, cols = w[n].shape[1:]
        views.append(g.reshape(N_CHIPS, 2, rows // 2, cols))
    got = _pair_swap(views)
    pair_sums = [_pair_add(g, r, place, f"pair_add_{n}") for n, g, r in zip(_BIG, views, got)]
    parts = _chip_scatter([p for p, _ in pair_sums])
    halves = [_chip_sum(own, p, place, f"chip_sum_{n}") for n, (_, own), p in zip(_BIG, pair_sums, parts)]
    full = _pair_gather(halves)
    g_out = {n: f.reshape(w[n].shape[1:]) for n, f in zip(_BIG, full)}

    dmod = jnp.pad(small["dmod"], ((0, 0), (0, _MOD_ROWS - 9), (0, 0))).reshape(bsz * _MOD_ROWS, D_MODEL)
    loss_rows = jnp.pad(small["loss"], ((0, 0), (0, D_MODEL - LANES)))
    buf = jnp.concatenate([dmod, small["ln1"], small["ln2"], small["ln3"], small["conv"], small["conv_w"],
                           loss_rows], axis=0)
    red = _small_all_reduce(buf, bsz)
    o = _MOD_ROWS
    g_out["b_ada"] = red[0:9].reshape(1, 9 * D_MODEL)
    g_out["ln1_g"], g_out["ln1_b"] = red[o:o + 1], red[o + 1:o + 2]
    g_out["ln2_g"], g_out["ln2_b"] = red[o + 8:o + 9], red[o + 9:o + 10]
    g_out["ln3_g"], g_out["ln3_b"] = red[o + 16:o + 17], red[o + 17:o + 18]
    g_out["conv_ln_g"], g_out["conv_ln_b"], g_out["conv_b"] = red[o + 24:o + 25], red[o + 25:o + 26], red[o + 26:o + 27]
    cw = w["conv_w"].shape[2]
    g_out["conv_w"] = lax.dynamic_slice(red[o + 32:o + 32 + CONV_TAPS], (0, chip * cw), (CONV_TAPS, cw))
    loss = red[o + 64, 0]

    outs_g, outs_d, outs_m, outs_v = [], [], [], []
    for n in _WEIGHTS:
        shape = w[n].shape
        flat = shape[1:] if len(shape) == 3 else shape
        d, nm, nv = _adamw(w[n].reshape(flat), g_out[n].reshape(flat), m[n].reshape(flat), v[n].reshape(flat),
                           f"adamw_{n}")
        outs_g.append(g_out[n].reshape(shape))
        outs_d.append(d.reshape(shape))
        outs_m.append(nm.reshape(shape))
        outs_v.append(nv.reshape(shape))
    return (loss, grad_x, *outs_g, *outs_d, *outs_m, *outs_v)


def kernel(x, c, w_ada, b_ada, ffn1_w_gu, ffn1_w_down, ln1_g, ln1_b, w_in, w_sb_out, conv_w, conv_b, conv_ln_g, conv_ln_b, w_conv_out, w_out, ln2_g, ln2_b, ffn2_w_gu, ffn2_w_down, ln3_g, ln3_b, loss_target, m_w_ada, m_b_ada, m_ffn1_w_gu, m_ffn1_w_down, m_ln1_g, m_ln1_b, m_w_in, m_w_sb_out, m_conv_w, m_conv_b, m_conv_ln_g, m_conv_ln_b, m_w_conv_out, m_w_out, m_ln2_g, m_ln2_b, m_ffn2_w_gu, m_ffn2_w_down, m_ln3_g, m_ln3_b, v_w_ada, v_b_ada, v_ffn1_w_gu, v_ffn1_w_down, v_ln1_g, v_ln1_b, v_w_in, v_w_sb_out, v_conv_w, v_conv_b, v_conv_ln_g, v_conv_ln_b, v_w_conv_out, v_w_out, v_ln2_g, v_ln2_b, v_ffn2_w_gu, v_ffn2_w_down, v_ln3_g, v_ln3_b):
    given = dict(locals())
    w = {n: given[n] for n in _WEIGHTS}
    m = {n: given["m_" + n] for n in _WEIGHTS}
    v = {n: given["v_" + n] for n in _WEIGHTS}
    return _step(x, c, loss_target, w, m, v)
```

```python
import functools

import jax
import jax.numpy as jnp
from jax import lax
from jax.experimental import pallas as pl
from jax.experimental.pallas import tpu as pltpu

F32 = jnp.float32
BF16 = jnp.bfloat16

D_MODEL = 1024
D_FF = 2816
HEADS = 16
HEAD_DIM = 64
LANES = 128
CONV_TAPS = 31
HALO = 32
N_CHIPS = 4
N_DEV = 8
ALPHA = 2.0 ** 0.25
LN_EPS = 1e-5
ATT_BLOCK = 256
VMEM_LIMIT = 56 * 1024 * 1024

ADAM_LR = 0.001
ADAM_B1 = 0.9
ADAM_B2 = 0.999
ADAM_EPS = 1e-08
ADAM_WD = 0.01
ADAM_STEP = 10

MESH = pl.DeviceIdType.MESH


def _pick(n, cands):
    for t in cands:
        if t <= n and n % t == 0:
            return t
    return n


def _params(sem):
    return pltpu.CompilerParams(dimension_semantics=sem, vmem_limit_bytes=VMEM_LIMIT)


def _sigmoid(z):
    t = jnp.exp(-jnp.abs(z))
    return jnp.where(z >= 0, 1.0, t) / (1.0 + t)


def _silu(z):
    return z * _sigmoid(z)


def _dsilu(z):
    s = _sigmoid(z)
    return s * (1.0 + z * (1.0 - s))


def _ln_stats(r):
    mu = jnp.mean(r, axis=-1, keepdims=True)
    d = r - mu
    var = jnp.mean(d * d, axis=-1, keepdims=True)
    rstd = lax.rsqrt(var + LN_EPS)
    return d * rstd, rstd


def _colsum(v):
    return jnp.sum(v, axis=0, keepdims=True)


_DIMS = {"nn": (((1,), (0,)), ((), ())), "nt": (((1,), (1,)), ((), ())), "tn": (((0,), (0,)), ((), ()))}
_TN_CANDS = (1408, 1792, 1152, 1024, 512, 256, 128)
_TK_CANDS = (1024, 1408, 896, 512, 256, 128)


def _matmul(a, b, *, mode, out_dtype, name, bias=None, out_stacked=False):
    a_halves = mode == "nt" and a.ndim == 3
    b_halves = mode == "tn" and b.ndim == 3
    b_stacked = b.ndim == 3 and not b_halves
    if mode == "nn":
        m, k = a.shape
        n_c = b.shape[-1]
        n = n_c * (N_CHIPS if b_stacked else 1)
        k_c = k
    elif mode == "nt":
        m = a.shape[-2]
        k = a.shape[-1] * (2 if a_halves else 1)
        n = b.shape[-2]
        k_c = b.shape[-1]
        n_c = n
    else:
        k, m = a.shape
        n = b.shape[-1] * (2 if b_halves else 1)
        n_c = n // N_CHIPS if out_stacked else n
        k_c = k
    if mode == "tn":
        tm = _pick(m, (1024, 1408, 512, 256, 128))
        tk = _pick(k, (512, 256, 128, 64, 32, 16))
    else:
        tm = _pick(m, (1024, 512, 256, 128, 64, 32, 16))
        tk = _pick(k_c, _TK_CANDS)
    tn = _pick(n_c, _TN_CANDS)
    nb = n_c // tn
    kb = k_c // tk
    nk = k // tk
    grid = (m // tm, n // tn, nk)

    if mode == "nn":
        a_spec = pl.BlockSpec((tm, tk), lambda i, j, kk: (i, kk))
        if b_stacked:
            b_spec = pl.BlockSpec((None, tk, tn), lambda i, j, kk: (j // nb, kk, j % nb))
        else:
            b_spec = pl.BlockSpec((tk, tn), lambda i, j, kk: (kk, j))
    elif mode == "nt":
        if a_halves:
            ka = a.shape[-1] // tk
            a_spec = pl.BlockSpec((None, tm, tk), lambda i, j, kk: (kk // ka, i, kk % ka))
        else:
            a_spec = pl.BlockSpec((tm, tk), lambda i, j, kk: (i, kk))
        if b_stacked:
            b_spec = pl.BlockSpec((None, tn, tk), lambda i, j, kk: (kk // kb, j, kk % kb))
        else:
            b_spec = pl.BlockSpec((tn, tk), lambda i, j, kk: (j, kk))
    else:
        a_spec = pl.BlockSpec((tk, tm), lambda i, j, kk: (kk, i))
        if b_halves:
            nh = b.shape[-1] // tn
            b_spec = pl.BlockSpec((None, tk, tn), lambda i, j, kk: (j // nh, kk, j % nh))
        else:
            b_spec = pl.BlockSpec((tk, tn), lambda i, j, kk: (kk, j))
    if out_stacked:
        out_shape = jax.ShapeDtypeStruct((N_CHIPS, m, n_c), out_dtype)
        o_spec = pl.BlockSpec((None, tm, tn), lambda i, j, kk: (j // nb, i, j % nb))
    else:
        out_shape = jax.ShapeDtypeStruct((m, n), out_dtype)
        o_spec = pl.BlockSpec((tm, tn), lambda i, j, kk: (i, j))
    in_specs = [a_spec, b_spec]
    args = [a, b]
    if bias is not None:
        in_specs.append(pl.BlockSpec((1, tn), lambda i, j, kk: (0, j)))
        args.append(bias)
    dims = _DIMS[mode]

    def body(*refs):
        a_ref, b_ref = refs[0], refs[1]
        bias_ref = refs[2] if bias is not None else None
        o_ref, acc_ref = refs[-2], refs[-1]
        kk = pl.program_id(2)

        @pl.when(kk == 0)
        def _():
            acc_ref[...] = jnp.zeros_like(acc_ref)

        acc_ref[...] += lax.dot_general(a_ref[...], b_ref[...], dims, preferred_element_type=F32)

        @pl.when(kk == nk - 1)
        def _():
            r = acc_ref[...]
            if bias_ref is not None:
                r = r + bias_ref[...]
            o_ref[...] = r.astype(o_ref.dtype)

    return pl.pallas_call(
        body, name=name, grid=grid, in_specs=in_specs, out_specs=o_spec, out_shape=out_shape,
        scratch_shapes=[pltpu.VMEM((tm, tn), F32)],
        compiler_params=_params(("parallel", "parallel", "arbitrary")),
    )(*args)


def _row_grid(bsz, seq, ts):
    ns = seq // ts
    return (bsz, ns), ns


def _rows(ts, width, ns, col=0):
    return pl.BlockSpec((ts, width), lambda b, s: (b * ns + s, col))


def _mod_spec():
    return pl.BlockSpec((None, 9, D_MODEL), lambda b, s: (b, 0, 0))


def _vec_spec(rows=1, width=D_MODEL):
    return pl.BlockSpec((rows, width), lambda b, s: (0, 0))


def _silu_pad(c):
    bsz = c.shape[0]

    def body(c_ref, o_ref):
        o_ref[...] = jnp.zeros_like(o_ref)
        o_ref[0:bsz, :] = _silu(c_ref[...]).astype(BF16)

    return pl.pallas_call(body, name="silu_pad", out_shape=jax.ShapeDtypeStruct((16, D_MODEL), BF16))(c)


def _mod_in(x, mod, bsz, seq, sub):
    ts = _pick(seq, (512, 256, 128))
    grid, ns = _row_grid(bsz, seq, ts)

    def body(x_ref, mod_ref, u_ref):
        sh = mod_ref[3 * sub:3 * sub + 1, :]
        sc = mod_ref[3 * sub + 1:3 * sub + 2, :]
        u_ref[...] = (x_ref[...] * (1.0 + sc) + sh).astype(BF16)

    return pl.pallas_call(
        body, name=f"mod_in{sub}", grid=grid, in_specs=[_rows(ts, D_MODEL, ns), _mod_spec()],
        out_specs=_rows(ts, D_MODEL, ns), out_shape=jax.ShapeDtypeStruct(x.shape, BF16),
        compiler_params=_params(("parallel", "parallel")),
    )(x, mod)


_FFN_TN = D_FF // 2


def _ffn_up_act(u, w_gu, name):
    t = u.shape[0]
    tm = _pick(t, (512, 256, 128))
    tn = _FFN_TN

    def body(u_ref, wa_ref, wg_ref, h_ref, p_ref):
        u_v = u_ref[...]
        a = jnp.dot(u_v, wa_ref[...], preferred_element_type=F32)
        g = jnp.dot(u_v, wg_ref[...], preferred_element_type=F32)
        h_ref[0] = a.astype(BF16)
        h_ref[1] = g.astype(BF16)
        p_ref[...] = (_silu(a) * g).astype(BF16)

    return pl.pallas_call(
        body, name=name, grid=(2, t // tm),
        in_specs=[pl.BlockSpec((tm, D_MODEL), lambda j, i: (i, 0)),
                  pl.BlockSpec((None, D_MODEL, tn), lambda j, i: (j, 0, 0)),
                  pl.BlockSpec((None, D_MODEL, tn), lambda j, i: (j + 2, 0, 0))],
        out_specs=[pl.BlockSpec((2, tm, tn), lambda j, i: (0, i, j)),
                   pl.BlockSpec((tm, tn), lambda j, i: (i, j))],
        out_shape=[jax.ShapeDtypeStruct((2, t, D_FF), BF16), jax.ShapeDtypeStruct((t, D_FF), BF16)],
        compiler_params=_params(("parallel", "parallel")),
    )(u, w_gu, w_gu)


def _ffn_down_bwd_act(df, w_down, h, name):
    t = df.shape[0]
    tm = _pick(t, (512, 256, 128))
    tn = _FFN_TN

    def body(df_ref, wd_ref, h_ref, dh_ref):
        dp = lax.dot_general(df_ref[...], wd_ref[...], _DIMS["nt"], preferred_element_type=F32)
        a = h_ref[0].astype(F32)
        g = h_ref[1].astype(F32)
        dh_ref[0] = (dp * g * _dsilu(a)).astype(BF16)
        dh_ref[1] = (dp * _silu(a)).astype(BF16)

    blk = pl.BlockSpec((2, tm, tn), lambda j, i: (0, i, j))
    return pl.pallas_call(
        body, name=name, grid=(2, t // tm),
        in_specs=[pl.BlockSpec((tm, D_MODEL), lambda j, i: (i, 0)),
                  pl.BlockSpec((tn, D_MODEL), lambda j, i: (j, 0)), blk],
        out_specs=blk, out_shape=jax.ShapeDtypeStruct((2, t, D_FF), BF16),
        compiler_params=_params(("parallel", "parallel")),
    )(df, w_down, h)


def _res_ln_fwd(x, f, mod, ln_g, ln_b, bsz, seq, sub, weight, target=None):
    ts = _pick(seq, (256, 128))
    grid, ns = _row_grid(bsz, seq, ts)
    last = target is not None

    def body(*refs):
        x_ref, f_ref, mod_ref, g_ref, b_ref = refs[:5]
        gate = mod_ref[3 * sub + 2:3 * sub + 3, :]
        r = ALPHA * x_ref[...] + gate * (weight * f_ref[...])
        xhat, _ = _ln_stats(r)
        xo = xhat * g_ref[...] + b_ref[...]
        if last:
            t_ref, r_ref, dy_ref, loss_ref = refs[5:]
            diff = xo - t_ref[...]
            dy_ref[...] = diff * (1.0 / D_MODEL)
            part = 0.5 * jnp.sum(jnp.mean(diff * diff, axis=-1, keepdims=True), axis=0, keepdims=True)

            @pl.when((pl.program_id(0) == 0) & (pl.program_id(1) == 0))
            def _():
                loss_ref[...] = jnp.zeros_like(loss_ref)

            loss_ref[...] += jnp.broadcast_to(part, loss_ref.shape)
        else:
            r_ref, xo_ref, u_ref = refs[5:]
            xo_ref[...] = xo
            sh = mod_ref[3 * sub + 3:3 * sub + 4, :]
            sc = mod_ref[3 * sub + 4:3 * sub + 5, :]
            u_ref[...] = (xo * (1.0 + sc) + sh).astype(BF16)
        r_ref[...] = r

    row = _rows(ts, D_MODEL, ns)
    in_specs = [row, row, _mod_spec(), _vec_spec(), _vec_spec()]
    args = [x, f, mod, ln_g, ln_b]
    if last:
        in_specs.append(row)
        args.append(target)
        out_specs = [row, row, _vec_spec(8, LANES)]
        out_shape = [jax.ShapeDtypeStruct(x.shape, F32), jax.ShapeDtypeStruct(x.shape, F32),
                     jax.ShapeDtypeStruct((8, LANES), F32)]
        sem = ("arbitrary", "arbitrary")
    else:
        out_specs = [row, row, row]
        out_shape = [jax.ShapeDtypeStruct(x.shape, F32), jax.ShapeDtypeStruct(x.shape, F32),
                     jax.ShapeDtypeStruct(x.shape, BF16)]
        sem = ("parallel", "parallel")
    return pl.pallas_call(
        body, name=f"res_ln_fwd{sub}", grid=grid, in_specs=in_specs, out_specs=out_specs, out_shape=out_shape,
        compiler_params=_params(sem),
    )(*args)


def _res_ln_bwd(r, dxo, f, mod, ln_g, bsz, seq, sub, weight):
    ts = _pick(seq, (256, 128))
    grid, ns = _row_grid(bsz, seq, ts)

    def body(r_ref, dxo_ref, f_ref, mod_ref, g_ref, dxres_ref, df_ref, lns_ref, gs_ref):
        b, s = pl.program_id(0), pl.program_id(1)
        gate = mod_ref[3 * sub + 2:3 * sub + 3, :]
        xhat, rstd = _ln_stats(r_ref[...])
        dxo_v = dxo_ref[...]
        dxhat = dxo_v * g_ref[...]
        m1 = jnp.mean(dxhat, axis=-1, keepdims=True)
        m2 = jnp.mean(dxhat * xhat, axis=-1, keepdims=True)
        dr = rstd * (dxhat - m1 - xhat * m2)
        dxres_ref[...] = ALPHA * dr
        df_ref[...] = (dr * (gate * weight)).astype(BF16)

        @pl.when((b == 0) & (s == 0))
        def _():
            lns_ref[...] = jnp.zeros_like(lns_ref)

        @pl.when(s == 0)
        def _():
            gs_ref[...] = jnp.zeros_like(gs_ref)

        lns_ref[0:1, :] += _colsum(dxo_v * xhat)
        lns_ref[1:2, :] += _colsum(dxo_v)
        gs_ref[0:1, :] += _colsum(dr * (weight * f_ref[...]))

    row = _rows(ts, D_MODEL, ns)
    return pl.pallas_call(
        body, name=f"res_ln_bwd{sub}", grid=grid,
        in_specs=[row, row, row, _mod_spec(), _vec_spec()],
        out_specs=[row, row, _vec_spec(8), pl.BlockSpec((None, 8, D_MODEL), lambda b, s: (b, 0, 0))],
        out_shape=[jax.ShapeDtypeStruct(r.shape, F32), jax.ShapeDtypeStruct(r.shape, BF16),
                   jax.ShapeDtypeStruct((8, D_MODEL), F32), jax.ShapeDtypeStruct((bsz, 8, D_MODEL), F32)],
        compiler_params=_params(("arbitrary", "arbitrary")),
    )(r, dxo, f, mod, ln_g)


def _mod_bwd(dxres, du, x, mod, bsz, seq, sub):
    ts = _pick(seq, (256, 128))
    grid, ns = _row_grid(bsz, seq, ts)

    def body(dxres_ref, du_ref, x_ref, mod_ref, dx_ref, st_ref):
        s = pl.program_id(1)
        sc = mod_ref[3 * sub + 1:3 * sub + 2, :]
        du_v = du_ref[...]
        dx_ref[...] = dxres_ref[...] + du_v * (1.0 + sc)

        @pl.when(s == 0)
        def _():
            st_ref[...] = jnp.zeros_like(st_ref)

        st_ref[0:1, :] += _colsum(du_v)
        st_ref[1:2, :] += _colsum(du_v * x_ref[...])

    row = _rows(ts, D_MODEL, ns)
    return pl.pallas_call(
        body, name=f"mod_bwd{sub}", grid=grid, in_specs=[row, row, row, _mod_spec()],
        out_specs=[row, pl.BlockSpec((None, 8, D_MODEL), lambda b, s: (b, 0, 0))],
        out_shape=[jax.ShapeDtypeStruct(x.shape, F32), jax.ShapeDtypeStruct((bsz, 8, D_MODEL), F32)],
        compiler_params=_params(("parallel", "arbitrary")),
    )(dxres, du, x, mod)


_COL_GLU_A, _COL_GLU_B, _COL_GATE_A, _COL_GATE_B = 3, 4, 5, 6


def _merge_fwd(proj, ysb, yconv, bsz, seq):
    ts = _pick(seq, (512, 256, 128))
    grid, ns = _row_grid(bsz, seq, ts)

    def body(ga_ref, gb_ref, ysb_ref, yc_ref, o_ref):
        o_ref[...] = (_sigmoid(ga_ref[...]) * ysb_ref[...] + _sigmoid(gb_ref[...]) * yc_ref[...]).astype(BF16)

    row = _rows(ts, D_MODEL, ns)
    return pl.pallas_call(
        body, name="merge_fwd", grid=grid,
        in_specs=[_rows(ts, D_MODEL, ns, _COL_GATE_A), _rows(ts, D_MODEL, ns, _COL_GATE_B), row, row],
        out_specs=row, out_shape=jax.ShapeDtypeStruct(ysb.shape, BF16),
        compiler_params=_params(("parallel", "parallel")),
    )(proj, proj, ysb, yconv)


def _merge_bwd(proj, ysb, yconv, dmerged, bsz, seq):
    ts = _pick(seq, (256, 128))
    grid, ns = _row_grid(bsz, seq, ts)

    def body(ga_ref, gb_ref, ysb_ref, yc_ref, dm_ref, dysb_ref, dyc_ref, dg_ref):
        sa = _sigmoid(ga_ref[...])
        sb = _sigmoid(gb_ref[...])
        dm = dm_ref[...]
        dysb_ref[...] = (dm * sa).astype(BF16)
        dyc_ref[...] = (dm * sb).astype(BF16)
        dg_ref[:, :D_MODEL] = (dm * ysb_ref[...] * sa * (1.0 - sa)).astype(BF16)
        dg_ref[:, D_MODEL:] = (dm * yc_ref[...] * sb * (1.0 - sb)).astype(BF16)

    row = _rows(ts, D_MODEL, ns)
    t = ysb.shape[0]
    return pl.pallas_call(
        body, name="merge_bwd", grid=grid,
        in_specs=[_rows(ts, D_MODEL, ns, _COL_GATE_A), _rows(ts, D_MODEL, ns, _COL_GATE_B), row, row, row],
        out_specs=[row, row, _rows(ts, 2 * D_MODEL, ns)],
        out_shape=[jax.ShapeDtypeStruct((t, D_MODEL), BF16), jax.ShapeDtypeStruct((t, D_MODEL), BF16),
                   jax.ShapeDtypeStruct((t, 2 * D_MODEL), BF16)],
        compiler_params=_params(("parallel", "parallel")),
    )(proj, proj, ysb, yconv, dmerged)


_CONV_ROWS = 128


def _halo_prev(tt, ns, col):
    r = tt // HALO
    return pl.BlockSpec((HALO, D_MODEL), lambda b, s: (jnp.maximum((b * ns + s) * r - 1, 0), col))


def _halo_next(tt, ns, nblk, col):
    r = tt // HALO
    return pl.BlockSpec((HALO, D_MODEL), lambda b, s: (jnp.minimum((b * ns + s + 1) * r, nblk - 1), col))


def _windows(pad_ref, stage_ref, tt, offsets):
    for r in range(8):
        mine = [o for o in offsets if o % 8 == r]
        if not mine:
            continue
        n = max(mine) - r + tt
        stage_ref[0:n, :] = pad_ref[r:r + n, :]
        for o in mine:
            yield o, stage_ref[o - r:o - r + tt, :]


def _fill_hc(hpad, a_ref, b_ref, ha_ref, hb_ref, s):
    halo = ha_ref[...] * _sigmoid(hb_ref[...])
    hpad[0:HALO, :] = jnp.where(s > 0, halo, 0.0)
    hpad[HALO:, :] = a_ref[...] * _sigmoid(b_ref[...])


def _conv_fwd(proj, conv_w, conv_b, ln_g, ln_b, bsz, seq):
    tt = _CONV_ROWS
    grid, ns = _row_grid(bsz, seq, tt)
    off = HALO - (CONV_TAPS - 1)

    def body(a_ref, b_ref, ha_ref, hb_ref, w_ref, cb_ref, g_ref, bb_ref, cs_ref, cv_ref, hpad, stage):
        _fill_hc(hpad, a_ref, b_ref, ha_ref, hb_ref, pl.program_id(1))
        acc = jnp.zeros((tt, D_MODEL), F32)
        for o, win in _windows(hpad, stage, tt, [off + j for j in range(CONV_TAPS)]):
            acc = acc + w_ref[o - off:o - off + 1, :] * win
        cv = acc + cb_ref[...]
        cv_ref[...] = cv
        xhat, _ = _ln_stats(cv)
        cs_ref[...] = _silu(xhat * g_ref[...] + bb_ref[...]).astype(BF16)

    row = _rows(tt, D_MODEL, ns)
    t = proj.shape[0]
    return pl.pallas_call(
        body, name="conv_fwd", grid=grid,
        in_specs=[_rows(tt, D_MODEL, ns, _COL_GLU_A), _rows(tt, D_MODEL, ns, _COL_GLU_B),
                  _halo_prev(tt, ns, _COL_GLU_A), _halo_prev(tt, ns, _COL_GLU_B),
                  _vec_spec(32), _vec_spec(), _vec_spec(), _vec_spec()],
        out_specs=[row, row],
        out_shape=[jax.ShapeDtypeStruct((t, D_MODEL), BF16), jax.ShapeDtypeStruct((t, D_MODEL), F32)],
        scratch_shapes=[pltpu.VMEM((HALO + tt, D_MODEL), F32), pltpu.VMEM((HALO + tt, D_MODEL), F32)],
        compiler_params=_params(("parallel", "parallel")),
    )(proj, proj, proj, proj, conv_w, conv_b, ln_g, ln_b)


def _conv_bwd_ln(dcs, cv, ln_g, ln_b, bsz, seq):
    ts = _pick(seq, (256, 128))
    grid, ns = _row_grid(bsz, seq, ts)

    def body(dcs_ref, cv_ref, g_ref, b_ref, dcv_ref, st_ref):
        xhat, rstd = _ln_stats(cv_ref[...])
        cl = xhat * g_ref[...] + b_ref[...]
        dcl = dcs_ref[...] * _dsilu(cl)
        dxhat = dcl * g_ref[...]
        m1 = jnp.mean(dxhat, axis=-1, keepdims=True)
        m2 = jnp.mean(dxhat * xhat, axis=-1, keepdims=True)
        dcv = rstd * (dxhat - m1 - xhat * m2)
        dcv_ref[...] = dcv

        @pl.when((pl.program_id(0) == 0) & (pl.program_id(1) == 0))
        def _():
            st_ref[...] = jnp.zeros_like(st_ref)

        st_ref[0:1, :] += _colsum(dcl * xhat)
        st_ref[1:2, :] += _colsum(dcl)
        st_ref[2:3, :] += _colsum(dcv)

    row = _rows(ts, D_MODEL, ns)
    return pl.pallas_call(
        body, name="conv_bwd_ln", grid=grid, in_specs=[row, row, _vec_spec(), _vec_spec()],
        out_specs=[row, _vec_spec(8)],
        out_shape=[jax.ShapeDtypeStruct(cv.shape, F32), jax.ShapeDtypeStruct((8, D_MODEL), F32)],
        compiler_params=_params(("arbitrary", "arbitrary")),
    )(dcs, cv, ln_g, ln_b)


def _conv_bwd_taps(proj, dcv, conv_w, bsz, seq):
    tt = _CONV_ROWS
    grid, ns = _row_grid(bsz, seq, tt)
    off = HALO - (CONV_TAPS - 1)
    nblk = proj.shape[0] // HALO

    def body(a_ref, b_ref, ha_ref, hb_ref, d_ref, dn_ref, w_ref, dglu_ref, dw_ref, hpad, dpad, stage):
        s = pl.program_id(1)
        _fill_hc(hpad, a_ref, b_ref, ha_ref, hb_ref, s)
        dcv = d_ref[...]
        dpad[0:tt, :] = dcv
        dpad[tt:, :] = jnp.where(s < ns - 1, dn_ref[...], 0.0)

        @pl.when((pl.program_id(0) == 0) & (s == 0))
        def _():
            dw_ref[...] = jnp.zeros_like(dw_ref)

        dhc = jnp.zeros((tt, D_MODEL), F32)
        for o, win in _windows(dpad, stage, tt, list(range(CONV_TAPS))):
            j = CONV_TAPS - 1 - o
            dhc = dhc + w_ref[j:j + 1, :] * win
        for o, win in _windows(hpad, stage, tt, [off + j for j in range(CONV_TAPS)]):
            dw_ref[o - off:o - off + 1, :] += _colsum(dcv * win)
        sb = _sigmoid(b_ref[...])
        dglu_ref[:, :D_MODEL] = (dhc * sb).astype(BF16)
        dglu_ref[:, D_MODEL:] = (dhc * a_ref[...] * sb * (1.0 - sb)).astype(BF16)

    t = proj.shape[0]
    return pl.pallas_call(
        body, name="conv_bwd_taps", grid=grid,
        in_specs=[_rows(tt, D_MODEL, ns, _COL_GLU_A), _rows(tt, D_MODEL, ns, _COL_GLU_B),
                  _halo_prev(tt, ns, _COL_GLU_A), _halo_prev(tt, ns, _COL_GLU_B),
                  _rows(tt, D_MODEL, ns), _halo_next(tt, ns, nblk, 0), _vec_spec(32)],
        out_specs=[_rows(tt, 2 * D_MODEL, ns), _vec_spec(32)],
        out_shape=[jax.ShapeDtypeStruct((t, 2 * D_MODEL), BF16), jax.ShapeDtypeStruct((32, D_MODEL), F32)],
        scratch_shapes=[pltpu.VMEM((HALO + tt, D_MODEL), F32), pltpu.VMEM((tt + HALO, D_MODEL), F32),
                        pltpu.VMEM((HALO + tt, D_MODEL), F32)],
        compiler_params=_params(("arbitrary", "arbitrary")),
    )(proj, proj, proj, proj, dcv, dcv, conv_w)


_NT = (((1,), (1,)), ((), ()))
_TN = (((0,), (0,)), ((), ()))


def _dot(a, b, dims=None):
    if dims is None:
        return jnp.dot(a, b, preferred_element_type=F32)
    return lax.dot_general(a, b, dims, preferred_element_type=F32)


def _tri_dot(v, tri2):
    hi = v.astype(BF16)
    lo = (v - hi.astype(F32)).astype(BF16)
    return _dot(jnp.concatenate([hi, lo], axis=1), tri2)


def _tri2(mask):
    t = mask.astype(BF16)
    return jnp.concatenate([t, t], axis=0)


def _softplus_parts(z):
    t = jnp.exp(-jnp.abs(z))
    den = 1.0 + t
    return jnp.maximum(z, 0.0) + jnp.log(den), t, den


def _attn_fwd(proj, bsz, seq):
    blk = ATT_BLOCK
    nq = seq // blk
    n_pairs = D_MODEL // LANES

    def body(q_ref, k_ref, v_ref, y_ref, rt_ref, zr_buf, ns_buf, run_buf, acc_buf):
        qi = pl.program_id(2)
        lane = lax.broadcasted_iota(jnp.int32, (blk, LANES), 1)
        first = lane < HEAD_DIM
        q2 = q_ref[...] * 0.125
        q_heads = (jnp.where(first, q2, 0.0).astype(BF16), jnp.where(first, 0.0, q2).astype(BF16))
        rr = lax.broadcasted_iota(jnp.int32, (blk, blk), 0)
        cc = lax.broadcasted_iota(jnp.int32, (blk, blk), 1)
        tri_ge = _tri2(rr >= cc)
        causal = cc < rr

        def scores(kb, slot, masked, heads=(0, 1)):
            k_blk = k_ref[pl.ds(pl.multiple_of(kb * blk, blk), blk), :].astype(BF16)
            for h in heads:
                z = _dot(q_heads[h], k_blk, _NT)
                if masked:
                    z = jnp.where(causal, z, -1e30)
                sp, _, _ = _softplus_parts(z)
                neg = -sp
                zr_buf[slot, h] = z + _tri_dot(neg, tri_ge)
                ns_buf[slot, h] = jnp.sum(neg, axis=1, keepdims=True)

        def weigh(kb, slot, heads=(0, 1)):
            v_blk = v_ref[pl.ds(pl.multiple_of(kb * blk, blk), blk), :].astype(BF16)
            for h in heads:
                run = run_buf[h]
                w = jnp.exp(zr_buf[slot, h] + run)
                acc_buf[h] += _dot(w.astype(BF16), v_blk)
                run_buf[h] = run + ns_buf[slot, h]

        def step(kb_next, kb, slot):
            for h in range(2):
                scores(kb_next, 1 - slot, False, (h,))
                weigh(kb, slot, (h,))

        run_buf[...] = jnp.zeros_like(run_buf)
        acc_buf[...] = jnp.zeros_like(acc_buf)
        scores(qi, 0, True)

        def two_steps(p, carry):
            t = 2 * p
            step(qi - t - 1, qi - t, 0)
            step(qi - t - 2, qi - t - 1, 1)
            return carry

        lax.fori_loop(0, qi // 2, two_steps, 0)

        @pl.when(qi % 2 == 1)
        def _():
            step(0, 1, 0)
            weigh(0, 1)

        @pl.when(qi % 2 == 0)
        def _():
            weigh(0, 0)

        y_ref[...] = jnp.where(first, acc_buf[0], acc_buf[1]).astype(BF16)
        rt_ref[...] = jnp.where(first, jnp.broadcast_to(run_buf[0], (blk, LANES)),
                                jnp.broadcast_to(run_buf[1], (blk, LANES)))

    t = proj.shape[0]
    q_spec = pl.BlockSpec((blk, LANES), lambda b, p, i: (b * nq + i, p))
    return pl.pallas_call(
        body, name="attn_fwd", grid=(bsz, n_pairs, nq),
        in_specs=[q_spec,
                  pl.BlockSpec((seq, LANES), lambda b, p, i: (b, n_pairs + p)),
                  pl.BlockSpec((seq, LANES), lambda b, p, i: (b, 2 * n_pairs + p))],
        out_specs=[q_spec, q_spec],
        out_shape=[jax.ShapeDtypeStruct((t, D_MODEL), BF16), jax.ShapeDtypeStruct((t, D_MODEL), F32)],
        scratch_shapes=[pltpu.VMEM((2, 2, blk, blk), F32), pltpu.VMEM((2, 2, blk, 1), F32),
                        pltpu.VMEM((2, blk, 1), F32), pltpu.VMEM((2, blk, LANES), F32)],
        compiler_params=_params(("parallel", "parallel", "arbitrary")),
    )(proj, proj, proj)


def _attn_bwd(proj, rtot, dy, bsz, seq):
    blk = ATT_BLOCK
    nq = seq // blk
    n_pairs = D_MODEL // LANES

    def body(q_ref, k_ref, v_ref, dy_ref, rt_ref, dq_ref, dk_ref, dv_ref, dk_acc, dv_acc,
             a_buf, sig_buf, dw_buf, ns_buf, pre_buf, es_buf, dq_buf):
        qi = pl.program_id(2)

        @pl.when(qi == 0)
        def _():
            dk_acc[...] = jnp.zeros_like(dk_acc)
            dv_acc[...] = jnp.zeros_like(dv_acc)

        lane = lax.broadcasted_iota(jnp.int32, (blk, LANES), 1)
        first = lane < HEAD_DIM
        head_row = lax.broadcasted_iota(jnp.int32, (LANES, blk), 0) < HEAD_DIM
        q2 = q_ref[...] * 0.125
        q_rows = (jnp.where(first, q2, 0.0).astype(BF16), jnp.where(first, 0.0, q2).astype(BF16))
        q_t = q2.T
        q_heads = (jnp.where(head_row, q_t, 0.0).astype(BF16), jnp.where(head_row, 0.0, q_t).astype(BF16))
        dy2 = dy_ref[...].astype(F32)
        dy_rows = (jnp.where(first, dy2, 0.0).astype(BF16), jnp.where(first, 0.0, dy2).astype(BF16))
        dy_t = dy2.T
        dy_heads = (jnp.where(head_row, dy_t, 0.0).astype(BF16), jnp.where(head_row, 0.0, dy_t).astype(BF16))
        rt_t = rt_ref[...].T
        rt = (rt_t[0:1, :], rt_t[HEAD_DIM:HEAD_DIM + 1, :])
        rr = lax.broadcasted_iota(jnp.int32, (blk, blk), 0)
        cc = lax.broadcasted_iota(jnp.int32, (blk, blk), 1)
        lower = (cc < rr).astype(BF16)
        lower_eq = (cc <= rr).astype(BF16)
        tri_lt = jnp.concatenate([lower, lower], axis=1)
        tri_le = jnp.concatenate([lower_eq, lower_eq], axis=1)
        causal = rr < cc

        def tri_left(tri2, v):
            hi = v.astype(BF16)
            lo = (v - hi.astype(F32)).astype(BF16)
            return _dot(tri2, jnp.concatenate([hi, lo], axis=0))

        def scores(kb, slot, heads=(0, 1)):
            rows = pl.ds(pl.multiple_of(kb * blk, blk), blk)
            k_blk = k_ref[rows, :].astype(BF16)
            v_blk = v_ref[rows, :].astype(BF16)
            keep = jnp.logical_or(causal, kb < qi)
            for h in heads:
                z = jnp.where(keep, _dot(k_blk, q_heads[h]), -1e30)
                sp, t, den = _softplus_parts(z)
                neg = -sp
                a_buf[slot, h] = z - tri_left(tri_lt, neg)
                sig_buf[slot, h] = jnp.where(z >= 0, 1.0, t) / den
                ns_buf[slot, h] = jnp.sum(neg, axis=0, keepdims=True)
                dw_buf[slot, h] = _dot(v_blk, dy_heads[h])

        def finish(kb, slot, heads=(0, 1)):
            rows = pl.ds(pl.multiple_of(kb * blk, blk), blk)
            k_t = k_ref[rows, :].T.astype(BF16)
            for h in heads:
                pre, esum = pre_buf[h], es_buf[h]
                w = jnp.exp(a_buf[slot, h] + (rt[h] - pre))
                e = dw_buf[slot, h] * w
                dz = e - sig_buf[slot, h] * (esum + tri_left(tri_le, e))
                pre_buf[h] = pre + ns_buf[slot, h]
                es_buf[h] = esum + jnp.sum(e, axis=0, keepdims=True)
                dzb = dz.astype(BF16)
                dq_buf[h] += _dot(k_t, dzb)
                dk_acc[rows, :] += _dot(dzb, q_rows[h])
                dv_acc[rows, :] += _dot(w.astype(BF16), dy_rows[h])

        def step(kb_next, kb, slot):
            for h in range(2):
                scores(kb_next, 1 - slot, (h,))
                finish(kb, slot, (h,))

        pre_buf[...] = jnp.zeros_like(pre_buf)
        es_buf[...] = jnp.zeros_like(es_buf)
        dq_buf[...] = jnp.zeros_like(dq_buf)
        scores(0, 0)

        def two_steps(p, carry):
            t = 2 * p
            step(t + 1, t, 0)
            step(t + 2, t + 1, 1)
            return carry

        lax.fori_loop(0, qi // 2, two_steps, 0)

        @pl.when(qi % 2 == 1)
        def _():
            step(qi, qi - 1, 0)
            finish(qi, 1)

        @pl.when(qi % 2 == 0)
        def _():
            finish(qi, 0)

        dq_ref[...] = (jnp.where(head_row, dq_buf[0], dq_buf[1]).T * 0.125).astype(BF16)

        @pl.when(qi == nq - 1)
        def _():
            dk_ref[...] = dk_acc[...].astype(BF16)
            dv_ref[...] = dv_acc[...].astype(BF16)

    t = proj.shape[0]
    q_spec = pl.BlockSpec((blk, LANES), lambda b, p, i: (b * nq + i, p))
    kv_out = pl.BlockSpec((seq, LANES), lambda b, p, i: (b, p))
    out = jax.ShapeDtypeStruct((t, D_MODEL), BF16)
    return pl.pallas_call(
        body, name="attn_bwd", grid=(bsz, n_pairs, nq),
        in_specs=[q_spec,
                  pl.BlockSpec((seq, LANES), lambda b, p, i: (b, n_pairs + p)),
                  pl.BlockSpec((seq, LANES), lambda b, p, i: (b, 2 * n_pairs + p)),
                  q_spec, q_spec],
        out_specs=[q_spec, kv_out, kv_out], out_shape=[out, out, out],
        scratch_shapes=[pltpu.VMEM((seq, LANES), F32), pltpu.VMEM((seq, LANES), F32),
                        pltpu.VMEM((2, 2, blk, blk), F32), pltpu.VMEM((2, 2, blk, blk), F32),
                        pltpu.VMEM((2, 2, blk, blk), F32), pltpu.VMEM((2, 2, 1, blk), F32),
                        pltpu.VMEM((2, 1, blk), F32), pltpu.VMEM((2, 1, blk), F32),
                        pltpu.VMEM((2, LANES, blk), F32)],
        compiler_params=_params(("parallel", "parallel", "arbitrary")),
    )(proj, proj, proj, dy, rtot)


def _adamw(w, g, m, v, name):
    rows, cols = w.shape
    tr = _pick(rows, (256, 352, 128, 64, 32, 16, 8))
    c1 = 1.0 - ADAM_B1 ** ADAM_STEP
    c2 = 1.0 - ADAM_B2 ** ADAM_STEP

    def body(w_ref, g_ref, m_ref, v_ref, d_ref, nm_ref, nv_ref):
        g_v = g_ref[...]
        nm = ADAM_B1 * m_ref[...] + (1.0 - ADAM_B1) * g_v
        nv = ADAM_B2 * v_ref[...] + (1.0 - ADAM_B2) * (g_v * g_v)
        nm_ref[...] = nm
        nv_ref[...] = nv
        d_ref[...] = -ADAM_LR * ((nm / c1) / (jnp.sqrt(nv / c2) + ADAM_EPS) + ADAM_WD * w_ref[...])

    spec = pl.BlockSpec((tr, cols), lambda i: (i, 0))
    shape = jax.ShapeDtypeStruct(w.shape, F32)
    return pl.pallas_call(
        body, name=name, grid=(rows // tr,), in_specs=[spec] * 4, out_specs=[spec] * 3, out_shape=[shape] * 3,
        compiler_params=_params(("parallel",)),
    )(w, g, m, v)


def _ffn_fwd(u, w_gu, w_down, bsz, seq, tag):
    h, p = _ffn_up_act(u, w_gu, f"{tag}_up")
    f = _matmul(p, w_down, mode="nn", out_dtype=F32, name=f"{tag}_down")
    return h, p, f


def _ffn_bwd(df, u, h, p, w_gu, w_down, bsz, seq, tag):
    dh = _ffn_down_bwd_act(df, w_down, h, f"{tag}_ddown")
    g_down = _matmul(p, df, mode="tn", out_dtype=F32, name=f"{tag}_gdown")
    g_gu = _matmul(u, dh, mode="tn", out_dtype=F32, name=f"{tag}_ggu", out_stacked=True)
    du = _matmul(dh, w_gu, mode="nt", out_dtype=F32, name=f"{tag}_dup")
    return du, g_gu, g_down


def _local_step(x, c, target, wts, vecs, late_weights=None, early_grads=None):
    wts = dict(wts)
    bsz, seq, _ = x.shape
    t = bsz * seq
    x0 = x.reshape(t, D_MODEL)
    tgt = target.reshape(t, D_MODEL)

    sc = _silu_pad(c)
    mod16 = _matmul(sc, wts["w_ada"], mode="nn", out_dtype=F32, name="ada_fwd", bias=vecs["b_ada"])
    mod = mod16[:bsz].reshape(bsz, 9, D_MODEL)

    u1 = _mod_in(x0, mod, bsz, seq, 0)
    h1, p1, f1 = _ffn_fwd(u1, wts["ffn1_w_gu"], wts["ffn1_w_down"], bsz, seq, "ffn1")
    r1, x1, u2 = _res_ln_fwd(x0, f1, mod, vecs["ln1_g"], vecs["ln1_b"], bsz, seq, 0, 0.5)
    if late_weights is not None:
        wts.update(late_weights(r1))

    proj = _matmul(u2, wts["w_in"], mode="nn", out_dtype=F32, name="mix_in")
    ya, rtot = _attn_fwd(proj, bsz, seq)
    cs, cv = _conv_fwd(proj, wts["conv_w"], vecs["conv_b"], vecs["conv_ln_g"], vecs["conv_ln_b"], bsz, seq)
    ysb = _matmul(ya, wts["w_sb_out"], mode="nn", out_dtype=F32, name="sb_out")
    yconv = _matmul(cs, wts["w_conv_out"], mode="nn", out_dtype=F32, name="conv_out")
    merged = _merge_fwd(proj, ysb, yconv, bsz, seq)
    o2 = _matmul(merged, wts["w_out"], mode="nn", out_dtype=F32, name="mix_out")
    r2, x2, u3 = _res_ln_fwd(x1, o2, mod, vecs["ln2_g"], vecs["ln2_b"], bsz, seq, 1, 1.0)

    h3, p3, f3 = _ffn_fwd(u3, wts["ffn2_w_gu"], wts["ffn2_w_down"], bsz, seq, "ffn2")
    r3, dy, loss_blk = _res_ln_fwd(x2, f3, mod, vecs["ln3_g"], vecs["ln3_b"], bsz, seq, 2, 0.5, target=tgt)

    grads = {}
    dxres, df, ln3s, g3s = _res_ln_bwd(r3, dy, f3, mod, vecs["ln3_g"], bsz, seq, 2, 0.5)
    du, grads["ffn2_w_gu"], grads["ffn2_w_down"] = _ffn_bwd(
        df, u3, h3, p3, wts["ffn2_w_gu"], wts["ffn2_w_down"], bsz, seq, "ffn2")
    dx2, m3s = _mod_bwd(dxres, du, x2, mod, bsz, seq, 2)

    dxres, do2, ln2s, g2s = _res_ln_bwd(r2, dx2, o2, mod, vecs["ln2_g"], bsz, seq, 1, 1.0)
    dmerged = _matmul(do2, wts["w_out"], mode="nt", out_dtype=F32, name="mix_out_d")
    grads["w_out"] = _matmul(merged, do2, mode="tn", out_dtype=F32, name="mix_out_g")
    dysb, dyconv, dgate = _merge_bwd(proj, ysb, yconv, dmerged, bsz, seq)
    dya = _matmul(dysb, wts["w_sb_out"], mode="nt", out_dtype=BF16, name="sb_out_d")
    grads["w_sb_out"] = _matmul(ya, dysb, mode="tn", out_dtype=F32, name="sb_out_g")
    dcs = _matmul(dyconv, wts["w_conv_out"], mode="nt", out_dtype=F32, name="conv_out_d")
    grads["w_conv_out"] = _matmul(cs, dyconv, mode="tn", out_dtype=F32, name="conv_out_g")
    dcv, convs = _conv_bwd_ln(dcs, cv, vecs["conv_ln_g"], vecs["conv_ln_b"], bsz, seq)
    dglu, g_conv_w = _conv_bwd_taps(proj, dcv, wts["conv_w"], bsz, seq)
    dq, dk, dv = _attn_bwd(proj, rtot, dya, bsz, seq)
    dproj = jnp.concatenate([dq, dk, dv, dglu, dgate], axis=1)
    grads["w_in"] = _matmul(u2, dproj, mode="tn", out_dtype=F32, name="mix_in_g", out_stacked=True)
    if early_grads is not None:
        mod = mod + early_grads({n: grads.pop(n) for n in list(grads)})
    du = _matmul(dproj, wts["w_in"], mode="nt", out_dtype=F32, name="mix_in_d")
    dx1, m2s = _mod_bwd(dxres, du, x1, mod, bsz, seq, 1)

    dxres, df, ln1s, g1s = _res_ln_bwd(r1, dx1, f1, mod, vecs["ln1_g"], bsz, seq, 0, 0.5)
    du, grads["ffn1_w_gu"], grads["ffn1_w_down"] = _ffn_bwd(
        df, u1, h1, p1, wts["ffn1_w_gu"], wts["ffn1_w_down"], bsz, seq, "ffn1")
    grad_x, m1s = _mod_bwd(dxres, du, x0, mod, bsz, seq, 0)

    dmod = jnp.stack([m1s[:, 0], m1s[:, 1], g1s[:, 0], m2s[:, 0], m2s[:, 1], g2s[:, 0],
                      m3s[:, 0], m3s[:, 1], g3s[:, 0]], axis=1)
    dmod16 = jnp.zeros((16, 9 * D_MODEL), F32).at[:bsz].set(dmod.reshape(bsz, 9 * D_MODEL))
    grads["w_ada"] = _matmul(sc, dmod16.astype(BF16), mode="tn", out_dtype=F32, name="ada_g", out_stacked=True)

    small = {"dmod": dmod, "ln1": ln1s, "ln2": ln2s, "ln3": ln3s, "conv": convs, "conv_w": g_conv_w,
             "loss": loss_blk}
    return grad_x.reshape(x.shape), grads, small


_HBM = pl.BlockSpec(memory_space=pltpu.HBM)


def _position():
    return lax.axis_index("x"), lax.axis_index("y"), lax.axis_index("c")


def _other_chips(x, y):
    return [(1 - x, y), (x, 1 - y), (1 - x, 1 - y)]


def _cast_into_stack(w_local, chip, name):
    rows, cols = w_local.shape
    tr = _pick(rows, (256, 352, 128, 64, 32, 16))

    def body(chip_ref, w_ref, o_ref):
        o_ref[...] = w_ref[...].astype(BF16)

    return pl.pallas_call(
        body, name=name,
        grid_spec=pltpu.PrefetchScalarGridSpec(
            num_scalar_prefetch=1, grid=(rows // tr,),
            in_specs=[pl.BlockSpec((tr, cols), lambda r, chip_ref: (r, 0))],
            out_specs=pl.BlockSpec((None, tr, cols), lambda r, chip_ref: (chip_ref[0], r, 0))),
        out_shape=jax.ShapeDtypeStruct((N_CHIPS, rows, cols), BF16),
        compiler_params=_params(("parallel",)),
    )(chip, w_local)


def _all_gather_weights(stacks, small):
    n = len(stacks)

    def body(*refs):
        ins, small_in, outs, small_out = refs[:n], refs[n], refs[n + 1:2 * n + 1], refs[2 * n + 1]
        send_sems, recv_sems, fwd_send_sems, fwd_recv_sems, small_sems = refs[2 * n + 2:]
        x, y, c = _position()
        me = 2 * x + y
        chips = _other_chips(x, y)

        def send(i, j):
            px, py = chips[j]
            return pltpu.make_async_remote_copy(
                src_ref=ins[i].at[me, c], dst_ref=outs[i].at[me, c], send_sem=send_sems.at[3 * i + j],
                recv_sem=recv_sems.at[3 * i + j], device_id=(px, py, c), device_id_type=MESH)

        def landed(i, j):
            px, py = chips[j]
            return pltpu.make_async_remote_copy(
                src_ref=ins[i].at[me, c], dst_ref=outs[i].at[2 * px + py, c], send_sem=send_sems.at[3 * i + j],
                recv_sem=recv_sems.at[3 * i + j], device_id=(px, py, c), device_id_type=MESH)

        def forward(i, j, half):
            px, py = chips[j]
            blk = outs[i].at[2 * px + py, half]
            return pltpu.make_async_remote_copy(
                src_ref=blk, dst_ref=blk, send_sem=fwd_send_sems.at[3 * i + j],
                recv_sem=fwd_recv_sems.at[3 * i + j], device_id=(x, y, 1 - c), device_id_type=MESH)

        def small_copy(j, slot):
            px, py = chips[j]
            return pltpu.make_async_remote_copy(
                src_ref=small_in, dst_ref=small_out.at[slot], send_sem=small_sems.at[j],
                recv_sem=small_sems.at[3 + j], device_id=(px, py, c), device_id_type=MESH)

        own_small = pltpu.make_async_copy(small_in, small_out.at[me], small_sems.at[6])
        own_small.start()
        for j in range(3):
            small_copy(j, me).start()
        for i in range(n):
            for j in range(3):
                send(i, j).start()
        for i in range(n):
            for j in range(3):
                landed(i, j).wait_recv()
                forward(i, j, c).start()
        for i in range(n):
            for j in range(3):
                forward(i, j, 1 - c).wait_recv()
        for j, (px, py) in enumerate(chips):
            small_copy(j, 2 * px + py).wait_recv()
        own_small.wait()
        for j in range(3):
            small_copy(j, me).wait_send()
        for i in range(n):
            for j in range(3):
                send(i, j).wait_send()
                forward(i, j, c).wait_send()

    return pl.pallas_call(
        body, name="all_gather_weights",
        out_shape=[jax.ShapeDtypeStruct(s.shape, s.dtype) for s in stacks]
        + [jax.ShapeDtypeStruct((N_CHIPS,) + small.shape, small.dtype)],
        in_specs=[_HBM] * (n + 1), out_specs=[_HBM] * (n + 1),
        input_output_aliases={i: i for i in range(n)},
        scratch_shapes=[pltpu.SemaphoreType.DMA((3 * n,)), pltpu.SemaphoreType.DMA((3 * n,)),
                        pltpu.SemaphoreType.DMA((3 * n,)), pltpu.SemaphoreType.DMA((3 * n,)),
                        pltpu.SemaphoreType.DMA((7,))],
    )(*stacks, small)


_SEM = pl.BlockSpec(memory_space=pltpu.SEMAPHORE)
_DATAFLOW = pltpu.SideEffectType.DATAFLOW_SIDE_EFFECTING


def _exchange_refs(kind, src, land, me, c, chips, j):
    px, py = chips[j]
    if kind == "gather":
        return src.at[me, c], land.at[me, c], land.at[2 * px + py, c]
    return src.at[2 * px + py], land.at[j], land.at[j]


def _exchange_start(kind, srcs, lands, name):
    n = len(srcs)
    in_place = lands is None
    n_in = n if in_place else 2 * n

    def body(*refs):
        src_refs = refs[:n]
        land_refs = src_refs if in_place else refs[n:2 * n]
        send_sems, recv_sems = refs[n_in], refs[n_in + 1]
        token = refs[-1]
        x, y, c = _position()
        me = 2 * x + y
        chips = _other_chips(x, y)
        for i in range(n):
            for j in range(3):
                src, dst, _ = _exchange_refs(kind, src_refs[i], land_refs[i], me, c, chips, j)
                pltpu.make_async_remote_copy(
                    src_ref=src, dst_ref=dst, send_sem=send_sems.at[3 * i + j], recv_sem=recv_sems.at[3 * i + j],
                    device_id=(*chips[j], c), device_id_type=MESH).start()
        token[...] = jnp.zeros_like(token)

    operands = list(srcs) + ([] if in_place else list(lands))
    operands = [pltpu.with_memory_space_constraint(o, pltpu.HBM) for o in operands]
    out = pl.pallas_call(
        body, name=name,
        out_shape=[pltpu.SemaphoreType.DMA((3 * n,)), pltpu.SemaphoreType.DMA((3 * n,))]
        + [pltpu.HBM(o.shape, o.dtype) for o in operands] + [jax.ShapeDtypeStruct((8, LANES), F32)],
        in_specs=[_HBM] * n_in, out_specs=[_SEM, _SEM] + [_HBM] * n_in + [pl.BlockSpec(memory_space=pltpu.VMEM)],
        input_output_aliases={i: 2 + i for i in range(n_in)},
        compiler_params=pltpu.CompilerParams(has_side_effects=_DATAFLOW),
    )(*operands)
    return out[0], out[1], list(out[2:2 + n_in]), out[-1]


def _exchange_wait(kind, send_sems, recv_sems, thru, in_place, after, name):
    n_in = len(thru)
    n = n_in if in_place else n_in // 2

    def body(*refs):
        src_refs = refs[:n]
        land_refs = src_refs if in_place else refs[n:2 * n]
        send_sems, recv_sems = refs[n_in], refs[n_in + 1]
        x, y, c = _position()
        me = 2 * x + y
        chips = _other_chips(x, y)
        for i in range(n):
            for j in range(3):
                src, _, here = _exchange_refs(kind, src_refs[i], land_refs[i], me, c, chips, j)
                copy = pltpu.make_async_remote_copy(
                    src_ref=src, dst_ref=here, send_sem=send_sems.at[3 * i + j], recv_sem=recv_sems.at[3 * i + j],
                    device_id=(*chips[j], c), device_id_type=MESH)
                copy.wait_send()
                copy.wait_recv()

    out = pl.pallas_call(
        body, name=name, out_shape=[pltpu.HBM(o.shape, o.dtype) for o in thru],
        in_specs=[_HBM] * n_in + [_SEM, _SEM, pl.BlockSpec(memory_space=pl.ANY)], out_specs=[_HBM] * n_in,
        input_output_aliases={i: i for i in range(n_in)},
        compiler_params=pltpu.CompilerParams(has_side_effects=_DATAFLOW),
    )(*thru, send_sems, recv_sems, after)
    return list(out[:n]) if in_place else list(out[n:])


def _gather_forward(stacks):
    n = len(stacks)

    def body(*refs):
        ins, outs = refs[:n], refs[n:2 * n]
        send_sems, recv_sems = refs[2 * n:]
        x, y, c = _position()
        chips = _other_chips(x, y)

        def copy(i, j, half):
            px, py = chips[j]
            return pltpu.make_async_remote_copy(
                src_ref=ins[i].at[2 * px + py, half], dst_ref=outs[i].at[2 * px + py, half],
                send_sem=send_sems.at[3 * i + j], recv_sem=recv_sems.at[3 * i + j],
                device_id=(x, y, 1 - c), device_id_type=MESH)

        for i in range(n):
            for j in range(3):
                copy(i, j, c).start()
        for i in range(n):
            for j in range(3):
                copy(i, j, 1 - c).wait_recv()
        for i in range(n):
            for j in range(3):
                copy(i, j, c).wait_send()

    return pl.pallas_call(
        body, name="gather_forward", out_shape=[jax.ShapeDtypeStruct(s.shape, s.dtype) for s in stacks],
        in_specs=[_HBM] * n, out_specs=[_HBM] * n, input_output_aliases={i: i for i in range(n)},
        scratch_shapes=[pltpu.SemaphoreType.DMA((3 * n,)), pltpu.SemaphoreType.DMA((3 * n,))],
    )(*stacks)


def _pair_swap(grads, name):
    n = len(grads)

    def body(*refs):
        ins, outs = refs[:n], refs[n:2 * n]
        send_sems, recv_sems = refs[2 * n:]
        x, y, c = _position()

        def copy(i, k):
            return pltpu.make_async_remote_copy(
                src_ref=ins[i].at[k, 1 - c], dst_ref=outs[i].at[k], send_sem=send_sems.at[N_CHIPS * i + k],
                recv_sem=recv_sems.at[N_CHIPS * i + k], device_id=(x, y, 1 - c), device_id_type=MESH)

        for i in range(n):
            for k in range(N_CHIPS):
                copy(i, k).start()
        for i in range(n):
            for k in range(N_CHIPS):
                copy(i, k).wait_recv()
        for i in range(n):
            for k in range(N_CHIPS):
                copy(i, k).wait_send()

    return pl.pallas_call(
        body, name=name,
        out_shape=[jax.ShapeDtypeStruct((N_CHIPS,) + g.shape[2:], F32) for g in grads],
        in_specs=[_HBM] * n, out_specs=[_HBM] * n,
        scratch_shapes=[pltpu.SemaphoreType.DMA((N_CHIPS * n,)), pltpu.SemaphoreType.DMA((N_CHIPS * n,))],
    )(*grads)


def _pair_add(g, got, place, name):
    _, _, rh, cols = g.shape
    tr = _pick(rh, (256, 176, 128, 64, 32, 16, 8))

    def body(place_ref, g_ref, got_ref, p_ref, own_ref):
        s = g_ref[...] + got_ref[...]
        p_ref[...] = s.astype(BF16)

        @pl.when(pl.program_id(1) == place_ref[1])
        def _():
            own_ref[...] = s

    blk = pl.BlockSpec((None, tr, cols), lambda r, k, place_ref: (k, r, 0))
    return pl.pallas_call(
        body, name=name,
        grid_spec=pltpu.PrefetchScalarGridSpec(
            num_scalar_prefetch=1, grid=(rh // tr, N_CHIPS),
            in_specs=[pl.BlockSpec((None, None, tr, cols), lambda r, k, place_ref: (k, place_ref[0], r, 0)), blk],
            out_specs=[blk, pl.BlockSpec((tr, cols), lambda r, k, place_ref: (r, 0))]),
        out_shape=[jax.ShapeDtypeStruct((N_CHIPS, rh, cols), BF16), jax.ShapeDtypeStruct((rh, cols), F32)],
        compiler_params=_params(("parallel", "arbitrary")),
    )(place, g, got)


def _chip_scatter(sums):
    n = len(sums)

    def body(*refs):
        ins, outs = refs[:n], refs[n:2 * n]
        send_sems, recv_sems = refs[2 * n:]
        x, y, c = _position()
        chips = _other_chips(x, y)

        def copy(i, j):
            px, py = chips[j]
            return pltpu.make_async_remote_copy(
                src_ref=ins[i].at[2 * px + py], dst_ref=outs[i].at[j], send_sem=send_sems.at[3 * i + j],
                recv_sem=recv_sems.at[3 * i + j], device_id=(px, py, c), device_id_type=MESH)

        for i in range(n):
            for j in range(3):
                copy(i, j).start()
        for i in range(n):
            for j in range(3):
                copy(i, j).wait_recv()
        for i in range(n):
            for j in range(3):
                copy(i, j).wait_send()

    return pl.pallas_call(
        body, name="grad_chip_scatter",
        out_shape=[jax.ShapeDtypeStruct((3,) + s.shape[1:], BF16) for s in sums],
        in_specs=[_HBM] * n, out_specs=[_HBM] * n,
        scratch_shapes=[pltpu.SemaphoreType.DMA((3 * n,)), pltpu.SemaphoreType.DMA((3 * n,))],
    )(*sums)


def _chip_sum(own, parts, place, name):
    rh, cols = own.shape
    tr = _pick(rh, (256, 176, 128, 64, 32, 16, 8))

    def body(place_ref, own_ref, p_ref, o_ref):
        o_ref[...] = ((own_ref[...] + p_ref[0].astype(F32)) + p_ref[1].astype(F32)) + p_ref[2].astype(F32)

    return pl.pallas_call(
        body, name=name,
        grid_spec=pltpu.PrefetchScalarGridSpec(
            num_scalar_prefetch=1, grid=(rh // tr,),
            in_specs=[pl.BlockSpec((tr, cols), lambda r, place_ref: (r, 0)),
                      pl.BlockSpec((3, tr, cols), lambda r, place_ref: (0, r, 0))],
            out_specs=pl.BlockSpec((None, tr, cols), lambda r, place_ref: (place_ref[0], r, 0))),
        out_shape=jax.ShapeDtypeStruct((2, rh, cols), F32),
        compiler_params=_params(("parallel",)),
    )(place, own, parts)


def _pair_gather(halves):
    n = len(halves)

    def body(*refs):
        ins, outs = refs[:n], refs[n:2 * n]
        send_sems, recv_sems = refs[2 * n:]
        x, y, c = _position()

        def send(i):
            return pltpu.make_async_remote_copy(
                src_ref=ins[i].at[c], dst_ref=outs[i].at[c], send_sem=send_sems.at[i], recv_sem=recv_sems.at[i],
                device_id=(x, y, 1 - c), device_id_type=MESH)

        def landed(i):
            return pltpu.make_async_remote_copy(
                src_ref=ins[i].at[c], dst_ref=outs[i].at[1 - c], send_sem=send_sems.at[i], recv_sem=recv_sems.at[i],
                device_id=(x, y, 1 - c), device_id_type=MESH)

        for i in range(n):
            send(i).start()
        for i in range(n):
            landed(i).wait_recv()
        for i in range(n):
            send(i).wait_send()

    return pl.pallas_call(
        body, name="grad_pair_gather",
        out_shape=[jax.ShapeDtypeStruct(h.shape, F32) for h in halves],
        in_specs=[_HBM] * n, out_specs=[_HBM] * n,
        input_output_aliases={i: i for i in range(n)},
        scratch_shapes=[pltpu.SemaphoreType.DMA((n,)), pltpu.SemaphoreType.DMA((n,))],
    )(*halves)


_MOD_ROWS = 16


def _small_all_reduce(buf, bsz):
    rows, cols = buf.shape
    head = bsz * _MOD_ROWS
    out_rows = rows - head + _MOD_ROWS

    def body(in_ref, o_ref, gath, send_sems, recv_sems):
        x, y, c = _position()
        me = 4 * x + 2 * y + c

        def peer(mask):
            return (x ^ (mask >> 2), y ^ ((mask >> 1) & 1), c ^ (mask & 1))

        def copy(mask):
            return pltpu.make_async_remote_copy(
                src_ref=in_ref, dst_ref=gath.at[me], send_sem=send_sems.at[mask - 1],
                recv_sem=recv_sems.at[mask - 1], device_id=peer(mask), device_id_type=MESH)

        def arrival(mask):
            px, py, pc = peer(mask)
            return pltpu.make_async_remote_copy(
                src_ref=in_ref, dst_ref=gath.at[4 * px + 2 * py + pc], send_sem=send_sems.at[mask - 1],
                recv_sem=recv_sems.at[mask - 1], device_id=peer(mask), device_id_type=MESH)

        for mask in range(1, N_DEV):
            copy(mask).start()
        gath[me] = in_ref[...]
        for mask in range(1, N_DEV):
            arrival(mask).wait_recv()
        for mask in range(1, N_DEV):
            copy(mask).wait_send()
        acc = gath[0]
        for d in range(1, N_DEV):
            acc = acc + gath[d]
        mod = acc[0:_MOD_ROWS]
        for s in range(1, bsz):
            mod = mod + acc[s * _MOD_ROWS:(s + 1) * _MOD_ROWS]
        o_ref[0:_MOD_ROWS, :] = mod
        o_ref[_MOD_ROWS:, :] = acc[head:]

    vm = pl.BlockSpec(memory_space=pltpu.VMEM)
    return pl.pallas_call(
        body, name="small_all_reduce", in_specs=[vm], out_specs=vm,
        out_shape=jax.ShapeDtypeStruct((out_rows, cols), F32),
        scratch_shapes=[pltpu.VMEM((N_DEV, rows, cols), F32), pltpu.SemaphoreType.DMA((N_DEV - 1,)),
                        pltpu.SemaphoreType.DMA((N_DEV - 1,))],
        compiler_params=pltpu.CompilerParams(vmem_limit_bytes=VMEM_LIMIT),
    )(buf)


_COL_SHARDED = ("w_ada", "ffn1_w_gu", "w_in", "ffn2_w_gu")
_ROW_SHARDED = ("ffn1_w_down", "w_sb_out", "w_conv_out", "w_out", "ffn2_w_down")
_FIRST = ["w_ada", "ffn1_w_gu", "ffn1_w_down"]
_LATER = ["w_in", "w_sb_out", "w_conv_out", "w_out", "ffn2_w_gu", "ffn2_w_down"]
_VECS = ("b_ada", "ln1_g", "ln1_b", "conv_b", "conv_ln_g", "conv_ln_b", "ln2_g", "ln2_b", "ln3_g", "ln3_b")
_WEIGHTS = ("w_ada", "b_ada", "ffn1_w_gu", "ffn1_w_down", "ln1_g", "ln1_b", "w_in", "w_sb_out", "conv_w", "conv_b",
            "conv_ln_g", "conv_ln_b", "w_conv_out", "w_out", "ln2_g", "ln2_b", "ffn2_w_gu", "ffn2_w_down",
            "ln3_g", "ln3_b")


def _step(x, c, target, w, m, v):
    bsz = x.shape[0]
    chip = 2 * lax.axis_index("x") + lax.axis_index("y")
    core = lax.axis_index("c")

    chip_arr = jnp.reshape(chip, (1,)).astype(jnp.int32)
    place = jnp.stack([core, chip]).astype(jnp.int32)

    def stack_of(n):
        rows, cols = w[n].shape[1:]
        return _cast_into_stack(w[n][0], chip_arr, f"cast_{n}").reshape(N_CHIPS, 2, rows // 2, cols)

    def gathered_form(n, g):
        rows, cols = w[n].shape[1:]
        return g.reshape(N_CHIPS, rows, cols) if n in _COL_SHARDED else g.reshape(N_CHIPS * rows, cols)

    conv_w_local = jnp.pad(w["conv_w"][0], ((0, 1), (0, 0)))
    gathered = _all_gather_weights([stack_of(n) for n in _FIRST], conv_w_local)
    wts = {n: gathered_form(n, g) for n, g in zip(_FIRST, gathered[:-1])}
    wts["conv_w"] = gathered[-1].transpose(1, 0, 2).reshape(32, D_MODEL)
    send_w, recv_w, thru_w, token_w = _exchange_start("gather", [stack_of(n) for n in _LATER], None, "gather_start")
    vecs = {n: w[n] for n in _VECS}
    vecs["b_ada"] = vecs["b_ada"] + token_w[0, 0]

    def late_weights(after):
        landed = _exchange_wait("gather", send_w, recv_w, thru_w, True, after, "gather_wait")
        return {n: gathered_form(n, g) for n, g in zip(_LATER, _gather_forward(landed))}

    def reduce_start(grads, names, tag):
        views = [grads[n].reshape(N_CHIPS, 2, w[n].shape[1] // 2, w[n].shape[2]) for n in names]
        got = _pair_swap(views, f"grad_pair_swap_{tag}")
        return [_pair_add(g, r, place, f"pair_add_{n}") for n, g, r in zip(names, views, got)]

    early = {}

    def early_grads(grads):
        sums = reduce_start(grads, _LATER, "later")
        lands = [lax.empty((3,) + p.shape[1:], BF16) for p, _ in sums]
        send_g, recv_g, thru_g, token_g = _exchange_start("scatter", [p for p, _ in sums], lands, "scatter_start")
        early.update(sums=sums, send=send_g, recv=recv_g, thru=thru_g)
        return token_g[0, 0]

    grad_x, grads, small = _local_step(x, c, target, wts, vecs, late_weights, early_grads)

    sums_first = reduce_start(grads, _FIRST, "first")
    parts_first = _chip_scatter([p for p, _ in sums_first])
    parts_later = _exchange_wait("scatter", early["send"], early["recv"], early["thru"], False, grad_x, "scatter_wait")
    names = _FIRST + _LATER
    halves = [_chip_sum(own, p, place, f"chip_sum_{n}")
              for n, (_, own), p in zip(names, sums_first + early["sums"], list(parts_first) + parts_later)]
    full = _pair_gather(halves)
    g_out = {n: f.reshape(w[n].shape[1:]) for n, f in zip(names, full)}

    dmod = jnp.pad(small["dmod"], ((0, 0), (0, _MOD_ROWS - 9), (0, 0))).reshape(bsz * _MOD_ROWS, D_MODEL)
    loss_rows = jnp.pad(small["loss"], ((0, 0), (0, D_MODEL - LANES)))
    buf = jnp.concatenate([dmod, small["ln1"], small["ln2"], small["ln3"], small["conv"], small["conv_w"],
                           loss_rows], axis=0)
    red = _small_all_reduce(buf, bsz)
    o = _MOD_ROWS
    g_out["b_ada"] = red[0:9].reshape(1, 9 * D_MODEL)
    g_out["ln1_g"], g_out["ln1_b"] = red[o:o + 1], red[o + 1:o + 2]
    g_out["ln2_g"], g_out["ln2_b"] = red[o + 8:o + 9], red[o + 9:o + 10]
    g_out["ln3_g"], g_out["ln3_b"] = red[o + 16:o + 17], red[o + 17:o + 18]
    g_out["conv_ln_g"], g_out["conv_ln_b"], g_out["conv_b"] = red[o + 24:o + 25], red[o + 25:o + 26], red[o + 26:o + 27]
    cw = w["conv_w"].shape[2]
    g_out["conv_w"] = lax.dynamic_slice(red[o + 32:o + 32 + CONV_TAPS], (0, chip * cw), (CONV_TAPS, cw))
    loss = red[o + 64, 0]

    outs_g, outs_d, outs_m, outs_v = [], [], [], []
    for n in _WEIGHTS:
        shape = w[n].shape
        flat = shape[1:] if len(shape) == 3 else shape
        d, nm, nv = _adamw(w[n].reshape(flat), g_out[n].reshape(flat), m[n].reshape(flat), v[n].reshape(flat),
                           f"adamw_{n}")
        outs_g.append(g_out[n].reshape(shape))
        outs_d.append(d.reshape(shape))
        outs_m.append(nm.reshape(shape))
        outs_v.append(nv.reshape(shape))
    return (loss, grad_x, *outs_g, *outs_d, *outs_m, *outs_v)


def kernel(x, c, w_ada, b_ada, ffn1_w_gu, ffn1_w_down, ln1_g, ln1_b, w_in, w_sb_out, conv_w, conv_b, conv_ln_g, conv_ln_b, w_conv_out, w_out, ln2_g, ln2_b, ffn2_w_gu, ffn2_w_down, ln3_g, ln3_b, loss_target, m_w_ada, m_b_ada, m_ffn1_w_gu, m_ffn1_w_down, m_ln1_g, m_ln1_b, m_w_in, m_w_sb_out, m_conv_w, m_conv_b, m_conv_ln_g, m_conv_ln_b, m_w_conv_out, m_w_out, m_ln2_g, m_ln2_b, m_ffn2_w_gu, m_ffn2_w_down, m_ln3_g, m_ln3_b, v_w_ada, v_b_ada, v_ffn1_w_gu, v_ffn1_w_down, v_ln1_g, v_ln1_b, v_w_in, v_w_sb_out, v_conv_w, v_conv_b, v_conv_ln_g, v_conv_ln_b, v_w_conv_out, v_w_out, v_ln2_g, v_ln2_b, v_ffn2_w_gu, v_ffn2_w_down, v_ln3_g, v_ln3_b):
    given = dict(locals())
    w = {n: given[n] for n in _WEIGHTS}
    m = {n: given["m_" + n] for n in _WEIGHTS}
    v = {n: given["v_" + n] for n in _WEIGHTS}
    return _step(x, c, loss_target, w, m, v)
```

```python
import functools

import jax
import jax.numpy as jnp
from jax import lax
from jax.experimental import pallas as pl
from jax.experimental.pallas import tpu as pltpu

F32 = jnp.float32
BF16 = jnp.bfloat16

D_MODEL = 1024
D_FF = 2816
HEADS = 16
HEAD_DIM = 64
LANES = 128
CONV_TAPS = 31
HALO = 32
N_CHIPS = 4
N_DEV = 8
ALPHA = 2.0 ** 0.25
LN_EPS = 1e-5
ATT_BLOCK = 256
VMEM_LIMIT = 56 * 1024 * 1024

ADAM_LR = 0.001
ADAM_B1 = 0.9
ADAM_B2 = 0.999
ADAM_EPS = 1e-08
ADAM_WD = 0.01
ADAM_STEP = 10

MESH = pl.DeviceIdType.MESH


def _pick(n, cands):
    for t in cands:
        if t <= n and n % t == 0:
            return t
    return n


def _params(sem):
    return pltpu.CompilerParams(dimension_semantics=sem, vmem_limit_bytes=VMEM_LIMIT)


def _sigmoid(z):
    t = jnp.exp(-jnp.abs(z))
    return jnp.where(z >= 0, 1.0, t) / (1.0 + t)


def _silu(z):
    return z * _sigmoid(z)


def _dsilu(z):
    s = _sigmoid(z)
    return s * (1.0 + z * (1.0 - s))


def _ln_stats(r):
    mu = jnp.mean(r, axis=-1, keepdims=True)
    d = r - mu
    var = jnp.mean(d * d, axis=-1, keepdims=True)
    rstd = lax.rsqrt(var + LN_EPS)
    return d * rstd, rstd


def _colsum(v):
    return jnp.sum(v, axis=0, keepdims=True)


_DIMS = {"nn": (((1,), (0,)), ((), ())), "nt": (((1,), (1,)), ((), ())), "tn": (((0,), (0,)), ((), ()))}
_TN_CANDS = (1408, 1792, 1152, 1024, 512, 256, 128)
_TK_CANDS = (1024, 1408, 896, 512, 256, 128)


def _matmul(a, b, *, mode, out_dtype, name, bias=None, out_stacked=False):
    a_halves = mode == "nt" and a.ndim == 3
    b_halves = mode == "tn" and b.ndim == 3
    b_stacked = b.ndim == 3 and not b_halves
    if mode == "nn":
        m, k = a.shape
        n_c = b.shape[-1]
        n = n_c * (N_CHIPS if b_stacked else 1)
        k_c = k
    elif mode == "nt":
        m = a.shape[-2]
        k = a.shape[-1] * (2 if a_halves else 1)
        n = b.shape[-2]
        k_c = b.shape[-1]
        n_c = n
    else:
        k, m = a.shape
        n = b.shape[-1] * (2 if b_halves else 1)
        n_c = n // N_CHIPS if out_stacked else n
        k_c = k
    if mode == "tn":
        tm = _pick(m, (1024, 1408, 512, 256, 128))
        tk = _pick(k, (512, 256, 128, 64, 32, 16))
    else:
        tm = _pick(m, (1024, 512, 256, 128, 64, 32, 16))
        tk = _pick(k_c, _TK_CANDS)
    tn = _pick(n_c, _TN_CANDS)
    nb = n_c // tn
    kb = k_c // tk
    nk = k // tk
    grid = (m // tm, n // tn, nk)

    if mode == "nn":
        a_spec = pl.BlockSpec((tm, tk), lambda i, j, kk: (i, kk))
        if b_stacked:
            b_spec = pl.BlockSpec((None, tk, tn), lambda i, j, kk: (j // nb, kk, j % nb))
        else:
            b_spec = pl.BlockSpec((tk, tn), lambda i, j, kk: (kk, j))
    elif mode == "nt":
        if a_halves:
            ka = a.shape[-1] // tk
            a_spec = pl.BlockSpec((None, tm, tk), lambda i, j, kk: (kk // ka, i, kk % ka))
        else:
            a_spec = pl.BlockSpec((tm, tk), lambda i, j, kk: (i, kk))
        if b_stacked:
            b_spec = pl.BlockSpec((None, tn, tk), lambda i, j, kk: (kk // kb, j, kk % kb))
        else:
            b_spec = pl.BlockSpec((tn, tk), lambda i, j, kk: (j, kk))
    else:
        a_spec = pl.BlockSpec((tk, tm), lambda i, j, kk: (kk, i))
        if b_halves:
            nh = b.shape[-1] // tn
            b_spec = pl.BlockSpec((None, tk, tn), lambda i, j, kk: (j // nh, kk, j % nh))
        else:
            b_spec = pl.BlockSpec((tk, tn), lambda i, j, kk: (kk, j))
    if out_stacked:
        out_shape = jax.ShapeDtypeStruct((N_CHIPS, m, n_c), out_dtype)
        o_spec = pl.BlockSpec((None, tm, tn), lambda i, j, kk: (j // nb, i, j % nb))
    else:
        out_shape = jax.ShapeDtypeStruct((m, n), out_dtype)
        o_spec = pl.BlockSpec((tm, tn), lambda i, j, kk: (i, j))
    in_specs = [a_spec, b_spec]
    args = [a, b]
    if bias is not None:
        in_specs.append(pl.BlockSpec((1, tn), lambda i, j, kk: (0, j)))
        args.append(bias)
    dims = _DIMS[mode]

    def body(*refs):
        a_ref, b_ref = refs[0], refs[1]
        bias_ref = refs[2] if bias is not None else None
        o_ref, acc_ref = refs[-2], refs[-1]
        kk = pl.program_id(2)

        @pl.when(kk == 0)
        def _():
            acc_ref[...] = jnp.zeros_like(acc_ref)

        acc_ref[...] += lax.dot_general(a_ref[...], b_ref[...], dims, preferred_element_type=F32)

        @pl.when(kk == nk - 1)
        def _():
            r = acc_ref[...]
            if bias_ref is not None:
                r = r + bias_ref[...]
            o_ref[...] = r.astype(o_ref.dtype)

    return pl.pallas_call(
        body, name=name, grid=grid, in_specs=in_specs, out_specs=o_spec, out_shape=out_shape,
        scratch_shapes=[pltpu.VMEM((tm, tn), F32)],
        compiler_params=_params(("parallel", "parallel", "arbitrary")),
    )(*args)


def _row_grid(bsz, seq, ts):
    ns = seq // ts
    return (bsz, ns), ns


def _rows(ts, width, ns, col=0):
    return pl.BlockSpec((ts, width), lambda b, s: (b * ns + s, col))


def _mod_spec():
    return pl.BlockSpec((None, 9, D_MODEL), lambda b, s: (b, 0, 0))


def _vec_spec(rows=1, width=D_MODEL):
    return pl.BlockSpec((rows, width), lambda b, s: (0, 0))


def _silu_pad(c):
    bsz = c.shape[0]

    def body(c_ref, o_ref):
        o_ref[...] = jnp.zeros_like(o_ref)
        o_ref[0:bsz, :] = _silu(c_ref[...]).astype(BF16)

    return pl.pallas_call(body, name="silu_pad", out_shape=jax.ShapeDtypeStruct((16, D_MODEL), BF16))(c)


def _mod_in(x, mod, bsz, seq, sub):
    ts = _pick(seq, (512, 256, 128))
    grid, ns = _row_grid(bsz, seq, ts)

    def body(x_ref, mod_ref, u_ref):
        sh = mod_ref[3 * sub:3 * sub + 1, :]
        sc = mod_ref[3 * sub + 1:3 * sub + 2, :]
        u_ref[...] = (x_ref[...] * (1.0 + sc) + sh).astype(BF16)

    return pl.pallas_call(
        body, name=f"mod_in{sub}", grid=grid, in_specs=[_rows(ts, D_MODEL, ns), _mod_spec()],
        out_specs=_rows(ts, D_MODEL, ns), out_shape=jax.ShapeDtypeStruct(x.shape, BF16),
        compiler_params=_params(("parallel", "parallel")),
    )(x, mod)


_FFN_TN = D_FF // 2


def _ffn_up_act(u, w_gu, name):
    t = u.shape[0]
    tm = _pick(t, (512, 256, 128))
    tn = _FFN_TN

    def body(u_ref, wa_ref, wg_ref, h_ref, p_ref):
        u_v = u_ref[...]
        a = jnp.dot(u_v, wa_ref[...], preferred_element_type=F32)
        g = jnp.dot(u_v, wg_ref[...], preferred_element_type=F32)
        h_ref[0] = a.astype(BF16)
        h_ref[1] = g.astype(BF16)
        p_ref[...] = (_silu(a) * g).astype(BF16)

    return pl.pallas_call(
        body, name=name, grid=(2, t // tm),
        in_specs=[pl.BlockSpec((tm, D_MODEL), lambda j, i: (i, 0)),
                  pl.BlockSpec((None, D_MODEL, tn), lambda j, i: (j, 0, 0)),
                  pl.BlockSpec((None, D_MODEL, tn), lambda j, i: (j + 2, 0, 0))],
        out_specs=[pl.BlockSpec((2, tm, tn), lambda j, i: (0, i, j)),
                   pl.BlockSpec((tm, tn), lambda j, i: (i, j))],
        out_shape=[jax.ShapeDtypeStruct((2, t, D_FF), BF16), jax.ShapeDtypeStruct((t, D_FF), BF16)],
        compiler_params=_params(("parallel", "parallel")),
    )(u, w_gu, w_gu)


def _ffn_down_bwd_act(df, w_down, h, name):
    t = df.shape[0]
    tm = _pick(t, (512, 256, 128))
    tn = _FFN_TN

    def body(df_ref, wd_ref, h_ref, dh_ref):
        dp = lax.dot_general(df_ref[...], wd_ref[...], _DIMS["nt"], preferred_element_type=F32)
        a = h_ref[0].astype(F32)
        g = h_ref[1].astype(F32)
        dh_ref[0] = (dp * g * _dsilu(a)).astype(BF16)
        dh_ref[1] = (dp * _silu(a)).astype(BF16)

    blk = pl.BlockSpec((2, tm, tn), lambda j, i: (0, i, j))
    return pl.pallas_call(
        body, name=name, grid=(2, t // tm),
        in_specs=[pl.BlockSpec((tm, D_MODEL), lambda j, i: (i, 0)),
                  pl.BlockSpec((tn, D_MODEL), lambda j, i: (j, 0)), blk],
        out_specs=blk, out_shape=jax.ShapeDtypeStruct((2, t, D_FF), BF16),
        compiler_params=_params(("parallel", "parallel")),
    )(df, w_down, h)


def _res_ln_fwd(x, f, mod, ln_g, ln_b, bsz, seq, sub, weight, target=None):
    ts = _pick(seq, (256, 128))
    grid, ns = _row_grid(bsz, seq, ts)
    last = target is not None

    def body(*refs):
        x_ref, f_ref, mod_ref, g_ref, b_ref = refs[:5]
        gate = mod_ref[3 * sub + 2:3 * sub + 3, :]
        r = ALPHA * x_ref[...] + gate * (weight * f_ref[...])
        xhat, _ = _ln_stats(r)
        xo = xhat * g_ref[...] + b_ref[...]
        if last:
            t_ref, r_ref, dy_ref, loss_ref = refs[5:]
            diff = xo - t_ref[...]
            dy_ref[...] = diff * (1.0 / D_MODEL)
            part = 0.5 * jnp.sum(jnp.mean(diff * diff, axis=-1, keepdims=True), axis=0, keepdims=True)

            @pl.when((pl.program_id(0) == 0) & (pl.program_id(1) == 0))
            def _():
                loss_ref[...] = jnp.zeros_like(loss_ref)

            loss_ref[...] += jnp.broadcast_to(part, loss_ref.shape)
        else:
            r_ref, xo_ref, u_ref = refs[5:]
            xo_ref[...] = xo
            sh = mod_ref[3 * sub + 3:3 * sub + 4, :]
            sc = mod_ref[3 * sub + 4:3 * sub + 5, :]
            u_ref[...] = (xo * (1.0 + sc) + sh).astype(BF16)
        r_ref[...] = r

    row = _rows(ts, D_MODEL, ns)
    in_specs = [row, row, _mod_spec(), _vec_spec(), _vec_spec()]
    args = [x, f, mod, ln_g, ln_b]
    if last:
        in_specs.append(row)
        args.append(target)
        out_specs = [row, row, _vec_spec(8, LANES)]
        out_shape = [jax.ShapeDtypeStruct(x.shape, F32), jax.ShapeDtypeStruct(x.shape, F32),
                     jax.ShapeDtypeStruct((8, LANES), F32)]
        sem = ("arbitrary", "arbitrary")
    else:
        out_specs = [row, row, row]
        out_shape = [jax.ShapeDtypeStruct(x.shape, F32), jax.ShapeDtypeStruct(x.shape, F32),
                     jax.ShapeDtypeStruct(x.shape, BF16)]
        sem = ("parallel", "parallel")
    return pl.pallas_call(
        body, name=f"res_ln_fwd{sub}", grid=grid, in_specs=in_specs, out_specs=out_specs, out_shape=out_shape,
        compiler_params=_params(sem),
    )(*args)


def _res_ln_bwd(r, dxo, f, mod, ln_g, bsz, seq, sub, weight):
    ts = _pick(seq, (256, 128))
    grid, ns = _row_grid(bsz, seq, ts)

    def body(r_ref, dxo_ref, f_ref, mod_ref, g_ref, dxres_ref, df_ref, lns_ref, gs_ref):
        b, s = pl.program_id(0), pl.program_id(1)
        gate = mod_ref[3 * sub + 2:3 * sub + 3, :]
        xhat, rstd = _ln_stats(r_ref[...])
        dxo_v = dxo_ref[...]
        dxhat = dxo_v * g_ref[...]
        m1 = jnp.mean(dxhat, axis=-1, keepdims=True)
        m2 = jnp.mean(dxhat * xhat, axis=-1, keepdims=True)
        dr = rstd * (dxhat - m1 - xhat * m2)
        dxres_ref[...] = ALPHA * dr
        df_ref[...] = (dr * (gate * weight)).astype(BF16)

        @pl.when((b == 0) & (s == 0))
        def _():
            lns_ref[...] = jnp.zeros_like(lns_ref)

        @pl.when(s == 0)
        def _():
            gs_ref[...] = jnp.zeros_like(gs_ref)

        lns_ref[0:1, :] += _colsum(dxo_v * xhat)
        lns_ref[1:2, :] += _colsum(dxo_v)
        gs_ref[0:1, :] += _colsum(dr * (weight * f_ref[...]))

    row = _rows(ts, D_MODEL, ns)
    return pl.pallas_call(
        body, name=f"res_ln_bwd{sub}", grid=grid,
        in_specs=[row, row, row, _mod_spec(), _vec_spec()],
        out_specs=[row, row, _vec_spec(8), pl.BlockSpec((None, 8, D_MODEL), lambda b, s: (b, 0, 0))],
        out_shape=[jax.ShapeDtypeStruct(r.shape, F32), jax.ShapeDtypeStruct(r.shape, BF16),
                   jax.ShapeDtypeStruct((8, D_MODEL), F32), jax.ShapeDtypeStruct((bsz, 8, D_MODEL), F32)],
        compiler_params=_params(("arbitrary", "arbitrary")),
    )(r, dxo, f, mod, ln_g)


def _mod_bwd(dxres, du, x, mod, bsz, seq, sub):
    ts = _pick(seq, (256, 128))
    grid, ns = _row_grid(bsz, seq, ts)

    def body(dxres_ref, du_ref, x_ref, mod_ref, dx_ref, st_ref):
        s = pl.program_id(1)
        sc = mod_ref[3 * sub + 1:3 * sub + 2, :]
        du_v = du_ref[...]
        dx_ref[...] = dxres_ref[...] + du_v * (1.0 + sc)

        @pl.when(s == 0)
        def _():
            st_ref[...] = jnp.zeros_like(st_ref)

        st_ref[0:1, :] += _colsum(du_v)
        st_ref[1:2, :] += _colsum(du_v * x_ref[...])

    row = _rows(ts, D_MODEL, ns)
    return pl.pallas_call(
        body, name=f"mod_bwd{sub}", grid=grid, in_specs=[row, row, row, _mod_spec()],
        out_specs=[row, pl.BlockSpec((None, 8, D_MODEL), lambda b, s: (b, 0, 0))],
        out_shape=[jax.ShapeDtypeStruct(x.shape, F32), jax.ShapeDtypeStruct((bsz, 8, D_MODEL), F32)],
        compiler_params=_params(("parallel", "arbitrary")),
    )(dxres, du, x, mod)


_COL_GLU_A, _COL_GLU_B, _COL_GATE_A, _COL_GATE_B = 3, 4, 5, 6


def _merge_fwd(proj, ysb, yconv, bsz, seq):
    ts = _pick(seq, (512, 256, 128))
    grid, ns = _row_grid(bsz, seq, ts)

    def body(ga_ref, gb_ref, ysb_ref, yc_ref, o_ref):
        o_ref[...] = (_sigmoid(ga_ref[...]) * ysb_ref[...] + _sigmoid(gb_ref[...]) * yc_ref[...]).astype(BF16)

    row = _rows(ts, D_MODEL, ns)
    return pl.pallas_call(
        body, name="merge_fwd", grid=grid,
        in_specs=[_rows(ts, D_MODEL, ns, _COL_GATE_A), _rows(ts, D_MODEL, ns, _COL_GATE_B), row, row],
        out_specs=row, out_shape=jax.ShapeDtypeStruct(ysb.shape, BF16),
        compiler_params=_params(("parallel", "parallel")),
    )(proj, proj, ysb, yconv)


def _merge_bwd(proj, ysb, yconv, dmerged, bsz, seq):
    ts = _pick(seq, (256, 128))
    grid, ns = _row_grid(bsz, seq, ts)

    def body(ga_ref, gb_ref, ysb_ref, yc_ref, dm_ref, dysb_ref, dyc_ref, dg_ref):
        sa = _sigmoid(ga_ref[...])
        sb = _sigmoid(gb_ref[...])
        dm = dm_ref[...]
        dysb_ref[...] = (dm * sa).astype(BF16)
        dyc_ref[...] = (dm * sb).astype(BF16)
        dg_ref[:, :D_MODEL] = (dm * ysb_ref[...] * sa * (1.0 - sa)).astype(BF16)
        dg_ref[:, D_MODEL:] = (dm * yc_ref[...] * sb * (1.0 - sb)).astype(BF16)

    row = _rows(ts, D_MODEL, ns)
    t = ysb.shape[0]
    return pl.pallas_call(
        body, name="merge_bwd", grid=grid,
        in_specs=[_rows(ts, D_MODEL, ns, _COL_GATE_A), _rows(ts, D_MODEL, ns, _COL_GATE_B), row, row, row],
        out_specs=[row, row, _rows(ts, 2 * D_MODEL, ns)],
        out_shape=[jax.ShapeDtypeStruct((t, D_MODEL), BF16), jax.ShapeDtypeStruct((t, D_MODEL), BF16),
                   jax.ShapeDtypeStruct((t, 2 * D_MODEL), BF16)],
        compiler_params=_params(("parallel", "parallel")),
    )(proj, proj, ysb, yconv, dmerged)


_CONV_ROWS = 128


def _halo_prev(tt, ns, col):
    r = tt // HALO
    return pl.BlockSpec((HALO, D_MODEL), lambda b, s: (jnp.maximum((b * ns + s) * r - 1, 0), col))


def _halo_next(tt, ns, nblk, col):
    r = tt // HALO
    return pl.BlockSpec((HALO, D_MODEL), lambda b, s: (jnp.minimum((b * ns + s + 1) * r, nblk - 1), col))


def _windows(pad_ref, stage_ref, tt, offsets):
    for r in range(8):
        mine = [o for o in offsets if o % 8 == r]
        if not mine:
            continue
        n = max(mine) - r + tt
        stage_ref[0:n, :] = pad_ref[r:r + n, :]
        for o in mine:
            yield o, stage_ref[o - r:o - r + tt, :]


def _fill_hc(hpad, a_ref, b_ref, ha_ref, hb_ref, s):
    halo = ha_ref[...] * _sigmoid(hb_ref[...])
    hpad[0:HALO, :] = jnp.where(s > 0, halo, 0.0)
    hpad[HALO:, :] = a_ref[...] * _sigmoid(b_ref[...])


def _conv_fwd(proj, conv_w, conv_b, ln_g, ln_b, bsz, seq):
    tt = _CONV_ROWS
    grid, ns = _row_grid(bsz, seq, tt)
    off = HALO - (CONV_TAPS - 1)

    def body(a_ref, b_ref, ha_ref, hb_ref, w_ref, cb_ref, g_ref, bb_ref, cs_ref, cv_ref, hpad, stage):
        _fill_hc(hpad, a_ref, b_ref, ha_ref, hb_ref, pl.program_id(1))
        acc = jnp.zeros((tt, D_MODEL), F32)
        for o, win in _windows(hpad, stage, tt, [off + j for j in range(CONV_TAPS)]):
            acc = acc + w_ref[o - off:o - off + 1, :] * win
        cv = acc + cb_ref[...]
        cv_ref[...] = cv
        xhat, _ = _ln_stats(cv)
        cs_ref[...] = _silu(xhat * g_ref[...] + bb_ref[...]).astype(BF16)

    row = _rows(tt, D_MODEL, ns)
    t = proj.shape[0]
    return pl.pallas_call(
        body, name="conv_fwd", grid=grid,
        in_specs=[_rows(tt, D_MODEL, ns, _COL_GLU_A), _rows(tt, D_MODEL, ns, _COL_GLU_B),
                  _halo_prev(tt, ns, _COL_GLU_A), _halo_prev(tt, ns, _COL_GLU_B),
                  _vec_spec(32), _vec_spec(), _vec_spec(), _vec_spec()],
        out_specs=[row, row],
        out_shape=[jax.ShapeDtypeStruct((t, D_MODEL), BF16), jax.ShapeDtypeStruct((t, D_MODEL), F32)],
        scratch_shapes=[pltpu.VMEM((HALO + tt, D_MODEL), F32), pltpu.VMEM((HALO + tt, D_MODEL), F32)],
        compiler_params=_params(("parallel", "parallel")),
    )(proj, proj, proj, proj, conv_w, conv_b, ln_g, ln_b)


def _conv_bwd_ln(dcs, cv, ln_g, ln_b, bsz, seq):
    ts = _pick(seq, (256, 128))
    grid, ns = _row_grid(bsz, seq, ts)

    def body(dcs_ref, cv_ref, g_ref, b_ref, dcv_ref, st_ref):
        xhat, rstd = _ln_stats(cv_ref[...])
        cl = xhat * g_ref[...] + b_ref[...]
        dcl = dcs_ref[...] * _dsilu(cl)
        dxhat = dcl * g_ref[...]
        m1 = jnp.mean(dxhat, axis=-1, keepdims=True)
        m2 = jnp.mean(dxhat * xhat, axis=-1, keepdims=True)
        dcv = rstd * (dxhat - m1 - xhat * m2)
        dcv_ref[...] = dcv

        @pl.when((pl.program_id(0) == 0) & (pl.program_id(1) == 0))
        def _():
            st_ref[...] = jnp.zeros_like(st_ref)

        st_ref[0:1, :] += _colsum(dcl * xhat)
        st_ref[1:2, :] += _colsum(dcl)
        st_ref[2:3, :] += _colsum(dcv)

    row = _rows(ts, D_MODEL, ns)
    return pl.pallas_call(
        body, name="conv_bwd_ln", grid=grid, in_specs=[row, row, _vec_spec(), _vec_spec()],
        out_specs=[row, _vec_spec(8)],
        out_shape=[jax.ShapeDtypeStruct(cv.shape, F32), jax.ShapeDtypeStruct((8, D_MODEL), F32)],
        compiler_params=_params(("arbitrary", "arbitrary")),
    )(dcs, cv, ln_g, ln_b)


def _conv_bwd_taps(proj, dcv, conv_w, bsz, seq):
    tt = _CONV_ROWS
    grid, ns = _row_grid(bsz, seq, tt)
    off = HALO - (CONV_TAPS - 1)
    nblk = proj.shape[0] // HALO

    def body(a_ref, b_ref, ha_ref, hb_ref, d_ref, dn_ref, w_ref, dglu_ref, dw_ref, hpad, dpad, stage):
        s = pl.program_id(1)
        _fill_hc(hpad, a_ref, b_ref, ha_ref, hb_ref, s)
        dcv = d_ref[...]
        dpad[0:tt, :] = dcv
        dpad[tt:, :] = jnp.where(s < ns - 1, dn_ref[...], 0.0)

        @pl.when((pl.program_id(0) == 0) & (s == 0))
        def _():
            dw_ref[...] = jnp.zeros_like(dw_ref)

        dhc = jnp.zeros((tt, D_MODEL), F32)
        for o, win in _windows(dpad, stage, tt, list(range(CONV_TAPS))):
            j = CONV_TAPS - 1 - o
            dhc = dhc + w_ref[j:j + 1, :] * win
        for o, win in _windows(hpad, stage, tt, [off + j for j in range(CONV_TAPS)]):
            dw_ref[o - off:o - off + 1, :] += _colsum(dcv * win)
        sb = _sigmoid(b_ref[...])
        dglu_ref[:, :D_MODEL] = (dhc * sb).astype(BF16)
        dglu_ref[:, D_MODEL:] = (dhc * a_ref[...] * sb * (1.0 - sb)).astype(BF16)

    t = proj.shape[0]
    return pl.pallas_call(
        body, name="conv_bwd_taps", grid=grid,
        in_specs=[_rows(tt, D_MODEL, ns, _COL_GLU_A), _rows(tt, D_MODEL, ns, _COL_GLU_B),
                  _halo_prev(tt, ns, _COL_GLU_A), _halo_prev(tt, ns, _COL_GLU_B),
                  _rows(tt, D_MODEL, ns), _halo_next(tt, ns, nblk, 0), _vec_spec(32)],
        out_specs=[_rows(tt, 2 * D_MODEL, ns), _vec_spec(32)],
        out_shape=[jax.ShapeDtypeStruct((t, 2 * D_MODEL), BF16), jax.ShapeDtypeStruct((32, D_MODEL), F32)],
        scratch_shapes=[pltpu.VMEM((HALO + tt, D_MODEL), F32), pltpu.VMEM((tt + HALO, D_MODEL), F32),
                        pltpu.VMEM((HALO + tt, D_MODEL), F32)],
        compiler_params=_params(("arbitrary", "arbitrary")),
    )(proj, proj, proj, proj, dcv, dcv, conv_w)


_NT = (((1,), (1,)), ((), ()))
_TN = (((0,), (0,)), ((), ()))


def _dot(a, b, dims=None):
    if dims is None:
        return jnp.dot(a, b, preferred_element_type=F32)
    return lax.dot_general(a, b, dims, preferred_element_type=F32)


def _tri_dot(v, tri2):
    hi = v.astype(BF16)
    lo = (v - hi.astype(F32)).astype(BF16)
    return _dot(jnp.concatenate([hi, lo], axis=1), tri2)


def _tri2(mask):
    t = mask.astype(BF16)
    return jnp.concatenate([t, t], axis=0)


def _softplus_parts(z):
    t = jnp.exp(-jnp.abs(z))
    den = 1.0 + t
    return jnp.maximum(z, 0.0) + jnp.log(den), t, den


def _attn_fwd(proj, bsz, seq):
    blk = ATT_BLOCK
    nq = seq // blk
    n_pairs = D_MODEL // LANES

    def body(q_ref, k_ref, v_ref, y_ref, rt_ref, zr_buf, ns_buf, run_buf, acc_buf):
        qi = pl.program_id(2)
        lane = lax.broadcasted_iota(jnp.int32, (blk, LANES), 1)
        first = lane < HEAD_DIM
        q2 = q_ref[...] * 0.125
        q_heads = (jnp.where(first, q2, 0.0).astype(BF16), jnp.where(first, 0.0, q2).astype(BF16))
        rr = lax.broadcasted_iota(jnp.int32, (blk, blk), 0)
        cc = lax.broadcasted_iota(jnp.int32, (blk, blk), 1)
        tri_ge = _tri2(rr >= cc)
        causal = cc < rr

        def scores(kb, slot, masked, heads=(0, 1)):
            k_blk = k_ref[pl.ds(pl.multiple_of(kb * blk, blk), blk), :].astype(BF16)
            for h in heads:
                z = _dot(q_heads[h], k_blk, _NT)
                if masked:
                    z = jnp.where(causal, z, -1e30)
                sp, _, _ = _softplus_parts(z)
                neg = -sp
                zr_buf[slot, h] = z + _tri_dot(neg, tri_ge)
                ns_buf[slot, h] = jnp.sum(neg, axis=1, keepdims=True)

        def weigh(kb, slot, heads=(0, 1)):
            v_blk = v_ref[pl.ds(pl.multiple_of(kb * blk, blk), blk), :].astype(BF16)
            for h in heads:
                run = run_buf[h]
                w = jnp.exp(zr_buf[slot, h] + run)
                acc_buf[h] += _dot(w.astype(BF16), v_blk)
                run_buf[h] = run + ns_buf[slot, h]

        def step(kb_next, kb, slot):
            for h in range(2):
                scores(kb_next, 1 - slot, False, (h,))
                weigh(kb, slot, (h,))

        run_buf[...] = jnp.zeros_like(run_buf)
        acc_buf[...] = jnp.zeros_like(acc_buf)
        scores(qi, 0, True)

        def two_steps(p, carry):
            t = 2 * p
            step(qi - t - 1, qi - t, 0)
            step(qi - t - 2, qi - t - 1, 1)
            return carry

        lax.fori_loop(0, qi // 2, two_steps, 0)

        @pl.when(qi % 2 == 1)
        def _():
            step(0, 1, 0)
            weigh(0, 1)

        @pl.when(qi % 2 == 0)
        def _():
            weigh(0, 0)

        y_ref[...] = jnp.where(first, acc_buf[0], acc_buf[1]).astype(BF16)
        rt_ref[...] = jnp.where(first, jnp.broadcast_to(run_buf[0], (blk, LANES)),
                                jnp.broadcast_to(run_buf[1], (blk, LANES)))

    t = proj.shape[0]
    q_spec = pl.BlockSpec((blk, LANES), lambda b, p, i: (b * nq + i, p))
    return pl.pallas_call(
        body, name="attn_fwd", grid=(bsz, n_pairs, nq),
        in_specs=[q_spec,
                  pl.BlockSpec((seq, LANES), lambda b, p, i: (b, n_pairs + p)),
                  pl.BlockSpec((seq, LANES), lambda b, p, i: (b, 2 * n_pairs + p))],
        out_specs=[q_spec, q_spec],
        out_shape=[jax.ShapeDtypeStruct((t, D_MODEL), BF16), jax.ShapeDtypeStruct((t, D_MODEL), F32)],
        scratch_shapes=[pltpu.VMEM((2, 2, blk, blk), F32), pltpu.VMEM((2, 2, blk, 1), F32),
                        pltpu.VMEM((2, blk, 1), F32), pltpu.VMEM((2, blk, LANES), F32)],
        compiler_params=_params(("parallel", "parallel", "arbitrary")),
    )(proj, proj, proj)


def _attn_bwd(proj, rtot, dy, bsz, seq):
    blk = ATT_BLOCK
    nq = seq // blk
    n_pairs = D_MODEL // LANES

    def body(q_ref, k_ref, v_ref, dy_ref, rt_ref, dq_ref, dk_ref, dv_ref, dk_acc, dv_acc,
             a_buf, sig_buf, dw_buf, ns_buf, pre_buf, es_buf, dq_buf):
        qi = pl.program_id(2)

        @pl.when(qi == 0)
        def _():
            dk_acc[...] = jnp.zeros_like(dk_acc)
            dv_acc[...] = jnp.zeros_like(dv_acc)

        lane = lax.broadcasted_iota(jnp.int32, (blk, LANES), 1)
        first = lane < HEAD_DIM
        head_row = lax.broadcasted_iota(jnp.int32, (LANES, blk), 0) < HEAD_DIM
        q2 = q_ref[...] * 0.125
        q_rows = (jnp.where(first, q2, 0.0).astype(BF16), jnp.where(first, 0.0, q2).astype(BF16))
        q_t = q2.T
        q_heads = (jnp.where(head_row, q_t, 0.0).astype(BF16), jnp.where(head_row, 0.0, q_t).astype(BF16))
        dy2 = dy_ref[...].astype(F32)
        dy_rows = (jnp.where(first, dy2, 0.0).astype(BF16), jnp.where(first, 0.0, dy2).astype(BF16))
        dy_t = dy2.T
        dy_heads = (jnp.where(head_row, dy_t, 0.0).astype(BF16), jnp.where(head_row, 0.0, dy_t).astype(BF16))
        rt_t = rt_ref[...].T
        rt = (rt_t[0:1, :], rt_t[HEAD_DIM:HEAD_DIM + 1, :])
        rr = lax.broadcasted_iota(jnp.int32, (blk, blk), 0)
        cc = lax.broadcasted_iota(jnp.int32, (blk, blk), 1)
        lower = (cc < rr).astype(BF16)
        lower_eq = (cc <= rr).astype(BF16)
        tri_lt = jnp.concatenate([lower, lower], axis=1)
        tri_le = jnp.concatenate([lower_eq, lower_eq], axis=1)
        causal = rr < cc

        def tri_left(tri2, v):
            hi = v.astype(BF16)
            lo = (v - hi.astype(F32)).astype(BF16)
            return _dot(tri2, jnp.concatenate([hi, lo], axis=0))

        def scores(kb, slot, heads=(0, 1)):
            rows = pl.ds(pl.multiple_of(kb * blk, blk), blk)
            k_blk = k_ref[rows, :].astype(BF16)
            v_blk = v_ref[rows, :].astype(BF16)
            keep = jnp.logical_or(causal, kb < qi)
            for h in heads:
                z = jnp.where(keep, _dot(k_blk, q_heads[h]), -1e30)
                sp, t, den = _softplus_parts(z)
                neg = -sp
                a_buf[slot, h] = z - tri_left(tri_lt, neg)
                sig_buf[slot, h] = jnp.where(z >= 0, 1.0, t) / den
                ns_buf[slot, h] = jnp.sum(neg, axis=0, keepdims=True)
                dw_buf[slot, h] = _dot(v_blk, dy_heads[h])

        def finish(kb, slot, heads=(0, 1)):
            rows = pl.ds(pl.multiple_of(kb * blk, blk), blk)
            k_t = k_ref[rows, :].T.astype(BF16)
            for h in heads:
                pre, esum = pre_buf[h], es_buf[h]
                w = jnp.exp(a_buf[slot, h] + (rt[h] - pre))
                e = dw_buf[slot, h] * w
                dz = e - sig_buf[slot, h] * (esum + tri_left(tri_le, e))
                pre_buf[h] = pre + ns_buf[slot, h]
                es_buf[h] = esum + jnp.sum(e, axis=0, keepdims=True)
                dzb = dz.astype(BF16)
                dq_buf[h] += _dot(k_t, dzb)
                dk_acc[rows, :] += _dot(dzb, q_rows[h])
                dv_acc[rows, :] += _dot(w.astype(BF16), dy_rows[h])

        def step(kb_next, kb, slot):
            for h in range(2):
                scores(kb_next, 1 - slot, (h,))
                finish(kb, slot, (h,))

        pre_buf[...] = jnp.zeros_like(pre_buf)
        es_buf[...] = jnp.zeros_like(es_buf)
        dq_buf[...] = jnp.zeros_like(dq_buf)
        scores(0, 0)

        def two_steps(p, carry):
            t = 2 * p
            step(t + 1, t, 0)
            step(t + 2, t + 1, 1)
            return carry

        lax.fori_loop(0, qi // 2, two_steps, 0)

        @pl.when(qi % 2 == 1)
        def _():
            step(qi, qi - 1, 0)
            finish(qi, 1)

        @pl.when(qi % 2 == 0)
        def _():
            finish(qi, 0)

        dq_ref[...] = (jnp.where(head_row, dq_buf[0], dq_buf[1]).T * 0.125).astype(BF16)

        @pl.when(qi == nq - 1)
        def _():
            dk_ref[...] = dk_acc[...].astype(BF16)
            dv_ref[...] = dv_acc[...].astype(BF16)

    t = proj.shape[0]
    q_spec = pl.BlockSpec((blk, LANES), lambda b, p, i: (b * nq + i, p))
    kv_out = pl.BlockSpec((seq, LANES), lambda b, p, i: (b, p))
    out = jax.ShapeDtypeStruct((t, D_MODEL), BF16)
    return pl.pallas_call(
        body, name="attn_bwd", grid=(bsz, n_pairs, nq),
        in_specs=[q_spec,
                  pl.BlockSpec((seq, LANES), lambda b, p, i: (b, n_pairs + p)),
                  pl.BlockSpec((seq, LANES), lambda b, p, i: (b, 2 * n_pairs + p)),
                  q_spec, q_spec],
        out_specs=[q_spec, kv_out, kv_out], out_shape=[out, out, out],
        scratch_shapes=[pltpu.VMEM((seq, LANES), F32), pltpu.VMEM((seq, LANES), F32),
                        pltpu.VMEM((2, 2, blk, blk), F32), pltpu.VMEM((2, 2, blk, blk), F32),
                        pltpu.VMEM((2, 2, blk, blk), F32), pltpu.VMEM((2, 2, 1, blk), F32),
                        pltpu.VMEM((2, 1, blk), F32), pltpu.VMEM((2, 1, blk), F32),
                        pltpu.VMEM((2, LANES, blk), F32)],
        compiler_params=_params(("parallel", "parallel", "arbitrary")),
    )(proj, proj, proj, dy, rtot)


def _adamw(w, g, m, v, name):
    rows, cols = w.shape
    tr = _pick(rows, (256, 352, 128, 64, 32, 16, 8))
    c1 = 1.0 - ADAM_B1 ** ADAM_STEP
    c2 = 1.0 - ADAM_B2 ** ADAM_STEP

    def body(w_ref, g_ref, m_ref, v_ref, d_ref, nm_ref, nv_ref):
        g_v = g_ref[...]
        nm = ADAM_B1 * m_ref[...] + (1.0 - ADAM_B1) * g_v
        nv = ADAM_B2 * v_ref[...] + (1.0 - ADAM_B2) * (g_v * g_v)
        nm_ref[...] = nm
        nv_ref[...] = nv
        d_ref[...] = -ADAM_LR * ((nm / c1) / (jnp.sqrt(nv / c2) + ADAM_EPS) + ADAM_WD * w_ref[...])

    spec = pl.BlockSpec((tr, cols), lambda i: (i, 0))
    shape = jax.ShapeDtypeStruct(w.shape, F32)
    return pl.pallas_call(
        body, name=name, grid=(rows // tr,), in_specs=[spec] * 4, out_specs=[spec] * 3, out_shape=[shape] * 3,
        compiler_params=_params(("parallel",)),
    )(w, g, m, v)


def _ffn_fwd(u, w_gu, w_down, bsz, seq, tag):
    h, p = _ffn_up_act(u, w_gu, f"{tag}_up")
    f = _matmul(p, w_down, mode="nn", out_dtype=F32, name=f"{tag}_down")
    return h, p, f


def _ffn_bwd(df, u, h, p, w_gu, w_down, bsz, seq, tag):
    dh = _ffn_down_bwd_act(df, w_down, h, f"{tag}_ddown")
    g_down = _matmul(p, df, mode="tn", out_dtype=F32, name=f"{tag}_gdown")
    g_gu = _matmul(u, dh, mode="tn", out_dtype=F32, name=f"{tag}_ggu", out_stacked=True)
    du = _matmul(dh, w_gu, mode="nt", out_dtype=F32, name=f"{tag}_dup")
    return du, g_gu, g_down


def _local_step(x, c, target, wts, vecs, fetch=None, early_grads=None):
    wts = dict(wts)
    bsz, seq, _ = x.shape
    t = bsz * seq
    x0 = x.reshape(t, D_MODEL)
    tgt = target.reshape(t, D_MODEL)

    sc = _silu_pad(c)
    mod16 = _matmul(sc, wts["w_ada"], mode="nn", out_dtype=F32, name="ada_fwd", bias=vecs["b_ada"])
    mod = mod16[:bsz].reshape(bsz, 9, D_MODEL)

    u1 = _mod_in(x0, mod, bsz, seq, 0)
    h1, p1 = _ffn_up_act(u1, wts["ffn1_w_gu"], "ffn1_up")
    if fetch is not None:
        wts.update(fetch("down", p1))
    f1 = _matmul(p1, wts["ffn1_w_down"], mode="nn", out_dtype=F32, name="ffn1_down")
    r1, x1, u2 = _res_ln_fwd(x0, f1, mod, vecs["ln1_g"], vecs["ln1_b"], bsz, seq, 0, 0.5)
    if fetch is not None:
        wts.update(fetch("later", r1))

    proj = _matmul(u2, wts["w_in"], mode="nn", out_dtype=F32, name="mix_in")
    ya, rtot = _attn_fwd(proj, bsz, seq)
    cs, cv = _conv_fwd(proj, wts["conv_w"], vecs["conv_b"], vecs["conv_ln_g"], vecs["conv_ln_b"], bsz, seq)
    ysb = _matmul(ya, wts["w_sb_out"], mode="nn", out_dtype=F32, name="sb_out")
    yconv = _matmul(cs, wts["w_conv_out"], mode="nn", out_dtype=F32, name="conv_out")
    merged = _merge_fwd(proj, ysb, yconv, bsz, seq)
    o2 = _matmul(merged, wts["w_out"], mode="nn", out_dtype=F32, name="mix_out")
    r2, x2, u3 = _res_ln_fwd(x1, o2, mod, vecs["ln2_g"], vecs["ln2_b"], bsz, seq, 1, 1.0)

    h3, p3, f3 = _ffn_fwd(u3, wts["ffn2_w_gu"], wts["ffn2_w_down"], bsz, seq, "ffn2")
    r3, dy, loss_blk = _res_ln_fwd(x2, f3, mod, vecs["ln3_g"], vecs["ln3_b"], bsz, seq, 2, 0.5, target=tgt)

    grads = {}
    dxres, df, ln3s, g3s = _res_ln_bwd(r3, dy, f3, mod, vecs["ln3_g"], bsz, seq, 2, 0.5)
    du, grads["ffn2_w_gu"], grads["ffn2_w_down"] = _ffn_bwd(
        df, u3, h3, p3, wts["ffn2_w_gu"], wts["ffn2_w_down"], bsz, seq, "ffn2")
    dx2, m3s = _mod_bwd(dxres, du, x2, mod, bsz, seq, 2)

    dxres, do2, ln2s, g2s = _res_ln_bwd(r2, dx2, o2, mod, vecs["ln2_g"], bsz, seq, 1, 1.0)
    dmerged = _matmul(do2, wts["w_out"], mode="nt", out_dtype=F32, name="mix_out_d")
    grads["w_out"] = _matmul(merged, do2, mode="tn", out_dtype=F32, name="mix_out_g")
    dysb, dyconv, dgate = _merge_bwd(proj, ysb, yconv, dmerged, bsz, seq)
    dya = _matmul(dysb, wts["w_sb_out"], mode="nt", out_dtype=BF16, name="sb_out_d")
    grads["w_sb_out"] = _matmul(ya, dysb, mode="tn", out_dtype=F32, name="sb_out_g")
    dcs = _matmul(dyconv, wts["w_conv_out"], mode="nt", out_dtype=F32, name="conv_out_d")
    grads["w_conv_out"] = _matmul(cs, dyconv, mode="tn", out_dtype=F32, name="conv_out_g")
    dcv, convs = _conv_bwd_ln(dcs, cv, vecs["conv_ln_g"], vecs["conv_ln_b"], bsz, seq)
    dglu, g_conv_w = _conv_bwd_taps(proj, dcv, wts["conv_w"], bsz, seq)
    dq, dk, dv = _attn_bwd(proj, rtot, dya, bsz, seq)
    dproj = jnp.concatenate([dq, dk, dv, dglu, dgate], axis=1)
    grads["w_in"] = _matmul(u2, dproj, mode="tn", out_dtype=F32, name="mix_in_g", out_stacked=True)
    if early_grads is not None:
        mod = mod + early_grads("start", {n: grads.pop(n) for n in list(grads)})
    du = _matmul(dproj, wts["w_in"], mode="nt", out_dtype=F32, name="mix_in_d")
    dx1, m2s = _mod_bwd(dxres, du, x1, mod, bsz, seq, 1)
    if early_grads is not None:
        mod = mod + early_grads("go", dx1)

    dxres, df, ln1s, g1s = _res_ln_bwd(r1, dx1, f1, mod, vecs["ln1_g"], bsz, seq, 0, 0.5)
    du, grads["ffn1_w_gu"], grads["ffn1_w_down"] = _ffn_bwd(
        df, u1, h1, p1, wts["ffn1_w_gu"], wts["ffn1_w_down"], bsz, seq, "ffn1")
    grad_x, m1s = _mod_bwd(dxres, du, x0, mod, bsz, seq, 0)

    dmod = jnp.stack([m1s[:, 0], m1s[:, 1], g1s[:, 0], m2s[:, 0], m2s[:, 1], g2s[:, 0],
                      m3s[:, 0], m3s[:, 1], g3s[:, 0]], axis=1)
    dmod16 = jnp.zeros((16, 9 * D_MODEL), F32).at[:bsz].set(dmod.reshape(bsz, 9 * D_MODEL))
    grads["w_ada"] = _matmul(sc, dmod16.astype(BF16), mode="tn", out_dtype=F32, name="ada_g", out_stacked=True)

    small = {"dmod": dmod, "ln1": ln1s, "ln2": ln2s, "ln3": ln3s, "conv": convs, "conv_w": g_conv_w,
             "loss": loss_blk}
    return grad_x.reshape(x.shape), grads, small


_HBM = pl.BlockSpec(memory_space=pltpu.HBM)


def _position():
    return lax.axis_index("x"), lax.axis_index("y"), lax.axis_index("c")


def _other_chips(x, y):
    return [(1 - x, y), (x, 1 - y), (1 - x, 1 - y)]


def _cast_into_stack(w_local, chip, name):
    rows, cols = w_local.shape
    tr = _pick(rows, (256, 352, 128, 64, 32, 16))

    def body(chip_ref, w_ref, o_ref):
        o_ref[...] = w_ref[...].astype(BF16)

    return pl.pallas_call(
        body, name=name,
        grid_spec=pltpu.PrefetchScalarGridSpec(
            num_scalar_prefetch=1, grid=(rows // tr,),
            in_specs=[pl.BlockSpec((tr, cols), lambda r, chip_ref: (r, 0))],
            out_specs=pl.BlockSpec((None, tr, cols), lambda r, chip_ref: (chip_ref[0], r, 0))),
        out_shape=jax.ShapeDtypeStruct((N_CHIPS, rows, cols), BF16),
        compiler_params=_params(("parallel",)),
    )(chip, w_local)


def _all_gather_weights(stacks, small):
    n = len(stacks)

    def body(*refs):
        ins, small_in, outs, small_out = refs[:n], refs[n], refs[n + 1:2 * n + 1], refs[2 * n + 1]
        send_sems, recv_sems, fwd_send_sems, fwd_recv_sems, small_sems = refs[2 * n + 2:]
        x, y, c = _position()
        me = 2 * x + y
        chips = _other_chips(x, y)

        def send(i, j):
            px, py = chips[j]
            return pltpu.make_async_remote_copy(
                src_ref=ins[i].at[me, c], dst_ref=outs[i].at[me, c], send_sem=send_sems.at[3 * i + j],
                recv_sem=recv_sems.at[3 * i + j], device_id=(px, py, c), device_id_type=MESH)

        def landed(i, j):
            px, py = chips[j]
            return pltpu.make_async_remote_copy(
                src_ref=ins[i].at[me, c], dst_ref=outs[i].at[2 * px + py, c], send_sem=send_sems.at[3 * i + j],
                recv_sem=recv_sems.at[3 * i + j], device_id=(px, py, c), device_id_type=MESH)

        def forward(i, j, half):
            px, py = chips[j]
            blk = outs[i].at[2 * px + py, half]
            return pltpu.make_async_remote_copy(
                src_ref=blk, dst_ref=blk, send_sem=fwd_send_sems.at[3 * i + j],
                recv_sem=fwd_recv_sems.at[3 * i + j], device_id=(x, y, 1 - c), device_id_type=MESH)

        def small_copy(j, slot):
            px, py = chips[j]
            return pltpu.make_async_remote_copy(
                src_ref=small_in, dst_ref=small_out.at[slot], send_sem=small_sems.at[j],
                recv_sem=small_sems.at[3 + j], device_id=(px, py, c), device_id_type=MESH)

        own_small = pltpu.make_async_copy(small_in, small_out.at[me], small_sems.at[6])
        own_small.start()
        for j in range(3):
            small_copy(j, me).start()
        for i in range(n):
            for j in range(3):
                send(i, j).start()
        for i in range(n):
            for j in range(3):
                landed(i, j).wait_recv()
                forward(i, j, c).start()
        for i in range(n):
            for j in range(3):
                forward(i, j, 1 - c).wait_recv()
        for j, (px, py) in enumerate(chips):
            small_copy(j, 2 * px + py).wait_recv()
        own_small.wait()
        for j in range(3):
            small_copy(j, me).wait_send()
        for i in range(n):
            for j in range(3):
                send(i, j).wait_send()
                forward(i, j, c).wait_send()

    return pl.pallas_call(
        body, name="all_gather_weights",
        out_shape=[jax.ShapeDtypeStruct(s.shape, s.dtype) for s in stacks]
        + [jax.ShapeDtypeStruct((N_CHIPS,) + small.shape, small.dtype)],
        in_specs=[_HBM] * (n + 1), out_specs=[_HBM] * (n + 1),
        input_output_aliases={i: i for i in range(n)},
        scratch_shapes=[pltpu.SemaphoreType.DMA((3 * n,)), pltpu.SemaphoreType.DMA((3 * n,)),
                        pltpu.SemaphoreType.DMA((3 * n,)), pltpu.SemaphoreType.DMA((3 * n,)),
                        pltpu.SemaphoreType.DMA((7,))],
    )(*stacks, small)


_SEM = pl.BlockSpec(memory_space=pltpu.SEMAPHORE)
_DATAFLOW = pltpu.SideEffectType.DATAFLOW_SIDE_EFFECTING


_COPIES = {"gather": 3, "scatter": 3, "swap": N_CHIPS}


def _exchange_plan(kind, src, land):
    x, y, c = _position()
    me = 2 * x + y
    if kind == "swap":
        return [(src.at[k, 1 - c], land.at[k], land.at[k], (x, y, 1 - c)) for k in range(N_CHIPS)]
    plan = []
    for j, (px, py) in enumerate(_other_chips(x, y)):
        if kind == "gather":
            plan.append((src.at[me, c], land.at[me, c], land.at[2 * px + py, c], (px, py, c)))
        else:
            plan.append((src.at[2 * px + py], land.at[j], land.at[j], (px, py, c)))
    return plan


def _exchange_start(kind, srcs, lands, name, after):
    n = len(srcs)
    per = _COPIES[kind]
    in_place = lands is None
    n_in = n if in_place else 2 * n

    def body(*refs):
        src_refs = refs[:n]
        land_refs = src_refs if in_place else refs[n:2 * n]
        send_sems, recv_sems = refs[n_in + 1], refs[n_in + 2]
        token = refs[-1]
        for i in range(n):
            for j, (src, dst, _, to) in enumerate(_exchange_plan(kind, src_refs[i], land_refs[i])):
                pltpu.make_async_remote_copy(
                    src_ref=src, dst_ref=dst, send_sem=send_sems.at[per * i + j], recv_sem=recv_sems.at[per * i + j],
                    device_id=to, device_id_type=MESH).start()
        token[...] = jnp.zeros_like(token)

    operands = list(srcs) + ([] if in_place else list(lands))
    operands = [pltpu.with_memory_space_constraint(o, pltpu.HBM) for o in operands]
    out = pl.pallas_call(
        body, name=name,
        out_shape=[pltpu.SemaphoreType.DMA((per * n,)), pltpu.SemaphoreType.DMA((per * n,))]
        + [pltpu.HBM(o.shape, o.dtype) for o in operands] + [jax.ShapeDtypeStruct((8, LANES), F32)],
        in_specs=[_HBM] * n_in + [pl.BlockSpec(memory_space=pl.ANY)],
        out_specs=[_SEM, _SEM] + [_HBM] * n_in + [pl.BlockSpec(memory_space=pltpu.VMEM)],
        input_output_aliases={i: 2 + i for i in range(n_in)},
        compiler_params=pltpu.CompilerParams(has_side_effects=_DATAFLOW),
    )(*operands, after)
    return out[0], out[1], list(out[2:2 + n_in]), out[-1]


def _exchange_wait(kind, send_sems, recv_sems, thru, in_place, after, name):
    n_in = len(thru)
    n = n_in if in_place else n_in // 2
    per = _COPIES[kind]

    def body(*refs):
        src_refs = refs[:n]
        land_refs = src_refs if in_place else refs[n:2 * n]
        send_sems, recv_sems = refs[n_in], refs[n_in + 1]
        for i in range(n):
            for j, (src, _, here, to) in enumerate(_exchange_plan(kind, src_refs[i], land_refs[i])):
                copy = pltpu.make_async_remote_copy(
                    src_ref=src, dst_ref=here, send_sem=send_sems.at[per * i + j], recv_sem=recv_sems.at[per * i + j],
                    device_id=to, device_id_type=MESH)
                copy.wait_send()
                copy.wait_recv()

    out = pl.pallas_call(
        body, name=name, out_shape=[pltpu.HBM(o.shape, o.dtype) for o in thru],
        in_specs=[_HBM] * n_in + [_SEM, _SEM, pl.BlockSpec(memory_space=pl.ANY)], out_specs=[_HBM] * n_in,
        input_output_aliases={i: i for i in range(n_in)},
        compiler_params=pltpu.CompilerParams(has_side_effects=_DATAFLOW),
    )(*thru, send_sems, recv_sems, after)
    return list(out[:n]), (list(out[:n]) if in_place else list(out[n:]))


def _gather_forward(stacks, name):
    n = len(stacks)

    def body(*refs):
        ins, outs = refs[:n], refs[n:2 * n]
        send_sems, recv_sems = refs[2 * n:]
        x, y, c = _position()
        chips = _other_chips(x, y)

        def copy(i, j, half):
            px, py = chips[j]
            return pltpu.make_async_remote_copy(
                src_ref=ins[i].at[2 * px + py, half], dst_ref=outs[i].at[2 * px + py, half],
                send_sem=send_sems.at[3 * i + j], recv_sem=recv_sems.at[3 * i + j],
                device_id=(x, y, 1 - c), device_id_type=MESH)

        for i in range(n):
            for j in range(3):
                copy(i, j, c).start()
        for i in range(n):
            for j in range(3):
                copy(i, j, 1 - c).wait_recv()
        for i in range(n):
            for j in range(3):
                copy(i, j, c).wait_send()

    return pl.pallas_call(
        body, name=name, out_shape=[jax.ShapeDtypeStruct(s.shape, s.dtype) for s in stacks],
        in_specs=[_HBM] * n, out_specs=[_HBM] * n, input_output_aliases={i: i for i in range(n)},
        scratch_shapes=[pltpu.SemaphoreType.DMA((3 * n,)), pltpu.SemaphoreType.DMA((3 * n,))],
    )(*stacks)


def _pair_add(g, got, place, name):
    _, _, rh, cols = g.shape
    tr = _pick(rh, (256, 176, 128, 64, 32, 16, 8))

    def body(place_ref, g_ref, got_ref, p_ref, own_ref):
        s = g_ref[...] + got_ref[...]
        p_ref[...] = s.astype(BF16)

        @pl.when(pl.program_id(1) == place_ref[1])
        def _():
            own_ref[...] = s

    blk = pl.BlockSpec((None, tr, cols), lambda r, k, place_ref: (k, r, 0))
    return pl.pallas_call(
        body, name=name,
        grid_spec=pltpu.PrefetchScalarGridSpec(
            num_scalar_prefetch=1, grid=(rh // tr, N_CHIPS),
            in_specs=[pl.BlockSpec((None, None, tr, cols), lambda r, k, place_ref: (k, place_ref[0], r, 0)), blk],
            out_specs=[blk, pl.BlockSpec((tr, cols), lambda r, k, place_ref: (r, 0))]),
        out_shape=[jax.ShapeDtypeStruct((N_CHIPS, rh, cols), BF16), jax.ShapeDtypeStruct((rh, cols), F32)],
        compiler_params=_params(("parallel", "arbitrary")),
    )(place, g, got)


def _chip_sum(own, parts, place, name):
    rh, cols = own.shape
    tr = _pick(rh, (256, 176, 128, 64, 32, 16, 8))

    def body(place_ref, own_ref, p_ref, o_ref):
        o_ref[...] = ((own_ref[...] + p_ref[0].astype(F32)) + p_ref[1].astype(F32)) + p_ref[2].astype(F32)

    return pl.pallas_call(
        body, name=name,
        grid_spec=pltpu.PrefetchScalarGridSpec(
            num_scalar_prefetch=1, grid=(rh // tr,),
            in_specs=[pl.BlockSpec((tr, cols), lambda r, place_ref: (r, 0)),
                      pl.BlockSpec((3, tr, cols), lambda r, place_ref: (0, r, 0))],
            out_specs=pl.BlockSpec((None, tr, cols), lambda r, place_ref: (place_ref[0], r, 0))),
        out_shape=jax.ShapeDtypeStruct((2, rh, cols), F32),
        compiler_params=_params(("parallel",)),
    )(place, own, parts)


def _pair_gather(halves, name):
    n = len(halves)

    def body(*refs):
        ins, outs = refs[:n], refs[n:2 * n]
        send_sems, recv_sems = refs[2 * n:]
        x, y, c = _position()

        def send(i):
            return pltpu.make_async_remote_copy(
                src_ref=ins[i].at[c], dst_ref=outs[i].at[c], send_sem=send_sems.at[i], recv_sem=recv_sems.at[i],
                device_id=(x, y, 1 - c), device_id_type=MESH)

        def landed(i):
            return pltpu.make_async_remote_copy(
                src_ref=ins[i].at[c], dst_ref=outs[i].at[1 - c], send_sem=send_sems.at[i], recv_sem=recv_sems.at[i],
                device_id=(x, y, 1 - c), device_id_type=MESH)

        for i in range(n):
            send(i).start()
        for i in range(n):
            landed(i).wait_recv()
        for i in range(n):
            send(i).wait_send()

    return pl.pallas_call(
        body, name=name,
        out_shape=[jax.ShapeDtypeStruct(h.shape, F32) for h in halves],
        in_specs=[_HBM] * n, out_specs=[_HBM] * n,
        input_output_aliases={i: i for i in range(n)},
        scratch_shapes=[pltpu.SemaphoreType.DMA((n,)), pltpu.SemaphoreType.DMA((n,))],
    )(*halves)


_MOD_ROWS = 16


def _small_all_reduce(buf, bsz):
    rows, cols = buf.shape
    head = bsz * _MOD_ROWS
    out_rows = rows - head + _MOD_ROWS

    def body(in_ref, o_ref, gath, send_sems, recv_sems):
        x, y, c = _position()
        me = 4 * x + 2 * y + c

        def peer(mask):
            return (x ^ (mask >> 2), y ^ ((mask >> 1) & 1), c ^ (mask & 1))

        def copy(mask):
            return pltpu.make_async_remote_copy(
                src_ref=in_ref, dst_ref=gath.at[me], send_sem=send_sems.at[mask - 1],
                recv_sem=recv_sems.at[mask - 1], device_id=peer(mask), device_id_type=MESH)

        def arrival(mask):
            px, py, pc = peer(mask)
            return pltpu.make_async_remote_copy(
                src_ref=in_ref, dst_ref=gath.at[4 * px + 2 * py + pc], send_sem=send_sems.at[mask - 1],
                recv_sem=recv_sems.at[mask - 1], device_id=peer(mask), device_id_type=MESH)

        for mask in range(1, N_DEV):
            copy(mask).start()
        gath[me] = in_ref[...]
        for mask in range(1, N_DEV):
            arrival(mask).wait_recv()
        for mask in range(1, N_DEV):
            copy(mask).wait_send()
        acc = gath[0]
        for d in range(1, N_DEV):
            acc = acc + gath[d]
        mod = acc[0:_MOD_ROWS]
        for s in range(1, bsz):
            mod = mod + acc[s * _MOD_ROWS:(s + 1) * _MOD_ROWS]
        o_ref[0:_MOD_ROWS, :] = mod
        o_ref[_MOD_ROWS:, :] = acc[head:]

    vm = pl.BlockSpec(memory_space=pltpu.VMEM)
    return pl.pallas_call(
        body, name="small_all_reduce", in_specs=[vm], out_specs=vm,
        out_shape=jax.ShapeDtypeStruct((out_rows, cols), F32),
        scratch_shapes=[pltpu.VMEM((N_DEV, rows, cols), F32), pltpu.SemaphoreType.DMA((N_DEV - 1,)),
                        pltpu.SemaphoreType.DMA((N_DEV - 1,))],
        compiler_params=pltpu.CompilerParams(vmem_limit_bytes=VMEM_LIMIT),
    )(buf)


_COL_SHARDED = ("w_ada", "ffn1_w_gu", "w_in", "ffn2_w_gu")
_ROW_SHARDED = ("ffn1_w_down", "w_sb_out", "w_conv_out", "w_out", "ffn2_w_down")
_NOW = ["w_ada", "ffn1_w_gu"]
_SOON = ["ffn1_w_down"]
_LATER = ["w_in", "w_sb_out", "w_conv_out", "w_out", "ffn2_w_gu", "ffn2_w_down"]
_VECS = ("b_ada", "ln1_g", "ln1_b", "conv_b", "conv_ln_g", "conv_ln_b", "ln2_g", "ln2_b", "ln3_g", "ln3_b")
_WEIGHTS = ("w_ada", "b_ada", "ffn1_w_gu", "ffn1_w_down", "ln1_g", "ln1_b", "w_in", "w_sb_out", "conv_w", "conv_b",
            "conv_ln_g", "conv_ln_b", "w_conv_out", "w_out", "ln2_g", "ln2_b", "ffn2_w_gu", "ffn2_w_down",
            "ln3_g", "ln3_b")


def _step(x, c, target, w, m, v):
    bsz = x.shape[0]
    chip = 2 * lax.axis_index("x") + lax.axis_index("y")
    core = lax.axis_index("c")

    chip_arr = jnp.reshape(chip, (1,)).astype(jnp.int32)
    place = jnp.stack([core, chip]).astype(jnp.int32)

    def stack_of(n):
        rows, cols = w[n].shape[1:]
        return _cast_into_stack(w[n][0], chip_arr, f"cast_{n}").reshape(N_CHIPS, 2, rows // 2, cols)

    def gathered_form(n, g):
        rows, cols = w[n].shape[1:]
        return g.reshape(N_CHIPS, rows, cols) if n in _COL_SHARDED else g.reshape(N_CHIPS * rows, cols)

    conv_w_local = jnp.pad(w["conv_w"][0], ((0, 1), (0, 0)))
    gathered = _all_gather_weights([stack_of(n) for n in _NOW], conv_w_local)
    wts = {n: gathered_form(n, g) for n, g in zip(_NOW, gathered[:-1])}
    wts["conv_w"] = gathered[-1].transpose(1, 0, 2).reshape(32, D_MODEL)
    pending, behind = {}, gathered[0]
    for stage, names in (("down", _SOON), ("later", _LATER)):
        send, recv, thru, token = _exchange_start(
            "gather", [stack_of(n) for n in names], None, f"gather_start_{stage}", behind)
        pending[stage] = (names, send, recv, thru)
        behind = token
    vecs = {n: w[n] for n in _VECS}
    vecs["b_ada"] = vecs["b_ada"] + behind[0, 0]

    def fetch(stage, after):
        names, send, recv, thru = pending[stage]
        landed, _ = _exchange_wait("gather", send, recv, thru, True, after, f"gather_wait_{stage}")
        forwarded = _gather_forward(landed, f"gather_forward_{stage}")
        return {n: gathered_form(n, g) for n, g in zip(names, forwarded)}

    g_out, updates = {}, {}

    def adam(n):
        shape = w[n].shape
        flat = shape[1:] if len(shape) == 3 else shape
        d, nm, nv = _adamw(w[n].reshape(flat), g_out[n].reshape(flat), m[n].reshape(flat), v[n].reshape(flat),
                           f"adamw_{n}")
        updates[n] = (g_out[n].reshape(shape), d.reshape(shape), nm.reshape(shape), nv.reshape(shape))
        return nv

    def swap_start(grads, names, tag):
        views = [grads[n].reshape(N_CHIPS, 2, w[n].shape[1] // 2, w[n].shape[2]) for n in names]
        lands = [lax.empty((N_CHIPS,) + g.shape[2:], F32) for g in views]
        send, recv, thru, token = _exchange_start("swap", views, lands, f"swap_start_{tag}", views[0])
        return (send, recv, thru), token

    def scatter_start(swap, names, tag, after):
        views, got = _exchange_wait("swap", *swap, False, after, f"swap_wait_{tag}")
        sums = [_pair_add(g, r, place, f"pair_add_{n}") for n, g, r in zip(names, views, got)]
        lands = [lax.empty((3,) + p.shape[1:], BF16) for p, _ in sums]
        send, recv, thru, token = _exchange_start(
            "scatter", [p for p, _ in sums], lands, f"scatter_start_{tag}", sums[-1][1])
        return (send, recv, thru), sums, token

    def finish(scatter, sums, names, tag, after):
        _, parts = _exchange_wait("scatter", *scatter, False, after, f"scatter_wait_{tag}")
        halves = [_chip_sum(own, p, place, f"chip_sum_{n}") for n, (_, own), p in zip(names, sums, parts)]
        for n, f in zip(names, _pair_gather(halves, f"grad_pair_gather_{tag}")):
            g_out[n] = f.reshape(w[n].shape[1:])
        return [adam(n) for n in names][-1]

    later = {}

    def early_grads(stage, value):
        if stage == "start":
            later["swap"], token = swap_start(value, _LATER, "later")
        else:
            later["scatter"], later["sums"], token = scatter_start(later["swap"], _LATER, "later", value)
        return token[0, 0]

    grad_x, grads, small = _local_step(x, c, target, wts, vecs, fetch, early_grads)

    first = _NOW + _SOON
    swap_first, token = swap_start(grads, first, "first")
    done = finish(later["scatter"], later["sums"], _LATER, "later", token)
    scatter_first, sums_first, token = scatter_start(swap_first, first, "first", done)

    dmod = jnp.pad(small["dmod"], ((0, 0), (0, _MOD_ROWS - 9), (0, 0))).reshape(bsz * _MOD_ROWS, D_MODEL)
    loss_rows = jnp.pad(small["loss"], ((0, 0), (0, D_MODEL - LANES)))
    buf = jnp.concatenate([dmod + token[0, 0], small["ln1"], small["ln2"], small["ln3"], small["conv"],
                           small["conv_w"], loss_rows], axis=0)
    red = _small_all_reduce(buf, bsz)
    o = _MOD_ROWS
    g_out["b_ada"] = red[0:9].reshape(1, 9 * D_MODEL)
    g_out["ln1_g"], g_out["ln1_b"] = red[o:o + 1], red[o + 1:o + 2]
    g_out["ln2_g"], g_out["ln2_b"] = red[o + 8:o + 9], red[o + 9:o + 10]
    g_out["ln3_g"], g_out["ln3_b"] = red[o + 16:o + 17], red[o + 17:o + 18]
    g_out["conv_ln_g"], g_out["conv_ln_b"], g_out["conv_b"] = red[o + 24:o + 25], red[o + 25:o + 26], red[o + 26:o + 27]
    cw = w["conv_w"].shape[2]
    g_out["conv_w"] = lax.dynamic_slice(red[o + 32:o + 32 + CONV_TAPS], (0, chip * cw), (CONV_TAPS, cw))
    loss = red[o + 64, 0]

    done = [adam(n) for n in _VECS + ("conv_w",)][-1]
    finish(scatter_first, sums_first, first, "first", done)
    return (loss, grad_x, *[updates[n][k] for k in range(4) for n in _WEIGHTS])


def kernel(x, c, w_ada, b_ada, ffn1_w_gu, ffn1_w_down, ln1_g, ln1_b, w_in, w_sb_out, conv_w, conv_b, conv_ln_g, conv_ln_b, w_conv_out, w_out, ln2_g, ln2_b, ffn2_w_gu, ffn2_w_down, ln3_g, ln3_b, loss_target, m_w_ada, m_b_ada, m_ffn1_w_gu, m_ffn1_w_down, m_ln1_g, m_ln1_b, m_w_in, m_w_sb_out, m_conv_w, m_conv_b, m_conv_ln_g, m_conv_ln_b, m_w_conv_out, m_w_out, m_ln2_g, m_ln2_b, m_ffn2_w_gu, m_ffn2_w_down, m_ln3_g, m_ln3_b, v_w_ada, v_b_ada, v_ffn1_w_gu, v_ffn1_w_down, v_ln1_g, v_ln1_b, v_w_in, v_w_sb_out, v_conv_w, v_conv_b, v_conv_ln_g, v_conv_ln_b, v_w_conv_out, v_w_out, v_ln2_g, v_ln2_b, v_ffn2_w_gu, v_ffn2_w_down, v_ln3_g, v_ln3_b):
    given = dict(locals())
    w = {n: given[n] for n in _WEIGHTS}
    m = {n: given["m_" + n] for n in _WEIGHTS}
    v = {n: given["v_" + n] for n in _WEIGHTS}
    return _step(x, c, loss_target, w, m, v)
```

```python
import functools

import jax
import jax.numpy as jnp
from jax import lax
from jax.experimental import pallas as pl
from jax.experimental.pallas import tpu as pltpu

F32 = jnp.float32
BF16 = jnp.bfloat16

D_MODEL = 1024
D_FF = 2816
HEADS = 16
HEAD_DIM = 64
LANES = 128
CONV_TAPS = 31
HALO = 32
N_CHIPS = 4
N_DEV = 8
ALPHA = 2.0 ** 0.25
LN_EPS = 1e-5
ATT_BLOCK = 256
VMEM_LIMIT = 56 * 1024 * 1024

ADAM_LR = 0.001
ADAM_B1 = 0.9
ADAM_B2 = 0.999
ADAM_EPS = 1e-08
ADAM_WD = 0.01
ADAM_STEP = 10

MESH = pl.DeviceIdType.MESH


def _pick(n, cands):
    for t in cands:
        if t <= n and n % t == 0:
            return t
    return n


def _params(sem):
    return pltpu.CompilerParams(dimension_semantics=sem, vmem_limit_bytes=VMEM_LIMIT)


def _sigmoid(z):
    t = jnp.exp(-jnp.abs(z))
    return jnp.where(z >= 0, 1.0, t) / (1.0 + t)


def _silu(z):
    return z * _sigmoid(z)


def _dsilu(z):
    s = _sigmoid(z)
    return s * (1.0 + z * (1.0 - s))


def _ln_stats(r):
    mu = jnp.mean(r, axis=-1, keepdims=True)
    d = r - mu
    var = jnp.mean(d * d, axis=-1, keepdims=True)
    rstd = lax.rsqrt(var + LN_EPS)
    return d * rstd, rstd


def _colsum(v):
    return jnp.sum(v, axis=0, keepdims=True)


_DIMS = {"nn": (((1,), (0,)), ((), ())), "nt": (((1,), (1,)), ((), ())), "tn": (((0,), (0,)), ((), ()))}
_TN_CANDS = (1408, 1792, 1152, 1024, 512, 256, 128)
_TK_CANDS = (1024, 1408, 896, 512, 256, 128)


def _matmul(a, b, *, mode, out_dtype, name, bias=None, out_stacked=False):
    a_halves = mode == "nt" and a.ndim == 3
    b_halves = mode == "tn" and b.ndim == 3
    b_stacked = b.ndim == 3 and not b_halves
    if mode == "nn":
        m, k = a.shape
        n_c = b.shape[-1]
        n = n_c * (N_CHIPS if b_stacked else 1)
        k_c = k
    elif mode == "nt":
        m = a.shape[-2]
        k = a.shape[-1] * (2 if a_halves else 1)
        n = b.shape[-2]
        k_c = b.shape[-1]
        n_c = n
    else:
        k, m = a.shape
        n = b.shape[-1] * (2 if b_halves else 1)
        n_c = n // N_CHIPS if out_stacked else n
        k_c = k
    if mode == "tn":
        tm = _pick(m, (1024, 1408, 512, 256, 128))
        tk = _pick(k, (512, 256, 128, 64, 32, 16))
    else:
        tm = _pick(m, (1024, 512, 256, 128, 64, 32, 16))
        tk = _pick(k_c, _TK_CANDS)
    tn = _pick(n_c, _TN_CANDS)
    nb = n_c // tn
    kb = k_c // tk
    nk = k // tk
    grid = (m // tm, n // tn, nk)

    if mode == "nn":
        a_spec = pl.BlockSpec((tm, tk), lambda i, j, kk: (i, kk))
        if b_stacked:
            b_spec = pl.BlockSpec((None, tk, tn), lambda i, j, kk: (j // nb, kk, j % nb))
        else:
            b_spec = pl.BlockSpec((tk, tn), lambda i, j, kk: (kk, j))
    elif mode == "nt":
        if a_halves:
            ka = a.shape[-1] // tk
            a_spec = pl.BlockSpec((None, tm, tk), lambda i, j, kk: (kk // ka, i, kk % ka))
        else:
            a_spec = pl.BlockSpec((tm, tk), lambda i, j, kk: (i, kk))
        if b_stacked:
            b_spec = pl.BlockSpec((None, tn, tk), lambda i, j, kk: (kk // kb, j, kk % kb))
        else:
            b_spec = pl.BlockSpec((tn, tk), lambda i, j, kk: (j, kk))
    else:
        a_spec = pl.BlockSpec((tk, tm), lambda i, j, kk: (kk, i))
        if b_halves:
            nh = b.shape[-1] // tn
            b_spec = pl.BlockSpec((None, tk, tn), lambda i, j, kk: (j // nh, kk, j % nh))
        else:
            b_spec = pl.BlockSpec((tk, tn), lambda i, j, kk: (kk, j))
    if out_stacked:
        out_shape = jax.ShapeDtypeStruct((N_CHIPS, m, n_c), out_dtype)
        o_spec = pl.BlockSpec((None, tm, tn), lambda i, j, kk: (j // nb, i, j % nb))
    else:
        out_shape = jax.ShapeDtypeStruct((m, n), out_dtype)
        o_spec = pl.BlockSpec((tm, tn), lambda i, j, kk: (i, j))
    in_specs = [a_spec, b_spec]
    args = [a, b]
    if bias is not None:
        in_specs.append(pl.BlockSpec((1, tn), lambda i, j, kk: (0, j)))
        args.append(bias)
    dims = _DIMS[mode]

    def body(*refs):
        a_ref, b_ref = refs[0], refs[1]
        bias_ref = refs[2] if bias is not None else None
        o_ref, acc_ref = refs[-2], refs[-1]
        kk = pl.program_id(2)

        @pl.when(kk == 0)
        def _():
            acc_ref[...] = jnp.zeros_like(acc_ref)

        acc_ref[...] += lax.dot_general(a_ref[...], b_ref[...], dims, preferred_element_type=F32)

        @pl.when(kk == nk - 1)
        def _():
            r = acc_ref[...]
            if bias_ref is not None:
                r = r + bias_ref[...]
            o_ref[...] = r.astype(o_ref.dtype)

    return pl.pallas_call(
        body, name=name, grid=grid, in_specs=in_specs, out_specs=o_spec, out_shape=out_shape,
        scratch_shapes=[pltpu.VMEM((tm, tn), F32)],
        compiler_params=_params(("parallel", "parallel", "arbitrary")),
    )(*args)


def _row_grid(bsz, seq, ts):
    ns = seq // ts
    return (bsz, ns), ns


def _rows(ts, width, ns, col=0):
    return pl.BlockSpec((ts, width), lambda b, s: (b * ns + s, col))


def _mod_spec():
    return pl.BlockSpec((None, 9, D_MODEL), lambda b, s: (b, 0, 0))


def _vec_spec(rows=1, width=D_MODEL):
    return pl.BlockSpec((rows, width), lambda b, s: (0, 0))


def _silu_pad(c):
    bsz = c.shape[0]

    def body(c_ref, o_ref):
        o_ref[...] = jnp.zeros_like(o_ref)
        o_ref[0:bsz, :] = _silu(c_ref[...]).astype(BF16)

    return pl.pallas_call(body, name="silu_pad", out_shape=jax.ShapeDtypeStruct((16, D_MODEL), BF16))(c)


def _mod_in(x, mod, bsz, seq, sub):
    ts = _pick(seq, (512, 256, 128))
    grid, ns = _row_grid(bsz, seq, ts)

    def body(x_ref, mod_ref, u_ref):
        sh = mod_ref[3 * sub:3 * sub + 1, :]
        sc = mod_ref[3 * sub + 1:3 * sub + 2, :]
        u_ref[...] = (x_ref[...] * (1.0 + sc) + sh).astype(BF16)

    return pl.pallas_call(
        body, name=f"mod_in{sub}", grid=grid, in_specs=[_rows(ts, D_MODEL, ns), _mod_spec()],
        out_specs=_rows(ts, D_MODEL, ns), out_shape=jax.ShapeDtypeStruct(x.shape, BF16),
        compiler_params=_params(("parallel", "parallel")),
    )(x, mod)


_FFN_TN = D_FF // 2


def _ffn_up_act(u, w_gu, name):
    t = u.shape[0]
    tm = _pick(t, (512, 256, 128))
    tn = _FFN_TN

    def body(u_ref, wa_ref, wg_ref, h_ref, p_ref):
        u_v = u_ref[...]
        a = jnp.dot(u_v, wa_ref[...], preferred_element_type=F32)
        g = jnp.dot(u_v, wg_ref[...], preferred_element_type=F32)
        h_ref[0] = a.astype(BF16)
        h_ref[1] = g.astype(BF16)
        p_ref[...] = (_silu(a) * g).astype(BF16)

    return pl.pallas_call(
        body, name=name, grid=(2, t // tm),
        in_specs=[pl.BlockSpec((tm, D_MODEL), lambda j, i: (i, 0)),
                  pl.BlockSpec((None, D_MODEL, tn), lambda j, i: (j, 0, 0)),
                  pl.BlockSpec((None, D_MODEL, tn), lambda j, i: (j + 2, 0, 0))],
        out_specs=[pl.BlockSpec((2, tm, tn), lambda j, i: (0, i, j)),
                   pl.BlockSpec((tm, tn), lambda j, i: (i, j))],
        out_shape=[jax.ShapeDtypeStruct((2, t, D_FF), BF16), jax.ShapeDtypeStruct((t, D_FF), BF16)],
        compiler_params=_params(("parallel", "parallel")),
    )(u, w_gu, w_gu)


def _ffn_down_bwd_act(df, w_down, h, name):
    t = df.shape[0]
    tm = _pick(t, (512, 256, 128))
    tn = _FFN_TN

    def body(df_ref, wd_ref, h_ref, dh_ref):
        dp = lax.dot_general(df_ref[...], wd_ref[...], _DIMS["nt"], preferred_element_type=F32)
        a = h_ref[0].astype(F32)
        g = h_ref[1].astype(F32)
        dh_ref[0] = (dp * g * _dsilu(a)).astype(BF16)
        dh_ref[1] = (dp * _silu(a)).astype(BF16)

    blk = pl.BlockSpec((2, tm, tn), lambda j, i: (0, i, j))
    return pl.pallas_call(
        body, name=name, grid=(2, t // tm),
        in_specs=[pl.BlockSpec((tm, D_MODEL), lambda j, i: (i, 0)),
                  pl.BlockSpec((tn, D_MODEL), lambda j, i: (j, 0)), blk],
        out_specs=blk, out_shape=jax.ShapeDtypeStruct((2, t, D_FF), BF16),
        compiler_params=_params(("parallel", "parallel")),
    )(df, w_down, h)


def _res_ln_fwd(x, f, mod, ln_g, ln_b, bsz, seq, sub, weight, target=None):
    ts = _pick(seq, (256, 128))
    grid, ns = _row_grid(bsz, seq, ts)
    last = target is not None

    def body(*refs):
        x_ref, f_ref, mod_ref, g_ref, b_ref = refs[:5]
        gate = mod_ref[3 * sub + 2:3 * sub + 3, :]
        r = ALPHA * x_ref[...] + gate * (weight * f_ref[...])
        xhat, _ = _ln_stats(r)
        xo = xhat * g_ref[...] + b_ref[...]
        if last:
            t_ref, r_ref, dy_ref, loss_ref = refs[5:]
            diff = xo - t_ref[...]
            dy_ref[...] = diff * (1.0 / D_MODEL)
            part = 0.5 * jnp.sum(jnp.mean(diff * diff, axis=-1, keepdims=True), axis=0, keepdims=True)

            @pl.when((pl.program_id(0) == 0) & (pl.program_id(1) == 0))
            def _():
                loss_ref[...] = jnp.zeros_like(loss_ref)

            loss_ref[...] += jnp.broadcast_to(part, loss_ref.shape)
        else:
            r_ref, xo_ref, u_ref = refs[5:]
            xo_ref[...] = xo
            sh = mod_ref[3 * sub + 3:3 * sub + 4, :]
            sc = mod_ref[3 * sub + 4:3 * sub + 5, :]
            u_ref[...] = (xo * (1.0 + sc) + sh).astype(BF16)
        r_ref[...] = r

    row = _rows(ts, D_MODEL, ns)
    in_specs = [row, row, _mod_spec(), _vec_spec(), _vec_spec()]
    args = [x, f, mod, ln_g, ln_b]
    if last:
        in_specs.append(row)
        args.append(target)
        out_specs = [row, row, _vec_spec(8, LANES)]
        out_shape = [jax.ShapeDtypeStruct(x.shape, F32), jax.ShapeDtypeStruct(x.shape, F32),
                     jax.ShapeDtypeStruct((8, LANES), F32)]
        sem = ("arbitrary", "arbitrary")
    else:
        out_specs = [row, row, row]
        out_shape = [jax.ShapeDtypeStruct(x.shape, F32), jax.ShapeDtypeStruct(x.shape, F32),
                     jax.ShapeDtypeStruct(x.shape, BF16)]
        sem = ("parallel", "parallel")
    return pl.pallas_call(
        body, name=f"res_ln_fwd{sub}", grid=grid, in_specs=in_specs, out_specs=out_specs, out_shape=out_shape,
        compiler_params=_params(sem),
    )(*args)


def _res_ln_bwd(r, dxo, f, mod, ln_g, bsz, seq, sub, weight):
    ts = _pick(seq, (256, 128))
    grid, ns = _row_grid(bsz, seq, ts)

    def body(r_ref, dxo_ref, f_ref, mod_ref, g_ref, dxres_ref, df_ref, lns_ref, gs_ref):
        b, s = pl.program_id(0), pl.program_id(1)
        gate = mod_ref[3 * sub + 2:3 * sub + 3, :]
        xhat, rstd = _ln_stats(r_ref[...])
        dxo_v = dxo_ref[...]
        dxhat = dxo_v * g_ref[...]
        m1 = jnp.mean(dxhat, axis=-1, keepdims=True)
        m2 = jnp.mean(dxhat * xhat, axis=-1, keepdims=True)
        dr = rstd * (dxhat - m1 - xhat * m2)
        dxres_ref[...] = ALPHA * dr
        df_ref[...] = (dr * (gate * weight)).astype(BF16)

        @pl.when((b == 0) & (s == 0))
        def _():
            lns_ref[...] = jnp.zeros_like(lns_ref)

        @pl.when(s == 0)
        def _():
            gs_ref[...] = jnp.zeros_like(gs_ref)

        lns_ref[0:1, :] += _colsum(dxo_v * xhat)
        lns_ref[1:2, :] += _colsum(dxo_v)
        gs_ref[0:1, :] += _colsum(dr * (weight * f_ref[...]))

    row = _rows(ts, D_MODEL, ns)
    return pl.pallas_call(
        body, name=f"res_ln_bwd{sub}", grid=grid,
        in_specs=[row, row, row, _mod_spec(), _vec_spec()],
        out_specs=[row, row, _vec_spec(8), pl.BlockSpec((None, 8, D_MODEL), lambda b, s: (b, 0, 0))],
        out_shape=[jax.ShapeDtypeStruct(r.shape, F32), jax.ShapeDtypeStruct(r.shape, BF16),
                   jax.ShapeDtypeStruct((8, D_MODEL), F32), jax.ShapeDtypeStruct((bsz, 8, D_MODEL), F32)],
        compiler_params=_params(("arbitrary", "arbitrary")),
    )(r, dxo, f, mod, ln_g)


def _mod_bwd(dxres, du, x, mod, bsz, seq, sub):
    ts = _pick(seq, (256, 128))
    grid, ns = _row_grid(bsz, seq, ts)

    def body(dxres_ref, du_ref, x_ref, mod_ref, dx_ref, st_ref):
        s = pl.program_id(1)
        sc = mod_ref[3 * sub + 1:3 * sub + 2, :]
        du_v = du_ref[...]
        dx_ref[...] = dxres_ref[...] + du_v * (1.0 + sc)

        @pl.when(s == 0)
        def _():
            st_ref[...] = jnp.zeros_like(st_ref)

        st_ref[0:1, :] += _colsum(du_v)
        st_ref[1:2, :] += _colsum(du_v * x_ref[...])

    row = _rows(ts, D_MODEL, ns)
    return pl.pallas_call(
        body, name=f"mod_bwd{sub}", grid=grid, in_specs=[row, row, row, _mod_spec()],
        out_specs=[row, pl.BlockSpec((None, 8, D_MODEL), lambda b, s: (b, 0, 0))],
        out_shape=[jax.ShapeDtypeStruct(x.shape, F32), jax.ShapeDtypeStruct((bsz, 8, D_MODEL), F32)],
        compiler_params=_params(("parallel", "arbitrary")),
    )(dxres, du, x, mod)


_COL_GLU_A, _COL_GLU_B, _COL_GATE_A, _COL_GATE_B = 3, 4, 5, 6


def _merge_fwd(proj, ysb, yconv, bsz, seq):
    ts = _pick(seq, (512, 256, 128))
    grid, ns = _row_grid(bsz, seq, ts)

    def body(ga_ref, gb_ref, ysb_ref, yc_ref, o_ref):
        o_ref[...] = (_sigmoid(ga_ref[...]) * ysb_ref[...] + _sigmoid(gb_ref[...]) * yc_ref[...]).astype(BF16)

    row = _rows(ts, D_MODEL, ns)
    return pl.pallas_call(
        body, name="merge_fwd", grid=grid,
        in_specs=[_rows(ts, D_MODEL, ns, _COL_GATE_A), _rows(ts, D_MODEL, ns, _COL_GATE_B), row, row],
        out_specs=row, out_shape=jax.ShapeDtypeStruct(ysb.shape, BF16),
        compiler_params=_params(("parallel", "parallel")),
    )(proj, proj, ysb, yconv)


def _merge_bwd(proj, ysb, yconv, dmerged, bsz, seq):
    ts = _pick(seq, (256, 128))
    grid, ns = _row_grid(bsz, seq, ts)

    def body(ga_ref, gb_ref, ysb_ref, yc_ref, dm_ref, dysb_ref, dyc_ref, dg_ref):
        sa = _sigmoid(ga_ref[...])
        sb = _sigmoid(gb_ref[...])
        dm = dm_ref[...]
        dysb_ref[...] = (dm * sa).astype(BF16)
        dyc_ref[...] = (dm * sb).astype(BF16)
        dg_ref[:, :D_MODEL] = (dm * ysb_ref[...] * sa * (1.0 - sa)).astype(BF16)
        dg_ref[:, D_MODEL:] = (dm * yc_ref[...] * sb * (1.0 - sb)).astype(BF16)

    row = _rows(ts, D_MODEL, ns)
    t = ysb.shape[0]
    return pl.pallas_call(
        body, name="merge_bwd", grid=grid,
        in_specs=[_rows(ts, D_MODEL, ns, _COL_GATE_A), _rows(ts, D_MODEL, ns, _COL_GATE_B), row, row, row],
        out_specs=[row, row, _rows(ts, 2 * D_MODEL, ns)],
        out_shape=[jax.ShapeDtypeStruct((t, D_MODEL), BF16), jax.ShapeDtypeStruct((t, D_MODEL), BF16),
                   jax.ShapeDtypeStruct((t, 2 * D_MODEL), BF16)],
        compiler_params=_params(("parallel", "parallel")),
    )(proj, proj, ysb, yconv, dmerged)


_CONV_ROWS = 128


def _halo_prev(tt, ns, col):
    r = tt // HALO
    return pl.BlockSpec((HALO, D_MODEL), lambda b, s: (jnp.maximum((b * ns + s) * r - 1, 0), col))


def _halo_next(tt, ns, nblk, col):
    r = tt // HALO
    return pl.BlockSpec((HALO, D_MODEL), lambda b, s: (jnp.minimum((b * ns + s + 1) * r, nblk - 1), col))


def _windows(pad_ref, stage_ref, tt, offsets):
    for r in range(8):
        mine = [o for o in offsets if o % 8 == r]
        if not mine:
            continue
        n = max(mine) - r + tt
        stage_ref[0:n, :] = pad_ref[r:r + n, :]
        for o in mine:
            yield o, stage_ref[o - r:o - r + tt, :]


def _fill_hc(hpad, a_ref, b_ref, ha_ref, hb_ref, s):
    halo = ha_ref[...] * _sigmoid(hb_ref[...])
    hpad[0:HALO, :] = jnp.where(s > 0, halo, 0.0)
    hpad[HALO:, :] = a_ref[...] * _sigmoid(b_ref[...])


def _conv_fwd(proj, conv_w, conv_b, ln_g, ln_b, bsz, seq):
    tt = _CONV_ROWS
    grid, ns = _row_grid(bsz, seq, tt)
    off = HALO - (CONV_TAPS - 1)

    def body(a_ref, b_ref, ha_ref, hb_ref, w_ref, cb_ref, g_ref, bb_ref, cs_ref, cv_ref, hpad, stage):
        _fill_hc(hpad, a_ref, b_ref, ha_ref, hb_ref, pl.program_id(1))
        acc = jnp.zeros((tt, D_MODEL), F32)
        for o, win in _windows(hpad, stage, tt, [off + j for j in range(CONV_TAPS)]):
            acc = acc + w_ref[o - off:o - off + 1, :] * win
        cv = acc + cb_ref[...]
        cv_ref[...] = cv
        xhat, _ = _ln_stats(cv)
        cs_ref[...] = _silu(xhat * g_ref[...] + bb_ref[...]).astype(BF16)

    row = _rows(tt, D_MODEL, ns)
    t = proj.shape[0]
    return pl.pallas_call(
        body, name="conv_fwd", grid=grid,
        in_specs=[_rows(tt, D_MODEL, ns, _COL_GLU_A), _rows(tt, D_MODEL, ns, _COL_GLU_B),
                  _halo_prev(tt, ns, _COL_GLU_A), _halo_prev(tt, ns, _COL_GLU_B),
                  _vec_spec(32), _vec_spec(), _vec_spec(), _vec_spec()],
        out_specs=[row, row],
        out_shape=[jax.ShapeDtypeStruct((t, D_MODEL), BF16), jax.ShapeDtypeStruct((t, D_MODEL), F32)],
        scratch_shapes=[pltpu.VMEM((HALO + tt, D_MODEL), F32), pltpu.VMEM((HALO + tt, D_MODEL), F32)],
        compiler_params=_params(("parallel", "parallel")),
    )(proj, proj, proj, proj, conv_w, conv_b, ln_g, ln_b)


def _conv_bwd_ln(dcs, cv, ln_g, ln_b, bsz, seq):
    ts = _pick(seq, (256, 128))
    grid, ns = _row_grid(bsz, seq, ts)

    def body(dcs_ref, cv_ref, g_ref, b_ref, dcv_ref, st_ref):
        xhat, rstd = _ln_stats(cv_ref[...])
        cl = xhat * g_ref[...] + b_ref[...]
        dcl = dcs_ref[...] * _dsilu(cl)
        dxhat = dcl * g_ref[...]
        m1 = jnp.mean(dxhat, axis=-1, keepdims=True)
        m2 = jnp.mean(dxhat * xhat, axis=-1, keepdims=True)
        dcv = rstd * (dxhat - m1 - xhat * m2)
        dcv_ref[...] = dcv

        @pl.when((pl.program_id(0) == 0) & (pl.program_id(1) == 0))
        def _():
            st_ref[...] = jnp.zeros_like(st_ref)

        st_ref[0:1, :] += _colsum(dcl * xhat)
        st_ref[1:2, :] += _colsum(dcl)
        st_ref[2:3, :] += _colsum(dcv)

    row = _rows(ts, D_MODEL, ns)
    return pl.pallas_call(
        body, name="conv_bwd_ln", grid=grid, in_specs=[row, row, _vec_spec(), _vec_spec()],
        out_specs=[row, _vec_spec(8)],
        out_shape=[jax.ShapeDtypeStruct(cv.shape, F32), jax.ShapeDtypeStruct((8, D_MODEL), F32)],
        compiler_params=_params(("arbitrary", "arbitrary")),
    )(dcs, cv, ln_g, ln_b)


def _conv_bwd_taps(proj, dcv, conv_w, bsz, seq):
    tt = _CONV_ROWS
    grid, ns = _row_grid(bsz, seq, tt)
    off = HALO - (CONV_TAPS - 1)
    nblk = proj.shape[0] // HALO

    def body(a_ref, b_ref, ha_ref, hb_ref, d_ref, dn_ref, w_ref, dglu_ref, dw_ref, hpad, dpad, stage):
        s = pl.program_id(1)
        _fill_hc(hpad, a_ref, b_ref, ha_ref, hb_ref, s)
        dcv = d_ref[...]
        dpad[0:tt, :] = dcv
        dpad[tt:, :] = jnp.where(s < ns - 1, dn_ref[...], 0.0)

        @pl.when((pl.program_id(0) == 0) & (s == 0))
        def _():
            dw_ref[...] = jnp.zeros_like(dw_ref)

        dhc = jnp.zeros((tt, D_MODEL), F32)
        for o, win in _windows(dpad, stage, tt, list(range(CONV_TAPS))):
            j = CONV_TAPS - 1 - o
            dhc = dhc + w_ref[j:j + 1, :] * win
        for o, win in _windows(hpad, stage, tt, [off + j for j in range(CONV_TAPS)]):
            dw_ref[o - off:o - off + 1, :] += _colsum(dcv * win)
        sb = _sigmoid(b_ref[...])
        dglu_ref[:, :D_MODEL] = (dhc * sb).astype(BF16)
        dglu_ref[:, D_MODEL:] = (dhc * a_ref[...] * sb * (1.0 - sb)).astype(BF16)

    t = proj.shape[0]
    return pl.pallas_call(
        body, name="conv_bwd_taps", grid=grid,
        in_specs=[_rows(tt, D_MODEL, ns, _COL_GLU_A), _rows(tt, D_MODEL, ns, _COL_GLU_B),
                  _halo_prev(tt, ns, _COL_GLU_A), _halo_prev(tt, ns, _COL_GLU_B),
                  _rows(tt, D_MODEL, ns), _halo_next(tt, ns, nblk, 0), _vec_spec(32)],
        out_specs=[_rows(tt, 2 * D_MODEL, ns), _vec_spec(32)],
        out_shape=[jax.ShapeDtypeStruct((t, 2 * D_MODEL), BF16), jax.ShapeDtypeStruct((32, D_MODEL), F32)],
        scratch_shapes=[pltpu.VMEM((HALO + tt, D_MODEL), F32), pltpu.VMEM((tt + HALO, D_MODEL), F32),
                        pltpu.VMEM((HALO + tt, D_MODEL), F32)],
        compiler_params=_params(("arbitrary", "arbitrary")),
    )(proj, proj, proj, proj, dcv, dcv, conv_w)


_NT = (((1,), (1,)), ((), ()))
_TN = (((0,), (0,)), ((), ()))


def _dot(a, b, dims=None):
    if dims is None:
        return jnp.dot(a, b, preferred_element_type=F32)
    return lax.dot_general(a, b, dims, preferred_element_type=F32)


def _tri_dot(v, tri2):
    hi = v.astype(BF16)
    lo = (v - hi.astype(F32)).astype(BF16)
    return _dot(jnp.concatenate([hi, lo], axis=1), tri2)


def _tri2(mask):
    t = mask.astype(BF16)
    return jnp.concatenate([t, t], axis=0)


def _softplus_parts(z):
    t = jnp.exp(-jnp.abs(z))
    den = 1.0 + t
    return jnp.maximum(z, 0.0) + jnp.log(den), t, den


def _attn_fwd(proj, bsz, seq):
    blk = ATT_BLOCK
    nq = seq // blk
    n_pairs = D_MODEL // LANES

    def body(q_ref, k_ref, v_ref, y_ref, rt_ref, zr_buf, ns_buf, run_buf, acc_buf):
        qi = pl.program_id(2)
        lane = lax.broadcasted_iota(jnp.int32, (blk, LANES), 1)
        first = lane < HEAD_DIM
        q2 = q_ref[...] * 0.125
        q_heads = (jnp.where(first, q2, 0.0).astype(BF16), jnp.where(first, 0.0, q2).astype(BF16))
        rr = lax.broadcasted_iota(jnp.int32, (blk, blk), 0)
        cc = lax.broadcasted_iota(jnp.int32, (blk, blk), 1)
        tri_ge = _tri2(rr >= cc)
        causal = cc < rr

        def scores(kb, slot, masked, heads=(0, 1)):
            k_blk = k_ref[pl.ds(pl.multiple_of(kb * blk, blk), blk), :].astype(BF16)
            for h in heads:
                z = _dot(q_heads[h], k_blk, _NT)
                if masked:
                    z = jnp.where(causal, z, -1e30)
                sp, _, _ = _softplus_parts(z)
                neg = -sp
                zr_buf[slot, h] = z + _tri_dot(neg, tri_ge)
                ns_buf[slot, h] = jnp.sum(neg, axis=1, keepdims=True)

        def weigh(kb, slot, heads=(0, 1)):
            v_blk = v_ref[pl.ds(pl.multiple_of(kb * blk, blk), blk), :].astype(BF16)
            for h in heads:
                run = run_buf[h]
                w = jnp.exp(zr_buf[slot, h] + run)
                acc_buf[h] += _dot(w.astype(BF16), v_blk)
                run_buf[h] = run + ns_buf[slot, h]

        def step(kb_next, kb, slot):
            for h in range(2):
                scores(kb_next, 1 - slot, False, (h,))
                weigh(kb, slot, (h,))

        run_buf[...] = jnp.zeros_like(run_buf)
        acc_buf[...] = jnp.zeros_like(acc_buf)
        scores(qi, 0, True)

        def two_steps(p, carry):
            t = 2 * p
            step(qi - t - 1, qi - t, 0)
            step(qi - t - 2, qi - t - 1, 1)
            return carry

        lax.fori_loop(0, qi // 2, two_steps, 0)

        @pl.when(qi % 2 == 1)
        def _():
            step(0, 1, 0)
            weigh(0, 1)

        @pl.when(qi % 2 == 0)
        def _():
            weigh(0, 0)

        y_ref[...] = jnp.where(first, acc_buf[0], acc_buf[1]).astype(BF16)
        rt_ref[...] = jnp.where(first, jnp.broadcast_to(run_buf[0], (blk, LANES)),
                                jnp.broadcast_to(run_buf[1], (blk, LANES)))

    t = proj.shape[0]
    q_spec = pl.BlockSpec((blk, LANES), lambda b, p, i: (b * nq + i, p))
    return pl.pallas_call(
        body, name="attn_fwd", grid=(bsz, n_pairs, nq),
        in_specs=[q_spec,
                  pl.BlockSpec((seq, LANES), lambda b, p, i: (b, n_pairs + p)),
                  pl.BlockSpec((seq, LANES), lambda b, p, i: (b, 2 * n_pairs + p))],
        out_specs=[q_spec, q_spec],
        out_shape=[jax.ShapeDtypeStruct((t, D_MODEL), BF16), jax.ShapeDtypeStruct((t, D_MODEL), F32)],
        scratch_shapes=[pltpu.VMEM((2, 2, blk, blk), F32), pltpu.VMEM((2, 2, blk, 1), F32),
                        pltpu.VMEM((2, blk, 1), F32), pltpu.VMEM((2, blk, LANES), F32)],
        compiler_params=_params(("parallel", "parallel", "arbitrary")),
    )(proj, proj, proj)


def _attn_bwd(proj, rtot, dy, bsz, seq):
    blk = ATT_BLOCK
    nq = seq // blk
    n_pairs = D_MODEL // LANES

    def body(q_ref, k_ref, v_ref, dy_ref, rt_ref, dq_ref, dk_ref, dv_ref, dk_acc, dv_acc,
             a_buf, sig_buf, dw_buf, ns_buf, pre_buf, es_buf, dq_buf):
        qi = pl.program_id(2)

        @pl.when(qi == 0)
        def _():
            dk_acc[...] = jnp.zeros_like(dk_acc)
            dv_acc[...] = jnp.zeros_like(dv_acc)

        lane = lax.broadcasted_iota(jnp.int32, (blk, LANES), 1)
        first = lane < HEAD_DIM
        head_row = lax.broadcasted_iota(jnp.int32, (LANES, blk), 0) < HEAD_DIM
        q2 = q_ref[...] * 0.125
        q_rows = (jnp.where(first, q2, 0.0).astype(BF16), jnp.where(first, 0.0, q2).astype(BF16))
        q_t = q2.T
        q_heads = (jnp.where(head_row, q_t, 0.0).astype(BF16), jnp.where(head_row, 0.0, q_t).astype(BF16))
        dy2 = dy_ref[...].astype(F32)
        dy_rows = (jnp.where(first, dy2, 0.0).astype(BF16), jnp.where(first, 0.0, dy2).astype(BF16))
        dy_t = dy2.T
        dy_heads = (jnp.where(head_row, dy_t, 0.0).astype(BF16), jnp.where(head_row, 0.0, dy_t).astype(BF16))
        rt_t = rt_ref[...].T
        rt = (rt_t[0:1, :], rt_t[HEAD_DIM:HEAD_DIM + 1, :])
        rr = lax.broadcasted_iota(jnp.int32, (blk, blk), 0)
        cc = lax.broadcasted_iota(jnp.int32, (blk, blk), 1)
        lower = (cc < rr).astype(BF16)
        lower_eq = (cc <= rr).astype(BF16)
        tri_lt = jnp.concatenate([lower, lower], axis=1)
        tri_le = jnp.concatenate([lower_eq, lower_eq], axis=1)
        causal = rr < cc

        def tri_left(tri2, v):
            hi = v.astype(BF16)
            lo = (v - hi.astype(F32)).astype(BF16)
            return _dot(tri2, jnp.concatenate([hi, lo], axis=0))

        def scores(kb, slot, heads=(0, 1)):
            rows = pl.ds(pl.multiple_of(kb * blk, blk), blk)
            k_blk = k_ref[rows, :].astype(BF16)
            v_blk = v_ref[rows, :].astype(BF16)
            keep = jnp.logical_or(causal, kb < qi)
            for h in heads:
                z = jnp.where(keep, _dot(k_blk, q_heads[h]), -1e30)
                sp, t, den = _softplus_parts(z)
                neg = -sp
                a_buf[slot, h] = z - tri_left(tri_lt, neg)
                sig_buf[slot, h] = jnp.where(z >= 0, 1.0, t) / den
                ns_buf[slot, h] = jnp.sum(neg, axis=0, keepdims=True)
                dw_buf[slot, h] = _dot(v_blk, dy_heads[h])

        def finish(kb, slot, heads=(0, 1)):
            rows = pl.ds(pl.multiple_of(kb * blk, blk), blk)
            k_t = k_ref[rows, :].T.astype(BF16)
            for h in heads:
                pre, esum = pre_buf[h], es_buf[h]
                w = jnp.exp(a_buf[slot, h] + (rt[h] - pre))
                e = dw_buf[slot, h] * w
                dz = e - sig_buf[slot, h] * (esum + tri_left(tri_le, e))
                pre_buf[h] = pre + ns_buf[slot, h]
                es_buf[h] = esum + jnp.sum(e, axis=0, keepdims=True)
                dzb = dz.astype(BF16)
                dq_buf[h] += _dot(k_t, dzb)
                dk_acc[rows, :] += _dot(dzb, q_rows[h])
                dv_acc[rows, :] += _dot(w.astype(BF16), dy_rows[h])

        def step(kb_next, kb, slot):
            for h in range(2):
                scores(kb_next, 1 - slot, (h,))
                finish(kb, slot, (h,))

        pre_buf[...] = jnp.zeros_like(pre_buf)
        es_buf[...] = jnp.zeros_like(es_buf)
        dq_buf[...] = jnp.zeros_like(dq_buf)
        scores(0, 0)

        def two_steps(p, carry):
            t = 2 * p
            step(t + 1, t, 0)
            step(t + 2, t + 1, 1)
            return carry

        lax.fori_loop(0, qi // 2, two_steps, 0)

        @pl.when(qi % 2 == 1)
        def _():
            step(qi, qi - 1, 0)
            finish(qi, 1)

        @pl.when(qi % 2 == 0)
        def _():
            finish(qi, 0)

        dq_ref[...] = (jnp.where(head_row, dq_buf[0], dq_buf[1]).T * 0.125).astype(BF16)

        @pl.when(qi == nq - 1)
        def _():
            dk_ref[...] = dk_acc[...].astype(BF16)
            dv_ref[...] = dv_acc[...].astype(BF16)

    t = proj.shape[0]
    q_spec = pl.BlockSpec((blk, LANES), lambda b, p, i: (b * nq + i, p))
    kv_out = pl.BlockSpec((seq, LANES), lambda b, p, i: (b, p))
    out = jax.ShapeDtypeStruct((t, D_MODEL), BF16)
    return pl.pallas_call(
        body, name="attn_bwd", grid=(bsz, n_pairs, nq),
        in_specs=[q_spec,
                  pl.BlockSpec((seq, LANES), lambda b, p, i: (b, n_pairs + p)),
                  pl.BlockSpec((seq, LANES), lambda b, p, i: (b, 2 * n_pairs + p)),
                  q_spec, q_spec],
        out_specs=[q_spec, kv_out, kv_out], out_shape=[out, out, out],
        scratch_shapes=[pltpu.VMEM((seq, LANES), F32), pltpu.VMEM((seq, LANES), F32),
                        pltpu.VMEM((2, 2, blk, blk), F32), pltpu.VMEM((2, 2, blk, blk), F32),
                        pltpu.VMEM((2, 2, blk, blk), F32), pltpu.VMEM((2, 2, 1, blk), F32),
                        pltpu.VMEM((2, 1, blk), F32), pltpu.VMEM((2, 1, blk), F32),
                        pltpu.VMEM((2, LANES, blk), F32)],
        compiler_params=_params(("parallel", "parallel", "arbitrary")),
    )(proj, proj, proj, dy, rtot)


def _adamw(w, g, m, v, name, after):
    rows, cols = w.shape
    tr = _pick(rows, (256, 352, 128, 64, 32, 16, 8))
    c1 = 1.0 - ADAM_B1 ** ADAM_STEP
    c2 = 1.0 - ADAM_B2 ** ADAM_STEP

    def body(w_ref, g_ref, m_ref, v_ref, after_ref, d_ref, nm_ref, nv_ref):
        g_v = g_ref[...]
        nm = ADAM_B1 * m_ref[...] + (1.0 - ADAM_B1) * g_v
        nv = ADAM_B2 * v_ref[...] + (1.0 - ADAM_B2) * (g_v * g_v)
        nm_ref[...] = nm
        nv_ref[...] = nv
        d_ref[...] = -ADAM_LR * ((nm / c1) / (jnp.sqrt(nv / c2) + ADAM_EPS) + ADAM_WD * w_ref[...])

    spec = pl.BlockSpec((tr, cols), lambda i: (i, 0))
    shape = jax.ShapeDtypeStruct(w.shape, F32)
    return pl.pallas_call(
        body, name=name, grid=(rows // tr,), in_specs=[spec] * 4 + [pl.BlockSpec(memory_space=pl.ANY)],
        out_specs=[spec] * 3, out_shape=[shape] * 3,
        compiler_params=_params(("parallel",)),
    )(w, g, m, v, after)


def _ffn_fwd(u, w_gu, w_down, bsz, seq, tag):
    h, p = _ffn_up_act(u, w_gu, f"{tag}_up")
    f = _matmul(p, w_down, mode="nn", out_dtype=F32, name=f"{tag}_down")
    return h, p, f


def _ffn_bwd(df, u, h, p, w_gu, w_down, bsz, seq, tag):
    dh = _ffn_down_bwd_act(df, w_down, h, f"{tag}_ddown")
    g_down = _matmul(p, df, mode="tn", out_dtype=F32, name=f"{tag}_gdown")
    g_gu = _matmul(u, dh, mode="tn", out_dtype=F32, name=f"{tag}_ggu", out_stacked=True)
    du = _matmul(dh, w_gu, mode="nt", out_dtype=F32, name=f"{tag}_dup")
    return du, g_gu, g_down


def _local_step(x, c, target, wts, vecs, fetch=None, early_grads=None):
    wts = dict(wts)
    bsz, seq, _ = x.shape
    t = bsz * seq
    x0 = x.reshape(t, D_MODEL)
    tgt = target.reshape(t, D_MODEL)

    sc = _silu_pad(c)
    mod16 = _matmul(sc, wts["w_ada"], mode="nn", out_dtype=F32, name="ada_fwd", bias=vecs["b_ada"])
    mod = mod16[:bsz].reshape(bsz, 9, D_MODEL)

    u1 = _mod_in(x0, mod, bsz, seq, 0)
    h1, p1 = _ffn_up_act(u1, wts["ffn1_w_gu"], "ffn1_up")
    if fetch is not None:
        wts.update(fetch("down", p1))
    f1 = _matmul(p1, wts["ffn1_w_down"], mode="nn", out_dtype=F32, name="ffn1_down")
    r1, x1, u2 = _res_ln_fwd(x0, f1, mod, vecs["ln1_g"], vecs["ln1_b"], bsz, seq, 0, 0.5)
    if fetch is not None:
        wts.update(fetch("later", r1))

    proj = _matmul(u2, wts["w_in"], mode="nn", out_dtype=F32, name="mix_in")
    ya, rtot = _attn_fwd(proj, bsz, seq)
    cs, cv = _conv_fwd(proj, wts["conv_w"], vecs["conv_b"], vecs["conv_ln_g"], vecs["conv_ln_b"], bsz, seq)
    ysb = _matmul(ya, wts["w_sb_out"], mode="nn", out_dtype=F32, name="sb_out")
    yconv = _matmul(cs, wts["w_conv_out"], mode="nn", out_dtype=F32, name="conv_out")
    merged = _merge_fwd(proj, ysb, yconv, bsz, seq)
    o2 = _matmul(merged, wts["w_out"], mode="nn", out_dtype=F32, name="mix_out")
    r2, x2, u3 = _res_ln_fwd(x1, o2, mod, vecs["ln2_g"], vecs["ln2_b"], bsz, seq, 1, 1.0)

    h3, p3, f3 = _ffn_fwd(u3, wts["ffn2_w_gu"], wts["ffn2_w_down"], bsz, seq, "ffn2")
    r3, dy, loss_blk = _res_ln_fwd(x2, f3, mod, vecs["ln3_g"], vecs["ln3_b"], bsz, seq, 2, 0.5, target=tgt)

    grads = {}
    dxres, df, ln3s, g3s = _res_ln_bwd(r3, dy, f3, mod, vecs["ln3_g"], bsz, seq, 2, 0.5)
    du, grads["ffn2_w_gu"], grads["ffn2_w_down"] = _ffn_bwd(
        df, u3, h3, p3, wts["ffn2_w_gu"], wts["ffn2_w_down"], bsz, seq, "ffn2")
    dx2, m3s = _mod_bwd(dxres, du, x2, mod, bsz, seq, 2)

    dxres, do2, ln2s, g2s = _res_ln_bwd(r2, dx2, o2, mod, vecs["ln2_g"], bsz, seq, 1, 1.0)
    dmerged = _matmul(do2, wts["w_out"], mode="nt", out_dtype=F32, name="mix_out_d")
    grads["w_out"] = _matmul(merged, do2, mode="tn", out_dtype=F32, name="mix_out_g")
    dysb, dyconv, dgate = _merge_bwd(proj, ysb, yconv, dmerged, bsz, seq)
    dya = _matmul(dysb, wts["w_sb_out"], mode="nt", out_dtype=BF16, name="sb_out_d")
    grads["w_sb_out"] = _matmul(ya, dysb, mode="tn", out_dtype=F32, name="sb_out_g")
    dcs = _matmul(dyconv, wts["w_conv_out"], mode="nt", out_dtype=F32, name="conv_out_d")
    grads["w_conv_out"] = _matmul(cs, dyconv, mode="tn", out_dtype=F32, name="conv_out_g")
    dcv, convs = _conv_bwd_ln(dcs, cv, vecs["conv_ln_g"], vecs["conv_ln_b"], bsz, seq)
    dglu, g_conv_w = _conv_bwd_taps(proj, dcv, wts["conv_w"], bsz, seq)
    dq, dk, dv = _attn_bwd(proj, rtot, dya, bsz, seq)
    dproj = jnp.concatenate([dq, dk, dv, dglu, dgate], axis=1)
    grads["w_in"] = _matmul(u2, dproj, mode="tn", out_dtype=F32, name="mix_in_g", out_stacked=True)
    if early_grads is not None:
        mod = mod + early_grads("later_start", {n: grads.pop(n) for n in list(grads)})
    du = _matmul(dproj, wts["w_in"], mode="nt", out_dtype=F32, name="mix_in_d")
    dx1, m2s = _mod_bwd(dxres, du, x1, mod, bsz, seq, 1)
    if early_grads is not None:
        mod = mod + early_grads("later_go", dx1)

    dxres, df, ln1s, g1s = _res_ln_bwd(r1, dx1, f1, mod, vecs["ln1_g"], bsz, seq, 0, 0.5)
    dh = _ffn_down_bwd_act(df, wts["ffn1_w_down"], h1, "ffn1_ddown")
    grads["ffn1_w_down"] = _matmul(p1, df, mode="tn", out_dtype=F32, name="ffn1_gdown")
    grads["ffn1_w_gu"] = _matmul(u1, dh, mode="tn", out_dtype=F32, name="ffn1_ggu", out_stacked=True)
    if early_grads is not None:
        mod = mod + early_grads("mid_start", {n: grads.pop(n) for n in list(grads)})
    du = _matmul(dh, wts["ffn1_w_gu"], mode="nt", out_dtype=F32, name="ffn1_dup")
    if early_grads is not None:
        mod = mod + early_grads("mid_go", du)
    grad_x, m1s = _mod_bwd(dxres, du, x0, mod, bsz, seq, 0)

    dmod = jnp.stack([m1s[:, 0], m1s[:, 1], g1s[:, 0], m2s[:, 0], m2s[:, 1], g2s[:, 0],
                      m3s[:, 0], m3s[:, 1], g3s[:, 0]], axis=1)
    dmod16 = jnp.zeros((16, 9 * D_MODEL), F32).at[:bsz].set(dmod.reshape(bsz, 9 * D_MODEL))
    grads["w_ada"] = _matmul(sc, dmod16.astype(BF16), mode="tn", out_dtype=F32, name="ada_g", out_stacked=True)

    small = {"dmod": dmod, "ln1": ln1s, "ln2": ln2s, "ln3": ln3s, "conv": convs, "conv_w": g_conv_w,
             "loss": loss_blk}
    return grad_x.reshape(x.shape), grads, small


_HBM = pl.BlockSpec(memory_space=pltpu.HBM)


def _position():
    return lax.axis_index("x"), lax.axis_index("y"), lax.axis_index("c")


def _other_chips(x, y):
    return [(1 - x, y), (x, 1 - y), (1 - x, 1 - y)]


def _cast_into_stack(w_local, chip, name):
    rows, cols = w_local.shape
    tr = _pick(rows, (256, 352, 128, 64, 32, 16))

    def body(chip_ref, w_ref, o_ref):
        o_ref[...] = w_ref[...].astype(BF16)

    return pl.pallas_call(
        body, name=name,
        grid_spec=pltpu.PrefetchScalarGridSpec(
            num_scalar_prefetch=1, grid=(rows // tr,),
            in_specs=[pl.BlockSpec((tr, cols), lambda r, chip_ref: (r, 0))],
            out_specs=pl.BlockSpec((None, tr, cols), lambda r, chip_ref: (chip_ref[0], r, 0))),
        out_shape=jax.ShapeDtypeStruct((N_CHIPS, rows, cols), BF16),
        compiler_params=_params(("parallel",)),
    )(chip, w_local)


def _all_gather_weights(stacks, small):
    n = len(stacks)

    def body(*refs):
        ins, small_in, outs, small_out = refs[:n], refs[n], refs[n + 1:2 * n + 1], refs[2 * n + 1]
        send_sems, recv_sems, fwd_send_sems, fwd_recv_sems, small_sems = refs[2 * n + 2:]
        x, y, c = _position()
        me = 2 * x + y
        chips = _other_chips(x, y)

        def send(i, j):
            px, py = chips[j]
            return pltpu.make_async_remote_copy(
                src_ref=ins[i].at[me, c], dst_ref=outs[i].at[me, c], send_sem=send_sems.at[3 * i + j],
                recv_sem=recv_sems.at[3 * i + j], device_id=(px, py, c), device_id_type=MESH)

        def landed(i, j):
            px, py = chips[j]
            return pltpu.make_async_remote_copy(
                src_ref=ins[i].at[me, c], dst_ref=outs[i].at[2 * px + py, c], send_sem=send_sems.at[3 * i + j],
                recv_sem=recv_sems.at[3 * i + j], device_id=(px, py, c), device_id_type=MESH)

        def forward(i, j, half):
            px, py = chips[j]
            blk = outs[i].at[2 * px + py, half]
            return pltpu.make_async_remote_copy(
                src_ref=blk, dst_ref=blk, send_sem=fwd_send_sems.at[3 * i + j],
                recv_sem=fwd_recv_sems.at[3 * i + j], device_id=(x, y, 1 - c), device_id_type=MESH)

        def small_copy(j, slot):
            px, py = chips[j]
            return pltpu.make_async_remote_copy(
                src_ref=small_in, dst_ref=small_out.at[slot], send_sem=small_sems.at[j],
                recv_sem=small_sems.at[3 + j], device_id=(px, py, c), device_id_type=MESH)

        own_small = pltpu.make_async_copy(small_in, small_out.at[me], small_sems.at[6])
        own_small.start()
        for j in range(3):
            small_copy(j, me).start()
        for i in range(n):
            for j in range(3):
                send(i, j).start()
        for i in range(n):
            for j in range(3):
                landed(i, j).wait_recv()
                forward(i, j, c).start()
        for i in range(n):
            for j in range(3):
                forward(i, j, 1 - c).wait_recv()
        for j, (px, py) in enumerate(chips):
            small_copy(j, 2 * px + py).wait_recv()
        own_small.wait()
        for j in range(3):
            small_copy(j, me).wait_send()
        for i in range(n):
            for j in range(3):
                send(i, j).wait_send()
                forward(i, j, c).wait_send()

    return pl.pallas_call(
        body, name="all_gather_weights",
        out_shape=[jax.ShapeDtypeStruct(s.shape, s.dtype) for s in stacks]
        + [jax.ShapeDtypeStruct((N_CHIPS,) + small.shape, small.dtype)],
        in_specs=[_HBM] * (n + 1), out_specs=[_HBM] * (n + 1),
        input_output_aliases={i: i for i in range(n)},
        scratch_shapes=[pltpu.SemaphoreType.DMA((3 * n,)), pltpu.SemaphoreType.DMA((3 * n,)),
                        pltpu.SemaphoreType.DMA((3 * n,)), pltpu.SemaphoreType.DMA((3 * n,)),
                        pltpu.SemaphoreType.DMA((7,))],
    )(*stacks, small)


_SEM = pl.BlockSpec(memory_space=pltpu.SEMAPHORE)
_DATAFLOW = pltpu.SideEffectType.DATAFLOW_SIDE_EFFECTING


_COPIES = {"gather": 3, "scatter": 3, "swap": N_CHIPS}


def _exchange_plan(kind, src, land):
    x, y, c = _position()
    me = 2 * x + y
    if kind == "swap":
        return [(src.at[k, 1 - c], land.at[k], land.at[k], (x, y, 1 - c)) for k in range(N_CHIPS)]
    plan = []
    for j, (px, py) in enumerate(_other_chips(x, y)):
        if kind == "gather":
            plan.append((src.at[me, c], land.at[me, c], land.at[2 * px + py, c], (px, py, c)))
        else:
            plan.append((src.at[2 * px + py], land.at[j], land.at[j], (px, py, c)))
    return plan


def _exchange_start(kind, srcs, lands, name, after):
    n = len(srcs)
    per = _COPIES[kind]
    in_place = lands is None
    n_in = n if in_place else 2 * n

    def body(*refs):
        src_refs = refs[:n]
        land_refs = src_refs if in_place else refs[n:2 * n]
        send_sems, recv_sems = refs[n_in + 1], refs[n_in + 2]
        token = refs[-1]
        for i in range(n):
            for j, (src, dst, _, to) in enumerate(_exchange_plan(kind, src_refs[i], land_refs[i])):
                pltpu.make_async_remote_copy(
                    src_ref=src, dst_ref=dst, send_sem=send_sems.at[per * i + j], recv_sem=recv_sems.at[per * i + j],
                    device_id=to, device_id_type=MESH).start()
        token[...] = jnp.zeros_like(token)

    operands = list(srcs) + ([] if in_place else list(lands))
    operands = [pltpu.with_memory_space_constraint(o, pltpu.HBM) for o in operands]
    out = pl.pallas_call(
        body, name=name,
        out_shape=[pltpu.SemaphoreType.DMA((per * n,)), pltpu.SemaphoreType.DMA((per * n,))]
        + [pltpu.HBM(o.shape, o.dtype) for o in operands] + [jax.ShapeDtypeStruct((8, LANES), F32)],
        in_specs=[_HBM] * n_in + [pl.BlockSpec(memory_space=pl.ANY)],
        out_specs=[_SEM, _SEM] + [_HBM] * n_in + [pl.BlockSpec(memory_space=pltpu.VMEM)],
        input_output_aliases={i: 2 + i for i in range(n_in)},
        compiler_params=pltpu.CompilerParams(has_side_effects=_DATAFLOW),
    )(*operands, after)
    return out[0], out[1], list(out[2:2 + n_in]), out[-1]


def _exchange_wait(kind, send_sems, recv_sems, thru, in_place, after, name):
    n_in = len(thru)
    n = n_in if in_place else n_in // 2
    per = _COPIES[kind]

    def body(*refs):
        src_refs = refs[:n]
        land_refs = src_refs if in_place else refs[n:2 * n]
        send_sems, recv_sems = refs[n_in], refs[n_in + 1]
        for i in range(n):
            for j, (src, _, here, to) in enumerate(_exchange_plan(kind, src_refs[i], land_refs[i])):
                copy = pltpu.make_async_remote_copy(
                    src_ref=src, dst_ref=here, send_sem=send_sems.at[per * i + j], recv_sem=recv_sems.at[per * i + j],
                    device_id=to, device_id_type=MESH)
                copy.wait_send()
                copy.wait_recv()

    out = pl.pallas_call(
        body, name=name, out_shape=[pltpu.HBM(o.shape, o.dtype) for o in thru],
        in_specs=[_HBM] * n_in + [_SEM, _SEM, pl.BlockSpec(memory_space=pl.ANY)], out_specs=[_HBM] * n_in,
        input_output_aliases={i: i for i in range(n_in)},
        compiler_params=pltpu.CompilerParams(has_side_effects=_DATAFLOW),
    )(*thru, send_sems, recv_sems, after)
    return list(out[:n]), (list(out[:n]) if in_place else list(out[n:]))


def _gather_forward(stacks, name):
    n = len(stacks)

    def body(*refs):
        ins, outs = refs[:n], refs[n:2 * n]
        send_sems, recv_sems = refs[2 * n:]
        x, y, c = _position()
        chips = _other_chips(x, y)

        def copy(i, j, half):
            px, py = chips[j]
            return pltpu.make_async_remote_copy(
                src_ref=ins[i].at[2 * px + py, half], dst_ref=outs[i].at[2 * px + py, half],
                send_sem=send_sems.at[3 * i + j], recv_sem=recv_sems.at[3 * i + j],
                device_id=(x, y, 1 - c), device_id_type=MESH)

        for i in range(n):
            for j in range(3):
                copy(i, j, c).start()
        for i in range(n):
            for j in range(3):
                copy(i, j, 1 - c).wait_recv()
        for i in range(n):
            for j in range(3):
                copy(i, j, c).wait_send()

    return pl.pallas_call(
        body, name=name, out_shape=[jax.ShapeDtypeStruct(s.shape, s.dtype) for s in stacks],
        in_specs=[_HBM] * n, out_specs=[_HBM] * n, input_output_aliases={i: i for i in range(n)},
        scratch_shapes=[pltpu.SemaphoreType.DMA((3 * n,)), pltpu.SemaphoreType.DMA((3 * n,))],
    )(*stacks)


def _pair_add(g, got, place, name):
    _, _, rh, cols = g.shape
    tr = _pick(rh, (256, 176, 128, 64, 32, 16, 8))

    def body(place_ref, g_ref, got_ref, p_ref, own_ref):
        s = g_ref[...] + got_ref[...]
        p_ref[...] = s.astype(BF16)

        @pl.when(pl.program_id(1) == place_ref[1])
        def _():
            own_ref[...] = s

    blk = pl.BlockSpec((None, tr, cols), lambda r, k, place_ref: (k, r, 0))
    return pl.pallas_call(
        body, name=name,
        grid_spec=pltpu.PrefetchScalarGridSpec(
            num_scalar_prefetch=1, grid=(rh // tr, N_CHIPS),
            in_specs=[pl.BlockSpec((None, None, tr, cols), lambda r, k, place_ref: (k, place_ref[0], r, 0)), blk],
            out_specs=[blk, pl.BlockSpec((tr, cols), lambda r, k, place_ref: (r, 0))]),
        out_shape=[jax.ShapeDtypeStruct((N_CHIPS, rh, cols), BF16), jax.ShapeDtypeStruct((rh, cols), F32)],
        compiler_params=_params(("parallel", "arbitrary")),
    )(place, g, got)


def _chip_sum(own, parts, place, name):
    rh, cols = own.shape
    tr = _pick(rh, (256, 176, 128, 64, 32, 16, 8))

    def body(place_ref, own_ref, p_ref, o_ref):
        o_ref[...] = ((own_ref[...] + p_ref[0].astype(F32)) + p_ref[1].astype(F32)) + p_ref[2].astype(F32)

    return pl.pallas_call(
        body, name=name,
        grid_spec=pltpu.PrefetchScalarGridSpec(
            num_scalar_prefetch=1, grid=(rh // tr,),
            in_specs=[pl.BlockSpec((tr, cols), lambda r, place_ref: (r, 0)),
                      pl.BlockSpec((3, tr, cols), lambda r, place_ref: (0, r, 0))],
            out_specs=pl.BlockSpec((None, tr, cols), lambda r, place_ref: (place_ref[0], r, 0))),
        out_shape=jax.ShapeDtypeStruct((2, rh, cols), F32),
        compiler_params=_params(("parallel",)),
    )(place, own, parts)


def _pair_gather(halves, name):
    n = len(halves)

    def body(*refs):
        ins, outs = refs[:n], refs[n:2 * n]
        send_sems, recv_sems = refs[2 * n:]
        x, y, c = _position()

        def send(i):
            return pltpu.make_async_remote_copy(
                src_ref=ins[i].at[c], dst_ref=outs[i].at[c], send_sem=send_sems.at[i], recv_sem=recv_sems.at[i],
                device_id=(x, y, 1 - c), device_id_type=MESH)

        def landed(i):
            return pltpu.make_async_remote_copy(
                src_ref=ins[i].at[c], dst_ref=outs[i].at[1 - c], send_sem=send_sems.at[i], recv_sem=recv_sems.at[i],
                device_id=(x, y, 1 - c), device_id_type=MESH)

        for i in range(n):
            send(i).start()
        for i in range(n):
            landed(i).wait_recv()
        for i in range(n):
            send(i).wait_send()

    return pl.pallas_call(
        body, name=name,
        out_shape=[jax.ShapeDtypeStruct(h.shape, F32) for h in halves],
        in_specs=[_HBM] * n, out_specs=[_HBM] * n,
        input_output_aliases={i: i for i in range(n)},
        scratch_shapes=[pltpu.SemaphoreType.DMA((n,)), pltpu.SemaphoreType.DMA((n,))],
    )(*halves)


_MOD_ROWS = 16


def _small_all_reduce(buf, bsz, after):
    rows, cols = buf.shape
    head = bsz * _MOD_ROWS
    out_rows = rows - head + _MOD_ROWS

    def body(in_ref, after_ref, o_ref, gath, send_sems, recv_sems):
        x, y, c = _position()
        me = 4 * x + 2 * y + c

        def peer(mask):
            return (x ^ (mask >> 2), y ^ ((mask >> 1) & 1), c ^ (mask & 1))

        def copy(mask):
            return pltpu.make_async_remote_copy(
                src_ref=in_ref, dst_ref=gath.at[me], send_sem=send_sems.at[mask - 1],
                recv_sem=recv_sems.at[mask - 1], device_id=peer(mask), device_id_type=MESH)

        def arrival(mask):
            px, py, pc = peer(mask)
            return pltpu.make_async_remote_copy(
                src_ref=in_ref, dst_ref=gath.at[4 * px + 2 * py + pc], send_sem=send_sems.at[mask - 1],
                recv_sem=recv_sems.at[mask - 1], device_id=peer(mask), device_id_type=MESH)

        for mask in range(1, N_DEV):
            copy(mask).start()
        gath[me] = in_ref[...]
        for mask in range(1, N_DEV):
            arrival(mask).wait_recv()
        for mask in range(1, N_DEV):
            copy(mask).wait_send()
        acc = gath[0]
        for d in range(1, N_DEV):
            acc = acc + gath[d]
        mod = acc[0:_MOD_ROWS]
        for s in range(1, bsz):
            mod = mod + acc[s * _MOD_ROWS:(s + 1) * _MOD_ROWS]
        o_ref[0:_MOD_ROWS, :] = mod
        o_ref[_MOD_ROWS:, :] = acc[head:]

    vm = pl.BlockSpec(memory_space=pltpu.VMEM)
    return pl.pallas_call(
        body, name="small_all_reduce", in_specs=[vm, pl.BlockSpec(memory_space=pl.ANY)], out_specs=vm,
        out_shape=jax.ShapeDtypeStruct((out_rows, cols), F32),
        scratch_shapes=[pltpu.VMEM((N_DEV, rows, cols), F32), pltpu.SemaphoreType.DMA((N_DEV - 1,)),
                        pltpu.SemaphoreType.DMA((N_DEV - 1,))],
        compiler_params=pltpu.CompilerParams(vmem_limit_bytes=VMEM_LIMIT),
    )(buf, after)


_COL_SHARDED = ("w_ada", "ffn1_w_gu", "w_in", "ffn2_w_gu")
_ROW_SHARDED = ("ffn1_w_down", "w_sb_out", "w_conv_out", "w_out", "ffn2_w_down")
_NOW = ["w_ada", "ffn1_w_gu"]
_SOON = ["ffn1_w_down"]
_LATER = ["w_in", "w_sb_out", "w_conv_out", "w_out", "ffn2_w_gu", "ffn2_w_down"]
_VECS = ("b_ada", "ln1_g", "ln1_b", "conv_b", "conv_ln_g", "conv_ln_b", "ln2_g", "ln2_b", "ln3_g", "ln3_b")
_WEIGHTS = ("w_ada", "b_ada", "ffn1_w_gu", "ffn1_w_down", "ln1_g", "ln1_b", "w_in", "w_sb_out", "conv_w", "conv_b",
            "conv_ln_g", "conv_ln_b", "w_conv_out", "w_out", "ln2_g", "ln2_b", "ffn2_w_gu", "ffn2_w_down",
            "ln3_g", "ln3_b")


def _step(x, c, target, w, m, v):
    bsz = x.shape[0]
    chip = 2 * lax.axis_index("x") + lax.axis_index("y")
    core = lax.axis_index("c")

    chip_arr = jnp.reshape(chip, (1,)).astype(jnp.int32)
    place = jnp.stack([core, chip]).astype(jnp.int32)

    def stack_of(n):
        rows, cols = w[n].shape[1:]
        return _cast_into_stack(w[n][0], chip_arr, f"cast_{n}").reshape(N_CHIPS, 2, rows // 2, cols)

    def gathered_form(n, g):
        rows, cols = w[n].shape[1:]
        return g.reshape(N_CHIPS, rows, cols) if n in _COL_SHARDED else g.reshape(N_CHIPS * rows, cols)

    conv_w_local = jnp.pad(w["conv_w"][0], ((0, 1), (0, 0)))
    gathered = _all_gather_weights([stack_of(n) for n in _NOW], conv_w_local)
    wts = {n: gathered_form(n, g) for n, g in zip(_NOW, gathered[:-1])}
    wts["conv_w"] = gathered[-1].transpose(1, 0, 2).reshape(32, D_MODEL)
    pending, behind = {}, gathered[0]
    for stage, names in (("down", _SOON), ("later", _LATER)):
        send, recv, thru, token = _exchange_start(
            "gather", [stack_of(n) for n in names], None, f"gather_start_{stage}", behind)
        pending[stage] = (names, send, recv, thru)
        behind = token
    vecs = {n: w[n] for n in _VECS}
    vecs["b_ada"] = vecs["b_ada"] + behind[0, 0]

    def fetch(stage, after):
        names, send, recv, thru = pending[stage]
        landed, _ = _exchange_wait("gather", send, recv, thru, True, after, f"gather_wait_{stage}")
        forwarded = _gather_forward(landed, f"gather_forward_{stage}")
        return {n: gathered_form(n, g) for n, g in zip(names, forwarded)}

    groups = {"later": _LATER, "mid": ["ffn1_w_gu", "ffn1_w_down"], "last": ["w_ada"]}
    g_out, updates, state = {}, {}, {}

    def adam(names, after):
        for n in names:
            shape = w[n].shape
            flat = shape[1:] if len(shape) == 3 else shape
            d, nm, nv = _adamw(w[n].reshape(flat), g_out[n].reshape(flat), m[n].reshape(flat), v[n].reshape(flat),
                               f"adamw_{n}", after)
            updates[n] = (g_out[n].reshape(shape), d.reshape(shape), nm.reshape(shape), nv.reshape(shape))
            after = nv
        return after

    def swap_start(tag, grads):
        views = [grads[n].reshape(N_CHIPS, 2, w[n].shape[1] // 2, w[n].shape[2]) for n in groups[tag]]
        lands = [lax.empty((N_CHIPS,) + g.shape[2:], F32) for g in views]
        send, recv, thru, token = _exchange_start("swap", views, lands, f"swap_start_{tag}", place)
        state[tag] = {"swap": (send, recv, thru)}
        return token

    def scatter_start(tag, after):
        views, got = _exchange_wait("swap", *state[tag]["swap"], False, after, f"swap_wait_{tag}")
        sums = [_pair_add(g, r, place, f"pair_add_{n}") for n, g, r in zip(groups[tag], views, got)]
        lands = [lax.empty((3,) + p.shape[1:], BF16) for p, _ in sums]
        send, recv, thru, token = _exchange_start(
            "scatter", [p for p, _ in sums], lands, f"scatter_start_{tag}", place)
        state[tag].update(scatter=(send, recv, thru), sums=sums)
        return token

    def collect(tag, after):
        _, parts = _exchange_wait("scatter", *state[tag]["scatter"], False, after, f"scatter_wait_{tag}")
        halves = [_chip_sum(own, p, place, f"chip_sum_{n}")
                  for n, (_, own), p in zip(groups[tag], state[tag]["sums"], parts)]
        for n, f in zip(groups[tag], _pair_gather(halves, f"grad_pair_gather_{tag}")):
            g_out[n] = f.reshape(w[n].shape[1:])
        return g_out[groups[tag][-1]]

    def early_grads(stage, value):
        tag, step = stage.split("_")
        token = swap_start(tag, value) if step == "start" else scatter_start(tag, value)
        return token[0, 0]

    grad_x, grads, small = _local_step(x, c, target, wts, vecs, fetch, early_grads)

    token = swap_start("last", grads)
    done = collect("later", token)
    token = scatter_start("last", done)
    done = adam(groups["later"], token)
    done = collect("mid", done)
    done = adam(groups["mid"], done)
    done = collect("last", done)

    dmod = jnp.pad(small["dmod"], ((0, 0), (0, _MOD_ROWS - 9), (0, 0))).reshape(bsz * _MOD_ROWS, D_MODEL)
    loss_rows = jnp.pad(small["loss"], ((0, 0), (0, D_MODEL - LANES)))
    buf = jnp.concatenate([dmod, small["ln1"], small["ln2"], small["ln3"], small["conv"], small["conv_w"],
                           loss_rows], axis=0)
    red = _small_all_reduce(buf, bsz, done)
    o = _MOD_ROWS
    g_out["b_ada"] = red[0:9].reshape(1, 9 * D_MODEL)
    g_out["ln1_g"], g_out["ln1_b"] = red[o:o + 1], red[o + 1:o + 2]
    g_out["ln2_g"], g_out["ln2_b"] = red[o + 8:o + 9], red[o + 9:o + 10]
    g_out["ln3_g"], g_out["ln3_b"] = red[o + 16:o + 17], red[o + 17:o + 18]
    g_out["conv_ln_g"], g_out["conv_ln_b"], g_out["conv_b"] = red[o + 24:o + 25], red[o + 25:o + 26], red[o + 26:o + 27]
    cw = w["conv_w"].shape[2]
    g_out["conv_w"] = lax.dynamic_slice(red[o + 32:o + 32 + CONV_TAPS], (0, chip * cw), (CONV_TAPS, cw))
    loss = red[o + 64, 0]

    adam(groups["last"] + list(_VECS) + ["conv_w"], place)
    return (loss, grad_x, *[updates[n][k] for k in range(4) for n in _WEIGHTS])


def kernel(x, c, w_ada, b_ada, ffn1_w_gu, ffn1_w_down, ln1_g, ln1_b, w_in, w_sb_out, conv_w, conv_b, conv_ln_g, conv_ln_b, w_conv_out, w_out, ln2_g, ln2_b, ffn2_w_gu, ffn2_w_down, ln3_g, ln3_b, loss_target, m_w_ada, m_b_ada, m_ffn1_w_gu, m_ffn1_w_down, m_ln1_g, m_ln1_b, m_w_in, m_w_sb_out, m_conv_w, m_conv_b, m_conv_ln_g, m_conv_ln_b, m_w_conv_out, m_w_out, m_ln2_g, m_ln2_b, m_ffn2_w_gu, m_ffn2_w_down, m_ln3_g, m_ln3_b, v_w_ada, v_b_ada, v_ffn1_w_gu, v_ffn1_w_down, v_ln1_g, v_ln1_b, v_w_in, v_w_sb_out, v_conv_w, v_conv_b, v_conv_ln_g, v_conv_ln_b, v_w_conv_out, v_w_out, v_ln2_g, v_ln2_b, v_ffn2_w_gu, v_ffn2_w_down, v_ln3_g, v_ln3_b):
    given = dict(locals())
    w = {n: given[n] for n in _WEIGHTS}
    m = {n: given["m_" + n] for n in _WEIGHTS}
    v = {n: given["v_" + n] for n in _WEIGHTS}
    return _step(x, c, loss_target, w, m, v)
```

```python
import functools

import jax
import jax.numpy as jnp
from jax import lax
from jax.experimental import pallas as pl
from jax.experimental.pallas import tpu as pltpu

F32 = jnp.float32
BF16 = jnp.bfloat16

D_MODEL = 1024
D_FF = 2816
HEADS = 16
HEAD_DIM = 64
LANES = 128
CONV_TAPS = 31
HALO = 32
N_CHIPS = 4
N_DEV = 8
ALPHA = 2.0 ** 0.25
LN_EPS = 1e-5
ATT_BLOCK = 256
VMEM_LIMIT = 56 * 1024 * 1024

ADAM_LR = 0.001
ADAM_B1 = 0.9
ADAM_B2 = 0.999
ADAM_EPS = 1e-08
ADAM_WD = 0.01
ADAM_STEP = 10

MESH = pl.DeviceIdType.MESH


def _pick(n, cands):
    for t in cands:
        if t <= n and n % t == 0:
            return t
    return n


def _params(sem):
    return pltpu.CompilerParams(dimension_semantics=sem, vmem_limit_bytes=VMEM_LIMIT)


def _sigmoid(z):
    t = jnp.exp(-jnp.abs(z))
    return jnp.where(z >= 0, 1.0, t) / (1.0 + t)


def _silu(z):
    return z * _sigmoid(z)


def _dsilu(z):
    s = _sigmoid(z)
    return s * (1.0 + z * (1.0 - s))


def _ln_stats(r):
    mu = jnp.mean(r, axis=-1, keepdims=True)
    d = r - mu
    var = jnp.mean(d * d, axis=-1, keepdims=True)
    rstd = lax.rsqrt(var + LN_EPS)
    return d * rstd, rstd


def _colsum(v):
    return jnp.sum(v, axis=0, keepdims=True)


_DIMS = {"nn": (((1,), (0,)), ((), ())), "nt": (((1,), (1,)), ((), ())), "tn": (((0,), (0,)), ((), ()))}
_TN_CANDS = (1408, 1792, 1152, 1024, 512, 256, 128)
_TK_CANDS = (1024, 1408, 896, 512, 256, 128)


def _matmul(a, b, *, mode, out_dtype, name, bias=None, out_stacked=False):
    a_halves = mode == "nt" and a.ndim == 3
    b_halves = mode == "tn" and b.ndim == 3
    b_stacked = b.ndim == 3 and not b_halves
    if mode == "nn":
        m, k = a.shape
        n_c = b.shape[-1]
        n = n_c * (N_CHIPS if b_stacked else 1)
        k_c = k
    elif mode == "nt":
        m = a.shape[-2]
        k = a.shape[-1] * (2 if a_halves else 1)
        n = b.shape[-2]
        k_c = b.shape[-1]
        n_c = n
    else:
        k, m = a.shape
        n = b.shape[-1] * (2 if b_halves else 1)
        n_c = n // N_CHIPS if out_stacked else n
        k_c = k
    if mode == "tn":
        tm = _pick(m, (1024, 1408, 512, 256, 128))
        tk = _pick(k, (512, 256, 128, 64, 32, 16))
    else:
        tm = _pick(m, (1024, 512, 256, 128, 64, 32, 16))
        tk = _pick(k_c, _TK_CANDS)
    tn = _pick(n_c, _TN_CANDS)
    nb = n_c // tn
    kb = k_c // tk
    nk = k // tk
    grid = (m // tm, n // tn, nk)

    if mode == "nn":
        a_spec = pl.BlockSpec((tm, tk), lambda i, j, kk: (i, kk))
        if b_stacked:
            b_spec = pl.BlockSpec((None, tk, tn), lambda i, j, kk: (j // nb, kk, j % nb))
        else:
            b_spec = pl.BlockSpec((tk, tn), lambda i, j, kk: (kk, j))
    elif mode == "nt":
        if a_halves:
            ka = a.shape[-1] // tk
            a_spec = pl.BlockSpec((None, tm, tk), lambda i, j, kk: (kk // ka, i, kk % ka))
        else:
            a_spec = pl.BlockSpec((tm, tk), lambda i, j, kk: (i, kk))
        if b_stacked:
            b_spec = pl.BlockSpec((None, tn, tk), lambda i, j, kk: (kk // kb, j, kk % kb))
        else:
            b_spec = pl.BlockSpec((tn, tk), lambda i, j, kk: (j, kk))
    else:
        a_spec = pl.BlockSpec((tk, tm), lambda i, j, kk: (kk, i))
        if b_halves:
            nh = b.shape[-1] // tn
            b_spec = pl.BlockSpec((None, tk, tn), lambda i, j, kk: (j // nh, kk, j % nh))
        else:
            b_spec = pl.BlockSpec((tk, tn), lambda i, j, kk: (kk, j))
    if out_stacked:
        out_shape = jax.ShapeDtypeStruct((N_CHIPS, m, n_c), out_dtype)
        o_spec = pl.BlockSpec((None, tm, tn), lambda i, j, kk: (j // nb, i, j % nb))
    else:
        out_shape = jax.ShapeDtypeStruct((m, n), out_dtype)
        o_spec = pl.BlockSpec((tm, tn), lambda i, j, kk: (i, j))
    in_specs = [a_spec, b_spec]
    args = [a, b]
    if bias is not None:
        in_specs.append(pl.BlockSpec((1, tn), lambda i, j, kk: (0, j)))
        args.append(bias)
    dims = _DIMS[mode]

    def body(*refs):
        a_ref, b_ref = refs[0], refs[1]
        bias_ref = refs[2] if bias is not None else None
        part = lax.dot_general(a_ref[...], b_ref[...], dims, preferred_element_type=F32)

        def write(r):
            if bias_ref is not None:
                r = r + bias_ref[...]
            o_ref[...] = r.astype(o_ref.dtype)

        if nk == 1:
            o_ref = refs[-1]
            write(part)
            return
        o_ref, acc_ref = refs[-2], refs[-1]
        kk = pl.program_id(2)

        @pl.when(kk == 0)
        def _():
            acc_ref[...] = part

        @pl.when(kk > 0)
        def _():
            acc_ref[...] += part

        @pl.when(kk == nk - 1)
        def _():
            write(acc_ref[...])

    return pl.pallas_call(
        body, name=name, grid=grid, in_specs=in_specs, out_specs=o_spec, out_shape=out_shape,
        scratch_shapes=[pltpu.VMEM((tm, tn), F32)] if nk > 1 else [],
        compiler_params=_params(("parallel", "parallel", "arbitrary")),
    )(*args)


def _row_grid(bsz, seq, ts):
    ns = seq // ts
    return (bsz, ns), ns


def _rows(ts, width, ns, col=0):
    return pl.BlockSpec((ts, width), lambda b, s: (b * ns + s, col))


def _mod_spec():
    return pl.BlockSpec((None, 9, D_MODEL), lambda b, s: (b, 0, 0))


def _vec_spec(rows=1, width=D_MODEL):
    return pl.BlockSpec((rows, width), lambda b, s: (0, 0))


def _silu_pad(c):
    bsz = c.shape[0]

    def body(c_ref, o_ref):
        o_ref[...] = jnp.zeros_like(o_ref)
        o_ref[0:bsz, :] = _silu(c_ref[...]).astype(BF16)

    return pl.pallas_call(body, name="silu_pad", out_shape=jax.ShapeDtypeStruct((16, D_MODEL), BF16))(c)


def _mod_in(x, mod, bsz, seq, sub):
    ts = _pick(seq, (512, 256, 128))
    grid, ns = _row_grid(bsz, seq, ts)

    def body(x_ref, mod_ref, u_ref):
        sh = mod_ref[3 * sub:3 * sub + 1, :]
        sc = mod_ref[3 * sub + 1:3 * sub + 2, :]
        u_ref[...] = (x_ref[...] * (1.0 + sc) + sh).astype(BF16)

    return pl.pallas_call(
        body, name=f"mod_in{sub}", grid=grid, in_specs=[_rows(ts, D_MODEL, ns), _mod_spec()],
        out_specs=_rows(ts, D_MODEL, ns), out_shape=jax.ShapeDtypeStruct(x.shape, BF16),
        compiler_params=_params(("parallel", "parallel")),
    )(x, mod)


_FFN_TN = D_FF // 2


def _ffn_up_act(u, w_gu, name):
    t = u.shape[0]
    tm = _pick(t, (512, 256, 128))
    tn = _FFN_TN

    def body(u_ref, wa_ref, wg_ref, h_ref, p_ref):
        u_v = u_ref[...]
        a = jnp.dot(u_v, wa_ref[...], preferred_element_type=F32)
        g = jnp.dot(u_v, wg_ref[...], preferred_element_type=F32)
        h_ref[0] = a.astype(BF16)
        h_ref[1] = g.astype(BF16)
        p_ref[...] = (_silu(a) * g).astype(BF16)

    return pl.pallas_call(
        body, name=name, grid=(2, t // tm),
        in_specs=[pl.BlockSpec((tm, D_MODEL), lambda j, i: (i, 0)),
                  pl.BlockSpec((None, D_MODEL, tn), lambda j, i: (j, 0, 0)),
                  pl.BlockSpec((None, D_MODEL, tn), lambda j, i: (j + 2, 0, 0))],
        out_specs=[pl.BlockSpec((2, tm, tn), lambda j, i: (0, i, j)),
                   pl.BlockSpec((tm, tn), lambda j, i: (i, j))],
        out_shape=[jax.ShapeDtypeStruct((2, t, D_FF), BF16), jax.ShapeDtypeStruct((t, D_FF), BF16)],
        compiler_params=_params(("parallel", "parallel")),
    )(u, w_gu, w_gu)


def _ffn_down_bwd_act(df, w_down, h, name):
    t = df.shape[0]
    tm = _pick(t, (512, 256, 128))
    tn = _FFN_TN

    def body(df_ref, wd_ref, h_ref, dh_ref):
        dp = lax.dot_general(df_ref[...], wd_ref[...], _DIMS["nt"], preferred_element_type=F32)
        a = h_ref[0].astype(F32)
        g = h_ref[1].astype(F32)
        dh_ref[0] = (dp * g * _dsilu(a)).astype(BF16)
        dh_ref[1] = (dp * _silu(a)).astype(BF16)

    blk = pl.BlockSpec((2, tm, tn), lambda j, i: (0, i, j))
    return pl.pallas_call(
        body, name=name, grid=(2, t // tm),
        in_specs=[pl.BlockSpec((tm, D_MODEL), lambda j, i: (i, 0)),
                  pl.BlockSpec((tn, D_MODEL), lambda j, i: (j, 0)), blk],
        out_specs=blk, out_shape=jax.ShapeDtypeStruct((2, t, D_FF), BF16),
        compiler_params=_params(("parallel", "parallel")),
    )(df, w_down, h)


def _res_ln_fwd(x, f, mod, ln_g, ln_b, bsz, seq, sub, weight, target=None):
    ts = _pick(seq, (256, 128))
    grid, ns = _row_grid(bsz, seq, ts)
    last = target is not None

    def body(*refs):
        x_ref, f_ref, mod_ref, g_ref, b_ref = refs[:5]
        gate = mod_ref[3 * sub + 2:3 * sub + 3, :]
        r = ALPHA * x_ref[...] + gate * (weight * f_ref[...])
        xhat, _ = _ln_stats(r)
        xo = xhat * g_ref[...] + b_ref[...]
        if last:
            t_ref, r_ref, dy_ref, loss_ref = refs[5:]
            diff = xo - t_ref[...]
            dy_ref[...] = diff * (1.0 / D_MODEL)
            part = 0.5 * jnp.sum(jnp.mean(diff * diff, axis=-1, keepdims=True), axis=0, keepdims=True)

            @pl.when((pl.program_id(0) == 0) & (pl.program_id(1) == 0))
            def _():
                loss_ref[...] = jnp.zeros_like(loss_ref)

            loss_ref[...] += jnp.broadcast_to(part, loss_ref.shape)
        else:
            r_ref, xo_ref, u_ref = refs[5:]
            xo_ref[...] = xo
            sh = mod_ref[3 * sub + 3:3 * sub + 4, :]
            sc = mod_ref[3 * sub + 4:3 * sub + 5, :]
            u_ref[...] = (xo * (1.0 + sc) + sh).astype(BF16)
        r_ref[...] = r

    row = _rows(ts, D_MODEL, ns)
    in_specs = [row, row, _mod_spec(), _vec_spec(), _vec_spec()]
    args = [x, f, mod, ln_g, ln_b]
    if last:
        in_specs.append(row)
        args.append(target)
        out_specs = [row, row, _vec_spec(8, LANES)]
        out_shape = [jax.ShapeDtypeStruct(x.shape, F32), jax.ShapeDtypeStruct(x.shape, F32),
                     jax.ShapeDtypeStruct((8, LANES), F32)]
        sem = ("arbitrary", "arbitrary")
    else:
        out_specs = [row, row, row]
        out_shape = [jax.ShapeDtypeStruct(x.shape, F32), jax.ShapeDtypeStruct(x.shape, F32),
                     jax.ShapeDtypeStruct(x.shape, BF16)]
        sem = ("parallel", "parallel")
    return pl.pallas_call(
        body, name=f"res_ln_fwd{sub}", grid=grid, in_specs=in_specs, out_specs=out_specs, out_shape=out_shape,
        compiler_params=_params(sem),
    )(*args)


def _res_ln_bwd(r, dxo, f, mod, ln_g, bsz, seq, sub, weight):
    ts = _pick(seq, (256, 128))
    grid, ns = _row_grid(bsz, seq, ts)

    def body(r_ref, dxo_ref, f_ref, mod_ref, g_ref, dxres_ref, df_ref, lns_ref, gs_ref):
        b, s = pl.program_id(0), pl.program_id(1)
        gate = mod_ref[3 * sub + 2:3 * sub + 3, :]
        xhat, rstd = _ln_stats(r_ref[...])
        dxo_v = dxo_ref[...]
        dxhat = dxo_v * g_ref[...]
        m1 = jnp.mean(dxhat, axis=-1, keepdims=True)
        m2 = jnp.mean(dxhat * xhat, axis=-1, keepdims=True)
        dr = rstd * (dxhat - m1 - xhat * m2)
        dxres_ref[...] = ALPHA * dr
        df_ref[...] = (dr * (gate * weight)).astype(BF16)

        @pl.when((b == 0) & (s == 0))
        def _():
            lns_ref[...] = jnp.zeros_like(lns_ref)

        @pl.when(s == 0)
        def _():
            gs_ref[...] = jnp.zeros_like(gs_ref)

        lns_ref[0:1, :] += _colsum(dxo_v * xhat)
        lns_ref[1:2, :] += _colsum(dxo_v)
        gs_ref[0:1, :] += _colsum(dr * (weight * f_ref[...]))

    row = _rows(ts, D_MODEL, ns)
    return pl.pallas_call(
        body, name=f"res_ln_bwd{sub}", grid=grid,
        in_specs=[row, row, row, _mod_spec(), _vec_spec()],
        out_specs=[row, row, _vec_spec(8), pl.BlockSpec((None, 8, D_MODEL), lambda b, s: (b, 0, 0))],
        out_shape=[jax.ShapeDtypeStruct(r.shape, F32), jax.ShapeDtypeStruct(r.shape, BF16),
                   jax.ShapeDtypeStruct((8, D_MODEL), F32), jax.ShapeDtypeStruct((bsz, 8, D_MODEL), F32)],
        compiler_params=_params(("arbitrary", "arbitrary")),
    )(r, dxo, f, mod, ln_g)


def _mod_bwd(dxres, du, x, mod, bsz, seq, sub):
    ts = _pick(seq, (256, 128))
    grid, ns = _row_grid(bsz, seq, ts)

    def body(dxres_ref, du_ref, x_ref, mod_ref, dx_ref, st_ref):
        s = pl.program_id(1)
        sc = mod_ref[3 * sub + 1:3 * sub + 2, :]
        du_v = du_ref[...]
        dx_ref[...] = dxres_ref[...] + du_v * (1.0 + sc)

        @pl.when(s == 0)
        def _():
            st_ref[...] = jnp.zeros_like(st_ref)

        st_ref[0:1, :] += _colsum(du_v)
        st_ref[1:2, :] += _colsum(du_v * x_ref[...])

    row = _rows(ts, D_MODEL, ns)
    return pl.pallas_call(
        body, name=f"mod_bwd{sub}", grid=grid, in_specs=[row, row, row, _mod_spec()],
        out_specs=[row, pl.BlockSpec((None, 8, D_MODEL), lambda b, s: (b, 0, 0))],
        out_shape=[jax.ShapeDtypeStruct(x.shape, F32), jax.ShapeDtypeStruct((bsz, 8, D_MODEL), F32)],
        compiler_params=_params(("parallel", "arbitrary")),
    )(dxres, du, x, mod)


_COL_GLU_A, _COL_GLU_B, _COL_GATE_A, _COL_GATE_B = 3, 4, 5, 6


def _merge_fwd(proj, ya, cs, w_sb, w_co):
    t = ya.shape[0]
    tm = _pick(t, (512, 256, 128))

    def body(ga_ref, gb_ref, ya_ref, cs_ref, wsb_ref, wco_ref, m_ref, ysb_ref, yc_ref):
        ysb = jnp.dot(ya_ref[...], wsb_ref[...], preferred_element_type=F32)
        yc = jnp.dot(cs_ref[...], wco_ref[...], preferred_element_type=F32)
        ysb_ref[...] = ysb
        yc_ref[...] = yc
        m_ref[...] = (_sigmoid(ga_ref[...]) * ysb + _sigmoid(gb_ref[...]) * yc).astype(BF16)

    row = pl.BlockSpec((tm, D_MODEL), lambda i: (i, 0))
    full = pl.BlockSpec((D_MODEL, D_MODEL), lambda i: (0, 0))
    return pl.pallas_call(
        body, name="merge_fwd", grid=(t // tm,),
        in_specs=[pl.BlockSpec((tm, D_MODEL), lambda i: (i, _COL_GATE_A)),
                  pl.BlockSpec((tm, D_MODEL), lambda i: (i, _COL_GATE_B)), row, row, full, full],
        out_specs=[row, row, row],
        out_shape=[jax.ShapeDtypeStruct((t, D_MODEL), BF16), jax.ShapeDtypeStruct((t, D_MODEL), F32),
                   jax.ShapeDtypeStruct((t, D_MODEL), F32)],
        compiler_params=_params(("parallel",)),
    )(proj, proj, ya, cs, w_sb, w_co)


def _merge_bwd(proj, ysb, yconv, do2, w_out):
    t = ysb.shape[0]
    tm = _pick(t, (512, 256, 128))

    def body(ga_ref, gb_ref, ysb_ref, yc_ref, do_ref, w_ref, dysb_ref, dyc_ref, dg_ref):
        dm = lax.dot_general(do_ref[...], w_ref[...], _DIMS["nt"], preferred_element_type=F32)
        sa = _sigmoid(ga_ref[...])
        sb = _sigmoid(gb_ref[...])
        dysb_ref[...] = (dm * sa).astype(BF16)
        dyc_ref[...] = (dm * sb).astype(BF16)
        dg_ref[:, :D_MODEL] = (dm * ysb_ref[...] * sa * (1.0 - sa)).astype(BF16)
        dg_ref[:, D_MODEL:] = (dm * yc_ref[...] * sb * (1.0 - sb)).astype(BF16)

    row = pl.BlockSpec((tm, D_MODEL), lambda i: (i, 0))
    return pl.pallas_call(
        body, name="merge_bwd", grid=(t // tm,),
        in_specs=[pl.BlockSpec((tm, D_MODEL), lambda i: (i, _COL_GATE_A)),
                  pl.BlockSpec((tm, D_MODEL), lambda i: (i, _COL_GATE_B)), row, row, row,
                  pl.BlockSpec((D_MODEL, D_MODEL), lambda i: (0, 0))],
        out_specs=[row, row, pl.BlockSpec((tm, 2 * D_MODEL), lambda i: (i, 0))],
        out_shape=[jax.ShapeDtypeStruct((t, D_MODEL), BF16), jax.ShapeDtypeStruct((t, D_MODEL), BF16),
                   jax.ShapeDtypeStruct((t, 2 * D_MODEL), BF16)],
        compiler_params=_params(("parallel",)),
    )(proj, proj, ysb, yconv, do2, w_out)


_CONV_ROWS = 128


def _halo_prev(tt, ns, col):
    r = tt // HALO
    return pl.BlockSpec((HALO, D_MODEL), lambda b, s: (jnp.maximum((b * ns + s) * r - 1, 0), col))


def _halo_next(tt, ns, nblk, col):
    r = tt // HALO
    return pl.BlockSpec((HALO, D_MODEL), lambda b, s: (jnp.minimum((b * ns + s + 1) * r, nblk - 1), col))


def _windows(pad_ref, stage_ref, tt, offsets):
    for r in range(8):
        mine = [o for o in offsets if o % 8 == r]
        if not mine:
            continue
        n = max(mine) - r + tt
        stage_ref[0:n, :] = pad_ref[r:r + n, :]
        for o in mine:
            yield o, stage_ref[o - r:o - r + tt, :]


def _fill_hc(hpad, a_ref, b_ref, ha_ref, hb_ref, s):
    halo = ha_ref[...] * _sigmoid(hb_ref[...])
    hpad[0:HALO, :] = jnp.where(s > 0, halo, 0.0)
    hpad[HALO:, :] = a_ref[...] * _sigmoid(b_ref[...])


def _conv_fwd(proj, conv_w, conv_b, ln_g, ln_b, bsz, seq):
    tt = _CONV_ROWS
    grid, ns = _row_grid(bsz, seq, tt)
    off = HALO - (CONV_TAPS - 1)

    def body(a_ref, b_ref, ha_ref, hb_ref, w_ref, cb_ref, g_ref, bb_ref, cs_ref, cv_ref, hpad, stage):
        _fill_hc(hpad, a_ref, b_ref, ha_ref, hb_ref, pl.program_id(1))
        acc = jnp.zeros((tt, D_MODEL), F32)
        for o, win in _windows(hpad, stage, tt, [off + j for j in range(CONV_TAPS)]):
            acc = acc + w_ref[o - off:o - off + 1, :] * win
        cv = acc + cb_ref[...]
        cv_ref[...] = cv
        xhat, _ = _ln_stats(cv)
        cs_ref[...] = _silu(xhat * g_ref[...] + bb_ref[...]).astype(BF16)

    row = _rows(tt, D_MODEL, ns)
    t = proj.shape[0]
    return pl.pallas_call(
        body, name="conv_fwd", grid=grid,
        in_specs=[_rows(tt, D_MODEL, ns, _COL_GLU_A), _rows(tt, D_MODEL, ns, _COL_GLU_B),
                  _halo_prev(tt, ns, _COL_GLU_A), _halo_prev(tt, ns, _COL_GLU_B),
                  _vec_spec(32), _vec_spec(), _vec_spec(), _vec_spec()],
        out_specs=[row, row],
        out_shape=[jax.ShapeDtypeStruct((t, D_MODEL), BF16), jax.ShapeDtypeStruct((t, D_MODEL), F32)],
        scratch_shapes=[pltpu.VMEM((HALO + tt, D_MODEL), F32), pltpu.VMEM((HALO + tt, D_MODEL), F32)],
        compiler_params=_params(("parallel", "parallel")),
    )(proj, proj, proj, proj, conv_w, conv_b, ln_g, ln_b)


def _conv_bwd_ln(dcs, cv, ln_g, ln_b, bsz, seq):
    ts = _pick(seq, (256, 128))
    grid, ns = _row_grid(bsz, seq, ts)

    def body(dcs_ref, cv_ref, g_ref, b_ref, dcv_ref, st_ref):
        xhat, rstd = _ln_stats(cv_ref[...])
        cl = xhat * g_ref[...] + b_ref[...]
        dcl = dcs_ref[...] * _dsilu(cl)
        dxhat = dcl * g_ref[...]
        m1 = jnp.mean(dxhat, axis=-1, keepdims=True)
        m2 = jnp.mean(dxhat * xhat, axis=-1, keepdims=True)
        dcv = rstd * (dxhat - m1 - xhat * m2)
        dcv_ref[...] = dcv

        @pl.when((pl.program_id(0) == 0) & (pl.program_id(1) == 0))
        def _():
            st_ref[...] = jnp.zeros_like(st_ref)

        st_ref[0:1, :] += _colsum(dcl * xhat)
        st_ref[1:2, :] += _colsum(dcl)
        st_ref[2:3, :] += _colsum(dcv)

    row = _rows(ts, D_MODEL, ns)
    return pl.pallas_call(
        body, name="conv_bwd_ln", grid=grid, in_specs=[row, row, _vec_spec(), _vec_spec()],
        out_specs=[row, _vec_spec(8)],
        out_shape=[jax.ShapeDtypeStruct(cv.shape, F32), jax.ShapeDtypeStruct((8, D_MODEL), F32)],
        compiler_params=_params(("arbitrary", "arbitrary")),
    )(dcs, cv, ln_g, ln_b)


def _conv_bwd_taps(proj, dcv, conv_w, bsz, seq):
    tt = _CONV_ROWS
    grid, ns = _row_grid(bsz, seq, tt)
    off = HALO - (CONV_TAPS - 1)
    nblk = proj.shape[0] // HALO

    def body(a_ref, b_ref, ha_ref, hb_ref, d_ref, dn_ref, w_ref, dglu_ref, dw_ref, hpad, dpad, stage):
        s = pl.program_id(1)
        _fill_hc(hpad, a_ref, b_ref, ha_ref, hb_ref, s)
        dcv = d_ref[...]
        dpad[0:tt, :] = dcv
        dpad[tt:, :] = jnp.where(s < ns - 1, dn_ref[...], 0.0)

        @pl.when((pl.program_id(0) == 0) & (s == 0))
        def _():
            dw_ref[...] = jnp.zeros_like(dw_ref)

        dhc = jnp.zeros((tt, D_MODEL), F32)
        for o, win in _windows(dpad, stage, tt, list(range(CONV_TAPS))):
            j = CONV_TAPS - 1 - o
            dhc = dhc + w_ref[j:j + 1, :] * win
        for o, win in _windows(hpad, stage, tt, [off + j for j in range(CONV_TAPS)]):
            dw_ref[o - off:o - off + 1, :] += _colsum(dcv * win)
        sb = _sigmoid(b_ref[...])
        dglu_ref[:, :D_MODEL] = (dhc * sb).astype(BF16)
        dglu_ref[:, D_MODEL:] = (dhc * a_ref[...] * sb * (1.0 - sb)).astype(BF16)

    t = proj.shape[0]
    return pl.pallas_call(
        body, name="conv_bwd_taps", grid=grid,
        in_specs=[_rows(tt, D_MODEL, ns, _COL_GLU_A), _rows(tt, D_MODEL, ns, _COL_GLU_B),
                  _halo_prev(tt, ns, _COL_GLU_A), _halo_prev(tt, ns, _COL_GLU_B),
                  _rows(tt, D_MODEL, ns), _halo_next(tt, ns, nblk, 0), _vec_spec(32)],
        out_specs=[_rows(tt, 2 * D_MODEL, ns), _vec_spec(32)],
        out_shape=[jax.ShapeDtypeStruct((t, 2 * D_MODEL), BF16), jax.ShapeDtypeStruct((32, D_MODEL), F32)],
        scratch_shapes=[pltpu.VMEM((HALO + tt, D_MODEL), F32), pltpu.VMEM((tt + HALO, D_MODEL), F32),
                        pltpu.VMEM((HALO + tt, D_MODEL), F32)],
        compiler_params=_params(("arbitrary", "arbitrary")),
    )(proj, proj, proj, proj, dcv, dcv, conv_w)


_NT = (((1,), (1,)), ((), ()))
_TN = (((0,), (0,)), ((), ()))


def _dot(a, b, dims=None):
    if dims is None:
        return jnp.dot(a, b, preferred_element_type=F32)
    return lax.dot_general(a, b, dims, preferred_element_type=F32)


def _tri_dot(v, tri2):
    hi = v.astype(BF16)
    lo = (v - hi.astype(F32)).astype(BF16)
    return _dot(jnp.concatenate([hi, lo], axis=1), tri2)


def _tri2(mask):
    t = mask.astype(BF16)
    return jnp.concatenate([t, t], axis=0)


def _softplus_parts(z):
    t = jnp.exp(-jnp.abs(z))
    den = 1.0 + t
    return jnp.maximum(z, 0.0) + jnp.log(den), t, den


def _attn_fwd(proj, bsz, seq):
    blk = ATT_BLOCK
    nq = seq // blk
    n_pairs = D_MODEL // LANES

    def body(q_ref, k_ref, v_ref, y_ref, rt_ref, zr_buf, ns_buf, run_buf, acc_buf):
        qi = pl.program_id(2)
        lane = lax.broadcasted_iota(jnp.int32, (blk, LANES), 1)
        first = lane < HEAD_DIM
        q2 = q_ref[...] * 0.125
        q_heads = (jnp.where(first, q2, 0.0).astype(BF16), jnp.where(first, 0.0, q2).astype(BF16))
        rr = lax.broadcasted_iota(jnp.int32, (blk, blk), 0)
        cc = lax.broadcasted_iota(jnp.int32, (blk, blk), 1)
        tri_ge = _tri2(rr >= cc)
        causal = cc < rr

        def scores(kb, slot, masked, heads=(0, 1)):
            k_blk = k_ref[pl.ds(pl.multiple_of(kb * blk, blk), blk), :].astype(BF16)
            for h in heads:
                z = _dot(q_heads[h], k_blk, _NT)
                if masked:
                    z = jnp.where(causal, z, -1e30)
                sp, _, _ = _softplus_parts(z)
                neg = -sp
                zr_buf[slot, h] = z + _tri_dot(neg, tri_ge)
                ns_buf[slot, h] = jnp.sum(neg, axis=1, keepdims=True)

        def weigh(kb, slot, heads=(0, 1)):
            v_blk = v_ref[pl.ds(pl.multiple_of(kb * blk, blk), blk), :].astype(BF16)
            for h in heads:
                run = run_buf[h]
                w = jnp.exp(zr_buf[slot, h] + run)
                acc_buf[h] += _dot(w.astype(BF16), v_blk)
                run_buf[h] = run + ns_buf[slot, h]

        def step(kb_next, kb, slot):
            for h in range(2):
                scores(kb_next, 1 - slot, False, (h,))
                weigh(kb, slot, (h,))

        run_buf[...] = jnp.zeros_like(run_buf)
        acc_buf[...] = jnp.zeros_like(acc_buf)
        scores(qi, 0, True)

        def two_steps(p, carry):
            t = 2 * p
            step(qi - t - 1, qi - t, 0)
            step(qi - t - 2, qi - t - 1, 1)
            return carry

        lax.fori_loop(0, qi // 2, two_steps, 0)

        @pl.when(qi % 2 == 1)
        def _():
            step(0, 1, 0)
            weigh(0, 1)

        @pl.when(qi % 2 == 0)
        def _():
            weigh(0, 0)

        y_ref[...] = jnp.where(first, acc_buf[0], acc_buf[1]).astype(BF16)
        rt_ref[...] = jnp.where(first, jnp.broadcast_to(run_buf[0], (blk, LANES)),
                                jnp.broadcast_to(run_buf[1], (blk, LANES)))

    t = proj.shape[0]
    q_spec = pl.BlockSpec((blk, LANES), lambda b, p, i: (b * nq + i, p))
    return pl.pallas_call(
        body, name="attn_fwd", grid=(bsz, n_pairs, nq),
        in_specs=[q_spec,
                  pl.BlockSpec((seq, LANES), lambda b, p, i: (b, n_pairs + p)),
                  pl.BlockSpec((seq, LANES), lambda b, p, i: (b, 2 * n_pairs + p))],
        out_specs=[q_spec, q_spec],
        out_shape=[jax.ShapeDtypeStruct((t, D_MODEL), BF16), jax.ShapeDtypeStruct((t, D_MODEL), F32)],
        scratch_shapes=[pltpu.VMEM((2, 2, blk, blk), F32), pltpu.VMEM((2, 2, blk, 1), F32),
                        pltpu.VMEM((2, blk, 1), F32), pltpu.VMEM((2, blk, LANES), F32)],
        compiler_params=_params(("parallel", "parallel", "arbitrary")),
    )(proj, proj, proj)


def _attn_bwd(proj, rtot, dy, bsz, seq):
    blk = ATT_BLOCK
    nq = seq // blk
    n_pairs = D_MODEL // LANES

    def body(q_ref, k_ref, v_ref, dy_ref, rt_ref, dq_ref, dk_ref, dv_ref, dk_acc, dv_acc,
             a_buf, sig_buf, dw_buf, ns_buf, pre_buf, es_buf, dq_buf):
        qi = pl.program_id(2)

        @pl.when(qi == 0)
        def _():
            dk_acc[...] = jnp.zeros_like(dk_acc)
            dv_acc[...] = jnp.zeros_like(dv_acc)

        lane = lax.broadcasted_iota(jnp.int32, (blk, LANES), 1)
        first = lane < HEAD_DIM
        head_row = lax.broadcasted_iota(jnp.int32, (LANES, blk), 0) < HEAD_DIM
        q2 = q_ref[...] * 0.125
        q_rows = (jnp.where(first, q2, 0.0).astype(BF16), jnp.where(first, 0.0, q2).astype(BF16))
        q_t = q2.T
        q_heads = (jnp.where(head_row, q_t, 0.0).astype(BF16), jnp.where(head_row, 0.0, q_t).astype(BF16))
        dy2 = dy_ref[...].astype(F32)
        dy_rows = (jnp.where(first, dy2, 0.0).astype(BF16), jnp.where(first, 0.0, dy2).astype(BF16))
        dy_t = dy2.T
        dy_heads = (jnp.where(head_row, dy_t, 0.0).astype(BF16), jnp.where(head_row, 0.0, dy_t).astype(BF16))
        rt_t = rt_ref[...].T
        rt = (rt_t[0:1, :], rt_t[HEAD_DIM:HEAD_DIM + 1, :])
        rr = lax.broadcasted_iota(jnp.int32, (blk, blk), 0)
        cc = lax.broadcasted_iota(jnp.int32, (blk, blk), 1)
        lower = (cc < rr).astype(BF16)
        lower_eq = (cc <= rr).astype(BF16)
        tri_lt = jnp.concatenate([lower, lower], axis=1)
        tri_le = jnp.concatenate([lower_eq, lower_eq], axis=1)
        causal = rr < cc

        def tri_left(tri2, v):
            hi = v.astype(BF16)
            lo = (v - hi.astype(F32)).astype(BF16)
            return _dot(tri2, jnp.concatenate([hi, lo], axis=0))

        def scores(kb, slot, heads=(0, 1)):
            rows = pl.ds(pl.multiple_of(kb * blk, blk), blk)
            k_blk = k_ref[rows, :].astype(BF16)
            v_blk = v_ref[rows, :].astype(BF16)
            keep = jnp.logical_or(causal, kb < qi)
            for h in heads:
                z = jnp.where(keep, _dot(k_blk, q_heads[h]), -1e30)
                sp, t, den = _softplus_parts(z)
                neg = -sp
                a_buf[slot, h] = z - tri_left(tri_lt, neg)
                sig_buf[slot, h] = jnp.where(z >= 0, 1.0, t) / den
                ns_buf[slot, h] = jnp.sum(neg, axis=0, keepdims=True)
                dw_buf[slot, h] = _dot(v_blk, dy_heads[h])

        def finish(kb, slot, heads=(0, 1)):
            rows = pl.ds(pl.multiple_of(kb * blk, blk), blk)
            k_t = k_ref[rows, :].T.astype(BF16)
            for h in heads:
                pre, esum = pre_buf[h], es_buf[h]
                w = jnp.exp(a_buf[slot, h] + (rt[h] - pre))
                e = dw_buf[slot, h] * w
                dz = e - sig_buf[slot, h] * (esum + tri_left(tri_le, e))
                pre_buf[h] = pre + ns_buf[slot, h]
                es_buf[h] = esum + jnp.sum(e, axis=0, keepdims=True)
                dzb = dz.astype(BF16)
                dq_buf[h] += _dot(k_t, dzb)
                dk_acc[rows, :] += _dot(dzb, q_rows[h])
                dv_acc[rows, :] += _dot(w.astype(BF16), dy_rows[h])

        def step(kb_next, kb, slot):
            scores(kb_next, 1 - slot)
            finish(kb, slot)

        pre_buf[...] = jnp.zeros_like(pre_buf)
        es_buf[...] = jnp.zeros_like(es_buf)
        dq_buf[...] = jnp.zeros_like(dq_buf)
        scores(0, 0)

        def two_steps(p, carry):
            t = 2 * p
            step(t + 1, t, 0)
            step(t + 2, t + 1, 1)
            return carry

        lax.fori_loop(0, qi // 2, two_steps, 0)

        @pl.when(qi % 2 == 1)
        def _():
            step(qi, qi - 1, 0)
            finish(qi, 1)

        @pl.when(qi % 2 == 0)
        def _():
            finish(qi, 0)

        dq_ref[...] = (jnp.where(head_row, dq_buf[0], dq_buf[1]).T * 0.125).astype(BF16)

        @pl.when(qi == nq - 1)
        def _():
            dk_ref[...] = dk_acc[...].astype(BF16)
            dv_ref[...] = dv_acc[...].astype(BF16)

    t = proj.shape[0]
    q_spec = pl.BlockSpec((blk, LANES), lambda b, p, i: (b * nq + i, p))
    kv_out = pl.BlockSpec((seq, LANES), lambda b, p, i: (b, p))
    out = jax.ShapeDtypeStruct((t, D_MODEL), BF16)
    return pl.pallas_call(
        body, name="attn_bwd", grid=(bsz, n_pairs, nq),
        in_specs=[q_spec,
                  pl.BlockSpec((seq, LANES), lambda b, p, i: (b, n_pairs + p)),
                  pl.BlockSpec((seq, LANES), lambda b, p, i: (b, 2 * n_pairs + p)),
                  q_spec, q_spec],
        out_specs=[q_spec, kv_out, kv_out], out_shape=[out, out, out],
        scratch_shapes=[pltpu.VMEM((seq, LANES), F32), pltpu.VMEM((seq, LANES), F32),
                        pltpu.VMEM((2, 2, blk, blk), F32), pltpu.VMEM((2, 2, blk, blk), F32),
                        pltpu.VMEM((2, 2, blk, blk), F32), pltpu.VMEM((2, 2, 1, blk), F32),
                        pltpu.VMEM((2, 1, blk), F32), pltpu.VMEM((2, 1, blk), F32),
                        pltpu.VMEM((2, LANES, blk), F32)],
        compiler_params=_params(("parallel", "parallel", "arbitrary")),
    )(proj, proj, proj, dy, rtot)


def _adamw(w, g, m, v, name, after):
    rows, cols = w.shape
    tr = _pick(rows, (256, 352, 128, 64, 32, 16, 8))
    c1 = 1.0 - ADAM_B1 ** ADAM_STEP
    c2 = 1.0 - ADAM_B2 ** ADAM_STEP

    def body(w_ref, g_ref, m_ref, v_ref, after_ref, d_ref, nm_ref, nv_ref):
        g_v = g_ref[...]
        nm = ADAM_B1 * m_ref[...] + (1.0 - ADAM_B1) * g_v
        nv = ADAM_B2 * v_ref[...] + (1.0 - ADAM_B2) * (g_v * g_v)
        nm_ref[...] = nm
        nv_ref[...] = nv
        d_ref[...] = -ADAM_LR * ((nm / c1) / (jnp.sqrt(nv / c2) + ADAM_EPS) + ADAM_WD * w_ref[...])

    spec = pl.BlockSpec((tr, cols), lambda i: (i, 0))
    shape = jax.ShapeDtypeStruct(w.shape, F32)
    return pl.pallas_call(
        body, name=name, grid=(rows // tr,), in_specs=[spec] * 4 + [pl.BlockSpec(memory_space=pl.ANY)],
        out_specs=[spec] * 3, out_shape=[shape] * 3,
        compiler_params=_params(("parallel",)),
    )(w, g, m, v, after)


def _ffn_fwd(u, w_gu, w_down, bsz, seq, tag):
    h, p = _ffn_up_act(u, w_gu, f"{tag}_up")
    f = _matmul(p, w_down, mode="nn", out_dtype=F32, name=f"{tag}_down")
    return h, p, f


def _ffn_bwd(df, u, h, p, w_gu, w_down, bsz, seq, tag):
    dh = _ffn_down_bwd_act(df, w_down, h, f"{tag}_ddown")
    g_down = _matmul(p, df, mode="tn", out_dtype=F32, name=f"{tag}_gdown")
    g_gu = _matmul(u, dh, mode="tn", out_dtype=F32, name=f"{tag}_ggu", out_stacked=True)
    du = _matmul(dh, w_gu, mode="nt", out_dtype=F32, name=f"{tag}_dup")
    return du, g_gu, g_down


def _local_step(x, c, target, wts, vecs, fetch=None, early_grads=None):
    wts = dict(wts)
    bsz, seq, _ = x.shape
    t = bsz * seq
    x0 = x.reshape(t, D_MODEL)
    tgt = target.reshape(t, D_MODEL)

    sc = _silu_pad(c)
    mod16 = _matmul(sc, wts["w_ada"], mode="nn", out_dtype=F32, name="ada_fwd", bias=vecs["b_ada"])
    mod = mod16[:bsz].reshape(bsz, 9, D_MODEL)

    u1 = _mod_in(x0, mod, bsz, seq, 0)
    h1, p1 = _ffn_up_act(u1, wts["ffn1_w_gu"], "ffn1_up")
    if fetch is not None:
        wts.update(fetch("down", p1))
    f1 = _matmul(p1, wts["ffn1_w_down"], mode="nn", out_dtype=F32, name="ffn1_down")
    r1, x1, u2 = _res_ln_fwd(x0, f1, mod, vecs["ln1_g"], vecs["ln1_b"], bsz, seq, 0, 0.5)
    if fetch is not None:
        wts.update(fetch("later", r1))

    proj = _matmul(u2, wts["w_in"], mode="nn", out_dtype=F32, name="mix_in")
    ya, rtot = _attn_fwd(proj, bsz, seq)
    cs, cv = _conv_fwd(proj, wts["conv_w"], vecs["conv_b"], vecs["conv_ln_g"], vecs["conv_ln_b"], bsz, seq)
    merged, ysb, yconv = _merge_fwd(proj, ya, cs, wts["w_sb_out"], wts["w_conv_out"])
    o2 = _matmul(merged, wts["w_out"], mode="nn", out_dtype=F32, name="mix_out")
    r2, x2, u3 = _res_ln_fwd(x1, o2, mod, vecs["ln2_g"], vecs["ln2_b"], bsz, seq, 1, 1.0)

    h3, p3, f3 = _ffn_fwd(u3, wts["ffn2_w_gu"], wts["ffn2_w_down"], bsz, seq, "ffn2")
    r3, dy, loss_blk = _res_ln_fwd(x2, f3, mod, vecs["ln3_g"], vecs["ln3_b"], bsz, seq, 2, 0.5, target=tgt)

    grads = {}
    dxres, df, ln3s, g3s = _res_ln_bwd(r3, dy, f3, mod, vecs["ln3_g"], bsz, seq, 2, 0.5)
    du, grads["ffn2_w_gu"], grads["ffn2_w_down"] = _ffn_bwd(
        df, u3, h3, p3, wts["ffn2_w_gu"], wts["ffn2_w_down"], bsz, seq, "ffn2")
    dx2, m3s = _mod_bwd(dxres, du, x2, mod, bsz, seq, 2)

    dxres, do2, ln2s, g2s = _res_ln_bwd(r2, dx2, o2, mod, vecs["ln2_g"], bsz, seq, 1, 1.0)
    grads["w_out"] = _matmul(merged, do2, mode="tn", out_dtype=F32, name="mix_out_g")
    dysb, dyconv, dgate = _merge_bwd(proj, ysb, yconv, do2, wts["w_out"])
    dya = _matmul(dysb, wts["w_sb_out"], mode="nt", out_dtype=BF16, name="sb_out_d")
    grads["w_sb_out"] = _matmul(ya, dysb, mode="tn", out_dtype=F32, name="sb_out_g")
    dcs = _matmul(dyconv, wts["w_conv_out"], mode="nt", out_dtype=F32, name="conv_out_d")
    grads["w_conv_out"] = _matmul(cs, dyconv, mode="tn", out_dtype=F32, name="conv_out_g")
    dcv, convs = _conv_bwd_ln(dcs, cv, vecs["conv_ln_g"], vecs["conv_ln_b"], bsz, seq)
    dglu, g_conv_w = _conv_bwd_taps(proj, dcv, wts["conv_w"], bsz, seq)
    dq, dk, dv = _attn_bwd(proj, rtot, dya, bsz, seq)
    dproj = jnp.concatenate([dq, dk, dv, dglu, dgate], axis=1)
    grads["w_in"] = _matmul(u2, dproj, mode="tn", out_dtype=F32, name="mix_in_g", out_stacked=True)
    if early_grads is not None:
        mod = mod + early_grads("later_start", {n: grads.pop(n) for n in list(grads)})
    du = _matmul(dproj, wts["w_in"], mode="nt", out_dtype=F32, name="mix_in_d")
    dx1, m2s = _mod_bwd(dxres, du, x1, mod, bsz, seq, 1)
    if early_grads is not None:
        mod = mod + early_grads("later_go", dx1)

    dxres, df, ln1s, g1s = _res_ln_bwd(r1, dx1, f1, mod, vecs["ln1_g"], bsz, seq, 0, 0.5)
    dh = _ffn_down_bwd_act(df, wts["ffn1_w_down"], h1, "ffn1_ddown")
    grads["ffn1_w_down"] = _matmul(p1, df, mode="tn", out_dtype=F32, name="ffn1_gdown")
    grads["ffn1_w_gu"] = _matmul(u1, dh, mode="tn", out_dtype=F32, name="ffn1_ggu", out_stacked=True)
    if early_grads is not None:
        mod = mod + early_grads("mid_start", {n: grads.pop(n) for n in list(grads)})
    du = _matmul(dh, wts["ffn1_w_gu"], mode="nt", out_dtype=F32, name="ffn1_dup")
    if early_grads is not None:
        mod = mod + early_grads("mid_go", du)
    grad_x, m1s = _mod_bwd(dxres, du, x0, mod, bsz, seq, 0)

    dmod = jnp.stack([m1s[:, 0], m1s[:, 1], g1s[:, 0], m2s[:, 0], m2s[:, 1], g2s[:, 0],
                      m3s[:, 0], m3s[:, 1], g3s[:, 0]], axis=1)
    dmod16 = jnp.zeros((16, 9 * D_MODEL), F32).at[:bsz].set(dmod.reshape(bsz, 9 * D_MODEL))
    grads["w_ada"] = _matmul(sc, dmod16.astype(BF16), mode="tn", out_dtype=F32, name="ada_g", out_stacked=True)

    small = {"dmod": dmod, "ln1": ln1s, "ln2": ln2s, "ln3": ln3s, "conv": convs, "conv_w": g_conv_w,
             "loss": loss_blk}
    return grad_x.reshape(x.shape), grads, small


_HBM = pl.BlockSpec(memory_space=pltpu.HBM)


def _position():
    return lax.axis_index("x"), lax.axis_index("y"), lax.axis_index("c")


def _other_chips(x, y):
    return [(1 - x, y), (x, 1 - y), (1 - x, 1 - y)]


def _cast_into_stack(w_local, chip, name):
    rows, cols = w_local.shape
    tr = _pick(rows, (256, 352, 128, 64, 32, 16))

    def body(chip_ref, w_ref, o_ref):
        o_ref[...] = w_ref[...].astype(BF16)

    return pl.pallas_call(
        body, name=name,
        grid_spec=pltpu.PrefetchScalarGridSpec(
            num_scalar_prefetch=1, grid=(rows // tr,),
            in_specs=[pl.BlockSpec((tr, cols), lambda r, chip_ref: (r, 0))],
            out_specs=pl.BlockSpec((None, tr, cols), lambda r, chip_ref: (chip_ref[0], r, 0))),
        out_shape=jax.ShapeDtypeStruct((N_CHIPS, rows, cols), BF16),
        compiler_params=_params(("parallel",)),
    )(chip, w_local)


def _all_gather_weights(stacks, small):
    n = len(stacks)

    def body(*refs):
        ins, small_in, outs, small_out = refs[:n], refs[n], refs[n + 1:2 * n + 1], refs[2 * n + 1]
        send_sems, recv_sems, fwd_send_sems, fwd_recv_sems, small_sems = refs[2 * n + 2:]
        x, y, c = _position()
        me = 2 * x + y
        chips = _other_chips(x, y)

        def send(i, j):
            px, py = chips[j]
            return pltpu.make_async_remote_copy(
                src_ref=ins[i].at[me, c], dst_ref=outs[i].at[me, c], send_sem=send_sems.at[3 * i + j],
                recv_sem=recv_sems.at[3 * i + j], device_id=(px, py, c), device_id_type=MESH)

        def landed(i, j):
            px, py = chips[j]
            return pltpu.make_async_remote_copy(
                src_ref=ins[i].at[me, c], dst_ref=outs[i].at[2 * px + py, c], send_sem=send_sems.at[3 * i + j],
                recv_sem=recv_sems.at[3 * i + j], device_id=(px, py, c), device_id_type=MESH)

        def forward(i, j, half):
            px, py = chips[j]
            blk = outs[i].at[2 * px + py, half]
            return pltpu.make_async_remote_copy(
                src_ref=blk, dst_ref=blk, send_sem=fwd_send_sems.at[3 * i + j],
                recv_sem=fwd_recv_sems.at[3 * i + j], device_id=(x, y, 1 - c), device_id_type=MESH)

        def small_copy(j, slot):
            px, py = chips[j]
            return pltpu.make_async_remote_copy(
                src_ref=small_in, dst_ref=small_out.at[slot], send_sem=small_sems.at[j],
                recv_sem=small_sems.at[3 + j], device_id=(px, py, c), device_id_type=MESH)

        own_small = pltpu.make_async_copy(small_in, small_out.at[me], small_sems.at[6])
        own_small.start()
        for j in range(3):
            small_copy(j, me).start()
        for i in range(n):
            for j in range(3):
                send(i, j).start()
        for i in range(n):
            for j in range(3):
                landed(i, j).wait_recv()
                forward(i, j, c).start()
        for i in range(n):
            for j in range(3):
                forward(i, j, 1 - c).wait_recv()
        for j, (px, py) in enumerate(chips):
            small_copy(j, 2 * px + py).wait_recv()
        own_small.wait()
        for j in range(3):
            small_copy(j, me).wait_send()
        for i in range(n):
            for j in range(3):
                send(i, j).wait_send()
                forward(i, j, c).wait_send()

    return pl.pallas_call(
        body, name="all_gather_weights",
        out_shape=[jax.ShapeDtypeStruct(s.shape, s.dtype) for s in stacks]
        + [jax.ShapeDtypeStruct((N_CHIPS,) + small.shape, small.dtype)],
        in_specs=[_HBM] * (n + 1), out_specs=[_HBM] * (n + 1),
        input_output_aliases={i: i for i in range(n)},
        scratch_shapes=[pltpu.SemaphoreType.DMA((3 * n,)), pltpu.SemaphoreType.DMA((3 * n,)),
                        pltpu.SemaphoreType.DMA((3 * n,)), pltpu.SemaphoreType.DMA((3 * n,)),
                        pltpu.SemaphoreType.DMA((7,))],
    )(*stacks, small)


_SEM = pl.BlockSpec(memory_space=pltpu.SEMAPHORE)
_DATAFLOW = pltpu.SideEffectType.DATAFLOW_SIDE_EFFECTING


_COPIES = {"gather": 3, "scatter": 3, "swap": N_CHIPS}


def _exchange_plan(kind, src, land):
    x, y, c = _position()
    me = 2 * x + y
    if kind == "swap":
        return [(src.at[k, 1 - c], land.at[k], land.at[k], (x, y, 1 - c)) for k in range(N_CHIPS)]
    plan = []
    for j, (px, py) in enumerate(_other_chips(x, y)):
        if kind == "gather":
            plan.append((src.at[me, c], land.at[me, c], land.at[2 * px + py, c], (px, py, c)))
        else:
            plan.append((src.at[2 * px + py], land.at[j], land.at[j], (px, py, c)))
    return plan


def _exchange_start(kind, srcs, lands, name, after):
    n = len(srcs)
    per = _COPIES[kind]
    in_place = lands is None
    n_in = n if in_place else 2 * n

    def body(*refs):
        src_refs = refs[:n]
        land_refs = src_refs if in_place else refs[n:2 * n]
        send_sems, recv_sems = refs[n_in + 1], refs[n_in + 2]
        token = refs[-1]
        for i in range(n):
            for j, (src, dst, _, to) in enumerate(_exchange_plan(kind, src_refs[i], land_refs[i])):
                pltpu.make_async_remote_copy(
                    src_ref=src, dst_ref=dst, send_sem=send_sems.at[per * i + j], recv_sem=recv_sems.at[per * i + j],
                    device_id=to, device_id_type=MESH).start()
        token[...] = jnp.zeros_like(token)

    operands = list(srcs) + ([] if in_place else list(lands))
    operands = [pltpu.with_memory_space_constraint(o, pltpu.HBM) for o in operands]
    out = pl.pallas_call(
        body, name=name,
        out_shape=[pltpu.SemaphoreType.DMA((per * n,)), pltpu.SemaphoreType.DMA((per * n,))]
        + [pltpu.HBM(o.shape, o.dtype) for o in operands] + [jax.ShapeDtypeStruct((8, LANES), F32)],
        in_specs=[_HBM] * n_in + [pl.BlockSpec(memory_space=pl.ANY)],
        out_specs=[_SEM, _SEM] + [_HBM] * n_in + [pl.BlockSpec(memory_space=pltpu.VMEM)],
        input_output_aliases={i: 2 + i for i in range(n_in)},
        compiler_params=pltpu.CompilerParams(has_side_effects=_DATAFLOW),
    )(*operands, after)
    return out[0], out[1], list(out[2:2 + n_in]), out[-1]


def _exchange_wait(kind, send_sems, recv_sems, thru, in_place, after, name):
    n_in = len(thru)
    n = n_in if in_place else n_in // 2
    per = _COPIES[kind]

    def body(*refs):
        src_refs = refs[:n]
        land_refs = src_refs if in_place else refs[n:2 * n]
        send_sems, recv_sems = refs[n_in], refs[n_in + 1]
        for i in range(n):
            for j, (src, _, here, to) in enumerate(_exchange_plan(kind, src_refs[i], land_refs[i])):
                copy = pltpu.make_async_remote_copy(
                    src_ref=src, dst_ref=here, send_sem=send_sems.at[per * i + j], recv_sem=recv_sems.at[per * i + j],
                    device_id=to, device_id_type=MESH)
                copy.wait_send()
                copy.wait_recv()

    out = pl.pallas_call(
        body, name=name, out_shape=[pltpu.HBM(o.shape, o.dtype) for o in thru],
        in_specs=[_HBM] * n_in + [_SEM, _SEM, pl.BlockSpec(memory_space=pl.ANY)], out_specs=[_HBM] * n_in,
        input_output_aliases={i: i for i in range(n_in)},
        compiler_params=pltpu.CompilerParams(has_side_effects=_DATAFLOW),
    )(*thru, send_sems, recv_sems, after)
    return list(out[:n]), (list(out[:n]) if in_place else list(out[n:]))


def _gather_forward(stacks, name):
    n = len(stacks)

    def body(*refs):
        ins, outs = refs[:n], refs[n:2 * n]
        send_sems, recv_sems = refs[2 * n:]
        x, y, c = _position()
        chips = _other_chips(x, y)

        def copy(i, j, half):
            px, py = chips[j]
            return pltpu.make_async_remote_copy(
                src_ref=ins[i].at[2 * px + py, half], dst_ref=outs[i].at[2 * px + py, half],
                send_sem=send_sems.at[3 * i + j], recv_sem=recv_sems.at[3 * i + j],
                device_id=(x, y, 1 - c), device_id_type=MESH)

        for i in range(n):
            for j in range(3):
                copy(i, j, c).start()
        for i in range(n):
            for j in range(3):
                copy(i, j, 1 - c).wait_recv()
        for i in range(n):
            for j in range(3):
                copy(i, j, c).wait_send()

    return pl.pallas_call(
        body, name=name, out_shape=[jax.ShapeDtypeStruct(s.shape, s.dtype) for s in stacks],
        in_specs=[_HBM] * n, out_specs=[_HBM] * n, input_output_aliases={i: i for i in range(n)},
        scratch_shapes=[pltpu.SemaphoreType.DMA((3 * n,)), pltpu.SemaphoreType.DMA((3 * n,))],
    )(*stacks)


def _pair_add(g, got, place, name):
    _, _, rh, cols = g.shape
    tr = _pick(rh, (256, 176, 128, 64, 32, 16, 8))

    def body(place_ref, g_ref, got_ref, p_ref, own_ref):
        s = g_ref[...] + got_ref[...]
        p_ref[...] = s.astype(BF16)

        @pl.when(pl.program_id(1) == place_ref[1])
        def _():
            own_ref[...] = s

    blk = pl.BlockSpec((None, tr, cols), lambda r, k, place_ref: (k, r, 0))
    return pl.pallas_call(
        body, name=name,
        grid_spec=pltpu.PrefetchScalarGridSpec(
            num_scalar_prefetch=1, grid=(rh // tr, N_CHIPS),
            in_specs=[pl.BlockSpec((None, None, tr, cols), lambda r, k, place_ref: (k, place_ref[0], r, 0)), blk],
            out_specs=[blk, pl.BlockSpec((tr, cols), lambda r, k, place_ref: (r, 0))]),
        out_shape=[jax.ShapeDtypeStruct((N_CHIPS, rh, cols), BF16), jax.ShapeDtypeStruct((rh, cols), F32)],
        compiler_params=_params(("parallel", "arbitrary")),
    )(place, g, got)


def _chip_sum(own, parts, place, name):
    rh, cols = own.shape
    tr = _pick(rh, (256, 176, 128, 64, 32, 16, 8))

    def body(place_ref, own_ref, p_ref, o_ref):
        o_ref[...] = ((own_ref[...] + p_ref[0].astype(F32)) + p_ref[1].astype(F32)) + p_ref[2].astype(F32)

    return pl.pallas_call(
        body, name=name,
        grid_spec=pltpu.PrefetchScalarGridSpec(
            num_scalar_prefetch=1, grid=(rh // tr,),
            in_specs=[pl.BlockSpec((tr, cols), lambda r, place_ref: (r, 0)),
                      pl.BlockSpec((3, tr, cols), lambda r, place_ref: (0, r, 0))],
            out_specs=pl.BlockSpec((None, tr, cols), lambda r, place_ref: (place_ref[0], r, 0))),
        out_shape=jax.ShapeDtypeStruct((2, rh, cols), F32),
        compiler_params=_params(("parallel",)),
    )(place, own, parts)


def _pair_gather(halves, name):
    n = len(halves)

    def body(*refs):
        ins, outs = refs[:n], refs[n:2 * n]
        send_sems, recv_sems = refs[2 * n:]
        x, y, c = _position()

        def send(i):
            return pltpu.make_async_remote_copy(
                src_ref=ins[i].at[c], dst_ref=outs[i].at[c], send_sem=send_sems.at[i], recv_sem=recv_sems.at[i],
                device_id=(x, y, 1 - c), device_id_type=MESH)

        def landed(i):
            return pltpu.make_async_remote_copy(
                src_ref=ins[i].at[c], dst_ref=outs[i].at[1 - c], send_sem=send_sems.at[i], recv_sem=recv_sems.at[i],
                device_id=(x, y, 1 - c), device_id_type=MESH)

        for i in range(n):
            send(i).start()
        for i in range(n):
            landed(i).wait_recv()
        for i in range(n):
            send(i).wait_send()

    return pl.pallas_call(
        body, name=name,
        out_shape=[jax.ShapeDtypeStruct(h.shape, F32) for h in halves],
        in_specs=[_HBM] * n, out_specs=[_HBM] * n,
        input_output_aliases={i: i for i in range(n)},
        scratch_shapes=[pltpu.SemaphoreType.DMA((n,)), pltpu.SemaphoreType.DMA((n,))],
    )(*halves)


_MOD_ROWS = 16


def _small_all_reduce(buf, bsz, after):
    rows, cols = buf.shape
    head = bsz * _MOD_ROWS
    out_rows = rows - head + _MOD_ROWS

    def body(in_ref, after_ref, o_ref, gath, send_sems, recv_sems):
        x, y, c = _position()
        me = 4 * x + 2 * y + c

        def peer(mask):
            return (x ^ (mask >> 2), y ^ ((mask >> 1) & 1), c ^ (mask & 1))

        def copy(mask):
            return pltpu.make_async_remote_copy(
                src_ref=in_ref, dst_ref=gath.at[me], send_sem=send_sems.at[mask - 1],
                recv_sem=recv_sems.at[mask - 1], device_id=peer(mask), device_id_type=MESH)

        def arrival(mask):
            px, py, pc = peer(mask)
            return pltpu.make_async_remote_copy(
                src_ref=in_ref, dst_ref=gath.at[4 * px + 2 * py + pc], send_sem=send_sems.at[mask - 1],
                recv_sem=recv_sems.at[mask - 1], device_id=peer(mask), device_id_type=MESH)

        for mask in range(1, N_DEV):
            copy(mask).start()
        gath[me] = in_ref[...]
        for mask in range(1, N_DEV):
            arrival(mask).wait_recv()
        for mask in range(1, N_DEV):
            copy(mask).wait_send()
        acc = gath[0]
        for d in range(1, N_DEV):
            acc = acc + gath[d]
        mod = acc[0:_MOD_ROWS]
        for s in range(1, bsz):
            mod = mod + acc[s * _MOD_ROWS:(s + 1) * _MOD_ROWS]
        o_ref[0:_MOD_ROWS, :] = mod
        o_ref[_MOD_ROWS:, :] = acc[head:]

    vm = pl.BlockSpec(memory_space=pltpu.VMEM)
    return pl.pallas_call(
        body, name="small_all_reduce", in_specs=[vm, pl.BlockSpec(memory_space=pl.ANY)], out_specs=vm,
        out_shape=jax.ShapeDtypeStruct((out_rows, cols), F32),
        scratch_shapes=[pltpu.VMEM((N_DEV, rows, cols), F32), pltpu.SemaphoreType.DMA((N_DEV - 1,)),
                        pltpu.SemaphoreType.DMA((N_DEV - 1,))],
        compiler_params=pltpu.CompilerParams(vmem_limit_bytes=VMEM_LIMIT),
    )(buf, after)


_COL_SHARDED = ("w_ada", "ffn1_w_gu", "w_in", "ffn2_w_gu")
_ROW_SHARDED = ("ffn1_w_down", "w_sb_out", "w_conv_out", "w_out", "ffn2_w_down")
_NOW = ["w_ada", "ffn1_w_gu"]
_SOON = ["ffn1_w_down"]
_LATER = ["w_in", "w_sb_out", "w_conv_out", "w_out", "ffn2_w_gu", "ffn2_w_down"]
_VECS = ("b_ada", "ln1_g", "ln1_b", "conv_b", "conv_ln_g", "conv_ln_b", "ln2_g", "ln2_b", "ln3_g", "ln3_b")
_WEIGHTS = ("w_ada", "b_ada", "ffn1_w_gu", "ffn1_w_down", "ln1_g", "ln1_b", "w_in", "w_sb_out", "conv_w", "conv_b",
            "conv_ln_g", "conv_ln_b", "w_conv_out", "w_out", "ln2_g", "ln2_b", "ffn2_w_gu", "ffn2_w_down",
            "ln3_g", "ln3_b")


def _step(x, c, target, w, m, v):
    bsz = x.shape[0]
    chip = 2 * lax.axis_index("x") + lax.axis_index("y")
    core = lax.axis_index("c")

    chip_arr = jnp.reshape(chip, (1,)).astype(jnp.int32)
    place = jnp.stack([core, chip]).astype(jnp.int32)

    def stack_of(n):
        rows, cols = w[n].shape[1:]
        return _cast_into_stack(w[n][0], chip_arr, f"cast_{n}").reshape(N_CHIPS, 2, rows // 2, cols)

    def gathered_form(n, g):
        rows, cols = w[n].shape[1:]
        return g.reshape(N_CHIPS, rows, cols) if n in _COL_SHARDED else g.reshape(N_CHIPS * rows, cols)

    conv_w_local = jnp.pad(w["conv_w"][0], ((0, 1), (0, 0)))
    gathered = _all_gather_weights([stack_of(n) for n in _NOW], conv_w_local)
    wts = {n: gathered_form(n, g) for n, g in zip(_NOW, gathered[:-1])}
    wts["conv_w"] = gathered[-1].transpose(1, 0, 2).reshape(32, D_MODEL)
    pending, behind = {}, gathered[0]
    for stage, names in (("down", _SOON), ("later", _LATER)):
        send, recv, thru, token = _exchange_start(
            "gather", [stack_of(n) for n in names], None, f"gather_start_{stage}", behind)
        pending[stage] = (names, send, recv, thru)
        behind = token
    vecs = {n: w[n] for n in _VECS}
    vecs["b_ada"] = vecs["b_ada"] + behind[0, 0]

    def fetch(stage, after):
        names, send, recv, thru = pending[stage]
        landed, _ = _exchange_wait("gather", send, recv, thru, True, after, f"gather_wait_{stage}")
        forwarded = _gather_forward(landed, f"gather_forward_{stage}")
        return {n: gathered_form(n, g) for n, g in zip(names, forwarded)}

    groups = {"later": _LATER, "mid": ["ffn1_w_gu", "ffn1_w_down"], "last": ["w_ada"]}
    g_out, updates, state = {}, {}, {}

    def adam(names, after):
        for n in names:
            shape = w[n].shape
            flat = shape[1:] if len(shape) == 3 else shape
            d, nm, nv = _adamw(w[n].reshape(flat), g_out[n].reshape(flat), m[n].reshape(flat), v[n].reshape(flat),
                               f"adamw_{n}", after)
            updates[n] = (g_out[n].reshape(shape), d.reshape(shape), nm.reshape(shape), nv.reshape(shape))
            after = nv
        return after

    def swap_start(tag, grads):
        views = [grads[n].reshape(N_CHIPS, 2, w[n].shape[1] // 2, w[n].shape[2]) for n in groups[tag]]
        lands = [lax.empty((N_CHIPS,) + g.shape[2:], F32) for g in views]
        send, recv, thru, token = _exchange_start("swap", views, lands, f"swap_start_{tag}", place)
        state[tag] = {"swap": (send, recv, thru)}
        return token

    def scatter_start(tag, after):
        views, got = _exchange_wait("swap", *state[tag]["swap"], False, after, f"swap_wait_{tag}")
        sums = [_pair_add(g, r, place, f"pair_add_{n}") for n, g, r in zip(groups[tag], views, got)]
        lands = [lax.empty((3,) + p.shape[1:], BF16) for p, _ in sums]
        send, recv, thru, token = _exchange_start(
            "scatter", [p for p, _ in sums], lands, f"scatter_start_{tag}", place)
        state[tag].update(scatter=(send, recv, thru), sums=sums)
        return token

    def collect(tag, after):
        _, parts = _exchange_wait("scatter", *state[tag]["scatter"], False, after, f"scatter_wait_{tag}")
        halves = [_chip_sum(own, p, place, f"chip_sum_{n}")
                  for n, (_, own), p in zip(groups[tag], state[tag]["sums"], parts)]
        for n, f in zip(groups[tag], _pair_gather(halves, f"grad_pair_gather_{tag}")):
            g_out[n] = f.reshape(w[n].shape[1:])
        return g_out[groups[tag][-1]]

    def early_grads(stage, value):
        tag, step = stage.split("_")
        token = swap_start(tag, value) if step == "start" else scatter_start(tag, value)
        return token[0, 0]

    grad_x, grads, small = _local_step(x, c, target, wts, vecs, fetch, early_grads)

    token = swap_start("last", grads)
    done = collect("later", token)
    token = scatter_start("last", done)
    done = adam(groups["later"], token)
    done = collect("mid", done)
    done = adam(groups["mid"], done)
    done = collect("last", done)

    dmod = jnp.pad(small["dmod"], ((0, 0), (0, _MOD_ROWS - 9), (0, 0))).reshape(bsz * _MOD_ROWS, D_MODEL)
    loss_rows = jnp.pad(small["loss"], ((0, 0), (0, D_MODEL - LANES)))
    buf = jnp.concatenate([dmod, small["ln1"], small["ln2"], small["ln3"], small["conv"], small["conv_w"],
                           loss_rows], axis=0)
    red = _small_all_reduce(buf, bsz, done)
    o = _MOD_ROWS
    g_out["b_ada"] = red[0:9].reshape(1, 9 * D_MODEL)
    g_out["ln1_g"], g_out["ln1_b"] = red[o:o + 1], red[o + 1:o + 2]
    g_out["ln2_g"], g_out["ln2_b"] = red[o + 8:o + 9], red[o + 9:o + 10]
    g_out["ln3_g"], g_out["ln3_b"] = red[o + 16:o + 17], red[o + 17:o + 18]
    g_out["conv_ln_g"], g_out["conv_ln_b"], g_out["conv_b"] = red[o + 24:o + 25], red[o + 25:o + 26], red[o + 26:o + 27]
    cw = w["conv_w"].shape[2]
    g_out["conv_w"] = lax.dynamic_slice(red[o + 32:o + 32 + CONV_TAPS], (0, chip * cw), (CONV_TAPS, cw))
    loss = red[o + 64, 0]

    adam(groups["last"] + list(_VECS) + ["conv_w"], place)
    return (loss, grad_x, *[updates[n][k] for k in range(4) for n in _WEIGHTS])


def kernel(x, c, w_ada, b_ada, ffn1_w_gu, ffn1_w_down, ln1_g, ln1_b, w_in, w_sb_out, conv_w, conv_b, conv_ln_g, conv_ln_b, w_conv_out, w_out, ln2_g, ln2_b, ffn2_w_gu, ffn2_w_down, ln3_g, ln3_b, loss_target, m_w_ada, m_b_ada, m_ffn1_w_gu, m_ffn1_w_down, m_ln1_g, m_ln1_b, m_w_in, m_w_sb_out, m_conv_w, m_conv_b, m_conv_ln_g, m_conv_ln_b, m_w_conv_out, m_w_out, m_ln2_g, m_ln2_b, m_ffn2_w_gu, m_ffn2_w_down, m_ln3_g, m_ln3_b, v_w_ada, v_b_ada, v_ffn1_w_gu, v_ffn1_w_down, v_ln1_g, v_ln1_b, v_w_in, v_w_sb_out, v_conv_w, v_conv_b, v_conv_ln_g, v_conv_ln_b, v_w_conv_out, v_w_out, v_ln2_g, v_ln2_b, v_ffn2_w_gu, v_ffn2_w_down, v_ln3_g, v_ln3_b):
    given = dict(locals())
    w = {n: given[n] for n in _WEIGHTS}
    m = {n: given["m_" + n] for n in _WEIGHTS}
    v = {n: given["v_" + n] for n in _WEIGHTS}
    return _step(x, c, loss_target, w, m, v)
```

```python
import functools

import jax
import jax.numpy as jnp
from jax import lax
from jax.experimental import pallas as pl
from jax.experimental.pallas import tpu as pltpu

F32 = jnp.float32
BF16 = jnp.bfloat16

D_MODEL = 1024
D_FF = 2816
HEADS = 16
HEAD_DIM = 64
LANES = 128
CONV_TAPS = 31
HALO = 32
N_CHIPS = 4
N_DEV = 8
ALPHA = 2.0 ** 0.25
LN_EPS = 1e-5
ATT_BLOCK = 256
VMEM_LIMIT = 56 * 1024 * 1024

ADAM_LR = 0.001
ADAM_B1 = 0.9
ADAM_B2 = 0.999
ADAM_EPS = 1e-08
ADAM_WD = 0.01
ADAM_STEP = 10

MESH = pl.DeviceIdType.MESH


def _pick(n, cands):
    for t in cands:
        if t <= n and n % t == 0:
            return t
    return n


def _params(sem):
    return pltpu.CompilerParams(dimension_semantics=sem, vmem_limit_bytes=VMEM_LIMIT)


def _sigmoid(z):
    t = jnp.exp(-jnp.abs(z))
    return jnp.where(z >= 0, 1.0, t) / (1.0 + t)


def _silu(z):
    return z * _sigmoid(z)


def _dsilu(z):
    s = _sigmoid(z)
    return s * (1.0 + z * (1.0 - s))


def _ln_stats(r):
    mu = jnp.mean(r, axis=-1, keepdims=True)
    d = r - mu
    var = jnp.mean(d * d, axis=-1, keepdims=True)
    rstd = lax.rsqrt(var + LN_EPS)
    return d * rstd, rstd


def _colsum(v):
    return jnp.sum(v, axis=0, keepdims=True)


_DIMS = {"nn": (((1,), (0,)), ((), ())), "nt": (((1,), (1,)), ((), ())), "tn": (((0,), (0,)), ((), ()))}
_TN_CANDS = (1408, 1792, 1152, 1024, 512, 256, 128)
_TK_CANDS = (1024, 1408, 896, 512, 256, 128)


def _matmul(a, b, *, mode, out_dtype, name, bias=None, out_stacked=False):
    a_halves = mode == "nt" and a.ndim == 3
    b_halves = mode == "tn" and b.ndim == 3
    b_stacked = b.ndim == 3 and not b_halves
    if mode == "nn":
        m, k = a.shape
        n_c = b.shape[-1]
        n = n_c * (N_CHIPS if b_stacked else 1)
        k_c = k
    elif mode == "nt":
        m = a.shape[-2]
        k = a.shape[-1] * (2 if a_halves else 1)
        n = b.shape[-2]
        k_c = b.shape[-1]
        n_c = n
    else:
        k, m = a.shape
        n = b.shape[-1] * (2 if b_halves else 1)
        n_c = n // N_CHIPS if out_stacked else n
        k_c = k
    if mode == "tn":
        tm = _pick(m, (1024, 1408, 512, 256, 128))
        tk = _pick(k, (512, 256, 128, 64, 32, 16))
    else:
        tm = _pick(m, (1024, 512, 256, 128, 64, 32, 16))
        tk = _pick(k_c, _TK_CANDS)
    tn = _pick(n_c, _TN_CANDS)
    nb = n_c // tn
    kb = k_c // tk
    nk = k // tk
    grid = (m // tm, n // tn, nk)

    if mode == "nn":
        a_spec = pl.BlockSpec((tm, tk), lambda i, j, kk: (i, kk))
        if b_stacked:
            b_spec = pl.BlockSpec((None, tk, tn), lambda i, j, kk: (j // nb, kk, j % nb))
        else:
            b_spec = pl.BlockSpec((tk, tn), lambda i, j, kk: (kk, j))
    elif mode == "nt":
        if a_halves:
            ka = a.shape[-1] // tk
            a_spec = pl.BlockSpec((None, tm, tk), lambda i, j, kk: (kk // ka, i, kk % ka))
        else:
            a_spec = pl.BlockSpec((tm, tk), lambda i, j, kk: (i, kk))
        if b_stacked:
            b_spec = pl.BlockSpec((None, tn, tk), lambda i, j, kk: (kk // kb, j, kk % kb))
        else:
            b_spec = pl.BlockSpec((tn, tk), lambda i, j, kk: (j, kk))
    else:
        a_spec = pl.BlockSpec((tk, tm), lambda i, j, kk: (kk, i))
        if b_halves:
            nh = b.shape[-1] // tn
            b_spec = pl.BlockSpec((None, tk, tn), lambda i, j, kk: (j // nh, kk, j % nh))
        else:
            b_spec = pl.BlockSpec((tk, tn), lambda i, j, kk: (kk, j))
    if out_stacked:
        out_shape = jax.ShapeDtypeStruct((N_CHIPS, m, n_c), out_dtype)
        o_spec = pl.BlockSpec((None, tm, tn), lambda i, j, kk: (j // nb, i, j % nb))
    else:
        out_shape = jax.ShapeDtypeStruct((m, n), out_dtype)
        o_spec = pl.BlockSpec((tm, tn), lambda i, j, kk: (i, j))
    in_specs = [a_spec, b_spec]
    args = [a, b]
    if bias is not None:
        in_specs.append(pl.BlockSpec((1, tn), lambda i, j, kk: (0, j)))
        args.append(bias)
    dims = _DIMS[mode]

    def body(*refs):
        a_ref, b_ref = refs[0], refs[1]
        bias_ref = refs[2] if bias is not None else None

        def write(r):
            if bias_ref is not None:
                r = r + bias_ref[...]
            o_ref[...] = r.astype(o_ref.dtype)

        if nk == 1:
            o_ref = refs[-1]
            write(lax.dot_general(a_ref[...], b_ref[...], dims, preferred_element_type=F32))
            return
        o_ref, acc_ref = refs[-2], refs[-1]
        kk = pl.program_id(2)

        @pl.when(kk == 0)
        def _():
            acc_ref[...] = jnp.zeros_like(acc_ref)

        acc_ref[...] += lax.dot_general(a_ref[...], b_ref[...], dims, preferred_element_type=F32)

        @pl.when(kk == nk - 1)
        def _():
            write(acc_ref[...])

    return pl.pallas_call(
        body, name=name, grid=grid, in_specs=in_specs, out_specs=o_spec, out_shape=out_shape,
        scratch_shapes=[pltpu.VMEM((tm, tn), F32)] if nk > 1 else [],
        compiler_params=_params(("parallel", "parallel", "arbitrary")),
    )(*args)


def _row_grid(bsz, seq, ts):
    ns = seq // ts
    return (bsz, ns), ns


def _rows(ts, width, ns, col=0):
    return pl.BlockSpec((ts, width), lambda b, s: (b * ns + s, col))


def _mod_spec():
    return pl.BlockSpec((None, 9, D_MODEL), lambda b, s: (b, 0, 0))


def _vec_spec(rows=1, width=D_MODEL):
    return pl.BlockSpec((rows, width), lambda b, s: (0, 0))


def _silu_pad(c):
    bsz = c.shape[0]

    def body(c_ref, o_ref):
        o_ref[...] = jnp.zeros_like(o_ref)
        o_ref[0:bsz, :] = _silu(c_ref[...]).astype(BF16)

    return pl.pallas_call(body, name="silu_pad", out_shape=jax.ShapeDtypeStruct((16, D_MODEL), BF16))(c)


def _mod_in(x, mod, bsz, seq, sub):
    ts = _pick(seq, (512, 256, 128))
    grid, ns = _row_grid(bsz, seq, ts)

    def body(x_ref, mod_ref, u_ref):
        sh = mod_ref[3 * sub:3 * sub + 1, :]
        sc = mod_ref[3 * sub + 1:3 * sub + 2, :]
        u_ref[...] = (x_ref[...] * (1.0 + sc) + sh).astype(BF16)

    return pl.pallas_call(
        body, name=f"mod_in{sub}", grid=grid, in_specs=[_rows(ts, D_MODEL, ns), _mod_spec()],
        out_specs=_rows(ts, D_MODEL, ns), out_shape=jax.ShapeDtypeStruct(x.shape, BF16),
        compiler_params=_params(("parallel", "parallel")),
    )(x, mod)


_FFN_TN = D_FF // 2


def _ffn_up_act(u, w_gu, name):
    t = u.shape[0]
    tm = _pick(t, (512, 256, 128))
    tn = _FFN_TN

    def body(u_ref, wa_ref, wg_ref, h_ref, p_ref):
        u_v = u_ref[...]
        a = jnp.dot(u_v, wa_ref[...], preferred_element_type=F32)
        g = jnp.dot(u_v, wg_ref[...], preferred_element_type=F32)
        h_ref[0] = a.astype(BF16)
        h_ref[1] = g.astype(BF16)
        p_ref[...] = (_silu(a) * g).astype(BF16)

    return pl.pallas_call(
        body, name=name, grid=(2, t // tm),
        in_specs=[pl.BlockSpec((tm, D_MODEL), lambda j, i: (i, 0)),
                  pl.BlockSpec((None, D_MODEL, tn), lambda j, i: (j, 0, 0)),
                  pl.BlockSpec((None, D_MODEL, tn), lambda j, i: (j + 2, 0, 0))],
        out_specs=[pl.BlockSpec((2, tm, tn), lambda j, i: (0, i, j)),
                   pl.BlockSpec((tm, tn), lambda j, i: (i, j))],
        out_shape=[jax.ShapeDtypeStruct((2, t, D_FF), BF16), jax.ShapeDtypeStruct((t, D_FF), BF16)],
        compiler_params=_params(("parallel", "parallel")),
    )(u, w_gu, w_gu)


def _ffn_down_bwd_act(df, w_down, h, name):
    t = df.shape[0]
    tm = _pick(t, (512, 256, 128))
    tn = _FFN_TN

    def body(df_ref, wd_ref, h_ref, dh_ref):
        dp = lax.dot_general(df_ref[...], wd_ref[...], _DIMS["nt"], preferred_element_type=F32)
        a = h_ref[0].astype(F32)
        g = h_ref[1].astype(F32)
        dh_ref[0] = (dp * g * _dsilu(a)).astype(BF16)
        dh_ref[1] = (dp * _silu(a)).astype(BF16)

    blk = pl.BlockSpec((2, tm, tn), lambda j, i: (0, i, j))
    return pl.pallas_call(
        body, name=name, grid=(2, t // tm),
        in_specs=[pl.BlockSpec((tm, D_MODEL), lambda j, i: (i, 0)),
                  pl.BlockSpec((tn, D_MODEL), lambda j, i: (j, 0)), blk],
        out_specs=blk, out_shape=jax.ShapeDtypeStruct((2, t, D_FF), BF16),
        compiler_params=_params(("parallel", "parallel")),
    )(df, w_down, h)


def _res_ln_fwd(x, f, mod, ln_g, ln_b, bsz, seq, sub, weight, target=None):
    ts = _pick(seq, (256, 128))
    grid, ns = _row_grid(bsz, seq, ts)
    last = target is not None

    def body(*refs):
        x_ref, f_ref, mod_ref, g_ref, b_ref = refs[:5]
        gate = mod_ref[3 * sub + 2:3 * sub + 3, :]
        r = ALPHA * x_ref[...] + gate * (weight * f_ref[...])
        xhat, _ = _ln_stats(r)
        xo = xhat * g_ref[...] + b_ref[...]
        if last:
            t_ref, r_ref, dy_ref, loss_ref = refs[5:]
            diff = xo - t_ref[...]
            dy_ref[...] = diff * (1.0 / D_MODEL)
            part = 0.5 * jnp.sum(jnp.mean(diff * diff, axis=-1, keepdims=True), axis=0, keepdims=True)

            @pl.when((pl.program_id(0) == 0) & (pl.program_id(1) == 0))
            def _():
                loss_ref[...] = jnp.zeros_like(loss_ref)

            loss_ref[...] += jnp.broadcast_to(part, loss_ref.shape)
        else:
            r_ref, xo_ref, u_ref = refs[5:]
            xo_ref[...] = xo
            sh = mod_ref[3 * sub + 3:3 * sub + 4, :]
            sc = mod_ref[3 * sub + 4:3 * sub + 5, :]
            u_ref[...] = (xo * (1.0 + sc) + sh).astype(BF16)
        r_ref[...] = r

    row = _rows(ts, D_MODEL, ns)
    in_specs = [row, row, _mod_spec(), _vec_spec(), _vec_spec()]
    args = [x, f, mod, ln_g, ln_b]
    if last:
        in_specs.append(row)
        args.append(target)
        out_specs = [row, row, _vec_spec(8, LANES)]
        out_shape = [jax.ShapeDtypeStruct(x.shape, F32), jax.ShapeDtypeStruct(x.shape, F32),
                     jax.ShapeDtypeStruct((8, LANES), F32)]
        sem = ("arbitrary", "arbitrary")
    else:
        out_specs = [row, row, row]
        out_shape = [jax.ShapeDtypeStruct(x.shape, F32), jax.ShapeDtypeStruct(x.shape, F32),
                     jax.ShapeDtypeStruct(x.shape, BF16)]
        sem = ("parallel", "parallel")
    return pl.pallas_call(
        body, name=f"res_ln_fwd{sub}", grid=grid, in_specs=in_specs, out_specs=out_specs, out_shape=out_shape,
        compiler_params=_params(sem),
    )(*args)


def _res_ln_bwd(r, dxo, f, mod, ln_g, bsz, seq, sub, weight):
    ts = _pick(seq, (256, 128))
    grid, ns = _row_grid(bsz, seq, ts)

    def body(r_ref, dxo_ref, f_ref, mod_ref, g_ref, dxres_ref, df_ref, lns_ref, gs_ref):
        b, s = pl.program_id(0), pl.program_id(1)
        gate = mod_ref[3 * sub + 2:3 * sub + 3, :]
        xhat, rstd = _ln_stats(r_ref[...])
        dxo_v = dxo_ref[...]
        dxhat = dxo_v * g_ref[...]
        m1 = jnp.mean(dxhat, axis=-1, keepdims=True)
        m2 = jnp.mean(dxhat * xhat, axis=-1, keepdims=True)
        dr = rstd * (dxhat - m1 - xhat * m2)
        dxres_ref[...] = ALPHA * dr
        df_ref[...] = (dr * (gate * weight)).astype(BF16)

        @pl.when((b == 0) & (s == 0))
        def _():
            lns_ref[...] = jnp.zeros_like(lns_ref)

        @pl.when(s == 0)
        def _():
            gs_ref[...] = jnp.zeros_like(gs_ref)

        lns_ref[0:1, :] += _colsum(dxo_v * xhat)
        lns_ref[1:2, :] += _colsum(dxo_v)
        gs_ref[0:1, :] += _colsum(dr * (weight * f_ref[...]))

    row = _rows(ts, D_MODEL, ns)
    return pl.pallas_call(
        body, name=f"res_ln_bwd{sub}", grid=grid,
        in_specs=[row, row, row, _mod_spec(), _vec_spec()],
        out_specs=[row, row, _vec_spec(8), pl.BlockSpec((None, 8, D_MODEL), lambda b, s: (b, 0, 0))],
        out_shape=[jax.ShapeDtypeStruct(r.shape, F32), jax.ShapeDtypeStruct(r.shape, BF16),
                   jax.ShapeDtypeStruct((8, D_MODEL), F32), jax.ShapeDtypeStruct((bsz, 8, D_MODEL), F32)],
        compiler_params=_params(("arbitrary", "arbitrary")),
    )(r, dxo, f, mod, ln_g)


def _mod_bwd(dxres, du, x, mod, bsz, seq, sub):
    ts = _pick(seq, (256, 128))
    grid, ns = _row_grid(bsz, seq, ts)

    def body(dxres_ref, du_ref, x_ref, mod_ref, dx_ref, st_ref):
        s = pl.program_id(1)
        sc = mod_ref[3 * sub + 1:3 * sub + 2, :]
        du_v = du_ref[...]
        dx_ref[...] = dxres_ref[...] + du_v * (1.0 + sc)

        @pl.when(s == 0)
        def _():
            st_ref[...] = jnp.zeros_like(st_ref)

        st_ref[0:1, :] += _colsum(du_v)
        st_ref[1:2, :] += _colsum(du_v * x_ref[...])

    row = _rows(ts, D_MODEL, ns)
    return pl.pallas_call(
        body, name=f"mod_bwd{sub}", grid=grid, in_specs=[row, row, row, _mod_spec()],
        out_specs=[row, pl.BlockSpec((None, 8, D_MODEL), lambda b, s: (b, 0, 0))],
        out_shape=[jax.ShapeDtypeStruct(x.shape, F32), jax.ShapeDtypeStruct((bsz, 8, D_MODEL), F32)],
        compiler_params=_params(("parallel", "arbitrary")),
    )(dxres, du, x, mod)


_COL_GLU_A, _COL_GLU_B, _COL_GATE_A, _COL_GATE_B = 3, 4, 5, 6


def _merge_fwd(proj, ya, cs, w_sb, w_co):
    t = ya.shape[0]
    tm = _pick(t, (512, 256, 128))

    def body(ga_ref, gb_ref, ya_ref, cs_ref, wsb_ref, wco_ref, m_ref, ysb_ref, yc_ref):
        ysb = jnp.dot(ya_ref[...], wsb_ref[...], preferred_element_type=F32)
        yc = jnp.dot(cs_ref[...], wco_ref[...], preferred_element_type=F32)
        ysb_ref[...] = ysb
        yc_ref[...] = yc
        m_ref[...] = (_sigmoid(ga_ref[...]) * ysb + _sigmoid(gb_ref[...]) * yc).astype(BF16)

    row = pl.BlockSpec((tm, D_MODEL), lambda i: (i, 0))
    full = pl.BlockSpec((D_MODEL, D_MODEL), lambda i: (0, 0))
    return pl.pallas_call(
        body, name="merge_fwd", grid=(t // tm,),
        in_specs=[pl.BlockSpec((tm, D_MODEL), lambda i: (i, _COL_GATE_A)),
                  pl.BlockSpec((tm, D_MODEL), lambda i: (i, _COL_GATE_B)), row, row, full, full],
        out_specs=[row, row, row],
        out_shape=[jax.ShapeDtypeStruct((t, D_MODEL), BF16), jax.ShapeDtypeStruct((t, D_MODEL), F32),
                   jax.ShapeDtypeStruct((t, D_MODEL), F32)],
        compiler_params=_params(("parallel",)),
    )(proj, proj, ya, cs, w_sb, w_co)


def _merge_bwd(proj, ysb, yconv, do2, w_out):
    t = ysb.shape[0]
    tm = _pick(t, (512, 256, 128))

    def body(ga_ref, gb_ref, ysb_ref, yc_ref, do_ref, w_ref, dysb_ref, dyc_ref, dg_ref):
        dm = lax.dot_general(do_ref[...], w_ref[...], _DIMS["nt"], preferred_element_type=F32)
        sa = _sigmoid(ga_ref[...])
        sb = _sigmoid(gb_ref[...])
        dysb_ref[...] = (dm * sa).astype(BF16)
        dyc_ref[...] = (dm * sb).astype(BF16)
        dg_ref[:, :D_MODEL] = (dm * ysb_ref[...] * sa * (1.0 - sa)).astype(BF16)
        dg_ref[:, D_MODEL:] = (dm * yc_ref[...] * sb * (1.0 - sb)).astype(BF16)

    row = pl.BlockSpec((tm, D_MODEL), lambda i: (i, 0))
    return pl.pallas_call(
        body, name="merge_bwd", grid=(t // tm,),
        in_specs=[pl.BlockSpec((tm, D_MODEL), lambda i: (i, _COL_GATE_A)),
                  pl.BlockSpec((tm, D_MODEL), lambda i: (i, _COL_GATE_B)), row, row, row,
                  pl.BlockSpec((D_MODEL, D_MODEL), lambda i: (0, 0))],
        out_specs=[row, row, pl.BlockSpec((tm, 2 * D_MODEL), lambda i: (i, 0))],
        out_shape=[jax.ShapeDtypeStruct((t, D_MODEL), BF16), jax.ShapeDtypeStruct((t, D_MODEL), BF16),
                   jax.ShapeDtypeStruct((t, 2 * D_MODEL), BF16)],
        compiler_params=_params(("parallel",)),
    )(proj, proj, ysb, yconv, do2, w_out)


_CONV_ROWS = 128


def _halo_prev(tt, ns, col):
    r = tt // HALO
    return pl.BlockSpec((HALO, D_MODEL), lambda b, s: (jnp.maximum((b * ns + s) * r - 1, 0), col))


def _halo_next(tt, ns, nblk, col):
    r = tt // HALO
    return pl.BlockSpec((HALO, D_MODEL), lambda b, s: (jnp.minimum((b * ns + s + 1) * r, nblk - 1), col))


def _windows(pad_ref, stage_ref, tt, offsets):
    for r in range(8):
        mine = [o for o in offsets if o % 8 == r]
        if not mine:
            continue
        n = max(mine) - r + tt
        stage_ref[0:n, :] = pad_ref[r:r + n, :]
        for o in mine:
            yield o, stage_ref[o - r:o - r + tt, :]


def _fill_hc(hpad, a_ref, b_ref, ha_ref, hb_ref, s):
    halo = ha_ref[...] * _sigmoid(hb_ref[...])
    hpad[0:HALO, :] = jnp.where(s > 0, halo, 0.0)
    hpad[HALO:, :] = a_ref[...] * _sigmoid(b_ref[...])


def _conv_fwd(proj, conv_w, conv_b, ln_g, ln_b, bsz, seq):
    tt = _CONV_ROWS
    grid, ns = _row_grid(bsz, seq, tt)
    off = HALO - (CONV_TAPS - 1)

    def body(a_ref, b_ref, ha_ref, hb_ref, w_ref, cb_ref, g_ref, bb_ref, cs_ref, cv_ref, hpad, stage):
        _fill_hc(hpad, a_ref, b_ref, ha_ref, hb_ref, pl.program_id(1))
        acc = jnp.zeros((tt, D_MODEL), F32)
        for o, win in _windows(hpad, stage, tt, [off + j for j in range(CONV_TAPS)]):
            acc = acc + w_ref[o - off:o - off + 1, :] * win
        cv = acc + cb_ref[...]
        cv_ref[...] = cv
        xhat, _ = _ln_stats(cv)
        cs_ref[...] = _silu(xhat * g_ref[...] + bb_ref[...]).astype(BF16)

    row = _rows(tt, D_MODEL, ns)
    t = proj.shape[0]
    return pl.pallas_call(
        body, name="conv_fwd", grid=grid,
        in_specs=[_rows(tt, D_MODEL, ns, _COL_GLU_A), _rows(tt, D_MODEL, ns, _COL_GLU_B),
                  _halo_prev(tt, ns, _COL_GLU_A), _halo_prev(tt, ns, _COL_GLU_B),
                  _vec_spec(32), _vec_spec(), _vec_spec(), _vec_spec()],
        out_specs=[row, row],
        out_shape=[jax.ShapeDtypeStruct((t, D_MODEL), BF16), jax.ShapeDtypeStruct((t, D_MODEL), F32)],
        scratch_shapes=[pltpu.VMEM((HALO + tt, D_MODEL), F32), pltpu.VMEM((HALO + tt, D_MODEL), F32)],
        compiler_params=_params(("parallel", "parallel")),
    )(proj, proj, proj, proj, conv_w, conv_b, ln_g, ln_b)


def _conv_bwd_ln(dcs, cv, ln_g, ln_b, bsz, seq):
    ts = _pick(seq, (256, 128))
    grid, ns = _row_grid(bsz, seq, ts)

    def body(dcs_ref, cv_ref, g_ref, b_ref, dcv_ref, st_ref):
        xhat, rstd = _ln_stats(cv_ref[...])
        cl = xhat * g_ref[...] + b_ref[...]
        dcl = dcs_ref[...] * _dsilu(cl)
        dxhat = dcl * g_ref[...]
        m1 = jnp.mean(dxhat, axis=-1, keepdims=True)
        m2 = jnp.mean(dxhat * xhat, axis=-1, keepdims=True)
        dcv = rstd * (dxhat - m1 - xhat * m2)
        dcv_ref[...] = dcv

        @pl.when((pl.program_id(0) == 0) & (pl.program_id(1) == 0))
        def _():
            st_ref[...] = jnp.zeros_like(st_ref)

        st_ref[0:1, :] += _colsum(dcl * xhat)
        st_ref[1:2, :] += _colsum(dcl)
        st_ref[2:3, :] += _colsum(dcv)

    row = _rows(ts, D_MODEL, ns)
    return pl.pallas_call(
        body, name="conv_bwd_ln", grid=grid, in_specs=[row, row, _vec_spec(), _vec_spec()],
        out_specs=[row, _vec_spec(8)],
        out_shape=[jax.ShapeDtypeStruct(cv.shape, F32), jax.ShapeDtypeStruct((8, D_MODEL), F32)],
        compiler_params=_params(("arbitrary", "arbitrary")),
    )(dcs, cv, ln_g, ln_b)


def _conv_bwd_taps(proj, dcv, conv_w, bsz, seq):
    tt = _CONV_ROWS
    grid, ns = _row_grid(bsz, seq, tt)
    off = HALO - (CONV_TAPS - 1)
    nblk = proj.shape[0] // HALO

    def body(a_ref, b_ref, ha_ref, hb_ref, d_ref, dn_ref, w_ref, dglu_ref, dw_ref, hpad, dpad, stage):
        s = pl.program_id(1)
        _fill_hc(hpad, a_ref, b_ref, ha_ref, hb_ref, s)
        dcv = d_ref[...]
        dpad[0:tt, :] = dcv
        dpad[tt:, :] = jnp.where(s < ns - 1, dn_ref[...], 0.0)

        @pl.when((pl.program_id(0) == 0) & (s == 0))
        def _():
            dw_ref[...] = jnp.zeros_like(dw_ref)

        dhc = jnp.zeros((tt, D_MODEL), F32)
        for o, win in _windows(dpad, stage, tt, list(range(CONV_TAPS))):
            j = CONV_TAPS - 1 - o
            dhc = dhc + w_ref[j:j + 1, :] * win
        for o, win in _windows(hpad, stage, tt, [off + j for j in range(CONV_TAPS)]):
            dw_ref[o - off:o - off + 1, :] += _colsum(dcv * win)
        sb = _sigmoid(b_ref[...])
        dglu_ref[:, :D_MODEL] = (dhc * sb).astype(BF16)
        dglu_ref[:, D_MODEL:] = (dhc * a_ref[...] * sb * (1.0 - sb)).astype(BF16)

    t = proj.shape[0]
    return pl.pallas_call(
        body, name="conv_bwd_taps", grid=grid,
        in_specs=[_rows(tt, D_MODEL, ns, _COL_GLU_A), _rows(tt, D_MODEL, ns, _COL_GLU_B),
                  _halo_prev(tt, ns, _COL_GLU_A), _halo_prev(tt, ns, _COL_GLU_B),
                  _rows(tt, D_MODEL, ns), _halo_next(tt, ns, nblk, 0), _vec_spec(32)],
        out_specs=[_rows(tt, 2 * D_MODEL, ns), _vec_spec(32)],
        out_shape=[jax.ShapeDtypeStruct((t, 2 * D_MODEL), BF16), jax.ShapeDtypeStruct((32, D_MODEL), F32)],
        scratch_shapes=[pltpu.VMEM((HALO + tt, D_MODEL), F32), pltpu.VMEM((tt + HALO, D_MODEL), F32),
                        pltpu.VMEM((HALO + tt, D_MODEL), F32)],
        compiler_params=_params(("arbitrary", "arbitrary")),
    )(proj, proj, proj, proj, dcv, dcv, conv_w)


_NT = (((1,), (1,)), ((), ()))
_TN = (((0,), (0,)), ((), ()))


def _dot(a, b, dims=None):
    if dims is None:
        return jnp.dot(a, b, preferred_element_type=F32)
    return lax.dot_general(a, b, dims, preferred_element_type=F32)


def _tri_dot(v, tri2):
    hi = v.astype(BF16)
    lo = (v - hi.astype(F32)).astype(BF16)
    return _dot(jnp.concatenate([hi, lo], axis=1), tri2)


def _tri2(mask):
    t = mask.astype(BF16)
    return jnp.concatenate([t, t], axis=0)


def _softplus_parts(z):
    t = jnp.exp(-jnp.abs(z))
    den = 1.0 + t
    return jnp.maximum(z, 0.0) + jnp.log(den), t, den


def _attn_fwd(proj, bsz, seq):
    blk = ATT_BLOCK
    nq = seq // blk
    n_pairs = D_MODEL // LANES

    def body(q_ref, k_ref, v_ref, y_ref, rt_ref, zr_buf, ns_buf, run_buf, acc_buf):
        qi = pl.program_id(2)
        lane = lax.broadcasted_iota(jnp.int32, (blk, LANES), 1)
        first = lane < HEAD_DIM
        q2 = q_ref[...] * 0.125
        q_heads = (jnp.where(first, q2, 0.0).astype(BF16), jnp.where(first, 0.0, q2).astype(BF16))
        rr = lax.broadcasted_iota(jnp.int32, (blk, blk), 0)
        cc = lax.broadcasted_iota(jnp.int32, (blk, blk), 1)
        tri_ge = _tri2(rr >= cc)
        causal = cc < rr

        def scores(kb, slot, masked, heads=(0, 1)):
            k_blk = k_ref[pl.ds(pl.multiple_of(kb * blk, blk), blk), :].astype(BF16)
            for h in heads:
                z = _dot(q_heads[h], k_blk, _NT)
                if masked:
                    z = jnp.where(causal, z, -1e30)
                sp, _, _ = _softplus_parts(z)
                neg = -sp
                zr_buf[slot, h] = z + _tri_dot(neg, tri_ge)
                ns_buf[slot, h] = jnp.sum(neg, axis=1, keepdims=True)

        def weigh(kb, slot, heads=(0, 1)):
            v_blk = v_ref[pl.ds(pl.multiple_of(kb * blk, blk), blk), :].astype(BF16)
            for h in heads:
                run = run_buf[h]
                w = jnp.exp(zr_buf[slot, h] + run)
                acc_buf[h] += _dot(w.astype(BF16), v_blk)
                run_buf[h] = run + ns_buf[slot, h]

        def step(kb_next, kb, slot):
            for h in range(2):
                scores(kb_next, 1 - slot, False, (h,))
                weigh(kb, slot, (h,))

        run_buf[...] = jnp.zeros_like(run_buf)
        acc_buf[...] = jnp.zeros_like(acc_buf)
        scores(qi, 0, True)

        def two_steps(p, carry):
            t = 2 * p
            step(qi - t - 1, qi - t, 0)
            step(qi - t - 2, qi - t - 1, 1)
            return carry

        lax.fori_loop(0, qi // 2, two_steps, 0)

        @pl.when(qi % 2 == 1)
        def _():
            step(0, 1, 0)
            weigh(0, 1)

        @pl.when(qi % 2 == 0)
        def _():
            weigh(0, 0)

        y_ref[...] = jnp.where(first, acc_buf[0], acc_buf[1]).astype(BF16)
        rt_ref[...] = jnp.where(first, jnp.broadcast_to(run_buf[0], (blk, LANES)),
                                jnp.broadcast_to(run_buf[1], (blk, LANES)))

    t = proj.shape[0]
    q_spec = pl.BlockSpec((blk, LANES), lambda b, p, i: (b * nq + i, p))
    return pl.pallas_call(
        body, name="attn_fwd", grid=(bsz, n_pairs, nq),
        in_specs=[q_spec,
                  pl.BlockSpec((seq, LANES), lambda b, p, i: (b, n_pairs + p)),
                  pl.BlockSpec((seq, LANES), lambda b, p, i: (b, 2 * n_pairs + p))],
        out_specs=[q_spec, q_spec],
        out_shape=[jax.ShapeDtypeStruct((t, D_MODEL), BF16), jax.ShapeDtypeStruct((t, D_MODEL), F32)],
        scratch_shapes=[pltpu.VMEM((2, 2, blk, blk), F32), pltpu.VMEM((2, 2, blk, 1), F32),
                        pltpu.VMEM((2, blk, 1), F32), pltpu.VMEM((2, blk, LANES), F32)],
        compiler_params=_params(("parallel", "parallel", "arbitrary")),
    )(proj, proj, proj)


def _attn_bwd(proj, rtot, dy, bsz, seq):
    blk = ATT_BLOCK
    nq = seq // blk
    n_pairs = D_MODEL // LANES

    def body(q_ref, k_ref, v_ref, dy_ref, rt_ref, dq_ref, dk_ref, dv_ref, dk_acc, dv_acc,
             a_buf, sig_buf, dw_buf, ns_buf, pre_buf, es_buf, dq_buf):
        qi = pl.program_id(2)

        @pl.when(qi == 0)
        def _():
            dk_acc[...] = jnp.zeros_like(dk_acc)
            dv_acc[...] = jnp.zeros_like(dv_acc)

        lane = lax.broadcasted_iota(jnp.int32, (blk, LANES), 1)
        first = lane < HEAD_DIM
        head_row = lax.broadcasted_iota(jnp.int32, (LANES, blk), 0) < HEAD_DIM
        q2 = q_ref[...] * 0.125
        q_rows = (jnp.where(first, q2, 0.0).astype(BF16), jnp.where(first, 0.0, q2).astype(BF16))
        q_t = q2.T
        q_heads = (jnp.where(head_row, q_t, 0.0).astype(BF16), jnp.where(head_row, 0.0, q_t).astype(BF16))
        dy2 = dy_ref[...].astype(F32)
        dy_rows = (jnp.where(first, dy2, 0.0).astype(BF16), jnp.where(first, 0.0, dy2).astype(BF16))
        dy_t = dy2.T
        dy_heads = (jnp.where(head_row, dy_t, 0.0).astype(BF16), jnp.where(head_row, 0.0, dy_t).astype(BF16))
        rt_t = rt_ref[...].T
        rt = (rt_t[0:1, :], rt_t[HEAD_DIM:HEAD_DIM + 1, :])
        rr = lax.broadcasted_iota(jnp.int32, (blk, blk), 0)
        cc = lax.broadcasted_iota(jnp.int32, (blk, blk), 1)
        lower = (cc < rr).astype(BF16)
        lower_eq = (cc <= rr).astype(BF16)
        tri_lt = jnp.concatenate([lower, lower], axis=1)
        tri_le = jnp.concatenate([lower_eq, lower_eq], axis=1)
        causal = rr < cc

        def tri_left(tri2, v):
            hi = v.astype(BF16)
            lo = (v - hi.astype(F32)).astype(BF16)
            return _dot(tri2, jnp.concatenate([hi, lo], axis=0))

        def scores(kb, slot, heads=(0, 1)):
            rows = pl.ds(pl.multiple_of(kb * blk, blk), blk)
            k_blk = k_ref[rows, :].astype(BF16)
            v_blk = v_ref[rows, :].astype(BF16)
            keep = jnp.logical_or(causal, kb < qi)
            for h in heads:
                z = jnp.where(keep, _dot(k_blk, q_heads[h]), -1e30)
                sp, t, den = _softplus_parts(z)
                neg = -sp
                a_buf[slot, h] = z - tri_left(tri_lt, neg)
                sig_buf[slot, h] = jnp.where(z >= 0, 1.0, t) / den
                ns_buf[slot, h] = jnp.sum(neg, axis=0, keepdims=True)
                dw_buf[slot, h] = _dot(v_blk, dy_heads[h])

        def finish(kb, slot, heads=(0, 1)):
            rows = pl.ds(pl.multiple_of(kb * blk, blk), blk)
            k_t = k_ref[rows, :].T.astype(BF16)
            for h in heads:
                pre, esum = pre_buf[h], es_buf[h]
                w = jnp.exp(a_buf[slot, h] + (rt[h] - pre))
                e = dw_buf[slot, h] * w
                dz = e - sig_buf[slot, h] * (esum + tri_left(tri_le, e))
                pre_buf[h] = pre + ns_buf[slot, h]
                es_buf[h] = esum + jnp.sum(e, axis=0, keepdims=True)
                dzb = dz.astype(BF16)
                dq_buf[h] += _dot(k_t, dzb)
                dk_acc[rows, :] += _dot(dzb, q_rows[h])
                dv_acc[rows, :] += _dot(w.astype(BF16), dy_rows[h])

        def step(kb_next, kb, slot):
            scores(kb_next, 1 - slot)
            finish(kb, slot)

        pre_buf[...] = jnp.zeros_like(pre_buf)
        es_buf[...] = jnp.zeros_like(es_buf)
        dq_buf[...] = jnp.zeros_like(dq_buf)
        scores(0, 0)

        def two_steps(p, carry):
            t = 2 * p
            step(t + 1, t, 0)
            step(t + 2, t + 1, 1)
            return carry

        lax.fori_loop(0, qi // 2, two_steps, 0)

        @pl.when(qi % 2 == 1)
        def _():
            step(qi, qi - 1, 0)
            finish(qi, 1)

        @pl.when(qi % 2 == 0)
        def _():
            finish(qi, 0)

        dq_ref[...] = (jnp.where(head_row, dq_buf[0], dq_buf[1]).T * 0.125).astype(BF16)

        @pl.when(qi == nq - 1)
        def _():
            dk_ref[...] = dk_acc[...].astype(BF16)
            dv_ref[...] = dv_acc[...].astype(BF16)

    t = proj.shape[0]
    q_spec = pl.BlockSpec((blk, LANES), lambda b, p, i: (b * nq + i, p))
    kv_out = pl.BlockSpec((seq, LANES), lambda b, p, i: (b, p))
    out = jax.ShapeDtypeStruct((t, D_MODEL), BF16)
    return pl.pallas_call(
        body, name="attn_bwd", grid=(bsz, n_pairs, nq),
        in_specs=[q_spec,
                  pl.BlockSpec((seq, LANES), lambda b, p, i: (b, n_pairs + p)),
                  pl.BlockSpec((seq, LANES), lambda b, p, i: (b, 2 * n_pairs + p)),
                  q_spec, q_spec],
        out_specs=[q_spec, kv_out, kv_out], out_shape=[out, out, out],
        scratch_shapes=[pltpu.VMEM((seq, LANES), F32), pltpu.VMEM((seq, LANES), F32),
                        pltpu.VMEM((2, 2, blk, blk), F32), pltpu.VMEM((2, 2, blk, blk), F32),
                        pltpu.VMEM((2, 2, blk, blk), F32), pltpu.VMEM((2, 2, 1, blk), F32),
                        pltpu.VMEM((2, 1, blk), F32), pltpu.VMEM((2, 1, blk), F32),
                        pltpu.VMEM((2, LANES, blk), F32)],
        compiler_params=_params(("parallel", "parallel", "arbitrary")),
    )(proj, proj, proj, dy, rtot)


def _adamw(w, g, m, v, name, after):
    rows, cols = w.shape
    tr = _pick(rows, (256, 352, 128, 64, 32, 16, 8))
    c1 = 1.0 - ADAM_B1 ** ADAM_STEP
    c2 = 1.0 - ADAM_B2 ** ADAM_STEP

    def body(w_ref, g_ref, m_ref, v_ref, after_ref, d_ref, nm_ref, nv_ref):
        g_v = g_ref[...]
        nm = ADAM_B1 * m_ref[...] + (1.0 - ADAM_B1) * g_v
        nv = ADAM_B2 * v_ref[...] + (1.0 - ADAM_B2) * (g_v * g_v)
        nm_ref[...] = nm
        nv_ref[...] = nv
        d_ref[...] = -ADAM_LR * ((nm / c1) / (jnp.sqrt(nv / c2) + ADAM_EPS) + ADAM_WD * w_ref[...])

    spec = pl.BlockSpec((tr, cols), lambda i: (i, 0))
    shape = jax.ShapeDtypeStruct(w.shape, F32)
    return pl.pallas_call(
        body, name=name, grid=(rows // tr,), in_specs=[spec] * 4 + [pl.BlockSpec(memory_space=pl.ANY)],
        out_specs=[spec] * 3, out_shape=[shape] * 3,
        compiler_params=_params(("parallel",)),
    )(w, g, m, v, after)


def _ffn_fwd(u, w_gu, w_down, bsz, seq, tag):
    h, p = _ffn_up_act(u, w_gu, f"{tag}_up")
    f = _matmul(p, w_down, mode="nn", out_dtype=F32, name=f"{tag}_down")
    return h, p, f


def _ffn_bwd(df, u, h, p, w_gu, w_down, bsz, seq, tag):
    dh = _ffn_down_bwd_act(df, w_down, h, f"{tag}_ddown")
    g_down = _matmul(p, df, mode="tn", out_dtype=F32, name=f"{tag}_gdown")
    g_gu = _matmul(u, dh, mode="tn", out_dtype=F32, name=f"{tag}_ggu", out_stacked=True)
    du = _matmul(dh, w_gu, mode="nt", out_dtype=F32, name=f"{tag}_dup")
    return du, g_gu, g_down


def _local_step(x, c, target, wts, vecs, fetch=None, early_grads=None):
    wts = dict(wts)
    bsz, seq, _ = x.shape
    t = bsz * seq
    x0 = x.reshape(t, D_MODEL)
    tgt = target.reshape(t, D_MODEL)

    sc = _silu_pad(c)
    mod16 = _matmul(sc, wts["w_ada"], mode="nn", out_dtype=F32, name="ada_fwd", bias=vecs["b_ada"])
    mod = mod16[:bsz].reshape(bsz, 9, D_MODEL)

    u1 = _mod_in(x0, mod, bsz, seq, 0)
    h1, p1 = _ffn_up_act(u1, wts["ffn1_w_gu"], "ffn1_up")
    if fetch is not None:
        wts.update(fetch("down", p1))
    f1 = _matmul(p1, wts["ffn1_w_down"], mode="nn", out_dtype=F32, name="ffn1_down")
    r1, x1, u2 = _res_ln_fwd(x0, f1, mod, vecs["ln1_g"], vecs["ln1_b"], bsz, seq, 0, 0.5)
    if fetch is not None:
        wts.update(fetch("later", r1))

    proj = _matmul(u2, wts["w_in"], mode="nn", out_dtype=F32, name="mix_in")
    ya, rtot = _attn_fwd(proj, bsz, seq)
    cs, cv = _conv_fwd(proj, wts["conv_w"], vecs["conv_b"], vecs["conv_ln_g"], vecs["conv_ln_b"], bsz, seq)
    merged, ysb, yconv = _merge_fwd(proj, ya, cs, wts["w_sb_out"], wts["w_conv_out"])
    o2 = _matmul(merged, wts["w_out"], mode="nn", out_dtype=F32, name="mix_out")
    r2, x2, u3 = _res_ln_fwd(x1, o2, mod, vecs["ln2_g"], vecs["ln2_b"], bsz, seq, 1, 1.0)

    h3, p3, f3 = _ffn_fwd(u3, wts["ffn2_w_gu"], wts["ffn2_w_down"], bsz, seq, "ffn2")
    r3, dy, loss_blk = _res_ln_fwd(x2, f3, mod, vecs["ln3_g"], vecs["ln3_b"], bsz, seq, 2, 0.5, target=tgt)

    grads = {}
    dxres, df, ln3s, g3s = _res_ln_bwd(r3, dy, f3, mod, vecs["ln3_g"], bsz, seq, 2, 0.5)
    du, grads["ffn2_w_gu"], grads["ffn2_w_down"] = _ffn_bwd(
        df, u3, h3, p3, wts["ffn2_w_gu"], wts["ffn2_w_down"], bsz, seq, "ffn2")
    dx2, m3s = _mod_bwd(dxres, du, x2, mod, bsz, seq, 2)

    dxres, do2, ln2s, g2s = _res_ln_bwd(r2, dx2, o2, mod, vecs["ln2_g"], bsz, seq, 1, 1.0)
    grads["w_out"] = _matmul(merged, do2, mode="tn", out_dtype=F32, name="mix_out_g")
    dysb, dyconv, dgate = _merge_bwd(proj, ysb, yconv, do2, wts["w_out"])
    dya = _matmul(dysb, wts["w_sb_out"], mode="nt", out_dtype=BF16, name="sb_out_d")
    grads["w_sb_out"] = _matmul(ya, dysb, mode="tn", out_dtype=F32, name="sb_out_g")
    dcs = _matmul(dyconv, wts["w_conv_out"], mode="nt", out_dtype=F32, name="conv_out_d")
    grads["w_conv_out"] = _matmul(cs, dyconv, mode="tn", out_dtype=F32, name="conv_out_g")
    dcv, convs = _conv_bwd_ln(dcs, cv, vecs["conv_ln_g"], vecs["conv_ln_b"], bsz, seq)
    dglu, g_conv_w = _conv_bwd_taps(proj, dcv, wts["conv_w"], bsz, seq)
    dq, dk, dv = _attn_bwd(proj, rtot, dya, bsz, seq)
    dproj = jnp.concatenate([dq, dk, dv, dglu, dgate], axis=1)
    grads["w_in"] = _matmul(u2, dproj, mode="tn", out_dtype=F32, name="mix_in_g", out_stacked=True)
    if early_grads is not None:
        mod = mod + early_grads("later_start", {n: grads.pop(n) for n in list(grads)})
    du = _matmul(dproj, wts["w_in"], mode="nt", out_dtype=F32, name="mix_in_d")
    dx1, m2s = _mod_bwd(dxres, du, x1, mod, bsz, seq, 1)
    if early_grads is not None:
        mod = mod + early_grads("later_go", dx1)

    dxres, df, ln1s, g1s = _res_ln_bwd(r1, dx1, f1, mod, vecs["ln1_g"], bsz, seq, 0, 0.5)
    dh = _ffn_down_bwd_act(df, wts["ffn1_w_down"], h1, "ffn1_ddown")
    grads["ffn1_w_down"] = _matmul(p1, df, mode="tn", out_dtype=F32, name="ffn1_gdown")
    grads["ffn1_w_gu"] = _matmul(u1, dh, mode="tn", out_dtype=F32, name="ffn1_ggu", out_stacked=True)
    if early_grads is not None:
        mod = mod + early_grads("mid_start", {n: grads.pop(n) for n in list(grads)})
    du = _matmul(dh, wts["ffn1_w_gu"], mode="nt", out_dtype=F32, name="ffn1_dup")
    if early_grads is not None:
        mod = mod + early_grads("mid_go", du)
    grad_x, m1s = _mod_bwd(dxres, du, x0, mod, bsz, seq, 0)

    dmod = jnp.stack([m1s[:, 0], m1s[:, 1], g1s[:, 0], m2s[:, 0], m2s[:, 1], g2s[:, 0],
                      m3s[:, 0], m3s[:, 1], g3s[:, 0]], axis=1)
    dmod16 = jnp.zeros((16, 9 * D_MODEL), F32).at[:bsz].set(dmod.reshape(bsz, 9 * D_MODEL))
    grads["w_ada"] = _matmul(sc, dmod16.astype(BF16), mode="tn", out_dtype=F32, name="ada_g", out_stacked=True)

    small = {"dmod": dmod, "ln1": ln1s, "ln2": ln2s, "ln3": ln3s, "conv": convs, "conv_w": g_conv_w,
             "loss": loss_blk}
    return grad_x.reshape(x.shape), grads, small


_HBM = pl.BlockSpec(memory_space=pltpu.HBM)


def _position():
    return lax.axis_index("x"), lax.axis_index("y"), lax.axis_index("c")


def _other_chips(x, y):
    return [(1 - x, y), (x, 1 - y), (1 - x, 1 - y)]


def _cast_into_stack(w_local, chip, name):
    rows, cols = w_local.shape
    tr = _pick(rows, (256, 352, 128, 64, 32, 16))

    def body(chip_ref, w_ref, o_ref):
        o_ref[...] = w_ref[...].astype(BF16)

    return pl.pallas_call(
        body, name=name,
        grid_spec=pltpu.PrefetchScalarGridSpec(
            num_scalar_prefetch=1, grid=(rows // tr,),
            in_specs=[pl.BlockSpec((tr, cols), lambda r, chip_ref: (r, 0))],
            out_specs=pl.BlockSpec((None, tr, cols), lambda r, chip_ref: (chip_ref[0], r, 0))),
        out_shape=jax.ShapeDtypeStruct((N_CHIPS, rows, cols), BF16),
        compiler_params=_params(("parallel",)),
    )(chip, w_local)


def _all_gather_weights(stacks, small):
    n = len(stacks)

    def body(*refs):
        ins, small_in, outs, small_out = refs[:n], refs[n], refs[n + 1:2 * n + 1], refs[2 * n + 1]
        send_sems, recv_sems, fwd_send_sems, fwd_recv_sems, small_sems = refs[2 * n + 2:]
        x, y, c = _position()
        me = 2 * x + y
        chips = _other_chips(x, y)

        def send(i, j):
            px, py = chips[j]
            return pltpu.make_async_remote_copy(
                src_ref=ins[i].at[me, c], dst_ref=outs[i].at[me, c], send_sem=send_sems.at[3 * i + j],
                recv_sem=recv_sems.at[3 * i + j], device_id=(px, py, c), device_id_type=MESH)

        def landed(i, j):
            px, py = chips[j]
            return pltpu.make_async_remote_copy(
                src_ref=ins[i].at[me, c], dst_ref=outs[i].at[2 * px + py, c], send_sem=send_sems.at[3 * i + j],
                recv_sem=recv_sems.at[3 * i + j], device_id=(px, py, c), device_id_type=MESH)

        def forward(i, j, half):
            px, py = chips[j]
            blk = outs[i].at[2 * px + py, half]
            return pltpu.make_async_remote_copy(
                src_ref=blk, dst_ref=blk, send_sem=fwd_send_sems.at[3 * i + j],
                recv_sem=fwd_recv_sems.at[3 * i + j], device_id=(x, y, 1 - c), device_id_type=MESH)

        def small_copy(j, slot):
            px, py = chips[j]
            return pltpu.make_async_remote_copy(
                src_ref=small_in, dst_ref=small_out.at[slot], send_sem=small_sems.at[j],
                recv_sem=small_sems.at[3 + j], device_id=(px, py, c), device_id_type=MESH)

        own_small = pltpu.make_async_copy(small_in, small_out.at[me], small_sems.at[6])
        own_small.start()
        for j in range(3):
            small_copy(j, me).start()
        for i in range(n):
            for j in range(3):
                send(i, j).start()
        for i in range(n):
            for j in range(3):
                landed(i, j).wait_recv()
                forward(i, j, c).start()
        for i in range(n):
            for j in range(3):
                forward(i, j, 1 - c).wait_recv()
        for j, (px, py) in enumerate(chips):
            small_copy(j, 2 * px + py).wait_recv()
        own_small.wait()
        for j in range(3):
            small_copy(j, me).wait_send()
        for i in range(n):
            for j in range(3):
                send(i, j).wait_send()
                forward(i, j, c).wait_send()

    return pl.pallas_call(
        body, name="all_gather_weights",
        out_shape=[jax.ShapeDtypeStruct(s.shape, s.dtype) for s in stacks]
        + [jax.ShapeDtypeStruct((N_CHIPS,) + small.shape, small.dtype)],
        in_specs=[_HBM] * (n + 1), out_specs=[_HBM] * (n + 1),
        input_output_aliases={i: i for i in range(n)},
        scratch_shapes=[pltpu.SemaphoreType.DMA((3 * n,)), pltpu.SemaphoreType.DMA((3 * n,)),
                        pltpu.SemaphoreType.DMA((3 * n,)), pltpu.SemaphoreType.DMA((3 * n,)),
                        pltpu.SemaphoreType.DMA((7,))],
    )(*stacks, small)


_SEM = pl.BlockSpec(memory_space=pltpu.SEMAPHORE)
_DATAFLOW = pltpu.SideEffectType.DATAFLOW_SIDE_EFFECTING


_COPIES = {"gather": 3, "scatter": 3, "swap": N_CHIPS}


def _exchange_plan(kind, src, land):
    x, y, c = _position()
    me = 2 * x + y
    if kind == "swap":
        return [(src.at[k, 1 - c], land.at[k], land.at[k], (x, y, 1 - c)) for k in range(N_CHIPS)]
    plan = []
    for j, (px, py) in enumerate(_other_chips(x, y)):
        if kind == "gather":
            plan.append((src.at[me, c], land.at[me, c], land.at[2 * px + py, c], (px, py, c)))
        else:
            plan.append((src.at[2 * px + py], land.at[j], land.at[j], (px, py, c)))
    return plan


def _exchange_start(kind, srcs, lands, name, after):
    n = len(srcs)
    per = _COPIES[kind]
    in_place = lands is None
    n_in = n if in_place else 2 * n

    def body(*refs):
        src_refs = refs[:n]
        land_refs = src_refs if in_place else refs[n:2 * n]
        send_sems, recv_sems = refs[n_in + 1], refs[n_in + 2]
        token = refs[-1]
        for i in range(n):
            for j, (src, dst, _, to) in enumerate(_exchange_plan(kind, src_refs[i], land_refs[i])):
                pltpu.make_async_remote_copy(
                    src_ref=src, dst_ref=dst, send_sem=send_sems.at[per * i + j], recv_sem=recv_sems.at[per * i + j],
                    device_id=to, device_id_type=MESH).start()
        token[...] = jnp.zeros_like(token)

    operands = list(srcs) + ([] if in_place else list(lands))
    operands = [pltpu.with_memory_space_constraint(o, pltpu.HBM) for o in operands]
    out = pl.pallas_call(
        body, name=name,
        out_shape=[pltpu.SemaphoreType.DMA((per * n,)), pltpu.SemaphoreType.DMA((per * n,))]
        + [pltpu.HBM(o.shape, o.dtype) for o in operands] + [jax.ShapeDtypeStruct((8, LANES), F32)],
        in_specs=[_HBM] * n_in + [pl.BlockSpec(memory_space=pl.ANY)],
        out_specs=[_SEM, _SEM] + [_HBM] * n_in + [pl.BlockSpec(memory_space=pltpu.VMEM)],
        input_output_aliases={i: 2 + i for i in range(n_in)},
        compiler_params=pltpu.CompilerParams(has_side_effects=_DATAFLOW),
    )(*operands, after)
    return out[0], out[1], list(out[2:2 + n_in]), out[-1]


def _exchange_wait(kind, send_sems, recv_sems, thru, in_place, after, name):
    n_in = len(thru)
    n = n_in if in_place else n_in // 2
    per = _COPIES[kind]

    def body(*refs):
        src_refs = refs[:n]
        land_refs = src_refs if in_place else refs[n:2 * n]
        send_sems, recv_sems = refs[n_in], refs[n_in + 1]
        for i in range(n):
            for j, (src, _, here, to) in enumerate(_exchange_plan(kind, src_refs[i], land_refs[i])):
                copy = pltpu.make_async_remote_copy(
                    src_ref=src, dst_ref=here, send_sem=send_sems.at[per * i + j], recv_sem=recv_sems.at[per * i + j],
                    device_id=to, device_id_type=MESH)
                copy.wait_send()
                copy.wait_recv()

    out = pl.pallas_call(
        body, name=name, out_shape=[pltpu.HBM(o.shape, o.dtype) for o in thru],
        in_specs=[_HBM] * n_in + [_SEM, _SEM, pl.BlockSpec(memory_space=pl.ANY)], out_specs=[_HBM] * n_in,
        input_output_aliases={i: i for i in range(n_in)},
        compiler_params=pltpu.CompilerParams(has_side_effects=_DATAFLOW),
    )(*thru, send_sems, recv_sems, after)
    return list(out[:n]), (list(out[:n]) if in_place else list(out[n:]))


def _gather_forward(stacks, name):
    n = len(stacks)

    def body(*refs):
        ins, outs = refs[:n], refs[n:2 * n]
        send_sems, recv_sems = refs[2 * n:]
        x, y, c = _position()
        chips = _other_chips(x, y)

        def copy(i, j, half):
            px, py = chips[j]
            return pltpu.make_async_remote_copy(
                src_ref=ins[i].at[2 * px + py, half], dst_ref=outs[i].at[2 * px + py, half],
                send_sem=send_sems.at[3 * i + j], recv_sem=recv_sems.at[3 * i + j],
                device_id=(x, y, 1 - c), device_id_type=MESH)

        for i in range(n):
            for j in range(3):
                copy(i, j, c).start()
        for i in range(n):
            for j in range(3):
                copy(i, j, 1 - c).wait_recv()
        for i in range(n):
            for j in range(3):
                copy(i, j, c).wait_send()

    return pl.pallas_call(
        body, name=name, out_shape=[jax.ShapeDtypeStruct(s.shape, s.dtype) for s in stacks],
        in_specs=[_HBM] * n, out_specs=[_HBM] * n, input_output_aliases={i: i for i in range(n)},
        scratch_shapes=[pltpu.SemaphoreType.DMA((3 * n,)), pltpu.SemaphoreType.DMA((3 * n,))],
    )(*stacks)


def _pair_add(g, got, place, name):
    _, _, rh, cols = g.shape
    tr = _pick(rh, (256, 176, 128, 64, 32, 16, 8))

    def body(place_ref, g_ref, got_ref, p_ref, own_ref):
        s = g_ref[...] + got_ref[...]
        p_ref[...] = s.astype(BF16)

        @pl.when(pl.program_id(1) == place_ref[1])
        def _():
            own_ref[...] = s

    blk = pl.BlockSpec((None, tr, cols), lambda r, k, place_ref: (k, r, 0))
    return pl.pallas_call(
        body, name=name,
        grid_spec=pltpu.PrefetchScalarGridSpec(
            num_scalar_prefetch=1, grid=(rh // tr, N_CHIPS),
            in_specs=[pl.BlockSpec((None, None, tr, cols), lambda r, k, place_ref: (k, place_ref[0], r, 0)), blk],
            out_specs=[blk, pl.BlockSpec((tr, cols), lambda r, k, place_ref: (r, 0))]),
        out_shape=[jax.ShapeDtypeStruct((N_CHIPS, rh, cols), BF16), jax.ShapeDtypeStruct((rh, cols), F32)],
        compiler_params=_params(("parallel", "arbitrary")),
    )(place, g, got)


def _chip_sum(own, parts, place, name):
    rh, cols = own.shape
    tr = _pick(rh, (256, 176, 128, 64, 32, 16, 8))

    def body(place_ref, own_ref, p_ref, o_ref):
        o_ref[...] = ((own_ref[...] + p_ref[0].astype(F32)) + p_ref[1].astype(F32)) + p_ref[2].astype(F32)

    return pl.pallas_call(
        body, name=name,
        grid_spec=pltpu.PrefetchScalarGridSpec(
            num_scalar_prefetch=1, grid=(rh // tr,),
            in_specs=[pl.BlockSpec((tr, cols), lambda r, place_ref: (r, 0)),
                      pl.BlockSpec((3, tr, cols), lambda r, place_ref: (0, r, 0))],
            out_specs=pl.BlockSpec((None, tr, cols), lambda r, place_ref: (place_ref[0], r, 0))),
        out_shape=jax.ShapeDtypeStruct((2, rh, cols), F32),
        compiler_params=_params(("parallel",)),
    )(place, own, parts)


def _pair_gather(halves, name):
    n = len(halves)

    def body(*refs):
        ins, outs = refs[:n], refs[n:2 * n]
        send_sems, recv_sems = refs[2 * n:]
        x, y, c = _position()

        def send(i):
            return pltpu.make_async_remote_copy(
                src_ref=ins[i].at[c], dst_ref=outs[i].at[c], send_sem=send_sems.at[i], recv_sem=recv_sems.at[i],
                device_id=(x, y, 1 - c), device_id_type=MESH)

        def landed(i):
            return pltpu.make_async_remote_copy(
                src_ref=ins[i].at[c], dst_ref=outs[i].at[1 - c], send_sem=send_sems.at[i], recv_sem=recv_sems.at[i],
                device_id=(x, y, 1 - c), device_id_type=MESH)

        for i in range(n):
            send(i).start()
        for i in range(n):
            landed(i).wait_recv()
        for i in range(n):
            send(i).wait_send()

    return pl.pallas_call(
        body, name=name,
        out_shape=[jax.ShapeDtypeStruct(h.shape, F32) for h in halves],
        in_specs=[_HBM] * n, out_specs=[_HBM] * n,
        input_output_aliases={i: i for i in range(n)},
        scratch_shapes=[pltpu.SemaphoreType.DMA((n,)), pltpu.SemaphoreType.DMA((n,))],
    )(*halves)


_MOD_ROWS = 16


def _small_all_reduce(buf, bsz, after):
    rows, cols = buf.shape
    head = bsz * _MOD_ROWS
    out_rows = rows - head + _MOD_ROWS

    def body(in_ref, after_ref, o_ref, gath, send_sems, recv_sems):
        x, y, c = _position()
        me = 4 * x + 2 * y + c

        def peer(mask):
            return (x ^ (mask >> 2), y ^ ((mask >> 1) & 1), c ^ (mask & 1))

        def copy(mask):
            return pltpu.make_async_remote_copy(
                src_ref=in_ref, dst_ref=gath.at[me], send_sem=send_sems.at[mask - 1],
                recv_sem=recv_sems.at[mask - 1], device_id=peer(mask), device_id_type=MESH)

        def arrival(mask):
            px, py, pc = peer(mask)
            return pltpu.make_async_remote_copy(
                src_ref=in_ref, dst_ref=gath.at[4 * px + 2 * py + pc], send_sem=send_sems.at[mask - 1],
                recv_sem=recv_sems.at[mask - 1], device_id=peer(mask), device_id_type=MESH)

        for mask in range(1, N_DEV):
            copy(mask).start()
        gath[me] = in_ref[...]
        for mask in range(1, N_DEV):
            arrival(mask).wait_recv()
        for mask in range(1, N_DEV):
            copy(mask).wait_send()
        acc = gath[0]
        for d in range(1, N_DEV):
            acc = acc + gath[d]
        mod = acc[0:_MOD_ROWS]
        for s in range(1, bsz):
            mod = mod + acc[s * _MOD_ROWS:(s + 1) * _MOD_ROWS]
        o_ref[0:_MOD_ROWS, :] = mod
        o_ref[_MOD_ROWS:, :] = acc[head:]

    vm = pl.BlockSpec(memory_space=pltpu.VMEM)
    return pl.pallas_call(
        body, name="small_all_reduce", in_specs=[vm, pl.BlockSpec(memory_space=pl.ANY)], out_specs=vm,
        out_shape=jax.ShapeDtypeStruct((out_rows, cols), F32),
        scratch_shapes=[pltpu.VMEM((N_DEV, rows, cols), F32), pltpu.SemaphoreType.DMA((N_DEV - 1,)),
                        pltpu.SemaphoreType.DMA((N_DEV - 1,))],
        compiler_params=pltpu.CompilerParams(vmem_limit_bytes=VMEM_LIMIT),
    )(buf, after)


_COL_SHARDED = ("w_ada", "ffn1_w_gu", "w_in", "ffn2_w_gu")
_ROW_SHARDED = ("ffn1_w_down", "w_sb_out", "w_conv_out", "w_out", "ffn2_w_down")
_NOW = ["w_ada", "ffn1_w_gu"]
_SOON = ["ffn1_w_down"]
_LATER = ["w_in", "w_sb_out", "w_conv_out", "w_out", "ffn2_w_gu", "ffn2_w_down"]
_VECS = ("b_ada", "ln1_g", "ln1_b", "conv_b", "conv_ln_g", "conv_ln_b", "ln2_g", "ln2_b", "ln3_g", "ln3_b")
_WEIGHTS = ("w_ada", "b_ada", "ffn1_w_gu", "ffn1_w_down", "ln1_g", "ln1_b", "w_in", "w_sb_out", "conv_w", "conv_b",
            "conv_ln_g", "conv_ln_b", "w_conv_out", "w_out", "ln2_g", "ln2_b", "ffn2_w_gu", "ffn2_w_down",
            "ln3_g", "ln3_b")


def _step(x, c, target, w, m, v):
    bsz = x.shape[0]
    chip = 2 * lax.axis_index("x") + lax.axis_index("y")
    core = lax.axis_index("c")

    chip_arr = jnp.reshape(chip, (1,)).astype(jnp.int32)
    place = jnp.stack([core, chip]).astype(jnp.int32)

    def stack_of(n):
        rows, cols = w[n].shape[1:]
        return _cast_into_stack(w[n][0], chip_arr, f"cast_{n}").reshape(N_CHIPS, 2, rows // 2, cols)

    def gathered_form(n, g):
        rows, cols = w[n].shape[1:]
        return g.reshape(N_CHIPS, rows, cols) if n in _COL_SHARDED else g.reshape(N_CHIPS * rows, cols)

    conv_w_local = jnp.pad(w["conv_w"][0], ((0, 1), (0, 0)))
    gathered = _all_gather_weights([stack_of(n) for n in _NOW], conv_w_local)
    wts = {n: gathered_form(n, g) for n, g in zip(_NOW, gathered[:-1])}
    wts["conv_w"] = gathered[-1].transpose(1, 0, 2).reshape(32, D_MODEL)
    pending, behind = {}, gathered[0]
    for stage, names in (("down", _SOON), ("later", _LATER)):
        send, recv, thru, token = _exchange_start(
            "gather", [stack_of(n) for n in names], None, f"gather_start_{stage}", behind)
        pending[stage] = (names, send, recv, thru)
        behind = token
    vecs = {n: w[n] for n in _VECS}
    vecs["b_ada"] = vecs["b_ada"] + behind[0, 0]

    def fetch(stage, after):
        names, send, recv, thru = pending[stage]
        landed, _ = _exchange_wait("gather", send, recv, thru, True, after, f"gather_wait_{stage}")
        forwarded = _gather_forward(landed, f"gather_forward_{stage}")
        return {n: gathered_form(n, g) for n, g in zip(names, forwarded)}

    groups = {"later": _LATER, "mid": ["ffn1_w_gu", "ffn1_w_down"], "last": ["w_ada"]}
    g_out, updates, state = {}, {}, {}

    def adam(names, after):
        for n in names:
            shape = w[n].shape
            flat = shape[1:] if len(shape) == 3 else shape
            d, nm, nv = _adamw(w[n].reshape(flat), g_out[n].reshape(flat), m[n].reshape(flat), v[n].reshape(flat),
                               f"adamw_{n}", after)
            updates[n] = (g_out[n].reshape(shape), d.reshape(shape), nm.reshape(shape), nv.reshape(shape))
            after = nv
        return after

    def swap_start(tag, grads):
        views = [grads[n].reshape(N_CHIPS, 2, w[n].shape[1] // 2, w[n].shape[2]) for n in groups[tag]]
        lands = [lax.empty((N_CHIPS,) + g.shape[2:], F32) for g in views]
        send, recv, thru, token = _exchange_start("swap", views, lands, f"swap_start_{tag}", place)
        state[tag] = {"swap": (send, recv, thru)}
        return token

    def scatter_start(tag, after):
        views, got = _exchange_wait("swap", *state[tag]["swap"], False, after, f"swap_wait_{tag}")
        sums = [_pair_add(g, r, place, f"pair_add_{n}") for n, g, r in zip(groups[tag], views, got)]
        lands = [lax.empty((3,) + p.shape[1:], BF16) for p, _ in sums]
        send, recv, thru, token = _exchange_start(
            "scatter", [p for p, _ in sums], lands, f"scatter_start_{tag}", place)
        state[tag].update(scatter=(send, recv, thru), sums=sums)
        return token

    def collect(tag, after):
        _, parts = _exchange_wait("scatter", *state[tag]["scatter"], False, after, f"scatter_wait_{tag}")
        halves = [_chip_sum(own, p, place, f"chip_sum_{n}")
                  for n, (_, own), p in zip(groups[tag], state[tag]["sums"], parts)]
        for n, f in zip(groups[tag], _pair_gather(halves, f"grad_pair_gather_{tag}")):
            g_out[n] = f.reshape(w[n].shape[1:])
        return g_out[groups[tag][-1]]

    def early_grads(stage, value):
        tag, step = stage.split("_")
        token = swap_start(tag, value) if step == "start" else scatter_start(tag, value)
        return token[0, 0]

    grad_x, grads, small = _local_step(x, c, target, wts, vecs, fetch, early_grads)

    token = swap_start("last", grads)
    done = collect("later", token)
    token = scatter_start("last", done)
    done = adam(groups["later"], token)
    done = collect("mid", done)
    done = adam(groups["mid"], done)
    done = collect("last", done)

    dmod = jnp.pad(small["dmod"], ((0, 0), (0, _MOD_ROWS - 9), (0, 0))).reshape(bsz * _MOD_ROWS, D_MODEL)
    loss_rows = jnp.pad(small["loss"], ((0, 0), (0, D_MODEL - LANES)))
    buf = jnp.concatenate([dmod, small["ln1"], small["ln2"], small["ln3"], small["conv"], small["conv_w"],
                           loss_rows], axis=0)
    red = _small_all_reduce(buf, bsz, done)
    o = _MOD_ROWS
    g_out["b_ada"] = red[0:9].reshape(1, 9 * D_MODEL)
    g_out["ln1_g"], g_out["ln1_b"] = red[o:o + 1], red[o + 1:o + 2]
    g_out["ln2_g"], g_out["ln2_b"] = red[o + 8:o + 9], red[o + 9:o + 10]
    g_out["ln3_g"], g_out["ln3_b"] = red[o + 16:o + 17], red[o + 17:o + 18]
    g_out["conv_ln_g"], g_out["conv_ln_b"], g_out["conv_b"] = red[o + 24:o + 25], red[o + 25:o + 26], red[o + 26:o + 27]
    cw = w["conv_w"].shape[2]
    g_out["conv_w"] = lax.dynamic_slice(red[o + 32:o + 32 + CONV_TAPS], (0, chip * cw), (CONV_TAPS, cw))
    loss = red[o + 64, 0]

    adam(groups["last"] + list(_VECS) + ["conv_w"], place)
    return (loss, grad_x, *[updates[n][k] for k in range(4) for n in _WEIGHTS])


def kernel(x, c, w_ada, b_ada, ffn1_w_gu, ffn1_w_down, ln1_g, ln1_b, w_in, w_sb_out, conv_w, conv_b, conv_ln_g, conv_ln_b, w_conv_out, w_out, ln2_g, ln2_b, ffn2_w_gu, ffn2_w_down, ln3_g, ln3_b, loss_target, m_w_ada, m_b_ada, m_ffn1_w_gu, m_ffn1_w_down, m_ln1_g, m_ln1_b, m_w_in, m_w_sb_out, m_conv_w, m_conv_b, m_conv_ln_g, m_conv_ln_b, m_w_conv_out, m_w_out, m_ln2_g, m_ln2_b, m_ffn2_w_gu, m_ffn2_w_down, m_ln3_g, m_ln3_b, v_w_ada, v_b_ada, v_ffn1_w_gu, v_ffn1_w_down, v_ln1_g, v_ln1_b, v_w_in, v_w_sb_out, v_conv_w, v_conv_b, v_conv_ln_g, v_conv_ln_b, v_w_conv_out, v_w_out, v_ln2_g, v_ln2_b, v_ffn2_w_gu, v_ffn2_w_down, v_ln3_g, v_ln3_b):
    given = dict(locals())
    w = {n: given[n] for n in _WEIGHTS}
    m = {n: given["m_" + n] for n in _WEIGHTS}
    v = {n: given["v_" + n] for n in _WEIGHTS}
    return _step(x, c, loss_target, w, m, v)
```

```python
import functools

import jax
import jax.numpy as jnp
from jax import lax
from jax.experimental import pallas as pl
from jax.experimental.pallas import tpu as pltpu

F32 = jnp.float32
BF16 = jnp.bfloat16

D_MODEL = 1024
D_FF = 2816
HEADS = 16
HEAD_DIM = 64
LANES = 128
CONV_TAPS = 31
HALO = 32
N_CHIPS = 4
N_DEV = 8
ALPHA = 2.0 ** 0.25
LN_EPS = 1e-5
ATT_BLOCK = 256
VMEM_LIMIT = 56 * 1024 * 1024

ADAM_LR = 0.001
ADAM_B1 = 0.9
ADAM_B2 = 0.999
ADAM_EPS = 1e-08
ADAM_WD = 0.01
ADAM_STEP = 10

MESH = pl.DeviceIdType.MESH


def _pick(n, cands):
    for t in cands:
        if t <= n and n % t == 0:
            return t
    return n


def _params(sem):
    return pltpu.CompilerParams(dimension_semantics=sem, vmem_limit_bytes=VMEM_LIMIT)


def _sigmoid(z):
    t = jnp.exp(-jnp.abs(z))
    return jnp.where(z >= 0, 1.0, t) / (1.0 + t)


def _silu(z):
    return z * _sigmoid(z)


def _dsilu(z):
    s = _sigmoid(z)
    return s * (1.0 + z * (1.0 - s))


def _ln_stats(r):
    mu = jnp.mean(r, axis=-1, keepdims=True)
    d = r - mu
    var = jnp.mean(d * d, axis=-1, keepdims=True)
    rstd = lax.rsqrt(var + LN_EPS)
    return d * rstd, rstd


def _colsum(v):
    return jnp.sum(v, axis=0, keepdims=True)


_DIMS = {"nn": (((1,), (0,)), ((), ())), "nt": (((1,), (1,)), ((), ())), "tn": (((0,), (0,)), ((), ()))}
_TN_CANDS = (1408, 1792, 1152, 1024, 512, 256, 128)
_TK_CANDS = (1024, 1408, 896, 512, 256, 128)


def _matmul(a, b, *, mode, out_dtype, name, bias=None, out_stacked=False):
    a_halves = mode == "nt" and a.ndim == 3
    b_halves = mode == "tn" and b.ndim == 3
    b_stacked = b.ndim == 3 and not b_halves
    if mode == "nn":
        m, k = a.shape
        n_c = b.shape[-1]
        n = n_c * (N_CHIPS if b_stacked else 1)
        k_c = k
    elif mode == "nt":
        m = a.shape[-2]
        k = a.shape[-1] * (2 if a_halves else 1)
        n = b.shape[-2]
        k_c = b.shape[-1]
        n_c = n
    else:
        k, m = a.shape
        n = b.shape[-1] * (2 if b_halves else 1)
        n_c = n // N_CHIPS if out_stacked else n
        k_c = k
    if mode == "tn":
        tm = _pick(m, (1024, 1408, 512, 256, 128))
        tk = _pick(k, (512, 256, 128, 64, 32, 16))
    else:
        tm = _pick(m, (1024, 512, 256, 128, 64, 32, 16))
        tk = _pick(k_c, _TK_CANDS)
    tn = _pick(n_c, _TN_CANDS)
    nb = n_c // tn
    kb = k_c // tk
    nk = k // tk
    grid = (m // tm, n // tn, nk)

    if mode == "nn":
        a_spec = pl.BlockSpec((tm, tk), lambda i, j, kk: (i, kk))
        if b_stacked:
            b_spec = pl.BlockSpec((None, tk, tn), lambda i, j, kk: (j // nb, kk, j % nb))
        else:
            b_spec = pl.BlockSpec((tk, tn), lambda i, j, kk: (kk, j))
    elif mode == "nt":
        if a_halves:
            ka = a.shape[-1] // tk
            a_spec = pl.BlockSpec((None, tm, tk), lambda i, j, kk: (kk // ka, i, kk % ka))
        else:
            a_spec = pl.BlockSpec((tm, tk), lambda i, j, kk: (i, kk))
        if b_stacked:
            b_spec = pl.BlockSpec((None, tn, tk), lambda i, j, kk: (kk // kb, j, kk % kb))
        else:
            b_spec = pl.BlockSpec((tn, tk), lambda i, j, kk: (j, kk))
    else:
        a_spec = pl.BlockSpec((tk, tm), lambda i, j, kk: (kk, i))
        if b_halves:
            nh = b.shape[-1] // tn
            b_spec = pl.BlockSpec((None, tk, tn), lambda i, j, kk: (j // nh, kk, j % nh))
        else:
            b_spec = pl.BlockSpec((tk, tn), lambda i, j, kk: (kk, j))
    if out_stacked:
        out_shape = jax.ShapeDtypeStruct((N_CHIPS, m, n_c), out_dtype)
        o_spec = pl.BlockSpec((None, tm, tn), lambda i, j, kk: (j // nb, i, j % nb))
    else:
        out_shape = jax.ShapeDtypeStruct((m, n), out_dtype)
        o_spec = pl.BlockSpec((tm, tn), lambda i, j, kk: (i, j))
    in_specs = [a_spec, b_spec]
    args = [a, b]
    if bias is not None:
        in_specs.append(pl.BlockSpec((1, tn), lambda i, j, kk: (0, j)))
        args.append(bias)
    dims = _DIMS[mode]

    def body(*refs):
        a_ref, b_ref = refs[0], refs[1]
        bias_ref = refs[2] if bias is not None else None

        def write(r):
            if bias_ref is not None:
                r = r + bias_ref[...]
            o_ref[...] = r.astype(o_ref.dtype)

        if nk == 1:
            o_ref = refs[-1]
            write(lax.dot_general(a_ref[...], b_ref[...], dims, preferred_element_type=F32))
            return
        o_ref, acc_ref = refs[-2], refs[-1]
        kk = pl.program_id(2)

        @pl.when(kk == 0)
        def _():
            acc_ref[...] = jnp.zeros_like(acc_ref)

        acc_ref[...] += lax.dot_general(a_ref[...], b_ref[...], dims, preferred_element_type=F32)

        @pl.when(kk == nk - 1)
        def _():
            write(acc_ref[...])

    return pl.pallas_call(
        body, name=name, grid=grid, in_specs=in_specs, out_specs=o_spec, out_shape=out_shape,
        scratch_shapes=[pltpu.VMEM((tm, tn), F32)] if nk > 1 else [],
        compiler_params=_params(("parallel", "parallel", "arbitrary")),
    )(*args)


def _row_grid(bsz, seq, ts):
    ns = seq // ts
    return (bsz, ns), ns


def _rows(ts, width, ns, col=0):
    return pl.BlockSpec((ts, width), lambda b, s: (b * ns + s, col))


def _mod_spec():
    return pl.BlockSpec((None, 9, D_MODEL), lambda b, s: (b, 0, 0))


def _vec_spec(rows=1, width=D_MODEL):
    return pl.BlockSpec((rows, width), lambda b, s: (0, 0))


def _silu_pad(c):
    bsz = c.shape[0]

    def body(c_ref, o_ref):
        o_ref[...] = jnp.zeros_like(o_ref)
        o_ref[0:bsz, :] = _silu(c_ref[...]).astype(BF16)

    return pl.pallas_call(body, name="silu_pad", out_shape=jax.ShapeDtypeStruct((16, D_MODEL), BF16))(c)


def _mod_in(x, mod, bsz, seq, sub):
    ts = _pick(seq, (512, 256, 128))
    grid, ns = _row_grid(bsz, seq, ts)

    def body(x_ref, mod_ref, u_ref):
        sh = mod_ref[3 * sub:3 * sub + 1, :]
        sc = mod_ref[3 * sub + 1:3 * sub + 2, :]
        u_ref[...] = (x_ref[...] * (1.0 + sc) + sh).astype(BF16)

    return pl.pallas_call(
        body, name=f"mod_in{sub}", grid=grid, in_specs=[_rows(ts, D_MODEL, ns), _mod_spec()],
        out_specs=_rows(ts, D_MODEL, ns), out_shape=jax.ShapeDtypeStruct(x.shape, BF16),
        compiler_params=_params(("parallel", "parallel")),
    )(x, mod)


_FFN_TN = D_FF // 2


def _ffn_up_act(u, w_gu, name):
    t = u.shape[0]
    tm = _pick(t, (512, 256, 128))
    tn = _FFN_TN

    def body(u_ref, wa_ref, wg_ref, h_ref, p_ref):
        u_v = u_ref[...]
        a = jnp.dot(u_v, wa_ref[...], preferred_element_type=F32)
        g = jnp.dot(u_v, wg_ref[...], preferred_element_type=F32)
        h_ref[0] = a.astype(BF16)
        h_ref[1] = g.astype(BF16)
        p_ref[...] = (_silu(a) * g).astype(BF16)

    return pl.pallas_call(
        body, name=name, grid=(2, t // tm),
        in_specs=[pl.BlockSpec((tm, D_MODEL), lambda j, i: (i, 0)),
                  pl.BlockSpec((None, D_MODEL, tn), lambda j, i: (j, 0, 0)),
                  pl.BlockSpec((None, D_MODEL, tn), lambda j, i: (j + 2, 0, 0))],
        out_specs=[pl.BlockSpec((2, tm, tn), lambda j, i: (0, i, j)),
                   pl.BlockSpec((tm, tn), lambda j, i: (i, j))],
        out_shape=[jax.ShapeDtypeStruct((2, t, D_FF), BF16), jax.ShapeDtypeStruct((t, D_FF), BF16)],
        compiler_params=_params(("parallel", "parallel")),
    )(u, w_gu, w_gu)


def _ffn_down_bwd_act(df, w_down, h, name):
    t = df.shape[0]
    tm = _pick(t, (512, 256, 128))
    tn = _FFN_TN

    def body(df_ref, wd_ref, h_ref, dh_ref):
        dp = lax.dot_general(df_ref[...], wd_ref[...], _DIMS["nt"], preferred_element_type=F32)
        a = h_ref[0].astype(F32)
        g = h_ref[1].astype(F32)
        dh_ref[0] = (dp * g * _dsilu(a)).astype(BF16)
        dh_ref[1] = (dp * _silu(a)).astype(BF16)

    blk = pl.BlockSpec((2, tm, tn), lambda j, i: (0, i, j))
    return pl.pallas_call(
        body, name=name, grid=(2, t // tm),
        in_specs=[pl.BlockSpec((tm, D_MODEL), lambda j, i: (i, 0)),
                  pl.BlockSpec((tn, D_MODEL), lambda j, i: (j, 0)), blk],
        out_specs=blk, out_shape=jax.ShapeDtypeStruct((2, t, D_FF), BF16),
        compiler_params=_params(("parallel", "parallel")),
    )(df, w_down, h)


def _res_ln_fwd(x, f, mod, ln_g, ln_b, bsz, seq, sub, weight, target=None):
    ts = _pick(seq, (256, 128))
    grid, ns = _row_grid(bsz, seq, ts)
    last = target is not None

    def body(*refs):
        x_ref, f_ref, mod_ref, g_ref, b_ref = refs[:5]
        gate = mod_ref[3 * sub + 2:3 * sub + 3, :]
        r = ALPHA * x_ref[...] + gate * (weight * f_ref[...])
        xhat, _ = _ln_stats(r)
        xo = xhat * g_ref[...] + b_ref[...]
        if last:
            t_ref, r_ref, dy_ref, loss_ref = refs[5:]
            diff = xo - t_ref[...]
            dy_ref[...] = diff * (1.0 / D_MODEL)
            part = 0.5 * jnp.sum(jnp.mean(diff * diff, axis=-1, keepdims=True), axis=0, keepdims=True)

            @pl.when((pl.program_id(0) == 0) & (pl.program_id(1) == 0))
            def _():
                loss_ref[...] = jnp.zeros_like(loss_ref)

            loss_ref[...] += jnp.broadcast_to(part, loss_ref.shape)
        else:
            r_ref, xo_ref, u_ref = refs[5:]
            xo_ref[...] = xo
            sh = mod_ref[3 * sub + 3:3 * sub + 4, :]
            sc = mod_ref[3 * sub + 4:3 * sub + 5, :]
            u_ref[...] = (xo * (1.0 + sc) + sh).astype(BF16)
        r_ref[...] = r

    row = _rows(ts, D_MODEL, ns)
    in_specs = [row, row, _mod_spec(), _vec_spec(), _vec_spec()]
    args = [x, f, mod, ln_g, ln_b]
    if last:
        in_specs.append(row)
        args.append(target)
        out_specs = [row, row, _vec_spec(8, LANES)]
        out_shape = [jax.ShapeDtypeStruct(x.shape, F32), jax.ShapeDtypeStruct(x.shape, F32),
                     jax.ShapeDtypeStruct((8, LANES), F32)]
        sem = ("arbitrary", "arbitrary")
    else:
        out_specs = [row, row, row]
        out_shape = [jax.ShapeDtypeStruct(x.shape, F32), jax.ShapeDtypeStruct(x.shape, F32),
                     jax.ShapeDtypeStruct(x.shape, BF16)]
        sem = ("parallel", "parallel")
    return pl.pallas_call(
        body, name=f"res_ln_fwd{sub}", grid=grid, in_specs=in_specs, out_specs=out_specs, out_shape=out_shape,
        compiler_params=_params(sem),
    )(*args)


def _res_ln_bwd(r, dxo, f, mod, ln_g, bsz, seq, sub, weight, above=None):
    ts = _pick(seq, (256, 128))
    grid, ns = _row_grid(bsz, seq, ts)
    folded = above is not None

    def body(*refs):
        r_ref, dxo_ref, f_ref, mod_ref, g_ref = refs[:5]
        dxres_ref, df_ref, lns_ref, gs_ref = refs[-5:-1] if folded else refs[-4:]
        b, s = pl.program_id(0), pl.program_id(1)
        gate = mod_ref[3 * sub + 2:3 * sub + 3, :]
        xhat, rstd = _ln_stats(r_ref[...])
        dxo_v = dxo_ref[...]
        if folded:
            du_v = refs[5][...]
            ms_ref = refs[-1]
            dxo_v = dxo_v + du_v * (1.0 + mod_ref[3 * sub + 4:3 * sub + 5, :])

            @pl.when(s == 0)
            def _():
                ms_ref[...] = jnp.zeros_like(ms_ref)

            ms_ref[0:1, :] += _colsum(du_v)
            ms_ref[1:2, :] += _colsum(du_v * refs[6][...])
        dxhat = dxo_v * g_ref[...]
        m1 = jnp.mean(dxhat, axis=-1, keepdims=True)
        m2 = jnp.mean(dxhat * xhat, axis=-1, keepdims=True)
        dr = rstd * (dxhat - m1 - xhat * m2)
        dxres_ref[...] = ALPHA * dr
        df_ref[...] = (dr * (gate * weight)).astype(BF16)

        @pl.when((b == 0) & (s == 0))
        def _():
            lns_ref[...] = jnp.zeros_like(lns_ref)

        @pl.when(s == 0)
        def _():
            gs_ref[...] = jnp.zeros_like(gs_ref)

        lns_ref[0:1, :] += _colsum(dxo_v * xhat)
        lns_ref[1:2, :] += _colsum(dxo_v)
        gs_ref[0:1, :] += _colsum(dr * (weight * f_ref[...]))

    row = _rows(ts, D_MODEL, ns)
    per_sample = pl.BlockSpec((None, 8, D_MODEL), lambda b, s: (b, 0, 0))
    stats = jax.ShapeDtypeStruct((bsz, 8, D_MODEL), F32)
    extra = 1 if folded else 0
    return pl.pallas_call(
        body, name=f"res_ln_bwd{sub}", grid=grid,
        in_specs=[row, row, row, _mod_spec(), _vec_spec()] + [row, row] * extra,
        out_specs=[row, row, _vec_spec(8), per_sample] + [per_sample] * extra,
        out_shape=[jax.ShapeDtypeStruct(r.shape, F32), jax.ShapeDtypeStruct(r.shape, BF16),
                   jax.ShapeDtypeStruct((8, D_MODEL), F32), stats] + [stats] * extra,
        compiler_params=_params(("arbitrary", "arbitrary")),
    )(r, dxo, f, mod, ln_g, *(above or ()))


def _mod_bwd(dxres, du, x, mod, bsz, seq, sub):
    ts = _pick(seq, (256, 128))
    grid, ns = _row_grid(bsz, seq, ts)

    def body(dxres_ref, du_ref, x_ref, mod_ref, dx_ref, st_ref):
        s = pl.program_id(1)
        sc = mod_ref[3 * sub + 1:3 * sub + 2, :]
        du_v = du_ref[...]
        dx_ref[...] = dxres_ref[...] + du_v * (1.0 + sc)

        @pl.when(s == 0)
        def _():
            st_ref[...] = jnp.zeros_like(st_ref)

        st_ref[0:1, :] += _colsum(du_v)
        st_ref[1:2, :] += _colsum(du_v * x_ref[...])

    row = _rows(ts, D_MODEL, ns)
    return pl.pallas_call(
        body, name=f"mod_bwd{sub}", grid=grid, in_specs=[row, row, row, _mod_spec()],
        out_specs=[row, pl.BlockSpec((None, 8, D_MODEL), lambda b, s: (b, 0, 0))],
        out_shape=[jax.ShapeDtypeStruct(x.shape, F32), jax.ShapeDtypeStruct((bsz, 8, D_MODEL), F32)],
        compiler_params=_params(("parallel", "arbitrary")),
    )(dxres, du, x, mod)


_COL_GLU_A, _COL_GLU_B, _COL_GATE_A, _COL_GATE_B = 3, 4, 5, 6


def _merge_fwd(proj, ya, cs, w_sb, w_co):
    t = ya.shape[0]
    tm = _pick(t, (512, 256, 128))

    def body(ga_ref, gb_ref, ya_ref, cs_ref, wsb_ref, wco_ref, m_ref, ysb_ref, yc_ref):
        ysb = jnp.dot(ya_ref[...], wsb_ref[...], preferred_element_type=F32)
        yc = jnp.dot(cs_ref[...], wco_ref[...], preferred_element_type=F32)
        ysb_ref[...] = ysb
        yc_ref[...] = yc
        m_ref[...] = (_sigmoid(ga_ref[...]) * ysb + _sigmoid(gb_ref[...]) * yc).astype(BF16)

    row = pl.BlockSpec((tm, D_MODEL), lambda i: (i, 0))
    full = pl.BlockSpec((D_MODEL, D_MODEL), lambda i: (0, 0))
    return pl.pallas_call(
        body, name="merge_fwd", grid=(t // tm,),
        in_specs=[pl.BlockSpec((tm, D_MODEL), lambda i: (i, _COL_GATE_A)),
                  pl.BlockSpec((tm, D_MODEL), lambda i: (i, _COL_GATE_B)), row, row, full, full],
        out_specs=[row, row, row],
        out_shape=[jax.ShapeDtypeStruct((t, D_MODEL), BF16), jax.ShapeDtypeStruct((t, D_MODEL), F32),
                   jax.ShapeDtypeStruct((t, D_MODEL), F32)],
        compiler_params=_params(("parallel",)),
    )(proj, proj, ya, cs, w_sb, w_co)


def _merge_bwd(proj, ysb, yconv, do2, w_out):
    t = ysb.shape[0]
    tm = _pick(t, (512, 256, 128))

    def body(ga_ref, gb_ref, ysb_ref, yc_ref, do_ref, w_ref, dysb_ref, dyc_ref, dg_ref):
        dm = lax.dot_general(do_ref[...], w_ref[...], _DIMS["nt"], preferred_element_type=F32)
        sa = _sigmoid(ga_ref[...])
        sb = _sigmoid(gb_ref[...])
        dysb_ref[...] = (dm * sa).astype(BF16)
        dyc_ref[...] = (dm * sb).astype(BF16)
        dg_ref[:, :D_MODEL] = (dm * ysb_ref[...] * sa * (1.0 - sa)).astype(BF16)
        dg_ref[:, D_MODEL:] = (dm * yc_ref[...] * sb * (1.0 - sb)).astype(BF16)

    row = pl.BlockSpec((tm, D_MODEL), lambda i: (i, 0))
    return pl.pallas_call(
        body, name="merge_bwd", grid=(t // tm,),
        in_specs=[pl.BlockSpec((tm, D_MODEL), lambda i: (i, _COL_GATE_A)),
                  pl.BlockSpec((tm, D_MODEL), lambda i: (i, _COL_GATE_B)), row, row, row,
                  pl.BlockSpec((D_MODEL, D_MODEL), lambda i: (0, 0))],
        out_specs=[row, row, pl.BlockSpec((tm, 2 * D_MODEL), lambda i: (i, 0))],
        out_shape=[jax.ShapeDtypeStruct((t, D_MODEL), BF16), jax.ShapeDtypeStruct((t, D_MODEL), BF16),
                   jax.ShapeDtypeStruct((t, 2 * D_MODEL), BF16)],
        compiler_params=_params(("parallel",)),
    )(proj, proj, ysb, yconv, do2, w_out)


_CONV_ROWS = 128


def _halo_prev(tt, ns, col):
    r = tt // HALO
    return pl.BlockSpec((HALO, D_MODEL), lambda b, s: (jnp.maximum((b * ns + s) * r - 1, 0), col))


def _halo_next(tt, ns, nblk, col):
    r = tt // HALO
    return pl.BlockSpec((HALO, D_MODEL), lambda b, s: (jnp.minimum((b * ns + s + 1) * r, nblk - 1), col))


def _windows(pad_ref, stage_ref, tt, offsets):
    for r in range(8):
        mine = [o for o in offsets if o % 8 == r]
        if not mine:
            continue
        n = max(mine) - r + tt
        stage_ref[0:n, :] = pad_ref[r:r + n, :]
        for o in mine:
            yield o, stage_ref[o - r:o - r + tt, :]


def _fill_hc(hpad, a_ref, b_ref, ha_ref, hb_ref, s):
    halo = ha_ref[...] * _sigmoid(hb_ref[...])
    hpad[0:HALO, :] = jnp.where(s > 0, halo, 0.0)
    hpad[HALO:, :] = a_ref[...] * _sigmoid(b_ref[...])


def _conv_fwd(proj, conv_w, conv_b, ln_g, ln_b, bsz, seq):
    tt = _CONV_ROWS
    grid, ns = _row_grid(bsz, seq, tt)
    off = HALO - (CONV_TAPS - 1)

    def body(a_ref, b_ref, ha_ref, hb_ref, w_ref, cb_ref, g_ref, bb_ref, cs_ref, cv_ref, hpad, stage):
        _fill_hc(hpad, a_ref, b_ref, ha_ref, hb_ref, pl.program_id(1))
        acc = jnp.zeros((tt, D_MODEL), F32)
        for o, win in _windows(hpad, stage, tt, [off + j for j in range(CONV_TAPS)]):
            acc = acc + w_ref[o - off:o - off + 1, :] * win
        cv = acc + cb_ref[...]
        cv_ref[...] = cv
        xhat, _ = _ln_stats(cv)
        cs_ref[...] = _silu(xhat * g_ref[...] + bb_ref[...]).astype(BF16)

    row = _rows(tt, D_MODEL, ns)
    t = proj.shape[0]
    return pl.pallas_call(
        body, name="conv_fwd", grid=grid,
        in_specs=[_rows(tt, D_MODEL, ns, _COL_GLU_A), _rows(tt, D_MODEL, ns, _COL_GLU_B),
                  _halo_prev(tt, ns, _COL_GLU_A), _halo_prev(tt, ns, _COL_GLU_B),
                  _vec_spec(32), _vec_spec(), _vec_spec(), _vec_spec()],
        out_specs=[row, row],
        out_shape=[jax.ShapeDtypeStruct((t, D_MODEL), BF16), jax.ShapeDtypeStruct((t, D_MODEL), F32)],
        scratch_shapes=[pltpu.VMEM((HALO + tt, D_MODEL), F32), pltpu.VMEM((HALO + tt, D_MODEL), F32)],
        compiler_params=_params(("parallel", "parallel")),
    )(proj, proj, proj, proj, conv_w, conv_b, ln_g, ln_b)


def _conv_bwd_ln(dcs, cv, ln_g, ln_b, bsz, seq):
    ts = _pick(seq, (256, 128))
    grid, ns = _row_grid(bsz, seq, ts)

    def body(dcs_ref, cv_ref, g_ref, b_ref, dcv_ref, st_ref):
        xhat, rstd = _ln_stats(cv_ref[...])
        cl = xhat * g_ref[...] + b_ref[...]
        dcl = dcs_ref[...] * _dsilu(cl)
        dxhat = dcl * g_ref[...]
        m1 = jnp.mean(dxhat, axis=-1, keepdims=True)
        m2 = jnp.mean(dxhat * xhat, axis=-1, keepdims=True)
        dcv = rstd * (dxhat - m1 - xhat * m2)
        dcv_ref[...] = dcv

        @pl.when((pl.program_id(0) == 0) & (pl.program_id(1) == 0))
        def _():
            st_ref[...] = jnp.zeros_like(st_ref)

        st_ref[0:1, :] += _colsum(dcl * xhat)
        st_ref[1:2, :] += _colsum(dcl)
        st_ref[2:3, :] += _colsum(dcv)

    row = _rows(ts, D_MODEL, ns)
    return pl.pallas_call(
        body, name="conv_bwd_ln", grid=grid, in_specs=[row, row, _vec_spec(), _vec_spec()],
        out_specs=[row, _vec_spec(8)],
        out_shape=[jax.ShapeDtypeStruct(cv.shape, F32), jax.ShapeDtypeStruct((8, D_MODEL), F32)],
        compiler_params=_params(("arbitrary", "arbitrary")),
    )(dcs, cv, ln_g, ln_b)


def _conv_bwd_taps(proj, dcv, conv_w, bsz, seq):
    tt = _CONV_ROWS
    grid, ns = _row_grid(bsz, seq, tt)
    off = HALO - (CONV_TAPS - 1)
    nblk = proj.shape[0] // HALO

    def body(a_ref, b_ref, ha_ref, hb_ref, d_ref, dn_ref, w_ref, dglu_ref, dw_ref, hpad, dpad, stage):
        s = pl.program_id(1)
        _fill_hc(hpad, a_ref, b_ref, ha_ref, hb_ref, s)
        dcv = d_ref[...]
        dpad[0:tt, :] = dcv
        dpad[tt:, :] = jnp.where(s < ns - 1, dn_ref[...], 0.0)

        @pl.when((pl.program_id(0) == 0) & (s == 0))
        def _():
            dw_ref[...] = jnp.zeros_like(dw_ref)

        dhc = jnp.zeros((tt, D_MODEL), F32)
        for o, win in _windows(dpad, stage, tt, list(range(CONV_TAPS))):
            j = CONV_TAPS - 1 - o
            dhc = dhc + w_ref[j:j + 1, :] * win
        for o, win in _windows(hpad, stage, tt, [off + j for j in range(CONV_TAPS)]):
            dw_ref[o - off:o - off + 1, :] += _colsum(dcv * win)
        sb = _sigmoid(b_ref[...])
        dglu_ref[:, :D_MODEL] = (dhc * sb).astype(BF16)
        dglu_ref[:, D_MODEL:] = (dhc * a_ref[...] * sb * (1.0 - sb)).astype(BF16)

    t = proj.shape[0]
    return pl.pallas_call(
        body, name="conv_bwd_taps", grid=grid,
        in_specs=[_rows(tt, D_MODEL, ns, _COL_GLU_A), _rows(tt, D_MODEL, ns, _COL_GLU_B),
                  _halo_prev(tt, ns, _COL_GLU_A), _halo_prev(tt, ns, _COL_GLU_B),
                  _rows(tt, D_MODEL, ns), _halo_next(tt, ns, nblk, 0), _vec_spec(32)],
        out_specs=[_rows(tt, 2 * D_MODEL, ns), _vec_spec(32)],
        out_shape=[jax.ShapeDtypeStruct((t, 2 * D_MODEL), BF16), jax.ShapeDtypeStruct((32, D_MODEL), F32)],
        scratch_shapes=[pltpu.VMEM((HALO + tt, D_MODEL), F32), pltpu.VMEM((tt + HALO, D_MODEL), F32),
                        pltpu.VMEM((HALO + tt, D_MODEL), F32)],
        compiler_params=_params(("arbitrary", "arbitrary")),
    )(proj, proj, proj, proj, dcv, dcv, conv_w)


_NT = (((1,), (1,)), ((), ()))
_TN = (((0,), (0,)), ((), ()))


def _dot(a, b, dims=None):
    if dims is None:
        return jnp.dot(a, b, preferred_element_type=F32)
    return lax.dot_general(a, b, dims, preferred_element_type=F32)


def _tri_dot(v, tri2):
    hi = v.astype(BF16)
    lo = (v - hi.astype(F32)).astype(BF16)
    return _dot(jnp.concatenate([hi, lo], axis=1), tri2)


def _tri2(mask):
    t = mask.astype(BF16)
    return jnp.concatenate([t, t], axis=0)


def _softplus_parts(z):
    t = jnp.exp(-jnp.abs(z))
    den = 1.0 + t
    return jnp.maximum(z, 0.0) + jnp.log(den), t, den


def _attn_fwd(proj, bsz, seq):
    blk = ATT_BLOCK
    nq = seq // blk
    n_pairs = D_MODEL // LANES

    def body(q_ref, k_ref, v_ref, y_ref, rt_ref, zr_buf, ns_buf, run_buf, acc_buf):
        qi = pl.program_id(2)
        lane = lax.broadcasted_iota(jnp.int32, (blk, LANES), 1)
        first = lane < HEAD_DIM
        q2 = q_ref[...] * 0.125
        q_heads = (jnp.where(first, q2, 0.0).astype(BF16), jnp.where(first, 0.0, q2).astype(BF16))
        rr = lax.broadcasted_iota(jnp.int32, (blk, blk), 0)
        cc = lax.broadcasted_iota(jnp.int32, (blk, blk), 1)
        tri_ge = _tri2(rr >= cc)
        causal = cc < rr

        def scores(kb, slot, masked, heads=(0, 1)):
            k_blk = k_ref[pl.ds(pl.multiple_of(kb * blk, blk), blk), :].astype(BF16)
            for h in heads:
                z = _dot(q_heads[h], k_blk, _NT)
                if masked:
                    z = jnp.where(causal, z, -1e30)
                sp, _, _ = _softplus_parts(z)
                neg = -sp
                zr_buf[slot, h] = z + _tri_dot(neg, tri_ge)
                ns_buf[slot, h] = jnp.sum(neg, axis=1, keepdims=True)

        def weigh(kb, slot, heads=(0, 1)):
            v_blk = v_ref[pl.ds(pl.multiple_of(kb * blk, blk), blk), :].astype(BF16)
            for h in heads:
                run = run_buf[h]
                w = jnp.exp(zr_buf[slot, h] + run)
                acc_buf[h] += _dot(w.astype(BF16), v_blk)
                run_buf[h] = run + ns_buf[slot, h]

        def step(kb_next, kb, slot):
            for h in range(2):
                scores(kb_next, 1 - slot, False, (h,))
                weigh(kb, slot, (h,))

        run_buf[...] = jnp.zeros_like(run_buf)
        acc_buf[...] = jnp.zeros_like(acc_buf)
        scores(qi, 0, True)

        def two_steps(p, carry):
            t = 2 * p
            step(qi - t - 1, qi - t, 0)
            step(qi - t - 2, qi - t - 1, 1)
            return carry

        lax.fori_loop(0, qi // 2, two_steps, 0)

        @pl.when(qi % 2 == 1)
        def _():
            step(0, 1, 0)
            weigh(0, 1)

        @pl.when(qi % 2 == 0)
        def _():
            weigh(0, 0)

        y_ref[...] = jnp.where(first, acc_buf[0], acc_buf[1]).astype(BF16)
        rt_ref[...] = jnp.where(first, jnp.broadcast_to(run_buf[0], (blk, LANES)),
                                jnp.broadcast_to(run_buf[1], (blk, LANES)))

    t = proj.shape[0]
    q_spec = pl.BlockSpec((blk, LANES), lambda b, p, i: (b * nq + i, p))
    return pl.pallas_call(
        body, name="attn_fwd", grid=(bsz, n_pairs, nq),
        in_specs=[q_spec,
                  pl.BlockSpec((seq, LANES), lambda b, p, i: (b, n_pairs + p)),
                  pl.BlockSpec((seq, LANES), lambda b, p, i: (b, 2 * n_pairs + p))],
        out_specs=[q_spec, q_spec],
        out_shape=[jax.ShapeDtypeStruct((t, D_MODEL), BF16), jax.ShapeDtypeStruct((t, D_MODEL), F32)],
        scratch_shapes=[pltpu.VMEM((2, 2, blk, blk), F32), pltpu.VMEM((2, 2, blk, 1), F32),
                        pltpu.VMEM((2, blk, 1), F32), pltpu.VMEM((2, blk, LANES), F32)],
        compiler_params=_params(("parallel", "parallel", "arbitrary")),
    )(proj, proj, proj)


def _attn_bwd(proj, rtot, dy, bsz, seq):
    blk = ATT_BLOCK
    nq = seq // blk
    n_pairs = D_MODEL // LANES

    def body(q_ref, k_ref, v_ref, dy_ref, rt_ref, dq_ref, dk_ref, dv_ref, dk_acc, dv_acc,
             a_buf, sig_buf, dw_buf, ns_buf, pre_buf, es_buf, dq_buf):
        qi = pl.program_id(2)

        @pl.when(qi == 0)
        def _():
            dk_acc[...] = jnp.zeros_like(dk_acc)
            dv_acc[...] = jnp.zeros_like(dv_acc)

        lane = lax.broadcasted_iota(jnp.int32, (blk, LANES), 1)
        first = lane < HEAD_DIM
        head_row = lax.broadcasted_iota(jnp.int32, (LANES, blk), 0) < HEAD_DIM
        q2 = q_ref[...] * 0.125
        q_rows = (jnp.where(first, q2, 0.0).astype(BF16), jnp.where(first, 0.0, q2).astype(BF16))
        q_t = q2.T
        q_heads = (jnp.where(head_row, q_t, 0.0).astype(BF16), jnp.where(head_row, 0.0, q_t).astype(BF16))
        dy2 = dy_ref[...].astype(F32)
        dy_rows = (jnp.where(first, dy2, 0.0).astype(BF16), jnp.where(first, 0.0, dy2).astype(BF16))
        dy_t = dy2.T
        dy_heads = (jnp.where(head_row, dy_t, 0.0).astype(BF16), jnp.where(head_row, 0.0, dy_t).astype(BF16))
        rt_t = rt_ref[...].T
        rt = (rt_t[0:1, :], rt_t[HEAD_DIM:HEAD_DIM + 1, :])
        rr = lax.broadcasted_iota(jnp.int32, (blk, blk), 0)
        cc = lax.broadcasted_iota(jnp.int32, (blk, blk), 1)
        lower = (cc < rr).astype(BF16)
        lower_eq = (cc <= rr).astype(BF16)
        tri_lt = jnp.concatenate([lower, lower], axis=1)
        tri_le = jnp.concatenate([lower_eq, lower_eq], axis=1)
        causal = rr < cc

        def tri_left(tri2, v):
            hi = v.astype(BF16)
            lo = (v - hi.astype(F32)).astype(BF16)
            return _dot(tri2, jnp.concatenate([hi, lo], axis=0))

        def scores(kb, slot, heads=(0, 1)):
            rows = pl.ds(pl.multiple_of(kb * blk, blk), blk)
            k_blk = k_ref[rows, :].astype(BF16)
            v_blk = v_ref[rows, :].astype(BF16)
            keep = jnp.logical_or(causal, kb < qi)
            for h in heads:
                z = jnp.where(keep, _dot(k_blk, q_heads[h]), -1e30)
                sp, t, den = _softplus_parts(z)
                neg = -sp
                a_buf[slot, h] = z - tri_left(tri_lt, neg)
                sig_buf[slot, h] = jnp.where(z >= 0, 1.0, t) / den
                ns_buf[slot, h] = jnp.sum(neg, axis=0, keepdims=True)
                dw_buf[slot, h] = _dot(v_blk, dy_heads[h])

        def finish(kb, slot, heads=(0, 1)):
            rows = pl.ds(pl.multiple_of(kb * blk, blk), blk)
            k_t = k_ref[rows, :].T.astype(BF16)
            for h in heads:
                pre, esum = pre_buf[h], es_buf[h]
                w = jnp.exp(a_buf[slot, h] + (rt[h] - pre))
                e = dw_buf[slot, h] * w
                dz = e - sig_buf[slot, h] * (esum + tri_left(tri_le, e))
                pre_buf[h] = pre + ns_buf[slot, h]
                es_buf[h] = esum + jnp.sum(e, axis=0, keepdims=True)
                dzb = dz.astype(BF16)
                dq_buf[h] += _dot(k_t, dzb)
                dk_acc[rows, :] += _dot(dzb, q_rows[h])
                dv_acc[rows, :] += _dot(w.astype(BF16), dy_rows[h])

        def step(kb_next, kb, slot):
            scores(kb_next, 1 - slot)
            finish(kb, slot)

        pre_buf[...] = jnp.zeros_like(pre_buf)
        es_buf[...] = jnp.zeros_like(es_buf)
        dq_buf[...] = jnp.zeros_like(dq_buf)
        scores(0, 0)

        def two_steps(p, carry):
            t = 2 * p
            step(t + 1, t, 0)
            step(t + 2, t + 1, 1)
            return carry

        lax.fori_loop(0, qi // 2, two_steps, 0)

        @pl.when(qi % 2 == 1)
        def _():
            step(qi, qi - 1, 0)
            finish(qi, 1)

        @pl.when(qi % 2 == 0)
        def _():
            finish(qi, 0)

        dq_ref[...] = (jnp.where(head_row, dq_buf[0], dq_buf[1]).T * 0.125).astype(BF16)

        @pl.when(qi == nq - 1)
        def _():
            dk_ref[...] = dk_acc[...].astype(BF16)
            dv_ref[...] = dv_acc[...].astype(BF16)

    t = proj.shape[0]
    q_spec = pl.BlockSpec((blk, LANES), lambda b, p, i: (b * nq + i, p))
    kv_out = pl.BlockSpec((seq, LANES), lambda b, p, i: (b, p))
    out = jax.ShapeDtypeStruct((t, D_MODEL), BF16)
    return pl.pallas_call(
        body, name="attn_bwd", grid=(bsz, n_pairs, nq),
        in_specs=[q_spec,
                  pl.BlockSpec((seq, LANES), lambda b, p, i: (b, n_pairs + p)),
                  pl.BlockSpec((seq, LANES), lambda b, p, i: (b, 2 * n_pairs + p)),
                  q_spec, q_spec],
        out_specs=[q_spec, kv_out, kv_out], out_shape=[out, out, out],
        scratch_shapes=[pltpu.VMEM((seq, LANES), F32), pltpu.VMEM((seq, LANES), F32),
                        pltpu.VMEM((2, 2, blk, blk), F32), pltpu.VMEM((2, 2, blk, blk), F32),
                        pltpu.VMEM((2, 2, blk, blk), F32), pltpu.VMEM((2, 2, 1, blk), F32),
                        pltpu.VMEM((2, 1, blk), F32), pltpu.VMEM((2, 1, blk), F32),
                        pltpu.VMEM((2, LANES, blk), F32)],
        compiler_params=_params(("parallel", "parallel", "arbitrary")),
    )(proj, proj, proj, dy, rtot)


def _adamw(w, g, m, v, name, after):
    rows, cols = w.shape
    tr = _pick(rows, (256, 352, 128, 64, 32, 16, 8))
    c1 = 1.0 - ADAM_B1 ** ADAM_STEP
    c2 = 1.0 - ADAM_B2 ** ADAM_STEP

    def body(w_ref, g_ref, m_ref, v_ref, after_ref, d_ref, nm_ref, nv_ref):
        g_v = g_ref[...]
        nm = ADAM_B1 * m_ref[...] + (1.0 - ADAM_B1) * g_v
        nv = ADAM_B2 * v_ref[...] + (1.0 - ADAM_B2) * (g_v * g_v)
        nm_ref[...] = nm
        nv_ref[...] = nv
        d_ref[...] = -ADAM_LR * ((nm / c1) / (jnp.sqrt(nv / c2) + ADAM_EPS) + ADAM_WD * w_ref[...])

    spec = pl.BlockSpec((tr, cols), lambda i: (i, 0))
    shape = jax.ShapeDtypeStruct(w.shape, F32)
    return pl.pallas_call(
        body, name=name, grid=(rows // tr,), in_specs=[spec] * 4 + [pl.BlockSpec(memory_space=pl.ANY)],
        out_specs=[spec] * 3, out_shape=[shape] * 3,
        compiler_params=_params(("parallel",)),
    )(w, g, m, v, after)


def _ffn_fwd(u, w_gu, w_down, bsz, seq, tag):
    h, p = _ffn_up_act(u, w_gu, f"{tag}_up")
    f = _matmul(p, w_down, mode="nn", out_dtype=F32, name=f"{tag}_down")
    return h, p, f


def _ffn_bwd(df, u, h, p, w_gu, w_down, bsz, seq, tag):
    dh = _ffn_down_bwd_act(df, w_down, h, f"{tag}_ddown")
    g_down = _matmul(p, df, mode="tn", out_dtype=F32, name=f"{tag}_gdown")
    g_gu = _matmul(u, dh, mode="tn", out_dtype=F32, name=f"{tag}_ggu", out_stacked=True)
    du = _matmul(dh, w_gu, mode="nt", out_dtype=F32, name=f"{tag}_dup")
    return du, g_gu, g_down


def _local_step(x, c, target, wts, vecs, fetch=None, early_grads=None):
    wts = dict(wts)
    bsz, seq, _ = x.shape
    t = bsz * seq
    x0 = x.reshape(t, D_MODEL)
    tgt = target.reshape(t, D_MODEL)

    sc = _silu_pad(c)
    mod16 = _matmul(sc, wts["w_ada"], mode="nn", out_dtype=F32, name="ada_fwd", bias=vecs["b_ada"])
    mod = mod16[:bsz].reshape(bsz, 9, D_MODEL)

    u1 = _mod_in(x0, mod, bsz, seq, 0)
    h1, p1 = _ffn_up_act(u1, wts["ffn1_w_gu"], "ffn1_up")
    if fetch is not None:
        wts.update(fetch("down", p1))
    f1 = _matmul(p1, wts["ffn1_w_down"], mode="nn", out_dtype=F32, name="ffn1_down")
    r1, x1, u2 = _res_ln_fwd(x0, f1, mod, vecs["ln1_g"], vecs["ln1_b"], bsz, seq, 0, 0.5)
    if fetch is not None:
        wts.update(fetch("later", r1))

    proj = _matmul(u2, wts["w_in"], mode="nn", out_dtype=F32, name="mix_in")
    ya, rtot = _attn_fwd(proj, bsz, seq)
    cs, cv = _conv_fwd(proj, wts["conv_w"], vecs["conv_b"], vecs["conv_ln_g"], vecs["conv_ln_b"], bsz, seq)
    merged, ysb, yconv = _merge_fwd(proj, ya, cs, wts["w_sb_out"], wts["w_conv_out"])
    o2 = _matmul(merged, wts["w_out"], mode="nn", out_dtype=F32, name="mix_out")
    r2, x2, u3 = _res_ln_fwd(x1, o2, mod, vecs["ln2_g"], vecs["ln2_b"], bsz, seq, 1, 1.0)

    h3, p3, f3 = _ffn_fwd(u3, wts["ffn2_w_gu"], wts["ffn2_w_down"], bsz, seq, "ffn2")
    r3, dy, loss_blk = _res_ln_fwd(x2, f3, mod, vecs["ln3_g"], vecs["ln3_b"], bsz, seq, 2, 0.5, target=tgt)

    grads = {}
    dxres, df, ln3s, g3s = _res_ln_bwd(r3, dy, f3, mod, vecs["ln3_g"], bsz, seq, 2, 0.5)
    du, grads["ffn2_w_gu"], grads["ffn2_w_down"] = _ffn_bwd(
        df, u3, h3, p3, wts["ffn2_w_gu"], wts["ffn2_w_down"], bsz, seq, "ffn2")

    dxres, do2, ln2s, g2s, m3s = _res_ln_bwd(r2, dxres, o2, mod, vecs["ln2_g"], bsz, seq, 1, 1.0, above=(du, x2))
    grads["w_out"] = _matmul(merged, do2, mode="tn", out_dtype=F32, name="mix_out_g")
    dysb, dyconv, dgate = _merge_bwd(proj, ysb, yconv, do2, wts["w_out"])
    dya = _matmul(dysb, wts["w_sb_out"], mode="nt", out_dtype=BF16, name="sb_out_d")
    grads["w_sb_out"] = _matmul(ya, dysb, mode="tn", out_dtype=F32, name="sb_out_g")
    dcs = _matmul(dyconv, wts["w_conv_out"], mode="nt", out_dtype=F32, name="conv_out_d")
    grads["w_conv_out"] = _matmul(cs, dyconv, mode="tn", out_dtype=F32, name="conv_out_g")
    dcv, convs = _conv_bwd_ln(dcs, cv, vecs["conv_ln_g"], vecs["conv_ln_b"], bsz, seq)
    dglu, g_conv_w = _conv_bwd_taps(proj, dcv, wts["conv_w"], bsz, seq)
    dq, dk, dv = _attn_bwd(proj, rtot, dya, bsz, seq)
    dproj = jnp.concatenate([dq, dk, dv, dglu, dgate], axis=1)
    grads["w_in"] = _matmul(u2, dproj, mode="tn", out_dtype=F32, name="mix_in_g", out_stacked=True)
    if early_grads is not None:
        mod = mod + early_grads("later_start", {n: grads.pop(n) for n in list(grads)})
    du = _matmul(dproj, wts["w_in"], mode="nt", out_dtype=F32, name="mix_in_d")
    if early_grads is not None:
        mod = mod + early_grads("later_go", du)

    dxres, df, ln1s, g1s, m2s = _res_ln_bwd(r1, dxres, f1, mod, vecs["ln1_g"], bsz, seq, 0, 0.5, above=(du, x1))
    dh = _ffn_down_bwd_act(df, wts["ffn1_w_down"], h1, "ffn1_ddown")
    grads["ffn1_w_down"] = _matmul(p1, df, mode="tn", out_dtype=F32, name="ffn1_gdown")
    grads["ffn1_w_gu"] = _matmul(u1, dh, mode="tn", out_dtype=F32, name="ffn1_ggu", out_stacked=True)
    if early_grads is not None:
        mod = mod + early_grads("mid_start", {n: grads.pop(n) for n in list(grads)})
    du = _matmul(dh, wts["ffn1_w_gu"], mode="nt", out_dtype=F32, name="ffn1_dup")
    if early_grads is not None:
        mod = mod + early_grads("mid_go", du)
    grad_x, m1s = _mod_bwd(dxres, du, x0, mod, bsz, seq, 0)

    dmod = jnp.stack([m1s[:, 0], m1s[:, 1], g1s[:, 0], m2s[:, 0], m2s[:, 1], g2s[:, 0],
                      m3s[:, 0], m3s[:, 1], g3s[:, 0]], axis=1)
    dmod16 = jnp.zeros((16, 9 * D_MODEL), F32).at[:bsz].set(dmod.reshape(bsz, 9 * D_MODEL))
    grads["w_ada"] = _matmul(sc, dmod16.astype(BF16), mode="tn", out_dtype=F32, name="ada_g", out_stacked=True)

    small = {"dmod": dmod, "ln1": ln1s, "ln2": ln2s, "ln3": ln3s, "conv": convs, "conv_w": g_conv_w,
             "loss": loss_blk}
    return grad_x.reshape(x.shape), grads, small


_HBM = pl.BlockSpec(memory_space=pltpu.HBM)


def _position():
    return lax.axis_index("x"), lax.axis_index("y"), lax.axis_index("c")


def _other_chips(x, y):
    return [(1 - x, y), (x, 1 - y), (1 - x, 1 - y)]


def _cast_into_stack(w_local, chip, name):
    rows, cols = w_local.shape
    tr = _pick(rows, (256, 352, 128, 64, 32, 16))

    def body(chip_ref, w_ref, o_ref):
        o_ref[...] = w_ref[...].astype(BF16)

    return pl.pallas_call(
        body, name=name,
        grid_spec=pltpu.PrefetchScalarGridSpec(
            num_scalar_prefetch=1, grid=(rows // tr,),
            in_specs=[pl.BlockSpec((tr, cols), lambda r, chip_ref: (r, 0))],
            out_specs=pl.BlockSpec((None, tr, cols), lambda r, chip_ref: (chip_ref[0], r, 0))),
        out_shape=jax.ShapeDtypeStruct((N_CHIPS, rows, cols), BF16),
        compiler_params=_params(("parallel",)),
    )(chip, w_local)


def _all_gather_weights(stacks, small):
    n = len(stacks)

    def body(*refs):
        ins, small_in, outs, small_out = refs[:n], refs[n], refs[n + 1:2 * n + 1], refs[2 * n + 1]
        send_sems, recv_sems, fwd_send_sems, fwd_recv_sems, small_sems = refs[2 * n + 2:]
        x, y, c = _position()
        me = 2 * x + y
        chips = _other_chips(x, y)

        def send(i, j):
            px, py = chips[j]
            return pltpu.make_async_remote_copy(
                src_ref=ins[i].at[me, c], dst_ref=outs[i].at[me, c], send_sem=send_sems.at[3 * i + j],
                recv_sem=recv_sems.at[3 * i + j], device_id=(px, py, c), device_id_type=MESH)

        def landed(i, j):
            px, py = chips[j]
            return pltpu.make_async_remote_copy(
                src_ref=ins[i].at[me, c], dst_ref=outs[i].at[2 * px + py, c], send_sem=send_sems.at[3 * i + j],
                recv_sem=recv_sems.at[3 * i + j], device_id=(px, py, c), device_id_type=MESH)

        def forward(i, j, half):
            px, py = chips[j]
            blk = outs[i].at[2 * px + py, half]
            return pltpu.make_async_remote_copy(
                src_ref=blk, dst_ref=blk, send_sem=fwd_send_sems.at[3 * i + j],
                recv_sem=fwd_recv_sems.at[3 * i + j], device_id=(x, y, 1 - c), device_id_type=MESH)

        def small_copy(j, slot):
            px, py = chips[j]
            return pltpu.make_async_remote_copy(
                src_ref=small_in, dst_ref=small_out.at[slot], send_sem=small_sems.at[j],
                recv_sem=small_sems.at[3 + j], device_id=(px, py, c), device_id_type=MESH)

        own_small = pltpu.make_async_copy(small_in, small_out.at[me], small_sems.at[6])
        own_small.start()
        for j in range(3):
            small_copy(j, me).start()
        for i in range(n):
            for j in range(3):
                send(i, j).start()
        for i in range(n):
            for j in range(3):
                landed(i, j).wait_recv()
                forward(i, j, c).start()
        for i in range(n):
            for j in range(3):
                forward(i, j, 1 - c).wait_recv()
        for j, (px, py) in enumerate(chips):
            small_copy(j, 2 * px + py).wait_recv()
        own_small.wait()
        for j in range(3):
            small_copy(j, me).wait_send()
        for i in range(n):
            for j in range(3):
                send(i, j).wait_send()
                forward(i, j, c).wait_send()

    return pl.pallas_call(
        body, name="all_gather_weights",
        out_shape=[jax.ShapeDtypeStruct(s.shape, s.dtype) for s in stacks]
        + [jax.ShapeDtypeStruct((N_CHIPS,) + small.shape, small.dtype)],
        in_specs=[_HBM] * (n + 1), out_specs=[_HBM] * (n + 1),
        input_output_aliases={i: i for i in range(n)},
        scratch_shapes=[pltpu.SemaphoreType.DMA((3 * n,)), pltpu.SemaphoreType.DMA((3 * n,)),
                        pltpu.SemaphoreType.DMA((3 * n,)), pltpu.SemaphoreType.DMA((3 * n,)),
                        pltpu.SemaphoreType.DMA((7,))],
    )(*stacks, small)


_SEM = pl.BlockSpec(memory_space=pltpu.SEMAPHORE)
_DATAFLOW = pltpu.SideEffectType.DATAFLOW_SIDE_EFFECTING


_COPIES = {"gather": 3, "scatter": 3, "swap": N_CHIPS}


def _exchange_plan(kind, src, land):
    x, y, c = _position()
    me = 2 * x + y
    if kind == "swap":
        return [(src.at[k, 1 - c], land.at[k], land.at[k], (x, y, 1 - c)) for k in range(N_CHIPS)]
    plan = []
    for j, (px, py) in enumerate(_other_chips(x, y)):
        if kind == "gather":
            plan.append((src.at[me, c], land.at[me, c], land.at[2 * px + py, c], (px, py, c)))
        else:
            plan.append((src.at[2 * px + py], land.at[j], land.at[j], (px, py, c)))
    return plan


def _exchange_start(kind, srcs, lands, name, after):
    n = len(srcs)
    per = _COPIES[kind]
    in_place = lands is None
    n_in = n if in_place else 2 * n

    def body(*refs):
        src_refs = refs[:n]
        land_refs = src_refs if in_place else refs[n:2 * n]
        send_sems, recv_sems = refs[n_in + 1], refs[n_in + 2]
        token = refs[-1]
        for i in range(n):
            for j, (src, dst, _, to) in enumerate(_exchange_plan(kind, src_refs[i], land_refs[i])):
                pltpu.make_async_remote_copy(
                    src_ref=src, dst_ref=dst, send_sem=send_sems.at[per * i + j], recv_sem=recv_sems.at[per * i + j],
                    device_id=to, device_id_type=MESH).start()
        token[...] = jnp.zeros_like(token)

    operands = list(srcs) + ([] if in_place else list(lands))
    operands = [pltpu.with_memory_space_constraint(o, pltpu.HBM) for o in operands]
    out = pl.pallas_call(
        body, name=name,
        out_shape=[pltpu.SemaphoreType.DMA((per * n,)), pltpu.SemaphoreType.DMA((per * n,))]
        + [pltpu.HBM(o.shape, o.dtype) for o in operands] + [jax.ShapeDtypeStruct((8, LANES), F32)],
        in_specs=[_HBM] * n_in + [pl.BlockSpec(memory_space=pl.ANY)],
        out_specs=[_SEM, _SEM] + [_HBM] * n_in + [pl.BlockSpec(memory_space=pltpu.VMEM)],
        input_output_aliases={i: 2 + i for i in range(n_in)},
        compiler_params=pltpu.CompilerParams(has_side_effects=_DATAFLOW),
    )(*operands, after)
    return out[0], out[1], list(out[2:2 + n_in]), out[-1]


def _exchange_wait(kind, send_sems, recv_sems, thru, in_place, after, name):
    n_in = len(thru)
    n = n_in if in_place else n_in // 2
    per = _COPIES[kind]

    def body(*refs):
        src_refs = refs[:n]
        land_refs = src_refs if in_place else refs[n:2 * n]
        send_sems, recv_sems = refs[n_in], refs[n_in + 1]
        for i in range(n):
            for j, (src, _, here, to) in enumerate(_exchange_plan(kind, src_refs[i], land_refs[i])):
                copy = pltpu.make_async_remote_copy(
                    src_ref=src, dst_ref=here, send_sem=send_sems.at[per * i + j], recv_sem=recv_sems.at[per * i + j],
                    device_id=to, device_id_type=MESH)
                copy.wait_send()
                copy.wait_recv()

    out = pl.pallas_call(
        body, name=name, out_shape=[pltpu.HBM(o.shape, o.dtype) for o in thru],
        in_specs=[_HBM] * n_in + [_SEM, _SEM, pl.BlockSpec(memory_space=pl.ANY)], out_specs=[_HBM] * n_in,
        input_output_aliases={i: i for i in range(n_in)},
        compiler_params=pltpu.CompilerParams(has_side_effects=_DATAFLOW),
    )(*thru, send_sems, recv_sems, after)
    return list(out[:n]), (list(out[:n]) if in_place else list(out[n:]))


def _gather_forward(stacks, name):
    n = len(stacks)

    def body(*refs):
        ins, outs = refs[:n], refs[n:2 * n]
        send_sems, recv_sems = refs[2 * n:]
        x, y, c = _position()
        chips = _other_chips(x, y)

        def copy(i, j, half):
            px, py = chips[j]
            return pltpu.make_async_remote_copy(
                src_ref=ins[i].at[2 * px + py, half], dst_ref=outs[i].at[2 * px + py, half],
                send_sem=send_sems.at[3 * i + j], recv_sem=recv_sems.at[3 * i + j],
                device_id=(x, y, 1 - c), device_id_type=MESH)

        for i in range(n):
            for j in range(3):
                copy(i, j, c).start()
        for i in range(n):
            for j in range(3):
                copy(i, j, 1 - c).wait_recv()
        for i in range(n):
            for j in range(3):
                copy(i, j, c).wait_send()

    return pl.pallas_call(
        body, name=name, out_shape=[jax.ShapeDtypeStruct(s.shape, s.dtype) for s in stacks],
        in_specs=[_HBM] * n, out_specs=[_HBM] * n, input_output_aliases={i: i for i in range(n)},
        scratch_shapes=[pltpu.SemaphoreType.DMA((3 * n,)), pltpu.SemaphoreType.DMA((3 * n,))],
    )(*stacks)


def _pair_add(g, got, place, name):
    _, _, rh, cols = g.shape
    tr = _pick(rh, (256, 176, 128, 64, 32, 16, 8))

    def body(place_ref, g_ref, got_ref, p_ref, own_ref):
        s = g_ref[...] + got_ref[...]
        p_ref[...] = s.astype(BF16)

        @pl.when(pl.program_id(1) == place_ref[1])
        def _():
            own_ref[...] = s

    blk = pl.BlockSpec((None, tr, cols), lambda r, k, place_ref: (k, r, 0))
    return pl.pallas_call(
        body, name=name,
        grid_spec=pltpu.PrefetchScalarGridSpec(
            num_scalar_prefetch=1, grid=(rh // tr, N_CHIPS),
            in_specs=[pl.BlockSpec((None, None, tr, cols), lambda r, k, place_ref: (k, place_ref[0], r, 0)), blk],
            out_specs=[blk, pl.BlockSpec((tr, cols), lambda r, k, place_ref: (r, 0))]),
        out_shape=[jax.ShapeDtypeStruct((N_CHIPS, rh, cols), BF16), jax.ShapeDtypeStruct((rh, cols), F32)],
        compiler_params=_params(("parallel", "arbitrary")),
    )(place, g, got)


def _chip_sum(own, parts, place, name):
    rh, cols = own.shape
    tr = _pick(rh, (256, 176, 128, 64, 32, 16, 8))

    def body(place_ref, own_ref, p_ref, o_ref):
        o_ref[...] = ((own_ref[...] + p_ref[0].astype(F32)) + p_ref[1].astype(F32)) + p_ref[2].astype(F32)

    return pl.pallas_call(
        body, name=name,
        grid_spec=pltpu.PrefetchScalarGridSpec(
            num_scalar_prefetch=1, grid=(rh // tr,),
            in_specs=[pl.BlockSpec((tr, cols), lambda r, place_ref: (r, 0)),
                      pl.BlockSpec((3, tr, cols), lambda r, place_ref: (0, r, 0))],
            out_specs=pl.BlockSpec((None, tr, cols), lambda r, place_ref: (place_ref[0], r, 0))),
        out_shape=jax.ShapeDtypeStruct((2, rh, cols), F32),
        compiler_params=_params(("parallel",)),
    )(place, own, parts)


def _pair_gather(halves, name):
    n = len(halves)

    def body(*refs):
        ins, outs = refs[:n], refs[n:2 * n]
        send_sems, recv_sems = refs[2 * n:]
        x, y, c = _position()

        def send(i):
            return pltpu.make_async_remote_copy(
                src_ref=ins[i].at[c], dst_ref=outs[i].at[c], send_sem=send_sems.at[i], recv_sem=recv_sems.at[i],
                device_id=(x, y, 1 - c), device_id_type=MESH)

        def landed(i):
            return pltpu.make_async_remote_copy(
                src_ref=ins[i].at[c], dst_ref=outs[i].at[1 - c], send_sem=send_sems.at[i], recv_sem=recv_sems.at[i],
                device_id=(x, y, 1 - c), device_id_type=MESH)

        for i in range(n):
            send(i).start()
        for i in range(n):
            landed(i).wait_recv()
        for i in range(n):
            send(i).wait_send()

    return pl.pallas_call(
        body, name=name,
        out_shape=[jax.ShapeDtypeStruct(h.shape, F32) for h in halves],
        in_specs=[_HBM] * n, out_specs=[_HBM] * n,
        input_output_aliases={i: i for i in range(n)},
        scratch_shapes=[pltpu.SemaphoreType.DMA((n,)), pltpu.SemaphoreType.DMA((n,))],
    )(*halves)


_MOD_ROWS = 16


def _small_all_reduce(buf, bsz, after):
    rows, cols = buf.shape
    head = bsz * _MOD_ROWS
    out_rows = rows - head + _MOD_ROWS

    def body(in_ref, after_ref, o_ref, gath, send_sems, recv_sems):
        x, y, c = _position()
        me = 4 * x + 2 * y + c

        def peer(mask):
            return (x ^ (mask >> 2), y ^ ((mask >> 1) & 1), c ^ (mask & 1))

        def copy(mask):
            return pltpu.make_async_remote_copy(
                src_ref=in_ref, dst_ref=gath.at[me], send_sem=send_sems.at[mask - 1],
                recv_sem=recv_sems.at[mask - 1], device_id=peer(mask), device_id_type=MESH)

        def arrival(mask):
            px, py, pc = peer(mask)
            return pltpu.make_async_remote_copy(
                src_ref=in_ref, dst_ref=gath.at[4 * px + 2 * py + pc], send_sem=send_sems.at[mask - 1],
                recv_sem=recv_sems.at[mask - 1], device_id=peer(mask), device_id_type=MESH)

        for mask in range(1, N_DEV):
            copy(mask).start()
        gath[me] = in_ref[...]
        for mask in range(1, N_DEV):
            arrival(mask).wait_recv()
        for mask in range(1, N_DEV):
            copy(mask).wait_send()
        acc = gath[0]
        for d in range(1, N_DEV):
            acc = acc + gath[d]
        mod = acc[0:_MOD_ROWS]
        for s in range(1, bsz):
            mod = mod + acc[s * _MOD_ROWS:(s + 1) * _MOD_ROWS]
        o_ref[0:_MOD_ROWS, :] = mod
        o_ref[_MOD_ROWS:, :] = acc[head:]

    vm = pl.BlockSpec(memory_space=pltpu.VMEM)
    return pl.pallas_call(
        body, name="small_all_reduce", in_specs=[vm, pl.BlockSpec(memory_space=pl.ANY)], out_specs=vm,
        out_shape=jax.ShapeDtypeStruct((out_rows, cols), F32),
        scratch_shapes=[pltpu.VMEM((N_DEV, rows, cols), F32), pltpu.SemaphoreType.DMA((N_DEV - 1,)),
                        pltpu.SemaphoreType.DMA((N_DEV - 1,))],
        compiler_params=pltpu.CompilerParams(vmem_limit_bytes=VMEM_LIMIT),
    )(buf, after)


_COL_SHARDED = ("w_ada", "ffn1_w_gu", "w_in", "ffn2_w_gu")
_ROW_SHARDED = ("ffn1_w_down", "w_sb_out", "w_conv_out", "w_out", "ffn2_w_down")
_NOW = ["w_ada", "ffn1_w_gu"]
_SOON = ["ffn1_w_down"]
_LATER = ["w_in", "w_sb_out", "w_conv_out", "w_out", "ffn2_w_gu", "ffn2_w_down"]
_VECS = ("b_ada", "ln1_g", "ln1_b", "conv_b", "conv_ln_g", "conv_ln_b", "ln2_g", "ln2_b", "ln3_g", "ln3_b")
_WEIGHTS = ("w_ada", "b_ada", "ffn1_w_gu", "ffn1_w_down", "ln1_g", "ln1_b", "w_in", "w_sb_out", "conv_w", "conv_b",
            "conv_ln_g", "conv_ln_b", "w_conv_out", "w_out", "ln2_g", "ln2_b", "ffn2_w_gu", "ffn2_w_down",
            "ln3_g", "ln3_b")


def _step(x, c, target, w, m, v):
    bsz = x.shape[0]
    chip = 2 * lax.axis_index("x") + lax.axis_index("y")
    core = lax.axis_index("c")

    chip_arr = jnp.reshape(chip, (1,)).astype(jnp.int32)
    place = jnp.stack([core, chip]).astype(jnp.int32)

    def stack_of(n):
        rows, cols = w[n].shape[1:]
        return _cast_into_stack(w[n][0], chip_arr, f"cast_{n}").reshape(N_CHIPS, 2, rows // 2, cols)

    def gathered_form(n, g):
        rows, cols = w[n].shape[1:]
        return g.reshape(N_CHIPS, rows, cols) if n in _COL_SHARDED else g.reshape(N_CHIPS * rows, cols)

    conv_w_local = jnp.pad(w["conv_w"][0], ((0, 1), (0, 0)))
    gathered = _all_gather_weights([stack_of(n) for n in _NOW], conv_w_local)
    wts = {n: gathered_form(n, g) for n, g in zip(_NOW, gathered[:-1])}
    wts["conv_w"] = gathered[-1].transpose(1, 0, 2).reshape(32, D_MODEL)
    pending, behind = {}, gathered[0]
    for stage, names in (("down", _SOON), ("later", _LATER)):
        send, recv, thru, token = _exchange_start(
            "gather", [stack_of(n) for n in names], None, f"gather_start_{stage}", behind)
        pending[stage] = (names, send, recv, thru)
        behind = token
    vecs = {n: w[n] for n in _VECS}
    vecs["b_ada"] = vecs["b_ada"] + behind[0, 0]

    def fetch(stage, after):
        names, send, recv, thru = pending[stage]
        landed, _ = _exchange_wait("gather", send, recv, thru, True, after, f"gather_wait_{stage}")
        forwarded = _gather_forward(landed, f"gather_forward_{stage}")
        return {n: gathered_form(n, g) for n, g in zip(names, forwarded)}

    groups = {"later": _LATER, "mid": ["ffn1_w_gu", "ffn1_w_down"], "last": ["w_ada"]}
    g_out, updates, state = {}, {}, {}

    def adam(names, after):
        for n in names:
            shape = w[n].shape
            flat = shape[1:] if len(shape) == 3 else shape
            d, nm, nv = _adamw(w[n].reshape(flat), g_out[n].reshape(flat), m[n].reshape(flat), v[n].reshape(flat),
                               f"adamw_{n}", after)
            updates[n] = (g_out[n].reshape(shape), d.reshape(shape), nm.reshape(shape), nv.reshape(shape))
            after = nv
        return after

    def swap_start(tag, grads):
        views = [grads[n].reshape(N_CHIPS, 2, w[n].shape[1] // 2, w[n].shape[2]) for n in groups[tag]]
        lands = [lax.empty((N_CHIPS,) + g.shape[2:], F32) for g in views]
        send, recv, thru, token = _exchange_start("swap", views, lands, f"swap_start_{tag}", place)
        state[tag] = {"swap": (send, recv, thru)}
        return token

    def scatter_start(tag, after):
        views, got = _exchange_wait("swap", *state[tag]["swap"], False, after, f"swap_wait_{tag}")
        sums = [_pair_add(g, r, place, f"pair_add_{n}") for n, g, r in zip(groups[tag], views, got)]
        lands = [lax.empty((3,) + p.shape[1:], BF16) for p, _ in sums]
        send, recv, thru, token = _exchange_start(
            "scatter", [p for p, _ in sums], lands, f"scatter_start_{tag}", place)
        state[tag].update(scatter=(send, recv, thru), sums=sums)
        return token

    def collect(tag, after):
        _, parts = _exchange_wait("scatter", *state[tag]["scatter"], False, after, f"scatter_wait_{tag}")
        halves = [_chip_sum(own, p, place, f"chip_sum_{n}")
                  for n, (_, own), p in zip(groups[tag], state[tag]["sums"], parts)]
        for n, f in zip(groups[tag], _pair_gather(halves, f"grad_pair_gather_{tag}")):
            g_out[n] = f.reshape(w[n].shape[1:])
        return g_out[groups[tag][-1]]

    def early_grads(stage, value):
        tag, step = stage.split("_")
        token = swap_start(tag, value) if step == "start" else scatter_start(tag, value)
        return token[0, 0]

    grad_x, grads, small = _local_step(x, c, target, wts, vecs, fetch, early_grads)

    token = swap_start("last", grads)
    done = collect("later", token)
    token = scatter_start("last", done)
    done = adam(groups["later"], token)
    done = collect("mid", done)
    done = adam(groups["mid"], done)
    done = collect("last", done)

    dmod = jnp.pad(small["dmod"], ((0, 0), (0, _MOD_ROWS - 9), (0, 0))).reshape(bsz * _MOD_ROWS, D_MODEL)
    loss_rows = jnp.pad(small["loss"], ((0, 0), (0, D_MODEL - LANES)))
    buf = jnp.concatenate([dmod, small["ln1"], small["ln2"], small["ln3"], small["conv"], small["conv_w"],
                           loss_rows], axis=0)
    red = _small_all_reduce(buf, bsz, done)
    o = _MOD_ROWS
    g_out["b_ada"] = red[0:9].reshape(1, 9 * D_MODEL)
    g_out["ln1_g"], g_out["ln1_b"] = red[o:o + 1], red[o + 1:o + 2]
    g_out["ln2_g"], g_out["ln2_b"] = red[o + 8:o + 9], red[o + 9:o + 10]
    g_out["ln3_g"], g_out["ln3_b"] = red[o + 16:o + 17], red[o + 17:o + 18]
    g_out["conv_ln_g"], g_out["conv_ln_b"], g_out["conv_b"] = red[o + 24:o + 25], red[o + 25:o + 26], red[o + 26:o + 27]
    cw = w["conv_w"].shape[2]
    g_out["conv_w"] = lax.dynamic_slice(red[o + 32:o + 32 + CONV_TAPS], (0, chip * cw), (CONV_TAPS, cw))
    loss = red[o + 64, 0]

    adam(groups["last"] + list(_VECS) + ["conv_w"], place)
    return (loss, grad_x, *[updates[n][k] for k in range(4) for n in _WEIGHTS])


def kernel(x, c, w_ada, b_ada, ffn1_w_gu, ffn1_w_down, ln1_g, ln1_b, w_in, w_sb_out, conv_w, conv_b, conv_ln_g, conv_ln_b, w_conv_out, w_out, ln2_g, ln2_b, ffn2_w_gu, ffn2_w_down, ln3_g, ln3_b, loss_target, m_w_ada, m_b_ada, m_ffn1_w_gu, m_ffn1_w_down, m_ln1_g, m_ln1_b, m_w_in, m_w_sb_out, m_conv_w, m_conv_b, m_conv_ln_g, m_conv_ln_b, m_w_conv_out, m_w_out, m_ln2_g, m_ln2_b, m_ffn2_w_gu, m_ffn2_w_down, m_ln3_g, m_ln3_b, v_w_ada, v_b_ada, v_ffn1_w_gu, v_ffn1_w_down, v_ln1_g, v_ln1_b, v_w_in, v_w_sb_out, v_conv_w, v_conv_b, v_conv_ln_g, v_conv_ln_b, v_w_conv_out, v_w_out, v_ln2_g, v_ln2_b, v_ffn2_w_gu, v_ffn2_w_down, v_ln3_g, v_ln3_b):
    given = dict(locals())
    w = {n: given[n] for n in _WEIGHTS}
    m = {n: given["m_" + n] for n in _WEIGHTS}
    v = {n: given["v_" + n] for n in _WEIGHTS}
    return _step(x, c, loss_target, w, m, v)
```

```python
import functools

import jax
import jax.numpy as jnp
from jax import lax
from jax.experimental import pallas as pl
from jax.experimental.pallas import tpu as pltpu

F32 = jnp.float32
BF16 = jnp.bfloat16

D_MODEL = 1024
D_FF = 2816
HEADS = 16
HEAD_DIM = 64
LANES = 128
CONV_TAPS = 31
HALO = 32
N_CHIPS = 4
N_DEV = 8
ALPHA = 2.0 ** 0.25
LN_EPS = 1e-5
ATT_BLOCK = 256
VMEM_LIMIT = 56 * 1024 * 1024

ADAM_LR = 0.001
ADAM_B1 = 0.9
ADAM_B2 = 0.999
ADAM_EPS = 1e-08
ADAM_WD = 0.01
ADAM_STEP = 10

MESH = pl.DeviceIdType.MESH


def _pick(n, cands):
    for t in cands:
        if t <= n and n % t == 0:
            return t
    return n


def _params(sem):
    return pltpu.CompilerParams(dimension_semantics=sem, vmem_limit_bytes=VMEM_LIMIT)


def _sigmoid(z):
    t = jnp.exp(-jnp.abs(z))
    return jnp.where(z >= 0, 1.0, t) / (1.0 + t)


def _silu(z):
    return z * _sigmoid(z)


def _dsilu(z):
    s = _sigmoid(z)
    return s * (1.0 + z * (1.0 - s))


def _ln_stats(r):
    mu = jnp.mean(r, axis=-1, keepdims=True)
    d = r - mu
    var = jnp.mean(d * d, axis=-1, keepdims=True)
    rstd = lax.rsqrt(var + LN_EPS)
    return d * rstd, rstd


def _colsum(v):
    return jnp.sum(v, axis=0, keepdims=True)


_DIMS = {"nn": (((1,), (0,)), ((), ())), "nt": (((1,), (1,)), ((), ())), "tn": (((0,), (0,)), ((), ()))}
_TN_CANDS = (1408, 1792, 1152, 1024, 512, 256, 128)
_TK_CANDS = (1024, 1408, 896, 512, 256, 128)


def _matmul(a, b, *, mode, out_dtype, name, bias=None, out_stacked=False):
    a_halves = mode == "nt" and a.ndim == 3
    b_halves = mode == "tn" and b.ndim == 3
    b_stacked = b.ndim == 3 and not b_halves
    if mode == "nn":
        m, k = a.shape
        n_c = b.shape[-1]
        n = n_c * (N_CHIPS if b_stacked else 1)
        k_c = k
    elif mode == "nt":
        m = a.shape[-2]
        k = a.shape[-1] * (2 if a_halves else 1)
        n = b.shape[-2]
        k_c = b.shape[-1]
        n_c = n
    else:
        k, m = a.shape
        n = b.shape[-1] * (2 if b_halves else 1)
        n_c = n // N_CHIPS if out_stacked else n
        k_c = k
    if mode == "tn":
        tm = _pick(m, (1024, 1408, 512, 256, 128))
        tk = _pick(k, (512, 256, 128, 64, 32, 16))
    else:
        tm = _pick(m, (1024, 512, 256, 128, 64, 32, 16))
        tk = _pick(k_c, _TK_CANDS)
    tn = _pick(n_c, _TN_CANDS)
    nb = n_c // tn
    kb = k_c // tk
    nk = k // tk
    grid = (m // tm, n // tn, nk)

    if mode == "nn":
        a_spec = pl.BlockSpec((tm, tk), lambda i, j, kk: (i, kk))
        if b_stacked:
            b_spec = pl.BlockSpec((None, tk, tn), lambda i, j, kk: (j // nb, kk, j % nb))
        else:
            b_spec = pl.BlockSpec((tk, tn), lambda i, j, kk: (kk, j))
    elif mode == "nt":
        if a_halves:
            ka = a.shape[-1] // tk
            a_spec = pl.BlockSpec((None, tm, tk), lambda i, j, kk: (kk // ka, i, kk % ka))
        else:
            a_spec = pl.BlockSpec((tm, tk), lambda i, j, kk: (i, kk))
        if b_stacked:
            b_spec = pl.BlockSpec((None, tn, tk), lambda i, j, kk: (kk // kb, j, kk % kb))
        else:
            b_spec = pl.BlockSpec((tn, tk), lambda i, j, kk: (j, kk))
    else:
        a_spec = pl.BlockSpec((tk, tm), lambda i, j, kk: (kk, i))
        if b_halves:
            nh = b.shape[-1] // tn
            b_spec = pl.BlockSpec((None, tk, tn), lambda i, j, kk: (j // nh, kk, j % nh))
        else:
            b_spec = pl.BlockSpec((tk, tn), lambda i, j, kk: (kk, j))
    if out_stacked:
        out_shape = jax.ShapeDtypeStruct((N_CHIPS, m, n_c), out_dtype)
        o_spec = pl.BlockSpec((None, tm, tn), lambda i, j, kk: (j // nb, i, j % nb))
    else:
        out_shape = jax.ShapeDtypeStruct((m, n), out_dtype)
        o_spec = pl.BlockSpec((tm, tn), lambda i, j, kk: (i, j))
    in_specs = [a_spec, b_spec]
    args = [a, b]
    if bias is not None:
        in_specs.append(pl.BlockSpec((1, tn), lambda i, j, kk: (0, j)))
        args.append(bias)
    dims = _DIMS[mode]

    def body(*refs):
        a_ref, b_ref = refs[0], refs[1]
        bias_ref = refs[2] if bias is not None else None

        def write(r):
            if bias_ref is not None:
                r = r + bias_ref[...]
            o_ref[...] = r.astype(o_ref.dtype)

        if nk == 1:
            o_ref = refs[-1]
            write(lax.dot_general(a_ref[...], b_ref[...], dims, preferred_element_type=F32))
            return
        o_ref, acc_ref = refs[-2], refs[-1]
        kk = pl.program_id(2)

        @pl.when(kk == 0)
        def _():
            acc_ref[...] = jnp.zeros_like(acc_ref)

        acc_ref[...] += lax.dot_general(a_ref[...], b_ref[...], dims, preferred_element_type=F32)

        @pl.when(kk == nk - 1)
        def _():
            write(acc_ref[...])

    return pl.pallas_call(
        body, name=name, grid=grid, in_specs=in_specs, out_specs=o_spec, out_shape=out_shape,
        scratch_shapes=[pltpu.VMEM((tm, tn), F32)] if nk > 1 else [],
        compiler_params=_params(("parallel", "parallel", "arbitrary")),
    )(*args)


def _row_grid(bsz, seq, ts):
    ns = seq // ts
    return (bsz, ns), ns


def _rows(ts, width, ns, col=0):
    return pl.BlockSpec((ts, width), lambda b, s: (b * ns + s, col))


def _mod_spec():
    return pl.BlockSpec((None, 9, D_MODEL), lambda b, s: (b, 0, 0))


def _vec_spec(rows=1, width=D_MODEL):
    return pl.BlockSpec((rows, width), lambda b, s: (0, 0))


def _silu_pad(c):
    bsz = c.shape[0]

    def body(c_ref, o_ref):
        o_ref[...] = jnp.zeros_like(o_ref)
        o_ref[0:bsz, :] = _silu(c_ref[...]).astype(BF16)

    return pl.pallas_call(body, name="silu_pad", out_shape=jax.ShapeDtypeStruct((16, D_MODEL), BF16))(c)


def _mod_in(x, mod, bsz, seq, sub):
    ts = _pick(seq, (512, 256, 128))
    grid, ns = _row_grid(bsz, seq, ts)

    def body(x_ref, mod_ref, u_ref):
        sh = mod_ref[3 * sub:3 * sub + 1, :]
        sc = mod_ref[3 * sub + 1:3 * sub + 2, :]
        u_ref[...] = (x_ref[...] * (1.0 + sc) + sh).astype(BF16)

    return pl.pallas_call(
        body, name=f"mod_in{sub}", grid=grid, in_specs=[_rows(ts, D_MODEL, ns), _mod_spec()],
        out_specs=_rows(ts, D_MODEL, ns), out_shape=jax.ShapeDtypeStruct(x.shape, BF16),
        compiler_params=_params(("parallel", "parallel")),
    )(x, mod)


_FFN_TN = D_FF // 2


def _ffn_up_act(u, w_gu, name):
    t = u.shape[0]
    tm = _pick(t, (512, 256, 128))
    tn = _FFN_TN

    def body(u_ref, wa_ref, wg_ref, h_ref, p_ref):
        u_v = u_ref[...]
        a = jnp.dot(u_v, wa_ref[...], preferred_element_type=F32)
        g = jnp.dot(u_v, wg_ref[...], preferred_element_type=F32)
        h_ref[0] = a.astype(BF16)
        h_ref[1] = g.astype(BF16)
        p_ref[...] = (_silu(a) * g).astype(BF16)

    return pl.pallas_call(
        body, name=name, grid=(2, t // tm),
        in_specs=[pl.BlockSpec((tm, D_MODEL), lambda j, i: (i, 0)),
                  pl.BlockSpec((None, D_MODEL, tn), lambda j, i: (j, 0, 0)),
                  pl.BlockSpec((None, D_MODEL, tn), lambda j, i: (j + 2, 0, 0))],
        out_specs=[pl.BlockSpec((2, tm, tn), lambda j, i: (0, i, j)),
                   pl.BlockSpec((tm, tn), lambda j, i: (i, j))],
        out_shape=[jax.ShapeDtypeStruct((2, t, D_FF), BF16), jax.ShapeDtypeStruct((t, D_FF), BF16)],
        compiler_params=_params(("parallel", "parallel")),
    )(u, w_gu, w_gu)


def _ffn_down_bwd_act(df, w_down, h, name):
    t = df.shape[0]
    tm = _pick(t, (512, 256, 128))
    tn = _FFN_TN

    def body(df_ref, wd_ref, h_ref, dh_ref):
        dp = lax.dot_general(df_ref[...], wd_ref[...], _DIMS["nt"], preferred_element_type=F32)
        a = h_ref[0].astype(F32)
        g = h_ref[1].astype(F32)
        dh_ref[0] = (dp * g * _dsilu(a)).astype(BF16)
        dh_ref[1] = (dp * _silu(a)).astype(BF16)

    blk = pl.BlockSpec((2, tm, tn), lambda j, i: (0, i, j))
    return pl.pallas_call(
        body, name=name, grid=(2, t // tm),
        in_specs=[pl.BlockSpec((tm, D_MODEL), lambda j, i: (i, 0)),
                  pl.BlockSpec((tn, D_MODEL), lambda j, i: (j, 0)), blk],
        out_specs=blk, out_shape=jax.ShapeDtypeStruct((2, t, D_FF), BF16),
        compiler_params=_params(("parallel", "parallel")),
    )(df, w_down, h)


def _res_ln_fwd(x, f, mod, ln_g, ln_b, bsz, seq, sub, weight, target=None):
    ts = _pick(seq, (256, 128))
    grid, ns = _row_grid(bsz, seq, ts)
    last = target is not None

    def body(*refs):
        x_ref, f_ref, mod_ref, g_ref, b_ref = refs[:5]
        gate = mod_ref[3 * sub + 2:3 * sub + 3, :]
        r = ALPHA * x_ref[...] + gate * (weight * f_ref[...])
        xhat, _ = _ln_stats(r)
        xo = xhat * g_ref[...] + b_ref[...]
        if last:
            t_ref, r_ref, dy_ref, loss_ref = refs[5:]
            diff = xo - t_ref[...]
            dy_ref[...] = diff * (1.0 / D_MODEL)
            part = 0.5 * jnp.sum(jnp.mean(diff * diff, axis=-1, keepdims=True), axis=0, keepdims=True)

            @pl.when((pl.program_id(0) == 0) & (pl.program_id(1) == 0))
            def _():
                loss_ref[...] = jnp.zeros_like(loss_ref)

            loss_ref[...] += jnp.broadcast_to(part, loss_ref.shape)
        else:
            r_ref, xo_ref, u_ref = refs[5:]
            xo_ref[...] = xo
            sh = mod_ref[3 * sub + 3:3 * sub + 4, :]
            sc = mod_ref[3 * sub + 4:3 * sub + 5, :]
            u_ref[...] = (xo * (1.0 + sc) + sh).astype(BF16)
        r_ref[...] = r

    row = _rows(ts, D_MODEL, ns)
    in_specs = [row, row, _mod_spec(), _vec_spec(), _vec_spec()]
    args = [x, f, mod, ln_g, ln_b]
    if last:
        in_specs.append(row)
        args.append(target)
        out_specs = [row, row, _vec_spec(8, LANES)]
        out_shape = [jax.ShapeDtypeStruct(x.shape, F32), jax.ShapeDtypeStruct(x.shape, F32),
                     jax.ShapeDtypeStruct((8, LANES), F32)]
        sem = ("arbitrary", "arbitrary")
    else:
        out_specs = [row, row, row]
        out_shape = [jax.ShapeDtypeStruct(x.shape, F32), jax.ShapeDtypeStruct(x.shape, F32),
                     jax.ShapeDtypeStruct(x.shape, BF16)]
        sem = ("parallel", "parallel")
    return pl.pallas_call(
        body, name=f"res_ln_fwd{sub}", grid=grid, in_specs=in_specs, out_specs=out_specs, out_shape=out_shape,
        compiler_params=_params(sem),
    )(*args)


def _res_ln_bwd(r, dxo, f, mod, ln_g, bsz, seq, sub, weight, above=None):
    ts = _pick(seq, (256, 128))
    grid, ns = _row_grid(bsz, seq, ts)
    folded = above is not None

    def body(*refs):
        r_ref, dxo_ref, f_ref, mod_ref, g_ref = refs[:5]
        dxres_ref, df_ref, lns_ref, gs_ref = refs[-5:-1] if folded else refs[-4:]
        b, s = pl.program_id(0), pl.program_id(1)
        gate = mod_ref[3 * sub + 2:3 * sub + 3, :]
        xhat, rstd = _ln_stats(r_ref[...])
        dxo_v = dxo_ref[...]
        if folded:
            du_v = refs[5][...]
            ms_ref = refs[-1]
            dxo_v = dxo_v + du_v * (1.0 + mod_ref[3 * sub + 4:3 * sub + 5, :])

            @pl.when(s == 0)
            def _():
                ms_ref[...] = jnp.zeros_like(ms_ref)

            ms_ref[0:1, :] += _colsum(du_v)
            ms_ref[1:2, :] += _colsum(du_v * refs[6][...])
        dxhat = dxo_v * g_ref[...]
        m1 = jnp.mean(dxhat, axis=-1, keepdims=True)
        m2 = jnp.mean(dxhat * xhat, axis=-1, keepdims=True)
        dr = rstd * (dxhat - m1 - xhat * m2)
        dxres_ref[...] = ALPHA * dr
        df_ref[...] = (dr * (gate * weight)).astype(BF16)

        @pl.when((b == 0) & (s == 0))
        def _():
            lns_ref[...] = jnp.zeros_like(lns_ref)

        @pl.when(s == 0)
        def _():
            gs_ref[...] = jnp.zeros_like(gs_ref)

        lns_ref[0:1, :] += _colsum(dxo_v * xhat)
        lns_ref[1:2, :] += _colsum(dxo_v)
        gs_ref[0:1, :] += _colsum(dr * (weight * f_ref[...]))

    row = _rows(ts, D_MODEL, ns)
    per_sample = pl.BlockSpec((None, 8, D_MODEL), lambda b, s: (b, 0, 0))
    stats = jax.ShapeDtypeStruct((bsz, 8, D_MODEL), F32)
    extra = 1 if folded else 0
    return pl.pallas_call(
        body, name=f"res_ln_bwd{sub}", grid=grid,
        in_specs=[row, row, row, _mod_spec(), _vec_spec()] + [row, row] * extra,
        out_specs=[row, row, _vec_spec(8), per_sample] + [per_sample] * extra,
        out_shape=[jax.ShapeDtypeStruct(r.shape, F32), jax.ShapeDtypeStruct(r.shape, BF16),
                   jax.ShapeDtypeStruct((8, D_MODEL), F32), stats] + [stats] * extra,
        compiler_params=_params(("arbitrary", "arbitrary")),
    )(r, dxo, f, mod, ln_g, *(above or ()))


def _mod_bwd(dxres, du, x, mod, bsz, seq, sub):
    ts = _pick(seq, (256, 128))
    grid, ns = _row_grid(bsz, seq, ts)

    def body(dxres_ref, du_ref, x_ref, mod_ref, dx_ref, st_ref):
        s = pl.program_id(1)
        sc = mod_ref[3 * sub + 1:3 * sub + 2, :]
        du_v = du_ref[...]
        dx_ref[...] = dxres_ref[...] + du_v * (1.0 + sc)

        @pl.when(s == 0)
        def _():
            st_ref[...] = jnp.zeros_like(st_ref)

        st_ref[0:1, :] += _colsum(du_v)
        st_ref[1:2, :] += _colsum(du_v * x_ref[...])

    row = _rows(ts, D_MODEL, ns)
    return pl.pallas_call(
        body, name=f"mod_bwd{sub}", grid=grid, in_specs=[row, row, row, _mod_spec()],
        out_specs=[row, pl.BlockSpec((None, 8, D_MODEL), lambda b, s: (b, 0, 0))],
        out_shape=[jax.ShapeDtypeStruct(x.shape, F32), jax.ShapeDtypeStruct((bsz, 8, D_MODEL), F32)],
        compiler_params=_params(("parallel", "arbitrary")),
    )(dxres, du, x, mod)


_COL_GLU_A, _COL_GLU_B, _COL_GATE_A, _COL_GATE_B = 3, 4, 5, 6


def _merge_fwd(proj, ya, cs, w_sb, w_co):
    t = ya.shape[0]
    tm = _pick(t, (512, 256, 128))

    def body(ga_ref, gb_ref, ya_ref, cs_ref, wsb_ref, wco_ref, m_ref, ysb_ref, yc_ref):
        ysb = jnp.dot(ya_ref[...], wsb_ref[...], preferred_element_type=F32)
        yc = jnp.dot(cs_ref[...], wco_ref[...], preferred_element_type=F32)
        ysb_ref[...] = ysb
        yc_ref[...] = yc
        m_ref[...] = (_sigmoid(ga_ref[...]) * ysb + _sigmoid(gb_ref[...]) * yc).astype(BF16)

    row = pl.BlockSpec((tm, D_MODEL), lambda i: (i, 0))
    full = pl.BlockSpec((D_MODEL, D_MODEL), lambda i: (0, 0))
    return pl.pallas_call(
        body, name="merge_fwd", grid=(t // tm,),
        in_specs=[pl.BlockSpec((tm, D_MODEL), lambda i: (i, _COL_GATE_A)),
                  pl.BlockSpec((tm, D_MODEL), lambda i: (i, _COL_GATE_B)), row, row, full, full],
        out_specs=[row, row, row],
        out_shape=[jax.ShapeDtypeStruct((t, D_MODEL), BF16), jax.ShapeDtypeStruct((t, D_MODEL), F32),
                   jax.ShapeDtypeStruct((t, D_MODEL), F32)],
        compiler_params=_params(("parallel",)),
    )(proj, proj, ya, cs, w_sb, w_co)


def _merge_bwd(proj, ysb, yconv, do2, w_out):
    t = ysb.shape[0]
    tm = _pick(t, (512, 256, 128))

    def body(ga_ref, gb_ref, ysb_ref, yc_ref, do_ref, w_ref, dysb_ref, dyc_ref, dg_ref):
        dm = lax.dot_general(do_ref[...], w_ref[...], _DIMS["nt"], preferred_element_type=F32)
        sa = _sigmoid(ga_ref[...])
        sb = _sigmoid(gb_ref[...])
        dysb_ref[...] = (dm * sa).astype(BF16)
        dyc_ref[...] = (dm * sb).astype(BF16)
        dg_ref[:, :D_MODEL] = (dm * ysb_ref[...] * sa * (1.0 - sa)).astype(BF16)
        dg_ref[:, D_MODEL:] = (dm * yc_ref[...] * sb * (1.0 - sb)).astype(BF16)

    row = pl.BlockSpec((tm, D_MODEL), lambda i: (i, 0))
    return pl.pallas_call(
        body, name="merge_bwd", grid=(t // tm,),
        in_specs=[pl.BlockSpec((tm, D_MODEL), lambda i: (i, _COL_GATE_A)),
                  pl.BlockSpec((tm, D_MODEL), lambda i: (i, _COL_GATE_B)), row, row, row,
                  pl.BlockSpec((D_MODEL, D_MODEL), lambda i: (0, 0))],
        out_specs=[row, row, pl.BlockSpec((tm, 2 * D_MODEL), lambda i: (i, 0))],
        out_shape=[jax.ShapeDtypeStruct((t, D_MODEL), BF16), jax.ShapeDtypeStruct((t, D_MODEL), BF16),
                   jax.ShapeDtypeStruct((t, 2 * D_MODEL), BF16)],
        compiler_params=_params(("parallel",)),
    )(proj, proj, ysb, yconv, do2, w_out)


_CONV_ROWS = 128


def _halo_prev(tt, ns, col):
    r = tt // HALO
    return pl.BlockSpec((HALO, D_MODEL), lambda b, s: (jnp.maximum((b * ns + s) * r - 1, 0), col))


def _halo_next(tt, ns, nblk, col):
    r = tt // HALO
    return pl.BlockSpec((HALO, D_MODEL), lambda b, s: (jnp.minimum((b * ns + s + 1) * r, nblk - 1), col))


def _windows(pad_ref, stage_ref, tt, offsets, cols):
    for r in range(8):
        mine = [o for o in offsets if o % 8 == r]
        if not mine:
            continue
        n = max(mine) - r + tt
        stage_ref[0:n, cols] = pad_ref[r:r + n, cols]
        for o in mine:
            yield o, stage_ref[o - r:o - r + tt, cols]


def _column_chunks():
    return [slice(c, c + LANES) for c in range(0, D_MODEL, LANES)]


def _fill_hc(hpad, a_ref, b_ref, ha_ref, hb_ref, s):
    halo = ha_ref[...] * _sigmoid(hb_ref[...])
    hpad[0:HALO, :] = jnp.where(s > 0, halo, 0.0)
    hpad[HALO:, :] = a_ref[...] * _sigmoid(b_ref[...])


def _conv_fwd(proj, conv_w, conv_b, ln_g, ln_b, bsz, seq):
    tt = _CONV_ROWS
    grid, ns = _row_grid(bsz, seq, tt)
    off = HALO - (CONV_TAPS - 1)

    def body(a_ref, b_ref, ha_ref, hb_ref, w_ref, cb_ref, g_ref, bb_ref, cs_ref, cv_ref, hpad, stage):
        _fill_hc(hpad, a_ref, b_ref, ha_ref, hb_ref, pl.program_id(1))
        for cols in _column_chunks():
            acc = jnp.zeros((tt, LANES), F32)
            for o, win in _windows(hpad, stage, tt, [off + j for j in range(CONV_TAPS)], cols):
                acc = acc + w_ref[o - off:o - off + 1, cols] * win
            cv_ref[:, cols] = acc + cb_ref[:, cols]
        xhat, _ = _ln_stats(cv_ref[...])
        cs_ref[...] = _silu(xhat * g_ref[...] + bb_ref[...]).astype(BF16)

    row = _rows(tt, D_MODEL, ns)
    t = proj.shape[0]
    return pl.pallas_call(
        body, name="conv_fwd", grid=grid,
        in_specs=[_rows(tt, D_MODEL, ns, _COL_GLU_A), _rows(tt, D_MODEL, ns, _COL_GLU_B),
                  _halo_prev(tt, ns, _COL_GLU_A), _halo_prev(tt, ns, _COL_GLU_B),
                  _vec_spec(32), _vec_spec(), _vec_spec(), _vec_spec()],
        out_specs=[row, row],
        out_shape=[jax.ShapeDtypeStruct((t, D_MODEL), BF16), jax.ShapeDtypeStruct((t, D_MODEL), F32)],
        scratch_shapes=[pltpu.VMEM((HALO + tt, D_MODEL), F32), pltpu.VMEM((HALO + tt, D_MODEL), F32)],
        compiler_params=_params(("parallel", "parallel")),
    )(proj, proj, proj, proj, conv_w, conv_b, ln_g, ln_b)


def _conv_bwd_ln(dcs, cv, ln_g, ln_b, bsz, seq):
    ts = _pick(seq, (256, 128))
    grid, ns = _row_grid(bsz, seq, ts)

    def body(dcs_ref, cv_ref, g_ref, b_ref, dcv_ref, st_ref):
        xhat, rstd = _ln_stats(cv_ref[...])
        cl = xhat * g_ref[...] + b_ref[...]
        dcl = dcs_ref[...] * _dsilu(cl)
        dxhat = dcl * g_ref[...]
        m1 = jnp.mean(dxhat, axis=-1, keepdims=True)
        m2 = jnp.mean(dxhat * xhat, axis=-1, keepdims=True)
        dcv = rstd * (dxhat - m1 - xhat * m2)
        dcv_ref[...] = dcv

        @pl.when((pl.program_id(0) == 0) & (pl.program_id(1) == 0))
        def _():
            st_ref[...] = jnp.zeros_like(st_ref)

        st_ref[0:1, :] += _colsum(dcl * xhat)
        st_ref[1:2, :] += _colsum(dcl)
        st_ref[2:3, :] += _colsum(dcv)

    row = _rows(ts, D_MODEL, ns)
    return pl.pallas_call(
        body, name="conv_bwd_ln", grid=grid, in_specs=[row, row, _vec_spec(), _vec_spec()],
        out_specs=[row, _vec_spec(8)],
        out_shape=[jax.ShapeDtypeStruct(cv.shape, F32), jax.ShapeDtypeStruct((8, D_MODEL), F32)],
        compiler_params=_params(("arbitrary", "arbitrary")),
    )(dcs, cv, ln_g, ln_b)


def _conv_bwd_taps(proj, dcv, conv_w, bsz, seq):
    tt = _CONV_ROWS
    grid, ns = _row_grid(bsz, seq, tt)
    off = HALO - (CONV_TAPS - 1)
    nblk = proj.shape[0] // HALO

    def body(a_ref, b_ref, ha_ref, hb_ref, d_ref, dn_ref, w_ref, dglu_ref, dw_ref, hpad, dpad, stage):
        s = pl.program_id(1)
        _fill_hc(hpad, a_ref, b_ref, ha_ref, hb_ref, s)
        dpad[0:tt, :] = d_ref[...]
        dpad[tt:, :] = jnp.where(s < ns - 1, dn_ref[...], 0.0)

        @pl.when((pl.program_id(0) == 0) & (s == 0))
        def _():
            dw_ref[...] = jnp.zeros_like(dw_ref)

        for cols in _column_chunks():
            dcv = d_ref[:, cols]
            dhc = jnp.zeros((tt, LANES), F32)
            for o, win in _windows(dpad, stage, tt, list(range(CONV_TAPS)), cols):
                j = CONV_TAPS - 1 - o
                dhc = dhc + w_ref[j:j + 1, cols] * win
            for o, win in _windows(hpad, stage, tt, [off + j for j in range(CONV_TAPS)], cols):
                dw_ref[o - off:o - off + 1, cols] += _colsum(dcv * win)
            sb = _sigmoid(b_ref[:, cols])
            dglu_ref[:, cols] = (dhc * sb).astype(BF16)
            dglu_ref[:, slice(D_MODEL + cols.start, D_MODEL + cols.stop)] = (
                dhc * a_ref[:, cols] * sb * (1.0 - sb)).astype(BF16)

    t = proj.shape[0]
    return pl.pallas_call(
        body, name="conv_bwd_taps", grid=grid,
        in_specs=[_rows(tt, D_MODEL, ns, _COL_GLU_A), _rows(tt, D_MODEL, ns, _COL_GLU_B),
                  _halo_prev(tt, ns, _COL_GLU_A), _halo_prev(tt, ns, _COL_GLU_B),
                  _rows(tt, D_MODEL, ns), _halo_next(tt, ns, nblk, 0), _vec_spec(32)],
        out_specs=[_rows(tt, 2 * D_MODEL, ns), _vec_spec(32)],
        out_shape=[jax.ShapeDtypeStruct((t, 2 * D_MODEL), BF16), jax.ShapeDtypeStruct((32, D_MODEL), F32)],
        scratch_shapes=[pltpu.VMEM((HALO + tt, D_MODEL), F32), pltpu.VMEM((tt + HALO, D_MODEL), F32),
                        pltpu.VMEM((HALO + tt, D_MODEL), F32)],
        compiler_params=_params(("arbitrary", "arbitrary")),
    )(proj, proj, proj, proj, dcv, dcv, conv_w)


_NT = (((1,), (1,)), ((), ()))
_TN = (((0,), (0,)), ((), ()))


def _dot(a, b, dims=None):
    if dims is None:
        return jnp.dot(a, b, preferred_element_type=F32)
    return lax.dot_general(a, b, dims, preferred_element_type=F32)


def _tri_dot(v, tri2):
    hi = v.astype(BF16)
    lo = (v - hi.astype(F32)).astype(BF16)
    return _dot(jnp.concatenate([hi, lo], axis=1), tri2)


def _tri2(mask):
    t = mask.astype(BF16)
    return jnp.concatenate([t, t], axis=0)


def _softplus_parts(z):
    t = jnp.exp(-jnp.abs(z))
    den = 1.0 + t
    return jnp.maximum(z, 0.0) + jnp.log(den), t, den


def _attn_fwd(proj, bsz, seq):
    blk = ATT_BLOCK
    nq = seq // blk
    n_pairs = D_MODEL // LANES

    def body(q_ref, k_ref, v_ref, y_ref, rt_ref, zr_buf, ns_buf, run_buf, acc_buf):
        qi = pl.program_id(2)
        lane = lax.broadcasted_iota(jnp.int32, (blk, LANES), 1)
        first = lane < HEAD_DIM
        q2 = q_ref[...] * 0.125
        q_heads = (jnp.where(first, q2, 0.0).astype(BF16), jnp.where(first, 0.0, q2).astype(BF16))
        rr = lax.broadcasted_iota(jnp.int32, (blk, blk), 0)
        cc = lax.broadcasted_iota(jnp.int32, (blk, blk), 1)
        tri_ge = _tri2(rr >= cc)
        causal = cc < rr

        def scores(kb, slot, masked, heads=(0, 1)):
            k_blk = k_ref[pl.ds(pl.multiple_of(kb * blk, blk), blk), :].astype(BF16)
            for h in heads:
                z = _dot(q_heads[h], k_blk, _NT)
                if masked:
                    z = jnp.where(causal, z, -1e30)
                sp, _, _ = _softplus_parts(z)
                neg = -sp
                zr_buf[slot, h] = z + _tri_dot(neg, tri_ge)
                ns_buf[slot, h] = jnp.sum(neg, axis=1, keepdims=True)

        def weigh(kb, slot, heads=(0, 1)):
            v_blk = v_ref[pl.ds(pl.multiple_of(kb * blk, blk), blk), :].astype(BF16)
            for h in heads:
                run = run_buf[h]
                w = jnp.exp(zr_buf[slot, h] + run)
                acc_buf[h] += _dot(w.astype(BF16), v_blk)
                run_buf[h] = run + ns_buf[slot, h]

        def step(kb_next, kb, slot):
            for h in range(2):
                scores(kb_next, 1 - slot, False, (h,))
                weigh(kb, slot, (h,))

        run_buf[...] = jnp.zeros_like(run_buf)
        acc_buf[...] = jnp.zeros_like(acc_buf)
        scores(qi, 0, True)

        def two_steps(p, carry):
            t = 2 * p
            step(qi - t - 1, qi - t, 0)
            step(qi - t - 2, qi - t - 1, 1)
            return carry

        lax.fori_loop(0, qi // 2, two_steps, 0)

        @pl.when(qi % 2 == 1)
        def _():
            step(0, 1, 0)
            weigh(0, 1)

        @pl.when(qi % 2 == 0)
        def _():
            weigh(0, 0)

        y_ref[...] = jnp.where(first, acc_buf[0], acc_buf[1]).astype(BF16)
        rt_ref[...] = jnp.where(first, jnp.broadcast_to(run_buf[0], (blk, LANES)),
                                jnp.broadcast_to(run_buf[1], (blk, LANES)))

    t = proj.shape[0]
    q_spec = pl.BlockSpec((blk, LANES), lambda b, p, i: (b * nq + i, p))
    return pl.pallas_call(
        body, name="attn_fwd", grid=(bsz, n_pairs, nq),
        in_specs=[q_spec,
                  pl.BlockSpec((seq, LANES), lambda b, p, i: (b, n_pairs + p)),
                  pl.BlockSpec((seq, LANES), lambda b, p, i: (b, 2 * n_pairs + p))],
        out_specs=[q_spec, q_spec],
        out_shape=[jax.ShapeDtypeStruct((t, D_MODEL), BF16), jax.ShapeDtypeStruct((t, D_MODEL), F32)],
        scratch_shapes=[pltpu.VMEM((2, 2, blk, blk), F32), pltpu.VMEM((2, 2, blk, 1), F32),
                        pltpu.VMEM((2, blk, 1), F32), pltpu.VMEM((2, blk, LANES), F32)],
        compiler_params=_params(("parallel", "parallel", "arbitrary")),
    )(proj, proj, proj)


def _attn_bwd(proj, rtot, dy, bsz, seq):
    blk = ATT_BLOCK
    nq = seq // blk
    n_pairs = D_MODEL // LANES

    def body(q_ref, k_ref, v_ref, dy_ref, rt_ref, dq_ref, dk_ref, dv_ref, dk_acc, dv_acc,
             a_buf, sig_buf, dw_buf, ns_buf, pre_buf, es_buf, dq_buf):
        qi = pl.program_id(2)

        @pl.when(qi == 0)
        def _():
            dk_acc[...] = jnp.zeros_like(dk_acc)
            dv_acc[...] = jnp.zeros_like(dv_acc)

        lane = lax.broadcasted_iota(jnp.int32, (blk, LANES), 1)
        first = lane < HEAD_DIM
        head_row = lax.broadcasted_iota(jnp.int32, (LANES, blk), 0) < HEAD_DIM
        q2 = q_ref[...] * 0.125
        q_rows = (jnp.where(first, q2, 0.0).astype(BF16), jnp.where(first, 0.0, q2).astype(BF16))
        q_t = q2.T
        q_heads = (jnp.where(head_row, q_t, 0.0).astype(BF16), jnp.where(head_row, 0.0, q_t).astype(BF16))
        dy2 = dy_ref[...].astype(F32)
        dy_rows = (jnp.where(first, dy2, 0.0).astype(BF16), jnp.where(first, 0.0, dy2).astype(BF16))
        dy_t = dy2.T
        dy_heads = (jnp.where(head_row, dy_t, 0.0).astype(BF16), jnp.where(head_row, 0.0, dy_t).astype(BF16))
        rt_t = rt_ref[...].T
        rt = (rt_t[0:1, :], rt_t[HEAD_DIM:HEAD_DIM + 1, :])
        rr = lax.broadcasted_iota(jnp.int32, (blk, blk), 0)
        cc = lax.broadcasted_iota(jnp.int32, (blk, blk), 1)
        lower = (cc < rr).astype(BF16)
        lower_eq = (cc <= rr).astype(BF16)
        tri_lt = jnp.concatenate([lower, lower], axis=1)
        tri_le = jnp.concatenate([lower_eq, lower_eq], axis=1)
        causal = rr < cc

        def tri_left(tri2, v):
            hi = v.astype(BF16)
            lo = (v - hi.astype(F32)).astype(BF16)
            return _dot(tri2, jnp.concatenate([hi, lo], axis=0))

        def scores(kb, slot, heads=(0, 1)):
            rows = pl.ds(pl.multiple_of(kb * blk, blk), blk)
            k_blk = k_ref[rows, :].astype(BF16)
            v_blk = v_ref[rows, :].astype(BF16)
            keep = jnp.logical_or(causal, kb < qi)
            for h in heads:
                z = jnp.where(keep, _dot(k_blk, q_heads[h]), -1e30)
                sp, t, den = _softplus_parts(z)
                neg = -sp
                a_buf[slot, h] = z - tri_left(tri_lt, neg)
                sig_buf[slot, h] = jnp.where(z >= 0, 1.0, t) / den
                ns_buf[slot, h] = jnp.sum(neg, axis=0, keepdims=True)
                dw_buf[slot, h] = _dot(v_blk, dy_heads[h])

        def finish(kb, slot, heads=(0, 1)):
            rows = pl.ds(pl.multiple_of(kb * blk, blk), blk)
            k_t = k_ref[rows, :].T.astype(BF16)
            for h in heads:
                pre, esum = pre_buf[h], es_buf[h]
                w = jnp.exp(a_buf[slot, h] + (rt[h] - pre))
                e = dw_buf[slot, h] * w
                dz = e - sig_buf[slot, h] * (esum + tri_left(tri_le, e))
                pre_buf[h] = pre + ns_buf[slot, h]
                es_buf[h] = esum + jnp.sum(e, axis=0, keepdims=True)
                dzb = dz.astype(BF16)
                dq_buf[h] += _dot(k_t, dzb)
                dk_acc[rows, :] += _dot(dzb, q_rows[h])
                dv_acc[rows, :] += _dot(w.astype(BF16), dy_rows[h])

        def step(kb_next, kb, slot):
            scores(kb_next, 1 - slot)
            finish(kb, slot)

        pre_buf[...] = jnp.zeros_like(pre_buf)
        es_buf[...] = jnp.zeros_like(es_buf)
        dq_buf[...] = jnp.zeros_like(dq_buf)
        scores(0, 0)

        def two_steps(p, carry):
            t = 2 * p
            step(t + 1, t, 0)
            step(t + 2, t + 1, 1)
            return carry

        lax.fori_loop(0, qi // 2, two_steps, 0)

        @pl.when(qi % 2 == 1)
        def _():
            step(qi, qi - 1, 0)
            finish(qi, 1)

        @pl.when(qi % 2 == 0)
        def _():
            finish(qi, 0)

        dq_ref[...] = (jnp.where(head_row, dq_buf[0], dq_buf[1]).T * 0.125).astype(BF16)

        @pl.when(qi == nq - 1)
        def _():
            dk_ref[...] = dk_acc[...].astype(BF16)
            dv_ref[...] = dv_acc[...].astype(BF16)

    t = proj.shape[0]
    q_spec = pl.BlockSpec((blk, LANES), lambda b, p, i: (b * nq + i, p))
    kv_out = pl.BlockSpec((seq, LANES), lambda b, p, i: (b, p))
    out = jax.ShapeDtypeStruct((t, D_MODEL), BF16)
    return pl.pallas_call(
        body, name="attn_bwd", grid=(bsz, n_pairs, nq),
        in_specs=[q_spec,
                  pl.BlockSpec((seq, LANES), lambda b, p, i: (b, n_pairs + p)),
                  pl.BlockSpec((seq, LANES), lambda b, p, i: (b, 2 * n_pairs + p)),
                  q_spec, q_spec],
        out_specs=[q_spec, kv_out, kv_out], out_shape=[out, out, out],
        scratch_shapes=[pltpu.VMEM((seq, LANES), F32), pltpu.VMEM((seq, LANES), F32),
                        pltpu.VMEM((2, 2, blk, blk), F32), pltpu.VMEM((2, 2, blk, blk), F32),
                        pltpu.VMEM((2, 2, blk, blk), F32), pltpu.VMEM((2, 2, 1, blk), F32),
                        pltpu.VMEM((2, 1, blk), F32), pltpu.VMEM((2, 1, blk), F32),
                        pltpu.VMEM((2, LANES, blk), F32)],
        compiler_params=_params(("parallel", "parallel", "arbitrary")),
    )(proj, proj, proj, dy, rtot)


def _adamw(w, g, m, v, name, after):
    rows, cols = w.shape
    tr = _pick(rows, (256, 352, 128, 64, 32, 16, 8))
    c1 = 1.0 - ADAM_B1 ** ADAM_STEP
    c2 = 1.0 - ADAM_B2 ** ADAM_STEP

    def body(w_ref, g_ref, m_ref, v_ref, after_ref, d_ref, nm_ref, nv_ref):
        g_v = g_ref[...]
        nm = ADAM_B1 * m_ref[...] + (1.0 - ADAM_B1) * g_v
        nv = ADAM_B2 * v_ref[...] + (1.0 - ADAM_B2) * (g_v * g_v)
        nm_ref[...] = nm
        nv_ref[...] = nv
        d_ref[...] = -ADAM_LR * ((nm / c1) / (jnp.sqrt(nv / c2) + ADAM_EPS) + ADAM_WD * w_ref[...])

    spec = pl.BlockSpec((tr, cols), lambda i: (i, 0))
    shape = jax.ShapeDtypeStruct(w.shape, F32)
    return pl.pallas_call(
        body, name=name, grid=(rows // tr,), in_specs=[spec] * 4 + [pl.BlockSpec(memory_space=pl.ANY)],
        out_specs=[spec] * 3, out_shape=[shape] * 3,
        compiler_params=_params(("parallel",)),
    )(w, g, m, v, after)


def _ffn_fwd(u, w_gu, w_down, bsz, seq, tag):
    h, p = _ffn_up_act(u, w_gu, f"{tag}_up")
    f = _matmul(p, w_down, mode="nn", out_dtype=F32, name=f"{tag}_down")
    return h, p, f


def _ffn_bwd(df, u, h, p, w_gu, w_down, bsz, seq, tag):
    dh = _ffn_down_bwd_act(df, w_down, h, f"{tag}_ddown")
    g_down = _matmul(p, df, mode="tn", out_dtype=F32, name=f"{tag}_gdown")
    g_gu = _matmul(u, dh, mode="tn", out_dtype=F32, name=f"{tag}_ggu", out_stacked=True)
    du = _matmul(dh, w_gu, mode="nt", out_dtype=F32, name=f"{tag}_dup")
    return du, g_gu, g_down


def _local_step(x, c, target, wts, vecs, fetch=None, early_grads=None):
    wts = dict(wts)
    bsz, seq, _ = x.shape
    t = bsz * seq
    x0 = x.reshape(t, D_MODEL)
    tgt = target.reshape(t, D_MODEL)

    sc = _silu_pad(c)
    mod16 = _matmul(sc, wts["w_ada"], mode="nn", out_dtype=F32, name="ada_fwd", bias=vecs["b_ada"])
    mod = mod16[:bsz].reshape(bsz, 9, D_MODEL)

    u1 = _mod_in(x0, mod, bsz, seq, 0)
    h1, p1 = _ffn_up_act(u1, wts["ffn1_w_gu"], "ffn1_up")
    if fetch is not None:
        wts.update(fetch("down", p1))
    f1 = _matmul(p1, wts["ffn1_w_down"], mode="nn", out_dtype=F32, name="ffn1_down")
    r1, x1, u2 = _res_ln_fwd(x0, f1, mod, vecs["ln1_g"], vecs["ln1_b"], bsz, seq, 0, 0.5)
    if fetch is not None:
        wts.update(fetch("later", r1))

    proj = _matmul(u2, wts["w_in"], mode="nn", out_dtype=F32, name="mix_in")
    ya, rtot = _attn_fwd(proj, bsz, seq)
    cs, cv = _conv_fwd(proj, wts["conv_w"], vecs["conv_b"], vecs["conv_ln_g"], vecs["conv_ln_b"], bsz, seq)
    merged, ysb, yconv = _merge_fwd(proj, ya, cs, wts["w_sb_out"], wts["w_conv_out"])
    o2 = _matmul(merged, wts["w_out"], mode="nn", out_dtype=F32, name="mix_out")
    r2, x2, u3 = _res_ln_fwd(x1, o2, mod, vecs["ln2_g"], vecs["ln2_b"], bsz, seq, 1, 1.0)

    h3, p3, f3 = _ffn_fwd(u3, wts["ffn2_w_gu"], wts["ffn2_w_down"], bsz, seq, "ffn2")
    r3, dy, loss_blk = _res_ln_fwd(x2, f3, mod, vecs["ln3_g"], vecs["ln3_b"], bsz, seq, 2, 0.5, target=tgt)

    grads = {}
    dxres, df, ln3s, g3s = _res_ln_bwd(r3, dy, f3, mod, vecs["ln3_g"], bsz, seq, 2, 0.5)
    du, grads["ffn2_w_gu"], grads["ffn2_w_down"] = _ffn_bwd(
        df, u3, h3, p3, wts["ffn2_w_gu"], wts["ffn2_w_down"], bsz, seq, "ffn2")

    dxres, do2, ln2s, g2s, m3s = _res_ln_bwd(r2, dxres, o2, mod, vecs["ln2_g"], bsz, seq, 1, 1.0, above=(du, x2))
    grads["w_out"] = _matmul(merged, do2, mode="tn", out_dtype=F32, name="mix_out_g")
    dysb, dyconv, dgate = _merge_bwd(proj, ysb, yconv, do2, wts["w_out"])
    dya = _matmul(dysb, wts["w_sb_out"], mode="nt", out_dtype=BF16, name="sb_out_d")
    grads["w_sb_out"] = _matmul(ya, dysb, mode="tn", out_dtype=F32, name="sb_out_g")
    dcs = _matmul(dyconv, wts["w_conv_out"], mode="nt", out_dtype=F32, name="conv_out_d")
    grads["w_conv_out"] = _matmul(cs, dyconv, mode="tn", out_dtype=F32, name="conv_out_g")
    dcv, convs = _conv_bwd_ln(dcs, cv, vecs["conv_ln_g"], vecs["conv_ln_b"], bsz, seq)
    dglu, g_conv_w = _conv_bwd_taps(proj, dcv, wts["conv_w"], bsz, seq)
    dq, dk, dv = _attn_bwd(proj, rtot, dya, bsz, seq)
    dproj = jnp.concatenate([dq, dk, dv, dglu, dgate], axis=1)
    grads["w_in"] = _matmul(u2, dproj, mode="tn", out_dtype=F32, name="mix_in_g", out_stacked=True)
    if early_grads is not None:
        mod = mod + early_grads("later_start", {n: grads.pop(n) for n in list(grads)})
    du = _matmul(dproj, wts["w_in"], mode="nt", out_dtype=F32, name="mix_in_d")
    if early_grads is not None:
        mod = mod + early_grads("later_go", du)

    dxres, df, ln1s, g1s, m2s = _res_ln_bwd(r1, dxres, f1, mod, vecs["ln1_g"], bsz, seq, 0, 0.5, above=(du, x1))
    dh = _ffn_down_bwd_act(df, wts["ffn1_w_down"], h1, "ffn1_ddown")
    grads["ffn1_w_down"] = _matmul(p1, df, mode="tn", out_dtype=F32, name="ffn1_gdown")
    grads["ffn1_w_gu"] = _matmul(u1, dh, mode="tn", out_dtype=F32, name="ffn1_ggu", out_stacked=True)
    if early_grads is not None:
        mod = mod + early_grads("mid_start", {n: grads.pop(n) for n in list(grads)})
    du = _matmul(dh, wts["ffn1_w_gu"], mode="nt", out_dtype=F32, name="ffn1_dup")
    if early_grads is not None:
        mod = mod + early_grads("mid_go", du)
    grad_x, m1s = _mod_bwd(dxres, du, x0, mod, bsz, seq, 0)

    dmod = jnp.stack([m1s[:, 0], m1s[:, 1], g1s[:, 0], m2s[:, 0], m2s[:, 1], g2s[:, 0],
                      m3s[:, 0], m3s[:, 1], g3s[:, 0]], axis=1)
    dmod16 = jnp.zeros((16, 9 * D_MODEL), F32).at[:bsz].set(dmod.reshape(bsz, 9 * D_MODEL))
    grads["w_ada"] = _matmul(sc, dmod16.astype(BF16), mode="tn", out_dtype=F32, name="ada_g", out_stacked=True)

    small = {"dmod": dmod, "ln1": ln1s, "ln2": ln2s, "ln3": ln3s, "conv": convs, "conv_w": g_conv_w,
             "loss": loss_blk}
    return grad_x.reshape(x.shape), grads, small


_HBM = pl.BlockSpec(memory_space=pltpu.HBM)


def _position():
    return lax.axis_index("x"), lax.axis_index("y"), lax.axis_index("c")


def _other_chips(x, y):
    return [(1 - x, y), (x, 1 - y), (1 - x, 1 - y)]


def _cast_into_stack(w_local, chip, name):
    rows, cols = w_local.shape
    tr = _pick(rows, (256, 352, 128, 64, 32, 16))

    def body(chip_ref, w_ref, o_ref):
        o_ref[...] = w_ref[...].astype(BF16)

    return pl.pallas_call(
        body, name=name,
        grid_spec=pltpu.PrefetchScalarGridSpec(
            num_scalar_prefetch=1, grid=(rows // tr,),
            in_specs=[pl.BlockSpec((tr, cols), lambda r, chip_ref: (r, 0))],
            out_specs=pl.BlockSpec((None, tr, cols), lambda r, chip_ref: (chip_ref[0], r, 0))),
        out_shape=jax.ShapeDtypeStruct((N_CHIPS, rows, cols), BF16),
        compiler_params=_params(("parallel",)),
    )(chip, w_local)


def _all_gather_weights(stacks, small):
    n = len(stacks)

    def body(*refs):
        ins, small_in, outs, small_out = refs[:n], refs[n], refs[n + 1:2 * n + 1], refs[2 * n + 1]
        send_sems, recv_sems, fwd_send_sems, fwd_recv_sems, small_sems = refs[2 * n + 2:]
        x, y, c = _position()
        me = 2 * x + y
        chips = _other_chips(x, y)

        def send(i, j):
            px, py = chips[j]
            return pltpu.make_async_remote_copy(
                src_ref=ins[i].at[me, c], dst_ref=outs[i].at[me, c], send_sem=send_sems.at[3 * i + j],
                recv_sem=recv_sems.at[3 * i + j], device_id=(px, py, c), device_id_type=MESH)

        def landed(i, j):
            px, py = chips[j]
            return pltpu.make_async_remote_copy(
                src_ref=ins[i].at[me, c], dst_ref=outs[i].at[2 * px + py, c], send_sem=send_sems.at[3 * i + j],
                recv_sem=recv_sems.at[3 * i + j], device_id=(px, py, c), device_id_type=MESH)

        def forward(i, j, half):
            px, py = chips[j]
            blk = outs[i].at[2 * px + py, half]
            return pltpu.make_async_remote_copy(
                src_ref=blk, dst_ref=blk, send_sem=fwd_send_sems.at[3 * i + j],
                recv_sem=fwd_recv_sems.at[3 * i + j], device_id=(x, y, 1 - c), device_id_type=MESH)

        def small_copy(j, slot):
            px, py = chips[j]
            return pltpu.make_async_remote_copy(
                src_ref=small_in, dst_ref=small_out.at[slot], send_sem=small_sems.at[j],
                recv_sem=small_sems.at[3 + j], device_id=(px, py, c), device_id_type=MESH)

        own_small = pltpu.make_async_copy(small_in, small_out.at[me], small_sems.at[6])
        own_small.start()
        for j in range(3):
            small_copy(j, me).start()
        for i in range(n):
            for j in range(3):
                send(i, j).start()
        for i in range(n):
            for j in range(3):
                landed(i, j).wait_recv()
                forward(i, j, c).start()
        for i in range(n):
            for j in range(3):
                forward(i, j, 1 - c).wait_recv()
        for j, (px, py) in enumerate(chips):
            small_copy(j, 2 * px + py).wait_recv()
        own_small.wait()
        for j in range(3):
            small_copy(j, me).wait_send()
        for i in range(n):
            for j in range(3):
                send(i, j).wait_send()
                forward(i, j, c).wait_send()

    return pl.pallas_call(
        body, name="all_gather_weights",
        out_shape=[jax.ShapeDtypeStruct(s.shape, s.dtype) for s in stacks]
        + [jax.ShapeDtypeStruct((N_CHIPS,) + small.shape, small.dtype)],
        in_specs=[_HBM] * (n + 1), out_specs=[_HBM] * (n + 1),
        input_output_aliases={i: i for i in range(n)},
        scratch_shapes=[pltpu.SemaphoreType.DMA((3 * n,)), pltpu.SemaphoreType.DMA((3 * n,)),
                        pltpu.SemaphoreType.DMA((3 * n,)), pltpu.SemaphoreType.DMA((3 * n,)),
                        pltpu.SemaphoreType.DMA((7,))],
    )(*stacks, small)


_SEM = pl.BlockSpec(memory_space=pltpu.SEMAPHORE)
_DATAFLOW = pltpu.SideEffectType.DATAFLOW_SIDE_EFFECTING


_COPIES = {"gather": 3, "scatter": 3, "swap": N_CHIPS}


def _exchange_plan(kind, src, land):
    x, y, c = _position()
    me = 2 * x + y
    if kind == "swap":
        return [(src.at[k, 1 - c], land.at[k], land.at[k], (x, y, 1 - c)) for k in range(N_CHIPS)]
    plan = []
    for j, (px, py) in enumerate(_other_chips(x, y)):
        if kind == "gather":
            plan.append((src.at[me, c], land.at[me, c], land.at[2 * px + py, c], (px, py, c)))
        else:
            plan.append((src.at[2 * px + py], land.at[j], land.at[j], (px, py, c)))
    return plan


def _exchange_start(kind, srcs, lands, name, after):
    n = len(srcs)
    per = _COPIES[kind]
    in_place = lands is None
    n_in = n if in_place else 2 * n

    def body(*refs):
        src_refs = refs[:n]
        land_refs = src_refs if in_place else refs[n:2 * n]
        send_sems, recv_sems = refs[n_in + 1], refs[n_in + 2]
        token = refs[-1]
        for i in range(n):
            for j, (src, dst, _, to) in enumerate(_exchange_plan(kind, src_refs[i], land_refs[i])):
                pltpu.make_async_remote_copy(
                    src_ref=src, dst_ref=dst, send_sem=send_sems.at[per * i + j], recv_sem=recv_sems.at[per * i + j],
                    device_id=to, device_id_type=MESH).start()
        token[...] = jnp.zeros_like(token)

    operands = list(srcs) + ([] if in_place else list(lands))
    operands = [pltpu.with_memory_space_constraint(o, pltpu.HBM) for o in operands]
    out = pl.pallas_call(
        body, name=name,
        out_shape=[pltpu.SemaphoreType.DMA((per * n,)), pltpu.SemaphoreType.DMA((per * n,))]
        + [pltpu.HBM(o.shape, o.dtype) for o in operands] + [jax.ShapeDtypeStruct((8, LANES), F32)],
        in_specs=[_HBM] * n_in + [pl.BlockSpec(memory_space=pl.ANY)],
        out_specs=[_SEM, _SEM] + [_HBM] * n_in + [pl.BlockSpec(memory_space=pltpu.VMEM)],
        input_output_aliases={i: 2 + i for i in range(n_in)},
        compiler_params=pltpu.CompilerParams(has_side_effects=_DATAFLOW),
    )(*operands, after)
    return out[0], out[1], list(out[2:2 + n_in]), out[-1]


def _exchange_wait(kind, send_sems, recv_sems, thru, in_place, after, name):
    n_in = len(thru)
    n = n_in if in_place else n_in // 2
    per = _COPIES[kind]

    def body(*refs):
        src_refs = refs[:n]
        land_refs = src_refs if in_place else refs[n:2 * n]
        send_sems, recv_sems = refs[n_in], refs[n_in + 1]
        for i in range(n):
            for j, (src, _, here, to) in enumerate(_exchange_plan(kind, src_refs[i], land_refs[i])):
                copy = pltpu.make_async_remote_copy(
                    src_ref=src, dst_ref=here, send_sem=send_sems.at[per * i + j], recv_sem=recv_sems.at[per * i + j],
                    device_id=to, device_id_type=MESH)
                copy.wait_send()
                copy.wait_recv()

    out = pl.pallas_call(
        body, name=name, out_shape=[pltpu.HBM(o.shape, o.dtype) for o in thru],
        in_specs=[_HBM] * n_in + [_SEM, _SEM, pl.BlockSpec(memory_space=pl.ANY)], out_specs=[_HBM] * n_in,
        input_output_aliases={i: i for i in range(n_in)},
        compiler_params=pltpu.CompilerParams(has_side_effects=_DATAFLOW),
    )(*thru, send_sems, recv_sems, after)
    return list(out[:n]), (list(out[:n]) if in_place else list(out[n:]))


def _gather_forward(stacks, name):
    n = len(stacks)

    def body(*refs):
        ins, outs = refs[:n], refs[n:2 * n]
        send_sems, recv_sems = refs[2 * n:]
        x, y, c = _position()
        chips = _other_chips(x, y)

        def copy(i, j, half):
            px, py = chips[j]
            return pltpu.make_async_remote_copy(
                src_ref=ins[i].at[2 * px + py, half], dst_ref=outs[i].at[2 * px + py, half],
                send_sem=send_sems.at[3 * i + j], recv_sem=recv_sems.at[3 * i + j],
                device_id=(x, y, 1 - c), device_id_type=MESH)

        for i in range(n):
            for j in range(3):
                copy(i, j, c).start()
        for i in range(n):
            for j in range(3):
                copy(i, j, 1 - c).wait_recv()
        for i in range(n):
            for j in range(3):
                copy(i, j, c).wait_send()

    return pl.pallas_call(
        body, name=name, out_shape=[jax.ShapeDtypeStruct(s.shape, s.dtype) for s in stacks],
        in_specs=[_HBM] * n, out_specs=[_HBM] * n, input_output_aliases={i: i for i in range(n)},
        scratch_shapes=[pltpu.SemaphoreType.DMA((3 * n,)), pltpu.SemaphoreType.DMA((3 * n,))],
    )(*stacks)


def _pair_add(g, got, place, name):
    _, _, rh, cols = g.shape
    tr = _pick(rh, (256, 176, 128, 64, 32, 16, 8))

    def body(place_ref, g_ref, got_ref, p_ref, own_ref):
        s = g_ref[...] + got_ref[...]
        p_ref[...] = s.astype(BF16)

        @pl.when(pl.program_id(1) == place_ref[1])
        def _():
            own_ref[...] = s

    blk = pl.BlockSpec((None, tr, cols), lambda r, k, place_ref: (k, r, 0))
    return pl.pallas_call(
        body, name=name,
        grid_spec=pltpu.PrefetchScalarGridSpec(
            num_scalar_prefetch=1, grid=(rh // tr, N_CHIPS),
            in_specs=[pl.BlockSpec((None, None, tr, cols), lambda r, k, place_ref: (k, place_ref[0], r, 0)), blk],
            out_specs=[blk, pl.BlockSpec((tr, cols), lambda r, k, place_ref: (r, 0))]),
        out_shape=[jax.ShapeDtypeStruct((N_CHIPS, rh, cols), BF16), jax.ShapeDtypeStruct((rh, cols), F32)],
        compiler_params=_params(("parallel", "arbitrary")),
    )(place, g, got)


def _chip_sum(own, parts, place, name):
    rh, cols = own.shape
    tr = _pick(rh, (256, 176, 128, 64, 32, 16, 8))

    def body(place_ref, own_ref, p_ref, o_ref):
        o_ref[...] = ((own_ref[...] + p_ref[0].astype(F32)) + p_ref[1].astype(F32)) + p_ref[2].astype(F32)

    return pl.pallas_call(
        body, name=name,
        grid_spec=pltpu.PrefetchScalarGridSpec(
            num_scalar_prefetch=1, grid=(rh // tr,),
            in_specs=[pl.BlockSpec((tr, cols), lambda r, place_ref: (r, 0)),
                      pl.BlockSpec((3, tr, cols), lambda r, place_ref: (0, r, 0))],
            out_specs=pl.BlockSpec((None, tr, cols), lambda r, place_ref: (place_ref[0], r, 0))),
        out_shape=jax.ShapeDtypeStruct((2, rh, cols), F32),
        compiler_params=_params(("parallel",)),
    )(place, own, parts)


def _pair_gather(halves, name):
    n = len(halves)

    def body(*refs):
        ins, outs = refs[:n], refs[n:2 * n]
        send_sems, recv_sems = refs[2 * n:]
        x, y, c = _position()

        def send(i):
            return pltpu.make_async_remote_copy(
                src_ref=ins[i].at[c], dst_ref=outs[i].at[c], send_sem=send_sems.at[i], recv_sem=recv_sems.at[i],
                device_id=(x, y, 1 - c), device_id_type=MESH)

        def landed(i):
            return pltpu.make_async_remote_copy(
                src_ref=ins[i].at[c], dst_ref=outs[i].at[1 - c], send_sem=send_sems.at[i], recv_sem=recv_sems.at[i],
                device_id=(x, y, 1 - c), device_id_type=MESH)

        for i in range(n):
            send(i).start()
        for i in range(n):
            landed(i).wait_recv()
        for i in range(n):
            send(i).wait_send()

    return pl.pallas_call(
        body, name=name,
        out_shape=[jax.ShapeDtypeStruct(h.shape, F32) for h in halves],
        in_specs=[_HBM] * n, out_specs=[_HBM] * n,
        input_output_aliases={i: i for i in range(n)},
        scratch_shapes=[pltpu.SemaphoreType.DMA((n,)), pltpu.SemaphoreType.DMA((n,))],
    )(*halves)


_MOD_ROWS = 16


def _small_all_reduce(buf, bsz, after):
    rows, cols = buf.shape
    head = bsz * _MOD_ROWS
    out_rows = rows - head + _MOD_ROWS

    def body(in_ref, after_ref, o_ref, gath, send_sems, recv_sems):
        x, y, c = _position()
        me = 4 * x + 2 * y + c

        def peer(mask):
            return (x ^ (mask >> 2), y ^ ((mask >> 1) & 1), c ^ (mask & 1))

        def copy(mask):
            return pltpu.make_async_remote_copy(
                src_ref=in_ref, dst_ref=gath.at[me], send_sem=send_sems.at[mask - 1],
                recv_sem=recv_sems.at[mask - 1], device_id=peer(mask), device_id_type=MESH)

        def arrival(mask):
            px, py, pc = peer(mask)
            return pltpu.make_async_remote_copy(
                src_ref=in_ref, dst_ref=gath.at[4 * px + 2 * py + pc], send_sem=send_sems.at[mask - 1],
                recv_sem=recv_sems.at[mask - 1], device_id=peer(mask), device_id_type=MESH)

        for mask in range(1, N_DEV):
            copy(mask).start()
        gath[me] = in_ref[...]
        for mask in range(1, N_DEV):
            arrival(mask).wait_recv()
        for mask in range(1, N_DEV):
            copy(mask).wait_send()
        acc = gath[0]
        for d in range(1, N_DEV):
            acc = acc + gath[d]
        mod = acc[0:_MOD_ROWS]
        for s in range(1, bsz):
            mod = mod + acc[s * _MOD_ROWS:(s + 1) * _MOD_ROWS]
        o_ref[0:_MOD_ROWS, :] = mod
        o_ref[_MOD_ROWS:, :] = acc[head:]

    vm = pl.BlockSpec(memory_space=pltpu.VMEM)
    return pl.pallas_call(
        body, name="small_all_reduce", in_specs=[vm, pl.BlockSpec(memory_space=pl.ANY)], out_specs=vm,
        out_shape=jax.ShapeDtypeStruct((out_rows, cols), F32),
        scratch_shapes=[pltpu.VMEM((N_DEV, rows, cols), F32), pltpu.SemaphoreType.DMA((N_DEV - 1,)),
                        pltpu.SemaphoreType.DMA((N_DEV - 1,))],
        compiler_params=pltpu.CompilerParams(vmem_limit_bytes=VMEM_LIMIT),
    )(buf, after)


_COL_SHARDED = ("w_ada", "ffn1_w_gu", "w_in", "ffn2_w_gu")
_ROW_SHARDED = ("ffn1_w_down", "w_sb_out", "w_conv_out", "w_out", "ffn2_w_down")
_NOW = ["w_ada", "ffn1_w_gu"]
_SOON = ["ffn1_w_down"]
_LATER = ["w_in", "w_sb_out", "w_conv_out", "w_out", "ffn2_w_gu", "ffn2_w_down"]
_VECS = ("b_ada", "ln1_g", "ln1_b", "conv_b", "conv_ln_g", "conv_ln_b", "ln2_g", "ln2_b", "ln3_g", "ln3_b")
_WEIGHTS = ("w_ada", "b_ada", "ffn1_w_gu", "ffn1_w_down", "ln1_g", "ln1_b", "w_in", "w_sb_out", "conv_w", "conv_b",
            "conv_ln_g", "conv_ln_b", "w_conv_out", "w_out", "ln2_g", "ln2_b", "ffn2_w_gu", "ffn2_w_down",
            "ln3_g", "ln3_b")


def _step(x, c, target, w, m, v):
    bsz = x.shape[0]
    chip = 2 * lax.axis_index("x") + lax.axis_index("y")
    core = lax.axis_index("c")

    chip_arr = jnp.reshape(chip, (1,)).astype(jnp.int32)
    place = jnp.stack([core, chip]).astype(jnp.int32)

    def stack_of(n):
        rows, cols = w[n].shape[1:]
        return _cast_into_stack(w[n][0], chip_arr, f"cast_{n}").reshape(N_CHIPS, 2, rows // 2, cols)

    def gathered_form(n, g):
        rows, cols = w[n].shape[1:]
        return g.reshape(N_CHIPS, rows, cols) if n in _COL_SHARDED else g.reshape(N_CHIPS * rows, cols)

    conv_w_local = jnp.pad(w["conv_w"][0], ((0, 1), (0, 0)))
    gathered = _all_gather_weights([stack_of(n) for n in _NOW], conv_w_local)
    wts = {n: gathered_form(n, g) for n, g in zip(_NOW, gathered[:-1])}
    wts["conv_w"] = gathered[-1].transpose(1, 0, 2).reshape(32, D_MODEL)
    pending, behind = {}, gathered[0]
    for stage, names in (("down", _SOON), ("later", _LATER)):
        send, recv, thru, token = _exchange_start(
            "gather", [stack_of(n) for n in names], None, f"gather_start_{stage}", behind)
        pending[stage] = (names, send, recv, thru)
        behind = token
    vecs = {n: w[n] for n in _VECS}
    vecs["b_ada"] = vecs["b_ada"] + behind[0, 0]

    def fetch(stage, after):
        names, send, recv, thru = pending[stage]
        landed, _ = _exchange_wait("gather", send, recv, thru, True, after, f"gather_wait_{stage}")
        forwarded = _gather_forward(landed, f"gather_forward_{stage}")
        return {n: gathered_form(n, g) for n, g in zip(names, forwarded)}

    groups = {"later": _LATER, "mid": ["ffn1_w_gu", "ffn1_w_down"], "last": ["w_ada"]}
    g_out, updates, state = {}, {}, {}

    def adam(names, after):
        for n in names:
            shape = w[n].shape
            flat = shape[1:] if len(shape) == 3 else shape
            d, nm, nv = _adamw(w[n].reshape(flat), g_out[n].reshape(flat), m[n].reshape(flat), v[n].reshape(flat),
                               f"adamw_{n}", after)
            updates[n] = (g_out[n].reshape(shape), d.reshape(shape), nm.reshape(shape), nv.reshape(shape))
            after = nv
        return after

    def swap_start(tag, grads):
        views = [grads[n].reshape(N_CHIPS, 2, w[n].shape[1] // 2, w[n].shape[2]) for n in groups[tag]]
        lands = [lax.empty((N_CHIPS,) + g.shape[2:], F32) for g in views]
        send, recv, thru, token = _exchange_start("swap", views, lands, f"swap_start_{tag}", place)
        state[tag] = {"swap": (send, recv, thru)}
        return token

    def scatter_start(tag, after):
        views, got = _exchange_wait("swap", *state[tag]["swap"], False, after, f"swap_wait_{tag}")
        sums = [_pair_add(g, r, place, f"pair_add_{n}") for n, g, r in zip(groups[tag], views, got)]
        lands = [lax.empty((3,) + p.shape[1:], BF16) for p, _ in sums]
        send, recv, thru, token = _exchange_start(
            "scatter", [p for p, _ in sums], lands, f"scatter_start_{tag}", place)
        state[tag].update(scatter=(send, recv, thru), sums=sums)
        return token

    def collect(tag, after):
        _, parts = _exchange_wait("scatter", *state[tag]["scatter"], False, after, f"scatter_wait_{tag}")
        halves = [_chip_sum(own, p, place, f"chip_sum_{n}")
                  for n, (_, own), p in zip(groups[tag], state[tag]["sums"], parts)]
        for n, f in zip(groups[tag], _pair_gather(halves, f"grad_pair_gather_{tag}")):
            g_out[n] = f.reshape(w[n].shape[1:])
        return g_out[groups[tag][-1]]

    def early_grads(stage, value):
        tag, step = stage.split("_")
        token = swap_start(tag, value) if step == "start" else scatter_start(tag, value)
        return token[0, 0]

    grad_x, grads, small = _local_step(x, c, target, wts, vecs, fetch, early_grads)

    token = swap_start("last", grads)
    done = collect("later", token)
    token = scatter_start("last", done)
    done = adam(groups["later"], token)
    done = collect("mid", done)
    done = adam(groups["mid"], done)
    done = collect("last", done)

    dmod = jnp.pad(small["dmod"], ((0, 0), (0, _MOD_ROWS - 9), (0, 0))).reshape(bsz * _MOD_ROWS, D_MODEL)
    loss_rows = jnp.pad(small["loss"], ((0, 0), (0, D_MODEL - LANES)))
    buf = jnp.concatenate([dmod, small["ln1"], small["ln2"], small["ln3"], small["conv"], small["conv_w"],
                           loss_rows], axis=0)
    red = _small_all_reduce(buf, bsz, done)
    o = _MOD_ROWS
    g_out["b_ada"] = red[0:9].reshape(1, 9 * D_MODEL)
    g_out["ln1_g"], g_out["ln1_b"] = red[o:o + 1], red[o + 1:o + 2]
    g_out["ln2_g"], g_out["ln2_b"] = red[o + 8:o + 9], red[o + 9:o + 10]
    g_out["ln3_g"], g_out["ln3_b"] = red[o + 16:o + 17], red[o + 17:o + 18]
    g_out["conv_ln_g"], g_out["conv_ln_b"], g_out["conv_b"] = red[o + 24:o + 25], red[o + 25:o + 26], red[o + 26:o + 27]
    cw = w["conv_w"].shape[2]
    g_out["conv_w"] = lax.dynamic_slice(red[o + 32:o + 32 + CONV_TAPS], (0, chip * cw), (CONV_TAPS, cw))
    loss = red[o + 64, 0]

    adam(groups["last"] + list(_VECS) + ["conv_w"], place)
    return (loss, grad_x, *[updates[n][k] for k in range(4) for n in _WEIGHTS])


def kernel(x, c, w_ada, b_ada, ffn1_w_gu, ffn1_w_down, ln1_g, ln1_b, w_in, w_sb_out, conv_w, conv_b, conv_ln_g, conv_ln_b, w_conv_out, w_out, ln2_g, ln2_b, ffn2_w_gu, ffn2_w_down, ln3_g, ln3_b, loss_target, m_w_ada, m_b_ada, m_ffn1_w_gu, m_ffn1_w_down, m_ln1_g, m_ln1_b, m_w_in, m_w_sb_out, m_conv_w, m_conv_b, m_conv_ln_g, m_conv_ln_b, m_w_conv_out, m_w_out, m_ln2_g, m_ln2_b, m_ffn2_w_gu, m_ffn2_w_down, m_ln3_g, m_ln3_b, v_w_ada, v_b_ada, v_ffn1_w_gu, v_ffn1_w_down, v_ln1_g, v_ln1_b, v_w_in, v_w_sb_out, v_conv_w, v_conv_b, v_conv_ln_g, v_conv_ln_b, v_w_conv_out, v_w_out, v_ln2_g, v_ln2_b, v_ffn2_w_gu, v_ffn2_w_down, v_ln3_g, v_ln3_b):
    given = dict(locals())
    w = {n: given[n] for n in _WEIGHTS}
    m = {n: given["m_" + n] for n in _WEIGHTS}
    v = {n: given["v_" + n] for n in _WEIGHTS}
    return _step(x, c, loss_target, w, m, v)
```

```python
import functools

import jax
import jax.numpy as jnp
from jax import lax
from jax.experimental import pallas as pl
from jax.experimental.pallas import tpu as pltpu

F32 = jnp.float32
BF16 = jnp.bfloat16

D_MODEL = 1024
D_FF = 2816
HEADS = 16
HEAD_DIM = 64
LANES = 128
CONV_TAPS = 31
HALO = 32
N_CHIPS = 4
N_DEV = 8
ALPHA = 2.0 ** 0.25
LN_EPS = 1e-5
ATT_BLOCK = 256
VMEM_LIMIT = 56 * 1024 * 1024

ADAM_LR = 0.001
ADAM_B1 = 0.9
ADAM_B2 = 0.999
ADAM_EPS = 1e-08
ADAM_WD = 0.01
ADAM_STEP = 10

MESH = pl.DeviceIdType.MESH


def _pick(n, cands):
    for t in cands:
        if t <= n and n % t == 0:
            return t
    return n


def _params(sem):
    return pltpu.CompilerParams(dimension_semantics=sem, vmem_limit_bytes=VMEM_LIMIT)


def _sigmoid(z):
    t = jnp.exp(-jnp.abs(z))
    return jnp.where(z >= 0, 1.0, t) / (1.0 + t)


def _silu(z):
    return z * _sigmoid(z)


def _dsilu(z):
    s = _sigmoid(z)
    return s * (1.0 + z * (1.0 - s))


def _ln_stats(r):
    mu = jnp.mean(r, axis=-1, keepdims=True)
    d = r - mu
    var = jnp.mean(d * d, axis=-1, keepdims=True)
    rstd = lax.rsqrt(var + LN_EPS)
    return d * rstd, rstd


def _colsum(v):
    return jnp.sum(v, axis=0, keepdims=True)


_DIMS = {"nn": (((1,), (0,)), ((), ())), "nt": (((1,), (1,)), ((), ())), "tn": (((0,), (0,)), ((), ()))}
_TN_CANDS = (1408, 1792, 1152, 1024, 512, 256, 128)
_TK_CANDS = (2816, 1024, 1408, 1792, 512, 256, 128)


def _matmul(a, b, *, mode, out_dtype, name, bias=None, out_stacked=False):
    a_halves = mode == "nt" and a.ndim == 3
    b_halves = mode == "tn" and b.ndim == 3
    b_stacked = b.ndim == 3 and not b_halves
    if mode == "nn":
        m, k = a.shape
        n_c = b.shape[-1]
        n = n_c * (N_CHIPS if b_stacked else 1)
        k_c = k
    elif mode == "nt":
        m = a.shape[-2]
        k = a.shape[-1] * (2 if a_halves else 1)
        n = b.shape[-2]
        k_c = b.shape[-1]
        n_c = n
    else:
        k, m = a.shape
        n = b.shape[-1] * (2 if b_halves else 1)
        n_c = n // N_CHIPS if out_stacked else n
        k_c = k
    if mode == "tn":
        tm = _pick(m, (1024, 1408, 512, 256, 128))
        tk = _pick(k, (2048, 1024, 512, 256, 128, 64, 32, 16))
    else:
        tm = _pick(m, (1024, 512, 256, 128, 64, 32, 16))
        tk = _pick(k_c, _TK_CANDS)
    tn = _pick(n_c, _TN_CANDS)
    nb = n_c // tn
    kb = k_c // tk
    nk = k // tk
    grid = (m // tm, n // tn, nk)

    if mode == "nn":
        a_spec = pl.BlockSpec((tm, tk), lambda i, j, kk: (i, kk))
        if b_stacked:
            b_spec = pl.BlockSpec((None, tk, tn), lambda i, j, kk: (j // nb, kk, j % nb))
        else:
            b_spec = pl.BlockSpec((tk, tn), lambda i, j, kk: (kk, j))
    elif mode == "nt":
        if a_halves:
            ka = a.shape[-1] // tk
            a_spec = pl.BlockSpec((None, tm, tk), lambda i, j, kk: (kk // ka, i, kk % ka))
        else:
            a_spec = pl.BlockSpec((tm, tk), lambda i, j, kk: (i, kk))
        if b_stacked:
            b_spec = pl.BlockSpec((None, tn, tk), lambda i, j, kk: (kk // kb, j, kk % kb))
        else:
            b_spec = pl.BlockSpec((tn, tk), lambda i, j, kk: (j, kk))
    else:
        a_spec = pl.BlockSpec((tk, tm), lambda i, j, kk: (kk, i))
        if b_halves:
            nh = b.shape[-1] // tn
            b_spec = pl.BlockSpec((None, tk, tn), lambda i, j, kk: (j // nh, kk, j % nh))
        else:
            b_spec = pl.BlockSpec((tk, tn), lambda i, j, kk: (kk, j))
    if out_stacked:
        out_shape = jax.ShapeDtypeStruct((N_CHIPS, m, n_c), out_dtype)
        o_spec = pl.BlockSpec((None, tm, tn), lambda i, j, kk: (j // nb, i, j % nb))
    else:
        out_shape = jax.ShapeDtypeStruct((m, n), out_dtype)
        o_spec = pl.BlockSpec((tm, tn), lambda i, j, kk: (i, j))
    in_specs = [a_spec, b_spec]
    args = [a, b]
    if bias is not None:
        in_specs.append(pl.BlockSpec((1, tn), lambda i, j, kk: (0, j)))
        args.append(bias)
    dims = _DIMS[mode]

    def body(*refs):
        a_ref, b_ref = refs[0], refs[1]
        bias_ref = refs[2] if bias is not None else None

        def write(r):
            if bias_ref is not None:
                r = r + bias_ref[...]
            o_ref[...] = r.astype(o_ref.dtype)

        if nk == 1:
            o_ref = refs[-1]
            write(lax.dot_general(a_ref[...], b_ref[...], dims, preferred_element_type=F32))
            return
        o_ref, acc_ref = refs[-2], refs[-1]
        kk = pl.program_id(2)

        @pl.when(kk == 0)
        def _():
            acc_ref[...] = jnp.zeros_like(acc_ref)

        acc_ref[...] += lax.dot_general(a_ref[...], b_ref[...], dims, preferred_element_type=F32)

        @pl.when(kk == nk - 1)
        def _():
            write(acc_ref[...])

    return pl.pallas_call(
        body, name=name, grid=grid, in_specs=in_specs, out_specs=o_spec, out_shape=out_shape,
        scratch_shapes=[pltpu.VMEM((tm, tn), F32)] if nk > 1 else [],
        compiler_params=_params(("parallel", "parallel", "arbitrary")),
    )(*args)


def _row_grid(bsz, seq, ts):
    ns = seq // ts
    return (bsz, ns), ns


def _rows(ts, width, ns, col=0):
    return pl.BlockSpec((ts, width), lambda b, s: (b * ns + s, col))


def _mod_spec():
    return pl.BlockSpec((None, 9, D_MODEL), lambda b, s: (b, 0, 0))


def _vec_spec(rows=1, width=D_MODEL):
    return pl.BlockSpec((rows, width), lambda b, s: (0, 0))


def _silu_pad(c):
    bsz = c.shape[0]

    def body(c_ref, o_ref):
        o_ref[...] = jnp.zeros_like(o_ref)
        o_ref[0:bsz, :] = _silu(c_ref[...]).astype(BF16)

    return pl.pallas_call(body, name="silu_pad", out_shape=jax.ShapeDtypeStruct((16, D_MODEL), BF16))(c)


def _mod_in(x, mod, bsz, seq, sub):
    ts = _pick(seq, (512, 256, 128))
    grid, ns = _row_grid(bsz, seq, ts)

    def body(x_ref, mod_ref, u_ref):
        sh = mod_ref[3 * sub:3 * sub + 1, :]
        sc = mod_ref[3 * sub + 1:3 * sub + 2, :]
        u_ref[...] = (x_ref[...] * (1.0 + sc) + sh).astype(BF16)

    return pl.pallas_call(
        body, name=f"mod_in{sub}", grid=grid, in_specs=[_rows(ts, D_MODEL, ns), _mod_spec()],
        out_specs=_rows(ts, D_MODEL, ns), out_shape=jax.ShapeDtypeStruct(x.shape, BF16),
        compiler_params=_params(("parallel", "parallel")),
    )(x, mod)


_FFN_TN = D_FF // 2


def _ffn_up_act(u, w_gu, name):
    t = u.shape[0]
    tm = _pick(t, (512, 256, 128))
    tn = _FFN_TN

    def body(u_ref, wa_ref, wg_ref, h_ref, p_ref):
        u_v = u_ref[...]
        a = jnp.dot(u_v, wa_ref[...], preferred_element_type=F32)
        g = jnp.dot(u_v, wg_ref[...], preferred_element_type=F32)
        h_ref[0] = a.astype(BF16)
        h_ref[1] = g.astype(BF16)
        p_ref[...] = (_silu(a) * g).astype(BF16)

    return pl.pallas_call(
        body, name=name, grid=(2, t // tm),
        in_specs=[pl.BlockSpec((tm, D_MODEL), lambda j, i: (i, 0)),
                  pl.BlockSpec((None, D_MODEL, tn), lambda j, i: (j, 0, 0)),
                  pl.BlockSpec((None, D_MODEL, tn), lambda j, i: (j + 2, 0, 0))],
        out_specs=[pl.BlockSpec((2, tm, tn), lambda j, i: (0, i, j)),
                   pl.BlockSpec((tm, tn), lambda j, i: (i, j))],
        out_shape=[jax.ShapeDtypeStruct((2, t, D_FF), BF16), jax.ShapeDtypeStruct((t, D_FF), BF16)],
        compiler_params=_params(("parallel", "parallel")),
    )(u, w_gu, w_gu)


def _ffn_down_bwd_act(df, w_down, h, name):
    t = df.shape[0]
    tm = _pick(t, (512, 256, 128))
    tn = _FFN_TN

    def body(df_ref, wd_ref, h_ref, dh_ref):
        dp = lax.dot_general(df_ref[...], wd_ref[...], _DIMS["nt"], preferred_element_type=F32)
        a = h_ref[0].astype(F32)
        g = h_ref[1].astype(F32)
        dh_ref[0] = (dp * g * _dsilu(a)).astype(BF16)
        dh_ref[1] = (dp * _silu(a)).astype(BF16)

    blk = pl.BlockSpec((2, tm, tn), lambda j, i: (0, i, j))
    return pl.pallas_call(
        body, name=name, grid=(2, t // tm),
        in_specs=[pl.BlockSpec((tm, D_MODEL), lambda j, i: (i, 0)),
                  pl.BlockSpec((tn, D_MODEL), lambda j, i: (j, 0)), blk],
        out_specs=blk, out_shape=jax.ShapeDtypeStruct((2, t, D_FF), BF16),
        compiler_params=_params(("parallel", "parallel")),
    )(df, w_down, h)


def _res_ln_fwd(x, f, mod, ln_g, ln_b, bsz, seq, sub, weight, target=None):
    ts = _pick(seq, (256, 128))
    grid, ns = _row_grid(bsz, seq, ts)
    last = target is not None

    def body(*refs):
        x_ref, f_ref, mod_ref, g_ref, b_ref = refs[:5]
        gate = mod_ref[3 * sub + 2:3 * sub + 3, :]
        r = ALPHA * x_ref[...] + gate * (weight * f_ref[...])
        xhat, _ = _ln_stats(r)
        xo = xhat * g_ref[...] + b_ref[...]
        if last:
            t_ref, r_ref, dy_ref, loss_ref = refs[5:]
            diff = xo - t_ref[...]
            dy_ref[...] = diff * (1.0 / D_MODEL)
            part = 0.5 * jnp.sum(jnp.mean(diff * diff, axis=-1, keepdims=True), axis=0, keepdims=True)

            @pl.when((pl.program_id(0) == 0) & (pl.program_id(1) == 0))
            def _():
                loss_ref[...] = jnp.zeros_like(loss_ref)

            loss_ref[...] += jnp.broadcast_to(part, loss_ref.shape)
        else:
            r_ref, xo_ref, u_ref = refs[5:]
            xo_ref[...] = xo
            sh = mod_ref[3 * sub + 3:3 * sub + 4, :]
            sc = mod_ref[3 * sub + 4:3 * sub + 5, :]
            u_ref[...] = (xo * (1.0 + sc) + sh).astype(BF16)
        r_ref[...] = r

    row = _rows(ts, D_MODEL, ns)
    in_specs = [row, row, _mod_spec(), _vec_spec(), _vec_spec()]
    args = [x, f, mod, ln_g, ln_b]
    if last:
        in_specs.append(row)
        args.append(target)
        out_specs = [row, row, _vec_spec(8, LANES)]
        out_shape = [jax.ShapeDtypeStruct(x.shape, F32), jax.ShapeDtypeStruct(x.shape, F32),
                     jax.ShapeDtypeStruct((8, LANES), F32)]
        sem = ("arbitrary", "arbitrary")
    else:
        out_specs = [row, row, row]
        out_shape = [jax.ShapeDtypeStruct(x.shape, F32), jax.ShapeDtypeStruct(x.shape, F32),
                     jax.ShapeDtypeStruct(x.shape, BF16)]
        sem = ("parallel", "parallel")
    return pl.pallas_call(
        body, name=f"res_ln_fwd{sub}", grid=grid, in_specs=in_specs, out_specs=out_specs, out_shape=out_shape,
        compiler_params=_params(sem),
    )(*args)


def _res_ln_bwd(r, dxo, f, mod, ln_g, bsz, seq, sub, weight, above=None):
    ts = _pick(seq, (256, 128))
    grid, ns = _row_grid(bsz, seq, ts)
    folded = above is not None

    def body(*refs):
        r_ref, dxo_ref, f_ref, mod_ref, g_ref = refs[:5]
        dxres_ref, df_ref, lns_ref, gs_ref = refs[-5:-1] if folded else refs[-4:]
        b, s = pl.program_id(0), pl.program_id(1)
        gate = mod_ref[3 * sub + 2:3 * sub + 3, :]
        xhat, rstd = _ln_stats(r_ref[...])
        dxo_v = dxo_ref[...]
        if folded:
            du_v = refs[5][...]
            ms_ref = refs[-1]
            dxo_v = dxo_v + du_v * (1.0 + mod_ref[3 * sub + 4:3 * sub + 5, :])

            @pl.when(s == 0)
            def _():
                ms_ref[...] = jnp.zeros_like(ms_ref)

            ms_ref[0:1, :] += _colsum(du_v)
            ms_ref[1:2, :] += _colsum(du_v * refs[6][...])
        dxhat = dxo_v * g_ref[...]
        m1 = jnp.mean(dxhat, axis=-1, keepdims=True)
        m2 = jnp.mean(dxhat * xhat, axis=-1, keepdims=True)
        dr = rstd * (dxhat - m1 - xhat * m2)
        dxres_ref[...] = ALPHA * dr
        df_ref[...] = (dr * (gate * weight)).astype(BF16)

        @pl.when((b == 0) & (s == 0))
        def _():
            lns_ref[...] = jnp.zeros_like(lns_ref)

        @pl.when(s == 0)
        def _():
            gs_ref[...] = jnp.zeros_like(gs_ref)

        lns_ref[0:1, :] += _colsum(dxo_v * xhat)
        lns_ref[1:2, :] += _colsum(dxo_v)
        gs_ref[0:1, :] += _colsum(dr * (weight * f_ref[...]))

    row = _rows(ts, D_MODEL, ns)
    per_sample = pl.BlockSpec((None, 8, D_MODEL), lambda b, s: (b, 0, 0))
    stats = jax.ShapeDtypeStruct((bsz, 8, D_MODEL), F32)
    extra = 1 if folded else 0
    return pl.pallas_call(
        body, name=f"res_ln_bwd{sub}", grid=grid,
        in_specs=[row, row, row, _mod_spec(), _vec_spec()] + [row, row] * extra,
        out_specs=[row, row, _vec_spec(8), per_sample] + [per_sample] * extra,
        out_shape=[jax.ShapeDtypeStruct(r.shape, F32), jax.ShapeDtypeStruct(r.shape, BF16),
                   jax.ShapeDtypeStruct((8, D_MODEL), F32), stats] + [stats] * extra,
        compiler_params=_params(("arbitrary", "arbitrary")),
    )(r, dxo, f, mod, ln_g, *(above or ()))


def _mod_bwd(dxres, du, x, mod, bsz, seq, sub):
    ts = _pick(seq, (256, 128))
    grid, ns = _row_grid(bsz, seq, ts)

    def body(dxres_ref, du_ref, x_ref, mod_ref, dx_ref, st_ref):
        s = pl.program_id(1)
        sc = mod_ref[3 * sub + 1:3 * sub + 2, :]
        du_v = du_ref[...]
        dx_ref[...] = dxres_ref[...] + du_v * (1.0 + sc)

        @pl.when(s == 0)
        def _():
            st_ref[...] = jnp.zeros_like(st_ref)

        st_ref[0:1, :] += _colsum(du_v)
        st_ref[1:2, :] += _colsum(du_v * x_ref[...])

    row = _rows(ts, D_MODEL, ns)
    return pl.pallas_call(
        body, name=f"mod_bwd{sub}", grid=grid, in_specs=[row, row, row, _mod_spec()],
        out_specs=[row, pl.BlockSpec((None, 8, D_MODEL), lambda b, s: (b, 0, 0))],
        out_shape=[jax.ShapeDtypeStruct(x.shape, F32), jax.ShapeDtypeStruct((bsz, 8, D_MODEL), F32)],
        compiler_params=_params(("parallel", "arbitrary")),
    )(dxres, du, x, mod)


_COL_GLU_A, _COL_GLU_B, _COL_GATE_A, _COL_GATE_B = 3, 4, 5, 6


def _merge_fwd(proj, ya, cs, w_sb, w_co):
    t = ya.shape[0]
    tm = _pick(t, (512, 256, 128))

    def body(ga_ref, gb_ref, ya_ref, cs_ref, wsb_ref, wco_ref, m_ref, ysb_ref, yc_ref):
        ysb = jnp.dot(ya_ref[...], wsb_ref[...], preferred_element_type=F32)
        yc = jnp.dot(cs_ref[...], wco_ref[...], preferred_element_type=F32)
        ysb_ref[...] = ysb
        yc_ref[...] = yc
        m_ref[...] = (_sigmoid(ga_ref[...]) * ysb + _sigmoid(gb_ref[...]) * yc).astype(BF16)

    row = pl.BlockSpec((tm, D_MODEL), lambda i: (i, 0))
    full = pl.BlockSpec((D_MODEL, D_MODEL), lambda i: (0, 0))
    return pl.pallas_call(
        body, name="merge_fwd", grid=(t // tm,),
        in_specs=[pl.BlockSpec((tm, D_MODEL), lambda i: (i, _COL_GATE_A)),
                  pl.BlockSpec((tm, D_MODEL), lambda i: (i, _COL_GATE_B)), row, row, full, full],
        out_specs=[row, row, row],
        out_shape=[jax.ShapeDtypeStruct((t, D_MODEL), BF16), jax.ShapeDtypeStruct((t, D_MODEL), F32),
                   jax.ShapeDtypeStruct((t, D_MODEL), F32)],
        compiler_params=_params(("parallel",)),
    )(proj, proj, ya, cs, w_sb, w_co)


def _merge_bwd(proj, ysb, yconv, do2, w_out):
    t = ysb.shape[0]
    tm = _pick(t, (512, 256, 128))

    def body(ga_ref, gb_ref, ysb_ref, yc_ref, do_ref, w_ref, dysb_ref, dyc_ref, dg_ref):
        dm = lax.dot_general(do_ref[...], w_ref[...], _DIMS["nt"], preferred_element_type=F32)
        sa = _sigmoid(ga_ref[...])
        sb = _sigmoid(gb_ref[...])
        dysb_ref[...] = (dm * sa).astype(BF16)
        dyc_ref[...] = (dm * sb).astype(BF16)
        dg_ref[:, :D_MODEL] = (dm * ysb_ref[...] * sa * (1.0 - sa)).astype(BF16)
        dg_ref[:, D_MODEL:] = (dm * yc_ref[...] * sb * (1.0 - sb)).astype(BF16)

    row = pl.BlockSpec((tm, D_MODEL), lambda i: (i, 0))
    return pl.pallas_call(
        body, name="merge_bwd", grid=(t // tm,),
        in_specs=[pl.BlockSpec((tm, D_MODEL), lambda i: (i, _COL_GATE_A)),
                  pl.BlockSpec((tm, D_MODEL), lambda i: (i, _COL_GATE_B)), row, row, row,
                  pl.BlockSpec((D_MODEL, D_MODEL), lambda i: (0, 0))],
        out_specs=[row, row, pl.BlockSpec((tm, 2 * D_MODEL), lambda i: (i, 0))],
        out_shape=[jax.ShapeDtypeStruct((t, D_MODEL), BF16), jax.ShapeDtypeStruct((t, D_MODEL), BF16),
                   jax.ShapeDtypeStruct((t, 2 * D_MODEL), BF16)],
        compiler_params=_params(("parallel",)),
    )(proj, proj, ysb, yconv, do2, w_out)


_CONV_ROWS = 128


def _halo_prev(tt, ns, col):
    r = tt // HALO
    return pl.BlockSpec((HALO, D_MODEL), lambda b, s: (jnp.maximum((b * ns + s) * r - 1, 0), col))


def _halo_next(tt, ns, nblk, col):
    r = tt // HALO
    return pl.BlockSpec((HALO, D_MODEL), lambda b, s: (jnp.minimum((b * ns + s + 1) * r, nblk - 1), col))


def _windows(pad_ref, stage_ref, tt, offsets, cols):
    for r in range(8):
        mine = [o for o in offsets if o % 8 == r]
        if not mine:
            continue
        n = max(mine) - r + tt
        stage_ref[0:n, cols] = pad_ref[r:r + n, cols]
        for o in mine:
            yield o, stage_ref[o - r:o - r + tt, cols]


def _column_chunks():
    return [slice(c, c + LANES) for c in range(0, D_MODEL, LANES)]


def _fill_hc(hpad, a_ref, b_ref, ha_ref, hb_ref, s):
    halo = ha_ref[...] * _sigmoid(hb_ref[...])
    hpad[0:HALO, :] = jnp.where(s > 0, halo, 0.0)
    hpad[HALO:, :] = a_ref[...] * _sigmoid(b_ref[...])


def _conv_fwd(proj, conv_w, conv_b, ln_g, ln_b, bsz, seq):
    tt = _CONV_ROWS
    grid, ns = _row_grid(bsz, seq, tt)
    off = HALO - (CONV_TAPS - 1)

    def body(a_ref, b_ref, ha_ref, hb_ref, w_ref, cb_ref, g_ref, bb_ref, cs_ref, cv_ref, hpad, stage):
        _fill_hc(hpad, a_ref, b_ref, ha_ref, hb_ref, pl.program_id(1))
        for cols in _column_chunks():
            acc = jnp.zeros((tt, LANES), F32)
            for o, win in _windows(hpad, stage, tt, [off + j for j in range(CONV_TAPS)], cols):
                acc = acc + w_ref[o - off:o - off + 1, cols] * win
            cv_ref[:, cols] = acc + cb_ref[:, cols]
        xhat, _ = _ln_stats(cv_ref[...])
        cs_ref[...] = _silu(xhat * g_ref[...] + bb_ref[...]).astype(BF16)

    row = _rows(tt, D_MODEL, ns)
    t = proj.shape[0]
    return pl.pallas_call(
        body, name="conv_fwd", grid=grid,
        in_specs=[_rows(tt, D_MODEL, ns, _COL_GLU_A), _rows(tt, D_MODEL, ns, _COL_GLU_B),
                  _halo_prev(tt, ns, _COL_GLU_A), _halo_prev(tt, ns, _COL_GLU_B),
                  _vec_spec(32), _vec_spec(), _vec_spec(), _vec_spec()],
        out_specs=[row, row],
        out_shape=[jax.ShapeDtypeStruct((t, D_MODEL), BF16), jax.ShapeDtypeStruct((t, D_MODEL), F32)],
        scratch_shapes=[pltpu.VMEM((HALO + tt, D_MODEL), F32), pltpu.VMEM((HALO + tt, D_MODEL), F32)],
        compiler_params=_params(("parallel", "parallel")),
    )(proj, proj, proj, proj, conv_w, conv_b, ln_g, ln_b)


def _conv_bwd_ln(dcs, cv, ln_g, ln_b, bsz, seq):
    ts = _pick(seq, (256, 128))
    grid, ns = _row_grid(bsz, seq, ts)

    def body(dcs_ref, cv_ref, g_ref, b_ref, dcv_ref, st_ref):
        xhat, rstd = _ln_stats(cv_ref[...])
        cl = xhat * g_ref[...] + b_ref[...]
        dcl = dcs_ref[...] * _dsilu(cl)
        dxhat = dcl * g_ref[...]
        m1 = jnp.mean(dxhat, axis=-1, keepdims=True)
        m2 = jnp.mean(dxhat * xhat, axis=-1, keepdims=True)
        dcv = rstd * (dxhat - m1 - xhat * m2)
        dcv_ref[...] = dcv

        @pl.when((pl.program_id(0) == 0) & (pl.program_id(1) == 0))
        def _():
            st_ref[...] = jnp.zeros_like(st_ref)

        st_ref[0:1, :] += _colsum(dcl * xhat)
        st_ref[1:2, :] += _colsum(dcl)
        st_ref[2:3, :] += _colsum(dcv)

    row = _rows(ts, D_MODEL, ns)
    return pl.pallas_call(
        body, name="conv_bwd_ln", grid=grid, in_specs=[row, row, _vec_spec(), _vec_spec()],
        out_specs=[row, _vec_spec(8)],
        out_shape=[jax.ShapeDtypeStruct(cv.shape, F32), jax.ShapeDtypeStruct((8, D_MODEL), F32)],
        compiler_params=_params(("arbitrary", "arbitrary")),
    )(dcs, cv, ln_g, ln_b)


def _conv_bwd_taps(proj, dcv, conv_w, bsz, seq):
    tt = _CONV_ROWS
    grid, ns = _row_grid(bsz, seq, tt)
    off = HALO - (CONV_TAPS - 1)
    nblk = proj.shape[0] // HALO

    def body(a_ref, b_ref, ha_ref, hb_ref, d_ref, dn_ref, w_ref, dglu_ref, dw_ref, hpad, dpad, stage):
        s = pl.program_id(1)
        _fill_hc(hpad, a_ref, b_ref, ha_ref, hb_ref, s)
        dpad[0:tt, :] = d_ref[...]
        dpad[tt:, :] = jnp.where(s < ns - 1, dn_ref[...], 0.0)

        @pl.when((pl.program_id(0) == 0) & (s == 0))
        def _():
            dw_ref[...] = jnp.zeros_like(dw_ref)

        for cols in _column_chunks():
            dcv = d_ref[:, cols]
            dhc = jnp.zeros((tt, LANES), F32)
            for o, win in _windows(dpad, stage, tt, list(range(CONV_TAPS)), cols):
                j = CONV_TAPS - 1 - o
                dhc = dhc + w_ref[j:j + 1, cols] * win
            for o, win in _windows(hpad, stage, tt, [off + j for j in range(CONV_TAPS)], cols):
                dw_ref[o - off:o - off + 1, cols] += _colsum(dcv * win)
            sb = _sigmoid(b_ref[:, cols])
            dglu_ref[:, cols] = (dhc * sb).astype(BF16)
            dglu_ref[:, slice(D_MODEL + cols.start, D_MODEL + cols.stop)] = (
                dhc * a_ref[:, cols] * sb * (1.0 - sb)).astype(BF16)

    t = proj.shape[0]
    return pl.pallas_call(
        body, name="conv_bwd_taps", grid=grid,
        in_specs=[_rows(tt, D_MODEL, ns, _COL_GLU_A), _rows(tt, D_MODEL, ns, _COL_GLU_B),
                  _halo_prev(tt, ns, _COL_GLU_A), _halo_prev(tt, ns, _COL_GLU_B),
                  _rows(tt, D_MODEL, ns), _halo_next(tt, ns, nblk, 0), _vec_spec(32)],
        out_specs=[_rows(tt, 2 * D_MODEL, ns), _vec_spec(32)],
        out_shape=[jax.ShapeDtypeStruct((t, 2 * D_MODEL), BF16), jax.ShapeDtypeStruct((32, D_MODEL), F32)],
        scratch_shapes=[pltpu.VMEM((HALO + tt, D_MODEL), F32), pltpu.VMEM((tt + HALO, D_MODEL), F32),
                        pltpu.VMEM((HALO + tt, D_MODEL), F32)],
        compiler_params=_params(("arbitrary", "arbitrary")),
    )(proj, proj, proj, proj, dcv, dcv, conv_w)


_NT = (((1,), (1,)), ((), ()))
_TN = (((0,), (0,)), ((), ()))


def _dot(a, b, dims=None):
    if dims is None:
        return jnp.dot(a, b, preferred_element_type=F32)
    return lax.dot_general(a, b, dims, preferred_element_type=F32)


def _tri_dot(v, tri2):
    hi = v.astype(BF16)
    lo = (v - hi.astype(F32)).astype(BF16)
    return _dot(jnp.concatenate([hi, lo], axis=1), tri2)


def _tri2(mask):
    t = mask.astype(BF16)
    return jnp.concatenate([t, t], axis=0)


def _softplus_parts(z):
    t = jnp.exp(-jnp.abs(z))
    den = 1.0 + t
    return jnp.maximum(z, 0.0) + jnp.log(den), t, den


def _attn_fwd(proj, bsz, seq):
    blk = ATT_BLOCK
    nq = seq // blk
    n_pairs = D_MODEL // LANES

    def body(q_ref, k_ref, v_ref, y_ref, rt_ref, zr_buf, ns_buf, run_buf, acc_buf):
        qi = pl.program_id(2)
        lane = lax.broadcasted_iota(jnp.int32, (blk, LANES), 1)
        first = lane < HEAD_DIM
        q2 = q_ref[...] * 0.125
        q_heads = (jnp.where(first, q2, 0.0).astype(BF16), jnp.where(first, 0.0, q2).astype(BF16))
        rr = lax.broadcasted_iota(jnp.int32, (blk, blk), 0)
        cc = lax.broadcasted_iota(jnp.int32, (blk, blk), 1)
        tri_ge = _tri2(rr >= cc)
        causal = cc < rr

        def scores(kb, slot, masked, heads=(0, 1)):
            k_blk = k_ref[pl.ds(pl.multiple_of(kb * blk, blk), blk), :].astype(BF16)
            for h in heads:
                z = _dot(q_heads[h], k_blk, _NT)
                if masked:
                    z = jnp.where(causal, z, -1e30)
                sp, _, _ = _softplus_parts(z)
                neg = -sp
                zr_buf[slot, h] = z + _tri_dot(neg, tri_ge)
                ns_buf[slot, h] = jnp.sum(neg, axis=1, keepdims=True)

        def weigh(kb, slot, heads=(0, 1)):
            v_blk = v_ref[pl.ds(pl.multiple_of(kb * blk, blk), blk), :].astype(BF16)
            for h in heads:
                run = run_buf[h]
                w = jnp.exp(zr_buf[slot, h] + run)
                acc_buf[h] += _dot(w.astype(BF16), v_blk)
                run_buf[h] = run + ns_buf[slot, h]

        def step(kb_next, kb, slot):
            for h in range(2):
                scores(kb_next, 1 - slot, False, (h,))
                weigh(kb, slot, (h,))

        run_buf[...] = jnp.zeros_like(run_buf)
        acc_buf[...] = jnp.zeros_like(acc_buf)
        scores(qi, 0, True)

        def two_steps(p, carry):
            t = 2 * p
            step(qi - t - 1, qi - t, 0)
            step(qi - t - 2, qi - t - 1, 1)
            return carry

        lax.fori_loop(0, qi // 2, two_steps, 0)

        @pl.when(qi % 2 == 1)
        def _():
            step(0, 1, 0)
            weigh(0, 1)

        @pl.when(qi % 2 == 0)
        def _():
            weigh(0, 0)

        y_ref[...] = jnp.where(first, acc_buf[0], acc_buf[1]).astype(BF16)
        rt_ref[...] = jnp.where(first, jnp.broadcast_to(run_buf[0], (blk, LANES)),
                                jnp.broadcast_to(run_buf[1], (blk, LANES)))

    t = proj.shape[0]
    q_spec = pl.BlockSpec((blk, LANES), lambda b, p, i: (b * nq + i, p))
    return pl.pallas_call(
        body, name="attn_fwd", grid=(bsz, n_pairs, nq),
        in_specs=[q_spec,
                  pl.BlockSpec((seq, LANES), lambda b, p, i: (b, n_pairs + p)),
                  pl.BlockSpec((seq, LANES), lambda b, p, i: (b, 2 * n_pairs + p))],
        out_specs=[q_spec, q_spec],
        out_shape=[jax.ShapeDtypeStruct((t, D_MODEL), BF16), jax.ShapeDtypeStruct((t, D_MODEL), F32)],
        scratch_shapes=[pltpu.VMEM((2, 2, blk, blk), F32), pltpu.VMEM((2, 2, blk, 1), F32),
                        pltpu.VMEM((2, blk, 1), F32), pltpu.VMEM((2, blk, LANES), F32)],
        compiler_params=_params(("parallel", "parallel", "arbitrary")),
    )(proj, proj, proj)


def _attn_bwd(proj, rtot, dy, bsz, seq):
    blk = ATT_BLOCK
    nq = seq // blk
    n_pairs = D_MODEL // LANES

    def body(q_ref, k_ref, v_ref, dy_ref, rt_ref, dq_ref, dk_ref, dv_ref, dk_acc, dv_acc,
             a_buf, sig_buf, dw_buf, ns_buf, pre_buf, es_buf, dq_buf):
        qi = pl.program_id(2)

        @pl.when(qi == 0)
        def _():
            dk_acc[...] = jnp.zeros_like(dk_acc)
            dv_acc[...] = jnp.zeros_like(dv_acc)

        lane = lax.broadcasted_iota(jnp.int32, (blk, LANES), 1)
        first = lane < HEAD_DIM
        head_row = lax.broadcasted_iota(jnp.int32, (LANES, blk), 0) < HEAD_DIM
        q2 = q_ref[...] * 0.125
        q_rows = (jnp.where(first, q2, 0.0).astype(BF16), jnp.where(first, 0.0, q2).astype(BF16))
        q_t = q2.T
        q_heads = (jnp.where(head_row, q_t, 0.0).astype(BF16), jnp.where(head_row, 0.0, q_t).astype(BF16))
        dy2 = dy_ref[...].astype(F32)
        dy_rows = (jnp.where(first, dy2, 0.0).astype(BF16), jnp.where(first, 0.0, dy2).astype(BF16))
        dy_t = dy2.T
        dy_heads = (jnp.where(head_row, dy_t, 0.0).astype(BF16), jnp.where(head_row, 0.0, dy_t).astype(BF16))
        rt_t = rt_ref[...].T
        rt = (rt_t[0:1, :], rt_t[HEAD_DIM:HEAD_DIM + 1, :])
        rr = lax.broadcasted_iota(jnp.int32, (blk, blk), 0)
        cc = lax.broadcasted_iota(jnp.int32, (blk, blk), 1)
        lower = (cc < rr).astype(BF16)
        lower_eq = (cc <= rr).astype(BF16)
        tri_lt = jnp.concatenate([lower, lower], axis=1)
        tri_le = jnp.concatenate([lower_eq, lower_eq], axis=1)
        causal = rr < cc

        def tri_left(tri2, v):
            hi = v.astype(BF16)
            lo = (v - hi.astype(F32)).astype(BF16)
            return _dot(tri2, jnp.concatenate([hi, lo], axis=0))

        def scores(kb, slot, heads=(0, 1)):
            rows = pl.ds(pl.multiple_of(kb * blk, blk), blk)
            k_blk = k_ref[rows, :].astype(BF16)
            v_blk = v_ref[rows, :].astype(BF16)
            keep = jnp.logical_or(causal, kb < qi)
            for h in heads:
                z = jnp.where(keep, _dot(k_blk, q_heads[h]), -1e30)
                sp, t, den = _softplus_parts(z)
                neg = -sp
                a_buf[slot, h] = z - tri_left(tri_lt, neg)
                sig_buf[slot, h] = jnp.where(z >= 0, 1.0, t) / den
                ns_buf[slot, h] = jnp.sum(neg, axis=0, keepdims=True)
                dw_buf[slot, h] = _dot(v_blk, dy_heads[h])

        def finish(kb, slot, heads=(0, 1)):
            rows = pl.ds(pl.multiple_of(kb * blk, blk), blk)
            k_t = k_ref[rows, :].T.astype(BF16)
            for h in heads:
                pre, esum = pre_buf[h], es_buf[h]
                w = jnp.exp(a_buf[slot, h] + (rt[h] - pre))
                e = dw_buf[slot, h] * w
                dz = e - sig_buf[slot, h] * (esum + tri_left(tri_le, e))
                pre_buf[h] = pre + ns_buf[slot, h]
                es_buf[h] = esum + jnp.sum(e, axis=0, keepdims=True)
                dzb = dz.astype(BF16)
                dq_buf[h] += _dot(k_t, dzb)
                dk_acc[rows, :] += _dot(dzb, q_rows[h])
                dv_acc[rows, :] += _dot(w.astype(BF16), dy_rows[h])

        def step(kb_next, kb, slot):
            scores(kb_next, 1 - slot)
            finish(kb, slot)

        pre_buf[...] = jnp.zeros_like(pre_buf)
        es_buf[...] = jnp.zeros_like(es_buf)
        dq_buf[...] = jnp.zeros_like(dq_buf)
        scores(0, 0)

        def two_steps(p, carry):
            t = 2 * p
            step(t + 1, t, 0)
            step(t + 2, t + 1, 1)
            return carry

        lax.fori_loop(0, qi // 2, two_steps, 0)

        @pl.when(qi % 2 == 1)
        def _():
            step(qi, qi - 1, 0)
            finish(qi, 1)

        @pl.when(qi % 2 == 0)
        def _():
            finish(qi, 0)

        dq_ref[...] = (jnp.where(head_row, dq_buf[0], dq_buf[1]).T * 0.125).astype(BF16)

        @pl.when(qi == nq - 1)
        def _():
            dk_ref[...] = dk_acc[...].astype(BF16)
            dv_ref[...] = dv_acc[...].astype(BF16)

    t = proj.shape[0]
    q_spec = pl.BlockSpec((blk, LANES), lambda b, p, i: (b * nq + i, p))
    kv_out = pl.BlockSpec((seq, LANES), lambda b, p, i: (b, p))
    out = jax.ShapeDtypeStruct((t, D_MODEL), BF16)
    return pl.pallas_call(
        body, name="attn_bwd", grid=(bsz, n_pairs, nq),
        in_specs=[q_spec,
                  pl.BlockSpec((seq, LANES), lambda b, p, i: (b, n_pairs + p)),
                  pl.BlockSpec((seq, LANES), lambda b, p, i: (b, 2 * n_pairs + p)),
                  q_spec, q_spec],
        out_specs=[q_spec, kv_out, kv_out], out_shape=[out, out, out],
        scratch_shapes=[pltpu.VMEM((seq, LANES), F32), pltpu.VMEM((seq, LANES), F32),
                        pltpu.VMEM((2, 2, blk, blk), F32), pltpu.VMEM((2, 2, blk, blk), F32),
                        pltpu.VMEM((2, 2, blk, blk), F32), pltpu.VMEM((2, 2, 1, blk), F32),
                        pltpu.VMEM((2, 1, blk), F32), pltpu.VMEM((2, 1, blk), F32),
                        pltpu.VMEM((2, LANES, blk), F32)],
        compiler_params=_params(("parallel", "parallel", "arbitrary")),
    )(proj, proj, proj, dy, rtot)


def _adamw(w, g, m, v, name, after):
    rows, cols = w.shape
    tr = _pick(rows, (256, 352, 128, 64, 32, 16, 8))
    c1 = 1.0 - ADAM_B1 ** ADAM_STEP
    c2 = 1.0 - ADAM_B2 ** ADAM_STEP

    def body(w_ref, g_ref, m_ref, v_ref, after_ref, d_ref, nm_ref, nv_ref):
        g_v = g_ref[...]
        nm = ADAM_B1 * m_ref[...] + (1.0 - ADAM_B1) * g_v
        nv = ADAM_B2 * v_ref[...] + (1.0 - ADAM_B2) * (g_v * g_v)
        nm_ref[...] = nm
        nv_ref[...] = nv
        d_ref[...] = -ADAM_LR * ((nm / c1) / (jnp.sqrt(nv / c2) + ADAM_EPS) + ADAM_WD * w_ref[...])

    spec = pl.BlockSpec((tr, cols), lambda i: (i, 0))
    shape = jax.ShapeDtypeStruct(w.shape, F32)
    return pl.pallas_call(
        body, name=name, grid=(rows // tr,), in_specs=[spec] * 4 + [pl.BlockSpec(memory_space=pl.ANY)],
        out_specs=[spec] * 3, out_shape=[shape] * 3,
        compiler_params=_params(("parallel",)),
    )(w, g, m, v, after)


def _ffn_fwd(u, w_gu, w_down, bsz, seq, tag):
    h, p = _ffn_up_act(u, w_gu, f"{tag}_up")
    f = _matmul(p, w_down, mode="nn", out_dtype=F32, name=f"{tag}_down")
    return h, p, f


def _ffn_bwd(df, u, h, p, w_gu, w_down, bsz, seq, tag):
    dh = _ffn_down_bwd_act(df, w_down, h, f"{tag}_ddown")
    g_down = _matmul(p, df, mode="tn", out_dtype=F32, name=f"{tag}_gdown")
    g_gu = _matmul(u, dh, mode="tn", out_dtype=F32, name=f"{tag}_ggu", out_stacked=True)
    du = _matmul(dh, w_gu, mode="nt", out_dtype=F32, name=f"{tag}_dup")
    return du, g_gu, g_down


def _local_step(x, c, target, wts, vecs, fetch=None, early_grads=None):
    wts = dict(wts)
    bsz, seq, _ = x.shape
    t = bsz * seq
    x0 = x.reshape(t, D_MODEL)
    tgt = target.reshape(t, D_MODEL)

    sc = _silu_pad(c)
    mod16 = _matmul(sc, wts["w_ada"], mode="nn", out_dtype=F32, name="ada_fwd", bias=vecs["b_ada"])
    mod = mod16[:bsz].reshape(bsz, 9, D_MODEL)

    u1 = _mod_in(x0, mod, bsz, seq, 0)
    h1, p1 = _ffn_up_act(u1, wts["ffn1_w_gu"], "ffn1_up")
    if fetch is not None:
        wts.update(fetch("down", p1))
    f1 = _matmul(p1, wts["ffn1_w_down"], mode="nn", out_dtype=F32, name="ffn1_down")
    r1, x1, u2 = _res_ln_fwd(x0, f1, mod, vecs["ln1_g"], vecs["ln1_b"], bsz, seq, 0, 0.5)
    if fetch is not None:
        wts.update(fetch("later", r1))

    proj = _matmul(u2, wts["w_in"], mode="nn", out_dtype=F32, name="mix_in")
    ya, rtot = _attn_fwd(proj, bsz, seq)
    cs, cv = _conv_fwd(proj, wts["conv_w"], vecs["conv_b"], vecs["conv_ln_g"], vecs["conv_ln_b"], bsz, seq)
    merged, ysb, yconv = _merge_fwd(proj, ya, cs, wts["w_sb_out"], wts["w_conv_out"])
    o2 = _matmul(merged, wts["w_out"], mode="nn", out_dtype=F32, name="mix_out")
    r2, x2, u3 = _res_ln_fwd(x1, o2, mod, vecs["ln2_g"], vecs["ln2_b"], bsz, seq, 1, 1.0)

    h3, p3, f3 = _ffn_fwd(u3, wts["ffn2_w_gu"], wts["ffn2_w_down"], bsz, seq, "ffn2")
    r3, dy, loss_blk = _res_ln_fwd(x2, f3, mod, vecs["ln3_g"], vecs["ln3_b"], bsz, seq, 2, 0.5, target=tgt)

    grads = {}
    dxres, df, ln3s, g3s = _res_ln_bwd(r3, dy, f3, mod, vecs["ln3_g"], bsz, seq, 2, 0.5)
    du, grads["ffn2_w_gu"], grads["ffn2_w_down"] = _ffn_bwd(
        df, u3, h3, p3, wts["ffn2_w_gu"], wts["ffn2_w_down"], bsz, seq, "ffn2")

    dxres, do2, ln2s, g2s, m3s = _res_ln_bwd(r2, dxres, o2, mod, vecs["ln2_g"], bsz, seq, 1, 1.0, above=(du, x2))
    grads["w_out"] = _matmul(merged, do2, mode="tn", out_dtype=F32, name="mix_out_g")
    dysb, dyconv, dgate = _merge_bwd(proj, ysb, yconv, do2, wts["w_out"])
    dya = _matmul(dysb, wts["w_sb_out"], mode="nt", out_dtype=BF16, name="sb_out_d")
    grads["w_sb_out"] = _matmul(ya, dysb, mode="tn", out_dtype=F32, name="sb_out_g")
    dcs = _matmul(dyconv, wts["w_conv_out"], mode="nt", out_dtype=F32, name="conv_out_d")
    grads["w_conv_out"] = _matmul(cs, dyconv, mode="tn", out_dtype=F32, name="conv_out_g")
    dcv, convs = _conv_bwd_ln(dcs, cv, vecs["conv_ln_g"], vecs["conv_ln_b"], bsz, seq)
    dglu, g_conv_w = _conv_bwd_taps(proj, dcv, wts["conv_w"], bsz, seq)
    dq, dk, dv = _attn_bwd(proj, rtot, dya, bsz, seq)
    dproj = jnp.concatenate([dq, dk, dv, dglu, dgate], axis=1)
    grads["w_in"] = _matmul(u2, dproj, mode="tn", out_dtype=F32, name="mix_in_g", out_stacked=True)
    if early_grads is not None:
        mod = mod + early_grads("later_start", {n: grads.pop(n) for n in list(grads)})
    du = _matmul(dproj, wts["w_in"], mode="nt", out_dtype=F32, name="mix_in_d")
    if early_grads is not None:
        mod = mod + early_grads("later_go", du)

    dxres, df, ln1s, g1s, m2s = _res_ln_bwd(r1, dxres, f1, mod, vecs["ln1_g"], bsz, seq, 0, 0.5, above=(du, x1))
    dh = _ffn_down_bwd_act(df, wts["ffn1_w_down"], h1, "ffn1_ddown")
    grads["ffn1_w_down"] = _matmul(p1, df, mode="tn", out_dtype=F32, name="ffn1_gdown")
    grads["ffn1_w_gu"] = _matmul(u1, dh, mode="tn", out_dtype=F32, name="ffn1_ggu", out_stacked=True)
    if early_grads is not None:
        mod = mod + early_grads("mid_start", {n: grads.pop(n) for n in list(grads)})
    du = _matmul(dh, wts["ffn1_w_gu"], mode="nt", out_dtype=F32, name="ffn1_dup")
    if early_grads is not None:
        mod = mod + early_grads("mid_go", du)
    grad_x, m1s = _mod_bwd(dxres, du, x0, mod, bsz, seq, 0)

    dmod = jnp.stack([m1s[:, 0], m1s[:, 1], g1s[:, 0], m2s[:, 0], m2s[:, 1], g2s[:, 0],
                      m3s[:, 0], m3s[:, 1], g3s[:, 0]], axis=1)
    dmod16 = jnp.zeros((16, 9 * D_MODEL), F32).at[:bsz].set(dmod.reshape(bsz, 9 * D_MODEL))
    grads["w_ada"] = _matmul(sc, dmod16.astype(BF16), mode="tn", out_dtype=F32, name="ada_g", out_stacked=True)

    small = {"dmod": dmod, "ln1": ln1s, "ln2": ln2s, "ln3": ln3s, "conv": convs, "conv_w": g_conv_w,
             "loss": loss_blk}
    return grad_x.reshape(x.shape), grads, small


_HBM = pl.BlockSpec(memory_space=pltpu.HBM)


def _position():
    return lax.axis_index("x"), lax.axis_index("y"), lax.axis_index("c")


def _other_chips(x, y):
    return [(1 - x, y), (x, 1 - y), (1 - x, 1 - y)]


def _cast_into_stack(w_local, chip, name):
    rows, cols = w_local.shape
    tr = _pick(rows, (256, 352, 128, 64, 32, 16))

    def body(chip_ref, w_ref, o_ref):
        o_ref[...] = w_ref[...].astype(BF16)

    return pl.pallas_call(
        body, name=name,
        grid_spec=pltpu.PrefetchScalarGridSpec(
            num_scalar_prefetch=1, grid=(rows // tr,),
            in_specs=[pl.BlockSpec((tr, cols), lambda r, chip_ref: (r, 0))],
            out_specs=pl.BlockSpec((None, tr, cols), lambda r, chip_ref: (chip_ref[0], r, 0))),
        out_shape=jax.ShapeDtypeStruct((N_CHIPS, rows, cols), BF16),
        compiler_params=_params(("parallel",)),
    )(chip, w_local)


def _all_gather_weights(stacks, small):
    n = len(stacks)

    def body(*refs):
        ins, small_in, outs, small_out = refs[:n], refs[n], refs[n + 1:2 * n + 1], refs[2 * n + 1]
        send_sems, recv_sems, fwd_send_sems, fwd_recv_sems, small_sems = refs[2 * n + 2:]
        x, y, c = _position()
        me = 2 * x + y
        chips = _other_chips(x, y)

        def send(i, j):
            px, py = chips[j]
            return pltpu.make_async_remote_copy(
                src_ref=ins[i].at[me, c], dst_ref=outs[i].at[me, c], send_sem=send_sems.at[3 * i + j],
                recv_sem=recv_sems.at[3 * i + j], device_id=(px, py, c), device_id_type=MESH)

        def landed(i, j):
            px, py = chips[j]
            return pltpu.make_async_remote_copy(
                src_ref=ins[i].at[me, c], dst_ref=outs[i].at[2 * px + py, c], send_sem=send_sems.at[3 * i + j],
                recv_sem=recv_sems.at[3 * i + j], device_id=(px, py, c), device_id_type=MESH)

        def forward(i, j, half):
            px, py = chips[j]
            blk = outs[i].at[2 * px + py, half]
            return pltpu.make_async_remote_copy(
                src_ref=blk, dst_ref=blk, send_sem=fwd_send_sems.at[3 * i + j],
                recv_sem=fwd_recv_sems.at[3 * i + j], device_id=(x, y, 1 - c), device_id_type=MESH)

        def small_copy(j, slot):
            px, py = chips[j]
            return pltpu.make_async_remote_copy(
                src_ref=small_in, dst_ref=small_out.at[slot], send_sem=small_sems.at[j],
                recv_sem=small_sems.at[3 + j], device_id=(px, py, c), device_id_type=MESH)

        own_small = pltpu.make_async_copy(small_in, small_out.at[me], small_sems.at[6])
        own_small.start()
        for j in range(3):
            small_copy(j, me).start()
        for i in range(n):
            for j in range(3):
                send(i, j).start()
        for i in range(n):
            for j in range(3):
                landed(i, j).wait_recv()
                forward(i, j, c).start()
        for i in range(n):
            for j in range(3):
                forward(i, j, 1 - c).wait_recv()
        for j, (px, py) in enumerate(chips):
            small_copy(j, 2 * px + py).wait_recv()
        own_small.wait()
        for j in range(3):
            small_copy(j, me).wait_send()
        for i in range(n):
            for j in range(3):
                send(i, j).wait_send()
                forward(i, j, c).wait_send()

    return pl.pallas_call(
        body, name="all_gather_weights",
        out_shape=[jax.ShapeDtypeStruct(s.shape, s.dtype) for s in stacks]
        + [jax.ShapeDtypeStruct((N_CHIPS,) + small.shape, small.dtype)],
        in_specs=[_HBM] * (n + 1), out_specs=[_HBM] * (n + 1),
        input_output_aliases={i: i for i in range(n)},
        scratch_shapes=[pltpu.SemaphoreType.DMA((3 * n,)), pltpu.SemaphoreType.DMA((3 * n,)),
                        pltpu.SemaphoreType.DMA((3 * n,)), pltpu.SemaphoreType.DMA((3 * n,)),
                        pltpu.SemaphoreType.DMA((7,))],
    )(*stacks, small)


_SEM = pl.BlockSpec(memory_space=pltpu.SEMAPHORE)
_DATAFLOW = pltpu.SideEffectType.DATAFLOW_SIDE_EFFECTING


_COPIES = {"gather": 3, "scatter": 3, "swap": N_CHIPS}


def _exchange_plan(kind, src, land):
    x, y, c = _position()
    me = 2 * x + y
    if kind == "swap":
        return [(src.at[k, 1 - c], land.at[k], land.at[k], (x, y, 1 - c)) for k in range(N_CHIPS)]
    plan = []
    for j, (px, py) in enumerate(_other_chips(x, y)):
        if kind == "gather":
            plan.append((src.at[me, c], land.at[me, c], land.at[2 * px + py, c], (px, py, c)))
        else:
            plan.append((src.at[2 * px + py], land.at[j], land.at[j], (px, py, c)))
    return plan


def _exchange_start(kind, srcs, lands, name, after):
    n = len(srcs)
    per = _COPIES[kind]
    in_place = lands is None
    n_in = n if in_place else 2 * n

    def body(*refs):
        src_refs = refs[:n]
        land_refs = src_refs if in_place else refs[n:2 * n]
        send_sems, recv_sems = refs[n_in + 1], refs[n_in + 2]
        token = refs[-1]
        for i in range(n):
            for j, (src, dst, _, to) in enumerate(_exchange_plan(kind, src_refs[i], land_refs[i])):
                pltpu.make_async_remote_copy(
                    src_ref=src, dst_ref=dst, send_sem=send_sems.at[per * i + j], recv_sem=recv_sems.at[per * i + j],
                    device_id=to, device_id_type=MESH).start()
        token[...] = jnp.zeros_like(token)

    operands = list(srcs) + ([] if in_place else list(lands))
    operands = [pltpu.with_memory_space_constraint(o, pltpu.HBM) for o in operands]
    out = pl.pallas_call(
        body, name=name,
        out_shape=[pltpu.SemaphoreType.DMA((per * n,)), pltpu.SemaphoreType.DMA((per * n,))]
        + [pltpu.HBM(o.shape, o.dtype) for o in operands] + [jax.ShapeDtypeStruct((8, LANES), F32)],
        in_specs=[_HBM] * n_in + [pl.BlockSpec(memory_space=pl.ANY)],
        out_specs=[_SEM, _SEM] + [_HBM] * n_in + [pl.BlockSpec(memory_space=pltpu.VMEM)],
        input_output_aliases={i: 2 + i for i in range(n_in)},
        compiler_params=pltpu.CompilerParams(has_side_effects=_DATAFLOW),
    )(*operands, after)
    return out[0], out[1], list(out[2:2 + n_in]), out[-1]


def _exchange_wait(kind, send_sems, recv_sems, thru, in_place, after, name):
    n_in = len(thru)
    n = n_in if in_place else n_in // 2
    per = _COPIES[kind]

    def body(*refs):
        src_refs = refs[:n]
        land_refs = src_refs if in_place else refs[n:2 * n]
        send_sems, recv_sems = refs[n_in], refs[n_in + 1]
        for i in range(n):
            for j, (src, _, here, to) in enumerate(_exchange_plan(kind, src_refs[i], land_refs[i])):
                copy = pltpu.make_async_remote_copy(
                    src_ref=src, dst_ref=here, send_sem=send_sems.at[per * i + j], recv_sem=recv_sems.at[per * i + j],
                    device_id=to, device_id_type=MESH)
                copy.wait_send()
                copy.wait_recv()

    out = pl.pallas_call(
        body, name=name, out_shape=[pltpu.HBM(o.shape, o.dtype) for o in thru],
        in_specs=[_HBM] * n_in + [_SEM, _SEM, pl.BlockSpec(memory_space=pl.ANY)], out_specs=[_HBM] * n_in,
        input_output_aliases={i: i for i in range(n_in)},
        compiler_params=pltpu.CompilerParams(has_side_effects=_DATAFLOW),
    )(*thru, send_sems, recv_sems, after)
    return list(out[:n]), (list(out[:n]) if in_place else list(out[n:]))


def _gather_forward(stacks, name):
    n = len(stacks)

    def body(*refs):
        ins, outs = refs[:n], refs[n:2 * n]
        send_sems, recv_sems = refs[2 * n:]
        x, y, c = _position()
        chips = _other_chips(x, y)

        def copy(i, j, half):
            px, py = chips[j]
            return pltpu.make_async_remote_copy(
                src_ref=ins[i].at[2 * px + py, half], dst_ref=outs[i].at[2 * px + py, half],
                send_sem=send_sems.at[3 * i + j], recv_sem=recv_sems.at[3 * i + j],
                device_id=(x, y, 1 - c), device_id_type=MESH)

        for i in range(n):
            for j in range(3):
                copy(i, j, c).start()
        for i in range(n):
            for j in range(3):
                copy(i, j, 1 - c).wait_recv()
        for i in range(n):
            for j in range(3):
                copy(i, j, c).wait_send()

    return pl.pallas_call(
        body, name=name, out_shape=[jax.ShapeDtypeStruct(s.shape, s.dtype) for s in stacks],
        in_specs=[_HBM] * n, out_specs=[_HBM] * n, input_output_aliases={i: i for i in range(n)},
        scratch_shapes=[pltpu.SemaphoreType.DMA((3 * n,)), pltpu.SemaphoreType.DMA((3 * n,))],
    )(*stacks)


def _pair_add(g, got, place, name):
    _, _, rh, cols = g.shape
    tr = _pick(rh, (256, 176, 128, 64, 32, 16, 8))

    def body(place_ref, g_ref, got_ref, p_ref, own_ref):
        s = g_ref[...] + got_ref[...]
        p_ref[...] = s.astype(BF16)

        @pl.when(pl.program_id(1) == place_ref[1])
        def _():
            own_ref[...] = s

    blk = pl.BlockSpec((None, tr, cols), lambda r, k, place_ref: (k, r, 0))
    return pl.pallas_call(
        body, name=name,
        grid_spec=pltpu.PrefetchScalarGridSpec(
            num_scalar_prefetch=1, grid=(rh // tr, N_CHIPS),
            in_specs=[pl.BlockSpec((None, None, tr, cols), lambda r, k, place_ref: (k, place_ref[0], r, 0)), blk],
            out_specs=[blk, pl.BlockSpec((tr, cols), lambda r, k, place_ref: (r, 0))]),
        out_shape=[jax.ShapeDtypeStruct((N_CHIPS, rh, cols), BF16), jax.ShapeDtypeStruct((rh, cols), F32)],
        compiler_params=_params(("parallel", "arbitrary")),
    )(place, g, got)


def _chip_sum(own, parts, place, name):
    rh, cols = own.shape
    tr = _pick(rh, (256, 176, 128, 64, 32, 16, 8))

    def body(place_ref, own_ref, p_ref, o_ref):
        o_ref[...] = ((own_ref[...] + p_ref[0].astype(F32)) + p_ref[1].astype(F32)) + p_ref[2].astype(F32)

    return pl.pallas_call(
        body, name=name,
        grid_spec=pltpu.PrefetchScalarGridSpec(
            num_scalar_prefetch=1, grid=(rh // tr,),
            in_specs=[pl.BlockSpec((tr, cols), lambda r, place_ref: (r, 0)),
                      pl.BlockSpec((3, tr, cols), lambda r, place_ref: (0, r, 0))],
            out_specs=pl.BlockSpec((None, tr, cols), lambda r, place_ref: (place_ref[0], r, 0))),
        out_shape=jax.ShapeDtypeStruct((2, rh, cols), F32),
        compiler_params=_params(("parallel",)),
    )(place, own, parts)


def _pair_gather(halves, name):
    n = len(halves)

    def body(*refs):
        ins, outs = refs[:n], refs[n:2 * n]
        send_sems, recv_sems = refs[2 * n:]
        x, y, c = _position()

        def send(i):
            return pltpu.make_async_remote_copy(
                src_ref=ins[i].at[c], dst_ref=outs[i].at[c], send_sem=send_sems.at[i], recv_sem=recv_sems.at[i],
                device_id=(x, y, 1 - c), device_id_type=MESH)

        def landed(i):
            return pltpu.make_async_remote_copy(
                src_ref=ins[i].at[c], dst_ref=outs[i].at[1 - c], send_sem=send_sems.at[i], recv_sem=recv_sems.at[i],
                device_id=(x, y, 1 - c), device_id_type=MESH)

        for i in range(n):
            send(i).start()
        for i in range(n):
            landed(i).wait_recv()
        for i in range(n):
            send(i).wait_send()

    return pl.pallas_call(
        body, name=name,
        out_shape=[jax.ShapeDtypeStruct(h.shape, F32) for h in halves],
        in_specs=[_HBM] * n, out_specs=[_HBM] * n,
        input_output_aliases={i: i for i in range(n)},
        scratch_shapes=[pltpu.SemaphoreType.DMA((n,)), pltpu.SemaphoreType.DMA((n,))],
    )(*halves)


_MOD_ROWS = 16


def _small_all_reduce(buf, bsz, after):
    rows, cols = buf.shape
    head = bsz * _MOD_ROWS
    out_rows = rows - head + _MOD_ROWS

    def body(in_ref, after_ref, o_ref, gath, send_sems, recv_sems):
        x, y, c = _position()
        me = 4 * x + 2 * y + c

        def peer(mask):
            return (x ^ (mask >> 2), y ^ ((mask >> 1) & 1), c ^ (mask & 1))

        def copy(mask):
            return pltpu.make_async_remote_copy(
                src_ref=in_ref, dst_ref=gath.at[me], send_sem=send_sems.at[mask - 1],
                recv_sem=recv_sems.at[mask - 1], device_id=peer(mask), device_id_type=MESH)

        def arrival(mask):
            px, py, pc = peer(mask)
            return pltpu.make_async_remote_copy(
                src_ref=in_ref, dst_ref=gath.at[4 * px + 2 * py + pc], send_sem=send_sems.at[mask - 1],
                recv_sem=recv_sems.at[mask - 1], device_id=peer(mask), device_id_type=MESH)

        for mask in range(1, N_DEV):
            copy(mask).start()
        gath[me] = in_ref[...]
        for mask in range(1, N_DEV):
            arrival(mask).wait_recv()
        for mask in range(1, N_DEV):
            copy(mask).wait_send()
        acc = gath[0]
        for d in range(1, N_DEV):
            acc = acc + gath[d]
        mod = acc[0:_MOD_ROWS]
        for s in range(1, bsz):
            mod = mod + acc[s * _MOD_ROWS:(s + 1) * _MOD_ROWS]
        o_ref[0:_MOD_ROWS, :] = mod
        o_ref[_MOD_ROWS:, :] = acc[head:]

    vm = pl.BlockSpec(memory_space=pltpu.VMEM)
    return pl.pallas_call(
        body, name="small_all_reduce", in_specs=[vm, pl.BlockSpec(memory_space=pl.ANY)], out_specs=vm,
        out_shape=jax.ShapeDtypeStruct((out_rows, cols), F32),
        scratch_shapes=[pltpu.VMEM((N_DEV, rows, cols), F32), pltpu.SemaphoreType.DMA((N_DEV - 1,)),
                        pltpu.SemaphoreType.DMA((N_DEV - 1,))],
        compiler_params=pltpu.CompilerParams(vmem_limit_bytes=VMEM_LIMIT),
    )(buf, after)


_COL_SHARDED = ("w_ada", "ffn1_w_gu", "w_in", "ffn2_w_gu")
_ROW_SHARDED = ("ffn1_w_down", "w_sb_out", "w_conv_out", "w_out", "ffn2_w_down")
_NOW = ["w_ada", "ffn1_w_gu"]
_SOON = ["ffn1_w_down"]
_LATER = ["w_in", "w_sb_out", "w_conv_out", "w_out", "ffn2_w_gu", "ffn2_w_down"]
_VECS = ("b_ada", "ln1_g", "ln1_b", "conv_b", "conv_ln_g", "conv_ln_b", "ln2_g", "ln2_b", "ln3_g", "ln3_b")
_WEIGHTS = ("w_ada", "b_ada", "ffn1_w_gu", "ffn1_w_down", "ln1_g", "ln1_b", "w_in", "w_sb_out", "conv_w", "conv_b",
            "conv_ln_g", "conv_ln_b", "w_conv_out", "w_out", "ln2_g", "ln2_b", "ffn2_w_gu", "ffn2_w_down",
            "ln3_g", "ln3_b")


def _step(x, c, target, w, m, v):
    bsz = x.shape[0]
    chip = 2 * lax.axis_index("x") + lax.axis_index("y")
    core = lax.axis_index("c")

    chip_arr = jnp.reshape(chip, (1,)).astype(jnp.int32)
    place = jnp.stack([core, chip]).astype(jnp.int32)

    def stack_of(n):
        rows, cols = w[n].shape[1:]
        return _cast_into_stack(w[n][0], chip_arr, f"cast_{n}").reshape(N_CHIPS, 2, rows // 2, cols)

    def gathered_form(n, g):
        rows, cols = w[n].shape[1:]
        return g.reshape(N_CHIPS, rows, cols) if n in _COL_SHARDED else g.reshape(N_CHIPS * rows, cols)

    conv_w_local = jnp.pad(w["conv_w"][0], ((0, 1), (0, 0)))
    gathered = _all_gather_weights([stack_of(n) for n in _NOW], conv_w_local)
    wts = {n: gathered_form(n, g) for n, g in zip(_NOW, gathered[:-1])}
    wts["conv_w"] = gathered[-1].transpose(1, 0, 2).reshape(32, D_MODEL)
    pending, behind = {}, gathered[0]
    for stage, names in (("down", _SOON), ("later", _LATER)):
        send, recv, thru, token = _exchange_start(
            "gather", [stack_of(n) for n in names], None, f"gather_start_{stage}", behind)
        pending[stage] = (names, send, recv, thru)
        behind = token
    vecs = {n: w[n] for n in _VECS}
    vecs["b_ada"] = vecs["b_ada"] + behind[0, 0]

    def fetch(stage, after):
        names, send, recv, thru = pending[stage]
        landed, _ = _exchange_wait("gather", send, recv, thru, True, after, f"gather_wait_{stage}")
        forwarded = _gather_forward(landed, f"gather_forward_{stage}")
        return {n: gathered_form(n, g) for n, g in zip(names, forwarded)}

    groups = {"later": _LATER, "mid": ["ffn1_w_gu", "ffn1_w_down"], "last": ["w_ada"]}
    g_out, updates, state = {}, {}, {}

    def adam(names, after):
        for n in names:
            shape = w[n].shape
            flat = shape[1:] if len(shape) == 3 else shape
            d, nm, nv = _adamw(w[n].reshape(flat), g_out[n].reshape(flat), m[n].reshape(flat), v[n].reshape(flat),
                               f"adamw_{n}", after)
            updates[n] = (g_out[n].reshape(shape), d.reshape(shape), nm.reshape(shape), nv.reshape(shape))
            after = nv
        return after

    def swap_start(tag, grads):
        views = [grads[n].reshape(N_CHIPS, 2, w[n].shape[1] // 2, w[n].shape[2]) for n in groups[tag]]
        lands = [lax.empty((N_CHIPS,) + g.shape[2:], F32) for g in views]
        send, recv, thru, token = _exchange_start("swap", views, lands, f"swap_start_{tag}", place)
        state[tag] = {"swap": (send, recv, thru)}
        return token

    def scatter_start(tag, after):
        views, got = _exchange_wait("swap", *state[tag]["swap"], False, after, f"swap_wait_{tag}")
        sums = [_pair_add(g, r, place, f"pair_add_{n}") for n, g, r in zip(groups[tag], views, got)]
        lands = [lax.empty((3,) + p.shape[1:], BF16) for p, _ in sums]
        send, recv, thru, token = _exchange_start(
            "scatter", [p for p, _ in sums], lands, f"scatter_start_{tag}", place)
        state[tag].update(scatter=(send, recv, thru), sums=sums)
        return token

    def collect(tag, after):
        _, parts = _exchange_wait("scatter", *state[tag]["scatter"], False, after, f"scatter_wait_{tag}")
        halves = [_chip_sum(own, p, place, f"chip_sum_{n}")
                  for n, (_, own), p in zip(groups[tag], state[tag]["sums"], parts)]
        for n, f in zip(groups[tag], _pair_gather(halves, f"grad_pair_gather_{tag}")):
            g_out[n] = f.reshape(w[n].shape[1:])
        return g_out[groups[tag][-1]]

    def early_grads(stage, value):
        tag, step = stage.split("_")
        token = swap_start(tag, value) if step == "start" else scatter_start(tag, value)
        return token[0, 0]

    grad_x, grads, small = _local_step(x, c, target, wts, vecs, fetch, early_grads)

    token = swap_start("last", grads)
    done = collect("later", token)
    token = scatter_start("last", done)
    done = adam(groups["later"], token)
    done = collect("mid", done)
    done = adam(groups["mid"], done)
    done = collect("last", done)

    dmod = jnp.pad(small["dmod"], ((0, 0), (0, _MOD_ROWS - 9), (0, 0))).reshape(bsz * _MOD_ROWS, D_MODEL)
    loss_rows = jnp.pad(small["loss"], ((0, 0), (0, D_MODEL - LANES)))
    buf = jnp.concatenate([dmod, small["ln1"], small["ln2"], small["ln3"], small["conv"], small["conv_w"],
                           loss_rows], axis=0)
    red = _small_all_reduce(buf, bsz, done)
    o = _MOD_ROWS
    g_out["b_ada"] = red[0:9].reshape(1, 9 * D_MODEL)
    g_out["ln1_g"], g_out["ln1_b"] = red[o:o + 1], red[o + 1:o + 2]
    g_out["ln2_g"], g_out["ln2_b"] = red[o + 8:o + 9], red[o + 9:o + 10]
    g_out["ln3_g"], g_out["ln3_b"] = red[o + 16:o + 17], red[o + 17:o + 18]
    g_out["conv_ln_g"], g_out["conv_ln_b"], g_out["conv_b"] = red[o + 24:o + 25], red[o + 25:o + 26], red[o + 26:o + 27]
    cw = w["conv_w"].shape[2]
    g_out["conv_w"] = lax.dynamic_slice(red[o + 32:o + 32 + CONV_TAPS], (0, chip * cw), (CONV_TAPS, cw))
    loss = red[o + 64, 0]

    adam(groups["last"] + list(_VECS) + ["conv_w"], place)
    return (loss, grad_x, *[updates[n][k] for k in range(4) for n in _WEIGHTS])


def kernel(x, c, w_ada, b_ada, ffn1_w_gu, ffn1_w_down, ln1_g, ln1_b, w_in, w_sb_out, conv_w, conv_b, conv_ln_g, conv_ln_b, w_conv_out, w_out, ln2_g, ln2_b, ffn2_w_gu, ffn2_w_down, ln3_g, ln3_b, loss_target, m_w_ada, m_b_ada, m_ffn1_w_gu, m_ffn1_w_down, m_ln1_g, m_ln1_b, m_w_in, m_w_sb_out, m_conv_w, m_conv_b, m_conv_ln_g, m_conv_ln_b, m_w_conv_out, m_w_out, m_ln2_g, m_ln2_b, m_ffn2_w_gu, m_ffn2_w_down, m_ln3_g, m_ln3_b, v_w_ada, v_b_ada, v_ffn1_w_gu, v_ffn1_w_down, v_ln1_g, v_ln1_b, v_w_in, v_w_sb_out, v_conv_w, v_conv_b, v_conv_ln_g, v_conv_ln_b, v_w_conv_out, v_w_out, v_ln2_g, v_ln2_b, v_ffn2_w_gu, v_ffn2_w_down, v_ln3_g, v_ln3_b):
    given = dict(locals())
    w = {n: given[n] for n in _WEIGHTS}
    m = {n: given["m_" + n] for n in _WEIGHTS}
    v = {n: given["v_" + n] for n in _WEIGHTS}
    return _step(x, c, loss_target, w, m, v)
```

```python
import functools

import jax
import jax.numpy as jnp
from jax import lax
from jax.experimental import pallas as pl
from jax.experimental.pallas import tpu as pltpu

F32 = jnp.float32
BF16 = jnp.bfloat16

D_MODEL = 1024
D_FF = 2816
HEADS = 16
HEAD_DIM = 64
LANES = 128
CONV_TAPS = 31
HALO = 32
N_CHIPS = 4
N_DEV = 8
ALPHA = 2.0 ** 0.25
LN_EPS = 1e-5
ATT_BLOCK = 256
VMEM_LIMIT = 56 * 1024 * 1024

ADAM_LR = 0.001
ADAM_B1 = 0.9
ADAM_B2 = 0.999
ADAM_EPS = 1e-08
ADAM_WD = 0.01
ADAM_STEP = 10

MESH = pl.DeviceIdType.MESH


def _pick(n, cands):
    for t in cands:
        if t <= n and n % t == 0:
            return t
    return n


def _params(sem):
    return pltpu.CompilerParams(dimension_semantics=sem, vmem_limit_bytes=VMEM_LIMIT)


def _sigmoid(z):
    t = jnp.exp(-jnp.abs(z))
    return jnp.where(z >= 0, 1.0, t) / (1.0 + t)


def _silu(z):
    return z * _sigmoid(z)


def _dsilu(z):
    s = _sigmoid(z)
    return s * (1.0 + z * (1.0 - s))


def _ln_stats(r):
    mu = jnp.mean(r, axis=-1, keepdims=True)
    d = r - mu
    var = jnp.mean(d * d, axis=-1, keepdims=True)
    rstd = lax.rsqrt(var + LN_EPS)
    return d * rstd, rstd


def _colsum(v):
    return jnp.sum(v, axis=0, keepdims=True)


_DIMS = {"nn": (((1,), (0,)), ((), ())), "nt": (((1,), (1,)), ((), ())), "tn": (((0,), (0,)), ((), ()))}
_TN_CANDS = (1408, 1792, 1152, 1024, 512, 256, 128)
_TK_CANDS = (2816, 1024, 1408, 1792, 512, 256, 128)


def _matmul(a, b, *, mode, out_dtype, name, bias=None, out_stacked=False):
    a_halves = mode == "nt" and a.ndim == 3
    b_halves = mode == "tn" and b.ndim == 3
    b_stacked = b.ndim == 3 and not b_halves
    if mode == "nn":
        m, k = a.shape
        n_c = b.shape[-1]
        n = n_c * (N_CHIPS if b_stacked else 1)
        k_c = k
    elif mode == "nt":
        m = a.shape[-2]
        k = a.shape[-1] * (2 if a_halves else 1)
        n = b.shape[-2]
        k_c = b.shape[-1]
        n_c = n
    else:
        k, m = a.shape
        n = b.shape[-1] * (2 if b_halves else 1)
        n_c = n // N_CHIPS if out_stacked else n
        k_c = k
    if mode == "tn":
        tm = _pick(m, (1024, 1408, 512, 256, 128))
        tk = _pick(k, (2048, 1024, 512, 256, 128, 64, 32, 16))
    else:
        tm = _pick(m, (1024, 512, 256, 128, 64, 32, 16))
        tk = _pick(k_c, _TK_CANDS)
    tn = _pick(n_c, _TN_CANDS)
    nb = n_c // tn
    kb = k_c // tk
    nk = k // tk
    grid = (m // tm, n // tn, nk)

    if mode == "nn":
        a_spec = pl.BlockSpec((tm, tk), lambda i, j, kk: (i, kk))
        if b_stacked:
            b_spec = pl.BlockSpec((None, tk, tn), lambda i, j, kk: (j // nb, kk, j % nb))
        else:
            b_spec = pl.BlockSpec((tk, tn), lambda i, j, kk: (kk, j))
    elif mode == "nt":
        if a_halves:
            ka = a.shape[-1] // tk
            a_spec = pl.BlockSpec((None, tm, tk), lambda i, j, kk: (kk // ka, i, kk % ka))
        else:
            a_spec = pl.BlockSpec((tm, tk), lambda i, j, kk: (i, kk))
        if b_stacked:
            b_spec = pl.BlockSpec((None, tn, tk), lambda i, j, kk: (kk // kb, j, kk % kb))
        else:
            b_spec = pl.BlockSpec((tn, tk), lambda i, j, kk: (j, kk))
    else:
        a_spec = pl.BlockSpec((tk, tm), lambda i, j, kk: (kk, i))
        if b_halves:
            nh = b.shape[-1] // tn
            b_spec = pl.BlockSpec((None, tk, tn), lambda i, j, kk: (j // nh, kk, j % nh))
        else:
            b_spec = pl.BlockSpec((tk, tn), lambda i, j, kk: (kk, j))
    if out_stacked:
        out_shape = jax.ShapeDtypeStruct((N_CHIPS, m, n_c), out_dtype)
        o_spec = pl.BlockSpec((None, tm, tn), lambda i, j, kk: (j // nb, i, j % nb))
    else:
        out_shape = jax.ShapeDtypeStruct((m, n), out_dtype)
        o_spec = pl.BlockSpec((tm, tn), lambda i, j, kk: (i, j))
    in_specs = [a_spec, b_spec]
    args = [a, b]
    if bias is not None:
        in_specs.append(pl.BlockSpec((1, tn), lambda i, j, kk: (0, j)))
        args.append(bias)
    dims = _DIMS[mode]

    def body(*refs):
        a_ref, b_ref = refs[0], refs[1]
        bias_ref = refs[2] if bias is not None else None

        def write(r):
            if bias_ref is not None:
                r = r + bias_ref[...]
            o_ref[...] = r.astype(o_ref.dtype)

        if nk == 1:
            o_ref = refs[-1]
            write(lax.dot_general(a_ref[...], b_ref[...], dims, preferred_element_type=F32))
            return
        o_ref, acc_ref = refs[-2], refs[-1]
        kk = pl.program_id(2)

        @pl.when(kk == 0)
        def _():
            acc_ref[...] = jnp.zeros_like(acc_ref)

        acc_ref[...] += lax.dot_general(a_ref[...], b_ref[...], dims, preferred_element_type=F32)

        @pl.when(kk == nk - 1)
        def _():
            write(acc_ref[...])

    return pl.pallas_call(
        body, name=name, grid=grid, in_specs=in_specs, out_specs=o_spec, out_shape=out_shape,
        scratch_shapes=[pltpu.VMEM((tm, tn), F32)] if nk > 1 else [],
        compiler_params=_params(("parallel", "parallel", "arbitrary")),
    )(*args)


def _row_grid(bsz, seq, ts):
    ns = seq // ts
    return (bsz, ns), ns


def _rows(ts, width, ns, col=0):
    return pl.BlockSpec((ts, width), lambda b, s: (b * ns + s, col))


def _mod_spec():
    return pl.BlockSpec((None, 9, D_MODEL), lambda b, s: (b, 0, 0))


def _vec_spec(rows=1, width=D_MODEL):
    return pl.BlockSpec((rows, width), lambda b, s: (0, 0))


def _silu_pad(c):
    bsz = c.shape[0]

    def body(c_ref, o_ref):
        o_ref[...] = jnp.zeros_like(o_ref)
        o_ref[0:bsz, :] = _silu(c_ref[...]).astype(BF16)

    return pl.pallas_call(body, name="silu_pad", out_shape=jax.ShapeDtypeStruct((16, D_MODEL), BF16))(c)


def _mod_in(x, mod, bsz, seq, sub):
    ts = _pick(seq, (512, 256, 128))
    grid, ns = _row_grid(bsz, seq, ts)

    def body(x_ref, mod_ref, u_ref):
        sh = mod_ref[3 * sub:3 * sub + 1, :]
        sc = mod_ref[3 * sub + 1:3 * sub + 2, :]
        u_ref[...] = (x_ref[...] * (1.0 + sc) + sh).astype(BF16)

    return pl.pallas_call(
        body, name=f"mod_in{sub}", grid=grid, in_specs=[_rows(ts, D_MODEL, ns), _mod_spec()],
        out_specs=_rows(ts, D_MODEL, ns), out_shape=jax.ShapeDtypeStruct(x.shape, BF16),
        compiler_params=_params(("parallel", "parallel")),
    )(x, mod)


_FFN_TN = D_FF // 2


def _ffn_up_act(u, w_gu, name):
    t = u.shape[0]
    tm = _pick(t, (512, 256, 128))
    tn = _FFN_TN

    def body(u_ref, wa_ref, wg_ref, h_ref, p_ref):
        u_v = u_ref[...]
        a = jnp.dot(u_v, wa_ref[...], preferred_element_type=F32)
        g = jnp.dot(u_v, wg_ref[...], preferred_element_type=F32)
        h_ref[0] = a.astype(BF16)
        h_ref[1] = g.astype(BF16)
        p_ref[...] = (_silu(a) * g).astype(BF16)

    return pl.pallas_call(
        body, name=name, grid=(2, t // tm),
        in_specs=[pl.BlockSpec((tm, D_MODEL), lambda j, i: (i, 0)),
                  pl.BlockSpec((None, D_MODEL, tn), lambda j, i: (j, 0, 0)),
                  pl.BlockSpec((None, D_MODEL, tn), lambda j, i: (j + 2, 0, 0))],
        out_specs=[pl.BlockSpec((2, tm, tn), lambda j, i: (0, i, j)),
                   pl.BlockSpec((tm, tn), lambda j, i: (i, j))],
        out_shape=[jax.ShapeDtypeStruct((2, t, D_FF), BF16), jax.ShapeDtypeStruct((t, D_FF), BF16)],
        compiler_params=_params(("parallel", "parallel")),
    )(u, w_gu, w_gu)


def _ffn_down_bwd_act(df, w_down, h, name):
    t = df.shape[0]
    tm = _pick(t, (512, 256, 128))
    tn = _FFN_TN

    def body(df_ref, wd_ref, h_ref, dh_ref):
        dp = lax.dot_general(df_ref[...], wd_ref[...], _DIMS["nt"], preferred_element_type=F32)
        a = h_ref[0].astype(F32)
        g = h_ref[1].astype(F32)
        dh_ref[0] = (dp * g * _dsilu(a)).astype(BF16)
        dh_ref[1] = (dp * _silu(a)).astype(BF16)

    blk = pl.BlockSpec((2, tm, tn), lambda j, i: (0, i, j))
    return pl.pallas_call(
        body, name=name, grid=(2, t // tm),
        in_specs=[pl.BlockSpec((tm, D_MODEL), lambda j, i: (i, 0)),
                  pl.BlockSpec((tn, D_MODEL), lambda j, i: (j, 0)), blk],
        out_specs=blk, out_shape=jax.ShapeDtypeStruct((2, t, D_FF), BF16),
        compiler_params=_params(("parallel", "parallel")),
    )(df, w_down, h)


def _res_ln_fwd(x, f, mod, ln_g, ln_b, bsz, seq, sub, weight, target=None):
    ts = _pick(seq, (512, 256, 128))
    grid, ns = _row_grid(bsz, seq, ts)
    last = target is not None

    def body(*refs):
        x_ref, f_ref, mod_ref, g_ref, b_ref = refs[:5]
        gate = mod_ref[3 * sub + 2:3 * sub + 3, :]
        r = ALPHA * x_ref[...] + gate * (weight * f_ref[...])
        xhat, _ = _ln_stats(r)
        xo = xhat * g_ref[...] + b_ref[...]
        if last:
            t_ref, r_ref, dy_ref, loss_ref = refs[5:]
            diff = xo - t_ref[...]
            dy_ref[...] = diff * (1.0 / D_MODEL)
            part = 0.5 * jnp.sum(jnp.mean(diff * diff, axis=-1, keepdims=True), axis=0, keepdims=True)

            @pl.when((pl.program_id(0) == 0) & (pl.program_id(1) == 0))
            def _():
                loss_ref[...] = jnp.zeros_like(loss_ref)

            loss_ref[...] += jnp.broadcast_to(part, loss_ref.shape)
        else:
            r_ref, xo_ref, u_ref = refs[5:]
            xo_ref[...] = xo
            sh = mod_ref[3 * sub + 3:3 * sub + 4, :]
            sc = mod_ref[3 * sub + 4:3 * sub + 5, :]
            u_ref[...] = (xo * (1.0 + sc) + sh).astype(BF16)
        r_ref[...] = r

    row = _rows(ts, D_MODEL, ns)
    in_specs = [row, row, _mod_spec(), _vec_spec(), _vec_spec()]
    args = [x, f, mod, ln_g, ln_b]
    if last:
        in_specs.append(row)
        args.append(target)
        out_specs = [row, row, _vec_spec(8, LANES)]
        out_shape = [jax.ShapeDtypeStruct(x.shape, F32), jax.ShapeDtypeStruct(x.shape, F32),
                     jax.ShapeDtypeStruct((8, LANES), F32)]
        sem = ("arbitrary", "arbitrary")
    else:
        out_specs = [row, row, row]
        out_shape = [jax.ShapeDtypeStruct(x.shape, F32), jax.ShapeDtypeStruct(x.shape, F32),
                     jax.ShapeDtypeStruct(x.shape, BF16)]
        sem = ("parallel", "parallel")
    return pl.pallas_call(
        body, name=f"res_ln_fwd{sub}", grid=grid, in_specs=in_specs, out_specs=out_specs, out_shape=out_shape,
        compiler_params=_params(sem),
    )(*args)


def _res_ln_bwd(r, dxo, f, mod, ln_g, bsz, seq, sub, weight, above=None):
    ts = _pick(seq, (512, 256, 128))
    grid, ns = _row_grid(bsz, seq, ts)
    folded = above is not None

    def body(*refs):
        r_ref, dxo_ref, f_ref, mod_ref, g_ref = refs[:5]
        dxres_ref, df_ref, lns_ref, gs_ref = refs[-5:-1] if folded else refs[-4:]
        b, s = pl.program_id(0), pl.program_id(1)
        gate = mod_ref[3 * sub + 2:3 * sub + 3, :]
        xhat, rstd = _ln_stats(r_ref[...])
        dxo_v = dxo_ref[...]
        if folded:
            du_v = refs[5][...]
            ms_ref = refs[-1]
            dxo_v = dxo_v + du_v * (1.0 + mod_ref[3 * sub + 4:3 * sub + 5, :])

            @pl.when(s == 0)
            def _():
                ms_ref[...] = jnp.zeros_like(ms_ref)

            ms_ref[0:1, :] += _colsum(du_v)
            ms_ref[1:2, :] += _colsum(du_v * refs[6][...])
        dxhat = dxo_v * g_ref[...]
        m1 = jnp.mean(dxhat, axis=-1, keepdims=True)
        m2 = jnp.mean(dxhat * xhat, axis=-1, keepdims=True)
        dr = rstd * (dxhat - m1 - xhat * m2)
        dxres_ref[...] = ALPHA * dr
        df_ref[...] = (dr * (gate * weight)).astype(BF16)

        @pl.when((b == 0) & (s == 0))
        def _():
            lns_ref[...] = jnp.zeros_like(lns_ref)

        @pl.when(s == 0)
        def _():
            gs_ref[...] = jnp.zeros_like(gs_ref)

        lns_ref[0:1, :] += _colsum(dxo_v * xhat)
        lns_ref[1:2, :] += _colsum(dxo_v)
        gs_ref[0:1, :] += _colsum(dr * (weight * f_ref[...]))

    row = _rows(ts, D_MODEL, ns)
    per_sample = pl.BlockSpec((None, 8, D_MODEL), lambda b, s: (b, 0, 0))
    stats = jax.ShapeDtypeStruct((bsz, 8, D_MODEL), F32)
    extra = 1 if folded else 0
    return pl.pallas_call(
        body, name=f"res_ln_bwd{sub}", grid=grid,
        in_specs=[row, row, row, _mod_spec(), _vec_spec()] + [row, row] * extra,
        out_specs=[row, row, _vec_spec(8), per_sample] + [per_sample] * extra,
        out_shape=[jax.ShapeDtypeStruct(r.shape, F32), jax.ShapeDtypeStruct(r.shape, BF16),
                   jax.ShapeDtypeStruct((8, D_MODEL), F32), stats] + [stats] * extra,
        compiler_params=_params(("arbitrary", "arbitrary")),
    )(r, dxo, f, mod, ln_g, *(above or ()))


def _mod_bwd(dxres, du, x, mod, bsz, seq, sub):
    ts = _pick(seq, (512, 256, 128))
    grid, ns = _row_grid(bsz, seq, ts)

    def body(dxres_ref, du_ref, x_ref, mod_ref, dx_ref, st_ref):
        s = pl.program_id(1)
        sc = mod_ref[3 * sub + 1:3 * sub + 2, :]
        du_v = du_ref[...]
        dx_ref[...] = dxres_ref[...] + du_v * (1.0 + sc)

        @pl.when(s == 0)
        def _():
            st_ref[...] = jnp.zeros_like(st_ref)

        st_ref[0:1, :] += _colsum(du_v)
        st_ref[1:2, :] += _colsum(du_v * x_ref[...])

    row = _rows(ts, D_MODEL, ns)
    return pl.pallas_call(
        body, name=f"mod_bwd{sub}", grid=grid, in_specs=[row, row, row, _mod_spec()],
        out_specs=[row, pl.BlockSpec((None, 8, D_MODEL), lambda b, s: (b, 0, 0))],
        out_shape=[jax.ShapeDtypeStruct(x.shape, F32), jax.ShapeDtypeStruct((bsz, 8, D_MODEL), F32)],
        compiler_params=_params(("parallel", "arbitrary")),
    )(dxres, du, x, mod)


_COL_GLU_A, _COL_GLU_B, _COL_GATE_A, _COL_GATE_B = 3, 4, 5, 6


def _merge_fwd(proj, ya, cs, w_sb, w_co):
    t = ya.shape[0]
    tm = _pick(t, (512, 256, 128))

    def body(ga_ref, gb_ref, ya_ref, cs_ref, wsb_ref, wco_ref, m_ref, ysb_ref, yc_ref):
        ysb = jnp.dot(ya_ref[...], wsb_ref[...], preferred_element_type=F32)
        yc = jnp.dot(cs_ref[...], wco_ref[...], preferred_element_type=F32)
        ysb_ref[...] = ysb
        yc_ref[...] = yc
        m_ref[...] = (_sigmoid(ga_ref[...]) * ysb + _sigmoid(gb_ref[...]) * yc).astype(BF16)

    row = pl.BlockSpec((tm, D_MODEL), lambda i: (i, 0))
    full = pl.BlockSpec((D_MODEL, D_MODEL), lambda i: (0, 0))
    return pl.pallas_call(
        body, name="merge_fwd", grid=(t // tm,),
        in_specs=[pl.BlockSpec((tm, D_MODEL), lambda i: (i, _COL_GATE_A)),
                  pl.BlockSpec((tm, D_MODEL), lambda i: (i, _COL_GATE_B)), row, row, full, full],
        out_specs=[row, row, row],
        out_shape=[jax.ShapeDtypeStruct((t, D_MODEL), BF16), jax.ShapeDtypeStruct((t, D_MODEL), F32),
                   jax.ShapeDtypeStruct((t, D_MODEL), F32)],
        compiler_params=_params(("parallel",)),
    )(proj, proj, ya, cs, w_sb, w_co)


def _merge_bwd(proj, ysb, yconv, do2, w_out):
    t = ysb.shape[0]
    tm = _pick(t, (512, 256, 128))

    def body(ga_ref, gb_ref, ysb_ref, yc_ref, do_ref, w_ref, dysb_ref, dyc_ref, dg_ref):
        dm = lax.dot_general(do_ref[...], w_ref[...], _DIMS["nt"], preferred_element_type=F32)
        sa = _sigmoid(ga_ref[...])
        sb = _sigmoid(gb_ref[...])
        dysb_ref[...] = (dm * sa).astype(BF16)
        dyc_ref[...] = (dm * sb).astype(BF16)
        dg_ref[:, :D_MODEL] = (dm * ysb_ref[...] * sa * (1.0 - sa)).astype(BF16)
        dg_ref[:, D_MODEL:] = (dm * yc_ref[...] * sb * (1.0 - sb)).astype(BF16)

    row = pl.BlockSpec((tm, D_MODEL), lambda i: (i, 0))
    return pl.pallas_call(
        body, name="merge_bwd", grid=(t // tm,),
        in_specs=[pl.BlockSpec((tm, D_MODEL), lambda i: (i, _COL_GATE_A)),
                  pl.BlockSpec((tm, D_MODEL), lambda i: (i, _COL_GATE_B)), row, row, row,
                  pl.BlockSpec((D_MODEL, D_MODEL), lambda i: (0, 0))],
        out_specs=[row, row, pl.BlockSpec((tm, 2 * D_MODEL), lambda i: (i, 0))],
        out_shape=[jax.ShapeDtypeStruct((t, D_MODEL), BF16), jax.ShapeDtypeStruct((t, D_MODEL), BF16),
                   jax.ShapeDtypeStruct((t, 2 * D_MODEL), BF16)],
        compiler_params=_params(("parallel",)),
    )(proj, proj, ysb, yconv, do2, w_out)


_CONV_ROWS = 128


def _halo_prev(tt, ns, col):
    r = tt // HALO
    return pl.BlockSpec((HALO, D_MODEL), lambda b, s: (jnp.maximum((b * ns + s) * r - 1, 0), col))


def _halo_next(tt, ns, nblk, col):
    r = tt // HALO
    return pl.BlockSpec((HALO, D_MODEL), lambda b, s: (jnp.minimum((b * ns + s + 1) * r, nblk - 1), col))


def _windows(pad_ref, stage_ref, tt, offsets, cols):
    for r in range(8):
        mine = [o for o in offsets if o % 8 == r]
        if not mine:
            continue
        n = max(mine) - r + tt
        stage_ref[0:n, cols] = pad_ref[r:r + n, cols]
        for o in mine:
            yield o, stage_ref[o - r:o - r + tt, cols]


def _column_chunks():
    return [slice(c, c + LANES) for c in range(0, D_MODEL, LANES)]


def _fill_hc(hpad, a_ref, b_ref, ha_ref, hb_ref, s):
    halo = ha_ref[...] * _sigmoid(hb_ref[...])
    hpad[0:HALO, :] = jnp.where(s > 0, halo, 0.0)
    hpad[HALO:, :] = a_ref[...] * _sigmoid(b_ref[...])


def _conv_fwd(proj, conv_w, conv_b, ln_g, ln_b, bsz, seq):
    tt = _CONV_ROWS
    grid, ns = _row_grid(bsz, seq, tt)
    off = HALO - (CONV_TAPS - 1)

    def body(a_ref, b_ref, ha_ref, hb_ref, w_ref, cb_ref, g_ref, bb_ref, cs_ref, cv_ref, hpad, stage):
        _fill_hc(hpad, a_ref, b_ref, ha_ref, hb_ref, pl.program_id(1))
        for cols in _column_chunks():
            acc = jnp.zeros((tt, LANES), F32)
            for o, win in _windows(hpad, stage, tt, [off + j for j in range(CONV_TAPS)], cols):
                acc = acc + w_ref[o - off:o - off + 1, cols] * win
            cv_ref[:, cols] = acc + cb_ref[:, cols]
        xhat, _ = _ln_stats(cv_ref[...])
        cs_ref[...] = _silu(xhat * g_ref[...] + bb_ref[...]).astype(BF16)

    row = _rows(tt, D_MODEL, ns)
    t = proj.shape[0]
    return pl.pallas_call(
        body, name="conv_fwd", grid=grid,
        in_specs=[_rows(tt, D_MODEL, ns, _COL_GLU_A), _rows(tt, D_MODEL, ns, _COL_GLU_B),
                  _halo_prev(tt, ns, _COL_GLU_A), _halo_prev(tt, ns, _COL_GLU_B),
                  _vec_spec(32), _vec_spec(), _vec_spec(), _vec_spec()],
        out_specs=[row, row],
        out_shape=[jax.ShapeDtypeStruct((t, D_MODEL), BF16), jax.ShapeDtypeStruct((t, D_MODEL), F32)],
        scratch_shapes=[pltpu.VMEM((HALO + tt, D_MODEL), F32), pltpu.VMEM((HALO + tt, D_MODEL), F32)],
        compiler_params=_params(("parallel", "parallel")),
    )(proj, proj, proj, proj, conv_w, conv_b, ln_g, ln_b)


def _conv_bwd_ln(dcs, cv, ln_g, ln_b, bsz, seq):
    ts = _pick(seq, (512, 256, 128))
    grid, ns = _row_grid(bsz, seq, ts)

    def body(dcs_ref, cv_ref, g_ref, b_ref, dcv_ref, st_ref):
        xhat, rstd = _ln_stats(cv_ref[...])
        cl = xhat * g_ref[...] + b_ref[...]
        dcl = dcs_ref[...] * _dsilu(cl)
        dxhat = dcl * g_ref[...]
        m1 = jnp.mean(dxhat, axis=-1, keepdims=True)
        m2 = jnp.mean(dxhat * xhat, axis=-1, keepdims=True)
        dcv = rstd * (dxhat - m1 - xhat * m2)
        dcv_ref[...] = dcv

        @pl.when((pl.program_id(0) == 0) & (pl.program_id(1) == 0))
        def _():
            st_ref[...] = jnp.zeros_like(st_ref)

        st_ref[0:1, :] += _colsum(dcl * xhat)
        st_ref[1:2, :] += _colsum(dcl)
        st_ref[2:3, :] += _colsum(dcv)

    row = _rows(ts, D_MODEL, ns)
    return pl.pallas_call(
        body, name="conv_bwd_ln", grid=grid, in_specs=[row, row, _vec_spec(), _vec_spec()],
        out_specs=[row, _vec_spec(8)],
        out_shape=[jax.ShapeDtypeStruct(cv.shape, F32), jax.ShapeDtypeStruct((8, D_MODEL), F32)],
        compiler_params=_params(("arbitrary", "arbitrary")),
    )(dcs, cv, ln_g, ln_b)


def _conv_bwd_taps(proj, dcv, conv_w, bsz, seq):
    tt = _CONV_ROWS
    grid, ns = _row_grid(bsz, seq, tt)
    off = HALO - (CONV_TAPS - 1)
    nblk = proj.shape[0] // HALO

    def body(a_ref, b_ref, ha_ref, hb_ref, d_ref, dn_ref, w_ref, dglu_ref, dw_ref, hpad, dpad, stage):
        s = pl.program_id(1)
        _fill_hc(hpad, a_ref, b_ref, ha_ref, hb_ref, s)
        dpad[0:tt, :] = d_ref[...]
        dpad[tt:, :] = jnp.where(s < ns - 1, dn_ref[...], 0.0)

        @pl.when((pl.program_id(0) == 0) & (s == 0))
        def _():
            dw_ref[...] = jnp.zeros_like(dw_ref)

        for cols in _column_chunks():
            dcv = d_ref[:, cols]
            dhc = jnp.zeros((tt, LANES), F32)
            for o, win in _windows(dpad, stage, tt, list(range(CONV_TAPS)), cols):
                j = CONV_TAPS - 1 - o
                dhc = dhc + w_ref[j:j + 1, cols] * win
            for o, win in _windows(hpad, stage, tt, [off + j for j in range(CONV_TAPS)], cols):
                dw_ref[o - off:o - off + 1, cols] += _colsum(dcv * win)
            sb = _sigmoid(b_ref[:, cols])
            dglu_ref[:, cols] = (dhc * sb).astype(BF16)
            dglu_ref[:, slice(D_MODEL + cols.start, D_MODEL + cols.stop)] = (
                dhc * a_ref[:, cols] * sb * (1.0 - sb)).astype(BF16)

    t = proj.shape[0]
    return pl.pallas_call(
        body, name="conv_bwd_taps", grid=grid,
        in_specs=[_rows(tt, D_MODEL, ns, _COL_GLU_A), _rows(tt, D_MODEL, ns, _COL_GLU_B),
                  _halo_prev(tt, ns, _COL_GLU_A), _halo_prev(tt, ns, _COL_GLU_B),
                  _rows(tt, D_MODEL, ns), _halo_next(tt, ns, nblk, 0), _vec_spec(32)],
        out_specs=[_rows(tt, 2 * D_MODEL, ns), _vec_spec(32)],
        out_shape=[jax.ShapeDtypeStruct((t, 2 * D_MODEL), BF16), jax.ShapeDtypeStruct((32, D_MODEL), F32)],
        scratch_shapes=[pltpu.VMEM((HALO + tt, D_MODEL), F32), pltpu.VMEM((tt + HALO, D_MODEL), F32),
                        pltpu.VMEM((HALO + tt, D_MODEL), F32)],
        compiler_params=_params(("arbitrary", "arbitrary")),
    )(proj, proj, proj, proj, dcv, dcv, conv_w)


_NT = (((1,), (1,)), ((), ()))
_TN = (((0,), (0,)), ((), ()))


def _dot(a, b, dims=None):
    if dims is None:
        return jnp.dot(a, b, preferred_element_type=F32)
    return lax.dot_general(a, b, dims, preferred_element_type=F32)


def _tri_dot(v, tri2):
    hi = v.astype(BF16)
    lo = (v - hi.astype(F32)).astype(BF16)
    return _dot(jnp.concatenate([hi, lo], axis=1), tri2)


def _tri2(mask):
    t = mask.astype(BF16)
    return jnp.concatenate([t, t], axis=0)


def _softplus_parts(z):
    t = jnp.exp(-jnp.abs(z))
    den = 1.0 + t
    return jnp.maximum(z, 0.0) + jnp.log(den), t, den


def _attn_fwd(proj, bsz, seq):
    blk = ATT_BLOCK
    nq = seq // blk
    n_pairs = D_MODEL // LANES

    def body(q_ref, k_ref, v_ref, y_ref, rt_ref, zr_buf, ns_buf, run_buf, acc_buf):
        qi = pl.program_id(2)
        lane = lax.broadcasted_iota(jnp.int32, (blk, LANES), 1)
        first = lane < HEAD_DIM
        q2 = q_ref[...] * 0.125
        q_heads = (jnp.where(first, q2, 0.0).astype(BF16), jnp.where(first, 0.0, q2).astype(BF16))
        rr = lax.broadcasted_iota(jnp.int32, (blk, blk), 0)
        cc = lax.broadcasted_iota(jnp.int32, (blk, blk), 1)
        tri_ge = _tri2(rr >= cc)
        causal = cc < rr

        def scores(kb, slot, masked, heads=(0, 1)):
            k_blk = k_ref[pl.ds(pl.multiple_of(kb * blk, blk), blk), :].astype(BF16)
            for h in heads:
                z = _dot(q_heads[h], k_blk, _NT)
                if masked:
                    z = jnp.where(causal, z, -1e30)
                sp, _, _ = _softplus_parts(z)
                neg = -sp
                zr_buf[slot, h] = z + _tri_dot(neg, tri_ge)
                ns_buf[slot, h] = jnp.sum(neg, axis=1, keepdims=True)

        def weigh(kb, slot, heads=(0, 1)):
            v_blk = v_ref[pl.ds(pl.multiple_of(kb * blk, blk), blk), :].astype(BF16)
            for h in heads:
                run = run_buf[h]
                w = jnp.exp(zr_buf[slot, h] + run)
                acc_buf[h] += _dot(w.astype(BF16), v_blk)
                run_buf[h] = run + ns_buf[slot, h]

        def step(kb_next, kb, slot):
            for h in range(2):
                scores(kb_next, 1 - slot, False, (h,))
                weigh(kb, slot, (h,))

        run_buf[...] = jnp.zeros_like(run_buf)
        acc_buf[...] = jnp.zeros_like(acc_buf)
        scores(qi, 0, True)

        def two_steps(p, carry):
            t = 2 * p
            step(qi - t - 1, qi - t, 0)
            step(qi - t - 2, qi - t - 1, 1)
            return carry

        lax.fori_loop(0, qi // 2, two_steps, 0)

        @pl.when(qi % 2 == 1)
        def _():
            step(0, 1, 0)
            weigh(0, 1)

        @pl.when(qi % 2 == 0)
        def _():
            weigh(0, 0)

        y_ref[...] = jnp.where(first, acc_buf[0], acc_buf[1]).astype(BF16)
        rt_ref[...] = jnp.where(first, jnp.broadcast_to(run_buf[0], (blk, LANES)),
                                jnp.broadcast_to(run_buf[1], (blk, LANES)))

    t = proj.shape[0]
    q_spec = pl.BlockSpec((blk, LANES), lambda b, p, i: (b * nq + i, p))
    return pl.pallas_call(
        body, name="attn_fwd", grid=(bsz, n_pairs, nq),
        in_specs=[q_spec,
                  pl.BlockSpec((seq, LANES), lambda b, p, i: (b, n_pairs + p)),
                  pl.BlockSpec((seq, LANES), lambda b, p, i: (b, 2 * n_pairs + p))],
        out_specs=[q_spec, q_spec],
        out_shape=[jax.ShapeDtypeStruct((t, D_MODEL), BF16), jax.ShapeDtypeStruct((t, D_MODEL), F32)],
        scratch_shapes=[pltpu.VMEM((2, 2, blk, blk), F32), pltpu.VMEM((2, 2, blk, 1), F32),
                        pltpu.VMEM((2, blk, 1), F32), pltpu.VMEM((2, blk, LANES), F32)],
        compiler_params=_params(("parallel", "parallel", "arbitrary")),
    )(proj, proj, proj)


def _attn_bwd(proj, rtot, dy, bsz, seq):
    blk = ATT_BLOCK
    nq = seq // blk
    n_pairs = D_MODEL // LANES

    def body(q_ref, k_ref, v_ref, dy_ref, rt_ref, dq_ref, dk_ref, dv_ref, dk_acc, dv_acc,
             a_buf, sig_buf, dw_buf, ns_buf, pre_buf, es_buf, dq_buf):
        qi = pl.program_id(2)

        @pl.when(qi == 0)
        def _():
            dk_acc[...] = jnp.zeros_like(dk_acc)
            dv_acc[...] = jnp.zeros_like(dv_acc)

        lane = lax.broadcasted_iota(jnp.int32, (blk, LANES), 1)
        first = lane < HEAD_DIM
        head_row = lax.broadcasted_iota(jnp.int32, (LANES, blk), 0) < HEAD_DIM
        q2 = q_ref[...] * 0.125
        q_rows = (jnp.where(first, q2, 0.0).astype(BF16), jnp.where(first, 0.0, q2).astype(BF16))
        q_t = q2.T
        q_heads = (jnp.where(head_row, q_t, 0.0).astype(BF16), jnp.where(head_row, 0.0, q_t).astype(BF16))
        dy2 = dy_ref[...].astype(F32)
        dy_rows = (jnp.where(first, dy2, 0.0).astype(BF16), jnp.where(first, 0.0, dy2).astype(BF16))
        dy_t = dy2.T
        dy_heads = (jnp.where(head_row, dy_t, 0.0).astype(BF16), jnp.where(head_row, 0.0, dy_t).astype(BF16))
        rt_t = rt_ref[...].T
        rt = (rt_t[0:1, :], rt_t[HEAD_DIM:HEAD_DIM + 1, :])
        rr = lax.broadcasted_iota(jnp.int32, (blk, blk), 0)
        cc = lax.broadcasted_iota(jnp.int32, (blk, blk), 1)
        lower = (cc < rr).astype(BF16)
        lower_eq = (cc <= rr).astype(BF16)
        tri_lt = jnp.concatenate([lower, lower], axis=1)
        tri_le = jnp.concatenate([lower_eq, lower_eq], axis=1)
        causal = rr < cc

        def tri_left(tri2, v):
            hi = v.astype(BF16)
            lo = (v - hi.astype(F32)).astype(BF16)
            return _dot(tri2, jnp.concatenate([hi, lo], axis=0))

        def scores(kb, slot, heads=(0, 1)):
            rows = pl.ds(pl.multiple_of(kb * blk, blk), blk)
            k_blk = k_ref[rows, :].astype(BF16)
            v_blk = v_ref[rows, :].astype(BF16)
            keep = jnp.logical_or(causal, kb < qi)
            for h in heads:
                z = jnp.where(keep, _dot(k_blk, q_heads[h]), -1e30)
                sp, t, den = _softplus_parts(z)
                neg = -sp
                a_buf[slot, h] = z - tri_left(tri_lt, neg)
                sig_buf[slot, h] = jnp.where(z >= 0, 1.0, t) / den
                ns_buf[slot, h] = jnp.sum(neg, axis=0, keepdims=True)
                dw_buf[slot, h] = _dot(v_blk, dy_heads[h])

        def finish(kb, slot, heads=(0, 1)):
            rows = pl.ds(pl.multiple_of(kb * blk, blk), blk)
            k_t = k_ref[rows, :].T.astype(BF16)
            for h in heads:
                pre, esum = pre_buf[h], es_buf[h]
                w = jnp.exp(a_buf[slot, h] + (rt[h] - pre))
                e = dw_buf[slot, h] * w
                dz = e - sig_buf[slot, h] * (esum + tri_left(tri_le, e))
                pre_buf[h] = pre + ns_buf[slot, h]
                es_buf[h] = esum + jnp.sum(e, axis=0, keepdims=True)
                dzb = dz.astype(BF16)
                dq_buf[h] += _dot(k_t, dzb)
                dk_acc[rows, :] += _dot(dzb, q_rows[h])
                dv_acc[rows, :] += _dot(w.astype(BF16), dy_rows[h])

        def step(kb_next, kb, slot):
            scores(kb_next, 1 - slot)
            finish(kb, slot)

        pre_buf[...] = jnp.zeros_like(pre_buf)
        es_buf[...] = jnp.zeros_like(es_buf)
        dq_buf[...] = jnp.zeros_like(dq_buf)
        scores(0, 0)

        def two_steps(p, carry):
            t = 2 * p
            step(t + 1, t, 0)
            step(t + 2, t + 1, 1)
            return carry

        lax.fori_loop(0, qi // 2, two_steps, 0)

        @pl.when(qi % 2 == 1)
        def _():
            step(qi, qi - 1, 0)
            finish(qi, 1)

        @pl.when(qi % 2 == 0)
        def _():
            finish(qi, 0)

        dq_ref[...] = (jnp.where(head_row, dq_buf[0], dq_buf[1]).T * 0.125).astype(BF16)

        @pl.when(qi == nq - 1)
        def _():
            dk_ref[...] = dk_acc[...].astype(BF16)
            dv_ref[...] = dv_acc[...].astype(BF16)

    t = proj.shape[0]
    q_spec = pl.BlockSpec((blk, LANES), lambda b, p, i: (b * nq + i, p))
    kv_out = pl.BlockSpec((seq, LANES), lambda b, p, i: (b, p))
    out = jax.ShapeDtypeStruct((t, D_MODEL), BF16)
    return pl.pallas_call(
        body, name="attn_bwd", grid=(bsz, n_pairs, nq),
        in_specs=[q_spec,
                  pl.BlockSpec((seq, LANES), lambda b, p, i: (b, n_pairs + p)),
                  pl.BlockSpec((seq, LANES), lambda b, p, i: (b, 2 * n_pairs + p)),
                  q_spec, q_spec],
        out_specs=[q_spec, kv_out, kv_out], out_shape=[out, out, out],
        scratch_shapes=[pltpu.VMEM((seq, LANES), F32), pltpu.VMEM((seq, LANES), F32),
                        pltpu.VMEM((2, 2, blk, blk), F32), pltpu.VMEM((2, 2, blk, blk), F32),
                        pltpu.VMEM((2, 2, blk, blk), F32), pltpu.VMEM((2, 2, 1, blk), F32),
                        pltpu.VMEM((2, 1, blk), F32), pltpu.VMEM((2, 1, blk), F32),
                        pltpu.VMEM((2, LANES, blk), F32)],
        compiler_params=_params(("parallel", "parallel", "arbitrary")),
    )(proj, proj, proj, dy, rtot)


def _adamw(w, g, m, v, name, after):
    rows, cols = w.shape
    tr = _pick(rows, (256, 352, 128, 64, 32, 16, 8))
    c1 = 1.0 - ADAM_B1 ** ADAM_STEP
    c2 = 1.0 - ADAM_B2 ** ADAM_STEP

    def body(w_ref, g_ref, m_ref, v_ref, after_ref, d_ref, nm_ref, nv_ref):
        g_v = g_ref[...]
        nm = ADAM_B1 * m_ref[...] + (1.0 - ADAM_B1) * g_v
        nv = ADAM_B2 * v_ref[...] + (1.0 - ADAM_B2) * (g_v * g_v)
        nm_ref[...] = nm
        nv_ref[...] = nv
        d_ref[...] = -ADAM_LR * ((nm / c1) / (jnp.sqrt(nv / c2) + ADAM_EPS) + ADAM_WD * w_ref[...])

    spec = pl.BlockSpec((tr, cols), lambda i: (i, 0))
    shape = jax.ShapeDtypeStruct(w.shape, F32)
    return pl.pallas_call(
        body, name=name, grid=(rows // tr,), in_specs=[spec] * 4 + [pl.BlockSpec(memory_space=pl.ANY)],
        out_specs=[spec] * 3, out_shape=[shape] * 3,
        compiler_params=_params(("parallel",)),
    )(w, g, m, v, after)


def _ffn_fwd(u, w_gu, w_down, bsz, seq, tag):
    h, p = _ffn_up_act(u, w_gu, f"{tag}_up")
    f = _matmul(p, w_down, mode="nn", out_dtype=F32, name=f"{tag}_down")
    return h, p, f


def _ffn_bwd(df, u, h, p, w_gu, w_down, bsz, seq, tag):
    dh = _ffn_down_bwd_act(df, w_down, h, f"{tag}_ddown")
    g_down = _matmul(p, df, mode="tn", out_dtype=F32, name=f"{tag}_gdown")
    g_gu = _matmul(u, dh, mode="tn", out_dtype=F32, name=f"{tag}_ggu", out_stacked=True)
    du = _matmul(dh, w_gu, mode="nt", out_dtype=F32, name=f"{tag}_dup")
    return du, g_gu, g_down


def _local_step(x, c, target, wts, vecs, fetch=None, early_grads=None):
    wts = dict(wts)
    bsz, seq, _ = x.shape
    t = bsz * seq
    x0 = x.reshape(t, D_MODEL)
    tgt = target.reshape(t, D_MODEL)

    sc = _silu_pad(c)
    mod16 = _matmul(sc, wts["w_ada"], mode="nn", out_dtype=F32, name="ada_fwd", bias=vecs["b_ada"])
    mod = mod16[:bsz].reshape(bsz, 9, D_MODEL)

    u1 = _mod_in(x0, mod, bsz, seq, 0)
    h1, p1 = _ffn_up_act(u1, wts["ffn1_w_gu"], "ffn1_up")
    if fetch is not None:
        wts.update(fetch("down", p1))
    f1 = _matmul(p1, wts["ffn1_w_down"], mode="nn", out_dtype=F32, name="ffn1_down")
    r1, x1, u2 = _res_ln_fwd(x0, f1, mod, vecs["ln1_g"], vecs["ln1_b"], bsz, seq, 0, 0.5)
    if fetch is not None:
        wts.update(fetch("later", r1))

    proj = _matmul(u2, wts["w_in"], mode="nn", out_dtype=F32, name="mix_in")
    ya, rtot = _attn_fwd(proj, bsz, seq)
    cs, cv = _conv_fwd(proj, wts["conv_w"], vecs["conv_b"], vecs["conv_ln_g"], vecs["conv_ln_b"], bsz, seq)
    merged, ysb, yconv = _merge_fwd(proj, ya, cs, wts["w_sb_out"], wts["w_conv_out"])
    o2 = _matmul(merged, wts["w_out"], mode="nn", out_dtype=F32, name="mix_out")
    r2, x2, u3 = _res_ln_fwd(x1, o2, mod, vecs["ln2_g"], vecs["ln2_b"], bsz, seq, 1, 1.0)

    h3, p3, f3 = _ffn_fwd(u3, wts["ffn2_w_gu"], wts["ffn2_w_down"], bsz, seq, "ffn2")
    r3, dy, loss_blk = _res_ln_fwd(x2, f3, mod, vecs["ln3_g"], vecs["ln3_b"], bsz, seq, 2, 0.5, target=tgt)

    grads = {}
    dxres, df, ln3s, g3s = _res_ln_bwd(r3, dy, f3, mod, vecs["ln3_g"], bsz, seq, 2, 0.5)
    du, grads["ffn2_w_gu"], grads["ffn2_w_down"] = _ffn_bwd(
        df, u3, h3, p3, wts["ffn2_w_gu"], wts["ffn2_w_down"], bsz, seq, "ffn2")

    dxres, do2, ln2s, g2s, m3s = _res_ln_bwd(r2, dxres, o2, mod, vecs["ln2_g"], bsz, seq, 1, 1.0, above=(du, x2))
    grads["w_out"] = _matmul(merged, do2, mode="tn", out_dtype=F32, name="mix_out_g")
    dysb, dyconv, dgate = _merge_bwd(proj, ysb, yconv, do2, wts["w_out"])
    dya = _matmul(dysb, wts["w_sb_out"], mode="nt", out_dtype=BF16, name="sb_out_d")
    grads["w_sb_out"] = _matmul(ya, dysb, mode="tn", out_dtype=F32, name="sb_out_g")
    dcs = _matmul(dyconv, wts["w_conv_out"], mode="nt", out_dtype=F32, name="conv_out_d")
    grads["w_conv_out"] = _matmul(cs, dyconv, mode="tn", out_dtype=F32, name="conv_out_g")
    dcv, convs = _conv_bwd_ln(dcs, cv, vecs["conv_ln_g"], vecs["conv_ln_b"], bsz, seq)
    dglu, g_conv_w = _conv_bwd_taps(proj, dcv, wts["conv_w"], bsz, seq)
    dq, dk, dv = _attn_bwd(proj, rtot, dya, bsz, seq)
    dproj = jnp.concatenate([dq, dk, dv, dglu, dgate], axis=1)
    grads["w_in"] = _matmul(u2, dproj, mode="tn", out_dtype=F32, name="mix_in_g", out_stacked=True)
    if early_grads is not None:
        mod = mod + early_grads("later_start", {n: grads.pop(n) for n in list(grads)})
    du = _matmul(dproj, wts["w_in"], mode="nt", out_dtype=F32, name="mix_in_d")
    if early_grads is not None:
        mod = mod + early_grads("later_go", du)

    dxres, df, ln1s, g1s, m2s = _res_ln_bwd(r1, dxres, f1, mod, vecs["ln1_g"], bsz, seq, 0, 0.5, above=(du, x1))
    dh = _ffn_down_bwd_act(df, wts["ffn1_w_down"], h1, "ffn1_ddown")
    grads["ffn1_w_down"] = _matmul(p1, df, mode="tn", out_dtype=F32, name="ffn1_gdown")
    grads["ffn1_w_gu"] = _matmul(u1, dh, mode="tn", out_dtype=F32, name="ffn1_ggu", out_stacked=True)
    if early_grads is not None:
        mod = mod + early_grads("mid_start", {n: grads.pop(n) for n in list(grads)})
    du = _matmul(dh, wts["ffn1_w_gu"], mode="nt", out_dtype=F32, name="ffn1_dup")
    if early_grads is not None:
        mod = mod + early_grads("mid_go", du)
    grad_x, m1s = _mod_bwd(dxres, du, x0, mod, bsz, seq, 0)

    dmod = jnp.stack([m1s[:, 0], m1s[:, 1], g1s[:, 0], m2s[:, 0], m2s[:, 1], g2s[:, 0],
                      m3s[:, 0], m3s[:, 1], g3s[:, 0]], axis=1)
    dmod16 = jnp.zeros((16, 9 * D_MODEL), F32).at[:bsz].set(dmod.reshape(bsz, 9 * D_MODEL))
    grads["w_ada"] = _matmul(sc, dmod16.astype(BF16), mode="tn", out_dtype=F32, name="ada_g", out_stacked=True)

    small = {"dmod": dmod, "ln1": ln1s, "ln2": ln2s, "ln3": ln3s, "conv": convs, "conv_w": g_conv_w,
             "loss": loss_blk}
    return grad_x.reshape(x.shape), grads, small


_HBM = pl.BlockSpec(memory_space=pltpu.HBM)


def _position():
    return lax.axis_index("x"), lax.axis_index("y"), lax.axis_index("c")


def _other_chips(x, y):
    return [(1 - x, y), (x, 1 - y), (1 - x, 1 - y)]


def _cast_into_stack(w_local, chip, name):
    rows, cols = w_local.shape
    tr = _pick(rows, (256, 352, 128, 64, 32, 16))

    def body(chip_ref, w_ref, o_ref):
        o_ref[...] = w_ref[...].astype(BF16)

    return pl.pallas_call(
        body, name=name,
        grid_spec=pltpu.PrefetchScalarGridSpec(
            num_scalar_prefetch=1, grid=(rows // tr,),
            in_specs=[pl.BlockSpec((tr, cols), lambda r, chip_ref: (r, 0))],
            out_specs=pl.BlockSpec((None, tr, cols), lambda r, chip_ref: (chip_ref[0], r, 0))),
        out_shape=jax.ShapeDtypeStruct((N_CHIPS, rows, cols), BF16),
        compiler_params=_params(("parallel",)),
    )(chip, w_local)


def _all_gather_weights(stacks, small):
    n = len(stacks)

    def body(*refs):
        ins, small_in, outs, small_out = refs[:n], refs[n], refs[n + 1:2 * n + 1], refs[2 * n + 1]
        send_sems, recv_sems, fwd_send_sems, fwd_recv_sems, small_sems = refs[2 * n + 2:]
        x, y, c = _position()
        me = 2 * x + y
        chips = _other_chips(x, y)

        def send(i, j):
            px, py = chips[j]
            return pltpu.make_async_remote_copy(
                src_ref=ins[i].at[me, c], dst_ref=outs[i].at[me, c], send_sem=send_sems.at[3 * i + j],
                recv_sem=recv_sems.at[3 * i + j], device_id=(px, py, c), device_id_type=MESH)

        def landed(i, j):
            px, py = chips[j]
            return pltpu.make_async_remote_copy(
                src_ref=ins[i].at[me, c], dst_ref=outs[i].at[2 * px + py, c], send_sem=send_sems.at[3 * i + j],
                recv_sem=recv_sems.at[3 * i + j], device_id=(px, py, c), device_id_type=MESH)

        def forward(i, j, half):
            px, py = chips[j]
            blk = outs[i].at[2 * px + py, half]
            return pltpu.make_async_remote_copy(
                src_ref=blk, dst_ref=blk, send_sem=fwd_send_sems.at[3 * i + j],
                recv_sem=fwd_recv_sems.at[3 * i + j], device_id=(x, y, 1 - c), device_id_type=MESH)

        def small_copy(j, slot):
            px, py = chips[j]
            return pltpu.make_async_remote_copy(
                src_ref=small_in, dst_ref=small_out.at[slot], send_sem=small_sems.at[j],
                recv_sem=small_sems.at[3 + j], device_id=(px, py, c), device_id_type=MESH)

        own_small = pltpu.make_async_copy(small_in, small_out.at[me], small_sems.at[6])
        own_small.start()
        for j in range(3):
            small_copy(j, me).start()
        for i in range(n):
            for j in range(3):
                send(i, j).start()
        for i in range(n):
            for j in range(3):
                landed(i, j).wait_recv()
                forward(i, j, c).start()
        for i in range(n):
            for j in range(3):
                forward(i, j, 1 - c).wait_recv()
        for j, (px, py) in enumerate(chips):
            small_copy(j, 2 * px + py).wait_recv()
        own_small.wait()
        for j in range(3):
            small_copy(j, me).wait_send()
        for i in range(n):
            for j in range(3):
                send(i, j).wait_send()
                forward(i, j, c).wait_send()

    return pl.pallas_call(
        body, name="all_gather_weights",
        out_shape=[jax.ShapeDtypeStruct(s.shape, s.dtype) for s in stacks]
        + [jax.ShapeDtypeStruct((N_CHIPS,) + small.shape, small.dtype)],
        in_specs=[_HBM] * (n + 1), out_specs=[_HBM] * (n + 1),
        input_output_aliases={i: i for i in range(n)},
        scratch_shapes=[pltpu.SemaphoreType.DMA((3 * n,)), pltpu.SemaphoreType.DMA((3 * n,)),
                        pltpu.SemaphoreType.DMA((3 * n,)), pltpu.SemaphoreType.DMA((3 * n,)),
                        pltpu.SemaphoreType.DMA((7,))],
    )(*stacks, small)


_SEM = pl.BlockSpec(memory_space=pltpu.SEMAPHORE)
_DATAFLOW = pltpu.SideEffectType.DATAFLOW_SIDE_EFFECTING


_COPIES = {"gather": 3, "scatter": 3, "swap": N_CHIPS}


def _exchange_plan(kind, src, land):
    x, y, c = _position()
    me = 2 * x + y
    if kind == "swap":
        return [(src.at[k, 1 - c], land.at[k], land.at[k], (x, y, 1 - c)) for k in range(N_CHIPS)]
    plan = []
    for j, (px, py) in enumerate(_other_chips(x, y)):
        if kind == "gather":
            plan.append((src.at[me, c], land.at[me, c], land.at[2 * px + py, c], (px, py, c)))
        else:
            plan.append((src.at[2 * px + py], land.at[j], land.at[j], (px, py, c)))
    return plan


def _exchange_start(kind, srcs, lands, name, after):
    n = len(srcs)
    per = _COPIES[kind]
    in_place = lands is None
    n_in = n if in_place else 2 * n

    def body(*refs):
        src_refs = refs[:n]
        land_refs = src_refs if in_place else refs[n:2 * n]
        send_sems, recv_sems = refs[n_in + 1], refs[n_in + 2]
        token = refs[-1]
        for i in range(n):
            for j, (src, dst, _, to) in enumerate(_exchange_plan(kind, src_refs[i], land_refs[i])):
                pltpu.make_async_remote_copy(
                    src_ref=src, dst_ref=dst, send_sem=send_sems.at[per * i + j], recv_sem=recv_sems.at[per * i + j],
                    device_id=to, device_id_type=MESH).start()
        token[...] = jnp.zeros_like(token)

    operands = list(srcs) + ([] if in_place else list(lands))
    operands = [pltpu.with_memory_space_constraint(o, pltpu.HBM) for o in operands]
    out = pl.pallas_call(
        body, name=name,
        out_shape=[pltpu.SemaphoreType.DMA((per * n,)), pltpu.SemaphoreType.DMA((per * n,))]
        + [pltpu.HBM(o.shape, o.dtype) for o in operands] + [jax.ShapeDtypeStruct((8, LANES), F32)],
        in_specs=[_HBM] * n_in + [pl.BlockSpec(memory_space=pl.ANY)],
        out_specs=[_SEM, _SEM] + [_HBM] * n_in + [pl.BlockSpec(memory_space=pltpu.VMEM)],
        input_output_aliases={i: 2 + i for i in range(n_in)},
        compiler_params=pltpu.CompilerParams(has_side_effects=_DATAFLOW),
    )(*operands, after)
    return out[0], out[1], list(out[2:2 + n_in]), out[-1]


def _exchange_wait(kind, send_sems, recv_sems, thru, in_place, after, name):
    n_in = len(thru)
    n = n_in if in_place else n_in // 2
    per = _COPIES[kind]

    def body(*refs):
        src_refs = refs[:n]
        land_refs = src_refs if in_place else refs[n:2 * n]
        send_sems, recv_sems = refs[n_in], refs[n_in + 1]
        for i in range(n):
            for j, (src, _, here, to) in enumerate(_exchange_plan(kind, src_refs[i], land_refs[i])):
                copy = pltpu.make_async_remote_copy(
                    src_ref=src, dst_ref=here, send_sem=send_sems.at[per * i + j], recv_sem=recv_sems.at[per * i + j],
                    device_id=to, device_id_type=MESH)
                copy.wait_send()
                copy.wait_recv()

    out = pl.pallas_call(
        body, name=name, out_shape=[pltpu.HBM(o.shape, o.dtype) for o in thru],
        in_specs=[_HBM] * n_in + [_SEM, _SEM, pl.BlockSpec(memory_space=pl.ANY)], out_specs=[_HBM] * n_in,
        input_output_aliases={i: i for i in range(n_in)},
        compiler_params=pltpu.CompilerParams(has_side_effects=_DATAFLOW),
    )(*thru, send_sems, recv_sems, after)
    return list(out[:n]), (list(out[:n]) if in_place else list(out[n:]))


def _gather_forward(stacks, name):
    n = len(stacks)

    def body(*refs):
        ins, outs = refs[:n], refs[n:2 * n]
        send_sems, recv_sems = refs[2 * n:]
        x, y, c = _position()
        chips = _other_chips(x, y)

        def copy(i, j, half):
            px, py = chips[j]
            return pltpu.make_async_remote_copy(
                src_ref=ins[i].at[2 * px + py, half], dst_ref=outs[i].at[2 * px + py, half],
                send_sem=send_sems.at[3 * i + j], recv_sem=recv_sems.at[3 * i + j],
                device_id=(x, y, 1 - c), device_id_type=MESH)

        for i in range(n):
            for j in range(3):
                copy(i, j, c).start()
        for i in range(n):
            for j in range(3):
                copy(i, j, 1 - c).wait_recv()
        for i in range(n):
            for j in range(3):
                copy(i, j, c).wait_send()

    return pl.pallas_call(
        body, name=name, out_shape=[jax.ShapeDtypeStruct(s.shape, s.dtype) for s in stacks],
        in_specs=[_HBM] * n, out_specs=[_HBM] * n, input_output_aliases={i: i for i in range(n)},
        scratch_shapes=[pltpu.SemaphoreType.DMA((3 * n,)), pltpu.SemaphoreType.DMA((3 * n,))],
    )(*stacks)


def _pair_add(g, got, place, name):
    _, _, rh, cols = g.shape
    tr = _pick(rh, (256, 176, 128, 64, 32, 16, 8))

    def body(place_ref, g_ref, got_ref, p_ref, own_ref):
        s = g_ref[...] + got_ref[...]
        p_ref[...] = s.astype(BF16)

        @pl.when(pl.program_id(1) == place_ref[1])
        def _():
            own_ref[...] = s

    blk = pl.BlockSpec((None, tr, cols), lambda r, k, place_ref: (k, r, 0))
    return pl.pallas_call(
        body, name=name,
        grid_spec=pltpu.PrefetchScalarGridSpec(
            num_scalar_prefetch=1, grid=(rh // tr, N_CHIPS),
            in_specs=[pl.BlockSpec((None, None, tr, cols), lambda r, k, place_ref: (k, place_ref[0], r, 0)), blk],
            out_specs=[blk, pl.BlockSpec((tr, cols), lambda r, k, place_ref: (r, 0))]),
        out_shape=[jax.ShapeDtypeStruct((N_CHIPS, rh, cols), BF16), jax.ShapeDtypeStruct((rh, cols), F32)],
        compiler_params=_params(("parallel", "arbitrary")),
    )(place, g, got)


def _chip_sum(own, parts, place, name):
    rh, cols = own.shape
    tr = _pick(rh, (256, 176, 128, 64, 32, 16, 8))

    def body(place_ref, own_ref, p_ref, o_ref):
        o_ref[...] = ((own_ref[...] + p_ref[0].astype(F32)) + p_ref[1].astype(F32)) + p_ref[2].astype(F32)

    return pl.pallas_call(
        body, name=name,
        grid_spec=pltpu.PrefetchScalarGridSpec(
            num_scalar_prefetch=1, grid=(rh // tr,),
            in_specs=[pl.BlockSpec((tr, cols), lambda r, place_ref: (r, 0)),
                      pl.BlockSpec((3, tr, cols), lambda r, place_ref: (0, r, 0))],
            out_specs=pl.BlockSpec((None, tr, cols), lambda r, place_ref: (place_ref[0], r, 0))),
        out_shape=jax.ShapeDtypeStruct((2, rh, cols), F32),
        compiler_params=_params(("parallel",)),
    )(place, own, parts)


def _pair_gather(halves, name):
    n = len(halves)

    def body(*refs):
        ins, outs = refs[:n], refs[n:2 * n]
        send_sems, recv_sems = refs[2 * n:]
        x, y, c = _position()

        def send(i):
            return pltpu.make_async_remote_copy(
                src_ref=ins[i].at[c], dst_ref=outs[i].at[c], send_sem=send_sems.at[i], recv_sem=recv_sems.at[i],
                device_id=(x, y, 1 - c), device_id_type=MESH)

        def landed(i):
            return pltpu.make_async_remote_copy(
                src_ref=ins[i].at[c], dst_ref=outs[i].at[1 - c], send_sem=send_sems.at[i], recv_sem=recv_sems.at[i],
                device_id=(x, y, 1 - c), device_id_type=MESH)

        for i in range(n):
            send(i).start()
        for i in range(n):
            landed(i).wait_recv()
        for i in range(n):
            send(i).wait_send()

    return pl.pallas_call(
        body, name=name,
        out_shape=[jax.ShapeDtypeStruct(h.shape, F32) for h in halves],
        in_specs=[_HBM] * n, out_specs=[_HBM] * n,
        input_output_aliases={i: i for i in range(n)},
        scratch_shapes=[pltpu.SemaphoreType.DMA((n,)), pltpu.SemaphoreType.DMA((n,))],
    )(*halves)


_MOD_ROWS = 16


def _small_all_reduce(buf, bsz, after):
    rows, cols = buf.shape
    head = bsz * _MOD_ROWS
    out_rows = rows - head + _MOD_ROWS

    def body(in_ref, after_ref, o_ref, gath, send_sems, recv_sems):
        x, y, c = _position()
        me = 4 * x + 2 * y + c

        def peer(mask):
            return (x ^ (mask >> 2), y ^ ((mask >> 1) & 1), c ^ (mask & 1))

        def copy(mask):
            return pltpu.make_async_remote_copy(
                src_ref=in_ref, dst_ref=gath.at[me], send_sem=send_sems.at[mask - 1],
                recv_sem=recv_sems.at[mask - 1], device_id=peer(mask), device_id_type=MESH)

        def arrival(mask):
            px, py, pc = peer(mask)
            return pltpu.make_async_remote_copy(
                src_ref=in_ref, dst_ref=gath.at[4 * px + 2 * py + pc], send_sem=send_sems.at[mask - 1],
                recv_sem=recv_sems.at[mask - 1], device_id=peer(mask), device_id_type=MESH)

        for mask in range(1, N_DEV):
            copy(mask).start()
        gath[me] = in_ref[...]
        for mask in range(1, N_DEV):
            arrival(mask).wait_recv()
        for mask in range(1, N_DEV):
            copy(mask).wait_send()
        acc = gath[0]
        for d in range(1, N_DEV):
            acc = acc + gath[d]
        mod = acc[0:_MOD_ROWS]
        for s in range(1, bsz):
            mod = mod + acc[s * _MOD_ROWS:(s + 1) * _MOD_ROWS]
        o_ref[0:_MOD_ROWS, :] = mod
        o_ref[_MOD_ROWS:, :] = acc[head:]

    vm = pl.BlockSpec(memory_space=pltpu.VMEM)
    return pl.pallas_call(
        body, name="small_all_reduce", in_specs=[vm, pl.BlockSpec(memory_space=pl.ANY)], out_specs=vm,
        out_shape=jax.ShapeDtypeStruct((out_rows, cols), F32),
        scratch_shapes=[pltpu.VMEM((N_DEV, rows, cols), F32), pltpu.SemaphoreType.DMA((N_DEV - 1,)),
                        pltpu.SemaphoreType.DMA((N_DEV - 1,))],
        compiler_params=pltpu.CompilerParams(vmem_limit_bytes=VMEM_LIMIT),
    )(buf, after)


_COL_SHARDED = ("w_ada", "ffn1_w_gu", "w_in", "ffn2_w_gu")
_ROW_SHARDED = ("ffn1_w_down", "w_sb_out", "w_conv_out", "w_out", "ffn2_w_down")
_NOW = ["w_ada", "ffn1_w_gu"]
_SOON = ["ffn1_w_down"]
_LATER = ["w_in", "w_sb_out", "w_conv_out", "w_out", "ffn2_w_gu", "ffn2_w_down"]
_VECS = ("b_ada", "ln1_g", "ln1_b", "conv_b", "conv_ln_g", "conv_ln_b", "ln2_g", "ln2_b", "ln3_g", "ln3_b")
_WEIGHTS = ("w_ada", "b_ada", "ffn1_w_gu", "ffn1_w_down", "ln1_g", "ln1_b", "w_in", "w_sb_out", "conv_w", "conv_b",
            "conv_ln_g", "conv_ln_b", "w_conv_out", "w_out", "ln2_g", "ln2_b", "ffn2_w_gu", "ffn2_w_down",
            "ln3_g", "ln3_b")


def _step(x, c, target, w, m, v):
    bsz = x.shape[0]
    chip = 2 * lax.axis_index("x") + lax.axis_index("y")
    core = lax.axis_index("c")

    chip_arr = jnp.reshape(chip, (1,)).astype(jnp.int32)
    place = jnp.stack([core, chip]).astype(jnp.int32)

    def stack_of(n):
        rows, cols = w[n].shape[1:]
        return _cast_into_stack(w[n][0], chip_arr, f"cast_{n}").reshape(N_CHIPS, 2, rows // 2, cols)

    def gathered_form(n, g):
        rows, cols = w[n].shape[1:]
        return g.reshape(N_CHIPS, rows, cols) if n in _COL_SHARDED else g.reshape(N_CHIPS * rows, cols)

    conv_w_local = jnp.pad(w["conv_w"][0], ((0, 1), (0, 0)))
    gathered = _all_gather_weights([stack_of(n) for n in _NOW], conv_w_local)
    wts = {n: gathered_form(n, g) for n, g in zip(_NOW, gathered[:-1])}
    wts["conv_w"] = gathered[-1].transpose(1, 0, 2).reshape(32, D_MODEL)
    pending, behind = {}, gathered[0]
    for stage, names in (("down", _SOON), ("later", _LATER)):
        send, recv, thru, token = _exchange_start(
            "gather", [stack_of(n) for n in names], None, f"gather_start_{stage}", behind)
        pending[stage] = (names, send, recv, thru)
        behind = token
    vecs = {n: w[n] for n in _VECS}
    vecs["b_ada"] = vecs["b_ada"] + behind[0, 0]

    def fetch(stage, after):
        names, send, recv, thru = pending[stage]
        landed, _ = _exchange_wait("gather", send, recv, thru, True, after, f"gather_wait_{stage}")
        forwarded = _gather_forward(landed, f"gather_forward_{stage}")
        return {n: gathered_form(n, g) for n, g in zip(names, forwarded)}

    groups = {"later": _LATER, "mid": ["ffn1_w_gu", "ffn1_w_down"], "last": ["w_ada"]}
    g_out, updates, state = {}, {}, {}

    def adam(names, after):
        for n in names:
            shape = w[n].shape
            flat = shape[1:] if len(shape) == 3 else shape
            d, nm, nv = _adamw(w[n].reshape(flat), g_out[n].reshape(flat), m[n].reshape(flat), v[n].reshape(flat),
                               f"adamw_{n}", after)
            updates[n] = (g_out[n].reshape(shape), d.reshape(shape), nm.reshape(shape), nv.reshape(shape))
            after = nv
        return after

    def swap_start(tag, grads):
        views = [grads[n].reshape(N_CHIPS, 2, w[n].shape[1] // 2, w[n].shape[2]) for n in groups[tag]]
        lands = [lax.empty((N_CHIPS,) + g.shape[2:], F32) for g in views]
        send, recv, thru, token = _exchange_start("swap", views, lands, f"swap_start_{tag}", place)
        state[tag] = {"swap": (send, recv, thru)}
        return token

    def scatter_start(tag, after):
        views, got = _exchange_wait("swap", *state[tag]["swap"], False, after, f"swap_wait_{tag}")
        sums = [_pair_add(g, r, place, f"pair_add_{n}") for n, g, r in zip(groups[tag], views, got)]
        lands = [lax.empty((3,) + p.shape[1:], BF16) for p, _ in sums]
        send, recv, thru, token = _exchange_start(
            "scatter", [p for p, _ in sums], lands, f"scatter_start_{tag}", place)
        state[tag].update(scatter=(send, recv, thru), sums=sums)
        return token

    def collect(tag, after):
        _, parts = _exchange_wait("scatter", *state[tag]["scatter"], False, after, f"scatter_wait_{tag}")
        halves = [_chip_sum(own, p, place, f"chip_sum_{n}")
                  for n, (_, own), p in zip(groups[tag], state[tag]["sums"], parts)]
        for n, f in zip(groups[tag], _pair_gather(halves, f"grad_pair_gather_{tag}")):
            g_out[n] = f.reshape(w[n].shape[1:])
        return g_out[groups[tag][-1]]

    def early_grads(stage, value):
        tag, step = stage.split("_")
        token = swap_start(tag, value) if step == "start" else scatter_start(tag, value)
        return token[0, 0]

    grad_x, grads, small = _local_step(x, c, target, wts, vecs, fetch, early_grads)

    token = swap_start("last", grads)
    done = collect("later", token)
    token = scatter_start("last", done)
    done = adam(groups["later"], token)
    done = collect("mid", done)
    done = adam(groups["mid"], done)
    done = collect("last", done)

    dmod = jnp.pad(small["dmod"], ((0, 0), (0, _MOD_ROWS - 9), (0, 0))).reshape(bsz * _MOD_ROWS, D_MODEL)
    loss_rows = jnp.pad(small["loss"], ((0, 0), (0, D_MODEL - LANES)))
    buf = jnp.concatenate([dmod, small["ln1"], small["ln2"], small["ln3"], small["conv"], small["conv_w"],
                           loss_rows], axis=0)
    red = _small_all_reduce(buf, bsz, done)
    o = _MOD_ROWS
    g_out["b_ada"] = red[0:9].reshape(1, 9 * D_MODEL)
    g_out["ln1_g"], g_out["ln1_b"] = red[o:o + 1], red[o + 1:o + 2]
    g_out["ln2_g"], g_out["ln2_b"] = red[o + 8:o + 9], red[o + 9:o + 10]
    g_out["ln3_g"], g_out["ln3_b"] = red[o + 16:o + 17], red[o + 17:o + 18]
    g_out["conv_ln_g"], g_out["conv_ln_b"], g_out["conv_b"] = red[o + 24:o + 25], red[o + 25:o + 26], red[o + 26:o + 27]
    cw = w["conv_w"].shape[2]
    g_out["conv_w"] = lax.dynamic_slice(red[o + 32:o + 32 + CONV_TAPS], (0, chip * cw), (CONV_TAPS, cw))
    loss = red[o + 64, 0]

    adam(groups["last"] + list(_VECS) + ["conv_w"], place)
    return (loss, grad_x, *[updates[n][k] for k in range(4) for n in _WEIGHTS])


def kernel(x, c, w_ada, b_ada, ffn1_w_gu, ffn1_w_down, ln1_g, ln1_b, w_in, w_sb_out, conv_w, conv_b, conv_ln_g, conv_ln_b, w_conv_out, w_out, ln2_g, ln2_b, ffn2_w_gu, ffn2_w_down, ln3_g, ln3_b, loss_target, m_w_ada, m_b_ada, m_ffn1_w_gu, m_ffn1_w_down, m_ln1_g, m_ln1_b, m_w_in, m_w_sb_out, m_conv_w, m_conv_b, m_conv_ln_g, m_conv_ln_b, m_w_conv_out, m_w_out, m_ln2_g, m_ln2_b, m_ffn2_w_gu, m_ffn2_w_down, m_ln3_g, m_ln3_b, v_w_ada, v_b_ada, v_ffn1_w_gu, v_ffn1_w_down, v_ln1_g, v_ln1_b, v_w_in, v_w_sb_out, v_conv_w, v_conv_b, v_conv_ln_g, v_conv_ln_b, v_w_conv_out, v_w_out, v_ln2_g, v_ln2_b, v_ffn2_w_gu, v_ffn2_w_down, v_ln3_g, v_ln3_b):
    given = dict(locals())
    w = {n: given[n] for n in _WEIGHTS}
    m = {n: given["m_" + n] for n in _WEIGHTS}
    v = {n: given["v_" + n] for n in _WEIGHTS}
    return _step(x, c, loss_target, w, m, v)
```

```python
import functools

import jax
import jax.numpy as jnp
from jax import lax
from jax.experimental import pallas as pl
from jax.experimental.pallas import tpu as pltpu

F32 = jnp.float32
BF16 = jnp.bfloat16

D_MODEL = 1024
D_FF = 2816
HEADS = 16
HEAD_DIM = 64
LANES = 128
CONV_TAPS = 31
HALO = 32
N_CHIPS = 4
N_DEV = 8
ALPHA = 2.0 ** 0.25
LN_EPS = 1e-5
ATT_BLOCK = 256
VMEM_LIMIT = 56 * 1024 * 1024

ADAM_LR = 0.001
ADAM_B1 = 0.9
ADAM_B2 = 0.999
ADAM_EPS = 1e-08
ADAM_WD = 0.01
ADAM_STEP = 10

MESH = pl.DeviceIdType.MESH


def _pick(n, cands):
    for t in cands:
        if t <= n and n % t == 0:
            return t
    return n


def _params(sem):
    return pltpu.CompilerParams(dimension_semantics=sem, vmem_limit_bytes=VMEM_LIMIT)


def _sigmoid(z):
    t = jnp.exp(-jnp.abs(z))
    return jnp.where(z >= 0, 1.0, t) / (1.0 + t)


def _silu(z):
    return z * _sigmoid(z)


def _dsilu(z):
    s = _sigmoid(z)
    return s * (1.0 + z * (1.0 - s))


def _ln_stats(r):
    mu = jnp.mean(r, axis=-1, keepdims=True)
    d = r - mu
    var = jnp.mean(d * d, axis=-1, keepdims=True)
    rstd = lax.rsqrt(var + LN_EPS)
    return d * rstd, rstd


def _colsum(v):
    return jnp.sum(v, axis=0, keepdims=True)


_DIMS = {"nn": (((1,), (0,)), ((), ())), "nt": (((1,), (1,)), ((), ())), "tn": (((0,), (0,)), ((), ()))}
_TN_CANDS = (1408, 1792, 1152, 1024, 512, 256, 128)
_TK_CANDS = (2816, 1024, 1408, 1792, 512, 256, 128)


def _matmul(a, b, *, mode, out_dtype, name, bias=None, out_stacked=False):
    a_halves = mode == "nt" and a.ndim == 3
    b_halves = mode == "tn" and b.ndim == 3
    b_stacked = b.ndim == 3 and not b_halves
    if mode == "nn":
        m, k = a.shape
        n_c = b.shape[-1]
        n = n_c * (N_CHIPS if b_stacked else 1)
        k_c = k
    elif mode == "nt":
        m = a.shape[-2]
        k = a.shape[-1] * (2 if a_halves else 1)
        n = b.shape[-2]
        k_c = b.shape[-1]
        n_c = n
    else:
        k, m = a.shape
        n = b.shape[-1] * (2 if b_halves else 1)
        n_c = n // N_CHIPS if out_stacked else n
        k_c = k
    if mode == "tn":
        tm = _pick(m, (1024, 1408, 512, 256, 128))
        tk = _pick(k, (2048, 1024, 512, 256, 128, 64, 32, 16))
    else:
        tm = _pick(m, (1024, 512, 256, 128, 64, 32, 16))
        tk = _pick(k_c, _TK_CANDS)
    tn = _pick(n_c, _TN_CANDS)
    nb = n_c // tn
    kb = k_c // tk
    nk = k // tk
    grid = (m // tm, n // tn, nk)

    if mode == "nn":
        a_spec = pl.BlockSpec((tm, tk), lambda i, j, kk: (i, kk))
        if b_stacked:
            b_spec = pl.BlockSpec((None, tk, tn), lambda i, j, kk: (j // nb, kk, j % nb))
        else:
            b_spec = pl.BlockSpec((tk, tn), lambda i, j, kk: (kk, j))
    elif mode == "nt":
        if a_halves:
            ka = a.shape[-1] // tk
            a_spec = pl.BlockSpec((None, tm, tk), lambda i, j, kk: (kk // ka, i, kk % ka))
        else:
            a_spec = pl.BlockSpec((tm, tk), lambda i, j, kk: (i, kk))
        if b_stacked:
            b_spec = pl.BlockSpec((None, tn, tk), lambda i, j, kk: (kk // kb, j, kk % kb))
        else:
            b_spec = pl.BlockSpec((tn, tk), lambda i, j, kk: (j, kk))
    else:
        a_spec = pl.BlockSpec((tk, tm), lambda i, j, kk: (kk, i))
        if b_halves:
            nh = b.shape[-1] // tn
            b_spec = pl.BlockSpec((None, tk, tn), lambda i, j, kk: (j // nh, kk, j % nh))
        else:
            b_spec = pl.BlockSpec((tk, tn), lambda i, j, kk: (kk, j))
    if out_stacked:
        out_shape = jax.ShapeDtypeStruct((N_CHIPS, m, n_c), out_dtype)
        o_spec = pl.BlockSpec((None, tm, tn), lambda i, j, kk: (j // nb, i, j % nb))
    else:
        out_shape = jax.ShapeDtypeStruct((m, n), out_dtype)
        o_spec = pl.BlockSpec((tm, tn), lambda i, j, kk: (i, j))
    in_specs = [a_spec, b_spec]
    args = [a, b]
    if bias is not None:
        in_specs.append(pl.BlockSpec((1, tn), lambda i, j, kk: (0, j)))
        args.append(bias)
    dims = _DIMS[mode]

    def body(*refs):
        a_ref, b_ref = refs[0], refs[1]
        bias_ref = refs[2] if bias is not None else None

        def write(r):
            if bias_ref is not None:
                r = r + bias_ref[...]
            o_ref[...] = r.astype(o_ref.dtype)

        if nk == 1:
            o_ref = refs[-1]
            write(lax.dot_general(a_ref[...], b_ref[...], dims, preferred_element_type=F32))
            return
        o_ref, acc_ref = refs[-2], refs[-1]
        kk = pl.program_id(2)

        @pl.when(kk == 0)
        def _():
            acc_ref[...] = jnp.zeros_like(acc_ref)

        acc_ref[...] += lax.dot_general(a_ref[...], b_ref[...], dims, preferred_element_type=F32)

        @pl.when(kk == nk - 1)
        def _():
            write(acc_ref[...])

    return pl.pallas_call(
        body, name=name, grid=grid, in_specs=in_specs, out_specs=o_spec, out_shape=out_shape,
        scratch_shapes=[pltpu.VMEM((tm, tn), F32)] if nk > 1 else [],
        compiler_params=_params(("parallel", "parallel", "arbitrary")),
    )(*args)


def _row_grid(bsz, seq, ts):
    ns = seq // ts
    return (bsz, ns), ns


def _rows(ts, width, ns, col=0):
    return pl.BlockSpec((ts, width), lambda b, s: (b * ns + s, col))


def _mod_spec():
    return pl.BlockSpec((None, 9, D_MODEL), lambda b, s: (b, 0, 0))


def _vec_spec(rows=1, width=D_MODEL):
    return pl.BlockSpec((rows, width), lambda b, s: (0, 0))


def _silu_pad(c):
    bsz = c.shape[0]

    def body(c_ref, o_ref):
        o_ref[...] = jnp.zeros_like(o_ref)
        o_ref[0:bsz, :] = _silu(c_ref[...]).astype(BF16)

    return pl.pallas_call(body, name="silu_pad", out_shape=jax.ShapeDtypeStruct((16, D_MODEL), BF16))(c)


def _mod_in(x, mod, bsz, seq, sub):
    ts = _pick(seq, (512, 256, 128))
    grid, ns = _row_grid(bsz, seq, ts)

    def body(x_ref, mod_ref, u_ref):
        sh = mod_ref[3 * sub:3 * sub + 1, :]
        sc = mod_ref[3 * sub + 1:3 * sub + 2, :]
        u_ref[...] = (x_ref[...] * (1.0 + sc) + sh).astype(BF16)

    return pl.pallas_call(
        body, name=f"mod_in{sub}", grid=grid, in_specs=[_rows(ts, D_MODEL, ns), _mod_spec()],
        out_specs=_rows(ts, D_MODEL, ns), out_shape=jax.ShapeDtypeStruct(x.shape, BF16),
        compiler_params=_params(("parallel", "parallel")),
    )(x, mod)


_FFN_TN = D_FF // 2


def _ffn_up_act(u, w_gu, name):
    t = u.shape[0]
    tm = _pick(t, (512, 256, 128))
    tn = _FFN_TN

    def body(u_ref, wa_ref, wg_ref, h_ref, p_ref):
        u_v = u_ref[...]
        a = jnp.dot(u_v, wa_ref[...], preferred_element_type=F32)
        g = jnp.dot(u_v, wg_ref[...], preferred_element_type=F32)
        h_ref[0] = a.astype(BF16)
        h_ref[1] = g.astype(BF16)
        p_ref[...] = (_silu(a) * g).astype(BF16)

    return pl.pallas_call(
        body, name=name, grid=(2, t // tm),
        in_specs=[pl.BlockSpec((tm, D_MODEL), lambda j, i: (i, 0)),
                  pl.BlockSpec((None, D_MODEL, tn), lambda j, i: (j, 0, 0)),
                  pl.BlockSpec((None, D_MODEL, tn), lambda j, i: (j + 2, 0, 0))],
        out_specs=[pl.BlockSpec((2, tm, tn), lambda j, i: (0, i, j)),
                   pl.BlockSpec((tm, tn), lambda j, i: (i, j))],
        out_shape=[jax.ShapeDtypeStruct((2, t, D_FF), BF16), jax.ShapeDtypeStruct((t, D_FF), BF16)],
        compiler_params=_params(("parallel", "parallel")),
    )(u, w_gu, w_gu)


def _ffn_down_bwd_act(df, w_down, h, name):
    t = df.shape[0]
    tm = _pick(t, (512, 256, 128))
    tn = _FFN_TN

    def body(df_ref, wd_ref, h_ref, dh_ref):
        dp = lax.dot_general(df_ref[...], wd_ref[...], _DIMS["nt"], preferred_element_type=F32)
        a = h_ref[0].astype(F32)
        g = h_ref[1].astype(F32)
        dh_ref[0] = (dp * g * _dsilu(a)).astype(BF16)
        dh_ref[1] = (dp * _silu(a)).astype(BF16)

    blk = pl.BlockSpec((2, tm, tn), lambda j, i: (0, i, j))
    return pl.pallas_call(
        body, name=name, grid=(2, t // tm),
        in_specs=[pl.BlockSpec((tm, D_MODEL), lambda j, i: (i, 0)),
                  pl.BlockSpec((tn, D_MODEL), lambda j, i: (j, 0)), blk],
        out_specs=blk, out_shape=jax.ShapeDtypeStruct((2, t, D_FF), BF16),
        compiler_params=_params(("parallel", "parallel")),
    )(df, w_down, h)


def _res_ln_fwd(x, f, mod, ln_g, ln_b, bsz, seq, sub, weight, target=None):
    ts = _pick(seq, (512, 256, 128))
    grid, ns = _row_grid(bsz, seq, ts)
    last = target is not None

    def body(*refs):
        x_ref, f_ref, mod_ref, g_ref, b_ref = refs[:5]
        gate = mod_ref[3 * sub + 2:3 * sub + 3, :]
        r = ALPHA * x_ref[...] + gate * (weight * f_ref[...])
        xhat, _ = _ln_stats(r)
        xo = xhat * g_ref[...] + b_ref[...]
        if last:
            t_ref, r_ref, dy_ref, loss_ref = refs[5:]
            diff = xo - t_ref[...]
            dy_ref[...] = diff * (1.0 / D_MODEL)
            part = 0.5 * jnp.sum(jnp.mean(diff * diff, axis=-1, keepdims=True), axis=0, keepdims=True)

            @pl.when((pl.program_id(0) == 0) & (pl.program_id(1) == 0))
            def _():
                loss_ref[...] = jnp.zeros_like(loss_ref)

            loss_ref[...] += jnp.broadcast_to(part, loss_ref.shape)
        else:
            r_ref, xo_ref, u_ref = refs[5:]
            xo_ref[...] = xo
            sh = mod_ref[3 * sub + 3:3 * sub + 4, :]
            sc = mod_ref[3 * sub + 4:3 * sub + 5, :]
            u_ref[...] = (xo * (1.0 + sc) + sh).astype(BF16)
        r_ref[...] = r

    row = _rows(ts, D_MODEL, ns)
    in_specs = [row, row, _mod_spec(), _vec_spec(), _vec_spec()]
    args = [x, f, mod, ln_g, ln_b]
    if last:
        in_specs.append(row)
        args.append(target)
        out_specs = [row, row, _vec_spec(8, LANES)]
        out_shape = [jax.ShapeDtypeStruct(x.shape, F32), jax.ShapeDtypeStruct(x.shape, F32),
                     jax.ShapeDtypeStruct((8, LANES), F32)]
        sem = ("arbitrary", "arbitrary")
    else:
        out_specs = [row, row, row]
        out_shape = [jax.ShapeDtypeStruct(x.shape, F32), jax.ShapeDtypeStruct(x.shape, F32),
                     jax.ShapeDtypeStruct(x.shape, BF16)]
        sem = ("parallel", "parallel")
    return pl.pallas_call(
        body, name=f"res_ln_fwd{sub}", grid=grid, in_specs=in_specs, out_specs=out_specs, out_shape=out_shape,
        compiler_params=_params(sem),
    )(*args)


_ROW_CHUNK = 32


def _res_ln_bwd(r, dxo, f, mod, ln_g, bsz, seq, sub, weight, above=None):
    ts = _pick(seq, (512, 256, 128))
    grid, ns = _row_grid(bsz, seq, ts)
    folded = above is not None

    def body(*refs):
        r_ref, dxo_ref, f_ref, mod_ref, g_ref = refs[:5]
        dxres_ref, df_ref, lns_ref, gs_ref = refs[-5:-1] if folded else refs[-4:]
        b, s = pl.program_id(0), pl.program_id(1)
        gate = mod_ref[3 * sub + 2:3 * sub + 3, :]
        ms_ref = refs[-1] if folded else None

        @pl.when((b == 0) & (s == 0))
        def _():
            lns_ref[...] = jnp.zeros_like(lns_ref)

        @pl.when(s == 0)
        def _():
            gs_ref[...] = jnp.zeros_like(gs_ref)
            if folded:
                ms_ref[...] = jnp.zeros_like(ms_ref)

        def chunk(i, carry):
            rows = pl.ds(pl.multiple_of(i * _ROW_CHUNK, _ROW_CHUNK), _ROW_CHUNK)
            xhat, rstd = _ln_stats(r_ref[rows, :])
            dxo_v = dxo_ref[rows, :]
            if folded:
                du_v = refs[5][rows, :]
                dxo_v = dxo_v + du_v * (1.0 + mod_ref[3 * sub + 4:3 * sub + 5, :])
                ms_ref[0:1, :] += _colsum(du_v)
                ms_ref[1:2, :] += _colsum(du_v * refs[6][rows, :])
            dxhat = dxo_v * g_ref[...]
            m1 = jnp.mean(dxhat, axis=-1, keepdims=True)
            m2 = jnp.mean(dxhat * xhat, axis=-1, keepdims=True)
            dr = rstd * (dxhat - m1 - xhat * m2)
            dxres_ref[rows, :] = ALPHA * dr
            df_ref[rows, :] = (dr * (gate * weight)).astype(BF16)
            lns_ref[0:1, :] += _colsum(dxo_v * xhat)
            lns_ref[1:2, :] += _colsum(dxo_v)
            gs_ref[0:1, :] += _colsum(dr * (weight * f_ref[rows, :]))
            return carry

        lax.fori_loop(0, ts // _ROW_CHUNK, chunk, 0, unroll=4)

    row = _rows(ts, D_MODEL, ns)
    per_sample = pl.BlockSpec((None, 8, D_MODEL), lambda b, s: (b, 0, 0))
    stats = jax.ShapeDtypeStruct((bsz, 8, D_MODEL), F32)
    extra = 1 if folded else 0
    return pl.pallas_call(
        body, name=f"res_ln_bwd{sub}", grid=grid,
        in_specs=[row, row, row, _mod_spec(), _vec_spec()] + [row, row] * extra,
        out_specs=[row, row, _vec_spec(8), per_sample] + [per_sample] * extra,
        out_shape=[jax.ShapeDtypeStruct(r.shape, F32), jax.ShapeDtypeStruct(r.shape, BF16),
                   jax.ShapeDtypeStruct((8, D_MODEL), F32), stats] + [stats] * extra,
        compiler_params=_params(("arbitrary", "arbitrary")),
    )(r, dxo, f, mod, ln_g, *(above or ()))


def _mod_bwd(dxres, du, x, mod, bsz, seq, sub):
    ts = _pick(seq, (512, 256, 128))
    grid, ns = _row_grid(bsz, seq, ts)

    def body(dxres_ref, du_ref, x_ref, mod_ref, dx_ref, st_ref):
        s = pl.program_id(1)
        sc = mod_ref[3 * sub + 1:3 * sub + 2, :]
        du_v = du_ref[...]
        dx_ref[...] = dxres_ref[...] + du_v * (1.0 + sc)

        @pl.when(s == 0)
        def _():
            st_ref[...] = jnp.zeros_like(st_ref)

        st_ref[0:1, :] += _colsum(du_v)
        st_ref[1:2, :] += _colsum(du_v * x_ref[...])

    row = _rows(ts, D_MODEL, ns)
    return pl.pallas_call(
        body, name=f"mod_bwd{sub}", grid=grid, in_specs=[row, row, row, _mod_spec()],
        out_specs=[row, pl.BlockSpec((None, 8, D_MODEL), lambda b, s: (b, 0, 0))],
        out_shape=[jax.ShapeDtypeStruct(x.shape, F32), jax.ShapeDtypeStruct((bsz, 8, D_MODEL), F32)],
        compiler_params=_params(("parallel", "arbitrary")),
    )(dxres, du, x, mod)


_COL_GLU_A, _COL_GLU_B, _COL_GATE_A, _COL_GATE_B = 3, 4, 5, 6


def _merge_fwd(proj, ya, cs, w_sb, w_co):
    t = ya.shape[0]
    tm = _pick(t, (512, 256, 128))

    def body(ga_ref, gb_ref, ya_ref, cs_ref, wsb_ref, wco_ref, m_ref, ysb_ref, yc_ref):
        ysb = jnp.dot(ya_ref[...], wsb_ref[...], preferred_element_type=F32)
        yc = jnp.dot(cs_ref[...], wco_ref[...], preferred_element_type=F32)
        ysb_ref[...] = ysb
        yc_ref[...] = yc
        m_ref[...] = (_sigmoid(ga_ref[...]) * ysb + _sigmoid(gb_ref[...]) * yc).astype(BF16)

    row = pl.BlockSpec((tm, D_MODEL), lambda i: (i, 0))
    full = pl.BlockSpec((D_MODEL, D_MODEL), lambda i: (0, 0))
    return pl.pallas_call(
        body, name="merge_fwd", grid=(t // tm,),
        in_specs=[pl.BlockSpec((tm, D_MODEL), lambda i: (i, _COL_GATE_A)),
                  pl.BlockSpec((tm, D_MODEL), lambda i: (i, _COL_GATE_B)), row, row, full, full],
        out_specs=[row, row, row],
        out_shape=[jax.ShapeDtypeStruct((t, D_MODEL), BF16), jax.ShapeDtypeStruct((t, D_MODEL), F32),
                   jax.ShapeDtypeStruct((t, D_MODEL), F32)],
        compiler_params=_params(("parallel",)),
    )(proj, proj, ya, cs, w_sb, w_co)


def _merge_bwd(proj, ysb, yconv, do2, w_out):
    t = ysb.shape[0]
    tm = _pick(t, (512, 256, 128))

    def body(ga_ref, gb_ref, ysb_ref, yc_ref, do_ref, w_ref, dysb_ref, dyc_ref, dg_ref):
        dm = lax.dot_general(do_ref[...], w_ref[...], _DIMS["nt"], preferred_element_type=F32)
        sa = _sigmoid(ga_ref[...])
        sb = _sigmoid(gb_ref[...])
        dysb_ref[...] = (dm * sa).astype(BF16)
        dyc_ref[...] = (dm * sb).astype(BF16)
        dg_ref[:, :D_MODEL] = (dm * ysb_ref[...] * sa * (1.0 - sa)).astype(BF16)
        dg_ref[:, D_MODEL:] = (dm * yc_ref[...] * sb * (1.0 - sb)).astype(BF16)

    row = pl.BlockSpec((tm, D_MODEL), lambda i: (i, 0))
    return pl.pallas_call(
        body, name="merge_bwd", grid=(t // tm,),
        in_specs=[pl.BlockSpec((tm, D_MODEL), lambda i: (i, _COL_GATE_A)),
                  pl.BlockSpec((tm, D_MODEL), lambda i: (i, _COL_GATE_B)), row, row, row,
                  pl.BlockSpec((D_MODEL, D_MODEL), lambda i: (0, 0))],
        out_specs=[row, row, pl.BlockSpec((tm, 2 * D_MODEL), lambda i: (i, 0))],
        out_shape=[jax.ShapeDtypeStruct((t, D_MODEL), BF16), jax.ShapeDtypeStruct((t, D_MODEL), BF16),
                   jax.ShapeDtypeStruct((t, 2 * D_MODEL), BF16)],
        compiler_params=_params(("parallel",)),
    )(proj, proj, ysb, yconv, do2, w_out)


_CONV_ROWS = 128


def _halo_prev(tt, ns, col):
    r = tt // HALO
    return pl.BlockSpec((HALO, D_MODEL), lambda b, s: (jnp.maximum((b * ns + s) * r - 1, 0), col))


def _halo_next(tt, ns, nblk, col):
    r = tt // HALO
    return pl.BlockSpec((HALO, D_MODEL), lambda b, s: (jnp.minimum((b * ns + s + 1) * r, nblk - 1), col))


def _windows(pad_ref, stage_ref, tt, offsets, cols):
    for r in range(8):
        mine = [o for o in offsets if o % 8 == r]
        if not mine:
            continue
        n = max(mine) - r + tt
        stage_ref[0:n, cols] = pad_ref[r:r + n, cols]
        for o in mine:
            yield o, stage_ref[o - r:o - r + tt, cols]


def _column_chunks():
    return [slice(c, c + LANES) for c in range(0, D_MODEL, LANES)]


def _fill_hc(hpad, a_ref, b_ref, ha_ref, hb_ref, s):
    halo = ha_ref[...] * _sigmoid(hb_ref[...])
    hpad[0:HALO, :] = jnp.where(s > 0, halo, 0.0)
    hpad[HALO:, :] = a_ref[...] * _sigmoid(b_ref[...])


def _conv_fwd(proj, conv_w, conv_b, ln_g, ln_b, bsz, seq):
    tt = _CONV_ROWS
    grid, ns = _row_grid(bsz, seq, tt)
    off = HALO - (CONV_TAPS - 1)

    def body(a_ref, b_ref, ha_ref, hb_ref, w_ref, cb_ref, g_ref, bb_ref, cs_ref, cv_ref, hpad, stage):
        _fill_hc(hpad, a_ref, b_ref, ha_ref, hb_ref, pl.program_id(1))
        for cols in _column_chunks():
            acc = jnp.zeros((tt, LANES), F32)
            for o, win in _windows(hpad, stage, tt, [off + j for j in range(CONV_TAPS)], cols):
                acc = acc + w_ref[o - off:o - off + 1, cols] * win
            cv_ref[:, cols] = acc + cb_ref[:, cols]
        xhat, _ = _ln_stats(cv_ref[...])
        cs_ref[...] = _silu(xhat * g_ref[...] + bb_ref[...]).astype(BF16)

    row = _rows(tt, D_MODEL, ns)
    t = proj.shape[0]
    return pl.pallas_call(
        body, name="conv_fwd", grid=grid,
        in_specs=[_rows(tt, D_MODEL, ns, _COL_GLU_A), _rows(tt, D_MODEL, ns, _COL_GLU_B),
                  _halo_prev(tt, ns, _COL_GLU_A), _halo_prev(tt, ns, _COL_GLU_B),
                  _vec_spec(32), _vec_spec(), _vec_spec(), _vec_spec()],
        out_specs=[row, row],
        out_shape=[jax.ShapeDtypeStruct((t, D_MODEL), BF16), jax.ShapeDtypeStruct((t, D_MODEL), F32)],
        scratch_shapes=[pltpu.VMEM((HALO + tt, D_MODEL), F32), pltpu.VMEM((HALO + tt, D_MODEL), F32)],
        compiler_params=_params(("parallel", "parallel")),
    )(proj, proj, proj, proj, conv_w, conv_b, ln_g, ln_b)


def _conv_bwd_ln(dcs, cv, ln_g, ln_b, bsz, seq):
    ts = _pick(seq, (512, 256, 128))
    grid, ns = _row_grid(bsz, seq, ts)

    def body(dcs_ref, cv_ref, g_ref, b_ref, dcv_ref, st_ref):
        xhat, rstd = _ln_stats(cv_ref[...])
        cl = xhat * g_ref[...] + b_ref[...]
        dcl = dcs_ref[...] * _dsilu(cl)
        dxhat = dcl * g_ref[...]
        m1 = jnp.mean(dxhat, axis=-1, keepdims=True)
        m2 = jnp.mean(dxhat * xhat, axis=-1, keepdims=True)
        dcv = rstd * (dxhat - m1 - xhat * m2)
        dcv_ref[...] = dcv

        @pl.when((pl.program_id(0) == 0) & (pl.program_id(1) == 0))
        def _():
            st_ref[...] = jnp.zeros_like(st_ref)

        st_ref[0:1, :] += _colsum(dcl * xhat)
        st_ref[1:2, :] += _colsum(dcl)
        st_ref[2:3, :] += _colsum(dcv)

    row = _rows(ts, D_MODEL, ns)
    return pl.pallas_call(
        body, name="conv_bwd_ln", grid=grid, in_specs=[row, row, _vec_spec(), _vec_spec()],
        out_specs=[row, _vec_spec(8)],
        out_shape=[jax.ShapeDtypeStruct(cv.shape, F32), jax.ShapeDtypeStruct((8, D_MODEL), F32)],
        compiler_params=_params(("arbitrary", "arbitrary")),
    )(dcs, cv, ln_g, ln_b)


def _conv_bwd_taps(proj, dcv, conv_w, bsz, seq):
    tt = _CONV_ROWS
    grid, ns = _row_grid(bsz, seq, tt)
    off = HALO - (CONV_TAPS - 1)
    nblk = proj.shape[0] // HALO

    def body(a_ref, b_ref, ha_ref, hb_ref, d_ref, dn_ref, w_ref, dglu_ref, dw_ref, hpad, dpad, stage):
        s = pl.program_id(1)
        _fill_hc(hpad, a_ref, b_ref, ha_ref, hb_ref, s)
        dpad[0:tt, :] = d_ref[...]
        dpad[tt:, :] = jnp.where(s < ns - 1, dn_ref[...], 0.0)

        @pl.when((pl.program_id(0) == 0) & (s == 0))
        def _():
            dw_ref[...] = jnp.zeros_like(dw_ref)

        for cols in _column_chunks():
            dcv = d_ref[:, cols]
            dhc = jnp.zeros((tt, LANES), F32)
            for o, win in _windows(dpad, stage, tt, list(range(CONV_TAPS)), cols):
                j = CONV_TAPS - 1 - o
                dhc = dhc + w_ref[j:j + 1, cols] * win
            for o, win in _windows(hpad, stage, tt, [off + j for j in range(CONV_TAPS)], cols):
                dw_ref[o - off:o - off + 1, cols] += _colsum(dcv * win)
            sb = _sigmoid(b_ref[:, cols])
            dglu_ref[:, cols] = (dhc * sb).astype(BF16)
            dglu_ref[:, slice(D_MODEL + cols.start, D_MODEL + cols.stop)] = (
                dhc * a_ref[:, cols] * sb * (1.0 - sb)).astype(BF16)

    t = proj.shape[0]
    return pl.pallas_call(
        body, name="conv_bwd_taps", grid=grid,
        in_specs=[_rows(tt, D_MODEL, ns, _COL_GLU_A), _rows(tt, D_MODEL, ns, _COL_GLU_B),
                  _halo_prev(tt, ns, _COL_GLU_A), _halo_prev(tt, ns, _COL_GLU_B),
                  _rows(tt, D_MODEL, ns), _halo_next(tt, ns, nblk, 0), _vec_spec(32)],
        out_specs=[_rows(tt, 2 * D_MODEL, ns), _vec_spec(32)],
        out_shape=[jax.ShapeDtypeStruct((t, 2 * D_MODEL), BF16), jax.ShapeDtypeStruct((32, D_MODEL), F32)],
        scratch_shapes=[pltpu.VMEM((HALO + tt, D_MODEL), F32), pltpu.VMEM((tt + HALO, D_MODEL), F32),
                        pltpu.VMEM((HALO + tt, D_MODEL), F32)],
        compiler_params=_params(("arbitrary", "arbitrary")),
    )(proj, proj, proj, proj, dcv, dcv, conv_w)


_NT = (((1,), (1,)), ((), ()))
_TN = (((0,), (0,)), ((), ()))


def _dot(a, b, dims=None):
    if dims is None:
        return jnp.dot(a, b, preferred_element_type=F32)
    return lax.dot_general(a, b, dims, preferred_element_type=F32)


def _tri_dot(v, tri2):
    hi = v.astype(BF16)
    lo = (v - hi.astype(F32)).astype(BF16)
    return _dot(jnp.concatenate([hi, lo], axis=1), tri2)


def _tri2(mask):
    t = mask.astype(BF16)
    return jnp.concatenate([t, t], axis=0)


def _softplus_parts(z):
    t = jnp.exp(-jnp.abs(z))
    den = 1.0 + t
    return jnp.maximum(z, 0.0) + jnp.log(den), t, den


def _attn_fwd(proj, bsz, seq):
    blk = ATT_BLOCK
    nq = seq // blk
    n_pairs = D_MODEL // LANES

    def body(q_ref, k_ref, v_ref, y_ref, rt_ref, zr_buf, ns_buf, run_buf, acc_buf):
        qi = pl.program_id(2)
        lane = lax.broadcasted_iota(jnp.int32, (blk, LANES), 1)
        first = lane < HEAD_DIM
        q2 = q_ref[...] * 0.125
        q_heads = (jnp.where(first, q2, 0.0).astype(BF16), jnp.where(first, 0.0, q2).astype(BF16))
        rr = lax.broadcasted_iota(jnp.int32, (blk, blk), 0)
        cc = lax.broadcasted_iota(jnp.int32, (blk, blk), 1)
        tri_ge = _tri2(rr >= cc)
        causal = cc < rr

        def scores(kb, slot, masked, heads=(0, 1)):
            k_blk = k_ref[pl.ds(pl.multiple_of(kb * blk, blk), blk), :].astype(BF16)
            for h in heads:
                z = _dot(q_heads[h], k_blk, _NT)
                if masked:
                    z = jnp.where(causal, z, -1e30)
                sp, _, _ = _softplus_parts(z)
                neg = -sp
                zr_buf[slot, h] = z + _tri_dot(neg, tri_ge)
                ns_buf[slot, h] = jnp.sum(neg, axis=1, keepdims=True)

        def weigh(kb, slot, heads=(0, 1)):
            v_blk = v_ref[pl.ds(pl.multiple_of(kb * blk, blk), blk), :].astype(BF16)
            for h in heads:
                run = run_buf[h]
                w = jnp.exp(zr_buf[slot, h] + run)
                acc_buf[h] += _dot(w.astype(BF16), v_blk)
                run_buf[h] = run + ns_buf[slot, h]

        def step(kb_next, kb, slot):
            for h in range(2):
                scores(kb_next, 1 - slot, False, (h,))
                weigh(kb, slot, (h,))

        run_buf[...] = jnp.zeros_like(run_buf)
        acc_buf[...] = jnp.zeros_like(acc_buf)
        scores(qi, 0, True)

        def two_steps(p, carry):
            t = 2 * p
            step(qi - t - 1, qi - t, 0)
            step(qi - t - 2, qi - t - 1, 1)
            return carry

        lax.fori_loop(0, qi // 2, two_steps, 0)

        @pl.when(qi % 2 == 1)
        def _():
            step(0, 1, 0)
            weigh(0, 1)

        @pl.when(qi % 2 == 0)
        def _():
            weigh(0, 0)

        y_ref[...] = jnp.where(first, acc_buf[0], acc_buf[1]).astype(BF16)
        rt_ref[...] = jnp.where(first, jnp.broadcast_to(run_buf[0], (blk, LANES)),
                                jnp.broadcast_to(run_buf[1], (blk, LANES)))

    t = proj.shape[0]
    q_spec = pl.BlockSpec((blk, LANES), lambda b, p, i: (b * nq + i, p))
    return pl.pallas_call(
        body, name="attn_fwd", grid=(bsz, n_pairs, nq),
        in_specs=[q_spec,
                  pl.BlockSpec((seq, LANES), lambda b, p, i: (b, n_pairs + p)),
                  pl.BlockSpec((seq, LANES), lambda b, p, i: (b, 2 * n_pairs + p))],
        out_specs=[q_spec, q_spec],
        out_shape=[jax.ShapeDtypeStruct((t, D_MODEL), BF16), jax.ShapeDtypeStruct((t, D_MODEL), F32)],
        scratch_shapes=[pltpu.VMEM((2, 2, blk, blk), F32), pltpu.VMEM((2, 2, blk, 1), F32),
                        pltpu.VMEM((2, blk, 1), F32), pltpu.VMEM((2, blk, LANES), F32)],
        compiler_params=_params(("parallel", "parallel", "arbitrary")),
    )(proj, proj, proj)


def _attn_bwd(proj, rtot, dy, bsz, seq):
    blk = ATT_BLOCK
    nq = seq // blk
    n_pairs = D_MODEL // LANES

    def body(q_ref, k_ref, v_ref, dy_ref, rt_ref, dq_ref, dk_ref, dv_ref, dk_acc, dv_acc,
             a_buf, sig_buf, dw_buf, ns_buf, pre_buf, es_buf, dq_buf):
        qi = pl.program_id(2)

        @pl.when(qi == 0)
        def _():
            dk_acc[...] = jnp.zeros_like(dk_acc)
            dv_acc[...] = jnp.zeros_like(dv_acc)

        lane = lax.broadcasted_iota(jnp.int32, (blk, LANES), 1)
        first = lane < HEAD_DIM
        head_row = lax.broadcasted_iota(jnp.int32, (LANES, blk), 0) < HEAD_DIM
        q2 = q_ref[...] * 0.125
        q_rows = (jnp.where(first, q2, 0.0).astype(BF16), jnp.where(first, 0.0, q2).astype(BF16))
        q_t = q2.T
        q_heads = (jnp.where(head_row, q_t, 0.0).astype(BF16), jnp.where(head_row, 0.0, q_t).astype(BF16))
        dy2 = dy_ref[...].astype(F32)
        dy_rows = (jnp.where(first, dy2, 0.0).astype(BF16), jnp.where(first, 0.0, dy2).astype(BF16))
        dy_t = dy2.T
        dy_heads = (jnp.where(head_row, dy_t, 0.0).astype(BF16), jnp.where(head_row, 0.0, dy_t).astype(BF16))
        rt_t = rt_ref[...].T
        rt = (rt_t[0:1, :], rt_t[HEAD_DIM:HEAD_DIM + 1, :])
        rr = lax.broadcasted_iota(jnp.int32, (blk, blk), 0)
        cc = lax.broadcasted_iota(jnp.int32, (blk, blk), 1)
        lower = (cc < rr).astype(BF16)
        lower_eq = (cc <= rr).astype(BF16)
        tri_lt = jnp.concatenate([lower, lower], axis=1)
        tri_le = jnp.concatenate([lower_eq, lower_eq], axis=1)
        causal = rr < cc

        def tri_left(tri2, v):
            hi = v.astype(BF16)
            lo = (v - hi.astype(F32)).astype(BF16)
            return _dot(tri2, jnp.concatenate([hi, lo], axis=0))

        def scores(kb, slot, heads=(0, 1)):
            rows = pl.ds(pl.multiple_of(kb * blk, blk), blk)
            k_blk = k_ref[rows, :].astype(BF16)
            v_blk = v_ref[rows, :].astype(BF16)
            keep = jnp.logical_or(causal, kb < qi)
            for h in heads:
                z = jnp.where(keep, _dot(k_blk, q_heads[h]), -1e30)
                sp, t, den = _softplus_parts(z)
                neg = -sp
                a_buf[slot, h] = z - tri_left(tri_lt, neg)
                sig_buf[slot, h] = jnp.where(z >= 0, 1.0, t) / den
                ns_buf[slot, h] = jnp.sum(neg, axis=0, keepdims=True)
                dw_buf[slot, h] = _dot(v_blk, dy_heads[h])

        def finish(kb, slot, heads=(0, 1)):
            rows = pl.ds(pl.multiple_of(kb * blk, blk), blk)
            k_t = k_ref[rows, :].T.astype(BF16)
            for h in heads:
                pre, esum = pre_buf[h], es_buf[h]
                w = jnp.exp(a_buf[slot, h] + (rt[h] - pre))
                e = dw_buf[slot, h] * w
                dz = e - sig_buf[slot, h] * (esum + tri_left(tri_le, e))
                pre_buf[h] = pre + ns_buf[slot, h]
                es_buf[h] = esum + jnp.sum(e, axis=0, keepdims=True)
                dzb = dz.astype(BF16)
                dq_buf[h] += _dot(k_t, dzb)
                dk_acc[rows, :] += _dot(dzb, q_rows[h])
                dv_acc[rows, :] += _dot(w.astype(BF16), dy_rows[h])

        def step(kb_next, kb, slot):
            scores(kb_next, 1 - slot)
            finish(kb, slot)

        pre_buf[...] = jnp.zeros_like(pre_buf)
        es_buf[...] = jnp.zeros_like(es_buf)
        dq_buf[...] = jnp.zeros_like(dq_buf)
        scores(0, 0)

        def two_steps(p, carry):
            t = 2 * p
            step(t + 1, t, 0)
            step(t + 2, t + 1, 1)
            return carry

        lax.fori_loop(0, qi // 2, two_steps, 0)

        @pl.when(qi % 2 == 1)
        def _():
            step(qi, qi - 1, 0)
            finish(qi, 1)

        @pl.when(qi % 2 == 0)
        def _():
            finish(qi, 0)

        dq_ref[...] = (jnp.where(head_row, dq_buf[0], dq_buf[1]).T * 0.125).astype(BF16)

        @pl.when(qi == nq - 1)
        def _():
            dk_ref[...] = dk_acc[...].astype(BF16)
            dv_ref[...] = dv_acc[...].astype(BF16)

    t = proj.shape[0]
    q_spec = pl.BlockSpec((blk, LANES), lambda b, p, i: (b * nq + i, p))
    kv_out = pl.BlockSpec((seq, LANES), lambda b, p, i: (b, p))
    out = jax.ShapeDtypeStruct((t, D_MODEL), BF16)
    return pl.pallas_call(
        body, name="attn_bwd", grid=(bsz, n_pairs, nq),
        in_specs=[q_spec,
                  pl.BlockSpec((seq, LANES), lambda b, p, i: (b, n_pairs + p)),
                  pl.BlockSpec((seq, LANES), lambda b, p, i: (b, 2 * n_pairs + p)),
                  q_spec, q_spec],
        out_specs=[q_spec, kv_out, kv_out], out_shape=[out, out, out],
        scratch_shapes=[pltpu.VMEM((seq, LANES), F32), pltpu.VMEM((seq, LANES), F32),
                        pltpu.VMEM((2, 2, blk, blk), F32), pltpu.VMEM((2, 2, blk, blk), F32),
                        pltpu.VMEM((2, 2, blk, blk), F32), pltpu.VMEM((2, 2, 1, blk), F32),
                        pltpu.VMEM((2, 1, blk), F32), pltpu.VMEM((2, 1, blk), F32),
                        pltpu.VMEM((2, LANES, blk), F32)],
        compiler_params=_params(("parallel", "parallel", "arbitrary")),
    )(proj, proj, proj, dy, rtot)


def _adamw(w, g, m, v, name, after):
    rows, cols = w.shape
    tr = _pick(rows, (256, 352, 128, 64, 32, 16, 8))
    c1 = 1.0 - ADAM_B1 ** ADAM_STEP
    c2 = 1.0 - ADAM_B2 ** ADAM_STEP

    def body(w_ref, g_ref, m_ref, v_ref, after_ref, d_ref, nm_ref, nv_ref):
        g_v = g_ref[...]
        nm = ADAM_B1 * m_ref[...] + (1.0 - ADAM_B1) * g_v
        nv = ADAM_B2 * v_ref[...] + (1.0 - ADAM_B2) * (g_v * g_v)
        nm_ref[...] = nm
        nv_ref[...] = nv
        d_ref[...] = -ADAM_LR * ((nm / c1) / (jnp.sqrt(nv / c2) + ADAM_EPS) + ADAM_WD * w_ref[...])

    spec = pl.BlockSpec((tr, cols), lambda i: (i, 0))
    shape = jax.ShapeDtypeStruct(w.shape, F32)
    return pl.pallas_call(
        body, name=name, grid=(rows // tr,), in_specs=[spec] * 4 + [pl.BlockSpec(memory_space=pl.ANY)],
        out_specs=[spec] * 3, out_shape=[shape] * 3,
        compiler_params=_params(("parallel",)),
    )(w, g, m, v, after)


def _ffn_fwd(u, w_gu, w_down, bsz, seq, tag):
    h, p = _ffn_up_act(u, w_gu, f"{tag}_up")
    f = _matmul(p, w_down, mode="nn", out_dtype=F32, name=f"{tag}_down")
    return h, p, f


def _ffn_bwd(df, u, h, p, w_gu, w_down, bsz, seq, tag):
    dh = _ffn_down_bwd_act(df, w_down, h, f"{tag}_ddown")
    g_down = _matmul(p, df, mode="tn", out_dtype=F32, name=f"{tag}_gdown")
    g_gu = _matmul(u, dh, mode="tn", out_dtype=F32, name=f"{tag}_ggu", out_stacked=True)
    du = _matmul(dh, w_gu, mode="nt", out_dtype=F32, name=f"{tag}_dup")
    return du, g_gu, g_down


def _local_step(x, c, target, wts, vecs, fetch=None, early_grads=None):
    wts = dict(wts)
    bsz, seq, _ = x.shape
    t = bsz * seq
    x0 = x.reshape(t, D_MODEL)
    tgt = target.reshape(t, D_MODEL)

    sc = _silu_pad(c)
    mod16 = _matmul(sc, wts["w_ada"], mode="nn", out_dtype=F32, name="ada_fwd", bias=vecs["b_ada"])
    mod = mod16[:bsz].reshape(bsz, 9, D_MODEL)

    u1 = _mod_in(x0, mod, bsz, seq, 0)
    h1, p1 = _ffn_up_act(u1, wts["ffn1_w_gu"], "ffn1_up")
    if fetch is not None:
        wts.update(fetch("down", p1))
    f1 = _matmul(p1, wts["ffn1_w_down"], mode="nn", out_dtype=F32, name="ffn1_down")
    r1, x1, u2 = _res_ln_fwd(x0, f1, mod, vecs["ln1_g"], vecs["ln1_b"], bsz, seq, 0, 0.5)
    if fetch is not None:
        wts.update(fetch("later", r1))

    proj = _matmul(u2, wts["w_in"], mode="nn", out_dtype=F32, name="mix_in")
    ya, rtot = _attn_fwd(proj, bsz, seq)
    cs, cv = _conv_fwd(proj, wts["conv_w"], vecs["conv_b"], vecs["conv_ln_g"], vecs["conv_ln_b"], bsz, seq)
    merged, ysb, yconv = _merge_fwd(proj, ya, cs, wts["w_sb_out"], wts["w_conv_out"])
    o2 = _matmul(merged, wts["w_out"], mode="nn", out_dtype=F32, name="mix_out")
    r2, x2, u3 = _res_ln_fwd(x1, o2, mod, vecs["ln2_g"], vecs["ln2_b"], bsz, seq, 1, 1.0)

    h3, p3, f3 = _ffn_fwd(u3, wts["ffn2_w_gu"], wts["ffn2_w_down"], bsz, seq, "ffn2")
    r3, dy, loss_blk = _res_ln_fwd(x2, f3, mod, vecs["ln3_g"], vecs["ln3_b"], bsz, seq, 2, 0.5, target=tgt)

    grads = {}
    dxres, df, ln3s, g3s = _res_ln_bwd(r3, dy, f3, mod, vecs["ln3_g"], bsz, seq, 2, 0.5)
    du, grads["ffn2_w_gu"], grads["ffn2_w_down"] = _ffn_bwd(
        df, u3, h3, p3, wts["ffn2_w_gu"], wts["ffn2_w_down"], bsz, seq, "ffn2")

    dxres, do2, ln2s, g2s, m3s = _res_ln_bwd(r2, dxres, o2, mod, vecs["ln2_g"], bsz, seq, 1, 1.0, above=(du, x2))
    grads["w_out"] = _matmul(merged, do2, mode="tn", out_dtype=F32, name="mix_out_g")
    dysb, dyconv, dgate = _merge_bwd(proj, ysb, yconv, do2, wts["w_out"])
    dya = _matmul(dysb, wts["w_sb_out"], mode="nt", out_dtype=BF16, name="sb_out_d")
    grads["w_sb_out"] = _matmul(ya, dysb, mode="tn", out_dtype=F32, name="sb_out_g")
    dcs = _matmul(dyconv, wts["w_conv_out"], mode="nt", out_dtype=F32, name="conv_out_d")
    grads["w_conv_out"] = _matmul(cs, dyconv, mode="tn", out_dtype=F32, name="conv_out_g")
    dcv, convs = _conv_bwd_ln(dcs, cv, vecs["conv_ln_g"], vecs["conv_ln_b"], bsz, seq)
    dglu, g_conv_w = _conv_bwd_taps(proj, dcv, wts["conv_w"], bsz, seq)
    dq, dk, dv = _attn_bwd(proj, rtot, dya, bsz, seq)
    dproj = jnp.concatenate([dq, dk, dv, dglu, dgate], axis=1)
    grads["w_in"] = _matmul(u2, dproj, mode="tn", out_dtype=F32, name="mix_in_g", out_stacked=True)
    if early_grads is not None:
        mod = mod + early_grads("later_start", {n: grads.pop(n) for n in list(grads)})
    du = _matmul(dproj, wts["w_in"], mode="nt", out_dtype=F32, name="mix_in_d")
    if early_grads is not None:
        mod = mod + early_grads("later_go", du)

    dxres, df, ln1s, g1s, m2s = _res_ln_bwd(r1, dxres, f1, mod, vecs["ln1_g"], bsz, seq, 0, 0.5, above=(du, x1))
    dh = _ffn_down_bwd_act(df, wts["ffn1_w_down"], h1, "ffn1_ddown")
    grads["ffn1_w_down"] = _matmul(p1, df, mode="tn", out_dtype=F32, name="ffn1_gdown")
    grads["ffn1_w_gu"] = _matmul(u1, dh, mode="tn", out_dtype=F32, name="ffn1_ggu", out_stacked=True)
    if early_grads is not None:
        mod = mod + early_grads("mid_start", {n: grads.pop(n) for n in list(grads)})
    du = _matmul(dh, wts["ffn1_w_gu"], mode="nt", out_dtype=F32, name="ffn1_dup")
    if early_grads is not None:
        mod = mod + early_grads("mid_go", du)
    grad_x, m1s = _mod_bwd(dxres, du, x0, mod, bsz, seq, 0)

    dmod = jnp.stack([m1s[:, 0], m1s[:, 1], g1s[:, 0], m2s[:, 0], m2s[:, 1], g2s[:, 0],
                      m3s[:, 0], m3s[:, 1], g3s[:, 0]], axis=1)
    dmod16 = jnp.zeros((16, 9 * D_MODEL), F32).at[:bsz].set(dmod.reshape(bsz, 9 * D_MODEL))
    grads["w_ada"] = _matmul(sc, dmod16.astype(BF16), mode="tn", out_dtype=F32, name="ada_g", out_stacked=True)

    small = {"dmod": dmod, "ln1": ln1s, "ln2": ln2s, "ln3": ln3s, "conv": convs, "conv_w": g_conv_w,
             "loss": loss_blk}
    return grad_x.reshape(x.shape), grads, small


_HBM = pl.BlockSpec(memory_space=pltpu.HBM)


def _position():
    return lax.axis_index("x"), lax.axis_index("y"), lax.axis_index("c")


def _other_chips(x, y):
    return [(1 - x, y), (x, 1 - y), (1 - x, 1 - y)]


def _cast_into_stack(w_local, chip, name):
    rows, cols = w_local.shape
    tr = _pick(rows, (256, 352, 128, 64, 32, 16))

    def body(chip_ref, w_ref, o_ref):
        o_ref[...] = w_ref[...].astype(BF16)

    return pl.pallas_call(
        body, name=name,
        grid_spec=pltpu.PrefetchScalarGridSpec(
            num_scalar_prefetch=1, grid=(rows // tr,),
            in_specs=[pl.BlockSpec((tr, cols), lambda r, chip_ref: (r, 0))],
            out_specs=pl.BlockSpec((None, tr, cols), lambda r, chip_ref: (chip_ref[0], r, 0))),
        out_shape=jax.ShapeDtypeStruct((N_CHIPS, rows, cols), BF16),
        compiler_params=_params(("parallel",)),
    )(chip, w_local)


def _all_gather_weights(stacks, small):
    n = len(stacks)

    def body(*refs):
        ins, small_in, outs, small_out = refs[:n], refs[n], refs[n + 1:2 * n + 1], refs[2 * n + 1]
        send_sems, recv_sems, fwd_send_sems, fwd_recv_sems, small_sems = refs[2 * n + 2:]
        x, y, c = _position()
        me = 2 * x + y
        chips = _other_chips(x, y)

        def send(i, j):
            px, py = chips[j]
            return pltpu.make_async_remote_copy(
                src_ref=ins[i].at[me, c], dst_ref=outs[i].at[me, c], send_sem=send_sems.at[3 * i + j],
                recv_sem=recv_sems.at[3 * i + j], device_id=(px, py, c), device_id_type=MESH)

        def landed(i, j):
            px, py = chips[j]
            return pltpu.make_async_remote_copy(
                src_ref=ins[i].at[me, c], dst_ref=outs[i].at[2 * px + py, c], send_sem=send_sems.at[3 * i + j],
                recv_sem=recv_sems.at[3 * i + j], device_id=(px, py, c), device_id_type=MESH)

        def forward(i, j, half):
            px, py = chips[j]
            blk = outs[i].at[2 * px + py, half]
            return pltpu.make_async_remote_copy(
                src_ref=blk, dst_ref=blk, send_sem=fwd_send_sems.at[3 * i + j],
                recv_sem=fwd_recv_sems.at[3 * i + j], device_id=(x, y, 1 - c), device_id_type=MESH)

        def small_copy(j, slot):
            px, py = chips[j]
            return pltpu.make_async_remote_copy(
                src_ref=small_in, dst_ref=small_out.at[slot], send_sem=small_sems.at[j],
                recv_sem=small_sems.at[3 + j], device_id=(px, py, c), device_id_type=MESH)

        own_small = pltpu.make_async_copy(small_in, small_out.at[me], small_sems.at[6])
        own_small.start()
        for j in range(3):
            small_copy(j, me).start()
        for i in range(n):
            for j in range(3):
                send(i, j).start()
        for i in range(n):
            for j in range(3):
                landed(i, j).wait_recv()
                forward(i, j, c).start()
        for i in range(n):
            for j in range(3):
                forward(i, j, 1 - c).wait_recv()
        for j, (px, py) in enumerate(chips):
            small_copy(j, 2 * px + py).wait_recv()
        own_small.wait()
        for j in range(3):
            small_copy(j, me).wait_send()
        for i in range(n):
            for j in range(3):
                send(i, j).wait_send()
                forward(i, j, c).wait_send()

    return pl.pallas_call(
        body, name="all_gather_weights",
        out_shape=[jax.ShapeDtypeStruct(s.shape, s.dtype) for s in stacks]
        + [jax.ShapeDtypeStruct((N_CHIPS,) + small.shape, small.dtype)],
        in_specs=[_HBM] * (n + 1), out_specs=[_HBM] * (n + 1),
        input_output_aliases={i: i for i in range(n)},
        scratch_shapes=[pltpu.SemaphoreType.DMA((3 * n,)), pltpu.SemaphoreType.DMA((3 * n,)),
                        pltpu.SemaphoreType.DMA((3 * n,)), pltpu.SemaphoreType.DMA((3 * n,)),
                        pltpu.SemaphoreType.DMA((7,))],
    )(*stacks, small)


_SEM = pl.BlockSpec(memory_space=pltpu.SEMAPHORE)
_DATAFLOW = pltpu.SideEffectType.DATAFLOW_SIDE_EFFECTING


_COPIES = {"gather": 3, "scatter": 3, "swap": N_CHIPS}


def _exchange_plan(kind, src, land):
    x, y, c = _position()
    me = 2 * x + y
    if kind == "swap":
        return [(src.at[k, 1 - c], land.at[k], land.at[k], (x, y, 1 - c)) for k in range(N_CHIPS)]
    plan = []
    for j, (px, py) in enumerate(_other_chips(x, y)):
        if kind == "gather":
            plan.append((src.at[me, c], land.at[me, c], land.at[2 * px + py, c], (px, py, c)))
        else:
            plan.append((src.at[2 * px + py], land.at[j], land.at[j], (px, py, c)))
    return plan


def _exchange_start(kind, srcs, lands, name, after):
    n = len(srcs)
    per = _COPIES[kind]
    in_place = lands is None
    n_in = n if in_place else 2 * n

    def body(*refs):
        src_refs = refs[:n]
        land_refs = src_refs if in_place else refs[n:2 * n]
        send_sems, recv_sems = refs[n_in + 1], refs[n_in + 2]
        token = refs[-1]
        for i in range(n):
            for j, (src, dst, _, to) in enumerate(_exchange_plan(kind, src_refs[i], land_refs[i])):
                pltpu.make_async_remote_copy(
                    src_ref=src, dst_ref=dst, send_sem=send_sems.at[per * i + j], recv_sem=recv_sems.at[per * i + j],
                    device_id=to, device_id_type=MESH).start()
        token[...] = jnp.zeros_like(token)

    operands = list(srcs) + ([] if in_place else list(lands))
    operands = [pltpu.with_memory_space_constraint(o, pltpu.HBM) for o in operands]
    out = pl.pallas_call(
        body, name=name,
        out_shape=[pltpu.SemaphoreType.DMA((per * n,)), pltpu.SemaphoreType.DMA((per * n,))]
        + [pltpu.HBM(o.shape, o.dtype) for o in operands] + [jax.ShapeDtypeStruct((8, LANES), F32)],
        in_specs=[_HBM] * n_in + [pl.BlockSpec(memory_space=pl.ANY)],
        out_specs=[_SEM, _SEM] + [_HBM] * n_in + [pl.BlockSpec(memory_space=pltpu.VMEM)],
        input_output_aliases={i: 2 + i for i in range(n_in)},
        compiler_params=pltpu.CompilerParams(has_side_effects=_DATAFLOW),
    )(*operands, after)
    return out[0], out[1], list(out[2:2 + n_in]), out[-1]


def _exchange_wait(kind, send_sems, recv_sems, thru, in_place, after, name):
    n_in = len(thru)
    n = n_in if in_place else n_in // 2
    per = _COPIES[kind]

    def body(*refs):
        src_refs = refs[:n]
        land_refs = src_refs if in_place else refs[n:2 * n]
        send_sems, recv_sems = refs[n_in], refs[n_in + 1]
        for i in range(n):
            for j, (src, _, here, to) in enumerate(_exchange_plan(kind, src_refs[i], land_refs[i])):
                copy = pltpu.make_async_remote_copy(
                    src_ref=src, dst_ref=here, send_sem=send_sems.at[per * i + j], recv_sem=recv_sems.at[per * i + j],
                    device_id=to, device_id_type=MESH)
                copy.wait_send()
                copy.wait_recv()

    out = pl.pallas_call(
        body, name=name, out_shape=[pltpu.HBM(o.shape, o.dtype) for o in thru],
        in_specs=[_HBM] * n_in + [_SEM, _SEM, pl.BlockSpec(memory_space=pl.ANY)], out_specs=[_HBM] * n_in,
        input_output_aliases={i: i for i in range(n_in)},
        compiler_params=pltpu.CompilerParams(has_side_effects=_DATAFLOW),
    )(*thru, send_sems, recv_sems, after)
    return list(out[:n]), (list(out[:n]) if in_place else list(out[n:]))


def _gather_forward(stacks, name):
    n = len(stacks)

    def body(*refs):
        ins, outs = refs[:n], refs[n:2 * n]
        send_sems, recv_sems = refs[2 * n:]
        x, y, c = _position()
        chips = _other_chips(x, y)

        def copy(i, j, half):
            px, py = chips[j]
            return pltpu.make_async_remote_copy(
                src_ref=ins[i].at[2 * px + py, half], dst_ref=outs[i].at[2 * px + py, half],
                send_sem=send_sems.at[3 * i + j], recv_sem=recv_sems.at[3 * i + j],
                device_id=(x, y, 1 - c), device_id_type=MESH)

        for i in range(n):
            for j in range(3):
                copy(i, j, c).start()
        for i in range(n):
            for j in range(3):
                copy(i, j, 1 - c).wait_recv()
        for i in range(n):
            for j in range(3):
                copy(i, j, c).wait_send()

    return pl.pallas_call(
        body, name=name, out_shape=[jax.ShapeDtypeStruct(s.shape, s.dtype) for s in stacks],
        in_specs=[_HBM] * n, out_specs=[_HBM] * n, input_output_aliases={i: i for i in range(n)},
        scratch_shapes=[pltpu.SemaphoreType.DMA((3 * n,)), pltpu.SemaphoreType.DMA((3 * n,))],
    )(*stacks)


def _pair_add(g, got, place, name):
    _, _, rh, cols = g.shape
    tr = _pick(rh, (256, 176, 128, 64, 32, 16, 8))

    def body(place_ref, g_ref, got_ref, p_ref, own_ref):
        s = g_ref[...] + got_ref[...]
        p_ref[...] = s.astype(BF16)

        @pl.when(pl.program_id(1) == place_ref[1])
        def _():
            own_ref[...] = s

    blk = pl.BlockSpec((None, tr, cols), lambda r, k, place_ref: (k, r, 0))
    return pl.pallas_call(
        body, name=name,
        grid_spec=pltpu.PrefetchScalarGridSpec(
            num_scalar_prefetch=1, grid=(rh // tr, N_CHIPS),
            in_specs=[pl.BlockSpec((None, None, tr, cols), lambda r, k, place_ref: (k, place_ref[0], r, 0)), blk],
            out_specs=[blk, pl.BlockSpec((tr, cols), lambda r, k, place_ref: (r, 0))]),
        out_shape=[jax.ShapeDtypeStruct((N_CHIPS, rh, cols), BF16), jax.ShapeDtypeStruct((rh, cols), F32)],
        compiler_params=_params(("parallel", "arbitrary")),
    )(place, g, got)


def _chip_sum(own, parts, place, name):
    rh, cols = own.shape
    tr = _pick(rh, (256, 176, 128, 64, 32, 16, 8))

    def body(place_ref, own_ref, p_ref, o_ref):
        o_ref[...] = ((own_ref[...] + p_ref[0].astype(F32)) + p_ref[1].astype(F32)) + p_ref[2].astype(F32)

    return pl.pallas_call(
        body, name=name,
        grid_spec=pltpu.PrefetchScalarGridSpec(
            num_scalar_prefetch=1, grid=(rh // tr,),
            in_specs=[pl.BlockSpec((tr, cols), lambda r, place_ref: (r, 0)),
                      pl.BlockSpec((3, tr, cols), lambda r, place_ref: (0, r, 0))],
            out_specs=pl.BlockSpec((None, tr, cols), lambda r, place_ref: (place_ref[0], r, 0))),
        out_shape=jax.ShapeDtypeStruct((2, rh, cols), F32),
        compiler_params=_params(("parallel",)),
    )(place, own, parts)


def _pair_gather(halves, name):
    n = len(halves)

    def body(*refs):
        ins, outs = refs[:n], refs[n:2 * n]
        send_sems, recv_sems = refs[2 * n:]
        x, y, c = _position()

        def send(i):
            return pltpu.make_async_remote_copy(
                src_ref=ins[i].at[c], dst_ref=outs[i].at[c], send_sem=send_sems.at[i], recv_sem=recv_sems.at[i],
                device_id=(x, y, 1 - c), device_id_type=MESH)

        def landed(i):
            return pltpu.make_async_remote_copy(
                src_ref=ins[i].at[c], dst_ref=outs[i].at[1 - c], send_sem=send_sems.at[i], recv_sem=recv_sems.at[i],
                device_id=(x, y, 1 - c), device_id_type=MESH)

        for i in range(n):
            send(i).start()
        for i in range(n):
            landed(i).wait_recv()
        for i in range(n):
            send(i).wait_send()

    return pl.pallas_call(
        body, name=name,
        out_shape=[jax.ShapeDtypeStruct(h.shape, F32) for h in halves],
        in_specs=[_HBM] * n, out_specs=[_HBM] * n,
        input_output_aliases={i: i for i in range(n)},
        scratch_shapes=[pltpu.SemaphoreType.DMA((n,)), pltpu.SemaphoreType.DMA((n,))],
    )(*halves)


_MOD_ROWS = 16


def _small_all_reduce(buf, bsz, after):
    rows, cols = buf.shape
    head = bsz * _MOD_ROWS
    out_rows = rows - head + _MOD_ROWS

    def body(in_ref, after_ref, o_ref, gath, send_sems, recv_sems):
        x, y, c = _position()
        me = 4 * x + 2 * y + c

        def peer(mask):
            return (x ^ (mask >> 2), y ^ ((mask >> 1) & 1), c ^ (mask & 1))

        def copy(mask):
            return pltpu.make_async_remote_copy(
                src_ref=in_ref, dst_ref=gath.at[me], send_sem=send_sems.at[mask - 1],
                recv_sem=recv_sems.at[mask - 1], device_id=peer(mask), device_id_type=MESH)

        def arrival(mask):
            px, py, pc = peer(mask)
            return pltpu.make_async_remote_copy(
                src_ref=in_ref, dst_ref=gath.at[4 * px + 2 * py + pc], send_sem=send_sems.at[mask - 1],
                recv_sem=recv_sems.at[mask - 1], device_id=peer(mask), device_id_type=MESH)

        for mask in range(1, N_DEV):
            copy(mask).start()
        gath[me] = in_ref[...]
        for mask in range(1, N_DEV):
            arrival(mask).wait_recv()
        for mask in range(1, N_DEV):
            copy(mask).wait_send()
        acc = gath[0]
        for d in range(1, N_DEV):
            acc = acc + gath[d]
        mod = acc[0:_MOD_ROWS]
        for s in range(1, bsz):
            mod = mod + acc[s * _MOD_ROWS:(s + 1) * _MOD_ROWS]
        o_ref[0:_MOD_ROWS, :] = mod
        o_ref[_MOD_ROWS:, :] = acc[head:]

    vm = pl.BlockSpec(memory_space=pltpu.VMEM)
    return pl.pallas_call(
        body, name="small_all_reduce", in_specs=[vm, pl.BlockSpec(memory_space=pl.ANY)], out_specs=vm,
        out_shape=jax.ShapeDtypeStruct((out_rows, cols), F32),
        scratch_shapes=[pltpu.VMEM((N_DEV, rows, cols), F32), pltpu.SemaphoreType.DMA((N_DEV - 1,)),
                        pltpu.SemaphoreType.DMA((N_DEV - 1,))],
        compiler_params=pltpu.CompilerParams(vmem_limit_bytes=VMEM_LIMIT),
    )(buf, after)


_COL_SHARDED = ("w_ada", "ffn1_w_gu", "w_in", "ffn2_w_gu")
_ROW_SHARDED = ("ffn1_w_down", "w_sb_out", "w_conv_out", "w_out", "ffn2_w_down")
_NOW = ["w_ada", "ffn1_w_gu"]
_SOON = ["ffn1_w_down"]
_LATER = ["w_in", "w_sb_out", "w_conv_out", "w_out", "ffn2_w_gu", "ffn2_w_down"]
_VECS = ("b_ada", "ln1_g", "ln1_b", "conv_b", "conv_ln_g", "conv_ln_b", "ln2_g", "ln2_b", "ln3_g", "ln3_b")
_WEIGHTS = ("w_ada", "b_ada", "ffn1_w_gu", "ffn1_w_down", "ln1_g", "ln1_b", "w_in", "w_sb_out", "conv_w", "conv_b",
            "conv_ln_g", "conv_ln_b", "w_conv_out", "w_out", "ln2_g", "ln2_b", "ffn2_w_gu", "ffn2_w_down",
            "ln3_g", "ln3_b")


def _step(x, c, target, w, m, v):
    bsz = x.shape[0]
    chip = 2 * lax.axis_index("x") + lax.axis_index("y")
    core = lax.axis_index("c")

    chip_arr = jnp.reshape(chip, (1,)).astype(jnp.int32)
    place = jnp.stack([core, chip]).astype(jnp.int32)

    def stack_of(n):
        rows, cols = w[n].shape[1:]
        return _cast_into_stack(w[n][0], chip_arr, f"cast_{n}").reshape(N_CHIPS, 2, rows // 2, cols)

    def gathered_form(n, g):
        rows, cols = w[n].shape[1:]
        return g.reshape(N_CHIPS, rows, cols) if n in _COL_SHARDED else g.reshape(N_CHIPS * rows, cols)

    conv_w_local = jnp.pad(w["conv_w"][0], ((0, 1), (0, 0)))
    gathered = _all_gather_weights([stack_of(n) for n in _NOW], conv_w_local)
    wts = {n: gathered_form(n, g) for n, g in zip(_NOW, gathered[:-1])}
    wts["conv_w"] = gathered[-1].transpose(1, 0, 2).reshape(32, D_MODEL)
    pending, behind = {}, gathered[0]
    for stage, names in (("down", _SOON), ("later", _LATER)):
        send, recv, thru, token = _exchange_start(
            "gather", [stack_of(n) for n in names], None, f"gather_start_{stage}", behind)
        pending[stage] = (names, send, recv, thru)
        behind = token
    vecs = {n: w[n] for n in _VECS}
    vecs["b_ada"] = vecs["b_ada"] + behind[0, 0]

    def fetch(stage, after):
        names, send, recv, thru = pending[stage]
        landed, _ = _exchange_wait("gather", send, recv, thru, True, after, f"gather_wait_{stage}")
        forwarded = _gather_forward(landed, f"gather_forward_{stage}")
        return {n: gathered_form(n, g) for n, g in zip(names, forwarded)}

    groups = {"later": _LATER, "mid": ["ffn1_w_gu", "ffn1_w_down"], "last": ["w_ada"]}
    g_out, updates, state = {}, {}, {}

    def adam(names, after):
        for n in names:
            shape = w[n].shape
            flat = shape[1:] if len(shape) == 3 else shape
            d, nm, nv = _adamw(w[n].reshape(flat), g_out[n].reshape(flat), m[n].reshape(flat), v[n].reshape(flat),
                               f"adamw_{n}", after)
            updates[n] = (g_out[n].reshape(shape), d.reshape(shape), nm.reshape(shape), nv.reshape(shape))
            after = nv
        return after

    def swap_start(tag, grads):
        views = [grads[n].reshape(N_CHIPS, 2, w[n].shape[1] // 2, w[n].shape[2]) for n in groups[tag]]
        lands = [lax.empty((N_CHIPS,) + g.shape[2:], F32) for g in views]
        send, recv, thru, token = _exchange_start("swap", views, lands, f"swap_start_{tag}", place)
        state[tag] = {"swap": (send, recv, thru)}
        return token

    def scatter_start(tag, after):
        views, got = _exchange_wait("swap", *state[tag]["swap"], False, after, f"swap_wait_{tag}")
        sums = [_pair_add(g, r, place, f"pair_add_{n}") for n, g, r in zip(groups[tag], views, got)]
        lands = [lax.empty((3,) + p.shape[1:], BF16) for p, _ in sums]
        send, recv, thru, token = _exchange_start(
            "scatter", [p for p, _ in sums], lands, f"scatter_start_{tag}", place)
        state[tag].update(scatter=(send, recv, thru), sums=sums)
        return token

    def collect(tag, after):
        _, parts = _exchange_wait("scatter", *state[tag]["scatter"], False, after, f"scatter_wait_{tag}")
        halves = [_chip_sum(own, p, place, f"chip_sum_{n}")
                  for n, (_, own), p in zip(groups[tag], state[tag]["sums"], parts)]
        for n, f in zip(groups[tag], _pair_gather(halves, f"grad_pair_gather_{tag}")):
            g_out[n] = f.reshape(w[n].shape[1:])
        return g_out[groups[tag][-1]]

    def early_grads(stage, value):
        tag, step = stage.split("_")
        token = swap_start(tag, value) if step == "start" else scatter_start(tag, value)
        return token[0, 0]

    grad_x, grads, small = _local_step(x, c, target, wts, vecs, fetch, early_grads)

    token = swap_start("last", grads)
    done = collect("later", token)
    token = scatter_start("last", done)
    done = adam(groups["later"], token)
    done = collect("mid", done)
    done = adam(groups["mid"], done)
    done = collect("last", done)

    dmod = jnp.pad(small["dmod"], ((0, 0), (0, _MOD_ROWS - 9), (0, 0))).reshape(bsz * _MOD_ROWS, D_MODEL)
    loss_rows = jnp.pad(small["loss"], ((0, 0), (0, D_MODEL - LANES)))
    buf = jnp.concatenate([dmod, small["ln1"], small["ln2"], small["ln3"], small["conv"], small["conv_w"],
                           loss_rows], axis=0)
    red = _small_all_reduce(buf, bsz, done)
    o = _MOD_ROWS
    g_out["b_ada"] = red[0:9].reshape(1, 9 * D_MODEL)
    g_out["ln1_g"], g_out["ln1_b"] = red[o:o + 1], red[o + 1:o + 2]
    g_out["ln2_g"], g_out["ln2_b"] = red[o + 8:o + 9], red[o + 9:o + 10]
    g_out["ln3_g"], g_out["ln3_b"] = red[o + 16:o + 17], red[o + 17:o + 18]
    g_out["conv_ln_g"], g_out["conv_ln_b"], g_out["conv_b"] = red[o + 24:o + 25], red[o + 25:o + 26], red[o + 26:o + 27]
    cw = w["conv_w"].shape[2]
    g_out["conv_w"] = lax.dynamic_slice(red[o + 32:o + 32 + CONV_TAPS], (0, chip * cw), (CONV_TAPS, cw))
    loss = red[o + 64, 0]

    adam(groups["last"] + list(_VECS) + ["conv_w"], place)
    return (loss, grad_x, *[updates[n][k] for k in range(4) for n in _WEIGHTS])


def kernel(x, c, w_ada, b_ada, ffn1_w_gu, ffn1_w_down, ln1_g, ln1_b, w_in, w_sb_out, conv_w, conv_b, conv_ln_g, conv_ln_b, w_conv_out, w_out, ln2_g, ln2_b, ffn2_w_gu, ffn2_w_down, ln3_g, ln3_b, loss_target, m_w_ada, m_b_ada, m_ffn1_w_gu, m_ffn1_w_down, m_ln1_g, m_ln1_b, m_w_in, m_w_sb_out, m_conv_w, m_conv_b, m_conv_ln_g, m_conv_ln_b, m_w_conv_out, m_w_out, m_ln2_g, m_ln2_b, m_ffn2_w_gu, m_ffn2_w_down, m_ln3_g, m_ln3_b, v_w_ada, v_b_ada, v_ffn1_w_gu, v_ffn1_w_down, v_ln1_g, v_ln1_b, v_w_in, v_w_sb_out, v_conv_w, v_conv_b, v_conv_ln_g, v_conv_ln_b, v_w_conv_out, v_w_out, v_ln2_g, v_ln2_b, v_ffn2_w_gu, v_ffn2_w_down, v_ln3_g, v_ln3_b):
    given = dict(locals())
    w = {n: given[n] for n in _WEIGHTS}
    m = {n: given["m_" + n] for n in _WEIGHTS}
    v = {n: given["v_" + n] for n in _WEIGHTS}
    return _step(x, c, loss_target, w, m, v)
```
